```python
import jax
import jax.numpy as jnp
from jax import lax
import numpy as np

D_MODEL = 1024
BATCH = 2
SEQ = 8192
DEPTH = 1
DEC_BATCH = 128
DEC_SEQ = 1
PAST_LEN = 16384
PAGE_SIZE = 128

N_META = 16
WINDOW = 128
ATT_HEAD_DIM = 64
ATT_Q_HEADS = 8
ATT_KV_HEADS = 2
ATT_GROUP = ATT_Q_HEADS // ATT_KV_HEADS
D_ATT = ATT_Q_HEADS * ATT_HEAD_DIM
D_KV = ATT_KV_HEADS * ATT_HEAD_DIM
HG_HEADS = 4
HG_DK = 128
HG_DV = 128
D_HG = HG_HEADS * HG_DK
HG_CHUNK = 64
D_FF = 4 * D_MODEL
EPS = 1e-6
IN_SIZES = (D_ATT, D_KV, D_KV, D_HG, D_HG, D_HG, D_HG, D_MODEL, D_MODEL)
D_IN = D_ATT + 2 * D_KV + 4 * D_HG + 2 * D_MODEL

kernel_name = 'hybrid_swa_sink_hgrn2_decoder_step'


def rmsnorm(x, g):
    xf = x.astype(jnp.float32)
    y = xf * lax.rsqrt(jnp.mean(jnp.square(xf), axis=-1, keepdims=True) + EPS)
    return (y * g.astype(jnp.float32)).astype(x.dtype)


def split_projection(z):
    lead = z.shape[:-1]
    parts, off = [], 0
    for n in IN_SIZES:
        parts.append(z[..., off:off + n])
        off += n
    qa, ka, va, qh, fh, ih, gh, ga, gb = parts
    return (qa.reshape(*lead, ATT_Q_HEADS, ATT_HEAD_DIM),
            ka.reshape(*lead, ATT_KV_HEADS, ATT_HEAD_DIM),
            va.reshape(*lead, ATT_KV_HEADS, ATT_HEAD_DIM),
            qh.reshape(*lead, HG_HEADS, HG_DK),
            fh.reshape(*lead, HG_HEADS, HG_DK),
            ih.reshape(*lead, HG_HEADS, HG_DV),
            gh.reshape(*lead, HG_HEADS, HG_DV),
            ga, gb)


def sink_probs(s, mask, sinks):
    sk = sinks.astype(jnp.float32).reshape(ATT_KV_HEADS, ATT_GROUP, 1, 1)
    s = jnp.where(mask, s, -jnp.inf)
    m = jnp.maximum(jnp.max(s, axis=-1, keepdims=True), sk)
    p = jnp.exp(s - m)
    return p / (jnp.sum(p, axis=-1, keepdims=True) + jnp.exp(sk - m))


def swa_prompt(q, k, v, sinks):
    B, T = q.shape[:2]
    pad = WINDOW - N_META
    nb = (T + pad) // WINDOW
    pw = ((0, 0), (pad, 0), (0, 0), (0, 0))
    qb = jnp.pad(q, pw).reshape(B, nb, WINDOW, ATT_KV_HEADS, ATT_GROUP, ATT_HEAD_DIM)
    kb = jnp.pad(k, pw).reshape(B, nb, WINDOW, ATT_KV_HEADS, ATT_HEAD_DIM)
    vb = jnp.pad(v, pw).reshape(B, nb, WINDOW, ATT_KV_HEADS, ATT_HEAD_DIM)

    def band_keys(a, meta):
        prev = jnp.pad(a, ((0, 0), (1, 0), (0, 0), (0, 0), (0, 0)))[:, :-1]
        meta_b = jnp.broadcast_to(meta[:, None], (B, nb, N_META, ATT_KV_HEADS, ATT_HEAD_DIM))
        return jnp.concatenate([meta_b, prev, a], axis=2)

    kk = band_keys(kb, k[:, :N_META])
    vv = band_keys(vb, v[:, :N_META])
    s = jnp.einsum('bnqhgd,bnkhd->bnhgqk', qb, kk,
                   preferred_element_type=jnp.float32) * (ATT_HEAD_DIM ** -0.5)
    blk = jnp.arange(nb)[:, None, None]
    qpos = blk * WINDOW + jnp.arange(WINDOW)[None, :, None]
    kpos = (blk - 1) * WINDOW + jnp.arange(2 * WINDOW)[None, None, :]
    rel = qpos - kpos
    m_reg = (rel >= 0) & (rel < WINDOW) & (kpos >= pad + N_META)
    m_meta = qpos >= pad + jnp.arange(N_META)[None, None, :]
    mask = jnp.concatenate([m_meta, m_reg], axis=-1)
    p = sink_probs(s, mask[None, :, None, None], sinks)
    o = jnp.einsum('bnhgqk,bnkhd->bnqhgd', p, vv.astype(jnp.float32))
    return o.reshape(B, nb * WINDOW, D_ATT)[:, pad:]


def swa_sample(q, k, v, k_win, v_win, k_meta, v_meta, sinks):
    Bd, S = q.shape[:2]
    R = k_win.shape[1]
    kw = jnp.concatenate([k_win.astype(k.dtype), k], axis=1)
    vw = jnp.concatenate([v_win.astype(v.dtype), v], axis=1)
    kk = jnp.concatenate([k_meta.astype(k.dtype), kw], axis=1)
    vv = jnp.concatenate([v_meta.astype(v.dtype), vw], axis=1)
    qg = q.reshape(Bd, S, ATT_KV_HEADS, ATT_GROUP, ATT_HEAD_DIM)
    s = jnp.einsum('bqhgd,bkhd->bhgqk', qg, kk,
                   preferred_element_type=jnp.float32) * (ATT_HEAD_DIM ** -0.5)
    qpos = PAST_LEN + jnp.arange(S)[:, None]
    kpos = PAST_LEN - R + jnp.arange(R + S)[None, :]
    rel = qpos - kpos
    m_reg = (rel >= 0) & (rel < WINDOW) & (kpos >= N_META)
    mask = jnp.concatenate([jnp.ones((S, N_META), dtype=bool), m_reg], axis=-1)
    p = sink_probs(s, mask, sinks)
    o = jnp.einsum('bhgqk,bkhd->bqhgd', p, vv.astype(jnp.float32)).reshape(Bd, S, D_ATT)
    return o, kw[:, S:], vw[:, S:]


def hgrn2_gates(qh, fh, ih, lb):
    q = qh.astype(jnp.float32) * (HG_DK ** -0.5)
    f = lb + (1.0 - lb) * jax.nn.sigmoid(fh.astype(jnp.float32))
    return q, f, ih.astype(jnp.float32)


def hgrn2_prompt(q, f, i_val):
    B, T = q.shape[:2]
    pad = (-N_META) % HG_CHUNK
    k = 1.0 - f
    logf = jnp.log(f)
    pw = ((0, 0), (pad, 0), (0, 0), (0, 0))
    L = T + pad
    nc = L // HG_CHUNK

    def rs(a):
        return jnp.pad(a, pw).reshape(B, nc, HG_CHUNK, HG_HEADS, a.shape[-1])

    q, k, logf, i_val = rs(q), rs(k), rs(logf), rs(i_val)
    b = jnp.cumsum(logf, axis=2)
    b_last = b[:, :, -1:]
    qe = q * jnp.exp(b)
    ke = k * jnp.exp(-b)
    kd = k * jnp.exp(b_last - b)
    causal = jnp.tril(jnp.ones((HG_CHUNK, HG_CHUNK), dtype=bool))
    A = jnp.where(causal, jnp.einsum('bnthd,bnshd->bnhts', qe, ke), 0.0)
    o_intra = jnp.einsum('bnhts,bnshv->bnthv', A, i_val)
    U = jnp.einsum('bnshd,bnshv->bnhdv', kd, i_val)
    decay = jnp.exp(b_last[:, :, 0])

    def body(S, xs):
        dec, u = xs
        return dec[..., None] * S + u, S

    S0 = jnp.zeros((B, HG_HEADS, HG_DK, HG_DV), jnp.float32)
    S_fin, S_prev = lax.scan(body, S0, (jnp.swapaxes(decay, 0, 1), jnp.swapaxes(U, 0, 1)))
    S_prev = jnp.swapaxes(S_prev, 0, 1)
    o_inter = jnp.einsum('bnthd,bnhdv->bnthv', qe, S_prev)
    o = (o_intra + o_inter).reshape(B, L, HG_HEADS, HG_DV)[:, pad:]
    return o, S_fin


def hgrn2_sample(q, f, i_val, S0):
    def step(S, xs):
        qt, ft, it = xs
        S = ft[..., None] * S + (1.0 - ft)[..., None] * it[:, :, None, :]
        return S, jnp.einsum('bhd,bhdv->bhv', qt, S)

    S, o = lax.scan(step, S0.astype(jnp.float32),
                    (jnp.swapaxes(q, 0, 1), jnp.swapaxes(f, 0, 1), jnp.swapaxes(i_val, 0, 1)))
    return jnp.swapaxes(o, 0, 1), S


def hgrn2_out(o, g, norm_w):
    o = o * lax.rsqrt(jnp.mean(jnp.square(o), axis=-1, keepdims=True) + EPS) * norm_w.astype(jnp.float32)
    o = o * jax.nn.silu(g.astype(jnp.float32))
    return o.reshape(*o.shape[:2], HG_HEADS * HG_DV)


def merge_branches(att, hg, ga, gb, w_att_out, w_hg_out, w_o):
    ya = att.astype(w_att_out.dtype) @ w_att_out
    yb = hg.astype(w_hg_out.dtype) @ w_hg_out
    return (jax.nn.sigmoid(ga) * ya + jax.nn.sigmoid(gb) * yb) @ w_o


def ffn_residual(h, ln, w_up, w_down):
    u = rmsnorm(h, ln) @ w_up
    return h + jnp.square(jax.nn.relu(u)) @ w_down


def setup_inputs(seed: int = 0) -> dict:
    key = jax.random.key(seed)
    ks = jax.random.split(key, 20)

    def nrm(k, shape, s):
        return jax.random.normal(k, shape, jnp.float32) * s

    rows = min(WINDOW, PAST_LEN)
    return {
        'x_prompt': nrm(ks[0], (BATCH, SEQ, D_MODEL), 1.0),
        'x_sample': nrm(ks[1], (DEC_BATCH, DEC_SEQ, D_MODEL), 1.0),
        'cache_k': nrm(ks[2], (DEPTH, DEC_BATCH, rows, ATT_KV_HEADS, ATT_HEAD_DIM), 1.0),
        'cache_v': nrm(ks[3], (DEPTH, DEC_BATCH, rows, ATT_KV_HEADS, ATT_HEAD_DIM), 1.0),
        'cache_meta_k': nrm(ks[4], (DEPTH, DEC_BATCH, N_META, ATT_KV_HEADS, ATT_HEAD_DIM), 1.0),
        'cache_meta_v': nrm(ks[5], (DEPTH, DEC_BATCH, N_META, ATT_KV_HEADS, ATT_HEAD_DIM), 1.0),
        'state_hgrn': nrm(ks[6], (DEPTH, DEC_BATCH, HG_HEADS, HG_DK, HG_DV), 0.3),
        'meta': nrm(ks[7], (N_META, D_MODEL), 1.0),
        'w_in': nrm(ks[8], (DEPTH, D_MODEL, D_IN), D_MODEL ** -0.5),
        'sinks': nrm(ks[9], (DEPTH, ATT_Q_HEADS), 0.5),
        'lb_param': 1.0 + nrm(ks[10], (DEPTH + 1, D_HG), 0.5),
        'hg_norm': 1.0 + nrm(ks[11], (DEPTH, HG_DV), 0.05),
        'w_att_out': nrm(ks[12], (DEPTH, D_ATT, D_MODEL), D_ATT ** -0.5),
        'w_hg_out': nrm(ks[13], (DEPTH, D_HG, D_MODEL), D_HG ** -0.5),
        'w_o': nrm(ks[14], (DEPTH, D_MODEL, D_MODEL), D_MODEL ** -0.5),
        'ln_mix': 1.0 + nrm(ks[15], (DEPTH, D_MODEL), 0.05),
        'ln_ffn': 1.0 + nrm(ks[16], (DEPTH, D_MODEL), 0.05),
        'w_up': nrm(ks[17], (DEPTH, D_MODEL, D_FF), D_MODEL ** -0.5),
        'w_down': nrm(ks[18], (DEPTH, D_FF, D_MODEL), D_FF ** -0.5),
        'ln_f': 1.0 + nrm(ks[19], (D_MODEL,), 0.05),
    }


def reference(x_prompt, x_sample, cache_k, cache_v, cache_meta_k, cache_meta_v, state_hgrn,
              meta, w_in, sinks, lb_param, hg_norm, w_att_out, w_hg_out, w_o,
              ln_mix, ln_ffn, w_up, w_down, ln_f):
    lb_all = jnp.cumsum(jax.nn.softmax(lb_param.astype(jnp.float32), axis=0), axis=0)
    B = x_prompt.shape[0]
    meta_b = jnp.broadcast_to(meta.astype(x_prompt.dtype)[None], (B, N_META, D_MODEL))
    hp = jnp.concatenate([meta_b, x_prompt], axis=1)
    hs = x_sample
    kp_l, vp_l, mkp_l, mvp_l, sp_l, ks_l, vs_l, ss_l = [], [], [], [], [], [], [], []
    for l in range(DEPTH):
        lb = lb_all[l].reshape(HG_HEADS, HG_DK)
        qa, ka, va, qh, fh, ih, gh, ga, gb = split_projection(rmsnorm(hp, ln_mix[l]) @ w_in[l])
        att = swa_prompt(qa, ka, va, sinks[l])
        q, f, i_val = hgrn2_gates(qh, fh, ih, lb)
        o, s_fin = hgrn2_prompt(q, f, i_val)
        hp = hp + merge_branches(att, hgrn2_out(o, gh, hg_norm[l]), ga, gb,
                                 w_att_out[l], w_hg_out[l], w_o[l])
        hp = ffn_residual(hp, ln_ffn[l], w_up[l], w_down[l])
        kp_l.append(ka[:, -WINDOW:])
        vp_l.append(va[:, -WINDOW:])
        mkp_l.append(ka[:, :N_META])
        mvp_l.append(va[:, :N_META])
        sp_l.append(s_fin.astype(state_hgrn.dtype))
        qa, ka, va, qh, fh, ih, gh, ga, gb = split_projection(rmsnorm(hs, ln_mix[l]) @ w_in[l])
        att, k_new, v_new = swa_sample(qa, ka, va, cache_k[l], cache_v[l],
                                       cache_meta_k[l], cache_meta_v[l], sinks[l])
        q, f, i_val = hgrn2_gates(qh, fh, ih, lb)
        o, s_new = hgrn2_sample(q, f, i_val, state_hgrn[l])
        hs = hs + merge_branches(att, hgrn2_out(o, gh, hg_norm[l]), ga, gb,
                                 w_att_out[l], w_hg_out[l], w_o[l])
        hs = ffn_residual(hs, ln_ffn[l], w_up[l], w_down[l])
        ks_l.append(k_new)
        vs_l.append(v_new)
        ss_l.append(s_new.astype(state_hgrn.dtype))
    y_prompt = rmsnorm(hp, ln_f)[:, N_META:]
    y_sample = rmsnorm(hs, ln_f)
    return (y_prompt, y_sample, jnp.stack(kp_l), jnp.stack(vp_l), jnp.stack(mkp_l), jnp.stack(mvp_l),
            jnp.stack(sp_l), jnp.stack(ks_l), jnp.stack(vs_l), jnp.stack(ss_l))
```

```python
import functools

import jax
import jax.numpy as jnp
from jax import lax
from jax.experimental import pallas as pl
from jax.experimental.pallas import tpu as pltpu

F32 = jnp.float32
BF16 = jnp.bfloat16

D_MODEL = 1024
N_META = 16
WINDOW = 128
HEAD_DIM = 64
Q_HEADS = 8
KV_HEADS = 2
D_ATT = Q_HEADS * HEAD_DIM
D_KV = KV_HEADS * HEAD_DIM
HG_HEADS = 4
HG_DK = 128
HG_DV = 128
D_HG = HG_HEADS * HG_DK
HG_CHUNK = 64
D_FF = 4 * D_MODEL
EPS = 1e-6
C_Q = 0
C_KV = C_Q + D_ATT
C_HQ = C_KV + 2 * D_KV
C_HF = C_HQ + D_HG
C_HI = C_HF + D_HG
C_HGATE = C_HI + D_HG
C_GA = C_HGATE + D_HG
C_GB = C_GA + D_MODEL
D_IN = C_GB + D_MODEL

VMEM_LIMIT_BYTES = 56 * 1024 * 1024
PROJ_ROWS = 512
MERGE_ROWS = 512
HGRN_ROWS = 256
SAMPLE_ATT_GROUP = 16
SAMPLE_HG_GROUP = 8

_NT = (((1,), (1,)), ((), ()))
_TN = (((0,), (0,)), ((), ()))


def _dot(a, b):
    return jnp.dot(a, b, preferred_element_type=F32)


def _dot_nt(a, b):
    return lax.dot_general(a, b, _NT, preferred_element_type=F32)


def _dot_tn(a, b):
    return lax.dot_general(a, b, _TN, preferred_element_type=F32)


def _rmsnorm(x, g):
    return x * lax.rsqrt(jnp.mean(x * x, axis=-1, keepdims=True) + EPS) * g


def _resident(shape):
    return pl.BlockSpec(shape, lambda *_: (0,) * len(shape), pipeline_mode=pl.Buffered(1))


def _params(*sem):
    return pltpu.CompilerParams(dimension_semantics=sem, vmem_limit_bytes=VMEM_LIMIT_BYTES)


def _proj_kernel(x_ref, g_ref, w_ref, q_ref, kv_ref, hq_ref, hf_ref, hi_ref, hgate_ref,
                 sga_ref, sgb_ref):
    xn = _rmsnorm(x_ref[...], g_ref[...]).astype(BF16)

    def cols(lo, n):
        return _dot(xn, w_ref[:, lo:lo + n])

    q_ref[...] = (cols(C_Q, D_ATT) * (HEAD_DIM ** -0.5)).astype(q_ref.dtype)
    kv_ref[...] = cols(C_KV, 2 * D_KV)
    hq_ref[...] = cols(C_HQ, D_HG).astype(hq_ref.dtype)
    hf_ref[...] = cols(C_HF, D_HG)
    hi_ref[...] = cols(C_HI, D_HG).astype(hi_ref.dtype)
    hgate_ref[...] = cols(C_HGATE, D_HG).astype(hgate_ref.dtype)
    sga_ref[...] = jax.nn.sigmoid(cols(C_GA, D_MODEL)).astype(sga_ref.dtype)
    sgb_ref[...] = jax.nn.sigmoid(cols(C_GB, D_MODEL)).astype(sgb_ref.dtype)


def _project(x, g, w, rows, hg_dtype):
    n = x.shape[0]
    widths = (D_ATT, 2 * D_KV, D_HG, D_HG, D_HG, D_HG, D_MODEL, D_MODEL)
    dtypes = (BF16, F32, hg_dtype, F32, hg_dtype, hg_dtype, BF16, BF16)
    return pl.pallas_call(
        _proj_kernel,
        grid=(n // rows,),
        in_specs=[pl.BlockSpec((rows, D_MODEL), lambda i: (i, 0)),
                  _resident((1, D_MODEL)),
                  _resident((D_MODEL, D_IN))],
        out_specs=[pl.BlockSpec((rows, c), lambda i: (i, 0)) for c in widths],
        out_shape=[jax.ShapeDtypeStruct((n, c), d) for c, d in zip(widths, dtypes)],
        compiler_params=_params("parallel"),
        name="proj",
    )(x, g, w)


def _attn_kernel(sink_ref, q_ref, kvc_ref, kvp_ref, kvm_ref, o_ref):
    n = pl.program_id(1)
    nk = 2 * WINDOW + N_META
    kv = jnp.concatenate([kvp_ref[...], kvc_ref[...], kvm_ref[...]], axis=0)
    lane = lax.broadcasted_iota(jnp.int32, (nk, D_KV), 1)
    low = lane < HEAD_DIM

    def variants(a):
        sw = pltpu.roll(a, HEAD_DIM, axis=1)
        z = jnp.zeros_like(a)
        return ((jnp.where(low, a, z).astype(BF16), jnp.where(low, z, sw).astype(BF16)),
                (jnp.where(low, sw, z).astype(BF16), jnp.where(low, z, a).astype(BF16)))

    kvar = variants(kv[:, :D_KV])
    vvar = variants(kv[:, D_KV:])

    row = lax.broadcasted_iota(jnp.int32, (WINDOW, nk), 0)
    col = lax.broadcasted_iota(jnp.int32, (WINDOW, nk), 1)
    prev_lo = jnp.where(n > 0, row, WINDOW - 1)
    mask = jnp.logical_and(col > prev_lo, col <= row + WINDOW)
    mask = jnp.logical_or(mask, col >= 2 * WINDOW)

    for p in range(Q_HEADS // 2):
        h = (2 * p) // (Q_HEADS // KV_HEADS)
        qp = q_ref[:, p * 128:(p + 1) * 128]
        acc = None
        for half in range(2):
            sk = sink_ref[2 * p + half]
            s = jnp.where(mask, _dot_nt(qp, kvar[h][half]), -jnp.inf)
            m = jnp.maximum(jnp.max(s, axis=1, keepdims=True), sk)
            e = jnp.exp(s - m)
            l = jnp.sum(e, axis=1, keepdims=True) + jnp.exp(sk - m)
            o = _dot(e.astype(BF16), vvar[h][half]) / l
            acc = o if acc is None else acc + o
        o_ref[:, p * 128:(p + 1) * 128] = acc.astype(o_ref.dtype)


def _attention(sinks, q, kv, kv_meta):
    b, t, _ = q.shape
    nb = t // WINDOW
    return pl.pallas_call(
        _attn_kernel,
        grid=(b, nb),
        in_specs=[pl.BlockSpec(memory_space=pltpu.SMEM),
                  pl.BlockSpec((None, WINDOW, D_ATT), lambda i, j: (i, j, 0)),
                  pl.BlockSpec((None, WINDOW, 2 * D_KV), lambda i, j: (i, j, 0)),
                  pl.BlockSpec((None, WINDOW, 2 * D_KV), lambda i, j: (i, jnp.maximum(j - 1, 0), 0)),
                  _resident((N_META, 2 * D_KV))],
        out_specs=pl.BlockSpec((None, WINDOW, D_ATT), lambda i, j: (i, j, 0)),
        out_shape=jax.ShapeDtypeStruct((b, t, D_ATT), BF16),
        compiler_params=_params("parallel", "arbitrary"),
        name="attn",
    )(sinks, q, kv, kv, kv_meta)


def _lower_bound(lbp):
    e = jnp.exp(lbp - jnp.max(lbp, axis=0, keepdims=True))
    return e[0:1] / jnp.sum(e, axis=0, keepdims=True)


def _cumsum_rows(x):
    t = x.shape[0]
    r = lax.broadcasted_iota(jnp.int32, (t, t), 0)
    c = lax.broadcasted_iota(jnp.int32, (t, t), 1)
    tri = jnp.where(r >= c, 1.0, 0.0).astype(BF16)
    hi = x.astype(BF16)
    r1 = x - hi.astype(F32)
    mid = r1.astype(BF16)
    lo = (r1 - mid.astype(F32)).astype(BF16)
    return _dot(tri, hi) + _dot(tri, mid) + _dot(tri, lo)


def _decay_terms(fh, lb):
    f = lb + (1.0 - lb) * jax.nn.sigmoid(fh)
    b = _cumsum_rows(jnp.log(f))
    return 1.0 - f, b, b[-1:]


def _hg_out(o, g, nw):
    parts = []
    for h in range(HG_HEADS):
        oh = o[:, h * HG_DV:(h + 1) * HG_DV]
        parts.append(oh * lax.rsqrt(jnp.mean(oh * oh, axis=-1, keepdims=True) + EPS) * nw)
    return jnp.concatenate(parts, axis=1) * (g * jax.nn.sigmoid(g))


def _hgrn_kernel(lbp_ref, nw_ref, mf_ref, mi_ref, q_ref, f_ref, i_ref, g_ref,
                 o_ref, sfin_ref, st_ref):
    n = pl.program_id(1)
    lb = _lower_bound(lbp_ref[...])
    nw = nw_ref[...]

    @pl.when(n == 0)
    def _():
        k, b, bl = _decay_terms(mf_ref[...], lb)
        kd = (k * jnp.exp(bl - b)).astype(BF16)
        iv = mi_ref[...].astype(BF16)
        for h in range(HG_HEADS):
            hs = slice(h * HG_DK, (h + 1) * HG_DK)
            st_ref[h] = _dot_tn(iv[:, hs], kd[:, hs])

    t = HG_CHUNK
    r = lax.broadcasted_iota(jnp.int32, (t, t), 0)
    c = lax.broadcasted_iota(jnp.int32, (t, t), 1)
    causal = r >= c
    for ci in range(q_ref.shape[0] // t):
        rows = slice(ci * t, (ci + 1) * t)
        k, b, bl = _decay_terms(f_ref[rows, :], lb)
        q = q_ref[rows, :].astype(F32) * (HG_DK ** -0.5)
        qe = (q * jnp.exp(b)).astype(BF16)
        ke = (k * jnp.exp(-b)).astype(BF16)
        kd = (k * jnp.exp(bl - b)).astype(BF16)
        dec = jnp.exp(bl)
        iv = i_ref[rows, :].astype(BF16)
        outs = []
        for h in range(HG_HEADS):
            hs = slice(h * HG_DK, (h + 1) * HG_DK)
            a = jnp.where(causal, _dot_nt(qe[:, hs], ke[:, hs]), 0.0).astype(BF16)
            st = st_ref[h]
            outs.append(_dot(a, iv[:, hs]) + _dot_nt(qe[:, hs], st.astype(BF16)))
            st_ref[h] = st * dec[:, hs] + _dot_tn(iv[:, hs], kd[:, hs])
        o = _hg_out(jnp.concatenate(outs, axis=1), g_ref[rows, :].astype(F32), nw)
        o_ref[rows, :] = o.astype(o_ref.dtype)

    @pl.when(n == pl.num_programs(1) - 1)
    def _():
        for h in range(HG_HEADS):
            sfin_ref[h] = st_ref[h].T


def _hgrn(lb_param, hg_norm, meta_f, meta_i, hq, hf, hi, hgate):
    b, t, _ = hq.shape
    blk = pl.BlockSpec((None, HGRN_ROWS, D_HG), lambda i, j: (i, j, 0))
    return pl.pallas_call(
        _hgrn_kernel,
        grid=(b, t // HGRN_ROWS),
        in_specs=[_resident(lb_param.shape), _resident((1, HG_DV)),
                  _resident((N_META, D_HG)), _resident((N_META, D_HG)),
                  blk, blk, blk, blk],
        out_specs=[blk, pl.BlockSpec((None, HG_HEADS, HG_DK, HG_DV), lambda i, j: (i, 0, 0, 0))],
        out_shape=[jax.ShapeDtypeStruct((b, t, D_HG), BF16),
                   jax.ShapeDtypeStruct((b, HG_HEADS, HG_DK, HG_DV), F32)],
        scratch_shapes=[pltpu.VMEM((HG_HEADS, HG_DV, HG_DK), F32)],
        compiler_params=_params("parallel", "arbitrary"),
        name="hgrn",
    )(lb_param, hg_norm, meta_f, meta_i, hq, hf, hi, hgate)


def _merge_ffn_kernel(x_ref, att_ref, hg_ref, sga_ref, sgb_ref, wa_ref, wb_ref, wo_ref,
                      lnffn_ref, wup_ref, wdn_ref, lnf_ref, y_ref):
    ya = _dot(att_ref[...].astype(BF16), wa_ref[...])
    yb = _dot(hg_ref[...].astype(BF16), wb_ref[...])
    mix = sga_ref[...].astype(F32) * ya + sgb_ref[...].astype(F32) * yb
    h1 = x_ref[...] + _dot(mix.astype(BF16), wo_ref[...])
    xn = _rmsnorm(h1, lnffn_ref[...]).astype(BF16)
    acc = jnp.zeros_like(h1)
    step = 1024
    for c in range(D_FF // step):
        u = jnp.maximum(_dot(xn, wup_ref[:, c * step:(c + 1) * step]), 0.0)
        acc = acc + _dot((u * u).astype(BF16), wdn_ref[c * step:(c + 1) * step, :])
    y_ref[...] = _rmsnorm(h1 + acc, lnf_ref[...])


def _merge_ffn(x, att, hg, sga, sgb, wa, wb, wo, ln_ffn, w_up, w_down, ln_f, rows):
    n = x.shape[0]

    def blk(c):
        return pl.BlockSpec((rows, c), lambda i: (i, 0))

    return pl.pallas_call(
        _merge_ffn_kernel,
        grid=(n // rows,),
        in_specs=[blk(D_MODEL), blk(D_ATT), blk(D_HG), blk(D_MODEL), blk(D_MODEL),
                  _resident(wa.shape), _resident(wb.shape), _resident(wo.shape),
                  _resident((1, D_MODEL)), _resident(w_up.shape), _resident(w_down.shape),
                  _resident((1, D_MODEL))],
        out_specs=blk(D_MODEL),
        out_shape=jax.ShapeDtypeStruct((n, D_MODEL), F32),
        compiler_params=_params("parallel"),
        name="merge_ffn",
    )(x, att, hg, sga, sgb, wa, wb, wo, ln_ffn, w_up, w_down, ln_f)


def _sample_attn_kernel(sink_ref, qm_ref, kvn_ref, ck_ref, cv_ref, mk_ref, mv_ref,
                        o_ref, nk_ref, nv_ref):
    r = WINDOW
    sk = sink_ref[...]
    for b in range(qm_ref.shape[0]):
        nk_ref[b, 0:r - 1, :] = ck_ref[b, 1:r, :]
        nk_ref[b, r - 1:r, :] = kvn_ref[b:b + 1, 0:D_KV]
        nv_ref[b, 0:r - 1, :] = cv_ref[b, 1:r, :]
        nv_ref[b, r - 1:r, :] = kvn_ref[b:b + 1, D_KV:2 * D_KV]
        qm = qm_ref[b]
        s_w = _dot_nt(qm, nk_ref[b].astype(BF16))
        s_m = _dot_nt(qm, mk_ref[b].astype(BF16))
        m = jnp.maximum(jnp.maximum(jnp.max(s_w, axis=1, keepdims=True),
                                    jnp.max(s_m, axis=1, keepdims=True)), sk)
        e_w = jnp.exp(s_w - m)
        e_m = jnp.exp(s_m - m)
        l = (jnp.sum(e_w, axis=1, keepdims=True) + jnp.sum(e_m, axis=1, keepdims=True)
             + jnp.exp(sk - m))
        o = _dot(e_w.astype(BF16), nv_ref[b].astype(BF16)) + _dot(e_m.astype(BF16), mv_ref[b].astype(BF16))
        o_ref[b] = o / l


def _sample_attention(sinks_col, qm, kv_new, ck, cv, mk, mv):
    nb = qm.shape[0]
    g = SAMPLE_ATT_GROUP

    def blk3(a, c):
        return pl.BlockSpec((g, a, c), lambda i: (i, 0, 0))

    return pl.pallas_call(
        _sample_attn_kernel,
        grid=(nb // g,),
        in_specs=[_resident((Q_HEADS, 1)), blk3(Q_HEADS, D_KV),
                  pl.BlockSpec((g, 2 * D_KV), lambda i: (i, 0)),
                  blk3(WINDOW, D_KV), blk3(WINDOW, D_KV), blk3(N_META, D_KV), blk3(N_META, D_KV)],
        out_specs=[blk3(Q_HEADS, D_KV), blk3(WINDOW, D_KV), blk3(WINDOW, D_KV)],
        out_shape=[jax.ShapeDtypeStruct((nb, Q_HEADS, D_KV), F32),
                   jax.ShapeDtypeStruct((nb, WINDOW, D_KV), F32),
                   jax.ShapeDtypeStruct((nb, WINDOW, D_KV), F32)],
        compiler_params=_params("parallel"),
        name="sample_attn",
    )(sinks_col, qm, kv_new, ck, cv, mk, mv)


def _sample_hgrn_kernel(lbp_ref, nw_ref, q_ref, f_ref, i_ref, g_ref, s_ref, o_ref, snew_ref, acc_ref):
    lb = _lower_bound(lbp_ref[...])
    f = lb + (1.0 - lb) * jax.nn.sigmoid(f_ref[...])
    q = q_ref[...] * (HG_DK ** -0.5)
    iv = i_ref[...]
    r = lax.broadcasted_iota(jnp.int32, (HG_DK, HG_DK), 0)
    c = lax.broadcasted_iota(jnp.int32, (HG_DK, HG_DK), 1)
    eye = r == c

    def column(row):
        return jnp.sum(jnp.where(eye, row, 0.0), axis=1, keepdims=True)

    for b in range(s_ref.shape[0]):
        for h in range(HG_HEADS):
            hs = slice(h * HG_DK, (h + 1) * HG_DK)
            fc = column(f[b:b + 1, hs])
            qc = column(q[b:b + 1, hs])
            s = fc * s_ref[b, h] + (1.0 - fc) * iv[b:b + 1, hs]
            snew_ref[b, h] = s
            acc_ref[b:b + 1, hs] = jnp.sum(qc * s, axis=0, keepdims=True)
    o_ref[...] = _hg_out(acc_ref[...], g_ref[...], nw_ref[...])


def _sample_hgrn(lb_param, hg_norm, hq, hf, hi, hgate, state):
    nb = hq.shape[0]
    g = SAMPLE_HG_GROUP
    row = pl.BlockSpec((g, D_HG), lambda i: (i, 0))
    sblk = pl.BlockSpec((g, HG_HEADS, HG_DK, HG_DV), lambda i: (i, 0, 0, 0))
    return pl.pallas_call(
        _sample_hgrn_kernel,
        grid=(nb // g,),
        in_specs=[_resident(lb_param.shape), _resident((1, HG_DV)), row, row, row, row, sblk],
        out_specs=[row, sblk],
        out_shape=[jax.ShapeDtypeStruct((nb, D_HG), F32),
                   jax.ShapeDtypeStruct(state.shape, F32)],
        scratch_shapes=[pltpu.VMEM((g, D_HG), F32)],
        compiler_params=_params("parallel"),
        name="sample_hgrn",
    )(lb_param, hg_norm, hq, hf, hi, hgate, state)


def kernel(x_prompt, x_sample, cache_k, cache_v, cache_meta_k, cache_meta_v, state_hgrn, meta,
           w_in, sinks, lb_param, hg_norm, w_att_out, w_hg_out, w_o, ln_mix, ln_ffn, w_up,
           w_down, ln_f):
    bsz, seq, _ = x_prompt.shape
    nb = x_sample.shape[0]
    w_in_b = w_in[0].astype(BF16)
    wa = w_att_out[0].astype(BF16)
    wb = w_hg_out[0].astype(BF16)
    wo = w_o[0].astype(BF16)
    wup = w_up[0].astype(BF16)
    wdn = w_down[0].astype(BF16)
    ln_mix2 = ln_mix.reshape(1, D_MODEL)
    ln_ffn2 = ln_ffn.reshape(1, D_MODEL)
    ln_f2 = ln_f.reshape(1, D_MODEL)
    nw = hg_norm.reshape(1, HG_DV)
    sink_vec = sinks.reshape(Q_HEADS)

    xp = x_prompt.reshape(bsz * seq, D_MODEL)
    q_p, kv_p, hq_p, hf_p, hi_p, hgate_p, sga_p, sgb_p = _project(xp, ln_mix2, w_in_b, PROJ_ROWS, BF16)
    xs = jnp.concatenate([x_sample.reshape(nb, D_MODEL), meta], axis=0)
    q_s, kv_s, hq_s, hf_s, hi_s, hgate_s, sga_s, sgb_s = _project(xs, ln_mix2, w_in_b, nb + N_META, F32)
    kv_meta = kv_s[nb:]

    att_p = _attention(sink_vec, q_p.reshape(bsz, seq, D_ATT), kv_p.reshape(bsz, seq, 2 * D_KV), kv_meta)
    hg_p, state_p = _hgrn(lb_param, nw, hf_s[nb:], hi_s[nb:],
                          hq_p.reshape(bsz, seq, D_HG), hf_p.reshape(bsz, seq, D_HG),
                          hi_p.reshape(bsz, seq, D_HG), hgate_p.reshape(bsz, seq, D_HG))
    y_p = _merge_ffn(xp, att_p.reshape(bsz * seq, D_ATT), hg_p.reshape(bsz * seq, D_HG), sga_p, sgb_p,
                     wa, wb, wo, ln_ffn2, wup, wdn, ln_f2, MERGE_ROWS)

    grp = Q_HEADS // KV_HEADS
    qs4 = q_s[:nb].reshape(nb, KV_HEADS, grp, 1, HEAD_DIM)
    sel = jnp.eye(KV_HEADS, dtype=BF16).reshape(1, KV_HEADS, 1, KV_HEADS, 1)
    qm = (qs4 * sel).reshape(nb, Q_HEADS, D_KV)
    o_s, nk_s, nv_s = _sample_attention(
        sinks.reshape(Q_HEADS, 1), qm, kv_s[:nb],
        cache_k[0].reshape(nb, WINDOW, D_KV), cache_v[0].reshape(nb, WINDOW, D_KV),
        cache_meta_k[0].reshape(nb, N_META, D_KV), cache_meta_v[0].reshape(nb, N_META, D_KV))
    o5 = o_s.reshape(nb, KV_HEADS, grp, KV_HEADS, HEAD_DIM)
    att_s = jnp.stack([o5[:, h, :, h, :] for h in range(KV_HEADS)], axis=1).reshape(nb, D_ATT)
    hg_s, state_s = _sample_hgrn(lb_param, nw, hq_s[:nb], hf_s[:nb], hi_s[:nb], hgate_s[:nb], state_hgrn[0])
    y_s = _merge_ffn(x_sample.reshape(nb, D_MODEL), att_s, hg_s, sga_s[:nb], sgb_s[:nb],
                     wa, wb, wo, ln_ffn2, wup, wdn, ln_f2, nb)

    kv5 = kv_p.reshape(bsz, seq, 2, KV_HEADS, HEAD_DIM)
    meta5 = jnp.broadcast_to(kv_meta.reshape(1, N_META, 2, KV_HEADS, HEAD_DIM),
                             (bsz, N_META, 2, KV_HEADS, HEAD_DIM))
    return (y_p.reshape(bsz, seq, D_MODEL),
            y_s.reshape(nb, 1, D_MODEL),
            kv5[None, :, seq - WINDOW:, 0],
            kv5[None, :, seq - WINDOW:, 1],
            meta5[None, :, :, 0],
            meta5[None, :, :, 1],
            state_p[None],
            nk_s.reshape(1, nb, WINDOW, KV_HEADS, HEAD_DIM),
            nv_s.reshape(1, nb, WINDOW, KV_HEADS, HEAD_DIM),
            state_s[None])
```

```python
import functools

import jax
import jax.numpy as jnp
from jax import lax
from jax.experimental import pallas as pl
from jax.experimental.pallas import tpu as pltpu

F32 = jnp.float32
BF16 = jnp.bfloat16

D_MODEL = 1024
N_META = 16
WINDOW = 128
HEAD_DIM = 64
Q_HEADS = 8
KV_HEADS = 2
D_ATT = Q_HEADS * HEAD_DIM
D_KV = KV_HEADS * HEAD_DIM
HG_HEADS = 4
HG_DK = 128
HG_DV = 128
D_HG = HG_HEADS * HG_DK
HG_CHUNK = 64
D_FF = 4 * D_MODEL
EPS = 1e-6
C_Q = 0
C_KV = C_Q + D_ATT
C_HQ = C_KV + 2 * D_KV
C_HF = C_HQ + D_HG
C_HI = C_HF + D_HG
C_HGATE = C_HI + D_HG
C_GA = C_HGATE + D_HG
C_GB = C_GA + D_MODEL
D_IN = C_GB + D_MODEL

VMEM_LIMIT_BYTES = 56 * 1024 * 1024
PROJ_ROWS = 512
MERGE_ROWS = 512
HGRN_ROWS = 512
ATT_ROWS = 512
SAMPLE_ATT_GROUP = 16
SAMPLE_HG_GROUP = 8

_NT = (((1,), (1,)), ((), ()))
_TN = (((0,), (0,)), ((), ()))


def _dot(a, b):
    return jnp.dot(a, b, preferred_element_type=F32)


def _dot_nt(a, b):
    return lax.dot_general(a, b, _NT, preferred_element_type=F32)


def _dot_tn(a, b):
    return lax.dot_general(a, b, _TN, preferred_element_type=F32)


def _rmsnorm(x, g):
    return x * lax.rsqrt(jnp.mean(x * x, axis=-1, keepdims=True) + EPS) * g


def _resident(shape):
    return pl.BlockSpec(shape, lambda *_: (0,) * len(shape), pipeline_mode=pl.Buffered(1))


def _params(*sem):
    return pltpu.CompilerParams(dimension_semantics=sem, vmem_limit_bytes=VMEM_LIMIT_BYTES)


def _proj_kernel(x_ref, g_ref, w_ref, q_ref, kv_ref, hq_ref, hf_ref, hi_ref, hgate_ref,
                 sga_ref, sgb_ref):
    xn = _rmsnorm(x_ref[...], g_ref[...]).astype(BF16)

    def cols(lo, n):
        return _dot(xn, w_ref[:, lo:lo + n])

    q_ref[...] = (cols(C_Q, D_ATT) * (HEAD_DIM ** -0.5)).astype(q_ref.dtype)
    kv_ref[...] = cols(C_KV, 2 * D_KV)
    hq_ref[...] = cols(C_HQ, D_HG).astype(hq_ref.dtype)
    hf_ref[...] = cols(C_HF, D_HG)
    hi_ref[...] = cols(C_HI, D_HG).astype(hi_ref.dtype)
    hgate_ref[...] = cols(C_HGATE, D_HG).astype(hgate_ref.dtype)
    sga_ref[...] = jax.nn.sigmoid(cols(C_GA, D_MODEL)).astype(sga_ref.dtype)
    sgb_ref[...] = jax.nn.sigmoid(cols(C_GB, D_MODEL)).astype(sgb_ref.dtype)


def _project(x, g, w, rows, hg_dtype):
    n = x.shape[0]
    widths = (D_ATT, 2 * D_KV, D_HG, D_HG, D_HG, D_HG, D_MODEL, D_MODEL)
    dtypes = (BF16, F32, hg_dtype, F32, hg_dtype, hg_dtype, BF16, BF16)
    return pl.pallas_call(
        _proj_kernel,
        grid=(n // rows,),
        in_specs=[pl.BlockSpec((rows, D_MODEL), lambda i: (i, 0)),
                  _resident((1, D_MODEL)),
                  _resident((D_MODEL, D_IN))],
        out_specs=[pl.BlockSpec((rows, c), lambda i: (i, 0)) for c in widths],
        out_shape=[jax.ShapeDtypeStruct((n, c), d) for c, d in zip(widths, dtypes)],
        compiler_params=_params("parallel"),
        name="proj",
    )(x, g, w)


def _attn_kernel(sink_ref, q_ref, kvc_ref, kvp_ref, kvm_ref, o_ref):
    n = pl.program_id(1)
    tq = q_ref.shape[0]
    nk = 2 * WINDOW + N_META
    meta0 = WINDOW + tq
    kv = jnp.concatenate([kvp_ref[...], kvc_ref[...], kvm_ref[...]], axis=0)
    lane = lax.broadcasted_iota(jnp.int32, (kv.shape[0], D_KV), 1)
    low = lane < HEAD_DIM

    k = kv[:, :D_KV]
    ksw = pltpu.roll(k, HEAD_DIM, axis=1)
    kboth = (jnp.where(low, k, ksw).astype(BF16), jnp.where(low, ksw, k).astype(BF16))
    qlane = lax.broadcasted_iota(jnp.int32, (1, 128), 1)
    keep = (jnp.where(qlane < HEAD_DIM, 1.0, 0.0).astype(BF16),
            jnp.where(qlane < HEAD_DIM, 0.0, 1.0).astype(BF16))
    r = lax.broadcasted_iota(jnp.int32, (D_KV, D_KV), 0)
    c = lax.broadcasted_iota(jnp.int32, (D_KV, D_KV), 1)
    eye = jnp.where(r == c, 1.0, 0.0).astype(BF16)
    vt = _dot_nt(eye, kv[:, D_KV:].astype(BF16)).astype(BF16)
    ones = jnp.ones((16, nk), BF16)

    key = lax.broadcasted_iota(jnp.int32, (nk, 2 * WINDOW), 0)
    col = lax.broadcasted_iota(jnp.int32, (nk, 2 * WINDOW), 1)
    qry = jnp.bitwise_and(col, WINDOW - 1)
    first_head = lax.broadcasted_iota(jnp.int32, (1, 2 * WINDOW), 1) < WINDOW
    def raw_scores(i):
        r0 = i * WINDOW
        scores = []
        for p in range(Q_HEADS // 2):
            kb = kboth[(2 * p) // (Q_HEADS // KV_HEADS)]
            kmat = jnp.concatenate([kb[r0:r0 + 2 * WINDOW], kb[meta0:]], axis=0)
            qp = q_ref[r0:r0 + WINDOW, p * 128:(p + 1) * 128]
            q2 = jnp.concatenate([qp * keep[0], qp * keep[1]], axis=0)
            scores.append(_dot_nt(kmat, q2))
        return scores

    nsub = tq // WINDOW
    ahead = raw_scores(0)
    for i in range(nsub):
        r0 = i * WINDOW
        scores = ahead
        if i + 1 < nsub:
            ahead = raw_scores(i + 1)
        lo = jnp.where(n > 0, qry, WINDOW - 1) if i == 0 else qry
        mask = jnp.logical_and(key > lo, key <= qry + WINDOW)
        mask = jnp.logical_or(mask, key >= 2 * WINDOW)
        vaug = []
        for h in range(KV_HEADS):
            vth = vt[h * HEAD_DIM:(h + 1) * HEAD_DIM]
            vaug.append(jnp.concatenate(
                [jnp.concatenate([vth[:, r0:r0 + 2 * WINDOW], vth[:, meta0:]], axis=1), ones], axis=0))
        outs = []
        for p in range(Q_HEADS // 2):
            h = (2 * p) // (Q_HEADS // KV_HEADS)
            s = jnp.where(mask, scores[p], -jnp.inf)
            sk = jnp.where(first_head, sink_ref[2 * p], sink_ref[2 * p + 1])
            m = jnp.maximum(jnp.max(s, axis=0, keepdims=True), sk)
            e = jnp.exp(s - m).astype(BF16)
            oa = _dot(vaug[h], e)
            o = oa[:HEAD_DIM] / (oa[HEAD_DIM:HEAD_DIM + 1] + jnp.exp(sk - m))
            outs += [o[:, :WINDOW], o[:, WINDOW:]]
        o_ref[r0:r0 + WINDOW, :] = jnp.concatenate(outs, axis=0).T.astype(o_ref.dtype)


def _attention(sinks, q, kv, kv_meta):
    b, t, _ = q.shape
    per = ATT_ROWS // WINDOW
    return pl.pallas_call(
        _attn_kernel,
        grid=(b, t // ATT_ROWS),
        in_specs=[pl.BlockSpec(memory_space=pltpu.SMEM),
                  pl.BlockSpec((None, ATT_ROWS, D_ATT), lambda i, j: (i, j, 0)),
                  pl.BlockSpec((None, ATT_ROWS, 2 * D_KV), lambda i, j: (i, j, 0)),
                  pl.BlockSpec((None, WINDOW, 2 * D_KV), lambda i, j: (i, jnp.maximum(j * per - 1, 0), 0)),
                  _resident((N_META, 2 * D_KV))],
        out_specs=pl.BlockSpec((None, ATT_ROWS, D_ATT), lambda i, j: (i, j, 0)),
        out_shape=jax.ShapeDtypeStruct((b, t, D_ATT), BF16),
        compiler_params=_params("parallel", "arbitrary"),
        name="attn",
    )(sinks, q, kv, kv, kv_meta)


def _lower_bound(lbp):
    e = jnp.exp(lbp - jnp.max(lbp, axis=0, keepdims=True))
    return e[0:1] / jnp.sum(e, axis=0, keepdims=True)


def _cumsum_rows(x):
    t = x.shape[0]
    r = lax.broadcasted_iota(jnp.int32, (t, t), 0)
    c = lax.broadcasted_iota(jnp.int32, (t, t), 1)
    tri = jnp.where(r >= c, 1.0, 0.0).astype(BF16)
    hi = x.astype(BF16)
    r1 = x - hi.astype(F32)
    mid = r1.astype(BF16)
    lo = (r1 - mid.astype(F32)).astype(BF16)
    return _dot(tri, hi) + _dot(tri, mid) + _dot(tri, lo)


def _decay_terms(fh, lb):
    f = lb + (1.0 - lb) * jax.nn.sigmoid(fh)
    b = _cumsum_rows(jnp.log(f))
    return 1.0 - f, b, b[-1:]


def _hg_out(o, g, nw):
    parts = []
    for h in range(HG_HEADS):
        oh = o[:, h * HG_DV:(h + 1) * HG_DV]
        parts.append(oh * lax.rsqrt(jnp.mean(oh * oh, axis=-1, keepdims=True) + EPS) * nw)
    return jnp.concatenate(parts, axis=1) * (g * jax.nn.sigmoid(g))


def _hgrn_kernel(lbp_ref, nw_ref, mf_ref, mi_ref, q_ref, f_ref, i_ref, g_ref,
                 o_ref, sfin_ref, st_ref):
    n = pl.program_id(1)
    lb = _lower_bound(lbp_ref[...])
    nw = nw_ref[...]

    @pl.when(n == 0)
    def _():
        k, b, bl = _decay_terms(mf_ref[...], lb)
        kd = (k * jnp.exp(bl - b)).astype(BF16)
        iv = mi_ref[...].astype(BF16)
        for h in range(HG_HEADS):
            hs = slice(h * HG_DK, (h + 1) * HG_DK)
            st_ref[h] = _dot_tn(iv[:, hs], kd[:, hs])

    t = HG_CHUNK
    r = lax.broadcasted_iota(jnp.int32, (t, t), 0)
    c = lax.broadcasted_iota(jnp.int32, (t, t), 1)
    causal = r >= c
    nchunk = q_ref.shape[0] // t
    heads = [slice(h * HG_DK, (h + 1) * HG_DK) for h in range(HG_HEADS)]
    decays = [_decay_terms(f_ref[ci * t:(ci + 1) * t, :], lb) for ci in range(nchunk)]
    terms = []
    for ci, (k, b, bl) in enumerate(decays):
        rows = slice(ci * t, (ci + 1) * t)
        q = q_ref[rows, :].astype(F32) * (HG_DK ** -0.5)
        qe = (q * jnp.exp(b)).astype(BF16)
        ke = (k * jnp.exp(-b)).astype(BF16)
        kd = (k * jnp.exp(bl - b)).astype(BF16)
        iv = i_ref[rows, :].astype(BF16)
        a = [jnp.where(causal, _dot_nt(qe[:, hs], ke[:, hs]), 0.0).astype(BF16) for hs in heads]
        u = [_dot_tn(iv[:, hs], kd[:, hs]) for hs in heads]
        terms.append((qe, iv, a, u, jnp.exp(bl)))
    st = [st_ref[h] for h in range(HG_HEADS)]
    entering = []
    for qe, iv, a, u, dec in terms:
        entering.append([s.astype(BF16) for s in st])
        st = [s * dec[:, hs] + uh for s, hs, uh in zip(st, heads, u)]
    for h in range(HG_HEADS):
        st_ref[h] = st[h]
    for ci, (qe, iv, a, u, dec) in enumerate(terms):
        rows = slice(ci * t, (ci + 1) * t)
        outs = [_dot(a[h], iv[:, hs]) + _dot_nt(qe[:, hs], entering[ci][h])
                for h, hs in enumerate(heads)]
        o = _hg_out(jnp.concatenate(outs, axis=1), g_ref[rows, :].astype(F32), nw)
        o_ref[rows, :] = o.astype(o_ref.dtype)

    @pl.when(n == pl.num_programs(1) - 1)
    def _():
        for h in range(HG_HEADS):
            sfin_ref[h] = st_ref[h].T


def _hgrn(lb_param, hg_norm, meta_f, meta_i, hq, hf, hi, hgate):
    b, t, _ = hq.shape
    blk = pl.BlockSpec((None, HGRN_ROWS, D_HG), lambda i, j: (i, j, 0))
    return pl.pallas_call(
        _hgrn_kernel,
        grid=(b, t // HGRN_ROWS),
        in_specs=[_resident(lb_param.shape), _resident((1, HG_DV)),
                  _resident((N_META, D_HG)), _resident((N_META, D_HG)),
                  blk, blk, blk, blk],
        out_specs=[blk, pl.BlockSpec((None, HG_HEADS, HG_DK, HG_DV), lambda i, j: (i, 0, 0, 0))],
        out_shape=[jax.ShapeDtypeStruct((b, t, D_HG), BF16),
                   jax.ShapeDtypeStruct((b, HG_HEADS, HG_DK, HG_DV), F32)],
        scratch_shapes=[pltpu.VMEM((HG_HEADS, HG_DV, HG_DK), F32)],
        compiler_params=_params("parallel", "arbitrary"),
        name="hgrn",
    )(lb_param, hg_norm, meta_f, meta_i, hq, hf, hi, hgate)


def _merge_ffn_kernel(x_ref, att_ref, hg_ref, sga_ref, sgb_ref, wa_ref, wb_ref, wo_ref,
                      lnffn_ref, wup_ref, wdn_ref, lnf_ref, y_ref):
    ya = _dot(att_ref[...].astype(BF16), wa_ref[...])
    yb = _dot(hg_ref[...].astype(BF16), wb_ref[...])
    mix = sga_ref[...].astype(F32) * ya + sgb_ref[...].astype(F32) * yb
    h1 = x_ref[...] + _dot(mix.astype(BF16), wo_ref[...])
    xn = _rmsnorm(h1, lnffn_ref[...]).astype(BF16)
    acc = jnp.zeros_like(h1)
    step = 1024
    for c in range(D_FF // step):
        u = jnp.maximum(_dot(xn, wup_ref[:, c * step:(c + 1) * step]), 0.0)
        acc = acc + _dot((u * u).astype(BF16), wdn_ref[c * step:(c + 1) * step, :])
    y_ref[...] = _rmsnorm(h1 + acc, lnf_ref[...])


def _merge_ffn(x, att, hg, sga, sgb, wa, wb, wo, ln_ffn, w_up, w_down, ln_f, rows):
    n = x.shape[0]

    def blk(c):
        return pl.BlockSpec((rows, c), lambda i: (i, 0))

    return pl.pallas_call(
        _merge_ffn_kernel,
        grid=(n // rows,),
        in_specs=[blk(D_MODEL), blk(D_ATT), blk(D_HG), blk(D_MODEL), blk(D_MODEL),
                  _resident(wa.shape), _resident(wb.shape), _resident(wo.shape),
                  _resident((1, D_MODEL)), _resident(w_up.shape), _resident(w_down.shape),
                  _resident((1, D_MODEL))],
        out_specs=blk(D_MODEL),
        out_shape=jax.ShapeDtypeStruct((n, D_MODEL), F32),
        compiler_params=_params("parallel"),
        name="merge_ffn",
    )(x, att, hg, sga, sgb, wa, wb, wo, ln_ffn, w_up, w_down, ln_f)


def _sample_attn_kernel(sink_ref, qm_ref, kvn_ref, ck_ref, cv_ref, mk_ref, mv_ref,
                        o_ref, nk_ref, nv_ref):
    r = WINDOW
    sk = sink_ref[...]
    for b in range(qm_ref.shape[0]):
        nk_ref[b, 0:r - 1, :] = ck_ref[b, 1:r, :]
        nk_ref[b, r - 1:r, :] = kvn_ref[b:b + 1, 0:D_KV]
        nv_ref[b, 0:r - 1, :] = cv_ref[b, 1:r, :]
        nv_ref[b, r - 1:r, :] = kvn_ref[b:b + 1, D_KV:2 * D_KV]
        qm = qm_ref[b]
        s_w = _dot_nt(qm, nk_ref[b].astype(BF16))
        s_m = _dot_nt(qm, mk_ref[b].astype(BF16))
        m = jnp.maximum(jnp.maximum(jnp.max(s_w, axis=1, keepdims=True),
                                    jnp.max(s_m, axis=1, keepdims=True)), sk)
        e_w = jnp.exp(s_w - m)
        e_m = jnp.exp(s_m - m)
        l = (jnp.sum(e_w, axis=1, keepdims=True) + jnp.sum(e_m, axis=1, keepdims=True)
             + jnp.exp(sk - m))
        o = _dot(e_w.astype(BF16), nv_ref[b].astype(BF16)) + _dot(e_m.astype(BF16), mv_ref[b].astype(BF16))
        o_ref[b] = o / l


def _sample_attention(sinks_col, qm, kv_new, ck, cv, mk, mv):
    nb = qm.shape[0]
    g = SAMPLE_ATT_GROUP

    def blk3(a, c):
        return pl.BlockSpec((g, a, c), lambda i: (i, 0, 0))

    return pl.pallas_call(
        _sample_attn_kernel,
        grid=(nb // g,),
        in_specs=[_resident((Q_HEADS, 1)), blk3(Q_HEADS, D_KV),
                  pl.BlockSpec((g, 2 * D_KV), lambda i: (i, 0)),
                  blk3(WINDOW, D_KV), blk3(WINDOW, D_KV), blk3(N_META, D_KV), blk3(N_META, D_KV)],
        out_specs=[blk3(Q_HEADS, D_KV), blk3(WINDOW, D_KV), blk3(WINDOW, D_KV)],
        out_shape=[jax.ShapeDtypeStruct((nb, Q_HEADS, D_KV), F32),
                   jax.ShapeDtypeStruct((nb, WINDOW, D_KV), F32),
                   jax.ShapeDtypeStruct((nb, WINDOW, D_KV), F32)],
        compiler_params=_params("parallel"),
        name="sample_attn",
    )(sinks_col, qm, kv_new, ck, cv, mk, mv)


def _sample_hgrn_kernel(lbp_ref, nw_ref, q_ref, f_ref, i_ref, g_ref, s_ref, o_ref, snew_ref, acc_ref):
    lb = _lower_bound(lbp_ref[...])
    f = lb + (1.0 - lb) * jax.nn.sigmoid(f_ref[...])
    q = q_ref[...] * (HG_DK ** -0.5)
    iv = i_ref[...]
    r = lax.broadcasted_iota(jnp.int32, (HG_DK, HG_DK), 0)
    c = lax.broadcasted_iota(jnp.int32, (HG_DK, HG_DK), 1)
    eye = r == c

    def column(row):
        return jnp.sum(jnp.where(eye, row, 0.0), axis=1, keepdims=True)

    for b in range(s_ref.shape[0]):
        for h in range(HG_HEADS):
            hs = slice(h * HG_DK, (h + 1) * HG_DK)
            fc = column(f[b:b + 1, hs])
            qc = column(q[b:b + 1, hs])
            s = fc * s_ref[b, h] + (1.0 - fc) * iv[b:b + 1, hs]
            snew_ref[b, h] = s
            acc_ref[b:b + 1, hs] = jnp.sum(qc * s, axis=0, keepdims=True)
    o_ref[...] = _hg_out(acc_ref[...], g_ref[...], nw_ref[...])


def _sample_hgrn(lb_param, hg_norm, hq, hf, hi, hgate, state):
    nb = hq.shape[0]
    g = SAMPLE_HG_GROUP
    row = pl.BlockSpec((g, D_HG), lambda i: (i, 0))
    sblk = pl.BlockSpec((g, HG_HEADS, HG_DK, HG_DV), lambda i: (i, 0, 0, 0))
    return pl.pallas_call(
        _sample_hgrn_kernel,
        grid=(nb // g,),
        in_specs=[_resident(lb_param.shape), _resident((1, HG_DV)), row, row, row, row, sblk],
        out_specs=[row, sblk],
        out_shape=[jax.ShapeDtypeStruct((nb, D_HG), F32),
                   jax.ShapeDtypeStruct(state.shape, F32)],
        scratch_shapes=[pltpu.VMEM((g, D_HG), F32)],
        compiler_params=_params("parallel"),
        name="sample_hgrn",
    )(lb_param, hg_norm, hq, hf, hi, hgate, state)


def kernel(x_prompt, x_sample, cache_k, cache_v, cache_meta_k, cache_meta_v, state_hgrn, meta,
           w_in, sinks, lb_param, hg_norm, w_att_out, w_hg_out, w_o, ln_mix, ln_ffn, w_up,
           w_down, ln_f):
    bsz, seq, _ = x_prompt.shape
    nb = x_sample.shape[0]
    w_in_b = w_in[0].astype(BF16)
    wa = w_att_out[0].astype(BF16)
    wb = w_hg_out[0].astype(BF16)
    wo = w_o[0].astype(BF16)
    wup = w_up[0].astype(BF16)
    wdn = w_down[0].astype(BF16)
    ln_mix2 = ln_mix.reshape(1, D_MODEL)
    ln_ffn2 = ln_ffn.reshape(1, D_MODEL)
    ln_f2 = ln_f.reshape(1, D_MODEL)
    nw = hg_norm.reshape(1, HG_DV)
    sink_vec = sinks.reshape(Q_HEADS)

    xp = x_prompt.reshape(bsz * seq, D_MODEL)
    q_p, kv_p, hq_p, hf_p, hi_p, hgate_p, sga_p, sgb_p = _project(xp, ln_mix2, w_in_b, PROJ_ROWS, BF16)
    xs = jnp.concatenate([x_sample.reshape(nb, D_MODEL), meta], axis=0)
    q_s, kv_s, hq_s, hf_s, hi_s, hgate_s, sga_s, sgb_s = _project(xs, ln_mix2, w_in_b, nb + N_META, F32)
    kv_meta = kv_s[nb:]

    att_p = _attention(sink_vec, q_p.reshape(bsz, seq, D_ATT), kv_p.reshape(bsz, seq, 2 * D_KV), kv_meta)
    hg_p, state_p = _hgrn(lb_param, nw, hf_s[nb:], hi_s[nb:],
                          hq_p.reshape(bsz, seq, D_HG), hf_p.reshape(bsz, seq, D_HG),
                          hi_p.reshape(bsz, seq, D_HG), hgate_p.reshape(bsz, seq, D_HG))
    y_p = _merge_ffn(xp, att_p.reshape(bsz * seq, D_ATT), hg_p.reshape(bsz * seq, D_HG), sga_p, sgb_p,
                     wa, wb, wo, ln_ffn2, wup, wdn, ln_f2, MERGE_ROWS)

    grp = Q_HEADS // KV_HEADS
    qs4 = q_s[:nb].reshape(nb, KV_HEADS, grp, 1, HEAD_DIM)
    sel = jnp.eye(KV_HEADS, dtype=BF16).reshape(1, KV_HEADS, 1, KV_HEADS, 1)
    qm = (qs4 * sel).reshape(nb, Q_HEADS, D_KV)
    o_s, nk_s, nv_s = _sample_attention(
        sinks.reshape(Q_HEADS, 1), qm, kv_s[:nb],
        cache_k[0].reshape(nb, WINDOW, D_KV), cache_v[0].reshape(nb, WINDOW, D_KV),
        cache_meta_k[0].reshape(nb, N_META, D_KV), cache_meta_v[0].reshape(nb, N_META, D_KV))
    o5 = o_s.reshape(nb, KV_HEADS, grp, KV_HEADS, HEAD_DIM)
    att_s = jnp.stack([o5[:, h, :, h, :] for h in range(KV_HEADS)], axis=1).reshape(nb, D_ATT)
    hg_s, state_s = _sample_hgrn(lb_param, nw, hq_s[:nb], hf_s[:nb], hi_s[:nb], hgate_s[:nb], state_hgrn[0])
    y_s = _merge_ffn(x_sample.reshape(nb, D_MODEL), att_s, hg_s, sga_s[:nb], sgb_s[:nb],
                     wa, wb, wo, ln_ffn2, wup, wdn, ln_f2, nb)

    kv5 = kv_p.reshape(bsz, seq, 2, KV_HEADS, HEAD_DIM)
    meta5 = jnp.broadcast_to(kv_meta.reshape(1, N_META, 2, KV_HEADS, HEAD_DIM),
                             (bsz, N_META, 2, KV_HEADS, HEAD_DIM))
    return (y_p.reshape(bsz, seq, D_MODEL),
            y_s.reshape(nb, 1, D_MODEL),
            kv5[None, :, seq - WINDOW:, 0],
            kv5[None, :, seq - WINDOW:, 1],
            meta5[None, :, :, 0],
            meta5[None, :, :, 1],
            state_p[None],
            nk_s.reshape(1, nb, WINDOW, KV_HEADS, HEAD_DIM),
            nv_s.reshape(1, nb, WINDOW, KV_HEADS, HEAD_DIM),
            state_s[None])
```

```python
import functools

import jax
import jax.numpy as jnp
from jax import lax
from jax.experimental import pallas as pl
from jax.experimental.pallas import tpu as pltpu

F32 = jnp.float32
BF16 = jnp.bfloat16

D_MODEL = 1024
N_META = 16
WINDOW = 128
HEAD_DIM = 64
Q_HEADS = 8
KV_HEADS = 2
D_ATT = Q_HEADS * HEAD_DIM
D_KV = KV_HEADS * HEAD_DIM
HG_HEADS = 4
HG_DK = 128
HG_DV = 128
D_HG = HG_HEADS * HG_DK
HG_CHUNK = 64
D_FF = 4 * D_MODEL
EPS = 1e-6
C_Q = 0
C_KV = C_Q + D_ATT
C_HQ = C_KV + 2 * D_KV
C_HF = C_HQ + D_HG
C_HI = C_HF + D_HG
C_HGATE = C_HI + D_HG
C_GA = C_HGATE + D_HG
C_GB = C_GA + D_MODEL
D_IN = C_GB + D_MODEL

VMEM_LIMIT_BYTES = 56 * 1024 * 1024
PROJ_ROWS = 512
MERGE_ROWS = 512
HGRN_ROWS = 512
ATT_ROWS = 512
SAMPLE_ATT_GROUP = 16
SAMPLE_HG_GROUP = 16

_NT = (((1,), (1,)), ((), ()))
_TN = (((0,), (0,)), ((), ()))


def _dot(a, b):
    return jnp.dot(a, b, preferred_element_type=F32)


def _dot_nt(a, b):
    return lax.dot_general(a, b, _NT, preferred_element_type=F32)


def _dot_tn(a, b):
    return lax.dot_general(a, b, _TN, preferred_element_type=F32)


def _rmsnorm(x, g):
    return x * lax.rsqrt(jnp.mean(x * x, axis=-1, keepdims=True) + EPS) * g


def _resident(shape):
    return pl.BlockSpec(shape, lambda *_: (0,) * len(shape), pipeline_mode=pl.Buffered(1))


def _params(*sem):
    return pltpu.CompilerParams(dimension_semantics=sem, vmem_limit_bytes=VMEM_LIMIT_BYTES)


def _proj_kernel(x_ref, g_ref, w_ref, q_ref, kv_ref, hq_ref, hf_ref, hi_ref, hgate_ref,
                 sga_ref, sgb_ref):
    xn = _rmsnorm(x_ref[...], g_ref[...]).astype(BF16)

    def cols(lo, n):
        return _dot(xn, w_ref[:, lo:lo + n])

    q_ref[...] = (cols(C_Q, D_ATT) * (HEAD_DIM ** -0.5)).astype(q_ref.dtype)
    kv_ref[...] = cols(C_KV, 2 * D_KV)
    hq_ref[...] = cols(C_HQ, D_HG).astype(hq_ref.dtype)
    hf_ref[...] = cols(C_HF, D_HG)
    hi_ref[...] = cols(C_HI, D_HG).astype(hi_ref.dtype)
    hgate_ref[...] = cols(C_HGATE, D_HG).astype(hgate_ref.dtype)
    sga_ref[...] = jax.nn.sigmoid(cols(C_GA, D_MODEL)).astype(sga_ref.dtype)
    sgb_ref[...] = jax.nn.sigmoid(cols(C_GB, D_MODEL)).astype(sgb_ref.dtype)


def _project(x, g, w, rows, hg_dtype):
    n = x.shape[0]
    widths = (D_ATT, 2 * D_KV, D_HG, D_HG, D_HG, D_HG, D_MODEL, D_MODEL)
    dtypes = (BF16, F32, hg_dtype, F32, hg_dtype, hg_dtype, BF16, BF16)
    return pl.pallas_call(
        _proj_kernel,
        grid=(n // rows,),
        in_specs=[pl.BlockSpec((rows, D_MODEL), lambda i: (i, 0)),
                  _resident((1, D_MODEL)),
                  _resident((D_MODEL, D_IN))],
        out_specs=[pl.BlockSpec((rows, c), lambda i: (i, 0)) for c in widths],
        out_shape=[jax.ShapeDtypeStruct((n, c), d) for c, d in zip(widths, dtypes)],
        compiler_params=_params("parallel"),
        name="proj",
    )(x, g, w)


def _attn_kernel(sink_ref, q_ref, kvc_ref, kvp_ref, kvm_ref, o_ref):
    n = pl.program_id(1)
    tq = q_ref.shape[0]
    nk = 2 * WINDOW + N_META
    meta0 = WINDOW + tq
    kv = jnp.concatenate([kvp_ref[...], kvc_ref[...], kvm_ref[...]], axis=0)
    lane = lax.broadcasted_iota(jnp.int32, (kv.shape[0], D_KV), 1)
    low = lane < HEAD_DIM

    k = kv[:, :D_KV]
    ksw = pltpu.roll(k, HEAD_DIM, axis=1)
    kboth = (jnp.where(low, k, ksw).astype(BF16), jnp.where(low, ksw, k).astype(BF16))
    qlane = lax.broadcasted_iota(jnp.int32, (1, 128), 1)
    keep = (jnp.where(qlane < HEAD_DIM, 1.0, 0.0).astype(BF16),
            jnp.where(qlane < HEAD_DIM, 0.0, 1.0).astype(BF16))
    r = lax.broadcasted_iota(jnp.int32, (D_KV, D_KV), 0)
    c = lax.broadcasted_iota(jnp.int32, (D_KV, D_KV), 1)
    eye = jnp.where(r == c, 1.0, 0.0).astype(BF16)
    vt = _dot_nt(eye, kv[:, D_KV:].astype(BF16)).astype(BF16)
    ones = jnp.ones((16, nk), BF16)

    key = lax.broadcasted_iota(jnp.int32, (nk, 2 * WINDOW), 0)
    col = lax.broadcasted_iota(jnp.int32, (nk, 2 * WINDOW), 1)
    qry = jnp.bitwise_and(col, WINDOW - 1)
    first_head = lax.broadcasted_iota(jnp.int32, (1, 2 * WINDOW), 1) < WINDOW
    def raw_scores(i):
        r0 = i * WINDOW
        scores = []
        for p in range(Q_HEADS // 2):
            kb = kboth[(2 * p) // (Q_HEADS // KV_HEADS)]
            kmat = jnp.concatenate([kb[r0:r0 + 2 * WINDOW], kb[meta0:]], axis=0)
            qp = q_ref[r0:r0 + WINDOW, p * 128:(p + 1) * 128]
            q2 = jnp.concatenate([qp * keep[0], qp * keep[1]], axis=0)
            scores.append(_dot_nt(kmat, q2))
        return scores

    nsub = tq // WINDOW
    ahead = raw_scores(0)
    for i in range(nsub):
        r0 = i * WINDOW
        scores = ahead
        if i + 1 < nsub:
            ahead = raw_scores(i + 1)
        lo = jnp.where(n > 0, qry, WINDOW - 1) if i == 0 else qry
        mask = jnp.logical_and(key > lo, key <= qry + WINDOW)
        mask = jnp.logical_or(mask, key >= 2 * WINDOW)
        vaug = []
        for h in range(KV_HEADS):
            vth = vt[h * HEAD_DIM:(h + 1) * HEAD_DIM]
            vaug.append(jnp.concatenate(
                [jnp.concatenate([vth[:, r0:r0 + 2 * WINDOW], vth[:, meta0:]], axis=1), ones], axis=0))
        outs = []
        for p in range(Q_HEADS // 2):
            h = (2 * p) // (Q_HEADS // KV_HEADS)
            s = jnp.where(mask, scores[p], -jnp.inf)
            sk = jnp.where(first_head, sink_ref[2 * p], sink_ref[2 * p + 1])
            m = jnp.maximum(jnp.max(s, axis=0, keepdims=True), sk)
            e = jnp.exp(s - m).astype(BF16)
            oa = _dot(vaug[h], e)
            o = oa[:HEAD_DIM] / (oa[HEAD_DIM:HEAD_DIM + 1] + jnp.exp(sk - m))
            outs += [o[:, :WINDOW], o[:, WINDOW:]]
        o_ref[r0:r0 + WINDOW, :] = jnp.concatenate(outs, axis=0).T.astype(o_ref.dtype)


def _attention(sinks, q, kv, kv_meta):
    b, t, _ = q.shape
    per = ATT_ROWS // WINDOW
    return pl.pallas_call(
        _attn_kernel,
        grid=(b, t // ATT_ROWS),
        in_specs=[pl.BlockSpec(memory_space=pltpu.SMEM),
                  pl.BlockSpec((None, ATT_ROWS, D_ATT), lambda i, j: (i, j, 0)),
                  pl.BlockSpec((None, ATT_ROWS, 2 * D_KV), lambda i, j: (i, j, 0)),
                  pl.BlockSpec((None, WINDOW, 2 * D_KV), lambda i, j: (i, jnp.maximum(j * per - 1, 0), 0)),
                  _resident((N_META, 2 * D_KV))],
        out_specs=pl.BlockSpec((None, ATT_ROWS, D_ATT), lambda i, j: (i, j, 0)),
        out_shape=jax.ShapeDtypeStruct((b, t, D_ATT), BF16),
        compiler_params=_params("parallel", "arbitrary"),
        name="attn",
    )(sinks, q, kv, kv, kv_meta)


def _lower_bound(lbp):
    e = jnp.exp(lbp - jnp.max(lbp, axis=0, keepdims=True))
    return e[0:1] / jnp.sum(e, axis=0, keepdims=True)


def _cumsum_rows(x):
    t = x.shape[0]
    r = lax.broadcasted_iota(jnp.int32, (t, t), 0)
    c = lax.broadcasted_iota(jnp.int32, (t, t), 1)
    tri = jnp.where(r >= c, 1.0, 0.0).astype(BF16)
    hi = x.astype(BF16)
    r1 = x - hi.astype(F32)
    mid = r1.astype(BF16)
    lo = (r1 - mid.astype(F32)).astype(BF16)
    return _dot(tri, hi) + _dot(tri, mid) + _dot(tri, lo)


def _decay_terms(fh, lb):
    f = lb + (1.0 - lb) * jax.nn.sigmoid(fh)
    b = _cumsum_rows(jnp.log(f))
    return 1.0 - f, b, b[-1:]


def _hg_out(o, g, nw):
    parts = []
    for h in range(HG_HEADS):
        oh = o[:, h * HG_DV:(h + 1) * HG_DV]
        parts.append(oh * lax.rsqrt(jnp.mean(oh * oh, axis=-1, keepdims=True) + EPS) * nw)
    return jnp.concatenate(parts, axis=1) * (g * jax.nn.sigmoid(g))


def _hgrn_kernel(lbp_ref, nw_ref, mf_ref, mi_ref, q_ref, f_ref, i_ref, g_ref,
                 o_ref, sfin_ref, st_ref):
    n = pl.program_id(1)
    lb = _lower_bound(lbp_ref[...])
    nw = nw_ref[...]

    @pl.when(n == 0)
    def _():
        k, b, bl = _decay_terms(mf_ref[...], lb)
        kd = (k * jnp.exp(bl - b)).astype(BF16)
        iv = mi_ref[...].astype(BF16)
        for h in range(HG_HEADS):
            hs = slice(h * HG_DK, (h + 1) * HG_DK)
            st_ref[h] = _dot_tn(iv[:, hs], kd[:, hs])

    t = HG_CHUNK
    r = lax.broadcasted_iota(jnp.int32, (t, t), 0)
    c = lax.broadcasted_iota(jnp.int32, (t, t), 1)
    causal = r >= c
    nchunk = q_ref.shape[0] // t
    heads = [slice(h * HG_DK, (h + 1) * HG_DK) for h in range(HG_HEADS)]
    decays = [_decay_terms(f_ref[ci * t:(ci + 1) * t, :], lb) for ci in range(nchunk)]
    terms = []
    for ci, (k, b, bl) in enumerate(decays):
        rows = slice(ci * t, (ci + 1) * t)
        q = q_ref[rows, :].astype(F32) * (HG_DK ** -0.5)
        qe = (q * jnp.exp(b)).astype(BF16)
        ke = (k * jnp.exp(-b)).astype(BF16)
        kd = (k * jnp.exp(bl - b)).astype(BF16)
        iv = i_ref[rows, :].astype(BF16)
        a = [jnp.where(causal, _dot_nt(qe[:, hs], ke[:, hs]), 0.0).astype(BF16) for hs in heads]
        u = [_dot_tn(iv[:, hs], kd[:, hs]) for hs in heads]
        terms.append((qe, iv, a, u, jnp.exp(bl)))
    st = [st_ref[h] for h in range(HG_HEADS)]
    entering = []
    for qe, iv, a, u, dec in terms:
        entering.append([s.astype(BF16) for s in st])
        st = [s * dec[:, hs] + uh for s, hs, uh in zip(st, heads, u)]
    for h in range(HG_HEADS):
        st_ref[h] = st[h]
    for ci, (qe, iv, a, u, dec) in enumerate(terms):
        rows = slice(ci * t, (ci + 1) * t)
        outs = [_dot(a[h], iv[:, hs]) + _dot_nt(qe[:, hs], entering[ci][h])
                for h, hs in enumerate(heads)]
        o = _hg_out(jnp.concatenate(outs, axis=1), g_ref[rows, :].astype(F32), nw)
        o_ref[rows, :] = o.astype(o_ref.dtype)

    @pl.when(n == pl.num_programs(1) - 1)
    def _():
        for h in range(HG_HEADS):
            sfin_ref[h] = st_ref[h].T


def _hgrn(lb_param, hg_norm, meta_f, meta_i, hq, hf, hi, hgate):
    b, t, _ = hq.shape
    blk = pl.BlockSpec((None, HGRN_ROWS, D_HG), lambda i, j: (i, j, 0))
    return pl.pallas_call(
        _hgrn_kernel,
        grid=(b, t // HGRN_ROWS),
        in_specs=[_resident(lb_param.shape), _resident((1, HG_DV)),
                  _resident((N_META, D_HG)), _resident((N_META, D_HG)),
                  blk, blk, blk, blk],
        out_specs=[blk, pl.BlockSpec((None, HG_HEADS, HG_DK, HG_DV), lambda i, j: (i, 0, 0, 0))],
        out_shape=[jax.ShapeDtypeStruct((b, t, D_HG), BF16),
                   jax.ShapeDtypeStruct((b, HG_HEADS, HG_DK, HG_DV), F32)],
        scratch_shapes=[pltpu.VMEM((HG_HEADS, HG_DV, HG_DK), F32)],
        compiler_params=_params("parallel", "arbitrary"),
        name="hgrn",
    )(lb_param, hg_norm, meta_f, meta_i, hq, hf, hi, hgate)


def _merge_ffn_kernel(x_ref, att_ref, hg_ref, sga_ref, sgb_ref, wa_ref, wb_ref, wo_ref,
                      lnffn_ref, wup_ref, wdn_ref, lnf_ref, y_ref):
    ya = _dot(att_ref[...].astype(BF16), wa_ref[...])
    yb = _dot(hg_ref[...].astype(BF16), wb_ref[...])
    mix = sga_ref[...].astype(F32) * ya + sgb_ref[...].astype(F32) * yb
    h1 = x_ref[...] + _dot(mix.astype(BF16), wo_ref[...])
    xn = _rmsnorm(h1, lnffn_ref[...]).astype(BF16)
    acc = jnp.zeros_like(h1)
    step = 1024
    for c in range(D_FF // step):
        u = jnp.maximum(_dot(xn, wup_ref[:, c * step:(c + 1) * step]), 0.0)
        acc = acc + _dot((u * u).astype(BF16), wdn_ref[c * step:(c + 1) * step, :])
    y_ref[...] = _rmsnorm(h1 + acc, lnf_ref[...])


def _merge_ffn(x, att, hg, sga, sgb, wa, wb, wo, ln_ffn, w_up, w_down, ln_f, rows):
    n = x.shape[0]

    def blk(c):
        return pl.BlockSpec((rows, c), lambda i: (i, 0))

    return pl.pallas_call(
        _merge_ffn_kernel,
        grid=(n // rows,),
        in_specs=[blk(D_MODEL), blk(D_ATT), blk(D_HG), blk(D_MODEL), blk(D_MODEL),
                  _resident(wa.shape), _resident(wb.shape), _resident(wo.shape),
                  _resident((1, D_MODEL)), _resident(w_up.shape), _resident(w_down.shape),
                  _resident((1, D_MODEL))],
        out_specs=blk(D_MODEL),
        out_shape=jax.ShapeDtypeStruct((n, D_MODEL), F32),
        compiler_params=_params("parallel"),
        name="merge_ffn",
    )(x, att, hg, sga, sgb, wa, wb, wo, ln_ffn, w_up, w_down, ln_f)


def _sample_attn_kernel(sink_ref, qm_ref, kvn_ref, ck_ref, cv_ref, mk_ref, mv_ref,
                        o_ref, nk_ref, nv_ref):
    r = WINDOW
    nb = qm_ref.shape[0]
    sk = sink_ref[...]
    for b in range(nb):
        nk_ref[b, 0:r - 1, :] = ck_ref[b, 1:r, :]
        nk_ref[b, r - 1:r, :] = kvn_ref[b:b + 1, 0:D_KV]
        nv_ref[b, 0:r - 1, :] = cv_ref[b, 1:r, :]
        nv_ref[b, r - 1:r, :] = kvn_ref[b:b + 1, D_KV:2 * D_KV]
    scores = [(_dot_nt(qm_ref[b], nk_ref[b].astype(BF16)),
               _dot_nt(qm_ref[b], mk_ref[b].astype(BF16)))
              for b in range(nb)]
    for b, (s_w, s_m) in enumerate(scores):
        m = jnp.maximum(jnp.maximum(jnp.max(s_w, axis=1, keepdims=True),
                                    jnp.max(s_m, axis=1, keepdims=True)), sk)
        e_w = jnp.exp(s_w - m)
        e_m = jnp.exp(s_m - m)
        l = (jnp.sum(e_w, axis=1, keepdims=True) + jnp.sum(e_m, axis=1, keepdims=True)
             + jnp.exp(sk - m))
        o = _dot(e_w.astype(BF16), nv_ref[b].astype(BF16)) + _dot(e_m.astype(BF16), mv_ref[b].astype(BF16))
        o_ref[b] = o / l


def _sample_attention(sinks_col, qm, kv_new, ck, cv, mk, mv):
    nb = qm.shape[0]
    g = SAMPLE_ATT_GROUP

    def blk3(a, c):
        return pl.BlockSpec((g, a, c), lambda i: (i, 0, 0))

    return pl.pallas_call(
        _sample_attn_kernel,
        grid=(nb // g,),
        in_specs=[_resident((Q_HEADS, 1)), blk3(Q_HEADS, D_KV),
                  pl.BlockSpec((g, 2 * D_KV), lambda i: (i, 0)),
                  blk3(WINDOW, D_KV), blk3(WINDOW, D_KV), blk3(N_META, D_KV), blk3(N_META, D_KV)],
        out_specs=[blk3(Q_HEADS, D_KV), blk3(WINDOW, D_KV), blk3(WINDOW, D_KV)],
        out_shape=[jax.ShapeDtypeStruct((nb, Q_HEADS, D_KV), F32),
                   jax.ShapeDtypeStruct((nb, WINDOW, D_KV), F32),
                   jax.ShapeDtypeStruct((nb, WINDOW, D_KV), F32)],
        compiler_params=_params("parallel"),
        name="sample_attn",
    )(sinks_col, qm, kv_new, ck, cv, mk, mv)


def _sample_hgrn_kernel(lbp_ref, nw_ref, q_ref, f_ref, i_ref, g_ref, s_ref, o_ref, snew_ref, acc_ref):
    lb = _lower_bound(lbp_ref[...])
    f = lb + (1.0 - lb) * jax.nn.sigmoid(f_ref[...])
    q = q_ref[...] * (HG_DK ** -0.5)
    iv = i_ref[...]
    r = lax.broadcasted_iota(jnp.int32, (HG_DK, HG_DK), 0)
    c = lax.broadcasted_iota(jnp.int32, (HG_DK, HG_DK), 1)
    eye = r == c

    def column(row):
        return jnp.sum(jnp.where(eye, row, 0.0), axis=1, keepdims=True)

    for b in range(s_ref.shape[0]):
        for h in range(HG_HEADS):
            hs = slice(h * HG_DK, (h + 1) * HG_DK)
            fc = column(f[b:b + 1, hs])
            qc = column(q[b:b + 1, hs])
            s = fc * s_ref[b, h] + (1.0 - fc) * iv[b:b + 1, hs]
            snew_ref[b, h] = s
            acc_ref[b:b + 1, hs] = jnp.sum(qc * s, axis=0, keepdims=True)
    o_ref[...] = _hg_out(acc_ref[...], g_ref[...], nw_ref[...])


def _sample_hgrn(lb_param, hg_norm, hq, hf, hi, hgate, state):
    nb = hq.shape[0]
    g = SAMPLE_HG_GROUP
    row = pl.BlockSpec((g, D_HG), lambda i: (i, 0))
    sblk = pl.BlockSpec((g, HG_HEADS, HG_DK, HG_DV), lambda i: (i, 0, 0, 0))
    return pl.pallas_call(
        _sample_hgrn_kernel,
        grid=(nb // g,),
        in_specs=[_resident(lb_param.shape), _resident((1, HG_DV)), row, row, row, row, sblk],
        out_specs=[row, sblk],
        out_shape=[jax.ShapeDtypeStruct((nb, D_HG), F32),
                   jax.ShapeDtypeStruct(state.shape, F32)],
        scratch_shapes=[pltpu.VMEM((g, D_HG), F32)],
        compiler_params=_params("parallel"),
        name="sample_hgrn",
    )(lb_param, hg_norm, hq, hf, hi, hgate, state)


def kernel(x_prompt, x_sample, cache_k, cache_v, cache_meta_k, cache_meta_v, state_hgrn, meta,
           w_in, sinks, lb_param, hg_norm, w_att_out, w_hg_out, w_o, ln_mix, ln_ffn, w_up,
           w_down, ln_f):
    bsz, seq, _ = x_prompt.shape
    nb = x_sample.shape[0]
    w_in_b = w_in[0].astype(BF16)
    wa = w_att_out[0].astype(BF16)
    wb = w_hg_out[0].astype(BF16)
    wo = w_o[0].astype(BF16)
    wup = w_up[0].astype(BF16)
    wdn = w_down[0].astype(BF16)
    ln_mix2 = ln_mix.reshape(1, D_MODEL)
    ln_ffn2 = ln_ffn.reshape(1, D_MODEL)
    ln_f2 = ln_f.reshape(1, D_MODEL)
    nw = hg_norm.reshape(1, HG_DV)
    sink_vec = sinks.reshape(Q_HEADS)

    xp = x_prompt.reshape(bsz * seq, D_MODEL)
    q_p, kv_p, hq_p, hf_p, hi_p, hgate_p, sga_p, sgb_p = _project(xp, ln_mix2, w_in_b, PROJ_ROWS, BF16)
    xs = jnp.concatenate([x_sample.reshape(nb, D_MODEL), meta], axis=0)
    q_s, kv_s, hq_s, hf_s, hi_s, hgate_s, sga_s, sgb_s = _project(xs, ln_mix2, w_in_b, nb + N_META, F32)
    kv_meta = kv_s[nb:]

    att_p = _attention(sink_vec, q_p.reshape(bsz, seq, D_ATT), kv_p.reshape(bsz, seq, 2 * D_KV), kv_meta)
    hg_p, state_p = _hgrn(lb_param, nw, hf_s[nb:], hi_s[nb:],
                          hq_p.reshape(bsz, seq, D_HG), hf_p.reshape(bsz, seq, D_HG),
                          hi_p.reshape(bsz, seq, D_HG), hgate_p.reshape(bsz, seq, D_HG))
    y_p = _merge_ffn(xp, att_p.reshape(bsz * seq, D_ATT), hg_p.reshape(bsz * seq, D_HG), sga_p, sgb_p,
                     wa, wb, wo, ln_ffn2, wup, wdn, ln_f2, MERGE_ROWS)

    grp = Q_HEADS // KV_HEADS
    qs4 = q_s[:nb].reshape(nb, KV_HEADS, grp, 1, HEAD_DIM)
    sel = jnp.eye(KV_HEADS, dtype=BF16).reshape(1, KV_HEADS, 1, KV_HEADS, 1)
    qm = (qs4 * sel).reshape(nb, Q_HEADS, D_KV)
    o_s, nk_s, nv_s = _sample_attention(
        sinks.reshape(Q_HEADS, 1), qm, kv_s[:nb],
        cache_k[0].reshape(nb, WINDOW, D_KV), cache_v[0].reshape(nb, WINDOW, D_KV),
        cache_meta_k[0].reshape(nb, N_META, D_KV), cache_meta_v[0].reshape(nb, N_META, D_KV))
    o5 = o_s.reshape(nb, KV_HEADS, grp, KV_HEADS, HEAD_DIM)
    att_s = jnp.stack([o5[:, h, :, h, :] for h in range(KV_HEADS)], axis=1).reshape(nb, D_ATT)
    hg_s, state_s = _sample_hgrn(lb_param, nw, hq_s[:nb], hf_s[:nb], hi_s[:nb], hgate_s[:nb], state_hgrn[0])
    y_s = _merge_ffn(x_sample.reshape(nb, D_MODEL), att_s, hg_s, sga_s[:nb], sgb_s[:nb],
                     wa, wb, wo, ln_ffn2, wup, wdn, ln_f2, nb)

    kv5 = kv_p.reshape(bsz, seq, 2 * D_KV)[:, seq - WINDOW:].reshape(bsz, WINDOW, 2, KV_HEADS, HEAD_DIM)
    meta5 = jnp.broadcast_to(kv_meta.reshape(1, N_META, 2, KV_HEADS, HEAD_DIM),
                             (bsz, N_META, 2, KV_HEADS, HEAD_DIM))
    return (y_p.reshape(bsz, seq, D_MODEL),
            y_s.reshape(nb, 1, D_MODEL),
            kv5[None, :, :, 0],
            kv5[None, :, :, 1],
            meta5[None, :, :, 0],
            meta5[None, :, :, 1],
            state_p[None],
            nk_s.reshape(1, nb, WINDOW, KV_HEADS, HEAD_DIM),
            nv_s.reshape(1, nb, WINDOW, KV_HEADS, HEAD_DIM),
            state_s[None])
```

```python
import functools

import jax
import jax.numpy as jnp
from jax import lax
from jax.experimental import pallas as pl
from jax.experimental.pallas import tpu as pltpu

F32 = jnp.float32
BF16 = jnp.bfloat16

D_MODEL = 1024
N_META = 16
WINDOW = 128
HEAD_DIM = 64
Q_HEADS = 8
KV_HEADS = 2
D_ATT = Q_HEADS * HEAD_DIM
D_KV = KV_HEADS * HEAD_DIM
HG_HEADS = 4
HG_DK = 128
HG_DV = 128
D_HG = HG_HEADS * HG_DK
HG_CHUNK = 64
D_FF = 4 * D_MODEL
EPS = 1e-6
C_Q = 0
C_KV = C_Q + D_ATT
C_HQ = C_KV + 2 * D_KV
C_HF = C_HQ + D_HG
C_HI = C_HF + D_HG
C_HGATE = C_HI + D_HG
C_GA = C_HGATE + D_HG
C_GB = C_GA + D_MODEL
D_IN = C_GB + D_MODEL

VMEM_LIMIT_BYTES = 56 * 1024 * 1024
PROJ_ROWS = 1024
MERGE_ROWS = 512
HGRN_ROWS = 512
ATT_ROWS = 512
SAMPLE_ATT_GROUP = 16
SAMPLE_HG_GROUP = 16

_NT = (((1,), (1,)), ((), ()))
_TN = (((0,), (0,)), ((), ()))


def _dot(a, b):
    return jnp.dot(a, b, preferred_element_type=F32)


def _dot_nt(a, b):
    return lax.dot_general(a, b, _NT, preferred_element_type=F32)


def _dot_tn(a, b):
    return lax.dot_general(a, b, _TN, preferred_element_type=F32)


def _rmsnorm(x, g):
    return x * lax.rsqrt(jnp.mean(x * x, axis=-1, keepdims=True) + EPS) * g


def _resident(shape):
    return pl.BlockSpec(shape, lambda *_: (0,) * len(shape), pipeline_mode=pl.Buffered(1))


def _params(*sem):
    return pltpu.CompilerParams(dimension_semantics=sem, vmem_limit_bytes=VMEM_LIMIT_BYTES)


def _proj_kernel(x_ref, g_ref, lbp_ref, w_ref, q_ref, kv_ref, hq_ref, hk_ref, hlogf_ref, hi_ref,
                 hgate_ref, sga_ref, sgb_ref):
    xn = _rmsnorm(x_ref[...], g_ref[...]).astype(BF16)

    def cols(lo, n):
        return _dot(xn, w_ref[:, lo:lo + n])

    q_ref[...] = (cols(C_Q, D_ATT) * (HEAD_DIM ** -0.5)).astype(q_ref.dtype)
    kv_ref[...] = cols(C_KV, 2 * D_KV)
    hq_ref[...] = (cols(C_HQ, D_HG) * (HG_DK ** -0.5)).astype(hq_ref.dtype)
    lb = _lower_bound(lbp_ref[...])
    f = lb + (1.0 - lb) * jax.nn.sigmoid(cols(C_HF, D_HG))
    hk_ref[...] = (1.0 - f).astype(hk_ref.dtype)
    hlogf_ref[...] = jnp.log(f)
    hi_ref[...] = cols(C_HI, D_HG).astype(hi_ref.dtype)
    g = cols(C_HGATE, D_HG)
    hgate_ref[...] = (g * jax.nn.sigmoid(g)).astype(hgate_ref.dtype)
    sga_ref[...] = jax.nn.sigmoid(cols(C_GA, D_MODEL)).astype(sga_ref.dtype)
    sgb_ref[...] = jax.nn.sigmoid(cols(C_GB, D_MODEL)).astype(sgb_ref.dtype)


def _project(x, g, lb_param, w, rows, hg_dtype):
    n = x.shape[0]
    widths = (D_ATT, 2 * D_KV, D_HG, D_HG, D_HG, D_HG, D_HG, D_MODEL, D_MODEL)
    dtypes = (BF16, F32, hg_dtype, hg_dtype, F32, hg_dtype, hg_dtype, BF16, BF16)
    return pl.pallas_call(
        _proj_kernel,
        grid=(n // rows,),
        in_specs=[pl.BlockSpec((rows, D_MODEL), lambda i: (i, 0)),
                  _resident((1, D_MODEL)),
                  _resident(lb_param.shape),
                  _resident((D_MODEL, D_IN))],
        out_specs=[pl.BlockSpec((rows, c), lambda i: (i, 0)) for c in widths],
        out_shape=[jax.ShapeDtypeStruct((n, c), d) for c, d in zip(widths, dtypes)],
        compiler_params=_params("parallel"),
        name="proj",
    )(x, g, lb_param, w)


def _attn_kernel(sink_ref, q_ref, kvc_ref, kvp_ref, kvm_ref, o_ref):
    n = pl.program_id(1)
    tq = q_ref.shape[0]
    nk = 2 * WINDOW + N_META
    meta0 = WINDOW + tq
    kv = jnp.concatenate([kvp_ref[...], kvc_ref[...], kvm_ref[...]], axis=0)
    lane = lax.broadcasted_iota(jnp.int32, (kv.shape[0], D_KV), 1)
    low = lane < HEAD_DIM

    k = kv[:, :D_KV]
    ksw = pltpu.roll(k, HEAD_DIM, axis=1)
    kboth = (jnp.where(low, k, ksw).astype(BF16), jnp.where(low, ksw, k).astype(BF16))
    qlane = lax.broadcasted_iota(jnp.int32, (1, 128), 1)
    keep = (jnp.where(qlane < HEAD_DIM, 1.0, 0.0).astype(BF16),
            jnp.where(qlane < HEAD_DIM, 0.0, 1.0).astype(BF16))
    r = lax.broadcasted_iota(jnp.int32, (D_KV, D_KV), 0)
    c = lax.broadcasted_iota(jnp.int32, (D_KV, D_KV), 1)
    eye = jnp.where(r == c, 1.0, 0.0).astype(BF16)
    vt = _dot_nt(eye, kv[:, D_KV:].astype(BF16)).astype(BF16)
    ones = jnp.ones((16, nk), BF16)

    key = lax.broadcasted_iota(jnp.int32, (nk, 2 * WINDOW), 0)
    col = lax.broadcasted_iota(jnp.int32, (nk, 2 * WINDOW), 1)
    qry = jnp.bitwise_and(col, WINDOW - 1)
    first_head = lax.broadcasted_iota(jnp.int32, (1, 2 * WINDOW), 1) < WINDOW
    def raw_scores(i):
        r0 = i * WINDOW
        scores = []
        for p in range(Q_HEADS // 2):
            kb = kboth[(2 * p) // (Q_HEADS // KV_HEADS)]
            kmat = jnp.concatenate([kb[r0:r0 + 2 * WINDOW], kb[meta0:]], axis=0)
            qp = q_ref[r0:r0 + WINDOW, p * 128:(p + 1) * 128]
            q2 = jnp.concatenate([qp * keep[0], qp * keep[1]], axis=0)
            scores.append(_dot_nt(kmat, q2))
        return scores

    nsub = tq // WINDOW
    ahead = raw_scores(0)
    for i in range(nsub):
        r0 = i * WINDOW
        scores = ahead
        if i + 1 < nsub:
            ahead = raw_scores(i + 1)
        lo = jnp.where(n > 0, qry, WINDOW - 1) if i == 0 else qry
        mask = jnp.logical_and(key > lo, key <= qry + WINDOW)
        mask = jnp.logical_or(mask, key >= 2 * WINDOW)
        vaug = []
        for h in range(KV_HEADS):
            vth = vt[h * HEAD_DIM:(h + 1) * HEAD_DIM]
            vaug.append(jnp.concatenate(
                [jnp.concatenate([vth[:, r0:r0 + 2 * WINDOW], vth[:, meta0:]], axis=1), ones], axis=0))
        outs = []
        for p in range(Q_HEADS // 2):
            h = (2 * p) // (Q_HEADS // KV_HEADS)
            s = jnp.where(mask, scores[p], -jnp.inf)
            sk = jnp.where(first_head, sink_ref[2 * p], sink_ref[2 * p + 1])
            m = jnp.maximum(jnp.max(s, axis=0, keepdims=True), sk)
            e = jnp.exp(s - m).astype(BF16)
            oa = _dot(vaug[h], e)
            o = oa[:HEAD_DIM] / (oa[HEAD_DIM:HEAD_DIM + 1] + jnp.exp(sk - m))
            outs += [o[:, :WINDOW], o[:, WINDOW:]]
        o_ref[r0:r0 + WINDOW, :] = jnp.concatenate(outs, axis=0).T.astype(o_ref.dtype)


def _attention(sinks, q, kv, kv_meta):
    b, t, _ = q.shape
    per = ATT_ROWS // WINDOW
    return pl.pallas_call(
        _attn_kernel,
        grid=(b, t // ATT_ROWS),
        in_specs=[pl.BlockSpec(memory_space=pltpu.SMEM),
                  pl.BlockSpec((None, ATT_ROWS, D_ATT), lambda i, j: (i, j, 0)),
                  pl.BlockSpec((None, ATT_ROWS, 2 * D_KV), lambda i, j: (i, j, 0)),
                  pl.BlockSpec((None, WINDOW, 2 * D_KV), lambda i, j: (i, jnp.maximum(j * per - 1, 0), 0)),
                  _resident((N_META, 2 * D_KV))],
        out_specs=pl.BlockSpec((None, ATT_ROWS, D_ATT), lambda i, j: (i, j, 0)),
        out_shape=jax.ShapeDtypeStruct((b, t, D_ATT), BF16),
        compiler_params=_params("parallel", "arbitrary"),
        name="attn",
    )(sinks, q, kv, kv, kv_meta)


def _lower_bound(lbp):
    e = jnp.exp(lbp - jnp.max(lbp, axis=0, keepdims=True))
    return e[0:1] / jnp.sum(e, axis=0, keepdims=True)


def _cumsum_rows(x):
    t = x.shape[0]
    r = lax.broadcasted_iota(jnp.int32, (t, t), 0)
    c = lax.broadcasted_iota(jnp.int32, (t, t), 1)
    tri = jnp.where(r >= c, 1.0, 0.0).astype(BF16)
    hi = x.astype(BF16)
    r1 = x - hi.astype(F32)
    mid = r1.astype(BF16)
    lo = (r1 - mid.astype(F32)).astype(BF16)
    return _dot(tri, hi) + _dot(tri, mid) + _dot(tri, lo)


def _hg_out(o, gate, nw):
    parts = []
    for h in range(HG_HEADS):
        oh = o[:, h * HG_DV:(h + 1) * HG_DV]
        parts.append(oh * lax.rsqrt(jnp.mean(oh * oh, axis=-1, keepdims=True) + EPS) * nw)
    return jnp.concatenate(parts, axis=1) * gate


def _hgrn_kernel(nw_ref, mk_ref, mlogf_ref, mi_ref, q_ref, k_ref, logf_ref, i_ref, g_ref,
                 o_ref, sfin_ref, st_ref):
    n = pl.program_id(1)
    nw = nw_ref[...]

    @pl.when(n == 0)
    def _():
        b = _cumsum_rows(mlogf_ref[...])
        kd = (mk_ref[...] * jnp.exp(b[-1:] - b)).astype(BF16)
        iv = mi_ref[...].astype(BF16)
        for h in range(HG_HEADS):
            hs = slice(h * HG_DK, (h + 1) * HG_DK)
            st_ref[h] = _dot_tn(iv[:, hs], kd[:, hs])

    t = HG_CHUNK
    r = lax.broadcasted_iota(jnp.int32, (t, t), 0)
    c = lax.broadcasted_iota(jnp.int32, (t, t), 1)
    causal = r >= c
    nchunk = q_ref.shape[0] // t
    heads = [slice(h * HG_DK, (h + 1) * HG_DK) for h in range(HG_HEADS)]
    cums = [_cumsum_rows(logf_ref[ci * t:(ci + 1) * t, :]) for ci in range(nchunk)]
    terms = []
    for ci, b in enumerate(cums):
        rows = slice(ci * t, (ci + 1) * t)
        bl = b[-1:]
        k = k_ref[rows, :].astype(F32)
        qe = (q_ref[rows, :].astype(F32) * jnp.exp(b)).astype(BF16)
        ke = (k * jnp.exp(-b)).astype(BF16)
        kd = (k * jnp.exp(bl - b)).astype(BF16)
        iv = i_ref[rows, :].astype(BF16)
        a = [jnp.where(causal, _dot_nt(qe[:, hs], ke[:, hs]), 0.0).astype(BF16) for hs in heads]
        u = [_dot_tn(iv[:, hs], kd[:, hs]) for hs in heads]
        terms.append((qe, iv, a, u, jnp.exp(bl)))
    st = [st_ref[h] for h in range(HG_HEADS)]
    entering = []
    for qe, iv, a, u, dec in terms:
        entering.append([s.astype(BF16) for s in st])
        st = [s * dec[:, hs] + uh for s, hs, uh in zip(st, heads, u)]
    for h in range(HG_HEADS):
        st_ref[h] = st[h]
    for ci, (qe, iv, a, u, dec) in enumerate(terms):
        rows = slice(ci * t, (ci + 1) * t)
        outs = [_dot(a[h], iv[:, hs]) + _dot_nt(qe[:, hs], entering[ci][h])
                for h, hs in enumerate(heads)]
        o = _hg_out(jnp.concatenate(outs, axis=1), g_ref[rows, :].astype(F32), nw)
        o_ref[rows, :] = o.astype(o_ref.dtype)

    @pl.when(n == pl.num_programs(1) - 1)
    def _():
        for h in range(HG_HEADS):
            sfin_ref[h] = st_ref[h].T


def _hgrn(hg_norm, meta_k, meta_logf, meta_i, hq, hk, hlogf, hi, hgate):
    b, t, _ = hq.shape
    blk = pl.BlockSpec((None, HGRN_ROWS, D_HG), lambda i, j: (i, j, 0))
    small = _resident((N_META, D_HG))
    return pl.pallas_call(
        _hgrn_kernel,
        grid=(b, t // HGRN_ROWS),
        in_specs=[_resident((1, HG_DV)), small, small, small, blk, blk, blk, blk, blk],
        out_specs=[blk, pl.BlockSpec((None, HG_HEADS, HG_DK, HG_DV), lambda i, j: (i, 0, 0, 0))],
        out_shape=[jax.ShapeDtypeStruct((b, t, D_HG), BF16),
                   jax.ShapeDtypeStruct((b, HG_HEADS, HG_DK, HG_DV), F32)],
        scratch_shapes=[pltpu.VMEM((HG_HEADS, HG_DV, HG_DK), F32)],
        compiler_params=_params("parallel", "arbitrary"),
        name="hgrn",
    )(hg_norm, meta_k, meta_logf, meta_i, hq, hk, hlogf, hi, hgate)


def _merge_ffn_kernel(x_ref, att_ref, hg_ref, sga_ref, sgb_ref, wa_ref, wb_ref, wo_ref,
                      lnffn_ref, wup_ref, wdn_ref, lnf_ref, y_ref):
    ya = _dot(att_ref[...].astype(BF16), wa_ref[...])
    yb = _dot(hg_ref[...].astype(BF16), wb_ref[...])
    mix = sga_ref[...].astype(F32) * ya + sgb_ref[...].astype(F32) * yb
    h1 = x_ref[...] + _dot(mix.astype(BF16), wo_ref[...])
    xn = _rmsnorm(h1, lnffn_ref[...]).astype(BF16)
    acc = jnp.zeros_like(h1)
    step = 1024
    for c in range(D_FF // step):
        u = jnp.maximum(_dot(xn, wup_ref[:, c * step:(c + 1) * step]), 0.0)
        acc = acc + _dot((u * u).astype(BF16), wdn_ref[c * step:(c + 1) * step, :])
    y_ref[...] = _rmsnorm(h1 + acc, lnf_ref[...])


def _merge_ffn(x, att, hg, sga, sgb, wa, wb, wo, ln_ffn, w_up, w_down, ln_f, rows):
    n = x.shape[0]

    def blk(c):
        return pl.BlockSpec((rows, c), lambda i: (i, 0))

    return pl.pallas_call(
        _merge_ffn_kernel,
        grid=(n // rows,),
        in_specs=[blk(D_MODEL), blk(D_ATT), blk(D_HG), blk(D_MODEL), blk(D_MODEL),
                  _resident(wa.shape), _resident(wb.shape), _resident(wo.shape),
                  _resident((1, D_MODEL)), _resident(w_up.shape), _resident(w_down.shape),
                  _resident((1, D_MODEL))],
        out_specs=blk(D_MODEL),
        out_shape=jax.ShapeDtypeStruct((n, D_MODEL), F32),
        compiler_params=_params("parallel"),
        name="merge_ffn",
    )(x, att, hg, sga, sgb, wa, wb, wo, ln_ffn, w_up, w_down, ln_f)


def _sample_attn_kernel(sink_ref, qm_ref, kvn_ref, ck_ref, cv_ref, mk_ref, mv_ref,
                        o_ref, nk_ref, nv_ref):
    r = WINDOW
    nb = qm_ref.shape[0]
    sk = sink_ref[...]
    for b in range(nb):
        nk_ref[b, 0:r - 1, :] = ck_ref[b, 1:r, :]
        nk_ref[b, r - 1:r, :] = kvn_ref[b:b + 1, 0:D_KV]
        nv_ref[b, 0:r - 1, :] = cv_ref[b, 1:r, :]
        nv_ref[b, r - 1:r, :] = kvn_ref[b:b + 1, D_KV:2 * D_KV]
    scores = [(_dot_nt(qm_ref[b], nk_ref[b].astype(BF16)),
               _dot_nt(qm_ref[b], mk_ref[b].astype(BF16)))
              for b in range(nb)]
    for b, (s_w, s_m) in enumerate(scores):
        m = jnp.maximum(jnp.maximum(jnp.max(s_w, axis=1, keepdims=True),
                                    jnp.max(s_m, axis=1, keepdims=True)), sk)
        e_w = jnp.exp(s_w - m)
        e_m = jnp.exp(s_m - m)
        l = (jnp.sum(e_w, axis=1, keepdims=True) + jnp.sum(e_m, axis=1, keepdims=True)
             + jnp.exp(sk - m))
        o = _dot(e_w.astype(BF16), nv_ref[b].astype(BF16)) + _dot(e_m.astype(BF16), mv_ref[b].astype(BF16))
        o_ref[b] = o / l


def _sample_attention(sinks_col, qm, kv_new, ck, cv, mk, mv):
    nb = qm.shape[0]
    g = SAMPLE_ATT_GROUP

    def blk3(a, c):
        return pl.BlockSpec((g, a, c), lambda i: (i, 0, 0))

    return pl.pallas_call(
        _sample_attn_kernel,
        grid=(nb // g,),
        in_specs=[_resident((Q_HEADS, 1)), blk3(Q_HEADS, D_KV),
                  pl.BlockSpec((g, 2 * D_KV), lambda i: (i, 0)),
                  blk3(WINDOW, D_KV), blk3(WINDOW, D_KV), blk3(N_META, D_KV), blk3(N_META, D_KV)],
        out_specs=[blk3(Q_HEADS, D_KV), blk3(WINDOW, D_KV), blk3(WINDOW, D_KV)],
        out_shape=[jax.ShapeDtypeStruct((nb, Q_HEADS, D_KV), F32),
                   jax.ShapeDtypeStruct((nb, WINDOW, D_KV), F32),
                   jax.ShapeDtypeStruct((nb, WINDOW, D_KV), F32)],
        compiler_params=_params("parallel"),
        name="sample_attn",
    )(sinks_col, qm, kv_new, ck, cv, mk, mv)


def _sample_hgrn_kernel(nw_ref, q_ref, k_ref, i_ref, g_ref, s_ref, o_ref, snew_ref, acc_ref):
    k = k_ref[...]
    q = q_ref[...]
    iv = i_ref[...]
    r = lax.broadcasted_iota(jnp.int32, (HG_DK, HG_DK), 0)
    c = lax.broadcasted_iota(jnp.int32, (HG_DK, HG_DK), 1)
    eye = r == c

    def column(row):
        return jnp.sum(jnp.where(eye, row, 0.0), axis=1, keepdims=True)

    for b in range(s_ref.shape[0]):
        for h in range(HG_HEADS):
            hs = slice(h * HG_DK, (h + 1) * HG_DK)
            kc = column(k[b:b + 1, hs])
            qc = column(q[b:b + 1, hs])
            s_old = s_ref[b, h]
            s = s_old + kc * (iv[b:b + 1, hs] - s_old)
            snew_ref[b, h] = s
            acc_ref[b:b + 1, hs] = jnp.sum(qc * s, axis=0, keepdims=True)
    o_ref[...] = _hg_out(acc_ref[...], g_ref[...], nw_ref[...])


def _sample_hgrn(hg_norm, hq, hk, hi, hgate, state):
    nb = hq.shape[0]
    g = SAMPLE_HG_GROUP
    row = pl.BlockSpec((g, D_HG), lambda i: (i, 0))
    sblk = pl.BlockSpec((g, HG_HEADS, HG_DK, HG_DV), lambda i: (i, 0, 0, 0))
    return pl.pallas_call(
        _sample_hgrn_kernel,
        grid=(nb // g,),
        in_specs=[_resident((1, HG_DV)), row, row, row, row, sblk],
        out_specs=[row, sblk],
        out_shape=[jax.ShapeDtypeStruct((nb, D_HG), F32),
                   jax.ShapeDtypeStruct(state.shape, F32)],
        scratch_shapes=[pltpu.VMEM((g, D_HG), F32)],
        compiler_params=_params("parallel"),
        name="sample_hgrn",
    )(hg_norm, hq, hk, hi, hgate, state)


def kernel(x_prompt, x_sample, cache_k, cache_v, cache_meta_k, cache_meta_v, state_hgrn, meta,
           w_in, sinks, lb_param, hg_norm, w_att_out, w_hg_out, w_o, ln_mix, ln_ffn, w_up,
           w_down, ln_f):
    bsz, seq, _ = x_prompt.shape
    nb = x_sample.shape[0]
    w_in_b = w_in[0].astype(BF16)
    wa = w_att_out[0].astype(BF16)
    wb = w_hg_out[0].astype(BF16)
    wo = w_o[0].astype(BF16)
    wup = w_up[0].astype(BF16)
    wdn = w_down[0].astype(BF16)
    ln_mix2 = ln_mix.reshape(1, D_MODEL)
    ln_ffn2 = ln_ffn.reshape(1, D_MODEL)
    ln_f2 = ln_f.reshape(1, D_MODEL)
    nw = hg_norm.reshape(1, HG_DV)
    sink_vec = sinks.reshape(Q_HEADS)

    xp = x_prompt.reshape(bsz * seq, D_MODEL)
    q_p, kv_p, hq_p, hk_p, hlogf_p, hi_p, hgate_p, sga_p, sgb_p = _project(
        xp, ln_mix2, lb_param, w_in_b, PROJ_ROWS, BF16)
    xs = jnp.concatenate([x_sample.reshape(nb, D_MODEL), meta], axis=0)
    q_s, kv_s, hq_s, hk_s, hlogf_s, hi_s, hgate_s, sga_s, sgb_s = _project(
        xs, ln_mix2, lb_param, w_in_b, nb + N_META, F32)
    kv_meta = kv_s[nb:]

    att_p = _attention(sink_vec, q_p.reshape(bsz, seq, D_ATT), kv_p.reshape(bsz, seq, 2 * D_KV), kv_meta)

    def per_seq(a):
        return a.reshape(bsz, seq, D_HG)

    hg_p, state_p = _hgrn(nw, hk_s[nb:], hlogf_s[nb:], hi_s[nb:], per_seq(hq_p), per_seq(hk_p),
                          per_seq(hlogf_p), per_seq(hi_p), per_seq(hgate_p))
    y_p = _merge_ffn(xp, att_p.reshape(bsz * seq, D_ATT), hg_p.reshape(bsz * seq, D_HG), sga_p, sgb_p,
                     wa, wb, wo, ln_ffn2, wup, wdn, ln_f2, MERGE_ROWS)

    grp = Q_HEADS // KV_HEADS
    qs4 = q_s[:nb].reshape(nb, KV_HEADS, grp, 1, HEAD_DIM)
    sel = jnp.eye(KV_HEADS, dtype=BF16).reshape(1, KV_HEADS, 1, KV_HEADS, 1)
    qm = (qs4 * sel).reshape(nb, Q_HEADS, D_KV)
    o_s, nk_s, nv_s = _sample_attention(
        sinks.reshape(Q_HEADS, 1), qm, kv_s[:nb],
        cache_k[0].reshape(nb, WINDOW, D_KV), cache_v[0].reshape(nb, WINDOW, D_KV),
        cache_meta_k[0].reshape(nb, N_META, D_KV), cache_meta_v[0].reshape(nb, N_META, D_KV))
    o5 = o_s.reshape(nb, KV_HEADS, grp, KV_HEADS, HEAD_DIM)
    att_s = jnp.stack([o5[:, h, :, h, :] for h in range(KV_HEADS)], axis=1).reshape(nb, D_ATT)
    hg_s, state_s = _sample_hgrn(nw, hq_s[:nb], hk_s[:nb], hi_s[:nb], hgate_s[:nb], state_hgrn[0])
    y_s = _merge_ffn(x_sample.reshape(nb, D_MODEL), att_s, hg_s, sga_s[:nb], sgb_s[:nb],
                     wa, wb, wo, ln_ffn2, wup, wdn, ln_f2, nb)

    kv5 = kv_p.reshape(bsz, seq, 2 * D_KV)[:, seq - WINDOW:].reshape(bsz, WINDOW, 2, KV_HEADS, HEAD_DIM)
    meta5 = jnp.broadcast_to(kv_meta.reshape(1, N_META, 2, KV_HEADS, HEAD_DIM),
                             (bsz, N_META, 2, KV_HEADS, HEAD_DIM))
    return (y_p.reshape(bsz, seq, D_MODEL),
            y_s.reshape(nb, 1, D_MODEL),
            kv5[None, :, :, 0],
            kv5[None, :, :, 1],
            meta5[None, :, :, 0],
            meta5[None, :, :, 1],
            state_p[None],
            nk_s.reshape(1, nb, WINDOW, KV_HEADS, HEAD_DIM),
            nv_s.reshape(1, nb, WINDOW, KV_HEADS, HEAD_DIM),
            state_s[None])
```

```python
import functools

import jax
import jax.numpy as jnp
from jax import lax
from jax.experimental import pallas as pl
from jax.experimental.pallas import tpu as pltpu

F32 = jnp.float32
BF16 = jnp.bfloat16

D_MODEL = 1024
N_META = 16
WINDOW = 128
HEAD_DIM = 64
Q_HEADS = 8
KV_HEADS = 2
D_ATT = Q_HEADS * HEAD_DIM
D_KV = KV_HEADS * HEAD_DIM
HG_HEADS = 4
HG_DK = 128
HG_DV = 128
D_HG = HG_HEADS * HG_DK
HG_CHUNK = 64
D_FF = 4 * D_MODEL
EPS = 1e-6
C_Q = 0
C_KV = C_Q + D_ATT
C_HQ = C_KV + 2 * D_KV
C_HF = C_HQ + D_HG
C_HI = C_HF + D_HG
C_HGATE = C_HI + D_HG
C_GA = C_HGATE + D_HG
C_GB = C_GA + D_MODEL
D_IN = C_GB + D_MODEL

VMEM_LIMIT_BYTES = 56 * 1024 * 1024
MIX_ROWS = 512
PROJ_GROUP = 256
MERGE_ROWS = 512
SAMPLE_ATT_GROUP = 16
SAMPLE_HG_GROUP = 16

_NT = (((1,), (1,)), ((), ()))
_TN = (((0,), (0,)), ((), ()))


def _dot(a, b):
    return jnp.dot(a, b, preferred_element_type=F32)


def _dot_nt(a, b):
    return lax.dot_general(a, b, _NT, preferred_element_type=F32)


def _dot_tn(a, b):
    return lax.dot_general(a, b, _TN, preferred_element_type=F32)


def _rmsnorm(x, g):
    return x * lax.rsqrt(jnp.mean(x * x, axis=-1, keepdims=True) + EPS) * g


def _resident(shape):
    return pl.BlockSpec(shape, lambda *_: (0,) * len(shape), pipeline_mode=pl.Buffered(1))


def _params(*sem):
    return pltpu.CompilerParams(dimension_semantics=sem, vmem_limit_bytes=VMEM_LIMIT_BYTES)


def _lower_bound(lbp):
    e = jnp.exp(lbp - jnp.max(lbp, axis=0, keepdims=True))
    return e[0:1] / jnp.sum(e, axis=0, keepdims=True)


class _Projection:
    def __init__(self, xn, lbp, w_ref):
        self.xn = xn
        self.w_ref = w_ref
        self.lb = _lower_bound(lbp)

    def cols(self, base, part):
        return _dot(self.xn[...], self.w_ref[:, base + part.start:base + part.stop])

    def q_att(self, part=slice(0, D_ATT)):
        return self.cols(C_Q, part) * (HEAD_DIM ** -0.5)

    def kv(self):
        return self.cols(C_KV, slice(0, 2 * D_KV))

    def q_hg(self, part=slice(0, D_HG)):
        return self.cols(C_HQ, part) * (HG_DK ** -0.5)

    def forget(self, part=slice(0, D_HG)):
        lb = self.lb[:, part]
        f = lb + (1.0 - lb) * jax.nn.sigmoid(self.cols(C_HF, part))
        return 1.0 - f, jnp.log(f)

    def i_hg(self, part=slice(0, D_HG)):
        return self.cols(C_HI, part)

    def swish_gate(self, part=slice(0, D_HG)):
        g = self.cols(C_HGATE, part)
        return g * jax.nn.sigmoid(g)

    def branch_gate(self, base, part=slice(0, D_MODEL)):
        return jax.nn.sigmoid(self.cols(base, part))


def _proj_kernel(x_ref, g_ref, lbp_ref, w_ref, q_ref, kv_ref, hq_ref, hk_ref, hlogf_ref, hi_ref,
                 hgate_ref, sga_ref, sgb_ref):
    p = _Projection(_rmsnorm(x_ref[...], g_ref[...]).astype(BF16), lbp_ref[...], w_ref)
    q_ref[...] = p.q_att().astype(q_ref.dtype)
    kv_ref[...] = p.kv()
    hq_ref[...] = p.q_hg().astype(hq_ref.dtype)
    k, logf = p.forget()
    hk_ref[...] = k.astype(hk_ref.dtype)
    hlogf_ref[...] = logf
    hi_ref[...] = p.i_hg().astype(hi_ref.dtype)
    hgate_ref[...] = p.swish_gate().astype(hgate_ref.dtype)
    sga_ref[...] = p.branch_gate(C_GA).astype(sga_ref.dtype)
    sgb_ref[...] = p.branch_gate(C_GB).astype(sgb_ref.dtype)


def _project(x, g, lb_param, w):
    n = x.shape[0]
    widths = (D_ATT, 2 * D_KV, D_HG, D_HG, D_HG, D_HG, D_HG, D_MODEL, D_MODEL)
    dtypes = (BF16, F32, F32, F32, F32, F32, F32, BF16, BF16)
    return pl.pallas_call(
        _proj_kernel,
        grid=(1,),
        in_specs=[_resident((n, D_MODEL)), _resident((1, D_MODEL)), _resident(lb_param.shape),
                  _resident((D_MODEL, D_IN))],
        out_specs=[pl.BlockSpec((n, c), lambda i: (0, 0)) for c in widths],
        out_shape=[jax.ShapeDtypeStruct((n, c), d) for c, d in zip(widths, dtypes)],
        compiler_params=_params("arbitrary"),
        name="proj",
    )(x, g, lb_param, w)


class _Attention:
    def __init__(self, sink_ref, q_ref, kv, first, o_ref):
        self.sink_ref, self.q_ref, self.first, self.o_ref = sink_ref, q_ref, first, o_ref
        self.nsub = q_ref.shape[0] // WINDOW
        self.nk = 2 * WINDOW + N_META
        self.meta0 = WINDOW + q_ref.shape[0]
        lane = lax.broadcasted_iota(jnp.int32, (kv.shape[0], D_KV), 1)
        low = lane < HEAD_DIM
        k = kv[:, :D_KV]
        ksw = pltpu.roll(k, HEAD_DIM, axis=1)
        self.kboth = (jnp.where(low, k, ksw).astype(BF16), jnp.where(low, ksw, k).astype(BF16))
        qlane = lax.broadcasted_iota(jnp.int32, (1, 128), 1)
        self.keep = (jnp.where(qlane < HEAD_DIM, 1.0, 0.0).astype(BF16),
                     jnp.where(qlane < HEAD_DIM, 0.0, 1.0).astype(BF16))
        r = lax.broadcasted_iota(jnp.int32, (D_KV, D_KV), 0)
        c = lax.broadcasted_iota(jnp.int32, (D_KV, D_KV), 1)
        eye = jnp.where(r == c, 1.0, 0.0).astype(BF16)
        self.vt = _dot_nt(eye, kv[:, D_KV:].astype(BF16)).astype(BF16)
        self.ones = jnp.ones((16, self.nk), BF16)
        self.key = lax.broadcasted_iota(jnp.int32, (self.nk, 2 * WINDOW), 0)
        col = lax.broadcasted_iota(jnp.int32, (self.nk, 2 * WINDOW), 1)
        self.qry = jnp.bitwise_and(col, WINDOW - 1)
        self.first_head = lax.broadcasted_iota(jnp.int32, (1, 2 * WINDOW), 1) < WINDOW
        self.scores = {}

    def issue_scores(self, i):
        r0 = i * WINDOW
        out = []
        for p in range(Q_HEADS // 2):
            kb = self.kboth[(2 * p) // (Q_HEADS // KV_HEADS)]
            kmat = jnp.concatenate([kb[r0:r0 + 2 * WINDOW], kb[self.meta0:]], axis=0)
            qp = self.q_ref[r0:r0 + WINDOW, p * 128:(p + 1) * 128]
            q2 = jnp.concatenate([qp * self.keep[0], qp * self.keep[1]], axis=0)
            out.append(_dot_nt(kmat, q2))
        self.scores[i] = out

    def finish(self, i):
        r0 = i * WINDOW
        lo = jnp.where(self.first, WINDOW - 1, self.qry) if i == 0 else self.qry
        mask = jnp.logical_and(self.key > lo, self.key <= self.qry + WINDOW)
        mask = jnp.logical_or(mask, self.key >= 2 * WINDOW)
        vaug = []
        for h in range(KV_HEADS):
            vth = self.vt[h * HEAD_DIM:(h + 1) * HEAD_DIM]
            vaug.append(jnp.concatenate(
                [jnp.concatenate([vth[:, r0:r0 + 2 * WINDOW], vth[:, self.meta0:]], axis=1),
                 self.ones], axis=0))
        outs = []
        for p, raw in enumerate(self.scores.pop(i)):
            h = (2 * p) // (Q_HEADS // KV_HEADS)
            s = jnp.where(mask, raw, -jnp.inf)
            sk = jnp.where(self.first_head, self.sink_ref[2 * p], self.sink_ref[2 * p + 1])
            m = jnp.maximum(jnp.max(s, axis=0, keepdims=True), sk)
            e = jnp.exp(s - m).astype(BF16)
            oa = _dot(vaug[h], e)
            o = oa[:HEAD_DIM] / (oa[HEAD_DIM:HEAD_DIM + 1] + jnp.exp(sk - m))
            outs += [o[:, :WINDOW], o[:, WINDOW:]]
        self.o_ref[r0:r0 + WINDOW, :] = jnp.concatenate(outs, axis=0).T.astype(self.o_ref.dtype)


def _cumsum_rows(x):
    t = x.shape[0]
    r = lax.broadcasted_iota(jnp.int32, (t, t), 0)
    c = lax.broadcasted_iota(jnp.int32, (t, t), 1)
    tri = jnp.where(r >= c, 1.0, 0.0).astype(BF16)
    hi = x.astype(BF16)
    r1 = x - hi.astype(F32)
    mid = r1.astype(BF16)
    lo = (r1 - mid.astype(F32)).astype(BF16)
    return _dot(tri, hi) + _dot(tri, mid) + _dot(tri, lo)


def _hg_out(o, gate, nw):
    parts = []
    for h in range(HG_HEADS):
        oh = o[:, h * HG_DV:(h + 1) * HG_DV]
        parts.append(oh * lax.rsqrt(jnp.mean(oh * oh, axis=-1, keepdims=True) + EPS) * nw)
    return jnp.concatenate(parts, axis=1) * gate


_HG_SLICES = [slice(h * HG_DK, (h + 1) * HG_DK) for h in range(HG_HEADS)]


def _meta_state(mk, mlogf, mi):
    b = _cumsum_rows(mlogf)
    kd = (mk * jnp.exp(b[-1:] - b)).astype(BF16)
    iv = mi.astype(BF16)
    return [_dot_tn(iv[:, hs], kd[:, hs]) for hs in _HG_SLICES]


class _Hgrn:
    def __init__(self, q_ref, k_ref, logf_ref, i_ref, g_ref, nw, state, o_ref):
        self.q_ref, self.k_ref, self.logf_ref, self.i_ref, self.g_ref = q_ref, k_ref, logf_ref, i_ref, g_ref
        self.nw, self.state, self.o_ref = nw, state, o_ref
        t = HG_CHUNK
        self.nchunk = q_ref.shape[0] // t
        r = lax.broadcasted_iota(jnp.int32, (t, t), 0)
        c = lax.broadcasted_iota(jnp.int32, (t, t), 1)
        self.causal = r >= c
        self.cums, self.terms, self.entering = {}, {}, {}

    def rows(self, ci):
        return slice(ci * HG_CHUNK, (ci + 1) * HG_CHUNK)

    def issue_cumsum(self, ci):
        self.cums[ci] = _cumsum_rows(self.logf_ref[self.rows(ci), :])

    def issue_local(self, ci):
        rows = self.rows(ci)
        b = self.cums.pop(ci)
        bl = b[-1:]
        k = self.k_ref[rows, :].astype(F32)
        qe = (self.q_ref[rows, :].astype(F32) * jnp.exp(b)).astype(BF16)
        ke = (k * jnp.exp(-b)).astype(BF16)
        kd = (k * jnp.exp(bl - b)).astype(BF16)
        iv = self.i_ref[rows, :].astype(BF16)
        a = [jnp.where(self.causal, _dot_nt(qe[:, hs], ke[:, hs]), 0.0).astype(BF16) for hs in _HG_SLICES]
        u = [_dot_tn(iv[:, hs], kd[:, hs]) for hs in _HG_SLICES]
        self.terms[ci] = (qe, iv, a)
        self.entering[ci] = [s.astype(BF16) for s in self.state]
        dec = jnp.exp(bl)
        self.state = [s * dec[:, hs] + uh for s, hs, uh in zip(self.state, _HG_SLICES, u)]

    def finish(self, ci):
        rows = self.rows(ci)
        qe, iv, a = self.terms.pop(ci)
        ent = self.entering.pop(ci)
        outs = [_dot(a[h], iv[:, hs]) + _dot_nt(qe[:, hs], ent[h]) for h, hs in enumerate(_HG_SLICES)]
        o = _hg_out(jnp.concatenate(outs, axis=1), self.g_ref[rows, :].astype(F32), self.nw)
        self.o_ref[rows, :] = o.astype(self.o_ref.dtype)


def _mixer_kernel(sink_ref, x0_ref, xnext_ref, g_ref, lbp_ref, w_ref, kvm_ref, nw_ref, mk_ref,
                  mlogf_ref, mi_ref,
                  att_ref, hg_ref, sga_ref, sgb_ref, sfin_ref, lastkv_ref,
                  xn_s, zq, zkv, zhq, zhk, zlogf, zhi, zhg, st_ref, mst_ref, *, tiles_per_seq):
    s = pl.program_id(0)
    t = xnext_ref.shape[0]

    @pl.when(s == 0)
    def _():
        xn_s[...] = _rmsnorm(x0_ref[...], g_ref[...]).astype(xn_s.dtype)
        for ref in (zq, zkv, zhq, zhk, zlogf, zhi, zhg, st_ref):
            ref[...] = jnp.zeros(ref.shape, ref.dtype)
        for h, m in enumerate(_meta_state(mk_ref[...], mlogf_ref[...], mi_ref[...])):
            mst_ref[h] = m

    first = lax.rem(jnp.maximum(s - 1, 0), tiles_per_seq) == 0

    att = _Attention(sink_ref, zq, jnp.concatenate([zkv[...], kvm_ref[...]], axis=0), first, att_ref)
    last = zkv[t:t + WINDOW, :]
    lastkv_ref[...] = last
    zkv[0:WINDOW, :] = last
    entering = [jnp.where(first, mst_ref[h], st_ref[h]) for h in range(HG_HEADS)]
    hg = _Hgrn(zhq, zhk, zlogf, zhi, zhg, nw_ref[...], entering, hg_ref)
    proj = _Projection(xn_s, lbp_ref[...], w_ref)

    def gate_a(p):
        sga_ref[:, p] = proj.branch_gate(C_GA, p).astype(sga_ref.dtype)

    def gate_b(p):
        sgb_ref[:, p] = proj.branch_gate(C_GB, p).astype(sgb_ref.dtype)

    def new_kv(_):
        zkv[WINDOW:WINDOW + t, :] = proj.kv()

    def new_forget(p):
        k_new, logf_new = proj.forget(p)
        zhk[:, p] = k_new.astype(zhk.dtype)
        zlogf[:, p] = logf_new

    def new_q(p):
        zq[:, p] = proj.q_att(p).astype(zq.dtype)

    def new_hq(p):
        zhq[:, p] = proj.q_hg(p).astype(zhq.dtype)

    def new_hi(p):
        zhi[:, p] = proj.i_hg(p).astype(zhi.dtype)

    def new_gate(p):
        zhg[:, p] = proj.swish_gate(p).astype(zhg.dtype)

    def parts(n):
        return [slice(c, c + PROJ_GROUP) for c in range(0, n, PROJ_GROUP)]

    groups = ([(gate_a, p) for p in parts(D_MODEL)] + [(gate_b, p) for p in parts(D_MODEL)]
              + [(new_kv, None)] + [(new_forget, p) for p in parts(D_HG)]
              + [(new_q, p) for p in parts(D_ATT)] + [(new_hq, p) for p in parts(D_HG)]
              + [(new_hi, p) for p in parts(D_HG)] + [(new_gate, p) for p in parts(D_HG)])

    def save_state():
        for h in range(HG_HEADS):
            st_ref[h] = hg.state[h]
            sfin_ref[h] = hg.state[h].T

    qk, cs, loc, fin, out = att.issue_scores, hg.issue_cumsum, hg.issue_local, att.finish, hg.finish
    mixer_work = {
        0: [(qk, 0), (cs, 0), (cs, 1), (cs, 2), (cs, 3)],
        1: [(cs, 4), (cs, 5), (loc, 0)],
        2: [(cs, 6), (cs, 7), (loc, 1)],
        3: [(qk, 1), (loc, 2)],
        4: [(fin, 0), (loc, 3)],
        5: [(out, 0), (loc, 4)],
        6: [(qk, 2), (out, 1), (loc, 5)],
        7: [(fin, 1), (out, 2), (loc, 6)],
        8: [(out, 3), (loc, 7), (save_state,)],
        9: [(qk, 3), (out, 4)],
        10: [(fin, 2), (out, 5)],
        11: [(out, 6)],
        12: [(out, 7)],
        13: [(fin, 3)],
    }
    for slot, (fn, p) in enumerate(groups):
        fn(p)
        for item in mixer_work.get(slot, []):
            item[0](*item[1:])
    xn_s[...] = _rmsnorm(xnext_ref[...], g_ref[...]).astype(xn_s.dtype)


def _mixers(sinks, x, g, lb_param, w, kv_meta, hg_norm, meta_k, meta_logf, meta_i, bsz):
    n = x.shape[0]
    t = MIX_ROWS
    nt = n // t
    per_seq = nt // bsz

    def this_tile(c):
        return pl.BlockSpec((t, c), lambda s: (jnp.minimum(s, nt - 1), 0))

    def prev_tile(c):
        return pl.BlockSpec((t, c), lambda s: (jnp.maximum(s - 1, 0), 0))

    def prev_seq(shape):
        return pl.BlockSpec((None,) + shape,
                            lambda s: (jnp.maximum(s - 1, 0) // per_seq,) + (0,) * len(shape))

    small = _resident((N_META, D_HG))
    first_tile = pl.BlockSpec((t, D_MODEL), lambda s: (0, 0), pipeline_mode=pl.Buffered(1))
    next_tile = pl.BlockSpec((t, D_MODEL), lambda s: (jnp.minimum(s + 1, nt - 1), 0))
    return pl.pallas_call(
        functools.partial(_mixer_kernel, tiles_per_seq=per_seq),
        grid=(nt + 1,),
        in_specs=[pl.BlockSpec(memory_space=pltpu.SMEM), first_tile, next_tile, _resident((1, D_MODEL)),
                  _resident(lb_param.shape), _resident((D_MODEL, D_IN)), _resident((N_META, 2 * D_KV)),
                  _resident((1, HG_DV)), small, small, small],
        out_specs=[prev_tile(D_ATT), prev_tile(D_HG), this_tile(D_MODEL), this_tile(D_MODEL),
                   prev_seq((HG_HEADS, HG_DK, HG_DV)), prev_seq((WINDOW, 2 * D_KV))],
        out_shape=[jax.ShapeDtypeStruct((n, D_ATT), BF16), jax.ShapeDtypeStruct((n, D_HG), BF16),
                   jax.ShapeDtypeStruct((n, D_MODEL), BF16), jax.ShapeDtypeStruct((n, D_MODEL), BF16),
                   jax.ShapeDtypeStruct((bsz, HG_HEADS, HG_DK, HG_DV), F32),
                   jax.ShapeDtypeStruct((bsz, WINDOW, 2 * D_KV), F32)],
        scratch_shapes=[pltpu.VMEM((t, D_MODEL), BF16),
                        pltpu.VMEM((t, D_ATT), BF16), pltpu.VMEM((WINDOW + t, 2 * D_KV), F32),
                        pltpu.VMEM((t, D_HG), BF16), pltpu.VMEM((t, D_HG), BF16),
                        pltpu.VMEM((t, D_HG), F32), pltpu.VMEM((t, D_HG), BF16),
                        pltpu.VMEM((t, D_HG), BF16),
                        pltpu.VMEM((HG_HEADS, HG_DV, HG_DK), F32),
                        pltpu.VMEM((HG_HEADS, HG_DV, HG_DK), F32)],
        compiler_params=_params("arbitrary"),
        name="mixers",
    )(sinks, x, x, g, lb_param, w, kv_meta, hg_norm, meta_k, meta_logf, meta_i)


def _merge_ffn_kernel(x_ref, att_ref, hg_ref, sga_ref, sgb_ref, wa_ref, wb_ref, wo_ref,
                      lnffn_ref, wup_ref, wdn_ref, lnf_ref, y_ref):
    ya = _dot(att_ref[...].astype(BF16), wa_ref[...])
    yb = _dot(hg_ref[...].astype(BF16), wb_ref[...])
    mix = sga_ref[...].astype(F32) * ya + sgb_ref[...].astype(F32) * yb
    h1 = x_ref[...] + _dot(mix.astype(BF16), wo_ref[...])
    xn = _rmsnorm(h1, lnffn_ref[...]).astype(BF16)
    acc = jnp.zeros_like(h1)
    step = 1024
    for c in range(D_FF // step):
        u = jnp.maximum(_dot(xn, wup_ref[:, c * step:(c + 1) * step]), 0.0)
        acc = acc + _dot((u * u).astype(BF16), wdn_ref[c * step:(c + 1) * step, :])
    y_ref[...] = _rmsnorm(h1 + acc, lnf_ref[...])


def _merge_ffn(x, att, hg, sga, sgb, wa, wb, wo, ln_ffn, w_up, w_down, ln_f, rows):
    n = x.shape[0]

    def blk(c):
        return pl.BlockSpec((rows, c), lambda i: (i, 0))

    return pl.pallas_call(
        _merge_ffn_kernel,
        grid=(n // rows,),
        in_specs=[blk(D_MODEL), blk(D_ATT), blk(D_HG), blk(D_MODEL), blk(D_MODEL),
                  _resident(wa.shape), _resident(wb.shape), _resident(wo.shape),
                  _resident((1, D_MODEL)), _resident(w_up.shape), _resident(w_down.shape),
                  _resident((1, D_MODEL))],
        out_specs=blk(D_MODEL),
        out_shape=jax.ShapeDtypeStruct((n, D_MODEL), F32),
        compiler_params=_params("parallel"),
        name="merge_ffn",
    )(x, att, hg, sga, sgb, wa, wb, wo, ln_ffn, w_up, w_down, ln_f)


def _sample_attn_kernel(sink_ref, qm_ref, kvn_ref, ck_ref, cv_ref, mk_ref, mv_ref,
                        o_ref, nk_ref, nv_ref):
    r = WINDOW
    nb = qm_ref.shape[0]
    sk = sink_ref[...]
    for b in range(nb):
        nk_ref[b, 0:r - 1, :] = ck_ref[b, 1:r, :]
        nk_ref[b, r - 1:r, :] = kvn_ref[b:b + 1, 0:D_KV]
        nv_ref[b, 0:r - 1, :] = cv_ref[b, 1:r, :]
        nv_ref[b, r - 1:r, :] = kvn_ref[b:b + 1, D_KV:2 * D_KV]
    scores = [(_dot_nt(qm_ref[b], nk_ref[b].astype(BF16)),
               _dot_nt(qm_ref[b], mk_ref[b].astype(BF16)))
              for b in range(nb)]
    for b, (s_w, s_m) in enumerate(scores):
        m = jnp.maximum(jnp.maximum(jnp.max(s_w, axis=1, keepdims=True),
                                    jnp.max(s_m, axis=1, keepdims=True)), sk)
        e_w = jnp.exp(s_w - m)
        e_m = jnp.exp(s_m - m)
        l = (jnp.sum(e_w, axis=1, keepdims=True) + jnp.sum(e_m, axis=1, keepdims=True)
             + jnp.exp(sk - m))
        o = _dot(e_w.astype(BF16), nv_ref[b].astype(BF16)) + _dot(e_m.astype(BF16), mv_ref[b].astype(BF16))
        o_ref[b] = o / l


def _sample_attention(sinks_col, qm, kv_new, ck, cv, mk, mv):
    nb = qm.shape[0]
    g = SAMPLE_ATT_GROUP

    def blk3(a, c):
        return pl.BlockSpec((g, a, c), lambda i: (i, 0, 0))

    return pl.pallas_call(
        _sample_attn_kernel,
        grid=(nb // g,),
        in_specs=[_resident((Q_HEADS, 1)), blk3(Q_HEADS, D_KV),
                  pl.BlockSpec((g, 2 * D_KV), lambda i: (i, 0)),
                  blk3(WINDOW, D_KV), blk3(WINDOW, D_KV), blk3(N_META, D_KV), blk3(N_META, D_KV)],
        out_specs=[blk3(Q_HEADS, D_KV), blk3(WINDOW, D_KV), blk3(WINDOW, D_KV)],
        out_shape=[jax.ShapeDtypeStruct((nb, Q_HEADS, D_KV), F32),
                   jax.ShapeDtypeStruct((nb, WINDOW, D_KV), F32),
                   jax.ShapeDtypeStruct((nb, WINDOW, D_KV), F32)],
        compiler_params=_params("parallel"),
        name="sample_attn",
    )(sinks_col, qm, kv_new, ck, cv, mk, mv)


def _sample_hgrn_kernel(nw_ref, q_ref, k_ref, i_ref, g_ref, s_ref, o_ref, snew_ref, acc_ref):
    k = k_ref[...]
    q = q_ref[...]
    iv = i_ref[...]
    r = lax.broadcasted_iota(jnp.int32, (HG_DK, HG_DK), 0)
    c = lax.broadcasted_iota(jnp.int32, (HG_DK, HG_DK), 1)
    eye = r == c

    def column(row):
        return jnp.sum(jnp.where(eye, row, 0.0), axis=1, keepdims=True)

    for b in range(s_ref.shape[0]):
        for h, hs in enumerate(_HG_SLICES):
            kc = column(k[b:b + 1, hs])
            qc = column(q[b:b + 1, hs])
            s_old = s_ref[b, h]
            s = s_old + kc * (iv[b:b + 1, hs] - s_old)
            snew_ref[b, h] = s
            acc_ref[b:b + 1, hs] = jnp.sum(qc * s, axis=0, keepdims=True)
    o_ref[...] = _hg_out(acc_ref[...], g_ref[...], nw_ref[...])


def _sample_hgrn(hg_norm, hq, hk, hi, hgate, state):
    nb = hq.shape[0]
    g = SAMPLE_HG_GROUP
    row = pl.BlockSpec((g, D_HG), lambda i: (i, 0))
    sblk = pl.BlockSpec((g, HG_HEADS, HG_DK, HG_DV), lambda i: (i, 0, 0, 0))
    return pl.pallas_call(
        _sample_hgrn_kernel,
        grid=(nb // g,),
        in_specs=[_resident((1, HG_DV)), row, row, row, row, sblk],
        out_specs=[row, sblk],
        out_shape=[jax.ShapeDtypeStruct((nb, D_HG), F32),
                   jax.ShapeDtypeStruct(state.shape, F32)],
        scratch_shapes=[pltpu.VMEM((g, D_HG), F32)],
        compiler_params=_params("parallel"),
        name="sample_hgrn",
    )(hg_norm, hq, hk, hi, hgate, state)


def kernel(x_prompt, x_sample, cache_k, cache_v, cache_meta_k, cache_meta_v, state_hgrn, meta,
           w_in, sinks, lb_param, hg_norm, w_att_out, w_hg_out, w_o, ln_mix, ln_ffn, w_up,
           w_down, ln_f):
    bsz, seq, _ = x_prompt.shape
    nb = x_sample.shape[0]
    w_in_b = w_in[0].astype(BF16)
    wa = w_att_out[0].astype(BF16)
    wb = w_hg_out[0].astype(BF16)
    wo = w_o[0].astype(BF16)
    wup = w_up[0].astype(BF16)
    wdn = w_down[0].astype(BF16)
    ln_mix2 = ln_mix.reshape(1, D_MODEL)
    ln_ffn2 = ln_ffn.reshape(1, D_MODEL)
    ln_f2 = ln_f.reshape(1, D_MODEL)
    nw = hg_norm.reshape(1, HG_DV)

    xs = jnp.concatenate([x_sample.reshape(nb, D_MODEL), meta], axis=0)
    q_s, kv_s, hq_s, hk_s, hlogf_s, hi_s, hgate_s, sga_s, sgb_s = _project(xs, ln_mix2, lb_param, w_in_b)
    kv_meta = kv_s[nb:]

    xp = x_prompt.reshape(bsz * seq, D_MODEL)
    att_p, hg_p, sga_p, sgb_p, state_p, lastkv_p = _mixers(
        sinks.reshape(Q_HEADS), xp, ln_mix2, lb_param, w_in_b, kv_meta, nw,
        hk_s[nb:], hlogf_s[nb:], hi_s[nb:], bsz)
    y_p = _merge_ffn(xp, att_p, hg_p, sga_p, sgb_p, wa, wb, wo, ln_ffn2, wup, wdn, ln_f2, MERGE_ROWS)

    grp = Q_HEADS // KV_HEADS
    qs4 = q_s[:nb].reshape(nb, KV_HEADS, grp, 1, HEAD_DIM)
    sel = jnp.eye(KV_HEADS, dtype=BF16).reshape(1, KV_HEADS, 1, KV_HEADS, 1)
    qm = (qs4 * sel).reshape(nb, Q_HEADS, D_KV)
    o_s, nk_s, nv_s = _sample_attention(
        sinks.reshape(Q_HEADS, 1), qm, kv_s[:nb],
        cache_k[0].reshape(nb, WINDOW, D_KV), cache_v[0].reshape(nb, WINDOW, D_KV),
        cache_meta_k[0].reshape(nb, N_META, D_KV), cache_meta_v[0].reshape(nb, N_META, D_KV))
    o5 = o_s.reshape(nb, KV_HEADS, grp, KV_HEADS, HEAD_DIM)
    att_s = jnp.stack([o5[:, h, :, h, :] for h in range(KV_HEADS)], axis=1).reshape(nb, D_ATT)
    hg_s, state_s = _sample_hgrn(nw, hq_s[:nb], hk_s[:nb], hi_s[:nb], hgate_s[:nb], state_hgrn[0])
    y_s = _merge_ffn(x_sample.reshape(nb, D_MODEL), att_s, hg_s, sga_s[:nb], sgb_s[:nb],
                     wa, wb, wo, ln_ffn2, wup, wdn, ln_f2, nb)

    kv5 = lastkv_p.reshape(bsz, WINDOW, 2, KV_HEADS, HEAD_DIM)
    meta5 = jnp.broadcast_to(kv_meta.reshape(1, N_META, 2, KV_HEADS, HEAD_DIM),
                             (bsz, N_META, 2, KV_HEADS, HEAD_DIM))
    return (y_p.reshape(bsz, seq, D_MODEL),
            y_s.reshape(nb, 1, D_MODEL),
            kv5[None, :, :, 0],
            kv5[None, :, :, 1],
            meta5[None, :, :, 0],
            meta5[None, :, :, 1],
            state_p[None],
            nk_s.reshape(1, nb, WINDOW, KV_HEADS, HEAD_DIM),
            nv_s.reshape(1, nb, WINDOW, KV_HEADS, HEAD_DIM),
            state_s[None])
```

```python
import functools

import jax
import jax.numpy as jnp
from jax import lax
from jax.experimental import pallas as pl
from jax.experimental.pallas import tpu as pltpu

F32 = jnp.float32
BF16 = jnp.bfloat16

D_MODEL = 1024
N_META = 16
WINDOW = 128
HEAD_DIM = 64
Q_HEADS = 8
KV_HEADS = 2
D_ATT = Q_HEADS * HEAD_DIM
D_KV = KV_HEADS * HEAD_DIM
HG_HEADS = 4
HG_DK = 128
HG_DV = 128
D_HG = HG_HEADS * HG_DK
HG_CHUNK = 64
D_FF = 4 * D_MODEL
EPS = 1e-6
C_Q = 0
C_KV = C_Q + D_ATT
C_HQ = C_KV + 2 * D_KV
C_HF = C_HQ + D_HG
C_HI = C_HF + D_HG
C_HGATE = C_HI + D_HG
C_GA = C_HGATE + D_HG
C_GB = C_GA + D_MODEL
D_IN = C_GB + D_MODEL

VMEM_LIMIT_BYTES = 56 * 1024 * 1024
MIX_ROWS = 512
PROJ_GROUP = 256
MERGE_ROWS = 512
SAMPLE_ATT_GROUP = 16
SAMPLE_HG_GROUP = 16

_NT = (((1,), (1,)), ((), ()))
_TN = (((0,), (0,)), ((), ()))


def _dot(a, b):
    return jnp.dot(a, b, preferred_element_type=F32)


def _dot_nt(a, b):
    return lax.dot_general(a, b, _NT, preferred_element_type=F32)


def _dot_tn(a, b):
    return lax.dot_general(a, b, _TN, preferred_element_type=F32)


def _rmsnorm(x, g):
    return x * lax.rsqrt(jnp.mean(x * x, axis=-1, keepdims=True) + EPS) * g


def _resident(shape):
    return pl.BlockSpec(shape, lambda *_: (0,) * len(shape), pipeline_mode=pl.Buffered(1))


def _params(*sem):
    return pltpu.CompilerParams(dimension_semantics=sem, vmem_limit_bytes=VMEM_LIMIT_BYTES)


def _lower_bound(lbp):
    e = jnp.exp(lbp - jnp.max(lbp, axis=0, keepdims=True))
    return e[0:1] / jnp.sum(e, axis=0, keepdims=True)


class _Projection:
    def __init__(self, xn, lbp, w_ref):
        self.xn = xn
        self.w_ref = w_ref
        self.lb = _lower_bound(lbp)

    def cols(self, base, part):
        return _dot(self.xn[...], self.w_ref[:, base + part.start:base + part.stop])

    def q_att(self, part=slice(0, D_ATT)):
        return self.cols(C_Q, part) * (HEAD_DIM ** -0.5)

    def kv(self):
        return self.cols(C_KV, slice(0, 2 * D_KV))

    def q_hg(self, part=slice(0, D_HG)):
        return self.cols(C_HQ, part) * (HG_DK ** -0.5)

    def forget(self, part=slice(0, D_HG)):
        lb = self.lb[:, part]
        f = lb + (1.0 - lb) * jax.nn.sigmoid(self.cols(C_HF, part))
        return 1.0 - f, jnp.log(f)

    def i_hg(self, part=slice(0, D_HG)):
        return self.cols(C_HI, part)

    def swish_gate(self, part=slice(0, D_HG)):
        g = self.cols(C_HGATE, part)
        return g * jax.nn.sigmoid(g)

    def branch_gate(self, base, part=slice(0, D_MODEL)):
        return jax.nn.sigmoid(self.cols(base, part))


def _proj_kernel(xs_ref, xm_ref, g_ref, lbp_ref, w_ref, q_ref, kv_ref, hq_ref, hk_ref, hlogf_ref,
                 hi_ref, hgate_ref, sga_ref, sgb_ref):
    x = jnp.concatenate([xs_ref[...], xm_ref[...]], axis=0)
    p = _Projection(_rmsnorm(x, g_ref[...]).astype(BF16), lbp_ref[...], w_ref)
    q_ref[...] = p.q_att().astype(q_ref.dtype)
    kv_ref[...] = p.kv()
    hq_ref[...] = p.q_hg().astype(hq_ref.dtype)
    k, logf = p.forget()
    hk_ref[...] = k.astype(hk_ref.dtype)
    hlogf_ref[...] = logf
    hi_ref[...] = p.i_hg().astype(hi_ref.dtype)
    hgate_ref[...] = p.swish_gate().astype(hgate_ref.dtype)
    sga_ref[...] = p.branch_gate(C_GA).astype(sga_ref.dtype)
    sgb_ref[...] = p.branch_gate(C_GB).astype(sgb_ref.dtype)


def _project(x_sample, x_meta, g, lb_param, w):
    n = x_sample.shape[0] + x_meta.shape[0]
    widths = (D_ATT, 2 * D_KV, D_HG, D_HG, D_HG, D_HG, D_HG, D_MODEL, D_MODEL)
    dtypes = (BF16, F32, F32, F32, F32, F32, F32, BF16, BF16)
    return pl.pallas_call(
        _proj_kernel,
        grid=(1,),
        in_specs=[_resident(x_sample.shape), _resident(x_meta.shape), _resident((1, D_MODEL)),
                  _resident(lb_param.shape), _resident((D_MODEL, D_IN))],
        out_specs=[pl.BlockSpec((n, c), lambda i: (0, 0)) for c in widths],
        out_shape=[jax.ShapeDtypeStruct((n, c), d) for c, d in zip(widths, dtypes)],
        compiler_params=_params("arbitrary"),
        name="proj",
    )(x_sample, x_meta, g, lb_param, w)


class _Attention:
    def __init__(self, sink_ref, q_ref, kv, first, o_ref):
        self.sink_ref, self.q_ref, self.first, self.o_ref = sink_ref, q_ref, first, o_ref
        self.nsub = q_ref.shape[0] // WINDOW
        self.nk = 2 * WINDOW + N_META
        self.meta0 = WINDOW + q_ref.shape[0]
        lane = lax.broadcasted_iota(jnp.int32, (kv.shape[0], D_KV), 1)
        low = lane < HEAD_DIM
        k = kv[:, :D_KV]
        ksw = pltpu.roll(k, HEAD_DIM, axis=1)
        self.kboth = (jnp.where(low, k, ksw).astype(BF16), jnp.where(low, ksw, k).astype(BF16))
        qlane = lax.broadcasted_iota(jnp.int32, (1, 128), 1)
        self.keep = (jnp.where(qlane < HEAD_DIM, 1.0, 0.0).astype(BF16),
                     jnp.where(qlane < HEAD_DIM, 0.0, 1.0).astype(BF16))
        r = lax.broadcasted_iota(jnp.int32, (D_KV, D_KV), 0)
        c = lax.broadcasted_iota(jnp.int32, (D_KV, D_KV), 1)
        eye = jnp.where(r == c, 1.0, 0.0).astype(BF16)
        self.vt = _dot_nt(eye, kv[:, D_KV:].astype(BF16)).astype(BF16)
        self.ones = jnp.ones((16, self.nk), BF16)
        self.key = lax.broadcasted_iota(jnp.int32, (self.nk, 2 * WINDOW), 0)
        col = lax.broadcasted_iota(jnp.int32, (self.nk, 2 * WINDOW), 1)
        self.qry = jnp.bitwise_and(col, WINDOW - 1)
        self.first_head = lax.broadcasted_iota(jnp.int32, (1, 2 * WINDOW), 1) < WINDOW
        self.scores = {}

    def issue_scores(self, i):
        r0 = i * WINDOW
        out = []
        for p in range(Q_HEADS // 2):
            kb = self.kboth[(2 * p) // (Q_HEADS // KV_HEADS)]
            kmat = jnp.concatenate([kb[r0:r0 + 2 * WINDOW], kb[self.meta0:]], axis=0)
            qp = self.q_ref[r0:r0 + WINDOW, p * 128:(p + 1) * 128]
            q2 = jnp.concatenate([qp * self.keep[0], qp * self.keep[1]], axis=0)
            out.append(_dot_nt(kmat, q2))
        self.scores[i] = out

    def finish(self, i):
        r0 = i * WINDOW
        lo = jnp.where(self.first, WINDOW - 1, self.qry) if i == 0 else self.qry
        mask = jnp.logical_and(self.key > lo, self.key <= self.qry + WINDOW)
        mask = jnp.logical_or(mask, self.key >= 2 * WINDOW)
        vaug = []
        for h in range(KV_HEADS):
            vth = self.vt[h * HEAD_DIM:(h + 1) * HEAD_DIM]
            vaug.append(jnp.concatenate(
                [jnp.concatenate([vth[:, r0:r0 + 2 * WINDOW], vth[:, self.meta0:]], axis=1),
                 self.ones], axis=0))
        outs = []
        for p, raw in enumerate(self.scores.pop(i)):
            h = (2 * p) // (Q_HEADS // KV_HEADS)
            s = jnp.where(mask, raw, -jnp.inf)
            sk = jnp.where(self.first_head, self.sink_ref[2 * p], self.sink_ref[2 * p + 1])
            m = jnp.maximum(jnp.max(s, axis=0, keepdims=True), sk)
            e = jnp.exp(s - m).astype(BF16)
            oa = _dot(vaug[h], e)
            o = oa[:HEAD_DIM] / (oa[HEAD_DIM:HEAD_DIM + 1] + jnp.exp(sk - m))
            outs += [o[:, :WINDOW], o[:, WINDOW:]]
        self.o_ref[r0:r0 + WINDOW, :] = jnp.concatenate(outs, axis=0).T.astype(self.o_ref.dtype)


def _cumsum_rows(x):
    t = x.shape[0]
    r = lax.broadcasted_iota(jnp.int32, (t, t), 0)
    c = lax.broadcasted_iota(jnp.int32, (t, t), 1)
    tri = jnp.where(r >= c, 1.0, 0.0).astype(BF16)
    hi = x.astype(BF16)
    r1 = x - hi.astype(F32)
    mid = r1.astype(BF16)
    lo = (r1 - mid.astype(F32)).astype(BF16)
    return _dot(tri, hi) + _dot(tri, mid) + _dot(tri, lo)


def _hg_out(o, gate, nw):
    parts = []
    for h in range(HG_HEADS):
        oh = o[:, h * HG_DV:(h + 1) * HG_DV]
        parts.append(oh * lax.rsqrt(jnp.mean(oh * oh, axis=-1, keepdims=True) + EPS) * nw)
    return jnp.concatenate(parts, axis=1) * gate


_HG_SLICES = [slice(h * HG_DK, (h + 1) * HG_DK) for h in range(HG_HEADS)]


def _meta_state(mk, mlogf, mi):
    b = _cumsum_rows(mlogf)
    kd = (mk * jnp.exp(b[-1:] - b)).astype(BF16)
    iv = mi.astype(BF16)
    return [_dot_tn(iv[:, hs], kd[:, hs]) for hs in _HG_SLICES]


class _Hgrn:
    def __init__(self, q_ref, k_ref, logf_ref, i_ref, g_ref, nw, state, o_ref):
        self.q_ref, self.k_ref, self.logf_ref, self.i_ref, self.g_ref = q_ref, k_ref, logf_ref, i_ref, g_ref
        self.nw, self.state, self.o_ref = nw, state, o_ref
        t = HG_CHUNK
        self.nchunk = q_ref.shape[0] // t
        r = lax.broadcasted_iota(jnp.int32, (t, t), 0)
        c = lax.broadcasted_iota(jnp.int32, (t, t), 1)
        self.causal = r >= c
        self.cums, self.terms, self.entering = {}, {}, {}

    def rows(self, ci):
        return slice(ci * HG_CHUNK, (ci + 1) * HG_CHUNK)

    def issue_cumsum(self, ci):
        self.cums[ci] = _cumsum_rows(self.logf_ref[self.rows(ci), :])

    def issue_local(self, ci):
        rows = self.rows(ci)
        b = self.cums.pop(ci)
        bl = b[-1:]
        k = self.k_ref[rows, :].astype(F32)
        qe = (self.q_ref[rows, :].astype(F32) * jnp.exp(b)).astype(BF16)
        ke = (k * jnp.exp(-b)).astype(BF16)
        kd = (k * jnp.exp(bl - b)).astype(BF16)
        iv = self.i_ref[rows, :].astype(BF16)
        a = [jnp.where(self.causal, _dot_nt(qe[:, hs], ke[:, hs]), 0.0).astype(BF16) for hs in _HG_SLICES]
        u = [_dot_tn(iv[:, hs], kd[:, hs]) for hs in _HG_SLICES]
        self.terms[ci] = (qe, iv, a)
        self.entering[ci] = [s.astype(BF16) for s in self.state]
        dec = jnp.exp(bl)
        self.state = [s * dec[:, hs] + uh for s, hs, uh in zip(self.state, _HG_SLICES, u)]

    def finish(self, ci):
        rows = self.rows(ci)
        qe, iv, a = self.terms.pop(ci)
        ent = self.entering.pop(ci)
        outs = [_dot(a[h], iv[:, hs]) + _dot_nt(qe[:, hs], ent[h]) for h, hs in enumerate(_HG_SLICES)]
        o = _hg_out(jnp.concatenate(outs, axis=1), self.g_ref[rows, :].astype(F32), self.nw)
        self.o_ref[rows, :] = o.astype(self.o_ref.dtype)


def _mixer_kernel(sink_ref, x0_ref, xnext_ref, g_ref, lbp_ref, w_ref, kvm_ref, nw_ref, mk_ref,
                  mlogf_ref, mi_ref,
                  att_ref, hg_ref, sga_ref, sgb_ref, sfin_ref, lastkv_ref,
                  xn_s, zq, zkv, zhq, zhk, zlogf, zhi, zhg, st_ref, mst_ref, *, tiles_per_seq):
    s = pl.program_id(0)
    t = xnext_ref.shape[0]

    @pl.when(s == 0)
    def _():
        xn_s[...] = _rmsnorm(x0_ref[...], g_ref[...]).astype(xn_s.dtype)
        for ref in (zq, zkv, zhq, zhk, zlogf, zhi, zhg, st_ref):
            ref[...] = jnp.zeros(ref.shape, ref.dtype)
        for h, m in enumerate(_meta_state(mk_ref[...], mlogf_ref[...], mi_ref[...])):
            mst_ref[h] = m

    first = lax.rem(jnp.maximum(s - 1, 0), tiles_per_seq) == 0

    att = _Attention(sink_ref, zq, jnp.concatenate([zkv[...], kvm_ref[...]], axis=0), first, att_ref)
    last = zkv[t:t + WINDOW, :]
    lastkv_ref[...] = last
    zkv[0:WINDOW, :] = last
    entering = [jnp.where(first, mst_ref[h], st_ref[h]) for h in range(HG_HEADS)]
    hg = _Hgrn(zhq, zhk, zlogf, zhi, zhg, nw_ref[...], entering, hg_ref)
    proj = _Projection(xn_s, lbp_ref[...], w_ref)

    def gate_a(p):
        sga_ref[:, p] = proj.branch_gate(C_GA, p).astype(sga_ref.dtype)

    def gate_b(p):
        sgb_ref[:, p] = proj.branch_gate(C_GB, p).astype(sgb_ref.dtype)

    def new_kv(_):
        zkv[WINDOW:WINDOW + t, :] = proj.kv()

    def new_forget(p):
        k_new, logf_new = proj.forget(p)
        zhk[:, p] = k_new.astype(zhk.dtype)
        zlogf[:, p] = logf_new

    def new_q(p):
        zq[:, p] = proj.q_att(p).astype(zq.dtype)

    def new_hq(p):
        zhq[:, p] = proj.q_hg(p).astype(zhq.dtype)

    def new_hi(p):
        zhi[:, p] = proj.i_hg(p).astype(zhi.dtype)

    def new_gate(p):
        zhg[:, p] = proj.swish_gate(p).astype(zhg.dtype)

    def parts(n):
        return [slice(c, c + PROJ_GROUP) for c in range(0, n, PROJ_GROUP)]

    groups = ([(gate_a, p) for p in parts(D_MODEL)] + [(gate_b, p) for p in parts(D_MODEL)]
              + [(new_kv, None)] + [(new_forget, p) for p in parts(D_HG)]
              + [(new_q, p) for p in parts(D_ATT)] + [(new_hq, p) for p in parts(D_HG)]
              + [(new_hi, p) for p in parts(D_HG)] + [(new_gate, p) for p in parts(D_HG)])

    def save_state():
        for h in range(HG_HEADS):
            st_ref[h] = hg.state[h]
            sfin_ref[h] = hg.state[h].T

    qk, cs, loc, fin, out = att.issue_scores, hg.issue_cumsum, hg.issue_local, att.finish, hg.finish
    mixer_work = {
        0: [(qk, 0), (cs, 0), (cs, 1), (cs, 2), (cs, 3)],
        1: [(cs, 4), (cs, 5), (loc, 0)],
        2: [(cs, 6), (cs, 7), (loc, 1)],
        3: [(qk, 1), (loc, 2)],
        4: [(fin, 0), (loc, 3)],
        5: [(out, 0), (loc, 4)],
        6: [(qk, 2), (out, 1), (loc, 5)],
        7: [(fin, 1), (out, 2), (loc, 6)],
        8: [(out, 3), (loc, 7), (save_state,)],
        9: [(qk, 3), (out, 4)],
        10: [(fin, 2), (out, 5)],
        11: [(out, 6)],
        12: [(out, 7)],
        13: [(fin, 3)],
    }
    for slot, (fn, p) in enumerate(groups):
        fn(p)
        for item in mixer_work.get(slot, []):
            item[0](*item[1:])
    xn_s[...] = _rmsnorm(xnext_ref[...], g_ref[...]).astype(xn_s.dtype)


def _mixers(sinks, x, g, lb_param, w, kv_small, hg_norm, hk_small, hlogf_small, hi_small, bsz):
    n = x.shape[0]
    t = MIX_ROWS
    nt = n // t
    per_seq = nt // bsz

    def this_tile(c):
        return pl.BlockSpec((t, c), lambda s: (jnp.minimum(s, nt - 1), 0))

    def prev_tile(c):
        return pl.BlockSpec((t, c), lambda s: (jnp.maximum(s - 1, 0), 0))

    def prev_seq(shape):
        return pl.BlockSpec((None,) + shape,
                            lambda s: (jnp.maximum(s - 1, 0) // per_seq,) + (0,) * len(shape))

    meta_block = kv_small.shape[0] // N_META - 1

    def meta_rows(c):
        return pl.BlockSpec((N_META, c), lambda s: (meta_block, 0), pipeline_mode=pl.Buffered(1))

    small = meta_rows(D_HG)
    first_tile = pl.BlockSpec((t, D_MODEL), lambda s: (0, 0), pipeline_mode=pl.Buffered(1))
    next_tile = pl.BlockSpec((t, D_MODEL), lambda s: (jnp.minimum(s + 1, nt - 1), 0))
    return pl.pallas_call(
        functools.partial(_mixer_kernel, tiles_per_seq=per_seq),
        grid=(nt + 1,),
        in_specs=[pl.BlockSpec(memory_space=pltpu.SMEM), first_tile, next_tile, _resident((1, D_MODEL)),
                  _resident(lb_param.shape), _resident((D_MODEL, D_IN)), meta_rows(2 * D_KV),
                  _resident((1, HG_DV)), small, small, small],
        out_specs=[prev_tile(D_ATT), prev_tile(D_HG), this_tile(D_MODEL), this_tile(D_MODEL),
                   prev_seq((HG_HEADS, HG_DK, HG_DV)), prev_seq((WINDOW, 2 * D_KV))],
        out_shape=[jax.ShapeDtypeStruct((n, D_ATT), BF16), jax.ShapeDtypeStruct((n, D_HG), BF16),
                   jax.ShapeDtypeStruct((n, D_MODEL), BF16), jax.ShapeDtypeStruct((n, D_MODEL), BF16),
                   jax.ShapeDtypeStruct((bsz, HG_HEADS, HG_DK, HG_DV), F32),
                   jax.ShapeDtypeStruct((bsz, WINDOW, 2 * D_KV), F32)],
        scratch_shapes=[pltpu.VMEM((t, D_MODEL), BF16),
                        pltpu.VMEM((t, D_ATT), BF16), pltpu.VMEM((WINDOW + t, 2 * D_KV), F32),
                        pltpu.VMEM((t, D_HG), BF16), pltpu.VMEM((t, D_HG), BF16),
                        pltpu.VMEM((t, D_HG), F32), pltpu.VMEM((t, D_HG), BF16),
                        pltpu.VMEM((t, D_HG), BF16),
                        pltpu.VMEM((HG_HEADS, HG_DV, HG_DK), F32),
                        pltpu.VMEM((HG_HEADS, HG_DV, HG_DK), F32)],
        compiler_params=_params("arbitrary"),
        name="mixers",
    )(sinks, x, x, g, lb_param, w, kv_small, hg_norm, hk_small, hlogf_small, hi_small)


def _merge_ffn_kernel(x_ref, att_ref, hg_ref, sga_ref, sgb_ref, wa_ref, wb_ref, wo_ref,
                      lnffn_ref, wup_ref, wdn_ref, lnf_ref, y_ref):
    ya = _dot(att_ref[...].astype(BF16), wa_ref[...])
    yb = _dot(hg_ref[...].astype(BF16), wb_ref[...])
    mix = sga_ref[...].astype(F32) * ya + sgb_ref[...].astype(F32) * yb
    h1 = x_ref[...] + _dot(mix.astype(BF16), wo_ref[...])
    xn = _rmsnorm(h1, lnffn_ref[...]).astype(BF16)
    acc = jnp.zeros_like(h1)
    step = 1024
    for c in range(D_FF // step):
        u = jnp.maximum(_dot(xn, wup_ref[:, c * step:(c + 1) * step]), 0.0)
        acc = acc + _dot((u * u).astype(BF16), wdn_ref[c * step:(c + 1) * step, :])
    y_ref[...] = _rmsnorm(h1 + acc, lnf_ref[...])


def _merge_ffn(x, att, hg, sga, sgb, wa, wb, wo, ln_ffn, w_up, w_down, ln_f, rows):
    n = x.shape[0]

    def blk(c):
        return pl.BlockSpec((rows, c), lambda i: (i, 0))

    return pl.pallas_call(
        _merge_ffn_kernel,
        grid=(n // rows,),
        in_specs=[blk(D_MODEL), blk(D_ATT), blk(D_HG), blk(D_MODEL), blk(D_MODEL),
                  _resident(wa.shape), _resident(wb.shape), _resident(wo.shape),
                  _resident((1, D_MODEL)), _resident(w_up.shape), _resident(w_down.shape),
                  _resident((1, D_MODEL))],
        out_specs=blk(D_MODEL),
        out_shape=jax.ShapeDtypeStruct((n, D_MODEL), F32),
        compiler_params=_params("parallel"),
        name="merge_ffn",
    )(x, att, hg, sga, sgb, wa, wb, wo, ln_ffn, w_up, w_down, ln_f)


def _column(row, eye):
    return jnp.sum(jnp.where(eye, row, 0.0), axis=1, keepdims=True)


def _sample_attn_kernel(sink_ref, qm_ref, kvn_ref, ckt_ref, cvt_ref, mk_ref, mv_ref,
                        o_ref, nkt_ref, nvt_ref):
    nb = qm_ref.shape[0]
    sk = sink_ref[...]
    r = lax.broadcasted_iota(jnp.int32, (D_KV, WINDOW), 0)
    c = lax.broadcasted_iota(jnp.int32, (D_KV, WINDOW), 1)
    eye = r == c
    newest = c == WINDOW - 1
    kvn = kvn_ref[...]
    for b in range(nb):
        kcol = _column(kvn[b:b + 1, :D_KV], eye)
        vcol = _column(kvn[b:b + 1, D_KV:], eye)
        nkt_ref[b] = jnp.where(newest, kcol, pltpu.roll(ckt_ref[b], WINDOW - 1, axis=1))
        nvt_ref[b] = jnp.where(newest, vcol, pltpu.roll(cvt_ref[b], WINDOW - 1, axis=1))
    scores = [(_dot(qm_ref[b], nkt_ref[b].astype(BF16)),
               _dot_nt(qm_ref[b], mk_ref[b].astype(BF16)))
              for b in range(nb)]
    for b, (s_w, s_m) in enumerate(scores):
        m = jnp.maximum(jnp.maximum(jnp.max(s_w, axis=1, keepdims=True),
                                    jnp.max(s_m, axis=1, keepdims=True)), sk)
        e_w = jnp.exp(s_w - m)
        e_m = jnp.exp(s_m - m)
        l = (jnp.sum(e_w, axis=1, keepdims=True) + jnp.sum(e_m, axis=1, keepdims=True)
             + jnp.exp(sk - m))
        o = (_dot_nt(e_w.astype(BF16), nvt_ref[b].astype(BF16))
             + _dot(e_m.astype(BF16), mv_ref[b].astype(BF16)))
        o_ref[b] = o / l


def _sample_attention(sinks_col, qm, kv_new, ck, cv, mk, mv):
    nb = ck.shape[0]
    g = SAMPLE_ATT_GROUP

    def blk3(a, c):
        return pl.BlockSpec((g, a, c), lambda i: (i, 0, 0))

    return pl.pallas_call(
        _sample_attn_kernel,
        grid=(nb // g,),
        in_specs=[_resident((Q_HEADS, 1)), blk3(Q_HEADS, D_KV),
                  pl.BlockSpec((g, 2 * D_KV), lambda i: (i, 0)),
                  blk3(D_KV, WINDOW), blk3(D_KV, WINDOW), blk3(N_META, D_KV), blk3(N_META, D_KV)],
        out_specs=[blk3(Q_HEADS, D_KV), blk3(D_KV, WINDOW), blk3(D_KV, WINDOW)],
        out_shape=[jax.ShapeDtypeStruct((nb, Q_HEADS, D_KV), F32),
                   jax.ShapeDtypeStruct((nb, D_KV, WINDOW), F32),
                   jax.ShapeDtypeStruct((nb, D_KV, WINDOW), F32)],
        compiler_params=_params("parallel"),
        name="sample_attn",
    )(sinks_col, qm, kv_new, ck, cv, mk, mv)


def _sample_hgrn_kernel(nw_ref, q_ref, k_ref, i_ref, g_ref, s_ref, o_ref, snew_ref, acc_ref):
    k = k_ref[...]
    q = q_ref[...]
    iv = i_ref[...]
    r = lax.broadcasted_iota(jnp.int32, (HG_DK, HG_DK), 0)
    c = lax.broadcasted_iota(jnp.int32, (HG_DK, HG_DK), 1)
    eye = r == c

    def column(row):
        return jnp.sum(jnp.where(eye, row, 0.0), axis=1, keepdims=True)

    for b in range(s_ref.shape[0]):
        for h, hs in enumerate(_HG_SLICES):
            kc = column(k[b:b + 1, hs])
            qc = column(q[b:b + 1, hs])
            s_old = s_ref[b, h]
            s = s_old + kc * (iv[b:b + 1, hs] - s_old)
            snew_ref[b, h] = s
            acc_ref[b:b + 1, hs] = jnp.sum(qc * s, axis=0, keepdims=True)
    o_ref[...] = _hg_out(acc_ref[...], g_ref[...], nw_ref[...])


def _sample_hgrn(hg_norm, hq, hk, hi, hgate, state):
    nb = state.shape[0]
    g = SAMPLE_HG_GROUP
    row = pl.BlockSpec((g, D_HG), lambda i: (i, 0))
    sblk = pl.BlockSpec((g, HG_HEADS, HG_DK, HG_DV), lambda i: (i, 0, 0, 0))
    return pl.pallas_call(
        _sample_hgrn_kernel,
        grid=(nb // g,),
        in_specs=[_resident((1, HG_DV)), row, row, row, row, sblk],
        out_specs=[row, sblk],
        out_shape=[jax.ShapeDtypeStruct((nb, D_HG), F32),
                   jax.ShapeDtypeStruct(state.shape, F32)],
        scratch_shapes=[pltpu.VMEM((g, D_HG), F32)],
        compiler_params=_params("parallel"),
        name="sample_hgrn",
    )(hg_norm, hq, hk, hi, hgate, state)


def kernel(x_prompt, x_sample, cache_k, cache_v, cache_meta_k, cache_meta_v, state_hgrn, meta,
           w_in, sinks, lb_param, hg_norm, w_att_out, w_hg_out, w_o, ln_mix, ln_ffn, w_up,
           w_down, ln_f):
    bsz, seq, _ = x_prompt.shape
    nb = x_sample.shape[0]
    w_in_b = w_in[0].astype(BF16)
    wa = w_att_out[0].astype(BF16)
    wb = w_hg_out[0].astype(BF16)
    wo = w_o[0].astype(BF16)
    wup = w_up[0].astype(BF16)
    wdn = w_down[0].astype(BF16)
    ln_mix2 = ln_mix.reshape(1, D_MODEL)
    ln_ffn2 = ln_ffn.reshape(1, D_MODEL)
    ln_f2 = ln_f.reshape(1, D_MODEL)
    nw = hg_norm.reshape(1, HG_DV)

    xs = x_sample.reshape(nb, D_MODEL)
    q_s, kv_s, hq_s, hk_s, hlogf_s, hi_s, hgate_s, sga_s, sgb_s = _project(xs, meta, ln_mix2, lb_param, w_in_b)

    xp = x_prompt.reshape(bsz * seq, D_MODEL)
    att_p, hg_p, sga_p, sgb_p, state_p, lastkv_p = _mixers(
        sinks.reshape(Q_HEADS), xp, ln_mix2, lb_param, w_in_b, kv_s, nw, hk_s, hlogf_s, hi_s, bsz)
    y_p = _merge_ffn(xp, att_p, hg_p, sga_p, sgb_p, wa, wb, wo, ln_ffn2, wup, wdn, ln_f2, MERGE_ROWS)

    grp = Q_HEADS // KV_HEADS
    rows = nb + N_META
    qs4 = q_s.reshape(rows, KV_HEADS, grp, 1, HEAD_DIM)
    sel = jnp.eye(KV_HEADS, dtype=BF16).reshape(1, KV_HEADS, 1, KV_HEADS, 1)
    qm = (qs4 * sel).reshape(rows, Q_HEADS, D_KV)

    def window_t(c):
        return jnp.swapaxes(c[0].reshape(nb, WINDOW, D_KV), 1, 2)

    o_s, nkt_s, nvt_s = _sample_attention(
        sinks.reshape(Q_HEADS, 1), qm, kv_s, window_t(cache_k), window_t(cache_v),
        cache_meta_k[0].reshape(nb, N_META, D_KV), cache_meta_v[0].reshape(nb, N_META, D_KV))
    o5 = o_s.reshape(nb, KV_HEADS, grp, KV_HEADS, HEAD_DIM)
    att_s = jnp.stack([o5[:, h, :, h, :] for h in range(KV_HEADS)], axis=1).reshape(nb, D_ATT)
    hg_s, state_s = _sample_hgrn(nw, hq_s, hk_s, hi_s, hgate_s, state_hgrn[0])
    y_s = _merge_ffn(xs, att_s, hg_s, sga_s, sgb_s, wa, wb, wo, ln_ffn2, wup, wdn, ln_f2, nb)

    kv5 = lastkv_p.reshape(bsz, WINDOW, 2, KV_HEADS, HEAD_DIM)
    meta5 = jnp.broadcast_to(kv_s[nb:].reshape(1, N_META, 2, KV_HEADS, HEAD_DIM),
                             (bsz, N_META, 2, KV_HEADS, HEAD_DIM))
    return (y_p.reshape(bsz, seq, D_MODEL),
            y_s.reshape(nb, 1, D_MODEL),
            kv5[None, :, :, 0],
            kv5[None, :, :, 1],
            meta5[None, :, :, 0],
            meta5[None, :, :, 1],
            state_p[None],
            jnp.swapaxes(nkt_s, 1, 2).reshape(1, nb, WINDOW, KV_HEADS, HEAD_DIM),
            jnp.swapaxes(nvt_s, 1, 2).reshape(1, nb, WINDOW, KV_HEADS, HEAD_DIM),
            state_s[None])
```

```python
import functools

import jax
import jax.numpy as jnp
from jax import lax
from jax.experimental import pallas as pl
from jax.experimental.pallas import tpu as pltpu

F32 = jnp.float32
BF16 = jnp.bfloat16

D_MODEL = 1024
N_META = 16
WINDOW = 128
HEAD_DIM = 64
Q_HEADS = 8
KV_HEADS = 2
D_ATT = Q_HEADS * HEAD_DIM
D_KV = KV_HEADS * HEAD_DIM
HG_HEADS = 4
HG_DK = 128
HG_DV = 128
D_HG = HG_HEADS * HG_DK
HG_CHUNK = 64
D_FF = 4 * D_MODEL
EPS = 1e-6
C_Q = 0
C_KV = C_Q + D_ATT
C_HQ = C_KV + 2 * D_KV
C_HF = C_HQ + D_HG
C_HI = C_HF + D_HG
C_HGATE = C_HI + D_HG
C_GA = C_HGATE + D_HG
C_GB = C_GA + D_MODEL
D_IN = C_GB + D_MODEL

VMEM_LIMIT_BYTES = 56 * 1024 * 1024
MIX_ROWS = 512
PROJ_GROUP = 256
MERGE_ROWS = 512
SAMPLE_ATT_GROUP = 16
SAMPLE_HG_GROUP = 16

_NT = (((1,), (1,)), ((), ()))
_TN = (((0,), (0,)), ((), ()))


def _dot(a, b):
    return jnp.dot(a, b, preferred_element_type=F32)


def _dot_nt(a, b):
    return lax.dot_general(a, b, _NT, preferred_element_type=F32)


def _dot_tn(a, b):
    return lax.dot_general(a, b, _TN, preferred_element_type=F32)


def _rmsnorm(x, g):
    return x * lax.rsqrt(jnp.mean(x * x, axis=-1, keepdims=True) + EPS) * g


def _resident(shape):
    return pl.BlockSpec(shape, lambda *_: (0,) * len(shape), pipeline_mode=pl.Buffered(1))


def _params(*sem):
    return pltpu.CompilerParams(dimension_semantics=sem, vmem_limit_bytes=VMEM_LIMIT_BYTES)


def _lower_bound(lbp):
    e = jnp.exp(lbp - jnp.max(lbp, axis=0, keepdims=True))
    return e[0:1] / jnp.sum(e, axis=0, keepdims=True)


class _Projection:
    def __init__(self, xn, lbp, w_ref):
        self.xn = xn
        self.w_ref = w_ref
        self.lb = _lower_bound(lbp)

    def cols(self, base, part):
        return _dot(self.xn[...], self.w_ref[:, base + part.start:base + part.stop])

    def q_att(self, part=slice(0, D_ATT)):
        return self.cols(C_Q, part) * (HEAD_DIM ** -0.5)

    def kv(self):
        return self.cols(C_KV, slice(0, 2 * D_KV))

    def q_hg(self, part=slice(0, D_HG)):
        return self.cols(C_HQ, part) * (HG_DK ** -0.5)

    def forget(self, part=slice(0, D_HG)):
        lb = self.lb[:, part]
        f = lb + (1.0 - lb) * jax.nn.sigmoid(self.cols(C_HF, part))
        return 1.0 - f, jnp.log(f)

    def i_hg(self, part=slice(0, D_HG)):
        return self.cols(C_HI, part)

    def swish_gate(self, part=slice(0, D_HG)):
        g = self.cols(C_HGATE, part)
        return g * jax.nn.sigmoid(g)

    def branch_gate(self, base, part=slice(0, D_MODEL)):
        return jax.nn.sigmoid(self.cols(base, part))


def _proj_kernel(xs_ref, xm_ref, g_ref, lbp_ref, w_ref, q_ref, kv_ref, hq_ref, hk_ref, hlogf_ref,
                 hi_ref, hgate_ref, sga_ref, sgb_ref):
    x = jnp.concatenate([xs_ref[...], xm_ref[...]], axis=0)
    p = _Projection(_rmsnorm(x, g_ref[...]).astype(BF16), lbp_ref[...], w_ref)
    q_ref[...] = p.q_att().astype(q_ref.dtype)
    kv_ref[...] = p.kv()
    hq_ref[...] = p.q_hg().astype(hq_ref.dtype)
    k, logf = p.forget()
    hk_ref[...] = k.astype(hk_ref.dtype)
    hlogf_ref[...] = logf
    hi_ref[...] = p.i_hg().astype(hi_ref.dtype)
    hgate_ref[...] = p.swish_gate().astype(hgate_ref.dtype)
    sga_ref[...] = p.branch_gate(C_GA).astype(sga_ref.dtype)
    sgb_ref[...] = p.branch_gate(C_GB).astype(sgb_ref.dtype)


def _project(x_sample, x_meta, g, lb_param, w):
    n = x_sample.shape[0] + x_meta.shape[0]
    widths = (D_ATT, 2 * D_KV, D_HG, D_HG, D_HG, D_HG, D_HG, D_MODEL, D_MODEL)
    dtypes = (BF16, F32, F32, F32, F32, F32, F32, BF16, BF16)
    return pl.pallas_call(
        _proj_kernel,
        grid=(1,),
        in_specs=[_resident(x_sample.shape), _resident(x_meta.shape), _resident((1, D_MODEL)),
                  _resident(lb_param.shape), _resident((D_MODEL, D_IN))],
        out_specs=[pl.BlockSpec((n, c), lambda i: (0, 0)) for c in widths],
        out_shape=[jax.ShapeDtypeStruct((n, c), d) for c, d in zip(widths, dtypes)],
        compiler_params=_params("arbitrary"),
        name="proj",
    )(x_sample, x_meta, g, lb_param, w)


class _Attention:
    def __init__(self, sink_ref, q_ref, kv, first, o_ref):
        self.sink_ref, self.q_ref, self.first, self.o_ref = sink_ref, q_ref, first, o_ref
        self.nsub = q_ref.shape[0] // WINDOW
        self.nk = 2 * WINDOW + N_META
        self.meta0 = WINDOW + q_ref.shape[0]
        lane = lax.broadcasted_iota(jnp.int32, (kv.shape[0], D_KV), 1)
        low = lane < HEAD_DIM
        k = kv[:, :D_KV]
        ksw = pltpu.roll(k, HEAD_DIM, axis=1)
        self.kboth = (jnp.where(low, k, ksw).astype(BF16), jnp.where(low, ksw, k).astype(BF16))
        qlane = lax.broadcasted_iota(jnp.int32, (1, 128), 1)
        self.keep = (jnp.where(qlane < HEAD_DIM, 1.0, 0.0).astype(BF16),
                     jnp.where(qlane < HEAD_DIM, 0.0, 1.0).astype(BF16))
        r = lax.broadcasted_iota(jnp.int32, (D_KV, D_KV), 0)
        c = lax.broadcasted_iota(jnp.int32, (D_KV, D_KV), 1)
        eye = jnp.where(r == c, 1.0, 0.0).astype(BF16)
        self.vt = _dot_nt(eye, kv[:, D_KV:].astype(BF16)).astype(BF16)
        self.ones = jnp.ones((16, self.nk), BF16)
        self.key = lax.broadcasted_iota(jnp.int32, (self.nk, 2 * WINDOW), 0)
        col = lax.broadcasted_iota(jnp.int32, (self.nk, 2 * WINDOW), 1)
        self.qry = jnp.bitwise_and(col, WINDOW - 1)
        self.first_head = lax.broadcasted_iota(jnp.int32, (1, 2 * WINDOW), 1) < WINDOW
        self.scores = {}

    def issue_scores(self, i):
        r0 = i * WINDOW
        out = []
        for p in range(Q_HEADS // 2):
            kb = self.kboth[(2 * p) // (Q_HEADS // KV_HEADS)]
            kmat = jnp.concatenate([kb[r0:r0 + 2 * WINDOW], kb[self.meta0:]], axis=0)
            qp = self.q_ref[r0:r0 + WINDOW, p * 128:(p + 1) * 128]
            q2 = jnp.concatenate([qp * self.keep[0], qp * self.keep[1]], axis=0)
            out.append(_dot_nt(kmat, q2))
        self.scores[i] = out

    def finish(self, i):
        r0 = i * WINDOW
        lo = jnp.where(self.first, WINDOW - 1, self.qry) if i == 0 else self.qry
        mask = jnp.logical_and(self.key > lo, self.key <= self.qry + WINDOW)
        mask = jnp.logical_or(mask, self.key >= 2 * WINDOW)
        vaug = []
        for h in range(KV_HEADS):
            vth = self.vt[h * HEAD_DIM:(h + 1) * HEAD_DIM]
            vaug.append(jnp.concatenate(
                [jnp.concatenate([vth[:, r0:r0 + 2 * WINDOW], vth[:, self.meta0:]], axis=1),
                 self.ones], axis=0))
        outs = []
        for p, raw in enumerate(self.scores.pop(i)):
            h = (2 * p) // (Q_HEADS // KV_HEADS)
            s = jnp.where(mask, raw, -jnp.inf)
            sk = jnp.where(self.first_head, self.sink_ref[2 * p], self.sink_ref[2 * p + 1])
            m = jnp.maximum(jnp.max(s, axis=0, keepdims=True), sk)
            e = jnp.exp(s - m).astype(BF16)
            oa = _dot(vaug[h], e)
            o = oa[:HEAD_DIM] / (oa[HEAD_DIM:HEAD_DIM + 1] + jnp.exp(sk - m))
            outs += [o[:, :WINDOW], o[:, WINDOW:]]
        self.o_ref[r0:r0 + WINDOW, :] = jnp.concatenate(outs, axis=0).T.astype(self.o_ref.dtype)


def _cumsum_rows(x):
    t = x.shape[0]
    r = lax.broadcasted_iota(jnp.int32, (t, t), 0)
    c = lax.broadcasted_iota(jnp.int32, (t, t), 1)
    tri = jnp.where(r >= c, 1.0, 0.0).astype(BF16)
    hi = x.astype(BF16)
    r1 = x - hi.astype(F32)
    mid = r1.astype(BF16)
    lo = (r1 - mid.astype(F32)).astype(BF16)
    return _dot(tri, hi) + _dot(tri, mid) + _dot(tri, lo)


def _hg_out(o, gate, nw):
    parts = []
    for h in range(HG_HEADS):
        oh = o[:, h * HG_DV:(h + 1) * HG_DV]
        parts.append(oh * lax.rsqrt(jnp.mean(oh * oh, axis=-1, keepdims=True) + EPS) * nw)
    return jnp.concatenate(parts, axis=1) * gate


_HG_SLICES = [slice(h * HG_DK, (h + 1) * HG_DK) for h in range(HG_HEADS)]


def _state_update(iv, kd):
    return jnp.concatenate([_dot_tn(iv[:, hs], kd[:, hs]) for hs in _HG_SLICES], axis=1)


def _meta_state(mk, mlogf, mi):
    b = _cumsum_rows(mlogf)
    kd = (mk * jnp.exp(b[-1:] - b)).astype(BF16)
    return _state_update(mi.astype(BF16), kd)


class _Hgrn:
    def __init__(self, q_ref, k_ref, logf_ref, i_ref, g_ref, nw, state, o_ref):
        self.q_ref, self.k_ref, self.logf_ref, self.i_ref, self.g_ref = q_ref, k_ref, logf_ref, i_ref, g_ref
        self.nw, self.state, self.o_ref = nw, state, o_ref
        t = HG_CHUNK
        self.nchunk = q_ref.shape[0] // t
        r = lax.broadcasted_iota(jnp.int32, (t, t), 0)
        c = lax.broadcasted_iota(jnp.int32, (t, t), 1)
        self.causal = r >= c
        self.cums, self.terms, self.entering = {}, {}, {}

    def rows(self, ci):
        return slice(ci * HG_CHUNK, (ci + 1) * HG_CHUNK)

    def issue_cumsum(self, ci):
        self.cums[ci] = _cumsum_rows(self.logf_ref[self.rows(ci), :])

    def issue_local(self, ci):
        rows = self.rows(ci)
        b = self.cums.pop(ci)
        bl = b[-1:]
        k = self.k_ref[rows, :].astype(F32)
        qe = (self.q_ref[rows, :].astype(F32) * jnp.exp(b)).astype(BF16)
        ke = (k * jnp.exp(-b)).astype(BF16)
        kd = (k * jnp.exp(bl - b)).astype(BF16)
        iv = self.i_ref[rows, :].astype(BF16)
        a = [jnp.where(self.causal, _dot_nt(qe[:, hs], ke[:, hs]), 0.0).astype(BF16) for hs in _HG_SLICES]
        self.terms[ci] = (qe, iv, a)
        self.entering[ci] = self.state.astype(BF16)
        self.state = self.state * jnp.exp(bl) + _state_update(iv, kd)

    def finish(self, ci):
        rows = self.rows(ci)
        qe, iv, a = self.terms.pop(ci)
        ent = self.entering.pop(ci)
        outs = [_dot(a[h], iv[:, hs]) + _dot_nt(qe[:, hs], ent[:, hs]) for h, hs in enumerate(_HG_SLICES)]
        o = _hg_out(jnp.concatenate(outs, axis=1), self.g_ref[rows, :].astype(F32), self.nw)
        self.o_ref[rows, :] = o.astype(self.o_ref.dtype)


def _mixer_kernel(sink_ref, x0_ref, xnext_ref, g_ref, lbp_ref, w_ref, kvm_ref, nw_ref, mk_ref,
                  mlogf_ref, mi_ref,
                  att_ref, hg_ref, sga_ref, sgb_ref, sfin_ref, lastkv_ref,
                  xn_s, zq, zkv, zhq, zhk, zlogf, zhi, zhg, st_ref, mst_ref, *, tiles_per_seq):
    s = pl.program_id(0)
    t = xnext_ref.shape[0]

    @pl.when(s == 0)
    def _():
        xn_s[...] = _rmsnorm(x0_ref[...], g_ref[...]).astype(xn_s.dtype)
        for ref in (zq, zkv, zhq, zhk, zlogf, zhi, zhg, st_ref):
            ref[...] = jnp.zeros(ref.shape, ref.dtype)
        mst_ref[...] = _meta_state(mk_ref[...], mlogf_ref[...], mi_ref[...])

    first = lax.rem(jnp.maximum(s - 1, 0), tiles_per_seq) == 0

    att = _Attention(sink_ref, zq, jnp.concatenate([zkv[...], kvm_ref[...]], axis=0), first, att_ref)
    last = zkv[t:t + WINDOW, :]
    lastkv_ref[...] = last
    zkv[0:WINDOW, :] = last
    entering = jnp.where(first, mst_ref[...], st_ref[...])
    hg = _Hgrn(zhq, zhk, zlogf, zhi, zhg, nw_ref[...], entering, hg_ref)
    proj = _Projection(xn_s, lbp_ref[...], w_ref)

    def gate_a(p):
        sga_ref[:, p] = proj.branch_gate(C_GA, p).astype(sga_ref.dtype)

    def gate_b(p):
        sgb_ref[:, p] = proj.branch_gate(C_GB, p).astype(sgb_ref.dtype)

    def new_kv(_):
        zkv[WINDOW:WINDOW + t, :] = proj.kv()

    def new_forget(p):
        k_new, logf_new = proj.forget(p)
        zhk[:, p] = k_new.astype(zhk.dtype)
        zlogf[:, p] = logf_new

    def new_q(p):
        zq[:, p] = proj.q_att(p).astype(zq.dtype)

    def new_hq(p):
        zhq[:, p] = proj.q_hg(p).astype(zhq.dtype)

    def new_hi(p):
        zhi[:, p] = proj.i_hg(p).astype(zhi.dtype)

    def new_gate(p):
        zhg[:, p] = proj.swish_gate(p).astype(zhg.dtype)

    def parts(n):
        return [slice(c, c + PROJ_GROUP) for c in range(0, n, PROJ_GROUP)]

    groups = ([(gate_a, p) for p in parts(D_MODEL)] + [(gate_b, p) for p in parts(D_MODEL)]
              + [(new_kv, None)] + [(new_forget, p) for p in parts(D_HG)]
              + [(new_q, p) for p in parts(D_ATT)] + [(new_hq, p) for p in parts(D_HG)]
              + [(new_hi, p) for p in parts(D_HG)] + [(new_gate, p) for p in parts(D_HG)])

    def save_state():
        st_ref[...] = hg.state
        for h, hs in enumerate(_HG_SLICES):
            sfin_ref[h] = hg.state[:, hs].T

    qk, cs, loc, fin, out = att.issue_scores, hg.issue_cumsum, hg.issue_local, att.finish, hg.finish
    mixer_work = {
        0: [(qk, 0), (cs, 0), (cs, 1), (cs, 2), (cs, 3)],
        1: [(cs, 4), (cs, 5), (loc, 0)],
        2: [(cs, 6), (cs, 7), (loc, 1)],
        3: [(qk, 1), (loc, 2)],
        4: [(fin, 0), (loc, 3)],
        5: [(out, 0), (loc, 4)],
        6: [(qk, 2), (out, 1), (loc, 5)],
        7: [(fin, 1), (out, 2), (loc, 6)],
        8: [(out, 3), (loc, 7), (save_state,)],
        9: [(qk, 3), (out, 4)],
        10: [(fin, 2), (out, 5)],
        11: [(out, 6)],
        12: [(out, 7)],
        13: [(fin, 3)],
    }
    for slot, (fn, p) in enumerate(groups):
        fn(p)
        for item in mixer_work.get(slot, []):
            item[0](*item[1:])
    xn_s[...] = _rmsnorm(xnext_ref[...], g_ref[...]).astype(xn_s.dtype)


def _mixers(sinks, x, g, lb_param, w, kv_small, hg_norm, hk_small, hlogf_small, hi_small, bsz):
    n = x.shape[0]
    t = MIX_ROWS
    nt = n // t
    per_seq = nt // bsz

    def this_tile(c):
        return pl.BlockSpec((t, c), lambda s: (jnp.minimum(s, nt - 1), 0))

    def prev_tile(c):
        return pl.BlockSpec((t, c), lambda s: (jnp.maximum(s - 1, 0), 0))

    def prev_seq(shape):
        return pl.BlockSpec((None,) + shape,
                            lambda s: (jnp.maximum(s - 1, 0) // per_seq,) + (0,) * len(shape))

    meta_block = kv_small.shape[0] // N_META - 1

    def meta_rows(c):
        return pl.BlockSpec((N_META, c), lambda s: (meta_block, 0), pipeline_mode=pl.Buffered(1))

    small = meta_rows(D_HG)
    first_tile = pl.BlockSpec((t, D_MODEL), lambda s: (0, 0), pipeline_mode=pl.Buffered(1))
    next_tile = pl.BlockSpec((t, D_MODEL), lambda s: (jnp.minimum(s + 1, nt - 1), 0))
    return pl.pallas_call(
        functools.partial(_mixer_kernel, tiles_per_seq=per_seq),
        grid=(nt + 1,),
        in_specs=[pl.BlockSpec(memory_space=pltpu.SMEM), first_tile, next_tile, _resident((1, D_MODEL)),
                  _resident(lb_param.shape), _resident((D_MODEL, D_IN)), meta_rows(2 * D_KV),
                  _resident((1, HG_DV)), small, small, small],
        out_specs=[prev_tile(D_ATT), prev_tile(D_HG), this_tile(D_MODEL), this_tile(D_MODEL),
                   prev_seq((HG_HEADS, HG_DK, HG_DV)), prev_seq((WINDOW, 2 * D_KV))],
        out_shape=[jax.ShapeDtypeStruct((n, D_ATT), BF16), jax.ShapeDtypeStruct((n, D_HG), BF16),
                   jax.ShapeDtypeStruct((n, D_MODEL), BF16), jax.ShapeDtypeStruct((n, D_MODEL), BF16),
                   jax.ShapeDtypeStruct((bsz, HG_HEADS, HG_DK, HG_DV), F32),
                   jax.ShapeDtypeStruct((bsz, WINDOW, 2 * D_KV), F32)],
        scratch_shapes=[pltpu.VMEM((t, D_MODEL), BF16),
                        pltpu.VMEM((t, D_ATT), BF16), pltpu.VMEM((WINDOW + t, 2 * D_KV), F32),
                        pltpu.VMEM((t, D_HG), BF16), pltpu.VMEM((t, D_HG), BF16),
                        pltpu.VMEM((t, D_HG), F32), pltpu.VMEM((t, D_HG), BF16),
                        pltpu.VMEM((t, D_HG), BF16),
                        pltpu.VMEM((HG_DV, D_HG), F32),
                        pltpu.VMEM((HG_DV, D_HG), F32)],
        compiler_params=_params("arbitrary"),
        name="mixers",
    )(sinks, x, x, g, lb_param, w, kv_small, hg_norm, hk_small, hlogf_small, hi_small)


def _merge_ffn_kernel(x_ref, att_ref, hg_ref, sga_ref, sgb_ref, wa_ref, wb_ref, wo_ref,
                      lnffn_ref, wup_ref, wdn_ref, lnf_ref, y_ref):
    ya = _dot(att_ref[...].astype(BF16), wa_ref[...])
    yb = _dot(hg_ref[...].astype(BF16), wb_ref[...])
    mix = sga_ref[...].astype(F32) * ya + sgb_ref[...].astype(F32) * yb
    h1 = x_ref[...] + _dot(mix.astype(BF16), wo_ref[...])
    xn = _rmsnorm(h1, lnffn_ref[...]).astype(BF16)
    acc = jnp.zeros_like(h1)
    step = 1024
    for c in range(D_FF // step):
        u = jnp.maximum(_dot(xn, wup_ref[:, c * step:(c + 1) * step]), 0.0)
        acc = acc + _dot((u * u).astype(BF16), wdn_ref[c * step:(c + 1) * step, :])
    y_ref[...] = _rmsnorm(h1 + acc, lnf_ref[...])


def _merge_ffn(x, att, hg, sga, sgb, wa, wb, wo, ln_ffn, w_up, w_down, ln_f, rows):
    n = x.shape[0]

    def blk(c):
        return pl.BlockSpec((rows, c), lambda i: (i, 0))

    return pl.pallas_call(
        _merge_ffn_kernel,
        grid=(n // rows,),
        in_specs=[blk(D_MODEL), blk(D_ATT), blk(D_HG), blk(D_MODEL), blk(D_MODEL),
                  _resident(wa.shape), _resident(wb.shape), _resident(wo.shape),
                  _resident((1, D_MODEL)), _resident(w_up.shape), _resident(w_down.shape),
                  _resident((1, D_MODEL))],
        out_specs=blk(D_MODEL),
        out_shape=jax.ShapeDtypeStruct((n, D_MODEL), F32),
        compiler_params=_params("parallel"),
        name="merge_ffn",
    )(x, att, hg, sga, sgb, wa, wb, wo, ln_ffn, w_up, w_down, ln_f)


def _column(row, eye):
    return jnp.sum(jnp.where(eye, row, 0.0), axis=1, keepdims=True)


def _sample_attn_kernel(sink_ref, qm_ref, kvn_ref, ckt_ref, cvt_ref, mk_ref, mv_ref,
                        o_ref, nkt_ref, nvt_ref):
    nb = qm_ref.shape[0]
    sk = sink_ref[...]
    newest = lax.broadcasted_iota(jnp.int32, (D_KV, WINDOW), 1) == WINDOW - 1
    kvn = kvn_ref[...]
    hi = kvn.astype(BF16)
    rest = kvn - hi.astype(F32)
    mid = rest.astype(BF16)
    lo = (rest - mid.astype(F32)).astype(BF16)
    r = lax.broadcasted_iota(jnp.int32, (2 * D_KV, 2 * D_KV), 0)
    c = lax.broadcasted_iota(jnp.int32, (2 * D_KV, 2 * D_KV), 1)
    eye = jnp.where(r == c, 1.0, 0.0).astype(BF16)
    pieces_t = _dot_nt(eye, jnp.concatenate([hi, mid, lo], axis=0)).astype(BF16)
    pr = lax.broadcasted_iota(jnp.int32, (3 * nb, WINDOW), 0)
    pc = lax.broadcasted_iota(jnp.int32, (3 * nb, WINDOW), 1)
    for b in range(nb):
        pick = jnp.logical_and(lax.rem(pr, nb) == b, pc == WINDOW - 1)
        new_cols = _dot(pieces_t, jnp.where(pick, 1.0, 0.0).astype(BF16))
        nkt_ref[b] = jnp.where(newest, new_cols[:D_KV], pltpu.roll(ckt_ref[b], WINDOW - 1, axis=1))
        nvt_ref[b] = jnp.where(newest, new_cols[D_KV:], pltpu.roll(cvt_ref[b], WINDOW - 1, axis=1))
    scores = [(_dot(qm_ref[b], nkt_ref[b].astype(BF16)),
               _dot_nt(qm_ref[b], mk_ref[b].astype(BF16)))
              for b in range(nb)]
    for b, (s_w, s_m) in enumerate(scores):
        m = jnp.maximum(jnp.maximum(jnp.max(s_w, axis=1, keepdims=True),
                                    jnp.max(s_m, axis=1, keepdims=True)), sk)
        e_w = jnp.exp(s_w - m)
        e_m = jnp.exp(s_m - m)
        l = (jnp.sum(e_w, axis=1, keepdims=True) + jnp.sum(e_m, axis=1, keepdims=True)
             + jnp.exp(sk - m))
        o = (_dot_nt(e_w.astype(BF16), nvt_ref[b].astype(BF16))
             + _dot(e_m.astype(BF16), mv_ref[b].astype(BF16)))
        o_ref[b] = o / l


def _sample_attention(sinks_col, qm, kv_new, ck, cv, mk, mv):
    nb = ck.shape[0]
    g = SAMPLE_ATT_GROUP

    def blk3(a, c):
        return pl.BlockSpec((g, a, c), lambda i: (i, 0, 0))

    return pl.pallas_call(
        _sample_attn_kernel,
        grid=(nb // g,),
        in_specs=[_resident((Q_HEADS, 1)), blk3(Q_HEADS, D_KV),
                  pl.BlockSpec((g, 2 * D_KV), lambda i: (i, 0)),
                  blk3(D_KV, WINDOW), blk3(D_KV, WINDOW), blk3(N_META, D_KV), blk3(N_META, D_KV)],
        out_specs=[blk3(Q_HEADS, D_KV), blk3(D_KV, WINDOW), blk3(D_KV, WINDOW)],
        out_shape=[jax.ShapeDtypeStruct((nb, Q_HEADS, D_KV), F32),
                   jax.ShapeDtypeStruct((nb, D_KV, WINDOW), F32),
                   jax.ShapeDtypeStruct((nb, D_KV, WINDOW), F32)],
        compiler_params=_params("parallel"),
        name="sample_attn",
    )(sinks_col, qm, kv_new, ck, cv, mk, mv)


def _sample_hgrn_kernel(nw_ref, q_ref, k_ref, i_ref, g_ref, s_ref, o_ref, snew_ref, acc_ref):
    k = k_ref[...]
    q = q_ref[...]
    iv = i_ref[...]
    r = lax.broadcasted_iota(jnp.int32, (HG_DK, HG_DK), 0)
    c = lax.broadcasted_iota(jnp.int32, (HG_DK, HG_DK), 1)
    eye = r == c

    def column(row):
        return jnp.sum(jnp.where(eye, row, 0.0), axis=1, keepdims=True)

    for b in range(s_ref.shape[0]):
        for h, hs in enumerate(_HG_SLICES):
            kc = column(k[b:b + 1, hs])
            qc = column(q[b:b + 1, hs])
            s_old = s_ref[b, h]
            s = s_old + kc * (iv[b:b + 1, hs] - s_old)
            snew_ref[b, h] = s
            acc_ref[b:b + 1, hs] = jnp.sum(qc * s, axis=0, keepdims=True)
    o_ref[...] = _hg_out(acc_ref[...], g_ref[...], nw_ref[...])


def _sample_hgrn(hg_norm, hq, hk, hi, hgate, state):
    nb = state.shape[0]
    g = SAMPLE_HG_GROUP
    row = pl.BlockSpec((g, D_HG), lambda i: (i, 0))
    sblk = pl.BlockSpec((g, HG_HEADS, HG_DK, HG_DV), lambda i: (i, 0, 0, 0))
    return pl.pallas_call(
        _sample_hgrn_kernel,
        grid=(nb // g,),
        in_specs=[_resident((1, HG_DV)), row, row, row, row, sblk],
        out_specs=[row, sblk],
        out_shape=[jax.ShapeDtypeStruct((nb, D_HG), F32),
                   jax.ShapeDtypeStruct(state.shape, F32)],
        scratch_shapes=[pltpu.VMEM((g, D_HG), F32)],
        compiler_params=_params("parallel"),
        name="sample_hgrn",
    )(hg_norm, hq, hk, hi, hgate, state)


def kernel(x_prompt, x_sample, cache_k, cache_v, cache_meta_k, cache_meta_v, state_hgrn, meta,
           w_in, sinks, lb_param, hg_norm, w_att_out, w_hg_out, w_o, ln_mix, ln_ffn, w_up,
           w_down, ln_f):
    bsz, seq, _ = x_prompt.shape
    nb = x_sample.shape[0]
    w_in_b = w_in[0].astype(BF16)
    wa = w_att_out[0].astype(BF16)
    wb = w_hg_out[0].astype(BF16)
    wo = w_o[0].astype(BF16)
    wup = w_up[0].astype(BF16)
    wdn = w_down[0].astype(BF16)
    ln_mix2 = ln_mix.reshape(1, D_MODEL)
    ln_ffn2 = ln_ffn.reshape(1, D_MODEL)
    ln_f2 = ln_f.reshape(1, D_MODEL)
    nw = hg_norm.reshape(1, HG_DV)

    xs = x_sample.reshape(nb, D_MODEL)
    q_s, kv_s, hq_s, hk_s, hlogf_s, hi_s, hgate_s, sga_s, sgb_s = _project(xs, meta, ln_mix2, lb_param, w_in_b)

    xp = x_prompt.reshape(bsz * seq, D_MODEL)
    att_p, hg_p, sga_p, sgb_p, state_p, lastkv_p = _mixers(
        sinks.reshape(Q_HEADS), xp, ln_mix2, lb_param, w_in_b, kv_s, nw, hk_s, hlogf_s, hi_s, bsz)
    y_p = _merge_ffn(xp, att_p, hg_p, sga_p, sgb_p, wa, wb, wo, ln_ffn2, wup, wdn, ln_f2, MERGE_ROWS)

    grp = Q_HEADS // KV_HEADS
    rows = nb + N_META
    qs4 = q_s.reshape(rows, KV_HEADS, grp, 1, HEAD_DIM)
    sel = jnp.eye(KV_HEADS, dtype=BF16).reshape(1, KV_HEADS, 1, KV_HEADS, 1)
    qm = (qs4 * sel).reshape(rows, Q_HEADS, D_KV)

    def window_t(c):
        return jnp.swapaxes(c[0].reshape(nb, WINDOW, D_KV), 1, 2)

    o_s, nkt_s, nvt_s = _sample_attention(
        sinks.reshape(Q_HEADS, 1), qm, kv_s, window_t(cache_k), window_t(cache_v),
        cache_meta_k[0].reshape(nb, N_META, D_KV), cache_meta_v[0].reshape(nb, N_META, D_KV))
    o5 = o_s.reshape(nb, KV_HEADS, grp, KV_HEADS, HEAD_DIM)
    att_s = jnp.stack([o5[:, h, :, h, :] for h in range(KV_HEADS)], axis=1).reshape(nb, D_ATT)
    hg_s, state_s = _sample_hgrn(nw, hq_s, hk_s, hi_s, hgate_s, state_hgrn[0])
    y_s = _merge_ffn(xs, att_s, hg_s, sga_s, sgb_s, wa, wb, wo, ln_ffn2, wup, wdn, ln_f2, nb)

    kv5 = lastkv_p.reshape(bsz, WINDOW, 2, KV_HEADS, HEAD_DIM)
    meta5 = jnp.broadcast_to(kv_s[nb:].reshape(1, N_META, 2, KV_HEADS, HEAD_DIM),
                             (bsz, N_META, 2, KV_HEADS, HEAD_DIM))
    return (y_p.reshape(bsz, seq, D_MODEL),
            y_s.reshape(nb, 1, D_MODEL),
            kv5[None, :, :, 0],
            kv5[None, :, :, 1],
            meta5[None, :, :, 0],
            meta5[None, :, :, 1],
            state_p[None],
            jnp.swapaxes(nkt_s, 1, 2).reshape(1, nb, WINDOW, KV_HEADS, HEAD_DIM),
            jnp.swapaxes(nvt_s, 1, 2).reshape(1, nb, WINDOW, KV_HEADS, HEAD_DIM),
            state_s[None])
```

```python
import functools

import jax
import jax.numpy as jnp
from jax import lax
from jax.experimental import pallas as pl
from jax.experimental.pallas import tpu as pltpu

F32 = jnp.float32
BF16 = jnp.bfloat16

D_MODEL = 1024
N_META = 16
WINDOW = 128
HEAD_DIM = 64
Q_HEADS = 8
KV_HEADS = 2
D_ATT = Q_HEADS * HEAD_DIM
D_KV = KV_HEADS * HEAD_DIM
HG_HEADS = 4
HG_DK = 128
HG_DV = 128
D_HG = HG_HEADS * HG_DK
HG_CHUNK = 64
D_FF = 4 * D_MODEL
EPS = 1e-6
C_Q = 0
C_KV = C_Q + D_ATT
C_HQ = C_KV + 2 * D_KV
C_HF = C_HQ + D_HG
C_HI = C_HF + D_HG
C_HGATE = C_HI + D_HG
C_GA = C_HGATE + D_HG
C_GB = C_GA + D_MODEL
D_IN = C_GB + D_MODEL

VMEM_LIMIT_BYTES = 56 * 1024 * 1024
MIX_ROWS = 512
PROJ_GROUP = 256
WEIGHT_SLAB_ROWS = 64
MERGE_ROWS = 512
SAMPLE_ATT_GROUP = 16
SAMPLE_HG_GROUP = 16

_NT = (((1,), (1,)), ((), ()))
_TN = (((0,), (0,)), ((), ()))


def _dot(a, b):
    return jnp.dot(a, b, preferred_element_type=F32)


def _dot_nt(a, b):
    return lax.dot_general(a, b, _NT, preferred_element_type=F32)


def _dot_tn(a, b):
    return lax.dot_general(a, b, _TN, preferred_element_type=F32)


def _rmsnorm(x, g):
    return x * lax.rsqrt(jnp.mean(x * x, axis=-1, keepdims=True) + EPS) * g


def _resident(shape):
    return pl.BlockSpec(shape, lambda *_: (0,) * len(shape), pipeline_mode=pl.Buffered(1))


def _params(*sem):
    return pltpu.CompilerParams(dimension_semantics=sem, vmem_limit_bytes=VMEM_LIMIT_BYTES)


def _lower_bound(lbp):
    e = jnp.exp(lbp - jnp.max(lbp, axis=0, keepdims=True))
    return e[0:1] / jnp.sum(e, axis=0, keepdims=True)


class _Projection:
    def __init__(self, xn, lbp, w_ref):
        self.xn = xn
        self.w_ref = w_ref
        self.lb = _lower_bound(lbp)

    def cols(self, base, part):
        return _dot(self.xn[...], self.w_ref[:, base + part.start:base + part.stop])

    def q_att(self, part=slice(0, D_ATT)):
        return self.cols(C_Q, part) * (HEAD_DIM ** -0.5)

    def kv(self):
        return self.cols(C_KV, slice(0, 2 * D_KV))

    def q_hg(self, part=slice(0, D_HG)):
        return self.cols(C_HQ, part) * (HG_DK ** -0.5)

    def forget(self, part=slice(0, D_HG)):
        lb = self.lb[:, part]
        f = lb + (1.0 - lb) * jax.nn.sigmoid(self.cols(C_HF, part))
        return 1.0 - f, jnp.log(f)

    def i_hg(self, part=slice(0, D_HG)):
        return self.cols(C_HI, part)

    def swish_gate(self, part=slice(0, D_HG)):
        g = self.cols(C_HGATE, part)
        return g * jax.nn.sigmoid(g)

    def branch_gate(self, base, part=slice(0, D_MODEL)):
        return jax.nn.sigmoid(self.cols(base, part))


class _Attention:
    def __init__(self, sink_ref, q_ref, kv, first, o_ref):
        self.sink_ref, self.q_ref, self.first, self.o_ref = sink_ref, q_ref, first, o_ref
        self.nsub = q_ref.shape[0] // WINDOW
        self.nk = 2 * WINDOW + N_META
        self.meta0 = WINDOW + q_ref.shape[0]
        lane = lax.broadcasted_iota(jnp.int32, (kv.shape[0], D_KV), 1)
        low = lane < HEAD_DIM
        k = kv[:, :D_KV]
        ksw = pltpu.roll(k, HEAD_DIM, axis=1)
        self.kboth = (jnp.where(low, k, ksw).astype(BF16), jnp.where(low, ksw, k).astype(BF16))
        qlane = lax.broadcasted_iota(jnp.int32, (1, 128), 1)
        self.keep = (jnp.where(qlane < HEAD_DIM, 1.0, 0.0).astype(BF16),
                     jnp.where(qlane < HEAD_DIM, 0.0, 1.0).astype(BF16))
        r = lax.broadcasted_iota(jnp.int32, (D_KV, D_KV), 0)
        c = lax.broadcasted_iota(jnp.int32, (D_KV, D_KV), 1)
        eye = jnp.where(r == c, 1.0, 0.0).astype(BF16)
        self.vt = _dot_nt(eye, kv[:, D_KV:].astype(BF16)).astype(BF16)
        self.ones = jnp.ones((16, self.nk), BF16)
        self.key = lax.broadcasted_iota(jnp.int32, (self.nk, 2 * WINDOW), 0)
        col = lax.broadcasted_iota(jnp.int32, (self.nk, 2 * WINDOW), 1)
        self.qry = jnp.bitwise_and(col, WINDOW - 1)
        self.first_head = lax.broadcasted_iota(jnp.int32, (1, 2 * WINDOW), 1) < WINDOW
        self.scores = {}

    def issue_scores(self, i):
        r0 = i * WINDOW
        out = []
        for p in range(Q_HEADS // 2):
            kb = self.kboth[(2 * p) // (Q_HEADS // KV_HEADS)]
            kmat = jnp.concatenate([kb[r0:r0 + 2 * WINDOW], kb[self.meta0:]], axis=0)
            qp = self.q_ref[r0:r0 + WINDOW, p * 128:(p + 1) * 128]
            q2 = jnp.concatenate([qp * self.keep[0], qp * self.keep[1]], axis=0)
            out.append(_dot_nt(kmat, q2))
        self.scores[i] = out

    def finish(self, i):
        r0 = i * WINDOW
        lo = jnp.where(self.first, WINDOW - 1, self.qry) if i == 0 else self.qry
        mask = jnp.logical_and(self.key > lo, self.key <= self.qry + WINDOW)
        mask = jnp.logical_or(mask, self.key >= 2 * WINDOW)
        vaug = []
        for h in range(KV_HEADS):
            vth = self.vt[h * HEAD_DIM:(h + 1) * HEAD_DIM]
            vaug.append(jnp.concatenate(
                [jnp.concatenate([vth[:, r0:r0 + 2 * WINDOW], vth[:, self.meta0:]], axis=1),
                 self.ones], axis=0))
        outs = []
        for p, raw in enumerate(self.scores.pop(i)):
            h = (2 * p) // (Q_HEADS // KV_HEADS)
            s = jnp.where(mask, raw, -jnp.inf)
            sk = jnp.where(self.first_head, self.sink_ref[2 * p], self.sink_ref[2 * p + 1])
            m = jnp.maximum(jnp.max(s, axis=0, keepdims=True), sk)
            e = jnp.exp(s - m).astype(BF16)
            oa = _dot(vaug[h], e)
            o = oa[:HEAD_DIM] / (oa[HEAD_DIM:HEAD_DIM + 1] + jnp.exp(sk - m))
            outs += [o[:, :WINDOW], o[:, WINDOW:]]
        self.o_ref[r0:r0 + WINDOW, :] = jnp.concatenate(outs, axis=0).T.astype(self.o_ref.dtype)


def _cumsum_rows(x):
    t = x.shape[0]
    r = lax.broadcasted_iota(jnp.int32, (t, t), 0)
    c = lax.broadcasted_iota(jnp.int32, (t, t), 1)
    tri = jnp.where(r >= c, 1.0, 0.0).astype(BF16)
    hi = x.astype(BF16)
    r1 = x - hi.astype(F32)
    mid = r1.astype(BF16)
    lo = (r1 - mid.astype(F32)).astype(BF16)
    return _dot(tri, hi) + _dot(tri, mid) + _dot(tri, lo)


def _hg_out(o, gate, nw):
    parts = []
    for h in range(HG_HEADS):
        oh = o[:, h * HG_DV:(h + 1) * HG_DV]
        parts.append(oh * lax.rsqrt(jnp.mean(oh * oh, axis=-1, keepdims=True) + EPS) * nw)
    return jnp.concatenate(parts, axis=1) * gate


_HG_SLICES = [slice(h * HG_DK, (h + 1) * HG_DK) for h in range(HG_HEADS)]


def _state_update(iv, kd):
    return jnp.concatenate([_dot_tn(iv[:, hs], kd[:, hs]) for hs in _HG_SLICES], axis=1)


def _meta_state(mk, mlogf, mi):
    b = _cumsum_rows(mlogf)
    kd = (mk * jnp.exp(b[-1:] - b)).astype(BF16)
    return _state_update(mi.astype(BF16), kd)


class _Hgrn:
    def __init__(self, q_ref, k_ref, logf_ref, i_ref, g_ref, nw, state, o_ref):
        self.q_ref, self.k_ref, self.logf_ref, self.i_ref, self.g_ref = q_ref, k_ref, logf_ref, i_ref, g_ref
        self.nw, self.state, self.o_ref = nw, state, o_ref
        t = HG_CHUNK
        self.nchunk = q_ref.shape[0] // t
        r = lax.broadcasted_iota(jnp.int32, (t, t), 0)
        c = lax.broadcasted_iota(jnp.int32, (t, t), 1)
        self.causal = r >= c
        self.cums, self.terms, self.entering = {}, {}, {}

    def rows(self, ci):
        return slice(ci * HG_CHUNK, (ci + 1) * HG_CHUNK)

    def issue_cumsum(self, ci):
        self.cums[ci] = _cumsum_rows(self.logf_ref[self.rows(ci), :])

    def issue_local(self, ci):
        rows = self.rows(ci)
        b = self.cums.pop(ci)
        bl = b[-1:]
        k = self.k_ref[rows, :].astype(F32)
        qe = (self.q_ref[rows, :].astype(F32) * jnp.exp(b)).astype(BF16)
        ke = (k * jnp.exp(-b)).astype(BF16)
        kd = (k * jnp.exp(bl - b)).astype(BF16)
        iv = self.i_ref[rows, :].astype(BF16)
        a = [jnp.where(self.causal, _dot_nt(qe[:, hs], ke[:, hs]), 0.0).astype(BF16) for hs in _HG_SLICES]
        self.terms[ci] = (qe, iv, a)
        self.entering[ci] = self.state.astype(BF16)
        self.state = self.state * jnp.exp(bl) + _state_update(iv, kd)

    def finish(self, ci):
        rows = self.rows(ci)
        qe, iv, a = self.terms.pop(ci)
        ent = self.entering.pop(ci)
        outs = [_dot(a[h], iv[:, hs]) + _dot_nt(qe[:, hs], ent[:, hs]) for h, hs in enumerate(_HG_SLICES)]
        o = _hg_out(jnp.concatenate(outs, axis=1), self.g_ref[rows, :].astype(F32), self.nw)
        self.o_ref[rows, :] = o.astype(self.o_ref.dtype)


def _load_as_bf16(w_hbm, w_vmem, stage, sem):
    rows = stage.shape[1]
    nslab = w_hbm.shape[0] // rows

    def slab_copy(c):
        return pltpu.make_async_copy(w_hbm.at[pl.ds(c * rows, rows), :], stage.at[c % 2], sem.at[c % 2])

    slab_copy(0).start()
    for c in range(nslab):
        if c + 1 < nslab:
            slab_copy(c + 1).start()
        slab_copy(c).wait()
        w_vmem[pl.ds(c * rows, rows), :] = stage[c % 2].astype(w_vmem.dtype)


def _mixer_kernel(sink_ref, x0_ref, xnext_ref, xs_ref, xm_ref, g_ref, lbp_ref, w_hbm, nw_ref,
                  att_ref, hg_ref, sga_ref, sgb_ref, sfin_ref, lastkv_ref,
                  qs_ref, kvs_ref, hqs_ref, hks_ref, his_ref, hgs_ref, sgas_ref, sgbs_ref,
                  w_ref, stage, sem, xn_s, zq, zkv, zhq, zhk, zlogf, zhi, zhg, kvm_ref, st_ref, mst_ref,
                  *, tiles_per_seq):
    s = pl.program_id(0)
    t = xnext_ref.shape[0]

    @pl.when(s == 0)
    def _():
        _load_as_bf16(w_hbm, w_ref, stage, sem)
        xn_s[...] = _rmsnorm(x0_ref[...], g_ref[...]).astype(xn_s.dtype)
        for ref in (zq, zkv, zhq, zhk, zlogf, zhi, zhg, st_ref):
            ref[...] = jnp.zeros(ref.shape, ref.dtype)
        nb = xs_ref.shape[0]
        small = jnp.concatenate([xs_ref[...], xm_ref[...]], axis=0)
        p = _Projection(_rmsnorm(small, g_ref[...]).astype(BF16), lbp_ref[...], w_ref)
        qs_ref[...] = p.q_att().astype(qs_ref.dtype)
        kv = p.kv()
        kvs_ref[...] = kv
        kvm_ref[...] = kv[nb:]
        hqs_ref[...] = p.q_hg()
        k, logf = p.forget()
        hks_ref[...] = k
        iv = p.i_hg()
        his_ref[...] = iv
        hgs_ref[...] = p.swish_gate()
        sgas_ref[...] = p.branch_gate(C_GA).astype(sgas_ref.dtype)
        sgbs_ref[...] = p.branch_gate(C_GB).astype(sgbs_ref.dtype)
        mst_ref[...] = _meta_state(k[nb:], logf[nb:], iv[nb:])

    first = lax.rem(jnp.maximum(s - 1, 0), tiles_per_seq) == 0

    att = _Attention(sink_ref, zq, jnp.concatenate([zkv[...], kvm_ref[...]], axis=0), first, att_ref)
    last = zkv[t:t + WINDOW, :]
    lastkv_ref[...] = last
    zkv[0:WINDOW, :] = last
    entering = jnp.where(first, mst_ref[...], st_ref[...])
    hg = _Hgrn(zhq, zhk, zlogf, zhi, zhg, nw_ref[...], entering, hg_ref)
    proj = _Projection(xn_s, lbp_ref[...], w_ref)

    def gate_a(p):
        sga_ref[:, p] = proj.branch_gate(C_GA, p).astype(sga_ref.dtype)

    def gate_b(p):
        sgb_ref[:, p] = proj.branch_gate(C_GB, p).astype(sgb_ref.dtype)

    def new_kv(_):
        zkv[WINDOW:WINDOW + t, :] = proj.kv()

    def new_forget(p):
        k_new, logf_new = proj.forget(p)
        zhk[:, p] = k_new.astype(zhk.dtype)
        zlogf[:, p] = logf_new

    def new_q(p):
        zq[:, p] = proj.q_att(p).astype(zq.dtype)

    def new_hq(p):
        zhq[:, p] = proj.q_hg(p).astype(zhq.dtype)

    def new_hi(p):
        zhi[:, p] = proj.i_hg(p).astype(zhi.dtype)

    def new_gate(p):
        zhg[:, p] = proj.swish_gate(p).astype(zhg.dtype)

    def parts(n):
        return [slice(c, c + PROJ_GROUP) for c in range(0, n, PROJ_GROUP)]

    groups = ([(gate_a, p) for p in parts(D_MODEL)] + [(gate_b, p) for p in parts(D_MODEL)]
              + [(new_kv, None)] + [(new_q, p) for p in parts(D_ATT)]
              + [(new_forget, p) for p in parts(D_HG)] + [(new_hq, p) for p in parts(D_HG)]
              + [(new_hi, p) for p in parts(D_HG)] + [(new_gate, p) for p in parts(D_HG)])

    def save_state():
        st_ref[...] = hg.state
        for h, hs in enumerate(_HG_SLICES):
            sfin_ref[h] = hg.state[:, hs].T

    qk, cs, loc, fin, out = att.issue_scores, hg.issue_cumsum, hg.issue_local, att.finish, hg.finish
    mixer_work = {
        0: [(qk, 0), (cs, 0), (cs, 1), (cs, 2), (cs, 3)],
        1: [(cs, 4), (cs, 5), (loc, 0)],
        2: [(cs, 6), (cs, 7), (loc, 1)],
        3: [(qk, 1), (loc, 2)],
        4: [(fin, 0), (loc, 3)],
        5: [(out, 0), (loc, 4)],
        6: [(qk, 2), (out, 1), (loc, 5)],
        7: [(fin, 1), (out, 2), (loc, 6)],
        8: [(qk, 3), (out, 3), (loc, 7), (save_state,)],
        9: [(out, 4)],
        10: [(fin, 2), (out, 5)],
        11: [(out, 6)],
        12: [(out, 7)],
        13: [(fin, 3)],
    }
    for slot, (fn, p) in enumerate(groups):
        fn(p)
        for item in mixer_work.get(slot, []):
            item[0](*item[1:])
    xn_s[...] = _rmsnorm(xnext_ref[...], g_ref[...]).astype(xn_s.dtype)


def _mixers(sinks, x, x_sample, x_meta, g, lb_param, w_f32, hg_norm, bsz):
    n = x.shape[0]
    t = MIX_ROWS
    nt = n // t
    per_seq = nt // bsz
    rows = x_sample.shape[0] + x_meta.shape[0]

    def this_tile(c):
        return pl.BlockSpec((t, c), lambda s: (jnp.minimum(s, nt - 1), 0))

    def prev_tile(c):
        return pl.BlockSpec((t, c), lambda s: (jnp.maximum(s - 1, 0), 0))

    def prev_seq(shape):
        return pl.BlockSpec((None,) + shape,
                            lambda s: (jnp.maximum(s - 1, 0) // per_seq,) + (0,) * len(shape))

    def small(c):
        return pl.BlockSpec((rows, c), lambda s: (0, 0))

    first_tile = pl.BlockSpec((t, D_MODEL), lambda s: (0, 0), pipeline_mode=pl.Buffered(1))
    next_tile = pl.BlockSpec((t, D_MODEL), lambda s: (jnp.minimum(s + 1, nt - 1), 0))
    small_widths = (D_ATT, 2 * D_KV, D_HG, D_HG, D_HG, D_HG, D_MODEL, D_MODEL)
    small_dtypes = (BF16, F32, F32, F32, F32, F32, BF16, BF16)
    return pl.pallas_call(
        functools.partial(_mixer_kernel, tiles_per_seq=per_seq),
        grid=(nt + 1,),
        in_specs=[pl.BlockSpec(memory_space=pltpu.SMEM), first_tile, next_tile,
                  _resident(x_sample.shape), _resident(x_meta.shape), _resident((1, D_MODEL)),
                  _resident(lb_param.shape), pl.BlockSpec(memory_space=pl.ANY), _resident((1, HG_DV))],
        out_specs=[prev_tile(D_ATT), prev_tile(D_HG), this_tile(D_MODEL), this_tile(D_MODEL),
                   prev_seq((HG_HEADS, HG_DK, HG_DV)), prev_seq((WINDOW, 2 * D_KV))]
                  + [small(c) for c in small_widths],
        out_shape=[jax.ShapeDtypeStruct((n, D_ATT), BF16), jax.ShapeDtypeStruct((n, D_HG), BF16),
                   jax.ShapeDtypeStruct((n, D_MODEL), BF16), jax.ShapeDtypeStruct((n, D_MODEL), BF16),
                   jax.ShapeDtypeStruct((bsz, HG_HEADS, HG_DK, HG_DV), F32),
                   jax.ShapeDtypeStruct((bsz, WINDOW, 2 * D_KV), F32)]
                  + [jax.ShapeDtypeStruct((rows, c), d) for c, d in zip(small_widths, small_dtypes)],
        scratch_shapes=[pltpu.VMEM((D_MODEL, D_IN), BF16),
                        pltpu.VMEM((2, WEIGHT_SLAB_ROWS, D_IN), F32),
                        pltpu.SemaphoreType.DMA((2,)),
                        pltpu.VMEM((t, D_MODEL), BF16),
                        pltpu.VMEM((t, D_ATT), BF16), pltpu.VMEM((WINDOW + t, 2 * D_KV), F32),
                        pltpu.VMEM((t, D_HG), BF16), pltpu.VMEM((t, D_HG), BF16),
                        pltpu.VMEM((t, D_HG), F32), pltpu.VMEM((t, D_HG), BF16),
                        pltpu.VMEM((t, D_HG), BF16),
                        pltpu.VMEM((N_META, 2 * D_KV), F32),
                        pltpu.VMEM((HG_DV, D_HG), F32),
                        pltpu.VMEM((HG_DV, D_HG), F32)],
        compiler_params=_params("arbitrary"),
        name="mixers",
    )(sinks, x, x, x_sample, x_meta, g, lb_param, w_f32, hg_norm)


def _merge_ffn_rows(x, att, hg, sga, sgb, wa_ref, wb_ref, wo_ref, ln_ffn, wup_ref, wdn_ref, ln_f):
    ya = _dot(att.astype(BF16), wa_ref[...])
    yb = _dot(hg.astype(BF16), wb_ref[...])
    mix = sga.astype(F32) * ya + sgb.astype(F32) * yb
    h1 = x + _dot(mix.astype(BF16), wo_ref[...])
    xn = _rmsnorm(h1, ln_ffn).astype(BF16)
    acc = jnp.zeros_like(h1)
    step = 1024
    for c in range(D_FF // step):
        u = jnp.maximum(_dot(xn, wup_ref[:, c * step:(c + 1) * step]), 0.0)
        acc = acc + _dot((u * u).astype(BF16), wdn_ref[c * step:(c + 1) * step, :])
    return _rmsnorm(h1 + acc, ln_f)


def _merge_ffn_kernel(x_ref, att_ref, hg_ref, sga_ref, sgb_ref,
                      xs_ref, atts_ref, hgs_ref, sgas_ref, sgbs_ref,
                      wa_hbm, wb_hbm, wo_hbm, wup_hbm, wdn_hbm, lnffn_ref, lnf_ref,
                      y_ref, ys_ref,
                      wa_ref, wb_ref, wo_ref, wup_ref, wdn_ref, stage, stage_up, sem, *, prompt_steps):
    i = pl.program_id(0)

    @pl.when(i == 0)
    def _():
        for hbm, vmem in ((wa_hbm, wa_ref), (wb_hbm, wb_ref), (wo_hbm, wo_ref), (wdn_hbm, wdn_ref)):
            _load_as_bf16(hbm, vmem, stage, sem)
        _load_as_bf16(wup_hbm, wup_ref, stage_up, sem)

    weights = (wa_ref, wb_ref, wo_ref, lnffn_ref[...], wup_ref, wdn_ref, lnf_ref[...])

    @pl.when(i < prompt_steps)
    def _():
        y_ref[...] = _merge_ffn_rows(x_ref[...], att_ref[...], hg_ref[...], sga_ref[...], sgb_ref[...],
                                     *weights)

    @pl.when(i == prompt_steps)
    def _():
        ys_ref[...] = _merge_ffn_rows(xs_ref[...], atts_ref[...], hgs_ref[...], sgas_ref[...],
                                      sgbs_ref[...], *weights)


def _merge_ffn(x, att, hg, sga, sgb, xs, att_s, hg_s, sga_s, sgb_s, wa, wb, wo, ln_ffn, w_up, w_down, ln_f):
    n = x.shape[0]
    nb = xs.shape[0]
    rows = MERGE_ROWS
    nt = n // rows

    def blk(c):
        return pl.BlockSpec((rows, c), lambda i: (jnp.minimum(i, nt - 1), 0))

    def sample(c):
        return pl.BlockSpec((nb, c), lambda i: (0, 0))

    hbm = pl.BlockSpec(memory_space=pl.ANY)
    return pl.pallas_call(
        functools.partial(_merge_ffn_kernel, prompt_steps=nt),
        grid=(nt + 1,),
        in_specs=[blk(D_MODEL), blk(D_ATT), blk(D_HG), blk(D_MODEL), blk(D_MODEL),
                  sample(D_MODEL), sample(D_ATT), sample(D_HG), sample(D_MODEL), sample(D_MODEL),
                  hbm, hbm, hbm, hbm, hbm, _resident((1, D_MODEL)), _resident((1, D_MODEL))],
        out_specs=[blk(D_MODEL), sample(D_MODEL)],
        out_shape=[jax.ShapeDtypeStruct((n, D_MODEL), F32), jax.ShapeDtypeStruct((nb, D_MODEL), F32)],
        scratch_shapes=[pltpu.VMEM(wa.shape, BF16), pltpu.VMEM(wb.shape, BF16), pltpu.VMEM(wo.shape, BF16),
                        pltpu.VMEM(w_up.shape, BF16), pltpu.VMEM(w_down.shape, BF16),
                        pltpu.VMEM((2, 4 * WEIGHT_SLAB_ROWS, D_MODEL), F32),
                        pltpu.VMEM((2, WEIGHT_SLAB_ROWS, D_FF), F32),
                        pltpu.SemaphoreType.DMA((2,))],
        compiler_params=_params("arbitrary"),
        name="merge_ffn",
    )(x, att, hg, sga, sgb, xs, att_s, hg_s, sga_s, sgb_s, wa, wb, wo, w_up, w_down, ln_ffn, ln_f)


def _column(row, eye):
    return jnp.sum(jnp.where(eye, row, 0.0), axis=1, keepdims=True)


def _sample_attn_kernel(sink_ref, qm_ref, kvn_ref, ckt_ref, cvt_ref, mk_ref, mv_ref,
                        o_ref, nkt_ref, nvt_ref):
    nb = qm_ref.shape[0]
    sk = sink_ref[...]
    newest = lax.broadcasted_iota(jnp.int32, (D_KV, WINDOW), 1) == WINDOW - 1
    kvn = kvn_ref[...]
    hi = kvn.astype(BF16)
    rest = kvn - hi.astype(F32)
    mid = rest.astype(BF16)
    lo = (rest - mid.astype(F32)).astype(BF16)
    r = lax.broadcasted_iota(jnp.int32, (2 * D_KV, 2 * D_KV), 0)
    c = lax.broadcasted_iota(jnp.int32, (2 * D_KV, 2 * D_KV), 1)
    eye = jnp.where(r == c, 1.0, 0.0).astype(BF16)
    pieces_t = _dot_nt(eye, jnp.concatenate([hi, mid, lo], axis=0)).astype(BF16)
    pr = lax.broadcasted_iota(jnp.int32, (3 * nb, WINDOW), 0)
    pc = lax.broadcasted_iota(jnp.int32, (3 * nb, WINDOW), 1)
    for b in range(nb):
        pick = jnp.logical_and(lax.rem(pr, nb) == b, pc == WINDOW - 1)
        new_cols = _dot(pieces_t, jnp.where(pick, 1.0, 0.0).astype(BF16))
        nkt_ref[b] = jnp.where(newest, new_cols[:D_KV], pltpu.roll(ckt_ref[b], WINDOW - 1, axis=1))
        nvt_ref[b] = jnp.where(newest, new_cols[D_KV:], pltpu.roll(cvt_ref[b], WINDOW - 1, axis=1))
    scores = [(_dot(qm_ref[b], nkt_ref[b].astype(BF16)),
               _dot_nt(qm_ref[b], mk_ref[b].astype(BF16)))
              for b in range(nb)]
    for b, (s_w, s_m) in enumerate(scores):
        m = jnp.maximum(jnp.maximum(jnp.max(s_w, axis=1, keepdims=True),
                                    jnp.max(s_m, axis=1, keepdims=True)), sk)
        e_w = jnp.exp(s_w - m)
        e_m = jnp.exp(s_m - m)
        l = (jnp.sum(e_w, axis=1, keepdims=True) + jnp.sum(e_m, axis=1, keepdims=True)
             + jnp.exp(sk - m))
        o = (_dot_nt(e_w.astype(BF16), nvt_ref[b].astype(BF16))
             + _dot(e_m.astype(BF16), mv_ref[b].astype(BF16)))
        o_ref[b] = o / l


def _sample_attention(sinks_col, qm, kv_new, ck, cv, mk, mv):
    nb = ck.shape[0]
    g = SAMPLE_ATT_GROUP

    def blk3(a, c):
        return pl.BlockSpec((g, a, c), lambda i: (i, 0, 0))

    return pl.pallas_call(
        _sample_attn_kernel,
        grid=(nb // g,),
        in_specs=[_resident((Q_HEADS, 1)), blk3(Q_HEADS, D_KV),
                  pl.BlockSpec((g, 2 * D_KV), lambda i: (i, 0)),
                  blk3(D_KV, WINDOW), blk3(D_KV, WINDOW), blk3(N_META, D_KV), blk3(N_META, D_KV)],
        out_specs=[blk3(Q_HEADS, D_KV), blk3(D_KV, WINDOW), blk3(D_KV, WINDOW)],
        out_shape=[jax.ShapeDtypeStruct((nb, Q_HEADS, D_KV), F32),
                   jax.ShapeDtypeStruct((nb, D_KV, WINDOW), F32),
                   jax.ShapeDtypeStruct((nb, D_KV, WINDOW), F32)],
        compiler_params=_params("parallel"),
        name="sample_attn",
    )(sinks_col, qm, kv_new, ck, cv, mk, mv)


def _sample_hgrn_kernel(nw_ref, q_ref, k_ref, i_ref, g_ref, s_ref, o_ref, snew_ref, acc_ref):
    k = k_ref[...]
    q = q_ref[...]
    iv = i_ref[...]
    r = lax.broadcasted_iota(jnp.int32, (HG_DK, HG_DK), 0)
    c = lax.broadcasted_iota(jnp.int32, (HG_DK, HG_DK), 1)
    eye = r == c

    def column(row):
        return jnp.sum(jnp.where(eye, row, 0.0), axis=1, keepdims=True)

    for b in range(s_ref.shape[0]):
        for h, hs in enumerate(_HG_SLICES):
            kc = column(k[b:b + 1, hs])
            qc = column(q[b:b + 1, hs])
            s_old = s_ref[b, h]
            s = s_old + kc * (iv[b:b + 1, hs] - s_old)
            snew_ref[b, h] = s
            acc_ref[b:b + 1, hs] = jnp.sum(qc * s, axis=0, keepdims=True)
    o_ref[...] = _hg_out(acc_ref[...], g_ref[...], nw_ref[...])


def _sample_hgrn(hg_norm, hq, hk, hi, hgate, state):
    nb = state.shape[0]
    g = SAMPLE_HG_GROUP
    row = pl.BlockSpec((g, D_HG), lambda i: (i, 0))
    sblk = pl.BlockSpec((g, HG_HEADS, HG_DK, HG_DV), lambda i: (i, 0, 0, 0))
    return pl.pallas_call(
        _sample_hgrn_kernel,
        grid=(nb // g,),
        in_specs=[_resident((1, HG_DV)), row, row, row, row, sblk],
        out_specs=[row, sblk],
        out_shape=[jax.ShapeDtypeStruct((nb, D_HG), F32),
                   jax.ShapeDtypeStruct(state.shape, F32)],
        scratch_shapes=[pltpu.VMEM((g, D_HG), F32)],
        compiler_params=_params("parallel"),
        name="sample_hgrn",
    )(hg_norm, hq, hk, hi, hgate, state)


def kernel(x_prompt, x_sample, cache_k, cache_v, cache_meta_k, cache_meta_v, state_hgrn, meta,
           w_in, sinks, lb_param, hg_norm, w_att_out, w_hg_out, w_o, ln_mix, ln_ffn, w_up,
           w_down, ln_f):
    bsz, seq, _ = x_prompt.shape
    nb = x_sample.shape[0]
    ln_mix2 = ln_mix.reshape(1, D_MODEL)
    ln_ffn2 = ln_ffn.reshape(1, D_MODEL)
    ln_f2 = ln_f.reshape(1, D_MODEL)
    nw = hg_norm.reshape(1, HG_DV)

    xs = x_sample.reshape(nb, D_MODEL)

    xp = x_prompt.reshape(bsz * seq, D_MODEL)
    (att_p, hg_p, sga_p, sgb_p, state_p, lastkv_p,
     q_s, kv_s, hq_s, hk_s, hi_s, hgate_s, sga_s, sgb_s) = _mixers(
        sinks.reshape(Q_HEADS), xp, xs, meta, ln_mix2, lb_param, w_in[0], nw, bsz)

    grp = Q_HEADS // KV_HEADS
    rows = nb + N_META
    qs4 = q_s.reshape(rows, KV_HEADS, grp, 1, HEAD_DIM)
    sel = jnp.eye(KV_HEADS, dtype=BF16).reshape(1, KV_HEADS, 1, KV_HEADS, 1)
    qm = (qs4 * sel).reshape(rows, Q_HEADS, D_KV)

    def window_t(c):
        return jnp.swapaxes(c[0].reshape(nb, WINDOW, D_KV), 1, 2)

    o_s, nkt_s, nvt_s = _sample_attention(
        sinks.reshape(Q_HEADS, 1), qm, kv_s, window_t(cache_k), window_t(cache_v),
        cache_meta_k[0].reshape(nb, N_META, D_KV), cache_meta_v[0].reshape(nb, N_META, D_KV))
    o5 = o_s.reshape(nb, KV_HEADS, grp, KV_HEADS, HEAD_DIM)
    att_s = jnp.stack([o5[:, h, :, h, :] for h in range(KV_HEADS)], axis=1).reshape(nb, D_ATT)
    hg_s, state_s = _sample_hgrn(nw, hq_s, hk_s, hi_s, hgate_s, state_hgrn[0])

    y_p, y_s = _merge_ffn(xp, att_p, hg_p, sga_p, sgb_p, xs, att_s, hg_s, sga_s, sgb_s,
                          w_att_out[0], w_hg_out[0], w_o[0], ln_ffn2, w_up[0], w_down[0], ln_f2)

    kv5 = lastkv_p.reshape(bsz, WINDOW, 2, KV_HEADS, HEAD_DIM)
    meta5 = jnp.broadcast_to(kv_s[nb:].reshape(1, N_META, 2, KV_HEADS, HEAD_DIM),
                             (bsz, N_META, 2, KV_HEADS, HEAD_DIM))
    return (y_p.reshape(bsz, seq, D_MODEL),
            y_s.reshape(nb, 1, D_MODEL),
            kv5[None, :, :, 0],
            kv5[None, :, :, 1],
            meta5[None, :, :, 0],
            meta5[None, :, :, 1],
            state_p[None],
            jnp.swapaxes(nkt_s, 1, 2).reshape(1, nb, WINDOW, KV_HEADS, HEAD_DIM),
            jnp.swapaxes(nvt_s, 1, 2).reshape(1, nb, WINDOW, KV_HEADS, HEAD_DIM),
            state_s[None])
```

```python
import functools

import jax
import jax.numpy as jnp
from jax import lax
from jax.experimental import pallas as pl
from jax.experimental.pallas import tpu as pltpu

F32 = jnp.float32
BF16 = jnp.bfloat16

D_MODEL = 1024
N_META = 16
WINDOW = 128
HEAD_DIM = 64
Q_HEADS = 8
KV_HEADS = 2
D_ATT = Q_HEADS * HEAD_DIM
D_KV = KV_HEADS * HEAD_DIM
HG_HEADS = 4
HG_DK = 128
HG_DV = 128
D_HG = HG_HEADS * HG_DK
HG_CHUNK = 64
D_FF = 4 * D_MODEL
EPS = 1e-6
C_Q = 0
C_KV = C_Q + D_ATT
C_HQ = C_KV + 2 * D_KV
C_HF = C_HQ + D_HG
C_HI = C_HF + D_HG
C_HGATE = C_HI + D_HG
C_GA = C_HGATE + D_HG
C_GB = C_GA + D_MODEL
D_IN = C_GB + D_MODEL

VMEM_LIMIT_BYTES = 56 * 1024 * 1024
MIX_ROWS = 512
PROJ_GROUP = 256
WEIGHT_SLAB_ROWS = 64
MERGE_ROWS = 512
SAMPLE_ATT_GROUP = 16
SAMPLE_HG_GROUP = 16

_NT = (((1,), (1,)), ((), ()))
_TN = (((0,), (0,)), ((), ()))


def _dot(a, b):
    return jnp.dot(a, b, preferred_element_type=F32)


def _dot_nt(a, b):
    return lax.dot_general(a, b, _NT, preferred_element_type=F32)


def _dot_tn(a, b):
    return lax.dot_general(a, b, _TN, preferred_element_type=F32)


def _rmsnorm(x, g):
    return x * lax.rsqrt(jnp.mean(x * x, axis=-1, keepdims=True) + EPS) * g


def _resident(shape):
    return pl.BlockSpec(shape, lambda *_: (0,) * len(shape), pipeline_mode=pl.Buffered(1))


def _params(*sem):
    return pltpu.CompilerParams(dimension_semantics=sem, vmem_limit_bytes=VMEM_LIMIT_BYTES)


def _lower_bound(lbp):
    e = jnp.exp(lbp - jnp.max(lbp, axis=0, keepdims=True))
    return e[0:1] / jnp.sum(e, axis=0, keepdims=True)


class _Projection:
    def __init__(self, xn, lbp, w_ref):
        self.xn = xn
        self.w_ref = w_ref
        self.lb = _lower_bound(lbp)

    def cols(self, base, part):
        return _dot(self.xn[...], self.w_ref[:, base + part.start:base + part.stop])

    def q_att(self, part=slice(0, D_ATT)):
        return self.cols(C_Q, part) * (HEAD_DIM ** -0.5)

    def kv(self):
        return self.cols(C_KV, slice(0, 2 * D_KV))

    def q_hg(self, part=slice(0, D_HG)):
        return self.cols(C_HQ, part) * (HG_DK ** -0.5)

    def forget(self, part=slice(0, D_HG)):
        lb = self.lb[:, part]
        f = lb + (1.0 - lb) * jax.nn.sigmoid(self.cols(C_HF, part))
        return 1.0 - f, jnp.log(f)

    def i_hg(self, part=slice(0, D_HG)):
        return self.cols(C_HI, part)

    def swish_gate(self, part=slice(0, D_HG)):
        g = self.cols(C_HGATE, part)
        return g * jax.nn.sigmoid(g)

    def branch_gate(self, base, part=slice(0, D_MODEL)):
        return jax.nn.sigmoid(self.cols(base, part))


class _Attention:
    def __init__(self, sink_ref, q_ref, kv, first, o_ref):
        self.sink_ref, self.q_ref, self.first, self.o_ref = sink_ref, q_ref, first, o_ref
        self.nsub = q_ref.shape[0] // WINDOW
        self.nk = 2 * WINDOW + N_META
        self.meta0 = WINDOW + q_ref.shape[0]
        lane = lax.broadcasted_iota(jnp.int32, (kv.shape[0], D_KV), 1)
        low = lane < HEAD_DIM
        k = kv[:, :D_KV]
        ksw = pltpu.roll(k, HEAD_DIM, axis=1)
        self.kboth = (jnp.where(low, k, ksw).astype(BF16), jnp.where(low, ksw, k).astype(BF16))
        qlane = lax.broadcasted_iota(jnp.int32, (1, 128), 1)
        self.keep = (jnp.where(qlane < HEAD_DIM, 1.0, 0.0).astype(BF16),
                     jnp.where(qlane < HEAD_DIM, 0.0, 1.0).astype(BF16))
        r = lax.broadcasted_iota(jnp.int32, (D_KV, D_KV), 0)
        c = lax.broadcasted_iota(jnp.int32, (D_KV, D_KV), 1)
        eye = jnp.where(r == c, 1.0, 0.0).astype(BF16)
        self.vt = _dot_nt(eye, kv[:, D_KV:].astype(BF16)).astype(BF16)
        self.ones = jnp.ones((16, self.nk), BF16)
        self.key = lax.broadcasted_iota(jnp.int32, (self.nk, 2 * WINDOW), 0)
        col = lax.broadcasted_iota(jnp.int32, (self.nk, 2 * WINDOW), 1)
        self.qry = jnp.bitwise_and(col, WINDOW - 1)
        self.first_head = lax.broadcasted_iota(jnp.int32, (1, 2 * WINDOW), 1) < WINDOW
        self.scores = {}

    def issue_scores(self, i):
        r0 = i * WINDOW
        out = []
        for p in range(Q_HEADS // 2):
            kb = self.kboth[(2 * p) // (Q_HEADS // KV_HEADS)]
            kmat = jnp.concatenate([kb[r0:r0 + 2 * WINDOW], kb[self.meta0:]], axis=0)
            qp = self.q_ref[r0:r0 + WINDOW, p * 128:(p + 1) * 128]
            q2 = jnp.concatenate([qp * self.keep[0], qp * self.keep[1]], axis=0)
            out.append(_dot_nt(kmat, q2))
        self.scores[i] = out

    def finish(self, i):
        r0 = i * WINDOW
        lo = jnp.where(self.first, WINDOW - 1, self.qry) if i == 0 else self.qry
        mask = jnp.logical_and(self.key > lo, self.key <= self.qry + WINDOW)
        mask = jnp.logical_or(mask, self.key >= 2 * WINDOW)
        vaug = []
        for h in range(KV_HEADS):
            vth = self.vt[h * HEAD_DIM:(h + 1) * HEAD_DIM]
            vaug.append(jnp.concatenate(
                [jnp.concatenate([vth[:, r0:r0 + 2 * WINDOW], vth[:, self.meta0:]], axis=1),
                 self.ones], axis=0))
        outs = []
        for p, raw in enumerate(self.scores.pop(i)):
            h = (2 * p) // (Q_HEADS // KV_HEADS)
            s = jnp.where(mask, raw, -jnp.inf)
            sk = jnp.where(self.first_head, self.sink_ref[2 * p], self.sink_ref[2 * p + 1])
            m = jnp.maximum(jnp.max(s, axis=0, keepdims=True), sk)
            e = jnp.exp(s - m).astype(BF16)
            oa = _dot(vaug[h], e)
            o = oa[:HEAD_DIM] / (oa[HEAD_DIM:HEAD_DIM + 1] + jnp.exp(sk - m))
            outs += [o[:, :WINDOW], o[:, WINDOW:]]
        self.o_ref[r0:r0 + WINDOW, :] = jnp.concatenate(outs, axis=0).T.astype(self.o_ref.dtype)


def _cumsum_rows(x):
    t = x.shape[0]
    r = lax.broadcasted_iota(jnp.int32, (t, t), 0)
    c = lax.broadcasted_iota(jnp.int32, (t, t), 1)
    tri = jnp.where(r >= c, 1.0, 0.0).astype(BF16)
    hi = x.astype(BF16)
    r1 = x - hi.astype(F32)
    mid = r1.astype(BF16)
    lo = (r1 - mid.astype(F32)).astype(BF16)
    return _dot(tri, hi) + _dot(tri, mid) + _dot(tri, lo)


def _hg_out(o, gate, nw):
    parts = []
    for h in range(HG_HEADS):
        oh = o[:, h * HG_DV:(h + 1) * HG_DV]
        parts.append(oh * lax.rsqrt(jnp.mean(oh * oh, axis=-1, keepdims=True) + EPS) * nw)
    return jnp.concatenate(parts, axis=1) * gate


_HG_SLICES = [slice(h * HG_DK, (h + 1) * HG_DK) for h in range(HG_HEADS)]


def _state_update(iv, kd):
    return jnp.concatenate([_dot_tn(iv[:, hs], kd[:, hs]) for hs in _HG_SLICES], axis=1)


def _meta_state(mk, mlogf, mi):
    b = _cumsum_rows(mlogf)
    kd = (mk * jnp.exp(b[-1:] - b)).astype(BF16)
    return _state_update(mi.astype(BF16), kd)


class _Hgrn:
    def __init__(self, q_ref, k_ref, logf_ref, i_ref, g_ref, nw, state, o_ref):
        self.q_ref, self.k_ref, self.logf_ref, self.i_ref, self.g_ref = q_ref, k_ref, logf_ref, i_ref, g_ref
        self.nw, self.state, self.o_ref = nw, state, o_ref
        t = HG_CHUNK
        self.nchunk = q_ref.shape[0] // t
        r = lax.broadcasted_iota(jnp.int32, (t, t), 0)
        c = lax.broadcasted_iota(jnp.int32, (t, t), 1)
        self.causal = r >= c
        self.cums, self.terms, self.entering = {}, {}, {}

    def rows(self, ci):
        return slice(ci * HG_CHUNK, (ci + 1) * HG_CHUNK)

    def issue_cumsum(self, ci):
        self.cums[ci] = _cumsum_rows(self.logf_ref[self.rows(ci), :])

    def issue_local(self, ci):
        rows = self.rows(ci)
        b = self.cums.pop(ci)
        bl = b[-1:]
        k = self.k_ref[rows, :].astype(F32)
        qe = (self.q_ref[rows, :].astype(F32) * jnp.exp(b)).astype(BF16)
        ke = (k * jnp.exp(-b)).astype(BF16)
        kd = (k * jnp.exp(bl - b)).astype(BF16)
        iv = self.i_ref[rows, :].astype(BF16)
        a = [jnp.where(self.causal, _dot_nt(qe[:, hs], ke[:, hs]), 0.0).astype(BF16) for hs in _HG_SLICES]
        self.terms[ci] = (qe, iv, a)
        self.entering[ci] = self.state.astype(BF16)
        self.state = self.state * jnp.exp(bl) + _state_update(iv, kd)

    def finish(self, ci):
        rows = self.rows(ci)
        qe, iv, a = self.terms.pop(ci)
        ent = self.entering.pop(ci)
        outs = [_dot(a[h], iv[:, hs]) + _dot_nt(qe[:, hs], ent[:, hs]) for h, hs in enumerate(_HG_SLICES)]
        o = _hg_out(jnp.concatenate(outs, axis=1), self.g_ref[rows, :].astype(F32), self.nw)
        self.o_ref[rows, :] = o.astype(self.o_ref.dtype)


def _load_as_bf16(w_hbm, w_vmem, stage, sem):
    rows = stage.shape[1]
    nslab = w_hbm.shape[0] // rows

    def slab_copy(c):
        return pltpu.make_async_copy(w_hbm.at[pl.ds(c * rows, rows), :], stage.at[c % 2], sem.at[c % 2])

    slab_copy(0).start()
    for c in range(nslab):
        if c + 1 < nslab:
            slab_copy(c + 1).start()
        slab_copy(c).wait()
        w_vmem[pl.ds(c * rows, rows), :] = stage[c % 2].astype(w_vmem.dtype)


def _mixer_kernel(sink_ref, x0_ref, xnext_ref, xs_ref, xm_ref, g_ref, lbp_ref, w_hbm, nw_ref,
                  wa32, wb32, wo32, wup32, wdn32,
                  att_ref, hg_ref, sga_ref, sgb_ref, sfin_ref, lastkv_ref,
                  qs_ref, kvs_ref, hqs_ref, hks_ref, his_ref, hgs_ref, sgas_ref, sgbs_ref,
                  wa16, wb16, wo16, wup16, wdn16,
                  w_ref, stage, sem, xn_s, zq, zkv, zhq, zhk, zlogf, zhi, zhg, kvm_ref, st_ref, mst_ref,
                  *, tiles_per_seq):
    s = pl.program_id(0)
    t = xnext_ref.shape[0]

    @pl.when(s == 0)
    def _():
        _load_as_bf16(w_hbm, w_ref, stage, sem)
        xn_s[...] = _rmsnorm(x0_ref[...], g_ref[...]).astype(xn_s.dtype)
        for ref in (zq, zkv, zhq, zhk, zlogf, zhi, zhg, st_ref):
            ref[...] = jnp.zeros(ref.shape, ref.dtype)
        nb = xs_ref.shape[0]
        small = jnp.concatenate([xs_ref[...], xm_ref[...]], axis=0)
        p = _Projection(_rmsnorm(small, g_ref[...]).astype(BF16), lbp_ref[...], w_ref)
        qs_ref[...] = p.q_att().astype(qs_ref.dtype)
        kv = p.kv()
        kvs_ref[...] = kv
        kvm_ref[...] = kv[nb:]
        hqs_ref[...] = p.q_hg()
        k, logf = p.forget()
        hks_ref[...] = k
        iv = p.i_hg()
        his_ref[...] = iv
        hgs_ref[...] = p.swish_gate()
        sgas_ref[...] = p.branch_gate(C_GA).astype(sgas_ref.dtype)
        sgbs_ref[...] = p.branch_gate(C_GB).astype(sgbs_ref.dtype)
        mst_ref[...] = _meta_state(k[nb:], logf[nb:], iv[nb:])

    first = lax.rem(jnp.maximum(s - 1, 0), tiles_per_seq) == 0

    att = _Attention(sink_ref, zq, jnp.concatenate([zkv[...], kvm_ref[...]], axis=0), first, att_ref)
    last = zkv[t:t + WINDOW, :]
    lastkv_ref[...] = last
    zkv[0:WINDOW, :] = last
    entering = jnp.where(first, mst_ref[...], st_ref[...])
    hg = _Hgrn(zhq, zhk, zlogf, zhi, zhg, nw_ref[...], entering, hg_ref)
    proj = _Projection(xn_s, lbp_ref[...], w_ref)

    def gate_a(p):
        sga_ref[:, p] = proj.branch_gate(C_GA, p).astype(sga_ref.dtype)

    def gate_b(p):
        sgb_ref[:, p] = proj.branch_gate(C_GB, p).astype(sgb_ref.dtype)

    def new_kv(_):
        zkv[WINDOW:WINDOW + t, :] = proj.kv()

    def new_forget(p):
        k_new, logf_new = proj.forget(p)
        zhk[:, p] = k_new.astype(zhk.dtype)
        zlogf[:, p] = logf_new

    def new_q(p):
        zq[:, p] = proj.q_att(p).astype(zq.dtype)

    def new_hq(p):
        zhq[:, p] = proj.q_hg(p).astype(zhq.dtype)

    def new_hi(p):
        zhi[:, p] = proj.i_hg(p).astype(zhi.dtype)

    def new_gate(p):
        zhg[:, p] = proj.swish_gate(p).astype(zhg.dtype)

    def parts(n):
        return [slice(c, c + PROJ_GROUP) for c in range(0, n, PROJ_GROUP)]

    groups = ([(gate_a, p) for p in parts(D_MODEL)] + [(gate_b, p) for p in parts(D_MODEL)]
              + [(new_kv, None)] + [(new_q, p) for p in parts(D_ATT)]
              + [(new_forget, p) for p in parts(D_HG)] + [(new_hq, p) for p in parts(D_HG)]
              + [(new_hi, p) for p in parts(D_HG)] + [(new_gate, p) for p in parts(D_HG)])

    def save_state():
        st_ref[...] = hg.state
        for h, hs in enumerate(_HG_SLICES):
            sfin_ref[h] = hg.state[:, hs].T

    qk, cs, loc, fin, out = att.issue_scores, hg.issue_cumsum, hg.issue_local, att.finish, hg.finish
    mixer_work = {
        0: [(qk, 0), (cs, 0), (cs, 1), (cs, 2), (cs, 3)],
        1: [(cs, 4), (cs, 5), (loc, 0)],
        2: [(cs, 6), (cs, 7), (loc, 1)],
        3: [(qk, 1), (loc, 2)],
        4: [(fin, 0), (loc, 3)],
        5: [(out, 0), (loc, 4)],
        6: [(qk, 2), (out, 1), (loc, 5)],
        7: [(fin, 1), (out, 2), (loc, 6)],
        8: [(qk, 3), (out, 3), (loc, 7), (save_state,)],
        9: [(out, 4)],
        10: [(fin, 2), (out, 5)],
        11: [(out, 6)],
        12: [(out, 7)],
        13: [(fin, 3)],
    }
    for slot, (fn, p) in enumerate(groups):
        fn(p)
        for item in mixer_work.get(slot, []):
            item[0](*item[1:])
    xn_s[...] = _rmsnorm(xnext_ref[...], g_ref[...]).astype(xn_s.dtype)
    for src, dst in ((wa32, wa16), (wb32, wb16), (wo32, wo16), (wup32, wup16), (wdn32, wdn16)):
        dst[...] = src[...].astype(dst.dtype)


def _mixers(sinks, x, x_sample, x_meta, g, lb_param, w_f32, hg_norm, later_weights, bsz):
    n = x.shape[0]
    t = MIX_ROWS
    nt = n // t
    per_seq = nt // bsz
    rows = x_sample.shape[0] + x_meta.shape[0]

    def this_tile(c):
        return pl.BlockSpec((t, c), lambda s: (jnp.minimum(s, nt - 1), 0))

    def prev_tile(c):
        return pl.BlockSpec((t, c), lambda s: (jnp.maximum(s - 1, 0), 0))

    def prev_seq(shape):
        return pl.BlockSpec((None,) + shape,
                            lambda s: (jnp.maximum(s - 1, 0) // per_seq,) + (0,) * len(shape))

    def small(c):
        return pl.BlockSpec((rows, c), lambda s: (0, 0))

    first_tile = pl.BlockSpec((t, D_MODEL), lambda s: (0, 0), pipeline_mode=pl.Buffered(1))
    next_tile = pl.BlockSpec((t, D_MODEL), lambda s: (jnp.minimum(s + 1, nt - 1), 0))
    small_widths = (D_ATT, 2 * D_KV, D_HG, D_HG, D_HG, D_HG, D_MODEL, D_MODEL)
    small_dtypes = (BF16, F32, F32, F32, F32, F32, BF16, BF16)

    def row_block(wt):
        return pl.BlockSpec((wt.shape[0] // nt, wt.shape[1]), lambda s: (jnp.minimum(s, nt - 1), 0))

    return pl.pallas_call(
        functools.partial(_mixer_kernel, tiles_per_seq=per_seq),
        grid=(nt + 1,),
        in_specs=[pl.BlockSpec(memory_space=pltpu.SMEM), first_tile, next_tile,
                  _resident(x_sample.shape), _resident(x_meta.shape), _resident((1, D_MODEL)),
                  _resident(lb_param.shape), pl.BlockSpec(memory_space=pl.ANY), _resident((1, HG_DV))]
                 + [row_block(wt) for wt in later_weights],
        out_specs=[prev_tile(D_ATT), prev_tile(D_HG), this_tile(D_MODEL), this_tile(D_MODEL),
                   prev_seq((HG_HEADS, HG_DK, HG_DV)), prev_seq((WINDOW, 2 * D_KV))]
                  + [small(c) for c in small_widths] + [row_block(wt) for wt in later_weights],
        out_shape=[jax.ShapeDtypeStruct((n, D_ATT), BF16), jax.ShapeDtypeStruct((n, D_HG), BF16),
                   jax.ShapeDtypeStruct((n, D_MODEL), BF16), jax.ShapeDtypeStruct((n, D_MODEL), BF16),
                   jax.ShapeDtypeStruct((bsz, HG_HEADS, HG_DK, HG_DV), F32),
                   jax.ShapeDtypeStruct((bsz, WINDOW, 2 * D_KV), F32)]
                  + [jax.ShapeDtypeStruct((rows, c), d) for c, d in zip(small_widths, small_dtypes)]
                  + [jax.ShapeDtypeStruct(wt.shape, BF16) for wt in later_weights],
        scratch_shapes=[pltpu.VMEM((D_MODEL, D_IN), BF16),
                        pltpu.VMEM((2, WEIGHT_SLAB_ROWS, D_IN), F32),
                        pltpu.SemaphoreType.DMA((2,)),
                        pltpu.VMEM((t, D_MODEL), BF16),
                        pltpu.VMEM((t, D_ATT), BF16), pltpu.VMEM((WINDOW + t, 2 * D_KV), F32),
                        pltpu.VMEM((t, D_HG), BF16), pltpu.VMEM((t, D_HG), BF16),
                        pltpu.VMEM((t, D_HG), F32), pltpu.VMEM((t, D_HG), BF16),
                        pltpu.VMEM((t, D_HG), BF16),
                        pltpu.VMEM((N_META, 2 * D_KV), F32),
                        pltpu.VMEM((HG_DV, D_HG), F32),
                        pltpu.VMEM((HG_DV, D_HG), F32)],
        compiler_params=_params("arbitrary"),
        name="mixers",
    )(sinks, x, x, x_sample, x_meta, g, lb_param, w_f32, hg_norm, *later_weights)


def _merge_ffn_rows(x, att, hg, sga, sgb, wa_ref, wb_ref, wo_ref, ln_ffn, wup_ref, wdn_ref, ln_f):
    ya = _dot(att.astype(BF16), wa_ref[...])
    yb = _dot(hg.astype(BF16), wb_ref[...])
    mix = sga.astype(F32) * ya + sgb.astype(F32) * yb
    h1 = x + _dot(mix.astype(BF16), wo_ref[...])
    xn = _rmsnorm(h1, ln_ffn).astype(BF16)
    acc = jnp.zeros_like(h1)
    step = 1024
    for c in range(D_FF // step):
        u = jnp.maximum(_dot(xn, wup_ref[:, c * step:(c + 1) * step]), 0.0)
        acc = acc + _dot((u * u).astype(BF16), wdn_ref[c * step:(c + 1) * step, :])
    return _rmsnorm(h1 + acc, ln_f)


def _merge_ffn_kernel(x_ref, att_ref, hg_ref, sga_ref, sgb_ref,
                      xs_ref, atts_ref, hgs_ref, sgas_ref, sgbs_ref,
                      wa_ref, wb_ref, wo_ref, wup_ref, wdn_ref, lnffn_ref, lnf_ref,
                      y_ref, ys_ref, *, prompt_steps):
    i = pl.program_id(0)
    weights = (wa_ref, wb_ref, wo_ref, lnffn_ref[...], wup_ref, wdn_ref, lnf_ref[...])

    @pl.when(i < prompt_steps)
    def _():
        y_ref[...] = _merge_ffn_rows(x_ref[...], att_ref[...], hg_ref[...], sga_ref[...], sgb_ref[...],
                                     *weights)

    @pl.when(i == prompt_steps)
    def _():
        ys_ref[...] = _merge_ffn_rows(xs_ref[...], atts_ref[...], hgs_ref[...], sgas_ref[...],
                                      sgbs_ref[...], *weights)


def _merge_ffn(x, att, hg, sga, sgb, xs, att_s, hg_s, sga_s, sgb_s, wa, wb, wo, ln_ffn, w_up, w_down, ln_f):
    n = x.shape[0]
    nb = xs.shape[0]
    rows = MERGE_ROWS
    nt = n // rows

    def blk(c):
        return pl.BlockSpec((rows, c), lambda i: (jnp.minimum(i, nt - 1), 0))

    def sample(c):
        return pl.BlockSpec((nb, c), lambda i: (0, 0))

    return pl.pallas_call(
        functools.partial(_merge_ffn_kernel, prompt_steps=nt),
        grid=(nt + 1,),
        in_specs=[blk(D_MODEL), blk(D_ATT), blk(D_HG), blk(D_MODEL), blk(D_MODEL),
                  sample(D_MODEL), sample(D_ATT), sample(D_HG), sample(D_MODEL), sample(D_MODEL),
                  _resident(wa.shape), _resident(wb.shape), _resident(wo.shape),
                  _resident(w_up.shape), _resident(w_down.shape),
                  _resident((1, D_MODEL)), _resident((1, D_MODEL))],
        out_specs=[blk(D_MODEL), sample(D_MODEL)],
        out_shape=[jax.ShapeDtypeStruct((n, D_MODEL), F32), jax.ShapeDtypeStruct((nb, D_MODEL), F32)],
        compiler_params=_params("arbitrary"),
        name="merge_ffn",
    )(x, att, hg, sga, sgb, xs, att_s, hg_s, sga_s, sgb_s, wa, wb, wo, w_up, w_down, ln_ffn, ln_f)


def _column(row, eye):
    return jnp.sum(jnp.where(eye, row, 0.0), axis=1, keepdims=True)


def _sample_attn_kernel(sink_ref, qm_ref, kvn_ref, ckt_ref, cvt_ref, mk_ref, mv_ref,
                        o_ref, nkt_ref, nvt_ref):
    nb = qm_ref.shape[0]
    sk = sink_ref[...]
    newest = lax.broadcasted_iota(jnp.int32, (D_KV, WINDOW), 1) == WINDOW - 1
    kvn = kvn_ref[...]
    hi = kvn.astype(BF16)
    rest = kvn - hi.astype(F32)
    mid = rest.astype(BF16)
    lo = (rest - mid.astype(F32)).astype(BF16)
    r = lax.broadcasted_iota(jnp.int32, (2 * D_KV, 2 * D_KV), 0)
    c = lax.broadcasted_iota(jnp.int32, (2 * D_KV, 2 * D_KV), 1)
    eye = jnp.where(r == c, 1.0, 0.0).astype(BF16)
    pieces_t = _dot_nt(eye, jnp.concatenate([hi, mid, lo], axis=0)).astype(BF16)
    pr = lax.broadcasted_iota(jnp.int32, (3 * nb, WINDOW), 0)
    pc = lax.broadcasted_iota(jnp.int32, (3 * nb, WINDOW), 1)
    for b in range(nb):
        pick = jnp.logical_and(lax.rem(pr, nb) == b, pc == WINDOW - 1)
        new_cols = _dot(pieces_t, jnp.where(pick, 1.0, 0.0).astype(BF16))
        nkt_ref[b] = jnp.where(newest, new_cols[:D_KV], pltpu.roll(ckt_ref[b], WINDOW - 1, axis=1))
        nvt_ref[b] = jnp.where(newest, new_cols[D_KV:], pltpu.roll(cvt_ref[b], WINDOW - 1, axis=1))
    scores = [(_dot(qm_ref[b], nkt_ref[b].astype(BF16)),
               _dot_nt(qm_ref[b], mk_ref[b].astype(BF16)))
              for b in range(nb)]
    for b, (s_w, s_m) in enumerate(scores):
        m = jnp.maximum(jnp.maximum(jnp.max(s_w, axis=1, keepdims=True),
                                    jnp.max(s_m, axis=1, keepdims=True)), sk)
        e_w = jnp.exp(s_w - m)
        e_m = jnp.exp(s_m - m)
        l = (jnp.sum(e_w, axis=1, keepdims=True) + jnp.sum(e_m, axis=1, keepdims=True)
             + jnp.exp(sk - m))
        o = (_dot_nt(e_w.astype(BF16), nvt_ref[b].astype(BF16))
             + _dot(e_m.astype(BF16), mv_ref[b].astype(BF16)))
        o_ref[b] = o / l


def _sample_attention(sinks_col, qm, kv_new, ck, cv, mk, mv):
    nb = ck.shape[0]
    g = SAMPLE_ATT_GROUP

    def blk3(a, c):
        return pl.BlockSpec((g, a, c), lambda i: (i, 0, 0))

    return pl.pallas_call(
        _sample_attn_kernel,
        grid=(nb // g,),
        in_specs=[_resident((Q_HEADS, 1)), blk3(Q_HEADS, D_KV),
                  pl.BlockSpec((g, 2 * D_KV), lambda i: (i, 0)),
                  blk3(D_KV, WINDOW), blk3(D_KV, WINDOW), blk3(N_META, D_KV), blk3(N_META, D_KV)],
        out_specs=[blk3(Q_HEADS, D_KV), blk3(D_KV, WINDOW), blk3(D_KV, WINDOW)],
        out_shape=[jax.ShapeDtypeStruct((nb, Q_HEADS, D_KV), F32),
                   jax.ShapeDtypeStruct((nb, D_KV, WINDOW), F32),
                   jax.ShapeDtypeStruct((nb, D_KV, WINDOW), F32)],
        compiler_params=_params("parallel"),
        name="sample_attn",
    )(sinks_col, qm, kv_new, ck, cv, mk, mv)


def _sample_hgrn_kernel(nw_ref, q_ref, k_ref, i_ref, g_ref, s_ref, o_ref, snew_ref, acc_ref):
    k = k_ref[...]
    q = q_ref[...]
    iv = i_ref[...]
    r = lax.broadcasted_iota(jnp.int32, (HG_DK, HG_DK), 0)
    c = lax.broadcasted_iota(jnp.int32, (HG_DK, HG_DK), 1)
    eye = r == c

    def column(row):
        return jnp.sum(jnp.where(eye, row, 0.0), axis=1, keepdims=True)

    for b in range(s_ref.shape[0]):
        for h, hs in enumerate(_HG_SLICES):
            kc = column(k[b:b + 1, hs])
            qc = column(q[b:b + 1, hs])
            s_old = s_ref[b, h]
            s = s_old + kc * (iv[b:b + 1, hs] - s_old)
            snew_ref[b, h] = s
            acc_ref[b:b + 1, hs] = jnp.sum(qc * s, axis=0, keepdims=True)
    o_ref[...] = _hg_out(acc_ref[...], g_ref[...], nw_ref[...])


def _sample_hgrn(hg_norm, hq, hk, hi, hgate, state):
    nb = state.shape[0]
    g = SAMPLE_HG_GROUP
    row = pl.BlockSpec((g, D_HG), lambda i: (i, 0))
    sblk = pl.BlockSpec((g, HG_HEADS, HG_DK, HG_DV), lambda i: (i, 0, 0, 0))
    return pl.pallas_call(
        _sample_hgrn_kernel,
        grid=(nb // g,),
        in_specs=[_resident((1, HG_DV)), row, row, row, row, sblk],
        out_specs=[row, sblk],
        out_shape=[jax.ShapeDtypeStruct((nb, D_HG), F32),
                   jax.ShapeDtypeStruct(state.shape, F32)],
        scratch_shapes=[pltpu.VMEM((g, D_HG), F32)],
        compiler_params=_params("parallel"),
        name="sample_hgrn",
    )(hg_norm, hq, hk, hi, hgate, state)


def kernel(x_prompt, x_sample, cache_k, cache_v, cache_meta_k, cache_meta_v, state_hgrn, meta,
           w_in, sinks, lb_param, hg_norm, w_att_out, w_hg_out, w_o, ln_mix, ln_ffn, w_up,
           w_down, ln_f):
    bsz, seq, _ = x_prompt.shape
    nb = x_sample.shape[0]
    ln_mix2 = ln_mix.reshape(1, D_MODEL)
    ln_ffn2 = ln_ffn.reshape(1, D_MODEL)
    ln_f2 = ln_f.reshape(1, D_MODEL)
    nw = hg_norm.reshape(1, HG_DV)

    xs = x_sample.reshape(nb, D_MODEL)

    xp = x_prompt.reshape(bsz * seq, D_MODEL)
    (att_p, hg_p, sga_p, sgb_p, state_p, lastkv_p,
     q_s, kv_s, hq_s, hk_s, hi_s, hgate_s, sga_s, sgb_s,
     wa, wb, wo, wup, wdn) = _mixers(
        sinks.reshape(Q_HEADS), xp, xs, meta, ln_mix2, lb_param, w_in[0], nw,
        (w_att_out[0], w_hg_out[0], w_o[0], w_up[0], w_down[0]), bsz)

    grp = Q_HEADS // KV_HEADS
    rows = nb + N_META
    qs4 = q_s.reshape(rows, KV_HEADS, grp, 1, HEAD_DIM)
    sel = jnp.eye(KV_HEADS, dtype=BF16).reshape(1, KV_HEADS, 1, KV_HEADS, 1)
    qm = (qs4 * sel).reshape(rows, Q_HEADS, D_KV)

    def window_t(c):
        return jnp.swapaxes(c[0].reshape(nb, WINDOW, D_KV), 1, 2)

    o_s, nkt_s, nvt_s = _sample_attention(
        sinks.reshape(Q_HEADS, 1), qm, kv_s, window_t(cache_k), window_t(cache_v),
        cache_meta_k[0].reshape(nb, N_META, D_KV), cache_meta_v[0].reshape(nb, N_META, D_KV))
    o5 = o_s.reshape(nb, KV_HEADS, grp, KV_HEADS, HEAD_DIM)
    att_s = jnp.stack([o5[:, h, :, h, :] for h in range(KV_HEADS)], axis=1).reshape(nb, D_ATT)
    hg_s, state_s = _sample_hgrn(nw, hq_s, hk_s, hi_s, hgate_s, state_hgrn[0])

    y_p, y_s = _merge_ffn(xp, att_p, hg_p, sga_p, sgb_p, xs, att_s, hg_s, sga_s, sgb_s,
                          wa, wb, wo, ln_ffn2, wup, wdn, ln_f2)

    kv5 = lastkv_p.reshape(bsz, WINDOW, 2, KV_HEADS, HEAD_DIM)
    meta5 = jnp.broadcast_to(kv_s[nb:].reshape(1, N_META, 2, KV_HEADS, HEAD_DIM),
                             (bsz, N_META, 2, KV_HEADS, HEAD_DIM))
    return (y_p.reshape(bsz, seq, D_MODEL),
            y_s.reshape(nb, 1, D_MODEL),
            kv5[None, :, :, 0],
            kv5[None, :, :, 1],
            meta5[None, :, :, 0],
            meta5[None, :, :, 1],
            state_p[None],
            jnp.swapaxes(nkt_s, 1, 2).reshape(1, nb, WINDOW, KV_HEADS, HEAD_DIM),
            jnp.swapaxes(nvt_s, 1, 2).reshape(1, nb, WINDOW, KV_HEADS, HEAD_DIM),
            state_s[None])
```

```python
import functools

import jax
import jax.numpy as jnp
from jax import lax
from jax.experimental import pallas as pl
from jax.experimental.pallas import tpu as pltpu

F32 = jnp.float32
BF16 = jnp.bfloat16

D_MODEL = 1024
N_META = 16
WINDOW = 128
HEAD_DIM = 64
Q_HEADS = 8
KV_HEADS = 2
D_ATT = Q_HEADS * HEAD_DIM
D_KV = KV_HEADS * HEAD_DIM
HG_HEADS = 4
HG_DK = 128
HG_DV = 128
D_HG = HG_HEADS * HG_DK
HG_CHUNK = 64
D_FF = 4 * D_MODEL
EPS = 1e-6
C_Q = 0
C_KV = C_Q + D_ATT
C_HQ = C_KV + 2 * D_KV
C_HF = C_HQ + D_HG
C_HI = C_HF + D_HG
C_HGATE = C_HI + D_HG
C_GA = C_HGATE + D_HG
C_GB = C_GA + D_MODEL
D_IN = C_GB + D_MODEL

VMEM_LIMIT_BYTES = 56 * 1024 * 1024
MIX_ROWS = 512
PROJ_GROUP = 256
WEIGHT_SLAB_ROWS = 64
MERGE_ROWS = 512
SAMPLE_ATT_GROUP = 16
SAMPLE_HG_GROUP = 8

_NT = (((1,), (1,)), ((), ()))
_TN = (((0,), (0,)), ((), ()))


def _dot(a, b):
    return jnp.dot(a, b, preferred_element_type=F32)


def _dot_nt(a, b):
    return lax.dot_general(a, b, _NT, preferred_element_type=F32)


def _dot_tn(a, b):
    return lax.dot_general(a, b, _TN, preferred_element_type=F32)


def _rmsnorm(x, g):
    return x * lax.rsqrt(jnp.mean(x * x, axis=-1, keepdims=True) + EPS) * g


def _resident(shape):
    return pl.BlockSpec(shape, lambda *_: (0,) * len(shape), pipeline_mode=pl.Buffered(1))


def _params(*sem):
    return pltpu.CompilerParams(dimension_semantics=sem, vmem_limit_bytes=VMEM_LIMIT_BYTES)


def _lower_bound(lbp):
    e = jnp.exp(lbp - jnp.max(lbp, axis=0, keepdims=True))
    return e[0:1] / jnp.sum(e, axis=0, keepdims=True)


class _Projection:
    def __init__(self, xn, lbp, w_ref):
        self.xn = xn
        self.w_ref = w_ref
        self.lb = _lower_bound(lbp)

    def cols(self, base, part):
        return _dot(self.xn[...], self.w_ref[:, base + part.start:base + part.stop])

    def q_att(self, part=slice(0, D_ATT)):
        return self.cols(C_Q, part) * (HEAD_DIM ** -0.5)

    def kv(self):
        return self.cols(C_KV, slice(0, 2 * D_KV))

    def q_hg(self, part=slice(0, D_HG)):
        return self.cols(C_HQ, part) * (HG_DK ** -0.5)

    def forget(self, part=slice(0, D_HG)):
        lb = self.lb[:, part]
        f = lb + (1.0 - lb) * jax.nn.sigmoid(self.cols(C_HF, part))
        return 1.0 - f, jnp.log(f)

    def i_hg(self, part=slice(0, D_HG)):
        return self.cols(C_HI, part)

    def swish_gate(self, part=slice(0, D_HG)):
        g = self.cols(C_HGATE, part)
        return g * jax.nn.sigmoid(g)

    def branch_gate(self, base, part=slice(0, D_MODEL)):
        return jax.nn.sigmoid(self.cols(base, part))


class _Attention:
    def __init__(self, sink_ref, q_ref, kv, first, o_ref):
        self.sink_ref, self.q_ref, self.first, self.o_ref = sink_ref, q_ref, first, o_ref
        self.nsub = q_ref.shape[0] // WINDOW
        self.nk = 2 * WINDOW + N_META
        self.meta0 = WINDOW + q_ref.shape[0]
        lane = lax.broadcasted_iota(jnp.int32, (kv.shape[0], D_KV), 1)
        low = lane < HEAD_DIM
        k = kv[:, :D_KV]
        ksw = pltpu.roll(k, HEAD_DIM, axis=1)
        self.kboth = (jnp.where(low, k, ksw).astype(BF16), jnp.where(low, ksw, k).astype(BF16))
        qlane = lax.broadcasted_iota(jnp.int32, (1, 128), 1)
        self.keep = (jnp.where(qlane < HEAD_DIM, 1.0, 0.0).astype(BF16),
                     jnp.where(qlane < HEAD_DIM, 0.0, 1.0).astype(BF16))
        r = lax.broadcasted_iota(jnp.int32, (D_KV, D_KV), 0)
        c = lax.broadcasted_iota(jnp.int32, (D_KV, D_KV), 1)
        eye = jnp.where(r == c, 1.0, 0.0).astype(BF16)
        self.vt = _dot_nt(eye, kv[:, D_KV:].astype(BF16)).astype(BF16)
        self.ones = jnp.ones((16, self.nk), BF16)
        self.key = lax.broadcasted_iota(jnp.int32, (self.nk, 2 * WINDOW), 0)
        col = lax.broadcasted_iota(jnp.int32, (self.nk, 2 * WINDOW), 1)
        self.qry = jnp.bitwise_and(col, WINDOW - 1)
        self.first_head = lax.broadcasted_iota(jnp.int32, (1, 2 * WINDOW), 1) < WINDOW
        self.scores = {}

    def issue_scores(self, i):
        r0 = i * WINDOW
        out = []
        for p in range(Q_HEADS // 2):
            kb = self.kboth[(2 * p) // (Q_HEADS // KV_HEADS)]
            kmat = jnp.concatenate([kb[r0:r0 + 2 * WINDOW], kb[self.meta0:]], axis=0)
            qp = self.q_ref[r0:r0 + WINDOW, p * 128:(p + 1) * 128]
            q2 = jnp.concatenate([qp * self.keep[0], qp * self.keep[1]], axis=0)
            out.append(_dot_nt(kmat, q2))
        self.scores[i] = out

    def finish(self, i):
        r0 = i * WINDOW
        lo = jnp.where(self.first, WINDOW - 1, self.qry) if i == 0 else self.qry
        mask = jnp.logical_and(self.key > lo, self.key <= self.qry + WINDOW)
        mask = jnp.logical_or(mask, self.key >= 2 * WINDOW)
        vaug = []
        for h in range(KV_HEADS):
            vth = self.vt[h * HEAD_DIM:(h + 1) * HEAD_DIM]
            vaug.append(jnp.concatenate(
                [jnp.concatenate([vth[:, r0:r0 + 2 * WINDOW], vth[:, self.meta0:]], axis=1),
                 self.ones], axis=0))
        outs = []
        for p, raw in enumerate(self.scores.pop(i)):
            h = (2 * p) // (Q_HEADS // KV_HEADS)
            s = jnp.where(mask, raw, -jnp.inf)
            sk = jnp.where(self.first_head, self.sink_ref[2 * p], self.sink_ref[2 * p + 1])
            m = jnp.maximum(jnp.max(s, axis=0, keepdims=True), sk)
            e = jnp.exp(s - m).astype(BF16)
            oa = _dot(vaug[h], e)
            o = oa[:HEAD_DIM] / (oa[HEAD_DIM:HEAD_DIM + 1] + jnp.exp(sk - m))
            outs += [o[:, :WINDOW], o[:, WINDOW:]]
        self.o_ref[r0:r0 + WINDOW, :] = jnp.concatenate(outs, axis=0).T.astype(self.o_ref.dtype)


def _cumsum_rows(x):
    t = x.shape[0]
    r = lax.broadcasted_iota(jnp.int32, (t, t), 0)
    c = lax.broadcasted_iota(jnp.int32, (t, t), 1)
    tri = jnp.where(r >= c, 1.0, 0.0).astype(BF16)
    hi = x.astype(BF16)
    r1 = x - hi.astype(F32)
    mid = r1.astype(BF16)
    lo = (r1 - mid.astype(F32)).astype(BF16)
    return _dot(tri, hi) + _dot(tri, mid) + _dot(tri, lo)


def _hg_out(o, gate, nw):
    parts = []
    for h in range(HG_HEADS):
        oh = o[:, h * HG_DV:(h + 1) * HG_DV]
        parts.append(oh * lax.rsqrt(jnp.mean(oh * oh, axis=-1, keepdims=True) + EPS) * nw)
    return jnp.concatenate(parts, axis=1) * gate


_HG_SLICES = [slice(h * HG_DK, (h + 1) * HG_DK) for h in range(HG_HEADS)]


def _state_update(iv, kd):
    return jnp.concatenate([_dot_tn(iv[:, hs], kd[:, hs]) for hs in _HG_SLICES], axis=1)


def _meta_state(mk, mlogf, mi):
    b = _cumsum_rows(mlogf)
    kd = (mk * jnp.exp(b[-1:] - b)).astype(BF16)
    return _state_update(mi.astype(BF16), kd)


class _Hgrn:
    def __init__(self, q_ref, k_ref, logf_ref, i_ref, g_ref, nw, state, o_ref):
        self.q_ref, self.k_ref, self.logf_ref, self.i_ref, self.g_ref = q_ref, k_ref, logf_ref, i_ref, g_ref
        self.nw, self.state, self.o_ref = nw, state, o_ref
        t = HG_CHUNK
        self.nchunk = q_ref.shape[0] // t
        r = lax.broadcasted_iota(jnp.int32, (t, t), 0)
        c = lax.broadcasted_iota(jnp.int32, (t, t), 1)
        self.causal = r >= c
        self.cums, self.terms, self.entering = {}, {}, {}

    def rows(self, ci):
        return slice(ci * HG_CHUNK, (ci + 1) * HG_CHUNK)

    def issue_cumsum(self, ci):
        self.cums[ci] = _cumsum_rows(self.logf_ref[self.rows(ci), :])

    def issue_local(self, ci):
        rows = self.rows(ci)
        b = self.cums.pop(ci)
        bl = b[-1:]
        k = self.k_ref[rows, :].astype(F32)
        qe = (self.q_ref[rows, :].astype(F32) * jnp.exp(b)).astype(BF16)
        ke = (k * jnp.exp(-b)).astype(BF16)
        kd = (k * jnp.exp(bl - b)).astype(BF16)
        iv = self.i_ref[rows, :].astype(BF16)
        a = [jnp.where(self.causal, _dot_nt(qe[:, hs], ke[:, hs]), 0.0).astype(BF16) for hs in _HG_SLICES]
        self.terms[ci] = (qe, iv, a)
        self.entering[ci] = self.state.astype(BF16)
        self.state = self.state * jnp.exp(bl) + _state_update(iv, kd)

    def finish(self, ci):
        rows = self.rows(ci)
        qe, iv, a = self.terms.pop(ci)
        ent = self.entering.pop(ci)
        outs = [_dot(a[h], iv[:, hs]) + _dot_nt(qe[:, hs], ent[:, hs]) for h, hs in enumerate(_HG_SLICES)]
        o = _hg_out(jnp.concatenate(outs, axis=1), self.g_ref[rows, :].astype(F32), self.nw)
        self.o_ref[rows, :] = o.astype(self.o_ref.dtype)


def _load_as_bf16(w_hbm, w_vmem, stage, sem):
    rows = stage.shape[1]
    nslab = w_hbm.shape[0] // rows

    def slab_copy(c):
        return pltpu.make_async_copy(w_hbm.at[pl.ds(c * rows, rows), :], stage.at[c % 2], sem.at[c % 2])

    slab_copy(0).start()
    for c in range(nslab):
        if c + 1 < nslab:
            slab_copy(c + 1).start()
        slab_copy(c).wait()
        w_vmem[pl.ds(c * rows, rows), :] = stage[c % 2].astype(w_vmem.dtype)


def _mixer_kernel(sink_ref, x0_ref, xnext_ref, xs_ref, xm_ref, g_ref, lbp_ref, w_hbm, nw_ref,
                  wa32, wb32, wo32, wup32, wdn32,
                  att_ref, hg_ref, sga_ref, sgb_ref, sfin_ref, lastkv_ref,
                  qs_ref, kvs_ref, hqs_ref, hks_ref, his_ref, hgs_ref, sgas_ref, sgbs_ref,
                  wa16, wb16, wo16, wup16, wdn16,
                  w_ref, stage, sem, xn_s, zq, zkv, zhq, zhk, zlogf, zhi, zhg, kvm_ref, st_ref, mst_ref,
                  *, tiles_per_seq):
    s = pl.program_id(0)
    t = xnext_ref.shape[0]

    @pl.when(s == 0)
    def _():
        _load_as_bf16(w_hbm, w_ref, stage, sem)
        xn_s[...] = _rmsnorm(x0_ref[...], g_ref[...]).astype(xn_s.dtype)
        for ref in (zq, zkv, zhq, zhk, zlogf, zhi, zhg, st_ref):
            ref[...] = jnp.zeros(ref.shape, ref.dtype)
        nb = xs_ref.shape[0]
        small = jnp.concatenate([xs_ref[...], xm_ref[...]], axis=0)
        p = _Projection(_rmsnorm(small, g_ref[...]).astype(BF16), lbp_ref[...], w_ref)
        qs_ref[...] = p.q_att().astype(qs_ref.dtype)
        kv = p.kv()
        kvs_ref[...] = kv
        kvm_ref[...] = kv[nb:]
        hqs_ref[...] = p.q_hg()
        k, logf = p.forget()
        hks_ref[...] = k
        iv = p.i_hg()
        his_ref[...] = iv
        hgs_ref[...] = p.swish_gate()
        sgas_ref[...] = p.branch_gate(C_GA).astype(sgas_ref.dtype)
        sgbs_ref[...] = p.branch_gate(C_GB).astype(sgbs_ref.dtype)
        mst_ref[...] = _meta_state(k[nb:], logf[nb:], iv[nb:])

    first = lax.rem(jnp.maximum(s - 1, 0), tiles_per_seq) == 0

    att = _Attention(sink_ref, zq, jnp.concatenate([zkv[...], kvm_ref[...]], axis=0), first, att_ref)
    last = zkv[t:t + WINDOW, :]
    lastkv_ref[...] = last
    zkv[0:WINDOW, :] = last
    entering = jnp.where(first, mst_ref[...], st_ref[...])
    hg = _Hgrn(zhq, zhk, zlogf, zhi, zhg, nw_ref[...], entering, hg_ref)
    proj = _Projection(xn_s, lbp_ref[...], w_ref)

    def gate_a(p):
        sga_ref[:, p] = proj.branch_gate(C_GA, p).astype(sga_ref.dtype)

    def gate_b(p):
        sgb_ref[:, p] = proj.branch_gate(C_GB, p).astype(sgb_ref.dtype)

    def new_kv(_):
        zkv[WINDOW:WINDOW + t, :] = proj.kv()

    def new_forget(p):
        k_new, logf_new = proj.forget(p)
        zhk[:, p] = k_new.astype(zhk.dtype)
        zlogf[:, p] = logf_new

    def new_q(p):
        zq[:, p] = proj.q_att(p).astype(zq.dtype)

    def new_hq(p):
        zhq[:, p] = proj.q_hg(p).astype(zhq.dtype)

    def new_hi(p):
        zhi[:, p] = proj.i_hg(p).astype(zhi.dtype)

    def new_gate(p):
        zhg[:, p] = proj.swish_gate(p).astype(zhg.dtype)

    def parts(n):
        return [slice(c, c + PROJ_GROUP) for c in range(0, n, PROJ_GROUP)]

    groups = ([(gate_a, p) for p in parts(D_MODEL)] + [(gate_b, p) for p in parts(D_MODEL)]
              + [(new_kv, None)] + [(new_q, p) for p in parts(D_ATT)]
              + [(new_forget, p) for p in parts(D_HG)] + [(new_hq, p) for p in parts(D_HG)]
              + [(new_hi, p) for p in parts(D_HG)] + [(new_gate, p) for p in parts(D_HG)])

    def save_state():
        st_ref[...] = hg.state
        for h, hs in enumerate(_HG_SLICES):
            sfin_ref[h] = hg.state[:, hs].T

    qk, cs, loc, fin, out = att.issue_scores, hg.issue_cumsum, hg.issue_local, att.finish, hg.finish
    mixer_work = {
        0: [(qk, 0), (cs, 0), (cs, 1), (cs, 2), (cs, 3)],
        1: [(cs, 4), (cs, 5), (loc, 0)],
        2: [(cs, 6), (cs, 7), (loc, 1)],
        3: [(qk, 1), (loc, 2)],
        4: [(fin, 0), (loc, 3)],
        5: [(out, 0), (loc, 4)],
        6: [(qk, 2), (out, 1), (loc, 5)],
        7: [(fin, 1), (out, 2), (loc, 6)],
        8: [(qk, 3), (out, 3), (loc, 7), (save_state,)],
        9: [(out, 4)],
        10: [(fin, 2), (out, 5)],
        11: [(out, 6)],
        12: [(out, 7)],
        13: [(fin, 3)],
    }
    for slot, (fn, p) in enumerate(groups):
        fn(p)
        for item in mixer_work.get(slot, []):
            item[0](*item[1:])
    xn_s[...] = _rmsnorm(xnext_ref[...], g_ref[...]).astype(xn_s.dtype)
    for src, dst in ((wa32, wa16), (wb32, wb16), (wo32, wo16), (wup32, wup16), (wdn32, wdn16)):
        dst[...] = src[...].astype(dst.dtype)


def _mixers(sinks, x, x_sample, x_meta, g, lb_param, w_f32, hg_norm, later_weights, bsz):
    n = x.shape[0]
    t = MIX_ROWS
    nt = n // t
    per_seq = nt // bsz
    rows = x_sample.shape[0] + x_meta.shape[0]

    def this_tile(c):
        return pl.BlockSpec((t, c), lambda s: (jnp.minimum(s, nt - 1), 0))

    def prev_tile(c):
        return pl.BlockSpec((t, c), lambda s: (jnp.maximum(s - 1, 0), 0))

    def prev_seq(shape):
        return pl.BlockSpec((None,) + shape,
                            lambda s: (jnp.maximum(s - 1, 0) // per_seq,) + (0,) * len(shape))

    def small(c):
        return pl.BlockSpec((rows, c), lambda s: (0, 0))

    first_tile = pl.BlockSpec((t, D_MODEL), lambda s: (0, 0), pipeline_mode=pl.Buffered(1))
    next_tile = pl.BlockSpec((t, D_MODEL), lambda s: (jnp.minimum(s + 1, nt - 1), 0))
    small_widths = (D_ATT, 2 * D_KV, D_HG, D_HG, D_HG, D_HG, D_MODEL, D_MODEL)
    small_dtypes = (BF16, F32, F32, F32, F32, F32, BF16, BF16)

    def row_block(wt):
        return pl.BlockSpec((wt.shape[0] // nt, wt.shape[1]), lambda s: (jnp.minimum(s, nt - 1), 0))

    return pl.pallas_call(
        functools.partial(_mixer_kernel, tiles_per_seq=per_seq),
        grid=(nt + 1,),
        in_specs=[pl.BlockSpec(memory_space=pltpu.SMEM), first_tile, next_tile,
                  _resident(x_sample.shape), _resident(x_meta.shape), _resident((1, D_MODEL)),
                  _resident(lb_param.shape), pl.BlockSpec(memory_space=pl.ANY), _resident((1, HG_DV))]
                 + [row_block(wt) for wt in later_weights],
        out_specs=[prev_tile(D_ATT), prev_tile(D_HG), this_tile(D_MODEL), this_tile(D_MODEL),
                   prev_seq((HG_HEADS, HG_DK, HG_DV)), prev_seq((WINDOW, 2 * D_KV))]
                  + [small(c) for c in small_widths] + [row_block(wt) for wt in later_weights],
        out_shape=[jax.ShapeDtypeStruct((n, D_ATT), BF16), jax.ShapeDtypeStruct((n, D_HG), BF16),
                   jax.ShapeDtypeStruct((n, D_MODEL), BF16), jax.ShapeDtypeStruct((n, D_MODEL), BF16),
                   jax.ShapeDtypeStruct((bsz, HG_HEADS, HG_DK, HG_DV), F32),
                   jax.ShapeDtypeStruct((bsz, WINDOW, 2 * D_KV), F32)]
                  + [jax.ShapeDtypeStruct((rows, c), d) for c, d in zip(small_widths, small_dtypes)]
                  + [jax.ShapeDtypeStruct(wt.shape, BF16) for wt in later_weights],
        scratch_shapes=[pltpu.VMEM((D_MODEL, D_IN), BF16),
                        pltpu.VMEM((2, WEIGHT_SLAB_ROWS, D_IN), F32),
                        pltpu.SemaphoreType.DMA((2,)),
                        pltpu.VMEM((t, D_MODEL), BF16),
                        pltpu.VMEM((t, D_ATT), BF16), pltpu.VMEM((WINDOW + t, 2 * D_KV), F32),
                        pltpu.VMEM((t, D_HG), BF16), pltpu.VMEM((t, D_HG), BF16),
                        pltpu.VMEM((t, D_HG), F32), pltpu.VMEM((t, D_HG), BF16),
                        pltpu.VMEM((t, D_HG), BF16),
                        pltpu.VMEM((N_META, 2 * D_KV), F32),
                        pltpu.VMEM((HG_DV, D_HG), F32),
                        pltpu.VMEM((HG_DV, D_HG), F32)],
        compiler_params=_params("arbitrary"),
        name="mixers",
    )(sinks, x, x, x_sample, x_meta, g, lb_param, w_f32, hg_norm, *later_weights)


def _merge_ffn_rows(x, att, hg, sga, sgb, wa_ref, wb_ref, wo_ref, ln_ffn, wup_ref, wdn_ref, ln_f):
    ya = _dot(att.astype(BF16), wa_ref[...])
    yb = _dot(hg.astype(BF16), wb_ref[...])
    mix = sga.astype(F32) * ya + sgb.astype(F32) * yb
    h1 = x + _dot(mix.astype(BF16), wo_ref[...])
    xn = _rmsnorm(h1, ln_ffn).astype(BF16)
    acc = jnp.zeros_like(h1)
    step = 1024
    for c in range(D_FF // step):
        u = jnp.maximum(_dot(xn, wup_ref[:, c * step:(c + 1) * step]), 0.0)
        acc = acc + _dot((u * u).astype(BF16), wdn_ref[c * step:(c + 1) * step, :])
    return _rmsnorm(h1 + acc, ln_f)


def _sample_hgrn_group(rows, nw, q_ref, k_ref, i_ref, g_ref, s_ref, snew_ref, hg_ref):
    k = k_ref[rows, :]
    q = q_ref[rows, :]
    iv = i_ref[rows, :]
    r = lax.broadcasted_iota(jnp.int32, (HG_DK, HG_DK), 0)
    c = lax.broadcasted_iota(jnp.int32, (HG_DK, HG_DK), 1)
    eye = r == c

    def column(row):
        return jnp.sum(jnp.where(eye, row, 0.0), axis=1, keepdims=True)

    outs = []
    for b in range(s_ref.shape[0]):
        heads = []
        for h, hs in enumerate(_HG_SLICES):
            kc = column(k[b:b + 1, hs])
            qc = column(q[b:b + 1, hs])
            s_old = s_ref[b, h]
            s = s_old + kc * (iv[b:b + 1, hs] - s_old)
            snew_ref[b, h] = s
            heads.append(jnp.sum(qc * s, axis=0, keepdims=True))
        outs.append(jnp.concatenate(heads, axis=1))
    hg_ref[rows, :] = _hg_out(jnp.concatenate(outs, axis=0), g_ref[rows, :], nw)


def _merge_ffn_kernel(x_ref, att_ref, hg_ref, sga_ref, sgb_ref,
                      xs_ref, atts_ref, sgas_ref, sgbs_ref,
                      nw_ref, hqs_ref, hks_ref, his_ref, hgates_ref, state_ref,
                      wa_ref, wb_ref, wo_ref, wup_ref, wdn_ref, lnffn_ref, lnf_ref,
                      y_ref, ys_ref, snew_ref, hgs, *, prompt_steps):
    i = pl.program_id(0)
    weights = (wa_ref, wb_ref, wo_ref, lnffn_ref[...], wup_ref, wdn_ref, lnf_ref[...])
    group = state_ref.shape[0]
    ngroups = xs_ref.shape[0] // group

    @pl.when(i < prompt_steps)
    def _():
        y_ref[...] = _merge_ffn_rows(x_ref[...], att_ref[...], hg_ref[...], sga_ref[...], sgb_ref[...],
                                     *weights)
        rows = pl.ds(pl.multiple_of(jnp.minimum(i, ngroups - 1) * group, group), group)
        _sample_hgrn_group(rows, nw_ref[...], hqs_ref, hks_ref, his_ref, hgates_ref,
                           state_ref, snew_ref, hgs)

    @pl.when(i == prompt_steps)
    def _():
        ys_ref[...] = _merge_ffn_rows(xs_ref[...], atts_ref[...], hgs[...], sgas_ref[...],
                                      sgbs_ref[...], *weights)


def _merge_ffn(x, att, hg, sga, sgb, xs, att_s, sga_s, sgb_s, hg_norm, hq_s, hk_s, hi_s, hgate_s, state,
               wa, wb, wo, ln_ffn, w_up, w_down, ln_f):
    n = x.shape[0]
    nb = xs.shape[0]
    rows = MERGE_ROWS
    nt = n // rows
    g = SAMPLE_HG_GROUP
    assert nb // g <= nt

    def blk(c):
        return pl.BlockSpec((rows, c), lambda i: (jnp.minimum(i, nt - 1), 0))

    def sample(c):
        return pl.BlockSpec((nb, c), lambda i: (0, 0))

    sblk = pl.BlockSpec((g, HG_HEADS, HG_DK, HG_DV), lambda i: (jnp.minimum(i, nb // g - 1), 0, 0, 0))
    return pl.pallas_call(
        functools.partial(_merge_ffn_kernel, prompt_steps=nt),
        grid=(nt + 1,),
        in_specs=[blk(D_MODEL), blk(D_ATT), blk(D_HG), blk(D_MODEL), blk(D_MODEL),
                  sample(D_MODEL), sample(D_ATT), sample(D_MODEL), sample(D_MODEL),
                  _resident((1, HG_DV)), _resident(hq_s.shape), _resident(hk_s.shape),
                  _resident(hi_s.shape), _resident(hgate_s.shape), sblk,
                  _resident(wa.shape), _resident(wb.shape), _resident(wo.shape),
                  _resident(w_up.shape), _resident(w_down.shape),
                  _resident((1, D_MODEL)), _resident((1, D_MODEL))],
        out_specs=[blk(D_MODEL), sample(D_MODEL), sblk],
        out_shape=[jax.ShapeDtypeStruct((n, D_MODEL), F32), jax.ShapeDtypeStruct((nb, D_MODEL), F32),
                   jax.ShapeDtypeStruct(state.shape, F32)],
        scratch_shapes=[pltpu.VMEM((nb, D_HG), F32)],
        compiler_params=_params("arbitrary"),
        name="merge_ffn",
    )(x, att, hg, sga, sgb, xs, att_s, sga_s, sgb_s, hg_norm, hq_s, hk_s, hi_s, hgate_s, state,
      wa, wb, wo, w_up, w_down, ln_ffn, ln_f)


def _sample_attn_kernel(sink_ref, qm_ref, kvn_ref, ckt_ref, cvt_ref, mk_ref, mv_ref,
                        o_ref, nkt_ref, nvt_ref):
    nb = qm_ref.shape[0]
    sk = sink_ref[...]
    newest = lax.broadcasted_iota(jnp.int32, (D_KV, WINDOW), 1) == WINDOW - 1
    kvn = kvn_ref[...]
    hi = kvn.astype(BF16)
    rest = kvn - hi.astype(F32)
    mid = rest.astype(BF16)
    lo = (rest - mid.astype(F32)).astype(BF16)
    r = lax.broadcasted_iota(jnp.int32, (2 * D_KV, 2 * D_KV), 0)
    c = lax.broadcasted_iota(jnp.int32, (2 * D_KV, 2 * D_KV), 1)
    eye = jnp.where(r == c, 1.0, 0.0).astype(BF16)
    pieces_t = _dot_nt(eye, jnp.concatenate([hi, mid, lo], axis=0)).astype(BF16)
    pr = lax.broadcasted_iota(jnp.int32, (3 * nb, WINDOW), 0)
    pc = lax.broadcasted_iota(jnp.int32, (3 * nb, WINDOW), 1)
    for b in range(nb):
        pick = jnp.logical_and(lax.rem(pr, nb) == b, pc == WINDOW - 1)
        new_cols = _dot(pieces_t, jnp.where(pick, 1.0, 0.0).astype(BF16))
        nkt_ref[b] = jnp.where(newest, new_cols[:D_KV], pltpu.roll(ckt_ref[b], WINDOW - 1, axis=1))
        nvt_ref[b] = jnp.where(newest, new_cols[D_KV:], pltpu.roll(cvt_ref[b], WINDOW - 1, axis=1))
    scores = [(_dot(qm_ref[b], nkt_ref[b].astype(BF16)),
               _dot_nt(qm_ref[b], mk_ref[b].astype(BF16)))
              for b in range(nb)]
    for b, (s_w, s_m) in enumerate(scores):
        m = jnp.maximum(jnp.maximum(jnp.max(s_w, axis=1, keepdims=True),
                                    jnp.max(s_m, axis=1, keepdims=True)), sk)
        e_w = jnp.exp(s_w - m)
        e_m = jnp.exp(s_m - m)
        l = (jnp.sum(e_w, axis=1, keepdims=True) + jnp.sum(e_m, axis=1, keepdims=True)
             + jnp.exp(sk - m))
        o = (_dot_nt(e_w.astype(BF16), nvt_ref[b].astype(BF16))
             + _dot(e_m.astype(BF16), mv_ref[b].astype(BF16)))
        o_ref[b] = o / l


def _sample_attention(sinks_col, qm, kv_new, ck, cv, mk, mv):
    nb = ck.shape[0]
    g = SAMPLE_ATT_GROUP

    def blk3(a, c):
        return pl.BlockSpec((g, a, c), lambda i: (i, 0, 0))

    return pl.pallas_call(
        _sample_attn_kernel,
        grid=(nb // g,),
        in_specs=[_resident((Q_HEADS, 1)), blk3(Q_HEADS, D_KV),
                  pl.BlockSpec((g, 2 * D_KV), lambda i: (i, 0)),
                  blk3(D_KV, WINDOW), blk3(D_KV, WINDOW), blk3(N_META, D_KV), blk3(N_META, D_KV)],
        out_specs=[blk3(Q_HEADS, D_KV), blk3(D_KV, WINDOW), blk3(D_KV, WINDOW)],
        out_shape=[jax.ShapeDtypeStruct((nb, Q_HEADS, D_KV), F32),
                   jax.ShapeDtypeStruct((nb, D_KV, WINDOW), F32),
                   jax.ShapeDtypeStruct((nb, D_KV, WINDOW), F32)],
        compiler_params=_params("parallel"),
        name="sample_attn",
    )(sinks_col, qm, kv_new, ck, cv, mk, mv)


def kernel(x_prompt, x_sample, cache_k, cache_v, cache_meta_k, cache_meta_v, state_hgrn, meta,
           w_in, sinks, lb_param, hg_norm, w_att_out, w_hg_out, w_o, ln_mix, ln_ffn, w_up,
           w_down, ln_f):
    bsz, seq, _ = x_prompt.shape
    nb = x_sample.shape[0]
    ln_mix2 = ln_mix.reshape(1, D_MODEL)
    ln_ffn2 = ln_ffn.reshape(1, D_MODEL)
    ln_f2 = ln_f.reshape(1, D_MODEL)
    nw = hg_norm.reshape(1, HG_DV)

    xs = x_sample.reshape(nb, D_MODEL)

    xp = x_prompt.reshape(bsz * seq, D_MODEL)
    (att_p, hg_p, sga_p, sgb_p, state_p, lastkv_p,
     q_s, kv_s, hq_s, hk_s, hi_s, hgate_s, sga_s, sgb_s,
     wa, wb, wo, wup, wdn) = _mixers(
        sinks.reshape(Q_HEADS), xp, xs, meta, ln_mix2, lb_param, w_in[0], nw,
        (w_att_out[0], w_hg_out[0], w_o[0], w_up[0], w_down[0]), bsz)

    grp = Q_HEADS // KV_HEADS
    rows = nb + N_META
    qs4 = q_s.reshape(rows, KV_HEADS, grp, 1, HEAD_DIM)
    sel = jnp.eye(KV_HEADS, dtype=BF16).reshape(1, KV_HEADS, 1, KV_HEADS, 1)
    qm = (qs4 * sel).reshape(rows, Q_HEADS, D_KV)

    def window_t(c):
        return jnp.swapaxes(c[0].reshape(nb, WINDOW, D_KV), 1, 2)

    o_s, nkt_s, nvt_s = _sample_attention(
        sinks.reshape(Q_HEADS, 1), qm, kv_s, window_t(cache_k), window_t(cache_v),
        cache_meta_k[0].reshape(nb, N_META, D_KV), cache_meta_v[0].reshape(nb, N_META, D_KV))
    o5 = o_s.reshape(nb, KV_HEADS, grp, KV_HEADS, HEAD_DIM)
    att_s = jnp.stack([o5[:, h, :, h, :] for h in range(KV_HEADS)], axis=1).reshape(nb, D_ATT)

    y_p, y_s, state_s = _merge_ffn(xp, att_p, hg_p, sga_p, sgb_p, xs, att_s, sga_s, sgb_s,
                                   nw, hq_s, hk_s, hi_s, hgate_s, state_hgrn[0],
                                   wa, wb, wo, ln_ffn2, wup, wdn, ln_f2)

    kv5 = lastkv_p.reshape(bsz, WINDOW, 2, KV_HEADS, HEAD_DIM)
    meta5 = jnp.broadcast_to(kv_s[nb:].reshape(1, N_META, 2, KV_HEADS, HEAD_DIM),
                             (bsz, N_META, 2, KV_HEADS, HEAD_DIM))
    return (y_p.reshape(bsz, seq, D_MODEL),
            y_s.reshape(nb, 1, D_MODEL),
            kv5[None, :, :, 0],
            kv5[None, :, :, 1],
            meta5[None, :, :, 0],
            meta5[None, :, :, 1],
            state_p[None],
            jnp.swapaxes(nkt_s, 1, 2).reshape(1, nb, WINDOW, KV_HEADS, HEAD_DIM),
            jnp.swapaxes(nvt_s, 1, 2).reshape(1, nb, WINDOW, KV_HEADS, HEAD_DIM),
            state_s[None])
```

```python
import functools

import jax
import jax.numpy as jnp
from jax import lax
from jax.experimental import pallas as pl
from jax.experimental.pallas import tpu as pltpu

F32 = jnp.float32
BF16 = jnp.bfloat16

D_MODEL = 1024
N_META = 16
WINDOW = 128
HEAD_DIM = 64
Q_HEADS = 8
KV_HEADS = 2
D_ATT = Q_HEADS * HEAD_DIM
D_KV = KV_HEADS * HEAD_DIM
HG_HEADS = 4
HG_DK = 128
HG_DV = 128
D_HG = HG_HEADS * HG_DK
HG_CHUNK = 64
D_FF = 4 * D_MODEL
EPS = 1e-6
C_Q = 0
C_KV = C_Q + D_ATT
C_HQ = C_KV + 2 * D_KV
C_HF = C_HQ + D_HG
C_HI = C_HF + D_HG
C_HGATE = C_HI + D_HG
C_GA = C_HGATE + D_HG
C_GB = C_GA + D_MODEL
D_IN = C_GB + D_MODEL

VMEM_LIMIT_BYTES = 56 * 1024 * 1024
MIX_ROWS = 512
PROJ_GROUP = 256
WEIGHT_SLAB_ROWS = 64
MERGE_ROWS = 512
SAMPLE_ATT_GROUP = 16
SAMPLE_HG_GROUP = 8

_NT = (((1,), (1,)), ((), ()))
_TN = (((0,), (0,)), ((), ()))


def _dot(a, b):
    return jnp.dot(a, b, preferred_element_type=F32)


def _dot_nt(a, b):
    return lax.dot_general(a, b, _NT, preferred_element_type=F32)


def _dot_tn(a, b):
    return lax.dot_general(a, b, _TN, preferred_element_type=F32)


def _rmsnorm(x, g):
    return x * lax.rsqrt(jnp.mean(x * x, axis=-1, keepdims=True) + EPS) * g


def _resident(shape):
    return pl.BlockSpec(shape, lambda *_: (0,) * len(shape), pipeline_mode=pl.Buffered(1))


def _params(*sem):
    return pltpu.CompilerParams(dimension_semantics=sem, vmem_limit_bytes=VMEM_LIMIT_BYTES)


def _lower_bound(lbp):
    e = jnp.exp(lbp - jnp.max(lbp, axis=0, keepdims=True))
    return e[0:1] / jnp.sum(e, axis=0, keepdims=True)


class _Projection:
    def __init__(self, xn, lbp, w_ref):
        self.xn = xn
        self.w_ref = w_ref
        self.lb = _lower_bound(lbp)

    def cols(self, base, part):
        return _dot(self.xn[...], self.w_ref[:, base + part.start:base + part.stop])

    def q_att(self, part=slice(0, D_ATT)):
        return self.cols(C_Q, part) * (HEAD_DIM ** -0.5)

    def kv(self):
        return self.cols(C_KV, slice(0, 2 * D_KV))

    def q_hg(self, part=slice(0, D_HG)):
        return self.cols(C_HQ, part) * (HG_DK ** -0.5)

    def forget(self, part=slice(0, D_HG)):
        lb = self.lb[:, part]
        f = lb + (1.0 - lb) * jax.nn.sigmoid(self.cols(C_HF, part))
        return 1.0 - f, jnp.log(f)

    def i_hg(self, part=slice(0, D_HG)):
        return self.cols(C_HI, part)

    def swish_gate(self, part=slice(0, D_HG)):
        g = self.cols(C_HGATE, part)
        return g * jax.nn.sigmoid(g)

    def branch_gate(self, base, part=slice(0, D_MODEL)):
        return jax.nn.sigmoid(self.cols(base, part))


class _Attention:
    def __init__(self, sink_ref, q_ref, kv, first, o_ref):
        self.sink_ref, self.q_ref, self.first, self.o_ref = sink_ref, q_ref, first, o_ref
        self.nsub = q_ref.shape[0] // WINDOW
        self.nk = 2 * WINDOW + N_META
        self.meta0 = WINDOW + q_ref.shape[0]
        lane = lax.broadcasted_iota(jnp.int32, (kv.shape[0], D_KV), 1)
        low = lane < HEAD_DIM
        k = kv[:, :D_KV]
        ksw = pltpu.roll(k, HEAD_DIM, axis=1)
        self.kboth = (jnp.where(low, k, ksw).astype(BF16), jnp.where(low, ksw, k).astype(BF16))
        qlane = lax.broadcasted_iota(jnp.int32, (1, 128), 1)
        self.keep = (jnp.where(qlane < HEAD_DIM, 1.0, 0.0).astype(BF16),
                     jnp.where(qlane < HEAD_DIM, 0.0, 1.0).astype(BF16))
        r = lax.broadcasted_iota(jnp.int32, (D_KV, D_KV), 0)
        c = lax.broadcasted_iota(jnp.int32, (D_KV, D_KV), 1)
        eye = jnp.where(r == c, 1.0, 0.0).astype(BF16)
        self.vt = _dot_nt(eye, kv[:, D_KV:].astype(BF16)).astype(BF16)
        self.ones = jnp.ones((16, self.nk), BF16)
        self.key = lax.broadcasted_iota(jnp.int32, (self.nk, 2 * WINDOW), 0)
        col = lax.broadcasted_iota(jnp.int32, (self.nk, 2 * WINDOW), 1)
        self.qry = jnp.bitwise_and(col, WINDOW - 1)
        self.first_head = lax.broadcasted_iota(jnp.int32, (1, 2 * WINDOW), 1) < WINDOW
        self.scores = {}

    def issue_scores(self, i):
        r0 = i * WINDOW
        out = []
        for p in range(Q_HEADS // 2):
            kb = self.kboth[(2 * p) // (Q_HEADS // KV_HEADS)]
            kmat = jnp.concatenate([kb[r0:r0 + 2 * WINDOW], kb[self.meta0:]], axis=0)
            qp = self.q_ref[r0:r0 + WINDOW, p * 128:(p + 1) * 128]
            q2 = jnp.concatenate([qp * self.keep[0], qp * self.keep[1]], axis=0)
            out.append(_dot_nt(kmat, q2))
        self.scores[i] = out

    def finish(self, i):
        r0 = i * WINDOW
        lo = jnp.where(self.first, WINDOW - 1, self.qry) if i == 0 else self.qry
        mask = jnp.logical_and(self.key > lo, self.key <= self.qry + WINDOW)
        mask = jnp.logical_or(mask, self.key >= 2 * WINDOW)
        vaug = []
        for h in range(KV_HEADS):
            vth = self.vt[h * HEAD_DIM:(h + 1) * HEAD_DIM]
            vaug.append(jnp.concatenate(
                [jnp.concatenate([vth[:, r0:r0 + 2 * WINDOW], vth[:, self.meta0:]], axis=1),
                 self.ones], axis=0))
        outs = []
        for p, raw in enumerate(self.scores.pop(i)):
            h = (2 * p) // (Q_HEADS // KV_HEADS)
            s = jnp.where(mask, raw, -jnp.inf)
            sk = jnp.where(self.first_head, self.sink_ref[2 * p], self.sink_ref[2 * p + 1])
            m = jnp.maximum(jnp.max(s, axis=0, keepdims=True), sk)
            e = jnp.exp(s - m).astype(BF16)
            oa = _dot(vaug[h], e)
            o = oa[:HEAD_DIM] / (oa[HEAD_DIM:HEAD_DIM + 1] + jnp.exp(sk - m))
            outs += [o[:, :WINDOW], o[:, WINDOW:]]
        self.o_ref[r0:r0 + WINDOW, :] = jnp.concatenate(outs, axis=0).T.astype(self.o_ref.dtype)


def _cumsum_rows(x):
    t = x.shape[0]
    row = lax.broadcasted_iota(jnp.int32, x.shape, 0)
    d = 1
    while d < t:
        x = x + jnp.where(row >= d, pltpu.roll(x, d, axis=0), 0.0)
        d *= 2
    return x


def _hg_out(o, gate, nw):
    parts = []
    for h in range(HG_HEADS):
        oh = o[:, h * HG_DV:(h + 1) * HG_DV]
        parts.append(oh * lax.rsqrt(jnp.mean(oh * oh, axis=-1, keepdims=True) + EPS) * nw)
    return jnp.concatenate(parts, axis=1) * gate


_HG_SLICES = [slice(h * HG_DK, (h + 1) * HG_DK) for h in range(HG_HEADS)]


def _state_update(iv, kd):
    return jnp.concatenate([_dot_tn(iv[:, hs], kd[:, hs]) for hs in _HG_SLICES], axis=1)


def _meta_state(mk, mlogf, mi):
    b = _cumsum_rows(mlogf)
    kd = (mk * jnp.exp(b[-1:] - b)).astype(BF16)
    return _state_update(mi.astype(BF16), kd)


class _Hgrn:
    def __init__(self, q_ref, k_ref, logf_ref, i_ref, g_ref, nw, state, o_ref):
        self.q_ref, self.k_ref, self.logf_ref, self.i_ref, self.g_ref = q_ref, k_ref, logf_ref, i_ref, g_ref
        self.nw, self.state, self.o_ref = nw, state, o_ref
        t = HG_CHUNK
        self.nchunk = q_ref.shape[0] // t
        r = lax.broadcasted_iota(jnp.int32, (t, t), 0)
        c = lax.broadcasted_iota(jnp.int32, (t, t), 1)
        self.causal = r >= c
        self.cums, self.terms, self.entering = {}, {}, {}

    def rows(self, ci):
        return slice(ci * HG_CHUNK, (ci + 1) * HG_CHUNK)

    def issue_cumsum(self, ci):
        self.cums[ci] = _cumsum_rows(self.logf_ref[self.rows(ci), :])

    def issue_local(self, ci):
        rows = self.rows(ci)
        b = self.cums.pop(ci)
        bl = b[-1:]
        k = self.k_ref[rows, :].astype(F32)
        qe = (self.q_ref[rows, :].astype(F32) * jnp.exp(b)).astype(BF16)
        ke = (k * jnp.exp(-b)).astype(BF16)
        kd = (k * jnp.exp(bl - b)).astype(BF16)
        iv = self.i_ref[rows, :].astype(BF16)
        a = [jnp.where(self.causal, _dot_nt(qe[:, hs], ke[:, hs]), 0.0).astype(BF16) for hs in _HG_SLICES]
        self.terms[ci] = (qe, iv, a)
        self.entering[ci] = self.state.astype(BF16)
        self.state = self.state * jnp.exp(bl) + _state_update(iv, kd)

    def finish(self, ci):
        rows = self.rows(ci)
        qe, iv, a = self.terms.pop(ci)
        ent = self.entering.pop(ci)
        outs = [_dot(a[h], iv[:, hs]) + _dot_nt(qe[:, hs], ent[:, hs]) for h, hs in enumerate(_HG_SLICES)]
        o = _hg_out(jnp.concatenate(outs, axis=1), self.g_ref[rows, :].astype(F32), self.nw)
        self.o_ref[rows, :] = o.astype(self.o_ref.dtype)


def _load_as_bf16(w_hbm, w_vmem, stage, sem):
    rows = stage.shape[1]
    nslab = w_hbm.shape[0] // rows

    def slab_copy(c):
        return pltpu.make_async_copy(w_hbm.at[pl.ds(c * rows, rows), :], stage.at[c % 2], sem.at[c % 2])

    slab_copy(0).start()
    for c in range(nslab):
        if c + 1 < nslab:
            slab_copy(c + 1).start()
        slab_copy(c).wait()
        w_vmem[pl.ds(c * rows, rows), :] = stage[c % 2].astype(w_vmem.dtype)


def _mixer_kernel(sink_ref, x0_ref, xnext_ref, xs_ref, xm_ref, g_ref, lbp_ref, w_hbm, nw_ref,
                  wa32, wb32, wo32, wup32, wdn32,
                  att_ref, hg_ref, sga_ref, sgb_ref, sfin_ref, lastkv_ref,
                  qs_ref, kvs_ref, hqs_ref, hks_ref, his_ref, hgs_ref, sgas_ref, sgbs_ref,
                  wa16, wb16, wo16, wup16, wdn16,
                  w_ref, stage, sem, xn_s, zq, zkv, zhq, zhk, zlogf, zhi, zhg, kvm_ref, st_ref, mst_ref,
                  *, tiles_per_seq):
    s = pl.program_id(0)
    t = xnext_ref.shape[0]

    @pl.when(s == 0)
    def _():
        _load_as_bf16(w_hbm, w_ref, stage, sem)
        xn_s[...] = _rmsnorm(x0_ref[...], g_ref[...]).astype(xn_s.dtype)
        for ref in (zq, zkv, zhq, zhk, zlogf, zhi, zhg, st_ref):
            ref[...] = jnp.zeros(ref.shape, ref.dtype)
        nb = xs_ref.shape[0]
        small = jnp.concatenate([xs_ref[...], xm_ref[...]], axis=0)
        p = _Projection(_rmsnorm(small, g_ref[...]).astype(BF16), lbp_ref[...], w_ref)
        qs_ref[...] = p.q_att().astype(qs_ref.dtype)
        kv = p.kv()
        kvs_ref[...] = kv
        kvm_ref[...] = kv[nb:]
        hqs_ref[...] = p.q_hg()
        k, logf = p.forget()
        hks_ref[...] = k
        iv = p.i_hg()
        his_ref[...] = iv
        hgs_ref[...] = p.swish_gate()
        sgas_ref[...] = p.branch_gate(C_GA).astype(sgas_ref.dtype)
        sgbs_ref[...] = p.branch_gate(C_GB).astype(sgbs_ref.dtype)
        mst_ref[...] = _meta_state(k[nb:], logf[nb:], iv[nb:])

    first = lax.rem(jnp.maximum(s - 1, 0), tiles_per_seq) == 0

    att = _Attention(sink_ref, zq, jnp.concatenate([zkv[...], kvm_ref[...]], axis=0), first, att_ref)
    last = zkv[t:t + WINDOW, :]
    lastkv_ref[...] = last
    zkv[0:WINDOW, :] = last
    entering = jnp.where(first, mst_ref[...], st_ref[...])
    hg = _Hgrn(zhq, zhk, zlogf, zhi, zhg, nw_ref[...], entering, hg_ref)
    proj = _Projection(xn_s, lbp_ref[...], w_ref)

    def gate_a(p):
        sga_ref[:, p] = proj.branch_gate(C_GA, p).astype(sga_ref.dtype)

    def gate_b(p):
        sgb_ref[:, p] = proj.branch_gate(C_GB, p).astype(sgb_ref.dtype)

    def new_kv(_):
        zkv[WINDOW:WINDOW + t, :] = proj.kv()

    def new_forget(p):
        k_new, logf_new = proj.forget(p)
        zhk[:, p] = k_new.astype(zhk.dtype)
        zlogf[:, p] = logf_new

    def new_q(p):
        zq[:, p] = proj.q_att(p).astype(zq.dtype)

    def new_hq(p):
        zhq[:, p] = proj.q_hg(p).astype(zhq.dtype)

    def new_hi(p):
        zhi[:, p] = proj.i_hg(p).astype(zhi.dtype)

    def new_gate(p):
        zhg[:, p] = proj.swish_gate(p).astype(zhg.dtype)

    def parts(n):
        return [slice(c, c + PROJ_GROUP) for c in range(0, n, PROJ_GROUP)]

    groups = ([(gate_a, p) for p in parts(D_MODEL)] + [(gate_b, p) for p in parts(D_MODEL)]
              + [(new_kv, None)] + [(new_q, p) for p in parts(D_ATT)]
              + [(new_forget, p) for p in parts(D_HG)] + [(new_hq, p) for p in parts(D_HG)]
              + [(new_hi, p) for p in parts(D_HG)] + [(new_gate, p) for p in parts(D_HG)])

    def save_state():
        st_ref[...] = hg.state
        for h, hs in enumerate(_HG_SLICES):
            sfin_ref[h] = hg.state[:, hs].T

    qk, cs, loc, fin, out = att.issue_scores, hg.issue_cumsum, hg.issue_local, att.finish, hg.finish
    mixer_work = {
        0: [(qk, 0), (cs, 0), (cs, 1), (cs, 2), (cs, 3)],
        1: [(cs, 4), (cs, 5), (loc, 0)],
        2: [(cs, 6), (cs, 7), (loc, 1)],
        3: [(qk, 1), (loc, 2)],
        4: [(fin, 0), (loc, 3)],
        5: [(out, 0), (loc, 4)],
        6: [(qk, 2), (out, 1), (loc, 5)],
        7: [(fin, 1), (out, 2), (loc, 6)],
        8: [(qk, 3), (out, 3), (loc, 7), (save_state,)],
        9: [(out, 4)],
        10: [(fin, 2), (out, 5)],
        11: [(out, 6)],
        12: [(out, 7)],
        13: [(fin, 3)],
    }
    for slot, (fn, p) in enumerate(groups):
        fn(p)
        for item in mixer_work.get(slot, []):
            item[0](*item[1:])
    xn_s[...] = _rmsnorm(xnext_ref[...], g_ref[...]).astype(xn_s.dtype)
    for src, dst in ((wa32, wa16), (wb32, wb16), (wo32, wo16), (wup32, wup16), (wdn32, wdn16)):
        dst[...] = src[...].astype(dst.dtype)


def _mixers(sinks, x, x_sample, x_meta, g, lb_param, w_f32, hg_norm, later_weights, bsz):
    n = x.shape[0]
    t = MIX_ROWS
    nt = n // t
    per_seq = nt // bsz
    rows = x_sample.shape[0] + x_meta.shape[0]

    def this_tile(c):
        return pl.BlockSpec((t, c), lambda s: (jnp.minimum(s, nt - 1), 0))

    def prev_tile(c):
        return pl.BlockSpec((t, c), lambda s: (jnp.maximum(s - 1, 0), 0))

    def prev_seq(shape):
        return pl.BlockSpec((None,) + shape,
                            lambda s: (jnp.maximum(s - 1, 0) // per_seq,) + (0,) * len(shape))

    def small(c):
        return pl.BlockSpec((rows, c), lambda s: (0, 0))

    first_tile = pl.BlockSpec((t, D_MODEL), lambda s: (0, 0), pipeline_mode=pl.Buffered(1))
    next_tile = pl.BlockSpec((t, D_MODEL), lambda s: (jnp.minimum(s + 1, nt - 1), 0))
    small_widths = (D_ATT, 2 * D_KV, D_HG, D_HG, D_HG, D_HG, D_MODEL, D_MODEL)
    small_dtypes = (BF16, F32, F32, F32, F32, F32, BF16, BF16)

    def row_block(wt):
        return pl.BlockSpec((wt.shape[0] // nt, wt.shape[1]), lambda s: (jnp.minimum(s, nt - 1), 0))

    return pl.pallas_call(
        functools.partial(_mixer_kernel, tiles_per_seq=per_seq),
        grid=(nt + 1,),
        in_specs=[pl.BlockSpec(memory_space=pltpu.SMEM), first_tile, next_tile,
                  _resident(x_sample.shape), _resident(x_meta.shape), _resident((1, D_MODEL)),
                  _resident(lb_param.shape), pl.BlockSpec(memory_space=pl.ANY), _resident((1, HG_DV))]
                 + [row_block(wt) for wt in later_weights],
        out_specs=[prev_tile(D_ATT), prev_tile(D_HG), this_tile(D_MODEL), this_tile(D_MODEL),
                   prev_seq((HG_HEADS, HG_DK, HG_DV)), prev_seq((WINDOW, 2 * D_KV))]
                  + [small(c) for c in small_widths] + [row_block(wt) for wt in later_weights],
        out_shape=[jax.ShapeDtypeStruct((n, D_ATT), BF16), jax.ShapeDtypeStruct((n, D_HG), BF16),
                   jax.ShapeDtypeStruct((n, D_MODEL), BF16), jax.ShapeDtypeStruct((n, D_MODEL), BF16),
                   jax.ShapeDtypeStruct((bsz, HG_HEADS, HG_DK, HG_DV), F32),
                   jax.ShapeDtypeStruct((bsz, WINDOW, 2 * D_KV), F32)]
                  + [jax.ShapeDtypeStruct((rows, c), d) for c, d in zip(small_widths, small_dtypes)]
                  + [jax.ShapeDtypeStruct(wt.shape, BF16) for wt in later_weights],
        scratch_shapes=[pltpu.VMEM((D_MODEL, D_IN), BF16),
                        pltpu.VMEM((2, WEIGHT_SLAB_ROWS, D_IN), F32),
                        pltpu.SemaphoreType.DMA((2,)),
                        pltpu.VMEM((t, D_MODEL), BF16),
                        pltpu.VMEM((t, D_ATT), BF16), pltpu.VMEM((WINDOW + t, 2 * D_KV), F32),
                        pltpu.VMEM((t, D_HG), BF16), pltpu.VMEM((t, D_HG), BF16),
                        pltpu.VMEM((t, D_HG), F32), pltpu.VMEM((t, D_HG), BF16),
                        pltpu.VMEM((t, D_HG), BF16),
                        pltpu.VMEM((N_META, 2 * D_KV), F32),
                        pltpu.VMEM((HG_DV, D_HG), F32),
                        pltpu.VMEM((HG_DV, D_HG), F32)],
        compiler_params=_params("arbitrary"),
        name="mixers",
    )(sinks, x, x, x_sample, x_meta, g, lb_param, w_f32, hg_norm, *later_weights)


def _merge_ffn_rows(x, att, hg, sga, sgb, wa_ref, wb_ref, wo_ref, ln_ffn, wup_ref, wdn_ref, ln_f):
    ya = _dot(att.astype(BF16), wa_ref[...])
    yb = _dot(hg.astype(BF16), wb_ref[...])
    mix = sga.astype(F32) * ya + sgb.astype(F32) * yb
    h1 = x + _dot(mix.astype(BF16), wo_ref[...])
    xn = _rmsnorm(h1, ln_ffn).astype(BF16)
    acc = jnp.zeros_like(h1)
    step = 1024
    for c in range(D_FF // step):
        u = jnp.maximum(_dot(xn, wup_ref[:, c * step:(c + 1) * step]), 0.0)
        acc = acc + _dot((u * u).astype(BF16), wdn_ref[c * step:(c + 1) * step, :])
    return _rmsnorm(h1 + acc, ln_f)


def _sample_hgrn_group(rows, nw, q_ref, k_ref, i_ref, g_ref, s_ref, snew_ref, hg_ref):
    k = k_ref[rows, :]
    q = q_ref[rows, :]
    iv = i_ref[rows, :]
    r = lax.broadcasted_iota(jnp.int32, (HG_DK, HG_DK), 0)
    c = lax.broadcasted_iota(jnp.int32, (HG_DK, HG_DK), 1)
    eye = r == c

    def column(row):
        return jnp.sum(jnp.where(eye, row, 0.0), axis=1, keepdims=True)

    outs = []
    for b in range(s_ref.shape[0]):
        heads = []
        for h, hs in enumerate(_HG_SLICES):
            kc = column(k[b:b + 1, hs])
            qc = column(q[b:b + 1, hs])
            s_old = s_ref[b, h]
            s = s_old + kc * (iv[b:b + 1, hs] - s_old)
            snew_ref[b, h] = s
            heads.append(jnp.sum(qc * s, axis=0, keepdims=True))
        outs.append(jnp.concatenate(heads, axis=1))
    hg_ref[rows, :] = _hg_out(jnp.concatenate(outs, axis=0), g_ref[rows, :], nw)


def _merge_ffn_kernel(x_ref, att_ref, hg_ref, sga_ref, sgb_ref,
                      xs_ref, atts_ref, sgas_ref, sgbs_ref,
                      nw_ref, hqs_ref, hks_ref, his_ref, hgates_ref, state_ref,
                      wa_ref, wb_ref, wo_ref, wup_ref, wdn_ref, lnffn_ref, lnf_ref,
                      y_ref, ys_ref, snew_ref, hgs, *, prompt_steps):
    i = pl.program_id(0)
    weights = (wa_ref, wb_ref, wo_ref, lnffn_ref[...], wup_ref, wdn_ref, lnf_ref[...])
    group = state_ref.shape[0]
    ngroups = xs_ref.shape[0] // group

    @pl.when(i < prompt_steps)
    def _():
        y_ref[...] = _merge_ffn_rows(x_ref[...], att_ref[...], hg_ref[...], sga_ref[...], sgb_ref[...],
                                     *weights)
        rows = pl.ds(pl.multiple_of(jnp.minimum(i, ngroups - 1) * group, group), group)
        _sample_hgrn_group(rows, nw_ref[...], hqs_ref, hks_ref, his_ref, hgates_ref,
                           state_ref, snew_ref, hgs)

    @pl.when(i == prompt_steps)
    def _():
        ys_ref[...] = _merge_ffn_rows(xs_ref[...], atts_ref[...], hgs[...], sgas_ref[...],
                                      sgbs_ref[...], *weights)


def _merge_ffn(x, att, hg, sga, sgb, xs, att_s, sga_s, sgb_s, hg_norm, hq_s, hk_s, hi_s, hgate_s, state,
               wa, wb, wo, ln_ffn, w_up, w_down, ln_f):
    n = x.shape[0]
    nb = xs.shape[0]
    rows = MERGE_ROWS
    nt = n // rows
    g = SAMPLE_HG_GROUP
    assert nb // g <= nt

    def blk(c):
        return pl.BlockSpec((rows, c), lambda i: (jnp.minimum(i, nt - 1), 0))

    def sample(c):
        return pl.BlockSpec((nb, c), lambda i: (0, 0))

    sblk = pl.BlockSpec((g, HG_HEADS, HG_DK, HG_DV), lambda i: (jnp.minimum(i, nb // g - 1), 0, 0, 0))
    return pl.pallas_call(
        functools.partial(_merge_ffn_kernel, prompt_steps=nt),
        grid=(nt + 1,),
        in_specs=[blk(D_MODEL), blk(D_ATT), blk(D_HG), blk(D_MODEL), blk(D_MODEL),
                  sample(D_MODEL), sample(D_ATT), sample(D_MODEL), sample(D_MODEL),
                  _resident((1, HG_DV)), _resident(hq_s.shape), _resident(hk_s.shape),
                  _resident(hi_s.shape), _resident(hgate_s.shape), sblk,
                  _resident(wa.shape), _resident(wb.shape), _resident(wo.shape),
                  _resident(w_up.shape), _resident(w_down.shape),
                  _resident((1, D_MODEL)), _resident((1, D_MODEL))],
        out_specs=[blk(D_MODEL), sample(D_MODEL), sblk],
        out_shape=[jax.ShapeDtypeStruct((n, D_MODEL), F32), jax.ShapeDtypeStruct((nb, D_MODEL), F32),
                   jax.ShapeDtypeStruct(state.shape, F32)],
        scratch_shapes=[pltpu.VMEM((nb, D_HG), F32)],
        compiler_params=_params("arbitrary"),
        name="merge_ffn",
    )(x, att, hg, sga, sgb, xs, att_s, sga_s, sgb_s, hg_norm, hq_s, hk_s, hi_s, hgate_s, state,
      wa, wb, wo, w_up, w_down, ln_ffn, ln_f)


def _sample_attn_kernel(sink_ref, qm_ref, kvn_ref, ckt_ref, cvt_ref, mkt_ref, mvt_ref,
                        o_ref, nkt_ref, nvt_ref):
    nb = qm_ref.shape[0]
    head = lax.broadcasted_iota(jnp.int32, (Q_HEADS, 1), 0)
    sk = jnp.zeros((Q_HEADS, 1), F32)
    for j in range(Q_HEADS):
        sk = jnp.where(head == j, sink_ref[j], sk)
    newest = lax.broadcasted_iota(jnp.int32, (D_KV, WINDOW), 1) == WINDOW - 1
    kvn = kvn_ref[...]
    hi = kvn.astype(BF16)
    rest = kvn - hi.astype(F32)
    mid = rest.astype(BF16)
    lo = (rest - mid.astype(F32)).astype(BF16)
    r = lax.broadcasted_iota(jnp.int32, (2 * D_KV, 2 * D_KV), 0)
    c = lax.broadcasted_iota(jnp.int32, (2 * D_KV, 2 * D_KV), 1)
    eye = jnp.where(r == c, 1.0, 0.0).astype(BF16)
    pieces_t = _dot_nt(eye, jnp.concatenate([hi, mid, lo], axis=0)).astype(BF16)
    pr = lax.broadcasted_iota(jnp.int32, (3 * nb, WINDOW), 0)
    pc = lax.broadcasted_iota(jnp.int32, (3 * nb, WINDOW), 1)
    for b in range(nb):
        pick = jnp.logical_and(lax.rem(pr, nb) == b, pc == WINDOW - 1)
        new_cols = _dot(pieces_t, jnp.where(pick, 1.0, 0.0).astype(BF16))
        nkt_ref[b] = jnp.where(newest, new_cols[:D_KV], pltpu.roll(ckt_ref[b], WINDOW - 1, axis=1))
        nvt_ref[b] = jnp.where(newest, new_cols[D_KV:], pltpu.roll(cvt_ref[b], WINDOW - 1, axis=1))
    scores = [(_dot(qm_ref[b], nkt_ref[b].astype(BF16)),
               _dot(qm_ref[b], mkt_ref[b].astype(BF16)))
              for b in range(nb)]
    for b, (s_w, s_m) in enumerate(scores):
        m = jnp.maximum(jnp.maximum(jnp.max(s_w, axis=1, keepdims=True),
                                    jnp.max(s_m, axis=1, keepdims=True)), sk)
        e_w = jnp.exp(s_w - m)
        e_m = jnp.exp(s_m - m)
        l = (jnp.sum(e_w, axis=1, keepdims=True) + jnp.sum(e_m, axis=1, keepdims=True)
             + jnp.exp(sk - m))
        o = (_dot_nt(e_w.astype(BF16), nvt_ref[b].astype(BF16))
             + _dot_nt(e_m.astype(BF16), mvt_ref[b].astype(BF16)))
        o_ref[b] = o / l


def _sample_attention(sinks, qm, kv_new, ck, cv, mk, mv):
    nb = ck.shape[0]
    g = SAMPLE_ATT_GROUP

    def blk3(a, c):
        return pl.BlockSpec((g, a, c), lambda i: (i, 0, 0))

    return pl.pallas_call(
        _sample_attn_kernel,
        grid=(nb // g,),
        in_specs=[pl.BlockSpec(memory_space=pltpu.SMEM), blk3(Q_HEADS, D_KV),
                  pl.BlockSpec((g, 2 * D_KV), lambda i: (i, 0)),
                  blk3(D_KV, WINDOW), blk3(D_KV, WINDOW), blk3(D_KV, N_META), blk3(D_KV, N_META)],
        out_specs=[blk3(Q_HEADS, D_KV), blk3(D_KV, WINDOW), blk3(D_KV, WINDOW)],
        out_shape=[jax.ShapeDtypeStruct((nb, Q_HEADS, D_KV), F32),
                   jax.ShapeDtypeStruct((nb, D_KV, WINDOW), F32),
                   jax.ShapeDtypeStruct((nb, D_KV, WINDOW), F32)],
        compiler_params=_params("parallel"),
        name="sample_attn",
    )(sinks, qm, kv_new, ck, cv, mk, mv)


def kernel(x_prompt, x_sample, cache_k, cache_v, cache_meta_k, cache_meta_v, state_hgrn, meta,
           w_in, sinks, lb_param, hg_norm, w_att_out, w_hg_out, w_o, ln_mix, ln_ffn, w_up,
           w_down, ln_f):
    bsz, seq, _ = x_prompt.shape
    nb = x_sample.shape[0]
    ln_mix2 = ln_mix.reshape(1, D_MODEL)
    ln_ffn2 = ln_ffn.reshape(1, D_MODEL)
    ln_f2 = ln_f.reshape(1, D_MODEL)
    nw = hg_norm.reshape(1, HG_DV)

    xs = x_sample.reshape(nb, D_MODEL)

    xp = x_prompt.reshape(bsz * seq, D_MODEL)
    (att_p, hg_p, sga_p, sgb_p, state_p, lastkv_p,
     q_s, kv_s, hq_s, hk_s, hi_s, hgate_s, sga_s, sgb_s,
     wa, wb, wo, wup, wdn) = _mixers(
        sinks.reshape(Q_HEADS), xp, xs, meta, ln_mix2, lb_param, w_in[0], nw,
        (w_att_out[0], w_hg_out[0], w_o[0], w_up[0], w_down[0]), bsz)

    grp = Q_HEADS // KV_HEADS
    rows = nb + N_META
    qs4 = q_s.reshape(rows, KV_HEADS, grp, 1, HEAD_DIM)
    sel = jnp.eye(KV_HEADS, dtype=BF16).reshape(1, KV_HEADS, 1, KV_HEADS, 1)
    qm = (qs4 * sel).reshape(rows, Q_HEADS, D_KV)

    def cache_t(c):
        return jnp.swapaxes(c[0].reshape(nb, c.shape[2], D_KV), 1, 2)

    o_s, nkt_s, nvt_s = _sample_attention(
        sinks.reshape(Q_HEADS), qm, kv_s, cache_t(cache_k), cache_t(cache_v),
        cache_t(cache_meta_k), cache_t(cache_meta_v))
    o5 = o_s.reshape(nb, KV_HEADS, grp, KV_HEADS, HEAD_DIM)
    att_s = jnp.stack([o5[:, h, :, h, :] for h in range(KV_HEADS)], axis=1).reshape(nb, D_ATT)

    y_p, y_s, state_s = _merge_ffn(xp, att_p, hg_p, sga_p, sgb_p, xs, att_s, sga_s, sgb_s,
                                   nw, hq_s, hk_s, hi_s, hgate_s, state_hgrn[0],
                                   wa, wb, wo, ln_ffn2, wup, wdn, ln_f2)

    kv5 = lastkv_p.reshape(bsz, WINDOW, 2, KV_HEADS, HEAD_DIM)
    meta5 = jnp.broadcast_to(kv_s[nb:].reshape(1, N_META, 2, KV_HEADS, HEAD_DIM),
                             (bsz, N_META, 2, KV_HEADS, HEAD_DIM))
    return (y_p.reshape(bsz, seq, D_MODEL),
            y_s.reshape(nb, 1, D_MODEL),
            kv5[None, :, :, 0],
            kv5[None, :, :, 1],
            meta5[None, :, :, 0],
            meta5[None, :, :, 1],
            state_p[None],
            jnp.swapaxes(nkt_s, 1, 2).reshape(1, nb, WINDOW, KV_HEADS, HEAD_DIM),
            jnp.swapaxes(nvt_s, 1, 2).reshape(1, nb, WINDOW, KV_HEADS, HEAD_DIM),
            state_s[None])
```

```python
import functools

import jax
import jax.numpy as jnp
from jax import lax
from jax.experimental import pallas as pl
from jax.experimental.pallas import tpu as pltpu

F32 = jnp.float32
BF16 = jnp.bfloat16

D_MODEL = 1024
N_META = 16
WINDOW = 128
HEAD_DIM = 64
Q_HEADS = 8
KV_HEADS = 2
D_ATT = Q_HEADS * HEAD_DIM
D_KV = KV_HEADS * HEAD_DIM
HG_HEADS = 4
HG_DK = 128
HG_DV = 128
D_HG = HG_HEADS * HG_DK
HG_CHUNK = 64
D_FF = 4 * D_MODEL
EPS = 1e-6
C_Q = 0
C_KV = C_Q + D_ATT
C_HQ = C_KV + 2 * D_KV
C_HF = C_HQ + D_HG
C_HI = C_HF + D_HG
C_HGATE = C_HI + D_HG
C_GA = C_HGATE + D_HG
C_GB = C_GA + D_MODEL
D_IN = C_GB + D_MODEL

VMEM_LIMIT_BYTES = 56 * 1024 * 1024
MIX_ROWS = 512
PROJ_GROUP = 256
WEIGHT_SLAB_ROWS = 64
MERGE_ROWS = 512
SAMPLE_ATT_GROUP = 16
SAMPLE_HG_GROUP = 8

_NT = (((1,), (1,)), ((), ()))
_TN = (((0,), (0,)), ((), ()))


def _dot(a, b):
    return jnp.dot(a, b, preferred_element_type=F32)


def _dot_nt(a, b):
    return lax.dot_general(a, b, _NT, preferred_element_type=F32)


def _dot_tn(a, b):
    return lax.dot_general(a, b, _TN, preferred_element_type=F32)


def _rmsnorm(x, g):
    return x * lax.rsqrt(jnp.mean(x * x, axis=-1, keepdims=True) + EPS) * g


def _resident(shape):
    return pl.BlockSpec(shape, lambda *_: (0,) * len(shape), pipeline_mode=pl.Buffered(1))


def _params(*sem):
    return pltpu.CompilerParams(dimension_semantics=sem, vmem_limit_bytes=VMEM_LIMIT_BYTES)


def _lower_bound(lbp):
    e = jnp.exp(lbp - jnp.max(lbp, axis=0, keepdims=True))
    return e[0:1] / jnp.sum(e, axis=0, keepdims=True)


class _Projection:
    def __init__(self, xn, lbp, w_ref):
        self.xn = xn
        self.w_ref = w_ref
        self.lb = _lower_bound(lbp)

    def cols(self, base, part):
        return _dot(self.xn[...], self.w_ref[:, base + part.start:base + part.stop])

    def q_att(self, part=slice(0, D_ATT)):
        return self.cols(C_Q, part) * (HEAD_DIM ** -0.5)

    def kv(self):
        return self.cols(C_KV, slice(0, 2 * D_KV))

    def q_hg(self, part=slice(0, D_HG)):
        return self.cols(C_HQ, part) * (HG_DK ** -0.5)

    def forget(self, part=slice(0, D_HG)):
        lb = self.lb[:, part]
        f = lb + (1.0 - lb) * jax.nn.sigmoid(self.cols(C_HF, part))
        return 1.0 - f, jnp.log(f)

    def i_hg(self, part=slice(0, D_HG)):
        return self.cols(C_HI, part)

    def swish_gate(self, part=slice(0, D_HG)):
        g = self.cols(C_HGATE, part)
        return g * jax.nn.sigmoid(g)

    def branch_gate(self, base, part=slice(0, D_MODEL)):
        return jax.nn.sigmoid(self.cols(base, part))


class _Attention:
    def __init__(self, sink_ref, q_ref, kv, first, o_ref):
        self.sink_ref, self.q_ref, self.first, self.o_ref = sink_ref, q_ref, first, o_ref
        self.nsub = q_ref.shape[0] // WINDOW
        self.nk = 2 * WINDOW + N_META
        self.meta0 = WINDOW + q_ref.shape[0]
        lane = lax.broadcasted_iota(jnp.int32, (kv.shape[0], D_KV), 1)
        low = lane < HEAD_DIM
        k = kv[:, :D_KV]
        ksw = pltpu.roll(k, HEAD_DIM, axis=1)
        self.kboth = (jnp.where(low, k, ksw).astype(BF16), jnp.where(low, ksw, k).astype(BF16))
        qlane = lax.broadcasted_iota(jnp.int32, (1, 128), 1)
        self.keep = (jnp.where(qlane < HEAD_DIM, 1.0, 0.0).astype(BF16),
                     jnp.where(qlane < HEAD_DIM, 0.0, 1.0).astype(BF16))
        r = lax.broadcasted_iota(jnp.int32, (D_KV, D_KV), 0)
        c = lax.broadcasted_iota(jnp.int32, (D_KV, D_KV), 1)
        eye = jnp.where(r == c, 1.0, 0.0).astype(BF16)
        self.vt = _dot_nt(eye, kv[:, D_KV:].astype(BF16)).astype(BF16)
        self.ones = jnp.ones((16, self.nk), BF16)
        self.key = lax.broadcasted_iota(jnp.int32, (self.nk, 2 * WINDOW), 0)
        col = lax.broadcasted_iota(jnp.int32, (self.nk, 2 * WINDOW), 1)
        self.qry = jnp.bitwise_and(col, WINDOW - 1)
        self.first_head = lax.broadcasted_iota(jnp.int32, (1, 2 * WINDOW), 1) < WINDOW
        self.scores = {}

    def issue_scores(self, i):
        r0 = i * WINDOW
        out = []
        for p in range(Q_HEADS // 2):
            kb = self.kboth[(2 * p) // (Q_HEADS // KV_HEADS)]
            kmat = jnp.concatenate([kb[r0:r0 + 2 * WINDOW], kb[self.meta0:]], axis=0)
            qp = self.q_ref[r0:r0 + WINDOW, p * 128:(p + 1) * 128]
            q2 = jnp.concatenate([qp * self.keep[0], qp * self.keep[1]], axis=0)
            out.append(_dot_nt(kmat, q2))
        self.scores[i] = out

    def finish(self, i):
        r0 = i * WINDOW
        lo = jnp.where(self.first, WINDOW - 1, self.qry) if i == 0 else self.qry
        mask = jnp.logical_and(self.key > lo, self.key <= self.qry + WINDOW)
        mask = jnp.logical_or(mask, self.key >= 2 * WINDOW)
        vaug = []
        for h in range(KV_HEADS):
            vth = self.vt[h * HEAD_DIM:(h + 1) * HEAD_DIM]
            vaug.append(jnp.concatenate(
                [jnp.concatenate([vth[:, r0:r0 + 2 * WINDOW], vth[:, self.meta0:]], axis=1),
                 self.ones], axis=0))
        outs = []
        for p, raw in enumerate(self.scores.pop(i)):
            h = (2 * p) // (Q_HEADS // KV_HEADS)
            s = jnp.where(mask, raw, -jnp.inf)
            sk = jnp.where(self.first_head, self.sink_ref[2 * p], self.sink_ref[2 * p + 1])
            m = jnp.maximum(jnp.max(s, axis=0, keepdims=True), sk)
            e = jnp.exp(s - m).astype(BF16)
            oa = _dot(vaug[h], e)
            o = oa[:HEAD_DIM] / (oa[HEAD_DIM:HEAD_DIM + 1] + jnp.exp(sk - m))
            outs += [o[:, :WINDOW], o[:, WINDOW:]]
        self.o_ref[r0:r0 + WINDOW, :] = jnp.concatenate(outs, axis=0).T.astype(self.o_ref.dtype)


def _cumsum_rows(x):
    t = x.shape[0]
    row = lax.broadcasted_iota(jnp.int32, x.shape, 0)
    d = 1
    while d < t:
        x = x + jnp.where(row >= d, pltpu.roll(x, d, axis=0), 0.0)
        d *= 2
    return x


def _hg_out(o, gate, nw):
    parts = []
    for h in range(HG_HEADS):
        oh = o[:, h * HG_DV:(h + 1) * HG_DV]
        parts.append(oh * lax.rsqrt(jnp.mean(oh * oh, axis=-1, keepdims=True) + EPS) * nw)
    return jnp.concatenate(parts, axis=1) * gate


_HG_SLICES = [slice(h * HG_DK, (h + 1) * HG_DK) for h in range(HG_HEADS)]


def _state_update(iv, kd):
    return jnp.concatenate([_dot_tn(iv[:, hs], kd[:, hs]) for hs in _HG_SLICES], axis=1)


def _meta_state(mk, mlogf, mi):
    b = _cumsum_rows(mlogf)
    kd = (mk * jnp.exp(b[-1:] - b)).astype(BF16)
    return _state_update(mi.astype(BF16), kd)


class _Hgrn:
    def __init__(self, q_ref, k_ref, logf_ref, i_ref, g_ref, nw, state, o_ref):
        self.q_ref, self.k_ref, self.logf_ref, self.i_ref, self.g_ref = q_ref, k_ref, logf_ref, i_ref, g_ref
        self.nw, self.state, self.o_ref = nw, state, o_ref
        t = HG_CHUNK
        self.nchunk = q_ref.shape[0] // t
        r = lax.broadcasted_iota(jnp.int32, (t, t), 0)
        c = lax.broadcasted_iota(jnp.int32, (t, t), 1)
        self.causal = r >= c
        self.cums, self.terms, self.entering = {}, {}, {}

    def rows(self, ci):
        return slice(ci * HG_CHUNK, (ci + 1) * HG_CHUNK)

    def issue_cumsum(self, ci):
        self.cums[ci] = _cumsum_rows(self.logf_ref[self.rows(ci), :])

    def issue_local(self, ci):
        rows = self.rows(ci)
        b = self.cums.pop(ci)
        bl = b[-1:]
        k = self.k_ref[rows, :].astype(F32)
        qe = (self.q_ref[rows, :].astype(F32) * jnp.exp(b)).astype(BF16)
        ke = (k * jnp.exp(-b)).astype(BF16)
        kd = (k * jnp.exp(bl - b)).astype(BF16)
        iv = self.i_ref[rows, :].astype(BF16)
        a = [jnp.where(self.causal, _dot_nt(qe[:, hs], ke[:, hs]), 0.0).astype(BF16) for hs in _HG_SLICES]
        self.terms[ci] = (qe, iv, a)
        self.entering[ci] = self.state.astype(BF16)
        self.state = self.state * jnp.exp(bl) + _state_update(iv, kd)

    def finish(self, ci):
        rows = self.rows(ci)
        qe, iv, a = self.terms.pop(ci)
        ent = self.entering.pop(ci)
        outs = [_dot(a[h], iv[:, hs]) + _dot_nt(qe[:, hs], ent[:, hs]) for h, hs in enumerate(_HG_SLICES)]
        o = _hg_out(jnp.concatenate(outs, axis=1), self.g_ref[rows, :].astype(F32), self.nw)
        self.o_ref[rows, :] = o.astype(self.o_ref.dtype)


def _load_as_bf16(w_hbm, w_vmem, stage, sem):
    rows = stage.shape[1]
    nslab = w_hbm.shape[0] // rows

    def slab_copy(c):
        return pltpu.make_async_copy(w_hbm.at[pl.ds(c * rows, rows), :], stage.at[c % 2], sem.at[c % 2])

    slab_copy(0).start()
    for c in range(nslab):
        if c + 1 < nslab:
            slab_copy(c + 1).start()
        slab_copy(c).wait()
        w_vmem[pl.ds(c * rows, rows), :] = stage[c % 2].astype(w_vmem.dtype)


def _mixer_kernel(sink_ref, x0_ref, xnext_ref, xs_ref, xm_ref, g_ref, lbp_ref, w_hbm, nw_ref,
                  wa32, wb32, wo32, wup32, wdn32,
                  att_ref, hg_ref, sga_ref, sgb_ref, sfin_ref, lastkv_ref,
                  qs_ref, kvs_ref, hqs_ref, hks_ref, his_ref, hgs_ref, sgas_ref, sgbs_ref,
                  wa16, wb16, wo16, wup16, wdn16,
                  w_ref, stage, sem, xn_s, zq, zkv, zhq, zhk, zlogf, zhi, zhg, kvm_ref, st_ref, mst_ref,
                  *, tiles_per_seq):
    s = pl.program_id(0)
    t = xnext_ref.shape[0]

    @pl.when(s == 0)
    def _():
        _load_as_bf16(w_hbm, w_ref, stage, sem)
        xn_s[...] = _rmsnorm(x0_ref[...], g_ref[...]).astype(xn_s.dtype)
        for ref in (zq, zkv, zhq, zhk, zlogf, zhi, zhg, st_ref):
            ref[...] = jnp.zeros(ref.shape, ref.dtype)
        nb = xs_ref.shape[0]
        small = jnp.concatenate([xs_ref[...], xm_ref[...]], axis=0)
        p = _Projection(_rmsnorm(small, g_ref[...]).astype(BF16), lbp_ref[...], w_ref)
        qs_ref[...] = p.q_att().astype(qs_ref.dtype)
        kv = p.kv()
        kvs_ref[...] = kv
        kvm_ref[...] = kv[nb:]
        hqs_ref[...] = p.q_hg()
        k, logf = p.forget()
        hks_ref[...] = k
        iv = p.i_hg()
        his_ref[...] = iv
        hgs_ref[...] = p.swish_gate()
        sgas_ref[...] = p.branch_gate(C_GA).astype(sgas_ref.dtype)
        sgbs_ref[...] = p.branch_gate(C_GB).astype(sgbs_ref.dtype)
        mst_ref[...] = _meta_state(k[nb:], logf[nb:], iv[nb:])

    first = lax.rem(jnp.maximum(s - 1, 0), tiles_per_seq) == 0

    att = _Attention(sink_ref, zq, jnp.concatenate([zkv[...], kvm_ref[...]], axis=0), first, att_ref)
    last = zkv[t:t + WINDOW, :]
    lastkv_ref[...] = last
    zkv[0:WINDOW, :] = last
    entering = jnp.where(first, mst_ref[...], st_ref[...])
    hg = _Hgrn(zhq, zhk, zlogf, zhi, zhg, nw_ref[...], entering, hg_ref)
    proj = _Projection(xn_s, lbp_ref[...], w_ref)

    def gate_a(p):
        sga_ref[:, p] = proj.branch_gate(C_GA, p).astype(sga_ref.dtype)

    def gate_b(p):
        sgb_ref[:, p] = proj.branch_gate(C_GB, p).astype(sgb_ref.dtype)

    def new_kv(_):
        zkv[WINDOW:WINDOW + t, :] = proj.kv()

    def new_forget(p):
        k_new, logf_new = proj.forget(p)
        zhk[:, p] = k_new.astype(zhk.dtype)
        zlogf[:, p] = logf_new

    def new_q(p):
        zq[:, p] = proj.q_att(p).astype(zq.dtype)

    def new_hq(p):
        zhq[:, p] = proj.q_hg(p).astype(zhq.dtype)

    def new_hi(p):
        zhi[:, p] = proj.i_hg(p).astype(zhi.dtype)

    def new_gate(p):
        zhg[:, p] = proj.swish_gate(p).astype(zhg.dtype)

    def parts(n):
        return [slice(c, c + PROJ_GROUP) for c in range(0, n, PROJ_GROUP)]

    groups = ([(gate_a, p) for p in parts(D_MODEL)] + [(gate_b, p) for p in parts(D_MODEL)]
              + [(new_kv, None)] + [(new_q, p) for p in parts(D_ATT)]
              + [(new_forget, p) for p in parts(D_HG)] + [(new_hq, p) for p in parts(D_HG)]
              + [(new_hi, p) for p in parts(D_HG)] + [(new_gate, p) for p in parts(D_HG)])

    def save_state():
        st_ref[...] = hg.state
        for h, hs in enumerate(_HG_SLICES):
            sfin_ref[h] = hg.state[:, hs].T

    qk, cs, loc, fin, out = att.issue_scores, hg.issue_cumsum, hg.issue_local, att.finish, hg.finish
    mixer_work = {
        0: [(qk, 0), (cs, 0), (cs, 1), (cs, 2), (cs, 3)],
        1: [(cs, 4), (cs, 5), (loc, 0)],
        2: [(cs, 6), (cs, 7), (loc, 1)],
        3: [(qk, 1), (loc, 2)],
        4: [(fin, 0), (loc, 3)],
        5: [(out, 0), (loc, 4)],
        6: [(qk, 2), (out, 1), (loc, 5)],
        7: [(fin, 1), (out, 2), (loc, 6)],
        8: [(qk, 3), (out, 3), (loc, 7), (save_state,)],
        9: [(out, 4)],
        10: [(fin, 2), (out, 5)],
        11: [(out, 6)],
        12: [(out, 7)],
        13: [(fin, 3)],
    }
    for slot, (fn, p) in enumerate(groups):
        fn(p)
        for item in mixer_work.get(slot, []):
            item[0](*item[1:])
    xn_s[...] = _rmsnorm(xnext_ref[...], g_ref[...]).astype(xn_s.dtype)
    for src, dst in ((wa32, wa16), (wb32, wb16), (wo32, wo16), (wup32, wup16), (wdn32, wdn16)):
        dst[...] = src[...].astype(dst.dtype)


def _mixers(sinks, x, x_sample, x_meta, g, lb_param, w_f32, hg_norm, later_weights, bsz):
    n = x.shape[0]
    t = MIX_ROWS
    nt = n // t
    per_seq = nt // bsz
    rows = x_sample.shape[0] + x_meta.shape[0]

    def this_tile(c):
        return pl.BlockSpec((t, c), lambda s: (jnp.minimum(s, nt - 1), 0))

    def prev_tile(c):
        return pl.BlockSpec((t, c), lambda s: (jnp.maximum(s - 1, 0), 0))

    def prev_seq(shape):
        return pl.BlockSpec((None,) + shape,
                            lambda s: (jnp.maximum(s - 1, 0) // per_seq,) + (0,) * len(shape))

    def small(c):
        return pl.BlockSpec((rows, c), lambda s: (0, 0))

    first_tile = pl.BlockSpec((t, D_MODEL), lambda s: (0, 0), pipeline_mode=pl.Buffered(1))
    next_tile = pl.BlockSpec((t, D_MODEL), lambda s: (jnp.minimum(s + 1, nt - 1), 0))
    small_widths = (D_ATT, 2 * D_KV, D_HG, D_HG, D_HG, D_HG, D_MODEL, D_MODEL)
    small_dtypes = (BF16, F32, F32, F32, F32, F32, BF16, BF16)

    def row_block(wt):
        return pl.BlockSpec((wt.shape[0] // nt, wt.shape[1]), lambda s: (jnp.minimum(s, nt - 1), 0))

    return pl.pallas_call(
        functools.partial(_mixer_kernel, tiles_per_seq=per_seq),
        grid=(nt + 1,),
        in_specs=[pl.BlockSpec(memory_space=pltpu.SMEM), first_tile, next_tile,
                  _resident(x_sample.shape), _resident(x_meta.shape), _resident((1, D_MODEL)),
                  _resident(lb_param.shape), pl.BlockSpec(memory_space=pl.ANY), _resident((1, HG_DV))]
                 + [row_block(wt) for wt in later_weights],
        out_specs=[prev_tile(D_ATT), prev_tile(D_HG), this_tile(D_MODEL), this_tile(D_MODEL),
                   prev_seq((HG_HEADS, HG_DK, HG_DV)), prev_seq((WINDOW, 2 * D_KV))]
                  + [small(c) for c in small_widths] + [row_block(wt) for wt in later_weights],
        out_shape=[jax.ShapeDtypeStruct((n, D_ATT), BF16), jax.ShapeDtypeStruct((n, D_HG), BF16),
                   jax.ShapeDtypeStruct((n, D_MODEL), BF16), jax.ShapeDtypeStruct((n, D_MODEL), BF16),
                   jax.ShapeDtypeStruct((bsz, HG_HEADS, HG_DK, HG_DV), F32),
                   jax.ShapeDtypeStruct((bsz, WINDOW, 2 * D_KV), F32)]
                  + [jax.ShapeDtypeStruct((rows, c), d) for c, d in zip(small_widths, small_dtypes)]
                  + [jax.ShapeDtypeStruct(wt.shape, BF16) for wt in later_weights],
        scratch_shapes=[pltpu.VMEM((D_MODEL, D_IN), BF16),
                        pltpu.VMEM((2, WEIGHT_SLAB_ROWS, D_IN), F32),
                        pltpu.SemaphoreType.DMA((2,)),
                        pltpu.VMEM((t, D_MODEL), BF16),
                        pltpu.VMEM((t, D_ATT), BF16), pltpu.VMEM((WINDOW + t, 2 * D_KV), F32),
                        pltpu.VMEM((t, D_HG), BF16), pltpu.VMEM((t, D_HG), BF16),
                        pltpu.VMEM((t, D_HG), F32), pltpu.VMEM((t, D_HG), BF16),
                        pltpu.VMEM((t, D_HG), BF16),
                        pltpu.VMEM((N_META, 2 * D_KV), F32),
                        pltpu.VMEM((HG_DV, D_HG), F32),
                        pltpu.VMEM((HG_DV, D_HG), F32)],
        compiler_params=_params("arbitrary"),
        name="mixers",
    )(sinks, x, x, x_sample, x_meta, g, lb_param, w_f32, hg_norm, *later_weights)


def _merge_ffn_rows(x, att, hg, sga, sgb, wa_ref, wb_ref, wo_ref, ln_ffn, wup_ref, wdn_ref, ln_f):
    ya = _dot(att.astype(BF16), wa_ref[...])
    yb = _dot(hg.astype(BF16), wb_ref[...])
    mix = sga.astype(F32) * ya + sgb.astype(F32) * yb
    h1 = x + _dot(mix.astype(BF16), wo_ref[...])
    xn = _rmsnorm(h1, ln_ffn).astype(BF16)
    acc = jnp.zeros_like(h1)
    step = 1024
    for c in range(D_FF // step):
        u = jnp.maximum(_dot(xn, wup_ref[:, c * step:(c + 1) * step]), 0.0)
        acc = acc + _dot((u * u).astype(BF16), wdn_ref[c * step:(c + 1) * step, :])
    return _rmsnorm(h1 + acc, ln_f)


def _sample_hgrn_group(rows, nw, q_ref, k_ref, i_ref, g_ref, s_ref, snew_ref, hg_ref):
    k = k_ref[rows, :]
    q = q_ref[rows, :]
    iv = i_ref[rows, :]
    r = lax.broadcasted_iota(jnp.int32, (HG_DK, HG_DK), 0)
    c = lax.broadcasted_iota(jnp.int32, (HG_DK, HG_DK), 1)
    eye = r == c

    def column(row):
        return jnp.sum(jnp.where(eye, row, 0.0), axis=1, keepdims=True)

    outs = []
    for b in range(s_ref.shape[0]):
        heads = []
        for h, hs in enumerate(_HG_SLICES):
            kc = column(k[b:b + 1, hs])
            qc = column(q[b:b + 1, hs])
            s_old = s_ref[b, h]
            s = s_old + kc * (iv[b:b + 1, hs] - s_old)
            snew_ref[b, h] = s
            heads.append(jnp.sum(qc * s, axis=0, keepdims=True))
        outs.append(jnp.concatenate(heads, axis=1))
    hg_ref[rows, :] = _hg_out(jnp.concatenate(outs, axis=0), g_ref[rows, :], nw)


def _merge_ffn_kernel(x_ref, att_ref, hg_ref, sga_ref, sgb_ref,
                      xs_ref, atts_ref, sgas_ref, sgbs_ref,
                      nw_ref, hqs_ref, hks_ref, his_ref, hgates_ref, state_ref,
                      wa_ref, wb_ref, wo_ref, wup_ref, wdn_ref, lnffn_ref, lnf_ref,
                      y_ref, ys_ref, snew_ref, hgs, *, prompt_steps):
    i = pl.program_id(0)
    weights = (wa_ref, wb_ref, wo_ref, lnffn_ref[...], wup_ref, wdn_ref, lnf_ref[...])
    group = state_ref.shape[0]
    ngroups = xs_ref.shape[0] // group

    @pl.when(i < prompt_steps)
    def _():
        y_ref[...] = _merge_ffn_rows(x_ref[...], att_ref[...], hg_ref[...], sga_ref[...], sgb_ref[...],
                                     *weights)
        rows = pl.ds(pl.multiple_of(jnp.minimum(i, ngroups - 1) * group, group), group)
        _sample_hgrn_group(rows, nw_ref[...], hqs_ref, hks_ref, his_ref, hgates_ref,
                           state_ref, snew_ref, hgs)

    @pl.when(i == prompt_steps)
    def _():
        ys_ref[...] = _merge_ffn_rows(xs_ref[...], atts_ref[...], hgs[...], sgas_ref[...],
                                      sgbs_ref[...], *weights)


def _merge_ffn(x, att, hg, sga, sgb, xs, att_s, sga_s, sgb_s, hg_norm, hq_s, hk_s, hi_s, hgate_s, state,
               wa, wb, wo, ln_ffn, w_up, w_down, ln_f):
    n = x.shape[0]
    nb = xs.shape[0]
    rows = MERGE_ROWS
    nt = n // rows
    g = SAMPLE_HG_GROUP
    assert nb // g <= nt

    def blk(c):
        return pl.BlockSpec((rows, c), lambda i: (jnp.minimum(i, nt - 1), 0))

    def sample(c):
        return pl.BlockSpec((nb, c), lambda i: (0, 0))

    sblk = pl.BlockSpec((g, HG_HEADS, HG_DK, HG_DV), lambda i: (jnp.minimum(i, nb // g - 1), 0, 0, 0))
    return pl.pallas_call(
        functools.partial(_merge_ffn_kernel, prompt_steps=nt),
        grid=(nt + 1,),
        in_specs=[blk(D_MODEL), blk(D_ATT), blk(D_HG), blk(D_MODEL), blk(D_MODEL),
                  sample(D_MODEL), sample(D_ATT), sample(D_MODEL), sample(D_MODEL),
                  _resident((1, HG_DV)), _resident(hq_s.shape), _resident(hk_s.shape),
                  _resident(hi_s.shape), _resident(hgate_s.shape), sblk,
                  _resident(wa.shape), _resident(wb.shape), _resident(wo.shape),
                  _resident(w_up.shape), _resident(w_down.shape),
                  _resident((1, D_MODEL)), _resident((1, D_MODEL))],
        out_specs=[blk(D_MODEL), sample(D_MODEL), sblk],
        out_shape=[jax.ShapeDtypeStruct((n, D_MODEL), F32), jax.ShapeDtypeStruct((nb, D_MODEL), F32),
                   jax.ShapeDtypeStruct(state.shape, F32)],
        scratch_shapes=[pltpu.VMEM((nb, D_HG), F32)],
        compiler_params=_params("arbitrary"),
        name="merge_ffn",
    )(x, att, hg, sga, sgb, xs, att_s, sga_s, sgb_s, hg_norm, hq_s, hk_s, hi_s, hgate_s, state,
      wa, wb, wo, w_up, w_down, ln_ffn, ln_f)


def _sample_attn_kernel(sink_ref, qm_ref, kvn_ref, ckt_ref, cvt_ref, mk_ref, mv_ref,
                        o_ref, nkt_ref, nvt_ref):
    nb = qm_ref.shape[0]
    head = lax.broadcasted_iota(jnp.int32, (Q_HEADS, 1), 0)
    sk = jnp.zeros((Q_HEADS, 1), F32)
    for j in range(Q_HEADS):
        sk = jnp.where(head == j, sink_ref[j], sk)
    newest = lax.broadcasted_iota(jnp.int32, (D_KV, WINDOW), 1) == WINDOW - 1
    kvn = kvn_ref[...]
    hi = kvn.astype(BF16)
    rest = kvn - hi.astype(F32)
    mid = rest.astype(BF16)
    lo = (rest - mid.astype(F32)).astype(BF16)
    r = lax.broadcasted_iota(jnp.int32, (2 * D_KV, 2 * D_KV), 0)
    c = lax.broadcasted_iota(jnp.int32, (2 * D_KV, 2 * D_KV), 1)
    eye = jnp.where(r == c, 1.0, 0.0).astype(BF16)
    pieces_t = _dot_nt(eye, jnp.concatenate([hi, mid, lo], axis=0)).astype(BF16)
    pr = lax.broadcasted_iota(jnp.int32, (3 * nb, WINDOW), 0)
    pc = lax.broadcasted_iota(jnp.int32, (3 * nb, WINDOW), 1)
    for b in range(nb):
        pick = jnp.logical_and(lax.rem(pr, nb) == b, pc == WINDOW - 1)
        new_cols = _dot(pieces_t, jnp.where(pick, 1.0, 0.0).astype(BF16))
        nkt_ref[b] = jnp.where(newest, new_cols[:D_KV], pltpu.roll(ckt_ref[b], WINDOW - 1, axis=1))
        nvt_ref[b] = jnp.where(newest, new_cols[D_KV:], pltpu.roll(cvt_ref[b], WINDOW - 1, axis=1))
    scores = [(_dot(qm_ref[b], nkt_ref[b].astype(BF16)),
               _dot_nt(qm_ref[b], mk_ref[b].astype(BF16)))
              for b in range(nb)]
    for b, (s_w, s_m) in enumerate(scores):
        m = jnp.maximum(jnp.maximum(jnp.max(s_w, axis=1, keepdims=True),
                                    jnp.max(s_m, axis=1, keepdims=True)), sk)
        e_w = jnp.exp(s_w - m)
        e_m = jnp.exp(s_m - m)
        l = (jnp.sum(e_w, axis=1, keepdims=True) + jnp.sum(e_m, axis=1, keepdims=True)
             + jnp.exp(sk - m))
        o = (_dot_nt(e_w.astype(BF16), nvt_ref[b].astype(BF16))
             + _dot(e_m.astype(BF16), mv_ref[b].astype(BF16)))
        o_ref[b] = o / l


def _sample_attention(sinks, qm, kv_new, ck, cv, mk, mv):
    nb = ck.shape[0]
    g = SAMPLE_ATT_GROUP

    def blk3(a, c):
        return pl.BlockSpec((g, a, c), lambda i: (i, 0, 0))

    return pl.pallas_call(
        _sample_attn_kernel,
        grid=(nb // g,),
        in_specs=[pl.BlockSpec(memory_space=pltpu.SMEM), blk3(Q_HEADS, D_KV),
                  pl.BlockSpec((g, 2 * D_KV), lambda i: (i, 0)),
                  blk3(D_KV, WINDOW), blk3(D_KV, WINDOW), blk3(N_META, D_KV), blk3(N_META, D_KV)],
        out_specs=[blk3(Q_HEADS, D_KV), blk3(D_KV, WINDOW), blk3(D_KV, WINDOW)],
        out_shape=[jax.ShapeDtypeStruct((nb, Q_HEADS, D_KV), F32),
                   jax.ShapeDtypeStruct((nb, D_KV, WINDOW), F32),
                   jax.ShapeDtypeStruct((nb, D_KV, WINDOW), F32)],
        compiler_params=_params("parallel"),
        name="sample_attn",
    )(sinks, qm, kv_new, ck, cv, mk, mv)


def kernel(x_prompt, x_sample, cache_k, cache_v, cache_meta_k, cache_meta_v, state_hgrn, meta,
           w_in, sinks, lb_param, hg_norm, w_att_out, w_hg_out, w_o, ln_mix, ln_ffn, w_up,
           w_down, ln_f):
    bsz, seq, _ = x_prompt.shape
    nb = x_sample.shape[0]
    ln_mix2 = ln_mix.reshape(1, D_MODEL)
    ln_ffn2 = ln_ffn.reshape(1, D_MODEL)
    ln_f2 = ln_f.reshape(1, D_MODEL)
    nw = hg_norm.reshape(1, HG_DV)

    xs = x_sample.reshape(nb, D_MODEL)

    xp = x_prompt.reshape(bsz * seq, D_MODEL)
    (att_p, hg_p, sga_p, sgb_p, state_p, lastkv_p,
     q_s, kv_s, hq_s, hk_s, hi_s, hgate_s, sga_s, sgb_s,
     wa, wb, wo, wup, wdn) = _mixers(
        sinks.reshape(Q_HEADS), xp, xs, meta, ln_mix2, lb_param, w_in[0], nw,
        (w_att_out[0], w_hg_out[0], w_o[0], w_up[0], w_down[0]), bsz)

    grp = Q_HEADS // KV_HEADS
    rows = nb + N_META
    qs4 = q_s.reshape(rows, KV_HEADS, grp, 1, HEAD_DIM)
    sel = jnp.eye(KV_HEADS, dtype=BF16).reshape(1, KV_HEADS, 1, KV_HEADS, 1)
    qm = (qs4 * sel).reshape(rows, Q_HEADS, D_KV)

    def window_t(c):
        return jnp.swapaxes(c[0].reshape(nb, WINDOW, D_KV), 1, 2)

    o_s, nkt_s, nvt_s = _sample_attention(
        sinks.reshape(Q_HEADS), qm, kv_s, window_t(cache_k), window_t(cache_v),
        cache_meta_k[0].reshape(nb, N_META, D_KV), cache_meta_v[0].reshape(nb, N_META, D_KV))
    o5 = o_s.reshape(nb, KV_HEADS, grp, KV_HEADS, HEAD_DIM)
    att_s = jnp.stack([o5[:, h, :, h, :] for h in range(KV_HEADS)], axis=1).reshape(nb, D_ATT)

    y_p, y_s, state_s = _merge_ffn(xp, att_p, hg_p, sga_p, sgb_p, xs, att_s, sga_s, sgb_s,
                                   nw, hq_s, hk_s, hi_s, hgate_s, state_hgrn[0],
                                   wa, wb, wo, ln_ffn2, wup, wdn, ln_f2)

    kv5 = lastkv_p.reshape(bsz, WINDOW, 2, KV_HEADS, HEAD_DIM)
    meta5 = jnp.broadcast_to(kv_s[nb:].reshape(1, N_META, 2, KV_HEADS, HEAD_DIM),
                             (bsz, N_META, 2, KV_HEADS, HEAD_DIM))
    return (y_p.reshape(bsz, seq, D_MODEL),
            y_s.reshape(nb, 1, D_MODEL),
            kv5[None, :, :, 0],
            kv5[None, :, :, 1],
            meta5[None, :, :, 0],
            meta5[None, :, :, 1],
            state_p[None],
            jnp.swapaxes(nkt_s, 1, 2).reshape(1, nb, WINDOW, KV_HEADS, HEAD_DIM),
            jnp.swapaxes(nvt_s, 1, 2).reshape(1, nb, WINDOW, KV_HEADS, HEAD_DIM),
            state_s[None])
```

```python
import functools

import jax
import jax.numpy as jnp
from jax import lax
from jax.experimental import pallas as pl
from jax.experimental.pallas import tpu as pltpu

F32 = jnp.float32
BF16 = jnp.bfloat16

D_MODEL = 1024
N_META = 16
WINDOW = 128
HEAD_DIM = 64
Q_HEADS = 8
KV_HEADS = 2
D_ATT = Q_HEADS * HEAD_DIM
D_KV = KV_HEADS * HEAD_DIM
HG_HEADS = 4
HG_DK = 128
HG_DV = 128
D_HG = HG_HEADS * HG_DK
HG_CHUNK = 64
D_FF = 4 * D_MODEL
EPS = 1e-6
C_Q = 0
C_KV = C_Q + D_ATT
C_HQ = C_KV + 2 * D_KV
C_HF = C_HQ + D_HG
C_HI = C_HF + D_HG
C_HGATE = C_HI + D_HG
C_GA = C_HGATE + D_HG
C_GB = C_GA + D_MODEL
D_IN = C_GB + D_MODEL

VMEM_LIMIT_BYTES = 56 * 1024 * 1024
MIX_ROWS = 512
PROJ_GROUP = 256
WEIGHT_SLAB_ROWS = 64
MERGE_ROWS = 512
SAMPLE_ATT_GROUP = 16
SAMPLE_HG_GROUP = 8

_NT = (((1,), (1,)), ((), ()))
_TN = (((0,), (0,)), ((), ()))


def _dot(a, b):
    return jnp.dot(a, b, preferred_element_type=F32)


def _dot_nt(a, b):
    return lax.dot_general(a, b, _NT, preferred_element_type=F32)


def _dot_tn(a, b):
    return lax.dot_general(a, b, _TN, preferred_element_type=F32)


def _rmsnorm(x, g):
    return x * lax.rsqrt(jnp.mean(x * x, axis=-1, keepdims=True) + EPS) * g


def _resident(shape):
    return pl.BlockSpec(shape, lambda *_: (0,) * len(shape), pipeline_mode=pl.Buffered(1))


def _params(*sem):
    return pltpu.CompilerParams(dimension_semantics=sem, vmem_limit_bytes=VMEM_LIMIT_BYTES)


def _lower_bound(lbp):
    e = jnp.exp(lbp - jnp.max(lbp, axis=0, keepdims=True))
    return e[0:1] / jnp.sum(e, axis=0, keepdims=True)


class _Projection:
    def __init__(self, xn, lbp, w_ref):
        self.xn = xn
        self.w_ref = w_ref
        self.lb = _lower_bound(lbp)

    def cols(self, base, part):
        return _dot(self.xn[...], self.w_ref[:, base + part.start:base + part.stop])

    def q_att(self, part=slice(0, D_ATT)):
        return self.cols(C_Q, part) * (HEAD_DIM ** -0.5)

    def kv(self):
        return self.cols(C_KV, slice(0, 2 * D_KV))

    def q_hg(self, part=slice(0, D_HG)):
        return self.cols(C_HQ, part) * (HG_DK ** -0.5)

    def forget(self, part=slice(0, D_HG)):
        lb = self.lb[:, part]
        f = lb + (1.0 - lb) * jax.nn.sigmoid(self.cols(C_HF, part))
        return 1.0 - f, jnp.log(f)

    def i_hg(self, part=slice(0, D_HG)):
        return self.cols(C_HI, part)

    def swish_gate(self, part=slice(0, D_HG)):
        g = self.cols(C_HGATE, part)
        return g * jax.nn.sigmoid(g)

    def branch_gate(self, base, part=slice(0, D_MODEL)):
        return jax.nn.sigmoid(self.cols(base, part))


class _Attention:
    def __init__(self, sink_ref, q_ref, kv, first, o_ref):
        self.sink_ref, self.q_ref, self.first, self.o_ref = sink_ref, q_ref, first, o_ref
        self.nsub = q_ref.shape[0] // WINDOW
        self.nk = 2 * WINDOW + N_META
        self.meta0 = WINDOW + q_ref.shape[0]
        lane = lax.broadcasted_iota(jnp.int32, (kv.shape[0], D_KV), 1)
        low = lane < HEAD_DIM
        k = kv[:, :D_KV]
        ksw = pltpu.roll(k, HEAD_DIM, axis=1)
        self.kboth = (jnp.where(low, k, ksw).astype(BF16), jnp.where(low, ksw, k).astype(BF16))
        qlane = lax.broadcasted_iota(jnp.int32, (1, 128), 1)
        self.keep = (jnp.where(qlane < HEAD_DIM, 1.0, 0.0).astype(BF16),
                     jnp.where(qlane < HEAD_DIM, 0.0, 1.0).astype(BF16))
        r = lax.broadcasted_iota(jnp.int32, (D_KV, D_KV), 0)
        c = lax.broadcasted_iota(jnp.int32, (D_KV, D_KV), 1)
        eye = jnp.where(r == c, 1.0, 0.0).astype(BF16)
        self.vt = _dot_nt(eye, kv[:, D_KV:].astype(BF16)).astype(BF16)
        self.ones = jnp.ones((16, self.nk), BF16)
        self.key = lax.broadcasted_iota(jnp.int32, (self.nk, 2 * WINDOW), 0)
        col = lax.broadcasted_iota(jnp.int32, (self.nk, 2 * WINDOW), 1)
        self.qry = jnp.bitwise_and(col, WINDOW - 1)
        self.first_head = lax.broadcasted_iota(jnp.int32, (1, 2 * WINDOW), 1) < WINDOW
        self.scores = {}

    def issue_scores(self, i):
        r0 = i * WINDOW
        out = []
        for p in range(Q_HEADS // 2):
            kb = self.kboth[(2 * p) // (Q_HEADS // KV_HEADS)]
            kmat = jnp.concatenate([kb[r0:r0 + 2 * WINDOW], kb[self.meta0:]], axis=0)
            qp = self.q_ref[r0:r0 + WINDOW, p * 128:(p + 1) * 128]
            q2 = jnp.concatenate([qp * self.keep[0], qp * self.keep[1]], axis=0)
            out.append(_dot_nt(kmat, q2))
        self.scores[i] = out

    def finish(self, i):
        r0 = i * WINDOW
        lo = jnp.where(self.first, WINDOW - 1, self.qry) if i == 0 else self.qry
        mask = jnp.logical_and(self.key > lo, self.key <= self.qry + WINDOW)
        mask = jnp.logical_or(mask, self.key >= 2 * WINDOW)
        vaug = []
        for h in range(KV_HEADS):
            vth = self.vt[h * HEAD_DIM:(h + 1) * HEAD_DIM]
            vaug.append(jnp.concatenate(
                [jnp.concatenate([vth[:, r0:r0 + 2 * WINDOW], vth[:, self.meta0:]], axis=1),
                 self.ones], axis=0))
        outs = []
        for p, raw in enumerate(self.scores.pop(i)):
            h = (2 * p) // (Q_HEADS // KV_HEADS)
            s = jnp.where(mask, raw, -jnp.inf)
            sk = jnp.where(self.first_head, self.sink_ref[2 * p], self.sink_ref[2 * p + 1])
            m = jnp.maximum(jnp.max(s, axis=0, keepdims=True), sk)
            e = jnp.exp(s - m).astype(BF16)
            oa = _dot(vaug[h], e)
            o = oa[:HEAD_DIM] / (oa[HEAD_DIM:HEAD_DIM + 1] + jnp.exp(sk - m))
            outs += [o[:, :WINDOW], o[:, WINDOW:]]
        self.o_ref[r0:r0 + WINDOW, :] = jnp.concatenate(outs, axis=0).T.astype(self.o_ref.dtype)


def _cumsum_rows(x):
    t = x.shape[0]
    row = lax.broadcasted_iota(jnp.int32, x.shape, 0)
    d = 1
    while d < t:
        x = x + jnp.where(row >= d, pltpu.roll(x, d, axis=0), 0.0)
        d *= 2
    return x


def _hg_out(o, gate, nw):
    parts = []
    for h in range(HG_HEADS):
        oh = o[:, h * HG_DV:(h + 1) * HG_DV]
        parts.append(oh * lax.rsqrt(jnp.mean(oh * oh, axis=-1, keepdims=True) + EPS) * nw)
    return jnp.concatenate(parts, axis=1) * gate


_HG_SLICES = [slice(h * HG_DK, (h + 1) * HG_DK) for h in range(HG_HEADS)]


def _state_update(iv, kd):
    return jnp.concatenate([_dot_tn(iv[:, hs], kd[:, hs]) for hs in _HG_SLICES], axis=1)


def _meta_state(mk, mlogf, mi):
    b = _cumsum_rows(mlogf)
    kd = (mk * jnp.exp(b[-1:] - b)).astype(BF16)
    return _state_update(mi.astype(BF16), kd)


class _Hgrn:
    def __init__(self, q_ref, k_ref, logf_ref, i_ref, g_ref, nw, state, o_ref):
        self.q_ref, self.k_ref, self.logf_ref, self.i_ref, self.g_ref = q_ref, k_ref, logf_ref, i_ref, g_ref
        self.nw, self.state, self.o_ref = nw, state, o_ref
        t = HG_CHUNK
        self.nchunk = q_ref.shape[0] // t
        r = lax.broadcasted_iota(jnp.int32, (t, t), 0)
        c = lax.broadcasted_iota(jnp.int32, (t, t), 1)
        self.causal = r >= c
        self.cums, self.terms, self.entering = {}, {}, {}

    def rows(self, ci):
        return slice(ci * HG_CHUNK, (ci + 1) * HG_CHUNK)

    def issue_cumsum(self, ci):
        self.cums[ci] = _cumsum_rows(self.logf_ref[self.rows(ci), :])

    def issue_local(self, ci):
        rows = self.rows(ci)
        b = self.cums.pop(ci)
        bl = b[-1:]
        k = self.k_ref[rows, :].astype(F32)
        qe = (self.q_ref[rows, :].astype(F32) * jnp.exp(b)).astype(BF16)
        ke = (k * jnp.exp(-b)).astype(BF16)
        kd = (k * jnp.exp(bl - b)).astype(BF16)
        iv = self.i_ref[rows, :].astype(BF16)
        a = [jnp.where(self.causal, _dot_nt(qe[:, hs], ke[:, hs]), 0.0).astype(BF16) for hs in _HG_SLICES]
        self.terms[ci] = (qe, iv, a)
        self.entering[ci] = self.state.astype(BF16)
        self.state = self.state * jnp.exp(bl) + _state_update(iv, kd)

    def finish(self, ci):
        rows = self.rows(ci)
        qe, iv, a = self.terms.pop(ci)
        ent = self.entering.pop(ci)
        outs = [_dot(a[h], iv[:, hs]) + _dot_nt(qe[:, hs], ent[:, hs]) for h, hs in enumerate(_HG_SLICES)]
        o = _hg_out(jnp.concatenate(outs, axis=1), self.g_ref[rows, :].astype(F32), self.nw)
        self.o_ref[rows, :] = o.astype(self.o_ref.dtype)


def _load_as_bf16(w_hbm, w_vmem, stage, sem):
    rows = stage.shape[1]
    nslab = w_hbm.shape[0] // rows

    def slab_copy(c):
        return pltpu.make_async_copy(w_hbm.at[pl.ds(c * rows, rows), :], stage.at[c % 2], sem.at[c % 2])

    slab_copy(0).start()
    for c in range(nslab):
        if c + 1 < nslab:
            slab_copy(c + 1).start()
        slab_copy(c).wait()
        w_vmem[pl.ds(c * rows, rows), :] = stage[c % 2].astype(w_vmem.dtype)


def _mixer_kernel(sink_ref, x0_ref, xnext_ref, xs_ref, xm_ref, g_ref, lbp_ref, w_hbm, nw_ref,
                  wa32, wb32, wo32, wup32, wdn32,
                  att_ref, hg_ref, sga_ref, sgb_ref, sfin_ref, lastkv_ref,
                  qs_ref, kvs_ref, hqs_ref, hks_ref, his_ref, hgs_ref, sgas_ref, sgbs_ref,
                  wa16, wb16, wo16, wup16, wdn16,
                  w_ref, stage, sem, xn_s, zq, zkv, zhq, zhk, zlogf, zhi, zhg, kvm_ref, st_ref, mst_ref,
                  *, tiles_per_seq):
    s = pl.program_id(0)
    t = xnext_ref.shape[0]

    @pl.when(s == 0)
    def _():
        _load_as_bf16(w_hbm, w_ref, stage, sem)
        xn_s[...] = _rmsnorm(x0_ref[...], g_ref[...]).astype(xn_s.dtype)
        for ref in (zq, zkv, zhq, zhk, zlogf, zhi, zhg, st_ref):
            ref[...] = jnp.zeros(ref.shape, ref.dtype)
        nb = xs_ref.shape[0]
        small = jnp.concatenate([xs_ref[...], xm_ref[...]], axis=0)
        p = _Projection(_rmsnorm(small, g_ref[...]).astype(BF16), lbp_ref[...], w_ref)
        qs_ref[...] = p.q_att().astype(qs_ref.dtype)
        kv = p.kv()
        kvs_ref[...] = kv
        kvm_ref[...] = kv[nb:]
        hqs_ref[...] = p.q_hg()
        k, logf = p.forget()
        hks_ref[...] = k
        iv = p.i_hg()
        his_ref[...] = iv
        hgs_ref[...] = p.swish_gate()
        sgas_ref[...] = p.branch_gate(C_GA).astype(sgas_ref.dtype)
        sgbs_ref[...] = p.branch_gate(C_GB).astype(sgbs_ref.dtype)
        mst_ref[...] = _meta_state(k[nb:], logf[nb:], iv[nb:])

    first = lax.rem(jnp.maximum(s - 1, 0), tiles_per_seq) == 0

    att = _Attention(sink_ref, zq, jnp.concatenate([zkv[...], kvm_ref[...]], axis=0), first, att_ref)
    last = zkv[t:t + WINDOW, :]
    lastkv_ref[...] = last
    zkv[0:WINDOW, :] = last
    entering = jnp.where(first, mst_ref[...], st_ref[...])
    hg = _Hgrn(zhq, zhk, zlogf, zhi, zhg, nw_ref[...], entering, hg_ref)
    proj = _Projection(xn_s, lbp_ref[...], w_ref)

    def gate_a(p):
        sga_ref[:, p] = proj.branch_gate(C_GA, p).astype(sga_ref.dtype)

    def gate_b(p):
        sgb_ref[:, p] = proj.branch_gate(C_GB, p).astype(sgb_ref.dtype)

    def new_kv(_):
        zkv[WINDOW:WINDOW + t, :] = proj.kv()

    def new_forget(p):
        k_new, logf_new = proj.forget(p)
        zhk[:, p] = k_new.astype(zhk.dtype)
        zlogf[:, p] = logf_new

    def new_q(p):
        zq[:, p] = proj.q_att(p).astype(zq.dtype)

    def new_hq(p):
        zhq[:, p] = proj.q_hg(p).astype(zhq.dtype)

    def new_hi(p):
        zhi[:, p] = proj.i_hg(p).astype(zhi.dtype)

    def new_gate(p):
        zhg[:, p] = proj.swish_gate(p).astype(zhg.dtype)

    def parts(n):
        return [slice(c, c + PROJ_GROUP) for c in range(0, n, PROJ_GROUP)]

    groups = ([(gate_a, p) for p in parts(D_MODEL)] + [(gate_b, p) for p in parts(D_MODEL)]
              + [(new_kv, None)] + [(new_q, p) for p in parts(D_ATT)]
              + [(new_forget, p) for p in parts(D_HG)] + [(new_hq, p) for p in parts(D_HG)]
              + [(new_hi, p) for p in parts(D_HG)] + [(new_gate, p) for p in parts(D_HG)])

    def save_state():
        st_ref[...] = hg.state
        for h, hs in enumerate(_HG_SLICES):
            sfin_ref[h] = hg.state[:, hs].T

    qk, cs, loc, fin, out = att.issue_scores, hg.issue_cumsum, hg.issue_local, att.finish, hg.finish
    mixer_work = {
        0: [(qk, 0), (cs, 0), (cs, 1), (cs, 2), (cs, 3)],
        1: [(cs, 4), (cs, 5), (loc, 0)],
        2: [(cs, 6), (cs, 7), (loc, 1)],
        3: [(qk, 1), (loc, 2)],
        4: [(fin, 0), (loc, 3)],
        5: [(out, 0), (loc, 4)],
        6: [(qk, 2), (out, 1), (loc, 5)],
        7: [(fin, 1), (out, 2), (loc, 6)],
        8: [(qk, 3), (out, 3), (loc, 7), (save_state,)],
        9: [(out, 4)],
        10: [(fin, 2), (out, 5)],
        11: [(out, 6)],
        12: [(out, 7)],
        13: [(fin, 3)],
    }
    for slot, (fn, p) in enumerate(groups):
        fn(p)
        for item in mixer_work.get(slot, []):
            item[0](*item[1:])
    xn_s[...] = _rmsnorm(xnext_ref[...], g_ref[...]).astype(xn_s.dtype)
    for src, dst in ((wa32, wa16), (wb32, wb16), (wo32, wo16), (wup32, wup16), (wdn32, wdn16)):
        dst[...] = src[...].astype(dst.dtype)


def _mixers(sinks, x, x_sample, x_meta, g, lb_param, w_f32, hg_norm, later_weights, bsz):
    n = x.shape[0]
    t = MIX_ROWS
    nt = n // t
    per_seq = nt // bsz
    rows = x_sample.shape[0] + x_meta.shape[0]

    def this_tile(c):
        return pl.BlockSpec((t, c), lambda s: (jnp.minimum(s, nt - 1), 0))

    def prev_tile(c):
        return pl.BlockSpec((t, c), lambda s: (jnp.maximum(s - 1, 0), 0))

    def prev_seq(shape):
        return pl.BlockSpec((None,) + shape,
                            lambda s: (jnp.maximum(s - 1, 0) // per_seq,) + (0,) * len(shape))

    def small(c):
        return pl.BlockSpec((rows, c), lambda s: (0, 0))

    first_tile = pl.BlockSpec((t, D_MODEL), lambda s: (0, 0), pipeline_mode=pl.Buffered(1))
    next_tile = pl.BlockSpec((t, D_MODEL), lambda s: (jnp.minimum(s + 1, nt - 1), 0))
    small_widths = (D_ATT, 2 * D_KV, D_HG, D_HG, D_HG, D_HG, D_MODEL, D_MODEL)
    small_dtypes = (BF16, F32, F32, F32, F32, F32, BF16, BF16)

    def row_block(wt):
        return pl.BlockSpec((wt.shape[0] // nt, wt.shape[1]), lambda s: (jnp.minimum(s, nt - 1), 0))

    return pl.pallas_call(
        functools.partial(_mixer_kernel, tiles_per_seq=per_seq),
        grid=(nt + 1,),
        in_specs=[pl.BlockSpec(memory_space=pltpu.SMEM), first_tile, next_tile,
                  _resident(x_sample.shape), _resident(x_meta.shape), _resident((1, D_MODEL)),
                  _resident(lb_param.shape), pl.BlockSpec(memory_space=pl.ANY), _resident((1, HG_DV))]
                 + [row_block(wt) for wt in later_weights],
        out_specs=[prev_tile(D_ATT), prev_tile(D_HG), this_tile(D_MODEL), this_tile(D_MODEL),
                   prev_seq((HG_HEADS, HG_DK, HG_DV)), prev_seq((WINDOW, 2 * D_KV))]
                  + [small(c) for c in small_widths] + [row_block(wt) for wt in later_weights],
        out_shape=[jax.ShapeDtypeStruct((n, D_ATT), BF16), jax.ShapeDtypeStruct((n, D_HG), BF16),
                   jax.ShapeDtypeStruct((n, D_MODEL), BF16), jax.ShapeDtypeStruct((n, D_MODEL), BF16),
                   jax.ShapeDtypeStruct((bsz, HG_HEADS, HG_DK, HG_DV), F32),
                   jax.ShapeDtypeStruct((bsz, WINDOW, 2 * D_KV), F32)]
                  + [jax.ShapeDtypeStruct((rows, c), d) for c, d in zip(small_widths, small_dtypes)]
                  + [jax.ShapeDtypeStruct(wt.shape, BF16) for wt in later_weights],
        scratch_shapes=[pltpu.VMEM((D_MODEL, D_IN), BF16),
                        pltpu.VMEM((2, WEIGHT_SLAB_ROWS, D_IN), F32),
                        pltpu.SemaphoreType.DMA((2,)),
                        pltpu.VMEM((t, D_MODEL), BF16),
                        pltpu.VMEM((t, D_ATT), BF16), pltpu.VMEM((WINDOW + t, 2 * D_KV), F32),
                        pltpu.VMEM((t, D_HG), BF16), pltpu.VMEM((t, D_HG), BF16),
                        pltpu.VMEM((t, D_HG), F32), pltpu.VMEM((t, D_HG), BF16),
                        pltpu.VMEM((t, D_HG), BF16),
                        pltpu.VMEM((N_META, 2 * D_KV), F32),
                        pltpu.VMEM((HG_DV, D_HG), F32),
                        pltpu.VMEM((HG_DV, D_HG), F32)],
        compiler_params=_params("arbitrary"),
        name="mixers",
    )(sinks, x, x, x_sample, x_meta, g, lb_param, w_f32, hg_norm, *later_weights)


def _merge_ffn_rows(x, att, hg, sga, sgb, wa_ref, wb_ref, wo_ref, ln_ffn, wup_ref, wdn_ref):
    ya = _dot(att.astype(BF16), wa_ref[...])
    yb = _dot(hg.astype(BF16), wb_ref[...])
    mix = sga.astype(F32) * ya + sgb.astype(F32) * yb
    h1 = x + _dot(mix.astype(BF16), wo_ref[...])
    xn = _rmsnorm(h1, ln_ffn).astype(BF16)
    acc = jnp.zeros_like(h1)
    step = 1024
    for c in range(D_FF // step):
        u = jnp.maximum(_dot(xn, wup_ref[:, c * step:(c + 1) * step]), 0.0)
        acc = acc + _dot((u * u).astype(BF16), wdn_ref[c * step:(c + 1) * step, :])
    return h1 + acc


def _sample_hgrn_group(rows, nw, q_ref, k_ref, i_ref, g_ref, s_ref, snew_ref, hg_ref):
    k = k_ref[rows, :]
    q = q_ref[rows, :]
    iv = i_ref[rows, :]
    r = lax.broadcasted_iota(jnp.int32, (HG_DK, HG_DK), 0)
    c = lax.broadcasted_iota(jnp.int32, (HG_DK, HG_DK), 1)
    eye = r == c

    def column(row):
        return jnp.sum(jnp.where(eye, row, 0.0), axis=1, keepdims=True)

    outs = []
    for b in range(s_ref.shape[0]):
        heads = []
        for h, hs in enumerate(_HG_SLICES):
            kc = column(k[b:b + 1, hs])
            qc = column(q[b:b + 1, hs])
            s_old = s_ref[b, h]
            s = s_old + kc * (iv[b:b + 1, hs] - s_old)
            snew_ref[b, h] = s
            heads.append(jnp.sum(qc * s, axis=0, keepdims=True))
        outs.append(jnp.concatenate(heads, axis=1))
    hg_ref[rows, :] = _hg_out(jnp.concatenate(outs, axis=0), g_ref[rows, :], nw)


def _merge_ffn_kernel(x_ref, att_ref, hg_ref, sga_ref, sgb_ref,
                      xs_ref, atts_ref, sgas_ref, sgbs_ref,
                      nw_ref, hqs_ref, hks_ref, his_ref, hgates_ref, state_ref,
                      wa_ref, wb_ref, wo_ref, wup_ref, wdn_ref, lnffn_ref, lnf_ref,
                      y_ref, ys_ref, snew_ref, hgs, h_last, *, prompt_steps):
    i = pl.program_id(0)
    weights = (wa_ref, wb_ref, wo_ref, lnffn_ref[...], wup_ref, wdn_ref)
    group = state_ref.shape[0]
    ngroups = xs_ref.shape[0] // group

    @pl.when(i == 0)
    def _():
        h_last[...] = jnp.zeros(h_last.shape, h_last.dtype)

    @pl.when(i < prompt_steps)
    def _():
        y_ref[...] = _rmsnorm(h_last[...], lnf_ref[...])
        h_last[...] = _merge_ffn_rows(x_ref[...], att_ref[...], hg_ref[...], sga_ref[...], sgb_ref[...],
                                      *weights)
        rows = pl.ds(pl.multiple_of(jnp.minimum(i, ngroups - 1) * group, group), group)
        _sample_hgrn_group(rows, nw_ref[...], hqs_ref, hks_ref, his_ref, hgates_ref,
                           state_ref, snew_ref, hgs)

    @pl.when(i == prompt_steps)
    def _():
        y_ref[...] = _rmsnorm(h_last[...], lnf_ref[...])
        ys_ref[...] = _rmsnorm(_merge_ffn_rows(xs_ref[...], atts_ref[...], hgs[...], sgas_ref[...],
                                               sgbs_ref[...], *weights), lnf_ref[...])


def _merge_ffn(x, att, hg, sga, sgb, xs, att_s, sga_s, sgb_s, hg_norm, hq_s, hk_s, hi_s, hgate_s, state,
               wa, wb, wo, ln_ffn, w_up, w_down, ln_f):
    n = x.shape[0]
    nb = xs.shape[0]
    rows = MERGE_ROWS
    nt = n // rows
    g = SAMPLE_HG_GROUP
    assert nb // g <= nt

    def blk(c):
        return pl.BlockSpec((rows, c), lambda i: (jnp.minimum(i, nt - 1), 0))

    def sample(c):
        return pl.BlockSpec((nb, c), lambda i: (0, 0))

    sblk = pl.BlockSpec((g, HG_HEADS, HG_DK, HG_DV), lambda i: (jnp.minimum(i, nb // g - 1), 0, 0, 0))
    return pl.pallas_call(
        functools.partial(_merge_ffn_kernel, prompt_steps=nt),
        grid=(nt + 1,),
        in_specs=[blk(D_MODEL), blk(D_ATT), blk(D_HG), blk(D_MODEL), blk(D_MODEL),
                  sample(D_MODEL), sample(D_ATT), sample(D_MODEL), sample(D_MODEL),
                  _resident((1, HG_DV)), _resident(hq_s.shape), _resident(hk_s.shape),
                  _resident(hi_s.shape), _resident(hgate_s.shape), sblk,
                  _resident(wa.shape), _resident(wb.shape), _resident(wo.shape),
                  _resident(w_up.shape), _resident(w_down.shape),
                  _resident((1, D_MODEL)), _resident((1, D_MODEL))],
        out_specs=[pl.BlockSpec((rows, D_MODEL), lambda i: (jnp.maximum(i - 1, 0), 0)),
                   sample(D_MODEL), sblk],
        out_shape=[jax.ShapeDtypeStruct((n, D_MODEL), F32), jax.ShapeDtypeStruct((nb, D_MODEL), F32),
                   jax.ShapeDtypeStruct(state.shape, F32)],
        scratch_shapes=[pltpu.VMEM((nb, D_HG), F32), pltpu.VMEM((rows, D_MODEL), F32)],
        compiler_params=_params("arbitrary"),
        name="merge_ffn",
    )(x, att, hg, sga, sgb, xs, att_s, sga_s, sgb_s, hg_norm, hq_s, hk_s, hi_s, hgate_s, state,
      wa, wb, wo, w_up, w_down, ln_ffn, ln_f)


def _sample_attn_kernel(sink_ref, qm_ref, kvn_ref, ckt_ref, cvt_ref, mk_ref, mv_ref,
                        o_ref, nkt_ref, nvt_ref):
    nb = qm_ref.shape[0]
    head = lax.broadcasted_iota(jnp.int32, (Q_HEADS, 1), 0)
    sk = jnp.zeros((Q_HEADS, 1), F32)
    for j in range(Q_HEADS):
        sk = jnp.where(head == j, sink_ref[j], sk)
    newest = lax.broadcasted_iota(jnp.int32, (D_KV, WINDOW), 1) == WINDOW - 1
    kvn = kvn_ref[...]
    hi = kvn.astype(BF16)
    rest = kvn - hi.astype(F32)
    mid = rest.astype(BF16)
    lo = (rest - mid.astype(F32)).astype(BF16)
    r = lax.broadcasted_iota(jnp.int32, (2 * D_KV, 2 * D_KV), 0)
    c = lax.broadcasted_iota(jnp.int32, (2 * D_KV, 2 * D_KV), 1)
    eye = jnp.where(r == c, 1.0, 0.0).astype(BF16)
    pieces_t = _dot_nt(eye, jnp.concatenate([hi, mid, lo], axis=0)).astype(BF16)
    pr = lax.broadcasted_iota(jnp.int32, (3 * nb, WINDOW), 0)
    pc = lax.broadcasted_iota(jnp.int32, (3 * nb, WINDOW), 1)
    for b in range(nb):
        pick = jnp.logical_and(lax.rem(pr, nb) == b, pc == WINDOW - 1)
        new_cols = _dot(pieces_t, jnp.where(pick, 1.0, 0.0).astype(BF16))
        nkt_ref[b] = jnp.where(newest, new_cols[:D_KV], pltpu.roll(ckt_ref[b], WINDOW - 1, axis=1))
        nvt_ref[b] = jnp.where(newest, new_cols[D_KV:], pltpu.roll(cvt_ref[b], WINDOW - 1, axis=1))
    scores = [(_dot(qm_ref[b], nkt_ref[b].astype(BF16)),
               _dot_nt(qm_ref[b], mk_ref[b].astype(BF16)))
              for b in range(nb)]
    for b, (s_w, s_m) in enumerate(scores):
        m = jnp.maximum(jnp.maximum(jnp.max(s_w, axis=1, keepdims=True),
                                    jnp.max(s_m, axis=1, keepdims=True)), sk)
        e_w = jnp.exp(s_w - m)
        e_m = jnp.exp(s_m - m)
        l = (jnp.sum(e_w, axis=1, keepdims=True) + jnp.sum(e_m, axis=1, keepdims=True)
             + jnp.exp(sk - m))
        o = (_dot_nt(e_w.astype(BF16), nvt_ref[b].astype(BF16))
             + _dot(e_m.astype(BF16), mv_ref[b].astype(BF16)))
        o_ref[b] = o / l


def _sample_attention(sinks, qm, kv_new, ck, cv, mk, mv):
    nb = ck.shape[0]
    g = SAMPLE_ATT_GROUP

    def blk3(a, c):
        return pl.BlockSpec((g, a, c), lambda i: (i, 0, 0))

    return pl.pallas_call(
        _sample_attn_kernel,
        grid=(nb // g,),
        in_specs=[pl.BlockSpec(memory_space=pltpu.SMEM), blk3(Q_HEADS, D_KV),
                  pl.BlockSpec((g, 2 * D_KV), lambda i: (i, 0)),
                  blk3(D_KV, WINDOW), blk3(D_KV, WINDOW), blk3(N_META, D_KV), blk3(N_META, D_KV)],
        out_specs=[blk3(Q_HEADS, D_KV), blk3(D_KV, WINDOW), blk3(D_KV, WINDOW)],
        out_shape=[jax.ShapeDtypeStruct((nb, Q_HEADS, D_KV), F32),
                   jax.ShapeDtypeStruct((nb, D_KV, WINDOW), F32),
                   jax.ShapeDtypeStruct((nb, D_KV, WINDOW), F32)],
        compiler_params=_params("parallel"),
        name="sample_attn",
    )(sinks, qm, kv_new, ck, cv, mk, mv)


def kernel(x_prompt, x_sample, cache_k, cache_v, cache_meta_k, cache_meta_v, state_hgrn, meta,
           w_in, sinks, lb_param, hg_norm, w_att_out, w_hg_out, w_o, ln_mix, ln_ffn, w_up,
           w_down, ln_f):
    bsz, seq, _ = x_prompt.shape
    nb = x_sample.shape[0]
    ln_mix2 = ln_mix.reshape(1, D_MODEL)
    ln_ffn2 = ln_ffn.reshape(1, D_MODEL)
    ln_f2 = ln_f.reshape(1, D_MODEL)
    nw = hg_norm.reshape(1, HG_DV)

    xs = x_sample.reshape(nb, D_MODEL)

    xp = x_prompt.reshape(bsz * seq, D_MODEL)
    (att_p, hg_p, sga_p, sgb_p, state_p, lastkv_p,
     q_s, kv_s, hq_s, hk_s, hi_s, hgate_s, sga_s, sgb_s,
     wa, wb, wo, wup, wdn) = _mixers(
        sinks.reshape(Q_HEADS), xp, xs, meta, ln_mix2, lb_param, w_in[0], nw,
        (w_att_out[0], w_hg_out[0], w_o[0], w_up[0], w_down[0]), bsz)

    grp = Q_HEADS // KV_HEADS
    rows = nb + N_META
    qs4 = q_s.reshape(rows, KV_HEADS, grp, 1, HEAD_DIM)
    sel = jnp.eye(KV_HEADS, dtype=BF16).reshape(1, KV_HEADS, 1, KV_HEADS, 1)
    qm = (qs4 * sel).reshape(rows, Q_HEADS, D_KV)

    def window_t(c):
        return jnp.swapaxes(c[0].reshape(nb, WINDOW, D_KV), 1, 2)

    o_s, nkt_s, nvt_s = _sample_attention(
        sinks.reshape(Q_HEADS), qm, kv_s, window_t(cache_k), window_t(cache_v),
        cache_meta_k[0].reshape(nb, N_META, D_KV), cache_meta_v[0].reshape(nb, N_META, D_KV))
    o5 = o_s.reshape(nb, KV_HEADS, grp, KV_HEADS, HEAD_DIM)
    att_s = jnp.stack([o5[:, h, :, h, :] for h in range(KV_HEADS)], axis=1).reshape(nb, D_ATT)

    y_p, y_s, state_s = _merge_ffn(xp, att_p, hg_p, sga_p, sgb_p, xs, att_s, sga_s, sgb_s,
                                   nw, hq_s, hk_s, hi_s, hgate_s, state_hgrn[0],
                                   wa, wb, wo, ln_ffn2, wup, wdn, ln_f2)

    kv5 = lastkv_p.reshape(bsz, WINDOW, 2, KV_HEADS, HEAD_DIM)
    meta5 = jnp.broadcast_to(kv_s[nb:].reshape(1, N_META, 2, KV_HEADS, HEAD_DIM),
                             (bsz, N_META, 2, KV_HEADS, HEAD_DIM))
    return (y_p.reshape(bsz, seq, D_MODEL),
            y_s.reshape(nb, 1, D_MODEL),
            kv5[None, :, :, 0],
            kv5[None, :, :, 1],
            meta5[None, :, :, 0],
            meta5[None, :, :, 1],
            state_p[None],
            jnp.swapaxes(nkt_s, 1, 2).reshape(1, nb, WINDOW, KV_HEADS, HEAD_DIM),
            jnp.swapaxes(nvt_s, 1, 2).reshape(1, nb, WINDOW, KV_HEADS, HEAD_DIM),
            state_s[None])
```

```python
import functools

import jax
import jax.numpy as jnp
from jax import lax
from jax.experimental import pallas as pl
from jax.experimental.pallas import tpu as pltpu

F32 = jnp.float32
BF16 = jnp.bfloat16

D_MODEL = 1024
N_META = 16
WINDOW = 128
HEAD_DIM = 64
Q_HEADS = 8
KV_HEADS = 2
D_ATT = Q_HEADS * HEAD_DIM
D_KV = KV_HEADS * HEAD_DIM
HG_HEADS = 4
HG_DK = 128
HG_DV = 128
D_HG = HG_HEADS * HG_DK
HG_CHUNK = 64
D_FF = 4 * D_MODEL
EPS = 1e-6
C_Q = 0
C_KV = C_Q + D_ATT
C_HQ = C_KV + 2 * D_KV
C_HF = C_HQ + D_HG
C_HI = C_HF + D_HG
C_HGATE = C_HI + D_HG
C_GA = C_HGATE + D_HG
C_GB = C_GA + D_MODEL
D_IN = C_GB + D_MODEL

VMEM_LIMIT_BYTES = 56 * 1024 * 1024
MIX_ROWS = 512
PROJ_GROUP = 256
WEIGHT_SLAB_ROWS = 64
MERGE_ROWS = 512
SAMPLE_ATT_GROUP = 16
SAMPLE_HG_GROUP = 8

_NT = (((1,), (1,)), ((), ()))
_TN = (((0,), (0,)), ((), ()))


def _dot(a, b):
    return jnp.dot(a, b, preferred_element_type=F32)


def _dot_nt(a, b):
    return lax.dot_general(a, b, _NT, preferred_element_type=F32)


def _dot_tn(a, b):
    return lax.dot_general(a, b, _TN, preferred_element_type=F32)


def _rmsnorm(x, g):
    return x * lax.rsqrt(jnp.mean(x * x, axis=-1, keepdims=True) + EPS) * g


def _resident(shape):
    return pl.BlockSpec(shape, lambda *_: (0,) * len(shape), pipeline_mode=pl.Buffered(1))


def _params(*sem):
    return pltpu.CompilerParams(dimension_semantics=sem, vmem_limit_bytes=VMEM_LIMIT_BYTES)


def _lower_bound(lbp):
    e = jnp.exp(lbp - jnp.max(lbp, axis=0, keepdims=True))
    return e[0:1] / jnp.sum(e, axis=0, keepdims=True)


class _Projection:
    def __init__(self, xn, lbp, w_ref):
        self.xn = xn
        self.w_ref = w_ref
        self.lb = _lower_bound(lbp)

    def cols(self, base, part):
        return _dot(self.xn[...], self.w_ref[:, base + part.start:base + part.stop])

    def q_att(self, part=slice(0, D_ATT)):
        return self.cols(C_Q, part) * (HEAD_DIM ** -0.5)

    def kv(self):
        return self.cols(C_KV, slice(0, 2 * D_KV))

    def q_hg(self, part=slice(0, D_HG)):
        return self.cols(C_HQ, part) * (HG_DK ** -0.5)

    def forget(self, part=slice(0, D_HG)):
        lb = self.lb[:, part]
        f = lb + (1.0 - lb) * jax.nn.sigmoid(self.cols(C_HF, part))
        return 1.0 - f, jnp.log(f)

    def i_hg(self, part=slice(0, D_HG)):
        return self.cols(C_HI, part)

    def swish_gate(self, part=slice(0, D_HG)):
        g = self.cols(C_HGATE, part)
        return g * jax.nn.sigmoid(g)

    def branch_gate(self, base, part=slice(0, D_MODEL)):
        return jax.nn.sigmoid(self.cols(base, part))


class _Attention:
    def __init__(self, sink_ref, q_ref, kv, first, o_ref):
        self.sink_ref, self.q_ref, self.first, self.o_ref = sink_ref, q_ref, first, o_ref
        self.nsub = q_ref.shape[0] // WINDOW
        self.nk = 2 * WINDOW + N_META
        self.meta0 = WINDOW + q_ref.shape[0]
        lane = lax.broadcasted_iota(jnp.int32, (kv.shape[0], D_KV), 1)
        low = lane < HEAD_DIM
        k = kv[:, :D_KV]
        ksw = pltpu.roll(k, HEAD_DIM, axis=1)
        self.kboth = (jnp.where(low, k, ksw).astype(BF16), jnp.where(low, ksw, k).astype(BF16))
        qlane = lax.broadcasted_iota(jnp.int32, (1, 128), 1)
        self.keep = (jnp.where(qlane < HEAD_DIM, 1.0, 0.0).astype(BF16),
                     jnp.where(qlane < HEAD_DIM, 0.0, 1.0).astype(BF16))
        r = lax.broadcasted_iota(jnp.int32, (D_KV, D_KV), 0)
        c = lax.broadcasted_iota(jnp.int32, (D_KV, D_KV), 1)
        eye = jnp.where(r == c, 1.0, 0.0).astype(BF16)
        self.vt = _dot_nt(eye, kv[:, D_KV:].astype(BF16)).astype(BF16)
        self.ones = jnp.ones((16, self.nk), BF16)
        self.key = lax.broadcasted_iota(jnp.int32, (self.nk, 2 * WINDOW), 0)
        col = lax.broadcasted_iota(jnp.int32, (self.nk, 2 * WINDOW), 1)
        self.qry = jnp.bitwise_and(col, WINDOW - 1)
        self.first_head = lax.broadcasted_iota(jnp.int32, (1, 2 * WINDOW), 1) < WINDOW
        self.scores = {}

    def issue_scores(self, i):
        r0 = i * WINDOW
        out = []
        for p in range(Q_HEADS // 2):
            kb = self.kboth[(2 * p) // (Q_HEADS // KV_HEADS)]
            kmat = jnp.concatenate([kb[r0:r0 + 2 * WINDOW], kb[self.meta0:]], axis=0)
            qp = self.q_ref[r0:r0 + WINDOW, p * 128:(p + 1) * 128]
            q2 = jnp.concatenate([qp * self.keep[0], qp * self.keep[1]], axis=0)
            out.append(_dot_nt(kmat, q2))
        self.scores[i] = out

    def finish(self, i):
        r0 = i * WINDOW
        lo = jnp.where(self.first, WINDOW - 1, self.qry) if i == 0 else self.qry
        mask = jnp.logical_and(self.key > lo, self.key <= self.qry + WINDOW)
        mask = jnp.logical_or(mask, self.key >= 2 * WINDOW)
        vaug = []
        for h in range(KV_HEADS):
            vth = self.vt[h * HEAD_DIM:(h + 1) * HEAD_DIM]
            vaug.append(jnp.concatenate(
                [jnp.concatenate([vth[:, r0:r0 + 2 * WINDOW], vth[:, self.meta0:]], axis=1),
                 self.ones], axis=0))
        outs = []
        for p, raw in enumerate(self.scores.pop(i)):
            h = (2 * p) // (Q_HEADS // KV_HEADS)
            s = jnp.where(mask, raw, -jnp.inf)
            sk = jnp.where(self.first_head, self.sink_ref[2 * p], self.sink_ref[2 * p + 1])
            m = jnp.maximum(jnp.max(s, axis=0, keepdims=True), sk)
            e = jnp.exp(s - m).astype(BF16)
            oa = _dot(vaug[h], e)
            o = oa[:HEAD_DIM] / (oa[HEAD_DIM:HEAD_DIM + 1] + jnp.exp(sk - m))
            outs += [o[:, :WINDOW], o[:, WINDOW:]]
        self.o_ref[r0:r0 + WINDOW, :] = jnp.concatenate(outs, axis=0).T.astype(self.o_ref.dtype)


def _cumsum_rows(x):
    t = x.shape[0]
    row = lax.broadcasted_iota(jnp.int32, x.shape, 0)
    d = 1
    while d < t:
        x = x + jnp.where(row >= d, pltpu.roll(x, d, axis=0), 0.0)
        d *= 2
    return x


def _hg_out(o, gate, nw):
    parts = []
    for h in range(HG_HEADS):
        oh = o[:, h * HG_DV:(h + 1) * HG_DV]
        parts.append(oh * lax.rsqrt(jnp.mean(oh * oh, axis=-1, keepdims=True) + EPS) * nw)
    return jnp.concatenate(parts, axis=1) * gate


_HG_SLICES = [slice(h * HG_DK, (h + 1) * HG_DK) for h in range(HG_HEADS)]


def _state_update(iv, kd):
    return jnp.concatenate([_dot_tn(iv[:, hs], kd[:, hs]) for hs in _HG_SLICES], axis=1)


def _meta_state(mk, mlogf, mi):
    b = _cumsum_rows(mlogf)
    kd = (mk * jnp.exp(b[-1:] - b)).astype(BF16)
    return _state_update(mi.astype(BF16), kd)


class _Hgrn:
    def __init__(self, q_ref, k_ref, logf_ref, i_ref, g_ref, nw, state, o_ref):
        self.q_ref, self.k_ref, self.logf_ref, self.i_ref, self.g_ref = q_ref, k_ref, logf_ref, i_ref, g_ref
        self.nw, self.state, self.o_ref = nw, state, o_ref
        t = HG_CHUNK
        self.nchunk = q_ref.shape[0] // t
        r = lax.broadcasted_iota(jnp.int32, (t, t), 0)
        c = lax.broadcasted_iota(jnp.int32, (t, t), 1)
        self.causal = r >= c
        self.cums, self.terms, self.entering = {}, {}, {}

    def rows(self, ci):
        return slice(ci * HG_CHUNK, (ci + 1) * HG_CHUNK)

    def issue_cumsum(self, ci):
        self.cums[ci] = _cumsum_rows(self.logf_ref[self.rows(ci), :])

    def issue_local(self, ci):
        rows = self.rows(ci)
        b = self.cums.pop(ci)
        bl = b[-1:]
        k = self.k_ref[rows, :].astype(F32)
        qe = (self.q_ref[rows, :].astype(F32) * jnp.exp(b)).astype(BF16)
        ke = (k * jnp.exp(-b)).astype(BF16)
        kd = (k * jnp.exp(bl - b)).astype(BF16)
        iv = self.i_ref[rows, :].astype(BF16)
        a = [jnp.where(self.causal, _dot_nt(qe[:, hs], ke[:, hs]), 0.0).astype(BF16) for hs in _HG_SLICES]
        self.terms[ci] = (qe, iv, a)
        self.entering[ci] = self.state.astype(BF16)
        self.state = self.state * jnp.exp(bl) + _state_update(iv, kd)

    def finish(self, ci):
        rows = self.rows(ci)
        qe, iv, a = self.terms.pop(ci)
        ent = self.entering.pop(ci)
        outs = [_dot(a[h], iv[:, hs]) + _dot_nt(qe[:, hs], ent[:, hs]) for h, hs in enumerate(_HG_SLICES)]
        o = _hg_out(jnp.concatenate(outs, axis=1), self.g_ref[rows, :].astype(F32), self.nw)
        self.o_ref[rows, :] = o.astype(self.o_ref.dtype)


def _load_as_bf16(w_hbm, w_vmem, stage, sem):
    rows = stage.shape[1]
    nslab = w_hbm.shape[0] // rows

    def slab_copy(c):
        return pltpu.make_async_copy(w_hbm.at[pl.ds(c * rows, rows), :], stage.at[c % 2], sem.at[c % 2])

    slab_copy(0).start()
    for c in range(nslab):
        if c + 1 < nslab:
            slab_copy(c + 1).start()
        slab_copy(c).wait()
        w_vmem[pl.ds(c * rows, rows), :] = stage[c % 2].astype(w_vmem.dtype)


def _mixer_kernel(sink_ref, x0_ref, xnext_ref, xs_ref, xm_ref, g_ref, lbp_ref, w_hbm, nw_ref,
                  wa32, wb32, wo32, wup32, wdn32,
                  att_ref, hg_ref, sga_ref, sgb_ref, sfin_ref, lastkv_ref,
                  qs_ref, kvs_ref, hqs_ref, hks_ref, his_ref, hgs_ref, sgas_ref, sgbs_ref,
                  wa16, wb16, wo16, wup16, wdn16,
                  w_ref, stage, sem, xn_s, zq, zkv, zhq, zhk, zlogf, zhi, zhg, kvm_ref, st_ref, mst_ref,
                  *, tiles_per_seq):
    s = pl.program_id(0)
    t = xnext_ref.shape[0]

    @pl.when(s == 0)
    def _():
        _load_as_bf16(w_hbm, w_ref, stage, sem)
        xn_s[...] = _rmsnorm(x0_ref[...], g_ref[...]).astype(xn_s.dtype)
        zkv[0:WINDOW, :] = jnp.zeros((WINDOW, zkv.shape[1]), zkv.dtype)
        st_ref[...] = jnp.zeros(st_ref.shape, st_ref.dtype)
        nb = xs_ref.shape[0]
        small = jnp.concatenate([xs_ref[...], xm_ref[...]], axis=0)
        p = _Projection(_rmsnorm(small, g_ref[...]).astype(BF16), lbp_ref[...], w_ref)
        qs_ref[...] = p.q_att().astype(qs_ref.dtype)
        kv = p.kv()
        kvs_ref[...] = kv
        kvm_ref[...] = kv[nb:]
        hqs_ref[...] = p.q_hg()
        k, logf = p.forget()
        hks_ref[...] = k
        iv = p.i_hg()
        his_ref[...] = iv
        hgs_ref[...] = p.swish_gate()
        sgas_ref[...] = p.branch_gate(C_GA).astype(sgas_ref.dtype)
        sgbs_ref[...] = p.branch_gate(C_GB).astype(sgbs_ref.dtype)
        mst_ref[...] = _meta_state(k[nb:], logf[nb:], iv[nb:])

    proj = _Projection(xn_s, lbp_ref[...], w_ref)

    def gate_a(p):
        sga_ref[:, p] = proj.branch_gate(C_GA, p).astype(sga_ref.dtype)

    def gate_b(p):
        sgb_ref[:, p] = proj.branch_gate(C_GB, p).astype(sgb_ref.dtype)

    def new_kv(_):
        zkv[WINDOW:WINDOW + t, :] = proj.kv()

    def new_forget(p):
        k_new, logf_new = proj.forget(p)
        zhk[:, p] = k_new.astype(zhk.dtype)
        zlogf[:, p] = logf_new

    def new_q(p):
        zq[:, p] = proj.q_att(p).astype(zq.dtype)

    def new_hq(p):
        zhq[:, p] = proj.q_hg(p).astype(zhq.dtype)

    def new_hi(p):
        zhi[:, p] = proj.i_hg(p).astype(zhi.dtype)

    def new_gate(p):
        zhg[:, p] = proj.swish_gate(p).astype(zhg.dtype)

    def parts(n):
        return [slice(c, c + PROJ_GROUP) for c in range(0, n, PROJ_GROUP)]

    groups = ([(gate_a, p) for p in parts(D_MODEL)] + [(gate_b, p) for p in parts(D_MODEL)]
              + [(new_kv, None)] + [(new_q, p) for p in parts(D_ATT)]
              + [(new_forget, p) for p in parts(D_HG)] + [(new_hq, p) for p in parts(D_HG)]
              + [(new_hi, p) for p in parts(D_HG)] + [(new_gate, p) for p in parts(D_HG)])

    mixer_slots = {
        0: [("qk", 0), ("cs", 0), ("cs", 1), ("cs", 2), ("cs", 3)],
        1: [("cs", 4), ("cs", 5), ("loc", 0)],
        2: [("cs", 6), ("cs", 7), ("loc", 1)],
        3: [("qk", 1), ("loc", 2)],
        4: [("fin", 0), ("loc", 3)],
        5: [("out", 0), ("loc", 4)],
        6: [("qk", 2), ("out", 1), ("loc", 5)],
        7: [("fin", 1), ("out", 2), ("loc", 6)],
        8: [("qk", 3), ("out", 3), ("loc", 7), ("save", None)],
        9: [("out", 4)],
        10: [("fin", 2), ("out", 5)],
        11: [("out", 6)],
        12: [("out", 7)],
        13: [("fin", 3)],
    }

    def body(do_proj, do_mix):
        if do_mix:
            first = lax.rem(s - 1, tiles_per_seq) == 0
            att = _Attention(sink_ref, zq, jnp.concatenate([zkv[...], kvm_ref[...]], axis=0), first, att_ref)
            last = zkv[t:t + WINDOW, :]
            lastkv_ref[...] = last
            zkv[0:WINDOW, :] = last
            hg = _Hgrn(zhq, zhk, zlogf, zhi, zhg, nw_ref[...],
                       jnp.where(first, mst_ref[...], st_ref[...]), hg_ref)

            def save_state(_):
                st_ref[...] = hg.state
                for h, hs in enumerate(_HG_SLICES):
                    sfin_ref[h] = hg.state[:, hs].T

            pieces = {"qk": att.issue_scores, "cs": hg.issue_cumsum, "loc": hg.issue_local,
                      "fin": att.finish, "out": hg.finish, "save": save_state}
        for slot, (fn, p) in enumerate(groups):
            if do_proj:
                fn(p)
            if do_mix:
                for name, idx in mixer_slots.get(slot, []):
                    pieces[name](idx)
        if do_proj:
            xn_s[...] = _rmsnorm(xnext_ref[...], g_ref[...]).astype(xn_s.dtype)
            for src, dst in ((wa32, wa16), (wb32, wb16), (wo32, wo16), (wup32, wup16), (wdn32, wdn16)):
                dst[...] = src[...].astype(dst.dtype)

    last_step = pl.num_programs(0) - 1
    pl.when(s == 0)(lambda: body(True, False))
    pl.when(jnp.logical_and(s > 0, s < last_step))(lambda: body(True, True))
    pl.when(s == last_step)(lambda: body(False, True))


def _mixers(sinks, x, x_sample, x_meta, g, lb_param, w_f32, hg_norm, later_weights, bsz):
    n = x.shape[0]
    t = MIX_ROWS
    nt = n // t
    per_seq = nt // bsz
    rows = x_sample.shape[0] + x_meta.shape[0]

    def this_tile(c):
        return pl.BlockSpec((t, c), lambda s: (jnp.minimum(s, nt - 1), 0))

    def prev_tile(c):
        return pl.BlockSpec((t, c), lambda s: (jnp.maximum(s - 1, 0), 0))

    def prev_seq(shape):
        return pl.BlockSpec((None,) + shape,
                            lambda s: (jnp.maximum(s - 1, 0) // per_seq,) + (0,) * len(shape))

    def small(c):
        return pl.BlockSpec((rows, c), lambda s: (0, 0))

    first_tile = pl.BlockSpec((t, D_MODEL), lambda s: (0, 0), pipeline_mode=pl.Buffered(1))
    next_tile = pl.BlockSpec((t, D_MODEL), lambda s: (jnp.minimum(s + 1, nt - 1), 0))
    small_widths = (D_ATT, 2 * D_KV, D_HG, D_HG, D_HG, D_HG, D_MODEL, D_MODEL)
    small_dtypes = (BF16, F32, F32, F32, F32, F32, BF16, BF16)

    def row_block(wt):
        return pl.BlockSpec((wt.shape[0] // nt, wt.shape[1]), lambda s: (jnp.minimum(s, nt - 1), 0))

    return pl.pallas_call(
        functools.partial(_mixer_kernel, tiles_per_seq=per_seq),
        grid=(nt + 1,),
        in_specs=[pl.BlockSpec(memory_space=pltpu.SMEM), first_tile, next_tile,
                  _resident(x_sample.shape), _resident(x_meta.shape), _resident((1, D_MODEL)),
                  _resident(lb_param.shape), pl.BlockSpec(memory_space=pl.ANY), _resident((1, HG_DV))]
                 + [row_block(wt) for wt in later_weights],
        out_specs=[prev_tile(D_ATT), prev_tile(D_HG), this_tile(D_MODEL), this_tile(D_MODEL),
                   prev_seq((HG_HEADS, HG_DK, HG_DV)), prev_seq((WINDOW, 2 * D_KV))]
                  + [small(c) for c in small_widths] + [row_block(wt) for wt in later_weights],
        out_shape=[jax.ShapeDtypeStruct((n, D_ATT), BF16), jax.ShapeDtypeStruct((n, D_HG), BF16),
                   jax.ShapeDtypeStruct((n, D_MODEL), BF16), jax.ShapeDtypeStruct((n, D_MODEL), BF16),
                   jax.ShapeDtypeStruct((bsz, HG_HEADS, HG_DK, HG_DV), F32),
                   jax.ShapeDtypeStruct((bsz, WINDOW, 2 * D_KV), F32)]
                  + [jax.ShapeDtypeStruct((rows, c), d) for c, d in zip(small_widths, small_dtypes)]
                  + [jax.ShapeDtypeStruct(wt.shape, BF16) for wt in later_weights],
        scratch_shapes=[pltpu.VMEM((D_MODEL, D_IN), BF16),
                        pltpu.VMEM((2, WEIGHT_SLAB_ROWS, D_IN), F32),
                        pltpu.SemaphoreType.DMA((2,)),
                        pltpu.VMEM((t, D_MODEL), BF16),
                        pltpu.VMEM((t, D_ATT), BF16), pltpu.VMEM((WINDOW + t, 2 * D_KV), F32),
                        pltpu.VMEM((t, D_HG), BF16), pltpu.VMEM((t, D_HG), BF16),
                        pltpu.VMEM((t, D_HG), F32), pltpu.VMEM((t, D_HG), BF16),
                        pltpu.VMEM((t, D_HG), BF16),
                        pltpu.VMEM((N_META, 2 * D_KV), F32),
                        pltpu.VMEM((HG_DV, D_HG), F32),
                        pltpu.VMEM((HG_DV, D_HG), F32)],
        compiler_params=_params("arbitrary"),
        name="mixers",
    )(sinks, x, x, x_sample, x_meta, g, lb_param, w_f32, hg_norm, *later_weights)


def _merge_ffn_rows(x, att, hg, sga, sgb, wa_ref, wb_ref, wo_ref, ln_ffn, wup_ref, wdn_ref):
    ya = _dot(att.astype(BF16), wa_ref[...])
    yb = _dot(hg.astype(BF16), wb_ref[...])
    mix = sga.astype(F32) * ya + sgb.astype(F32) * yb
    h1 = x + _dot(mix.astype(BF16), wo_ref[...])
    xn = _rmsnorm(h1, ln_ffn).astype(BF16)
    acc = jnp.zeros_like(h1)
    step = 1024
    for c in range(D_FF // step):
        u = jnp.maximum(_dot(xn, wup_ref[:, c * step:(c + 1) * step]), 0.0)
        acc = acc + _dot((u * u).astype(BF16), wdn_ref[c * step:(c + 1) * step, :])
    return h1 + acc


def _sample_hgrn_group(rows, nw, q_ref, k_ref, i_ref, g_ref, s_ref, snew_ref, hg_ref):
    k = k_ref[rows, :]
    q = q_ref[rows, :]
    iv = i_ref[rows, :]
    r = lax.broadcasted_iota(jnp.int32, (HG_DK, HG_DK), 0)
    c = lax.broadcasted_iota(jnp.int32, (HG_DK, HG_DK), 1)
    eye = r == c

    def column(row):
        return jnp.sum(jnp.where(eye, row, 0.0), axis=1, keepdims=True)

    outs = []
    for b in range(s_ref.shape[0]):
        heads = []
        for h, hs in enumerate(_HG_SLICES):
            kc = column(k[b:b + 1, hs])
            qc = column(q[b:b + 1, hs])
            s_old = s_ref[b, h]
            s = s_old + kc * (iv[b:b + 1, hs] - s_old)
            snew_ref[b, h] = s
            heads.append(jnp.sum(qc * s, axis=0, keepdims=True))
        outs.append(jnp.concatenate(heads, axis=1))
    hg_ref[rows, :] = _hg_out(jnp.concatenate(outs, axis=0), g_ref[rows, :], nw)


def _merge_ffn_kernel(x_ref, att_ref, hg_ref, sga_ref, sgb_ref,
                      xs_ref, atts_ref, sgas_ref, sgbs_ref,
                      nw_ref, hqs_ref, hks_ref, his_ref, hgates_ref, state_ref,
                      wa_ref, wb_ref, wo_ref, wup_ref, wdn_ref, lnffn_ref, lnf_ref,
                      y_ref, ys_ref, snew_ref, hgs, *, prompt_steps):
    i = pl.program_id(0)
    weights = (wa_ref, wb_ref, wo_ref, lnffn_ref[...], wup_ref, wdn_ref)
    group = state_ref.shape[0]
    ngroups = xs_ref.shape[0] // group

    @pl.when(i < prompt_steps)
    def _():
        y_ref[...] = _rmsnorm(_merge_ffn_rows(x_ref[...], att_ref[...], hg_ref[...], sga_ref[...],
                                              sgb_ref[...], *weights), lnf_ref[...])
        rows = pl.ds(pl.multiple_of(jnp.minimum(i, ngroups - 1) * group, group), group)
        _sample_hgrn_group(rows, nw_ref[...], hqs_ref, hks_ref, his_ref, hgates_ref,
                           state_ref, snew_ref, hgs)

    @pl.when(i == prompt_steps)
    def _():
        ys_ref[...] = _rmsnorm(_merge_ffn_rows(xs_ref[...], atts_ref[...], hgs[...], sgas_ref[...],
                                               sgbs_ref[...], *weights), lnf_ref[...])


def _merge_ffn(x, att, hg, sga, sgb, xs, att_s, sga_s, sgb_s, hg_norm, hq_s, hk_s, hi_s, hgate_s, state,
               wa, wb, wo, ln_ffn, w_up, w_down, ln_f):
    n = x.shape[0]
    nb = xs.shape[0]
    rows = MERGE_ROWS
    nt = n // rows
    g = SAMPLE_HG_GROUP
    assert nb // g <= nt

    def blk(c):
        return pl.BlockSpec((rows, c), lambda i: (jnp.minimum(i, nt - 1), 0))

    def sample(c):
        return pl.BlockSpec((nb, c), lambda i: (0, 0))

    sblk = pl.BlockSpec((g, HG_HEADS, HG_DK, HG_DV), lambda i: (jnp.minimum(i, nb // g - 1), 0, 0, 0))
    return pl.pallas_call(
        functools.partial(_merge_ffn_kernel, prompt_steps=nt),
        grid=(nt + 1,),
        in_specs=[blk(D_MODEL), blk(D_ATT), blk(D_HG), blk(D_MODEL), blk(D_MODEL),
                  sample(D_MODEL), sample(D_ATT), sample(D_MODEL), sample(D_MODEL),
                  _resident((1, HG_DV)), _resident(hq_s.shape), _resident(hk_s.shape),
                  _resident(hi_s.shape), _resident(hgate_s.shape), sblk,
                  _resident(wa.shape), _resident(wb.shape), _resident(wo.shape),
                  _resident(w_up.shape), _resident(w_down.shape),
                  _resident((1, D_MODEL)), _resident((1, D_MODEL))],
        out_specs=[blk(D_MODEL), sample(D_MODEL), sblk],
        out_shape=[jax.ShapeDtypeStruct((n, D_MODEL), F32), jax.ShapeDtypeStruct((nb, D_MODEL), F32),
                   jax.ShapeDtypeStruct(state.shape, F32)],
        scratch_shapes=[pltpu.VMEM((nb, D_HG), F32)],
        compiler_params=_params("arbitrary"),
        name="merge_ffn",
    )(x, att, hg, sga, sgb, xs, att_s, sga_s, sgb_s, hg_norm, hq_s, hk_s, hi_s, hgate_s, state,
      wa, wb, wo, w_up, w_down, ln_ffn, ln_f)


def _sample_attn_kernel(sink_ref, qm_ref, kvn_ref, ckt_ref, cvt_ref, mk_ref, mv_ref,
                        o_ref, nkt_ref, nvt_ref):
    nb = qm_ref.shape[0]
    head = lax.broadcasted_iota(jnp.int32, (Q_HEADS, 1), 0)
    sk = jnp.zeros((Q_HEADS, 1), F32)
    for j in range(Q_HEADS):
        sk = jnp.where(head == j, sink_ref[j], sk)
    newest = lax.broadcasted_iota(jnp.int32, (D_KV, WINDOW), 1) == WINDOW - 1
    kvn = kvn_ref[...]
    hi = kvn.astype(BF16)
    rest = kvn - hi.astype(F32)
    mid = rest.astype(BF16)
    lo = (rest - mid.astype(F32)).astype(BF16)
    r = lax.broadcasted_iota(jnp.int32, (2 * D_KV, 2 * D_KV), 0)
    c = lax.broadcasted_iota(jnp.int32, (2 * D_KV, 2 * D_KV), 1)
    eye = jnp.where(r == c, 1.0, 0.0).astype(BF16)
    pieces_t = _dot_nt(eye, jnp.concatenate([hi, mid, lo], axis=0)).astype(BF16)
    pr = lax.broadcasted_iota(jnp.int32, (3 * nb, WINDOW), 0)
    pc = lax.broadcasted_iota(jnp.int32, (3 * nb, WINDOW), 1)
    for b in range(nb):
        pick = jnp.logical_and(lax.rem(pr, nb) == b, pc == WINDOW - 1)
        new_cols = _dot(pieces_t, jnp.where(pick, 1.0, 0.0).astype(BF16))
        nkt_ref[b] = jnp.where(newest, new_cols[:D_KV], pltpu.roll(ckt_ref[b], WINDOW - 1, axis=1))
        nvt_ref[b] = jnp.where(newest, new_cols[D_KV:], pltpu.roll(cvt_ref[b], WINDOW - 1, axis=1))
    scores = [(_dot(qm_ref[b], nkt_ref[b].astype(BF16)),
               _dot_nt(qm_ref[b], mk_ref[b].astype(BF16)))
              for b in range(nb)]
    for b, (s_w, s_m) in enumerate(scores):
        m = jnp.maximum(jnp.maximum(jnp.max(s_w, axis=1, keepdims=True),
                                    jnp.max(s_m, axis=1, keepdims=True)), sk)
        e_w = jnp.exp(s_w - m)
        e_m = jnp.exp(s_m - m)
        l = (jnp.sum(e_w, axis=1, keepdims=True) + jnp.sum(e_m, axis=1, keepdims=True)
             + jnp.exp(sk - m))
        o = (_dot_nt(e_w.astype(BF16), nvt_ref[b].astype(BF16))
             + _dot(e_m.astype(BF16), mv_ref[b].astype(BF16)))
        o_ref[b] = o / l


def _sample_attention(sinks, qm, kv_new, ck, cv, mk, mv):
    nb = ck.shape[0]
    g = SAMPLE_ATT_GROUP

    def blk3(a, c):
        return pl.BlockSpec((g, a, c), lambda i: (i, 0, 0))

    return pl.pallas_call(
        _sample_attn_kernel,
        grid=(nb // g,),
        in_specs=[pl.BlockSpec(memory_space=pltpu.SMEM), blk3(Q_HEADS, D_KV),
                  pl.BlockSpec((g, 2 * D_KV), lambda i: (i, 0)),
                  blk3(D_KV, WINDOW), blk3(D_KV, WINDOW), blk3(N_META, D_KV), blk3(N_META, D_KV)],
        out_specs=[blk3(Q_HEADS, D_KV), blk3(D_KV, WINDOW), blk3(D_KV, WINDOW)],
        out_shape=[jax.ShapeDtypeStruct((nb, Q_HEADS, D_KV), F32),
                   jax.ShapeDtypeStruct((nb, D_KV, WINDOW), F32),
                   jax.ShapeDtypeStruct((nb, D_KV, WINDOW), F32)],
        compiler_params=_params("parallel"),
        name="sample_attn",
    )(sinks, qm, kv_new, ck, cv, mk, mv)


def kernel(x_prompt, x_sample, cache_k, cache_v, cache_meta_k, cache_meta_v, state_hgrn, meta,
           w_in, sinks, lb_param, hg_norm, w_att_out, w_hg_out, w_o, ln_mix, ln_ffn, w_up,
           w_down, ln_f):
    bsz, seq, _ = x_prompt.shape
    nb = x_sample.shape[0]
    ln_mix2 = ln_mix.reshape(1, D_MODEL)
    ln_ffn2 = ln_ffn.reshape(1, D_MODEL)
    ln_f2 = ln_f.reshape(1, D_MODEL)
    nw = hg_norm.reshape(1, HG_DV)

    xs = x_sample.reshape(nb, D_MODEL)

    xp = x_prompt.reshape(bsz * seq, D_MODEL)
    (att_p, hg_p, sga_p, sgb_p, state_p, lastkv_p,
     q_s, kv_s, hq_s, hk_s, hi_s, hgate_s, sga_s, sgb_s,
     wa, wb, wo, wup, wdn) = _mixers(
        sinks.reshape(Q_HEADS), xp, xs, meta, ln_mix2, lb_param, w_in[0], nw,
        (w_att_out[0], w_hg_out[0], w_o[0], w_up[0], w_down[0]), bsz)

    grp = Q_HEADS // KV_HEADS
    rows = nb + N_META
    qs4 = q_s.reshape(rows, KV_HEADS, grp, 1, HEAD_DIM)
    sel = jnp.eye(KV_HEADS, dtype=BF16).reshape(1, KV_HEADS, 1, KV_HEADS, 1)
    qm = (qs4 * sel).reshape(rows, Q_HEADS, D_KV)

    def window_t(c):
        return jnp.swapaxes(c[0].reshape(nb, WINDOW, D_KV), 1, 2)

    o_s, nkt_s, nvt_s = _sample_attention(
        sinks.reshape(Q_HEADS), qm, kv_s, window_t(cache_k), window_t(cache_v),
        cache_meta_k[0].reshape(nb, N_META, D_KV), cache_meta_v[0].reshape(nb, N_META, D_KV))
    o5 = o_s.reshape(nb, KV_HEADS, grp, KV_HEADS, HEAD_DIM)
    att_s = jnp.stack([o5[:, h, :, h, :] for h in range(KV_HEADS)], axis=1).reshape(nb, D_ATT)

    y_p, y_s, state_s = _merge_ffn(xp, att_p, hg_p, sga_p, sgb_p, xs, att_s, sga_s, sgb_s,
                                   nw, hq_s, hk_s, hi_s, hgate_s, state_hgrn[0],
                                   wa, wb, wo, ln_ffn2, wup, wdn, ln_f2)

    kv5 = lastkv_p.reshape(bsz, WINDOW, 2, KV_HEADS, HEAD_DIM)
    meta5 = jnp.broadcast_to(kv_s[nb:].reshape(1, N_META, 2, KV_HEADS, HEAD_DIM),
                             (bsz, N_META, 2, KV_HEADS, HEAD_DIM))
    return (y_p.reshape(bsz, seq, D_MODEL),
            y_s.reshape(nb, 1, D_MODEL),
            kv5[None, :, :, 0],
            kv5[None, :, :, 1],
            meta5[None, :, :, 0],
            meta5[None, :, :, 1],
            state_p[None],
            jnp.swapaxes(nkt_s, 1, 2).reshape(1, nb, WINDOW, KV_HEADS, HEAD_DIM),
            jnp.swapaxes(nvt_s, 1, 2).reshape(1, nb, WINDOW, KV_HEADS, HEAD_DIM),
            state_s[None])
```

```python
import functools

import jax
import jax.numpy as jnp
from jax import lax
from jax.experimental import pallas as pl
from jax.experimental.pallas import tpu as pltpu

F32 = jnp.float32
BF16 = jnp.bfloat16

D_MODEL = 1024
N_META = 16
WINDOW = 128
HEAD_DIM = 64
Q_HEADS = 8
KV_HEADS = 2
D_ATT = Q_HEADS * HEAD_DIM
D_KV = KV_HEADS * HEAD_DIM
HG_HEADS = 4
HG_DK = 128
HG_DV = 128
D_HG = HG_HEADS * HG_DK
HG_CHUNK = 64
D_FF = 4 * D_MODEL
EPS = 1e-6
C_Q = 0
C_KV = C_Q + D_ATT
C_HQ = C_KV + 2 * D_KV
C_HF = C_HQ + D_HG
C_HI = C_HF + D_HG
C_HGATE = C_HI + D_HG
C_GA = C_HGATE + D_HG
C_GB = C_GA + D_MODEL
D_IN = C_GB + D_MODEL

VMEM_LIMIT_BYTES = 56 * 1024 * 1024
MIX_ROWS = 512
PROJ_GROUP = 256
WEIGHT_SLAB_ROWS = 64
MERGE_ROWS = 512
FFN_CHUNK = 1024
SAMPLE_ATT_GROUP = 16
SAMPLE_CACHE_GROUP = 8
SAMPLE_HG_GROUP = 8

_NT = (((1,), (1,)), ((), ()))
_TN = (((0,), (0,)), ((), ()))


def _dot(a, b):
    return jnp.dot(a, b, preferred_element_type=F32)


def _dot_nt(a, b):
    return lax.dot_general(a, b, _NT, preferred_element_type=F32)


def _dot_tn(a, b):
    return lax.dot_general(a, b, _TN, preferred_element_type=F32)


def _rmsnorm(x, g):
    return x * lax.rsqrt(jnp.mean(x * x, axis=-1, keepdims=True) + EPS) * g


def _resident(shape):
    return pl.BlockSpec(shape, lambda *_: (0,) * len(shape), pipeline_mode=pl.Buffered(1))


def _params(*sem):
    return pltpu.CompilerParams(dimension_semantics=sem, vmem_limit_bytes=VMEM_LIMIT_BYTES)


def _lower_bound(lbp):
    e = jnp.exp(lbp - jnp.max(lbp, axis=0, keepdims=True))
    return e[0:1] / jnp.sum(e, axis=0, keepdims=True)


class _Projection:
    def __init__(self, xn, lbp, w_ref):
        self.xn = xn
        self.w_ref = w_ref
        self.lb = _lower_bound(lbp)

    def cols(self, base, part):
        return _dot(self.xn[...], self.w_ref[:, base + part.start:base + part.stop])

    def q_att(self, part=slice(0, D_ATT)):
        return self.cols(C_Q, part) * (HEAD_DIM ** -0.5)

    def kv(self):
        return self.cols(C_KV, slice(0, 2 * D_KV))

    def q_hg(self, part=slice(0, D_HG)):
        return self.cols(C_HQ, part) * (HG_DK ** -0.5)

    def forget(self, part=slice(0, D_HG)):
        lb = self.lb[:, part]
        f = lb + (1.0 - lb) * jax.nn.sigmoid(self.cols(C_HF, part))
        return 1.0 - f, jnp.log(f)

    def i_hg(self, part=slice(0, D_HG)):
        return self.cols(C_HI, part)

    def swish_gate(self, part=slice(0, D_HG)):
        g = self.cols(C_HGATE, part)
        return g * jax.nn.sigmoid(g)

    def branch_gate(self, base, part=slice(0, D_MODEL)):
        return jax.nn.sigmoid(self.cols(base, part))


class _Attention:
    def __init__(self, sink_ref, q_ref, kv, first, o_ref):
        self.sink_ref, self.q_ref, self.first, self.o_ref = sink_ref, q_ref, first, o_ref
        self.nsub = q_ref.shape[0] // WINDOW
        self.nk = 2 * WINDOW + N_META
        self.meta0 = WINDOW + q_ref.shape[0]
        lane = lax.broadcasted_iota(jnp.int32, (kv.shape[0], D_KV), 1)
        low = lane < HEAD_DIM
        k = kv[:, :D_KV]
        ksw = pltpu.roll(k, HEAD_DIM, axis=1)
        self.kboth = (jnp.where(low, k, ksw).astype(BF16), jnp.where(low, ksw, k).astype(BF16))
        qlane = lax.broadcasted_iota(jnp.int32, (1, 128), 1)
        self.keep = (jnp.where(qlane < HEAD_DIM, 1.0, 0.0).astype(BF16),
                     jnp.where(qlane < HEAD_DIM, 0.0, 1.0).astype(BF16))
        r = lax.broadcasted_iota(jnp.int32, (D_KV, D_KV), 0)
        c = lax.broadcasted_iota(jnp.int32, (D_KV, D_KV), 1)
        eye = jnp.where(r == c, 1.0, 0.0).astype(BF16)
        self.vt = _dot_nt(eye, kv[:, D_KV:].astype(BF16)).astype(BF16)
        self.ones = jnp.ones((16, self.nk), BF16)
        self.key = lax.broadcasted_iota(jnp.int32, (self.nk, 2 * WINDOW), 0)
        col = lax.broadcasted_iota(jnp.int32, (self.nk, 2 * WINDOW), 1)
        self.qry = jnp.bitwise_and(col, WINDOW - 1)
        self.first_head = lax.broadcasted_iota(jnp.int32, (1, 2 * WINDOW), 1) < WINDOW
        self.scores = {}

    def issue_scores(self, i):
        r0 = i * WINDOW
        out = []
        for p in range(Q_HEADS // 2):
            kb = self.kboth[(2 * p) // (Q_HEADS // KV_HEADS)]
            kmat = jnp.concatenate([kb[r0:r0 + 2 * WINDOW], kb[self.meta0:]], axis=0)
            qp = self.q_ref[r0:r0 + WINDOW, p * 128:(p + 1) * 128]
            q2 = jnp.concatenate([qp * self.keep[0], qp * self.keep[1]], axis=0)
            out.append(_dot_nt(kmat, q2))
        self.scores[i] = out

    def finish(self, i):
        r0 = i * WINDOW
        lo = jnp.where(self.first, WINDOW - 1, self.qry) if i == 0 else self.qry
        mask = jnp.logical_and(self.key > lo, self.key <= self.qry + WINDOW)
        mask = jnp.logical_or(mask, self.key >= 2 * WINDOW)
        vaug = []
        for h in range(KV_HEADS):
            vth = self.vt[h * HEAD_DIM:(h + 1) * HEAD_DIM]
            vaug.append(jnp.concatenate(
                [jnp.concatenate([vth[:, r0:r0 + 2 * WINDOW], vth[:, self.meta0:]], axis=1),
                 self.ones], axis=0))
        outs = []
        for p, raw in enumerate(self.scores.pop(i)):
            h = (2 * p) // (Q_HEADS // KV_HEADS)
            s = jnp.where(mask, raw, -jnp.inf)
            sk = jnp.where(self.first_head, self.sink_ref[2 * p], self.sink_ref[2 * p + 1])
            m = jnp.maximum(jnp.max(s, axis=0, keepdims=True), sk)
            e = jnp.exp(s - m).astype(BF16)
            oa = _dot(vaug[h], e)
            o = oa[:HEAD_DIM] / (oa[HEAD_DIM:HEAD_DIM + 1] + jnp.exp(sk - m))
            outs += [o[:, :WINDOW], o[:, WINDOW:]]
        self.o_ref[r0:r0 + WINDOW, :] = jnp.concatenate(outs, axis=0).T.astype(self.o_ref.dtype)


def _cumsum_rows(x):
    t = x.shape[0]
    row = lax.broadcasted_iota(jnp.int32, x.shape, 0)
    d = 1
    while d < t:
        x = x + jnp.where(row >= d, pltpu.roll(x, d, axis=0), 0.0)
        d *= 2
    return x


def _hg_out(o, gate, nw):
    parts = []
    for h in range(HG_HEADS):
        oh = o[:, h * HG_DV:(h + 1) * HG_DV]
        parts.append(oh * lax.rsqrt(jnp.mean(oh * oh, axis=-1, keepdims=True) + EPS) * nw)
    return jnp.concatenate(parts, axis=1) * gate


_HG_SLICES = [slice(h * HG_DK, (h + 1) * HG_DK) for h in range(HG_HEADS)]


def _state_update(iv, kd):
    return jnp.concatenate([_dot_tn(iv[:, hs], kd[:, hs]) for hs in _HG_SLICES], axis=1)


def _meta_state(mk, mlogf, mi):
    b = _cumsum_rows(mlogf)
    kd = (mk * jnp.exp(b[-1:] - b)).astype(BF16)
    return _state_update(mi.astype(BF16), kd)


class _Hgrn:
    def __init__(self, q_ref, k_ref, logf_ref, i_ref, g_ref, nw, state, o_ref):
        self.q_ref, self.k_ref, self.logf_ref, self.i_ref, self.g_ref = q_ref, k_ref, logf_ref, i_ref, g_ref
        self.nw, self.state, self.o_ref = nw, state, o_ref
        t = HG_CHUNK
        self.nchunk = q_ref.shape[0] // t
        r = lax.broadcasted_iota(jnp.int32, (t, t), 0)
        c = lax.broadcasted_iota(jnp.int32, (t, t), 1)
        self.causal = r >= c
        self.cums, self.terms, self.entering = {}, {}, {}

    def rows(self, ci):
        return slice(ci * HG_CHUNK, (ci + 1) * HG_CHUNK)

    def issue_cumsum(self, ci):
        self.cums[ci] = _cumsum_rows(self.logf_ref[self.rows(ci), :])

    def issue_local(self, ci):
        rows = self.rows(ci)
        b = self.cums.pop(ci)
        bl = b[-1:]
        k = self.k_ref[rows, :].astype(F32)
        qe = (self.q_ref[rows, :].astype(F32) * jnp.exp(b)).astype(BF16)
        ke = (k * jnp.exp(-b)).astype(BF16)
        kd = (k * jnp.exp(bl - b)).astype(BF16)
        iv = self.i_ref[rows, :].astype(BF16)
        a = [jnp.where(self.causal, _dot_nt(qe[:, hs], ke[:, hs]), 0.0).astype(BF16) for hs in _HG_SLICES]
        self.terms[ci] = (qe, iv, a)
        self.entering[ci] = self.state.astype(BF16)
        self.state = self.state * jnp.exp(bl) + _state_update(iv, kd)

    def finish(self, ci):
        rows = self.rows(ci)
        qe, iv, a = self.terms.pop(ci)
        ent = self.entering.pop(ci)
        outs = [_dot(a[h], iv[:, hs]) + _dot_nt(qe[:, hs], ent[:, hs]) for h, hs in enumerate(_HG_SLICES)]
        o = _hg_out(jnp.concatenate(outs, axis=1), self.g_ref[rows, :].astype(F32), self.nw)
        self.o_ref[rows, :] = o.astype(self.o_ref.dtype)


def _load_as_bf16(w_hbm, w_vmem, stage, sem):
    rows = stage.shape[1]
    nslab = w_hbm.shape[0] // rows

    def slab_copy(c):
        return pltpu.make_async_copy(w_hbm.at[pl.ds(c * rows, rows), :], stage.at[c % 2], sem.at[c % 2])

    slab_copy(0).start()
    for c in range(nslab):
        if c + 1 < nslab:
            slab_copy(c + 1).start()
        slab_copy(c).wait()
        w_vmem[pl.ds(c * rows, rows), :] = stage[c % 2].astype(w_vmem.dtype)


def _mixer_kernel(sink_ref, x0_ref, xnext_ref, xs_ref, xm_ref, g_ref, lbp_ref, w_hbm, nw_ref,
                  wa32, wb32, wo32, wup32, wdn32, ckt_ref, cvt_ref,
                  att_ref, hg_ref, sga_ref, sgb_ref, sfin_ref, lastkv_ref,
                  qs_ref, kvs_ref, hqs_ref, hks_ref, his_ref, hgs_ref, sgas_ref, sgbs_ref,
                  wa16, wb16, wo16, wup16, wdn16, nkt_ref, nvt_ref,
                  w_ref, stage, sem, xn_s, zq, zkv, zhq, zhk, zlogf, zhi, zhg, kvm_ref, st_ref, mst_ref,
                  *, tiles_per_seq):
    s = pl.program_id(0)
    t = xnext_ref.shape[0]

    @pl.when(s == 0)
    def _():
        _load_as_bf16(w_hbm, w_ref, stage, sem)
        xn_s[...] = _rmsnorm(x0_ref[...], g_ref[...]).astype(xn_s.dtype)
        for ref in (zq, zkv, zhq, zhk, zlogf, zhi, zhg, st_ref):
            ref[...] = jnp.zeros(ref.shape, ref.dtype)
        nb = xs_ref.shape[0]
        small = jnp.concatenate([xs_ref[...], xm_ref[...]], axis=0)
        p = _Projection(_rmsnorm(small, g_ref[...]).astype(BF16), lbp_ref[...], w_ref)
        qs_ref[...] = p.q_att().astype(qs_ref.dtype)
        kv = p.kv()
        kvs_ref[...] = kv
        kvm_ref[...] = kv[nb:]
        hqs_ref[...] = p.q_hg()
        k, logf = p.forget()
        hks_ref[...] = k
        iv = p.i_hg()
        his_ref[...] = iv
        hgs_ref[...] = p.swish_gate()
        sgas_ref[...] = p.branch_gate(C_GA).astype(sgas_ref.dtype)
        sgbs_ref[...] = p.branch_gate(C_GB).astype(sgbs_ref.dtype)
        mst_ref[...] = _meta_state(k[nb:], logf[nb:], iv[nb:])

    proj = _Projection(xn_s, lbp_ref[...], w_ref)

    def gate_a(p):
        sga_ref[:, p] = proj.branch_gate(C_GA, p).astype(sga_ref.dtype)

    def gate_b(p):
        sgb_ref[:, p] = proj.branch_gate(C_GB, p).astype(sgb_ref.dtype)

    def new_kv(_):
        zkv[WINDOW:WINDOW + t, :] = proj.kv()

    def new_forget(p):
        k_new, logf_new = proj.forget(p)
        zhk[:, p] = k_new.astype(zhk.dtype)
        zlogf[:, p] = logf_new

    def new_q(p):
        zq[:, p] = proj.q_att(p).astype(zq.dtype)

    def new_hq(p):
        zhq[:, p] = proj.q_hg(p).astype(zhq.dtype)

    def new_hi(p):
        zhi[:, p] = proj.i_hg(p).astype(zhi.dtype)

    def new_gate(p):
        zhg[:, p] = proj.swish_gate(p).astype(zhg.dtype)

    def parts(n):
        return [slice(c, c + PROJ_GROUP) for c in range(0, n, PROJ_GROUP)]

    groups = ([(gate_a, p) for p in parts(D_MODEL)] + [(gate_b, p) for p in parts(D_MODEL)]
              + [(new_kv, None)] + [(new_q, p) for p in parts(D_ATT)]
              + [(new_forget, p) for p in parts(D_HG)] + [(new_hq, p) for p in parts(D_HG)]
              + [(new_hi, p) for p in parts(D_HG)] + [(new_gate, p) for p in parts(D_HG)])

    mixer_slots = {
        0: [("qk", 0), ("cs", 0), ("cs", 1), ("cs", 2), ("cs", 3)],
        1: [("cs", 4), ("cs", 5), ("loc", 0)],
        2: [("cs", 6), ("cs", 7), ("loc", 1)],
        3: [("qk", 1), ("loc", 2)],
        4: [("fin", 0), ("loc", 3)],
        5: [("out", 0), ("loc", 4)],
        6: [("qk", 2), ("out", 1), ("loc", 5)],
        7: [("fin", 1), ("out", 2), ("loc", 6)],
        8: [("qk", 3), ("out", 3), ("loc", 7), ("save", None)],
        9: [("out", 4)],
        10: [("fin", 2), ("out", 5)],
        11: [("out", 6)],
        12: [("out", 7)],
        13: [("fin", 3)],
    }

    def body(do_proj, do_mix):
        if do_mix:
            first = lax.rem(jnp.maximum(s - 1, 0), tiles_per_seq) == 0
            att = _Attention(sink_ref, zq, jnp.concatenate([zkv[...], kvm_ref[...]], axis=0), first, att_ref)
            last = zkv[t:t + WINDOW, :]
            lastkv_ref[...] = last
            zkv[0:WINDOW, :] = last
            hg = _Hgrn(zhq, zhk, zlogf, zhi, zhg, nw_ref[...],
                       jnp.where(first, mst_ref[...], st_ref[...]), hg_ref)

            def save_state(_):
                st_ref[...] = hg.state
                for h, hs in enumerate(_HG_SLICES):
                    sfin_ref[h] = hg.state[:, hs].T

            pieces = {"qk": att.issue_scores, "cs": hg.issue_cumsum, "loc": hg.issue_local,
                      "fin": att.finish, "out": hg.finish, "save": save_state}
        for slot, (fn, p) in enumerate(groups):
            if do_proj:
                fn(p)
            if do_mix:
                for name, idx in mixer_slots.get(slot, []):
                    pieces[name](idx)
        if do_proj:
            xn_s[...] = _rmsnorm(xnext_ref[...], g_ref[...]).astype(xn_s.dtype)
            for src, dst in ((wa32, wa16), (wb32, wb16), (wo32, wo16), (wup32, wup16), (wdn32, wdn16)):
                dst[...] = src[...].astype(dst.dtype)
            group = ckt_ref.shape[0]
            g0 = pl.multiple_of(jnp.minimum(s, xs_ref.shape[0] // group - 1) * group, group)
            kvn = kvs_ref[pl.ds(g0, group), :]
            r = lax.broadcasted_iota(jnp.int32, (D_KV, WINDOW), 0)
            c = lax.broadcasted_iota(jnp.int32, (D_KV, WINDOW), 1)
            for b in range(group):
                for src, dst, lo in ((ckt_ref, nkt_ref, 0), (cvt_ref, nvt_ref, D_KV)):
                    col = jnp.sum(jnp.where(r == c, kvn[b:b + 1, lo:lo + D_KV], 0.0), axis=1, keepdims=True)
                    dst[b] = jnp.where(c == WINDOW - 1, col, pltpu.roll(src[b], WINDOW - 1, axis=1))

    body(True, True)


def _mixers(sinks, x, x_sample, x_meta, g, lb_param, w_f32, hg_norm, later_weights, caches_t, bsz):
    n = x.shape[0]
    t = MIX_ROWS
    nt = n // t
    per_seq = nt // bsz
    rows = x_sample.shape[0] + x_meta.shape[0]

    def this_tile(c):
        return pl.BlockSpec((t, c), lambda s: (jnp.minimum(s, nt - 1), 0))

    def prev_tile(c):
        return pl.BlockSpec((t, c), lambda s: (jnp.maximum(s - 1, 0), 0))

    def prev_seq(shape):
        return pl.BlockSpec((None,) + shape,
                            lambda s: (jnp.maximum(s - 1, 0) // per_seq,) + (0,) * len(shape))

    def small(c):
        return pl.BlockSpec((rows, c), lambda s: (0, 0))

    first_tile = pl.BlockSpec((t, D_MODEL), lambda s: (0, 0), pipeline_mode=pl.Buffered(1))
    next_tile = pl.BlockSpec((t, D_MODEL), lambda s: (jnp.minimum(s + 1, nt - 1), 0))
    small_widths = (D_ATT, 2 * D_KV, D_HG, D_HG, D_HG, D_HG, D_MODEL, D_MODEL)
    small_dtypes = (BF16, F32, F32, F32, F32, F32, BF16, BF16)

    cache_groups = x_sample.shape[0] // SAMPLE_CACHE_GROUP
    assert cache_groups <= nt
    cache_block = pl.BlockSpec((SAMPLE_CACHE_GROUP, D_KV, WINDOW),
                               lambda s: (jnp.minimum(s, cache_groups - 1), 0, 0))

    def row_block(wt):
        return pl.BlockSpec((wt.shape[0] // nt, wt.shape[1]), lambda s: (jnp.minimum(s, nt - 1), 0))

    return pl.pallas_call(
        functools.partial(_mixer_kernel, tiles_per_seq=per_seq),
        grid=(nt + 1,),
        in_specs=[pl.BlockSpec(memory_space=pltpu.SMEM), first_tile, next_tile,
                  _resident(x_sample.shape), _resident(x_meta.shape), _resident((1, D_MODEL)),
                  _resident(lb_param.shape), pl.BlockSpec(memory_space=pl.ANY), _resident((1, HG_DV))]
                 + [row_block(wt) for wt in later_weights] + [cache_block, cache_block],
        out_specs=[prev_tile(D_ATT), prev_tile(D_HG), this_tile(D_MODEL), this_tile(D_MODEL),
                   prev_seq((HG_HEADS, HG_DK, HG_DV)), prev_seq((WINDOW, 2 * D_KV))]
                  + [small(c) for c in small_widths] + [row_block(wt) for wt in later_weights]
                  + [cache_block, cache_block],
        out_shape=[jax.ShapeDtypeStruct((n, D_ATT), BF16), jax.ShapeDtypeStruct((n, D_HG), BF16),
                   jax.ShapeDtypeStruct((n, D_MODEL), BF16), jax.ShapeDtypeStruct((n, D_MODEL), BF16),
                   jax.ShapeDtypeStruct((bsz, HG_HEADS, HG_DK, HG_DV), F32),
                   jax.ShapeDtypeStruct((bsz, WINDOW, 2 * D_KV), F32)]
                  + [jax.ShapeDtypeStruct((rows, c), d) for c, d in zip(small_widths, small_dtypes)]
                  + [jax.ShapeDtypeStruct(wt.shape, BF16) for wt in later_weights]
                  + [jax.ShapeDtypeStruct(ct.shape, F32) for ct in caches_t],
        scratch_shapes=[pltpu.VMEM((D_MODEL, D_IN), BF16),
                        pltpu.VMEM((2, WEIGHT_SLAB_ROWS, D_IN), F32),
                        pltpu.SemaphoreType.DMA((2,)),
                        pltpu.VMEM((t, D_MODEL), BF16),
                        pltpu.VMEM((t, D_ATT), BF16), pltpu.VMEM((WINDOW + t, 2 * D_KV), F32),
                        pltpu.VMEM((t, D_HG), BF16), pltpu.VMEM((t, D_HG), BF16),
                        pltpu.VMEM((t, D_HG), F32), pltpu.VMEM((t, D_HG), BF16),
                        pltpu.VMEM((t, D_HG), BF16),
                        pltpu.VMEM((N_META, 2 * D_KV), F32),
                        pltpu.VMEM((HG_DV, D_HG), F32),
                        pltpu.VMEM((HG_DV, D_HG), F32)],
        compiler_params=_params("arbitrary"),
        name="mixers",
    )(sinks, x, x, x_sample, x_meta, g, lb_param, w_f32, hg_norm, *later_weights, *caches_t)


def _merge_ffn_rows(x, att, hg, sga, sgb, wa_ref, wb_ref, wo_ref, ln_ffn, wup_ref, wdn_ref):
    ya = _dot(att.astype(BF16), wa_ref[...])
    yb = _dot(hg.astype(BF16), wb_ref[...])
    mix = sga.astype(F32) * ya + sgb.astype(F32) * yb
    h1 = x + _dot(mix.astype(BF16), wo_ref[...])
    xn = _rmsnorm(h1, ln_ffn).astype(BF16)
    acc = jnp.zeros_like(h1)
    for c in range(0, D_FF, FFN_CHUNK):
        u = jnp.maximum(_dot(xn, wup_ref[:, c:c + FFN_CHUNK]), 0.0)
        acc = acc + _dot((u * u).astype(BF16), wdn_ref[c:c + FFN_CHUNK, :])
    return h1 + acc


def _sample_hgrn_group(rows, nw, q_ref, k_ref, i_ref, g_ref, s_ref, snew_ref, hg_ref):
    k = k_ref[rows, :]
    q = q_ref[rows, :]
    iv = i_ref[rows, :]
    r = lax.broadcasted_iota(jnp.int32, (HG_DK, HG_DK), 0)
    c = lax.broadcasted_iota(jnp.int32, (HG_DK, HG_DK), 1)
    eye = r == c

    def column(row):
        return jnp.sum(jnp.where(eye, row, 0.0), axis=1, keepdims=True)

    outs = []
    for b in range(s_ref.shape[0]):
        heads = []
        for h, hs in enumerate(_HG_SLICES):
            kc = column(k[b:b + 1, hs])
            qc = column(q[b:b + 1, hs])
            s_old = s_ref[b, h]
            s = s_old + kc * (iv[b:b + 1, hs] - s_old)
            snew_ref[b, h] = s
            heads.append(jnp.sum(qc * s, axis=0, keepdims=True))
        outs.append(jnp.concatenate(heads, axis=1))
    hg_ref[rows, :] = _hg_out(jnp.concatenate(outs, axis=0), g_ref[rows, :], nw)


def _merge_ffn_kernel(x_ref, att_ref, hg_ref, sga_ref, sgb_ref,
                      xs_ref, atts_ref, sgas_ref, sgbs_ref,
                      nw_ref, hqs_ref, hks_ref, his_ref, hgates_ref, state_ref,
                      wa_ref, wb_ref, wo_ref, wup_ref, wdn_ref, lnffn_ref, lnf_ref,
                      y_ref, ys_ref, snew_ref, hgs, *, prompt_steps):
    i = pl.program_id(0)
    weights = (wa_ref, wb_ref, wo_ref, lnffn_ref[...], wup_ref, wdn_ref)
    group = state_ref.shape[0]
    ngroups = xs_ref.shape[0] // group

    @pl.when(i < prompt_steps)
    def _():
        y_ref[...] = _rmsnorm(_merge_ffn_rows(x_ref[...], att_ref[...], hg_ref[...], sga_ref[...],
                                              sgb_ref[...], *weights), lnf_ref[...])
        rows = pl.ds(pl.multiple_of(jnp.minimum(i, ngroups - 1) * group, group), group)
        _sample_hgrn_group(rows, nw_ref[...], hqs_ref, hks_ref, his_ref, hgates_ref,
                           state_ref, snew_ref, hgs)

    @pl.when(i == prompt_steps)
    def _():
        ys_ref[...] = _rmsnorm(_merge_ffn_rows(xs_ref[...], atts_ref[...], hgs[...], sgas_ref[...],
                                               sgbs_ref[...], *weights), lnf_ref[...])


def _merge_ffn(x, att, hg, sga, sgb, xs, att_s, sga_s, sgb_s, hg_norm, hq_s, hk_s, hi_s, hgate_s, state,
               wa, wb, wo, ln_ffn, w_up, w_down, ln_f):
    n = x.shape[0]
    nb = xs.shape[0]
    rows = MERGE_ROWS
    nt = n // rows
    g = SAMPLE_HG_GROUP
    assert nb // g <= nt

    def blk(c):
        return pl.BlockSpec((rows, c), lambda i: (jnp.minimum(i, nt - 1), 0))

    def sample(c):
        return pl.BlockSpec((nb, c), lambda i: (0, 0))

    sblk = pl.BlockSpec((g, HG_HEADS, HG_DK, HG_DV), lambda i: (jnp.minimum(i, nb // g - 1), 0, 0, 0))
    return pl.pallas_call(
        functools.partial(_merge_ffn_kernel, prompt_steps=nt),
        grid=(nt + 1,),
        in_specs=[blk(D_MODEL), blk(D_ATT), blk(D_HG), blk(D_MODEL), blk(D_MODEL),
                  sample(D_MODEL), sample(D_ATT), sample(D_MODEL), sample(D_MODEL),
                  _resident((1, HG_DV)), _resident(hq_s.shape), _resident(hk_s.shape),
                  _resident(hi_s.shape), _resident(hgate_s.shape), sblk,
                  _resident(wa.shape), _resident(wb.shape), _resident(wo.shape),
                  _resident(w_up.shape), _resident(w_down.shape),
                  _resident((1, D_MODEL)), _resident((1, D_MODEL))],
        out_specs=[blk(D_MODEL), sample(D_MODEL), sblk],
        out_shape=[jax.ShapeDtypeStruct((n, D_MODEL), F32), jax.ShapeDtypeStruct((nb, D_MODEL), F32),
                   jax.ShapeDtypeStruct(state.shape, F32)],
        scratch_shapes=[pltpu.VMEM((nb, D_HG), F32)],
        compiler_params=_params("arbitrary"),
        name="merge_ffn",
    )(x, att, hg, sga, sgb, xs, att_s, sga_s, sgb_s, hg_norm, hq_s, hk_s, hi_s, hgate_s, state,
      wa, wb, wo, w_up, w_down, ln_ffn, ln_f)


def _sample_attn_kernel(sink_ref, qm_ref, nkt_ref, nvt_ref, mk_ref, mv_ref, o_ref):
    nb = qm_ref.shape[0]
    head = lax.broadcasted_iota(jnp.int32, (Q_HEADS, 1), 0)
    sk = jnp.zeros((Q_HEADS, 1), F32)
    for j in range(Q_HEADS):
        sk = jnp.where(head == j, sink_ref[j], sk)
    scores = [(_dot(qm_ref[b], nkt_ref[b].astype(BF16)),
               _dot_nt(qm_ref[b], mk_ref[b].astype(BF16)))
              for b in range(nb)]
    for b, (s_w, s_m) in enumerate(scores):
        m = jnp.maximum(jnp.maximum(jnp.max(s_w, axis=1, keepdims=True),
                                    jnp.max(s_m, axis=1, keepdims=True)), sk)
        e_w = jnp.exp(s_w - m)
        e_m = jnp.exp(s_m - m)
        l = (jnp.sum(e_w, axis=1, keepdims=True) + jnp.sum(e_m, axis=1, keepdims=True)
             + jnp.exp(sk - m))
        o = (_dot_nt(e_w.astype(BF16), nvt_ref[b].astype(BF16))
             + _dot(e_m.astype(BF16), mv_ref[b].astype(BF16)))
        o_ref[b] = o / l


def _sample_attention(sinks, qm, nkt, nvt, mk, mv):
    nb = nkt.shape[0]
    g = SAMPLE_ATT_GROUP

    def blk3(a, c):
        return pl.BlockSpec((g, a, c), lambda i: (i, 0, 0))

    return pl.pallas_call(
        _sample_attn_kernel,
        grid=(nb // g,),
        in_specs=[pl.BlockSpec(memory_space=pltpu.SMEM), blk3(Q_HEADS, D_KV),
                  blk3(D_KV, WINDOW), blk3(D_KV, WINDOW), blk3(N_META, D_KV), blk3(N_META, D_KV)],
        out_specs=blk3(Q_HEADS, D_KV),
        out_shape=jax.ShapeDtypeStruct((nb, Q_HEADS, D_KV), F32),
        compiler_params=_params("parallel"),
        name="sample_attn",
    )(sinks, qm, nkt, nvt, mk, mv)


def kernel(x_prompt, x_sample, cache_k, cache_v, cache_meta_k, cache_meta_v, state_hgrn, meta,
           w_in, sinks, lb_param, hg_norm, w_att_out, w_hg_out, w_o, ln_mix, ln_ffn, w_up,
           w_down, ln_f):
    bsz, seq, _ = x_prompt.shape
    nb = x_sample.shape[0]
    ln_mix2 = ln_mix.reshape(1, D_MODEL)
    ln_ffn2 = ln_ffn.reshape(1, D_MODEL)
    ln_f2 = ln_f.reshape(1, D_MODEL)
    nw = hg_norm.reshape(1, HG_DV)

    xs = x_sample.reshape(nb, D_MODEL)

    def window_t(c):
        return jnp.swapaxes(c[0].reshape(nb, WINDOW, D_KV), 1, 2)

    xp = x_prompt.reshape(bsz * seq, D_MODEL)
    (att_p, hg_p, sga_p, sgb_p, state_p, lastkv_p,
     q_s, kv_s, hq_s, hk_s, hi_s, hgate_s, sga_s, sgb_s,
     wa, wb, wo, wup, wdn, nkt_s, nvt_s) = _mixers(
        sinks.reshape(Q_HEADS), xp, xs, meta, ln_mix2, lb_param, w_in[0], nw,
        (w_att_out[0], w_hg_out[0], w_o[0], w_up[0], w_down[0]),
        (window_t(cache_k), window_t(cache_v)), bsz)

    grp = Q_HEADS // KV_HEADS
    rows = nb + N_META
    qs4 = q_s.reshape(rows, KV_HEADS, grp, 1, HEAD_DIM)
    sel = jnp.eye(KV_HEADS, dtype=BF16).reshape(1, KV_HEADS, 1, KV_HEADS, 1)
    qm = (qs4 * sel).reshape(rows, Q_HEADS, D_KV)

    o_s = _sample_attention(
        sinks.reshape(Q_HEADS), qm, nkt_s, nvt_s,
        cache_meta_k[0].reshape(nb, N_META, D_KV), cache_meta_v[0].reshape(nb, N_META, D_KV))
    o5 = o_s.reshape(nb, KV_HEADS, grp, KV_HEADS, HEAD_DIM)
    att_s = jnp.stack([o5[:, h, :, h, :] for h in range(KV_HEADS)], axis=1).reshape(nb, D_ATT)

    y_p, y_s, state_s = _merge_ffn(xp, att_p, hg_p, sga_p, sgb_p, xs, att_s, sga_s, sgb_s,
                                   nw, hq_s, hk_s, hi_s, hgate_s, state_hgrn[0],
                                   wa, wb, wo, ln_ffn2, wup, wdn, ln_f2)

    kv5 = lastkv_p.reshape(bsz, WINDOW, 2, KV_HEADS, HEAD_DIM)
    meta5 = jnp.broadcast_to(kv_s[nb:].reshape(1, N_META, 2, KV_HEADS, HEAD_DIM),
                             (bsz, N_META, 2, KV_HEADS, HEAD_DIM))
    return (y_p.reshape(bsz, seq, D_MODEL),
            y_s.reshape(nb, 1, D_MODEL),
            kv5[None, :, :, 0],
            kv5[None, :, :, 1],
            meta5[None, :, :, 0],
            meta5[None, :, :, 1],
            state_p[None],
            jnp.swapaxes(nkt_s, 1, 2).reshape(1, nb, WINDOW, KV_HEADS, HEAD_DIM),
            jnp.swapaxes(nvt_s, 1, 2).reshape(1, nb, WINDOW, KV_HEADS, HEAD_DIM),
            state_s[None])
```

```python
import functools

import jax
import jax.numpy as jnp
from jax import lax
from jax.experimental import pallas as pl
from jax.experimental.pallas import tpu as pltpu

F32 = jnp.float32
BF16 = jnp.bfloat16

D_MODEL = 1024
N_META = 16
WINDOW = 128
HEAD_DIM = 64
Q_HEADS = 8
KV_HEADS = 2
D_ATT = Q_HEADS * HEAD_DIM
D_KV = KV_HEADS * HEAD_DIM
HG_HEADS = 4
HG_DK = 128
HG_DV = 128
D_HG = HG_HEADS * HG_DK
HG_CHUNK = 64
D_FF = 4 * D_MODEL
EPS = 1e-6
C_Q = 0
C_KV = C_Q + D_ATT
C_HQ = C_KV + 2 * D_KV
C_HF = C_HQ + D_HG
C_HI = C_HF + D_HG
C_HGATE = C_HI + D_HG
C_GA = C_HGATE + D_HG
C_GB = C_GA + D_MODEL
D_IN = C_GB + D_MODEL

VMEM_LIMIT_BYTES = 56 * 1024 * 1024
MIX_ROWS = 512
PROJ_GROUP = 256
WEIGHT_SLAB_ROWS = 64
MERGE_ROWS = 512
SAMPLE_ATT_GROUP = 16
SAMPLE_HG_GROUP = 8

_NT = (((1,), (1,)), ((), ()))
_TN = (((0,), (0,)), ((), ()))


def _dot(a, b):
    return jnp.dot(a, b, preferred_element_type=F32)


def _dot_nt(a, b):
    return lax.dot_general(a, b, _NT, preferred_element_type=F32)


def _dot_tn(a, b):
    return lax.dot_general(a, b, _TN, preferred_element_type=F32)


def _rmsnorm(x, g):
    return x * lax.rsqrt(jnp.mean(x * x, axis=-1, keepdims=True) + EPS) * g


def _resident(shape):
    return pl.BlockSpec(shape, lambda *_: (0,) * len(shape), pipeline_mode=pl.Buffered(1))


def _params(*sem):
    return pltpu.CompilerParams(dimension_semantics=sem, vmem_limit_bytes=VMEM_LIMIT_BYTES)


def _lower_bound(lbp):
    e = jnp.exp(lbp - jnp.max(lbp, axis=0, keepdims=True))
    return e[0:1] / jnp.sum(e, axis=0, keepdims=True)


class _Projection:
    def __init__(self, xn, lbp, w_ref):
        self.xn = xn
        self.w_ref = w_ref
        self.lb = _lower_bound(lbp)

    def cols(self, base, part):
        return _dot(self.xn[...], self.w_ref[:, base + part.start:base + part.stop])

    def q_att(self, part=slice(0, D_ATT)):
        return self.cols(C_Q, part) * (HEAD_DIM ** -0.5)

    def kv(self):
        return self.cols(C_KV, slice(0, 2 * D_KV))

    def q_hg(self, part=slice(0, D_HG)):
        return self.cols(C_HQ, part) * (HG_DK ** -0.5)

    def forget(self, part=slice(0, D_HG)):
        lb = self.lb[:, part]
        f = lb + (1.0 - lb) * jax.nn.sigmoid(self.cols(C_HF, part))
        return 1.0 - f, jnp.log(f)

    def i_hg(self, part=slice(0, D_HG)):
        return self.cols(C_HI, part)

    def swish_gate(self, part=slice(0, D_HG)):
        g = self.cols(C_HGATE, part)
        return g * jax.nn.sigmoid(g)

    def branch_gate(self, base, part=slice(0, D_MODEL)):
        return jax.nn.sigmoid(self.cols(base, part))


class _Attention:
    def __init__(self, sink_ref, q_ref, kv, first, o_ref):
        self.sink_ref, self.q_ref, self.first, self.o_ref = sink_ref, q_ref, first, o_ref
        self.nsub = q_ref.shape[0] // WINDOW
        self.nk = 2 * WINDOW + N_META
        self.meta0 = WINDOW + q_ref.shape[0]
        lane = lax.broadcasted_iota(jnp.int32, (kv.shape[0], D_KV), 1)
        low = lane < HEAD_DIM
        k = kv[:, :D_KV]
        ksw = pltpu.roll(k, HEAD_DIM, axis=1)
        self.kboth = (jnp.where(low, k, ksw).astype(BF16), jnp.where(low, ksw, k).astype(BF16))
        qlane = lax.broadcasted_iota(jnp.int32, (1, 128), 1)
        self.keep = (jnp.where(qlane < HEAD_DIM, 1.0, 0.0).astype(BF16),
                     jnp.where(qlane < HEAD_DIM, 0.0, 1.0).astype(BF16))
        r = lax.broadcasted_iota(jnp.int32, (D_KV, D_KV), 0)
        c = lax.broadcasted_iota(jnp.int32, (D_KV, D_KV), 1)
        eye = jnp.where(r == c, 1.0, 0.0).astype(BF16)
        self.vt = _dot_nt(eye, kv[:, D_KV:].astype(BF16)).astype(BF16)
        self.ones = jnp.ones((16, self.nk), BF16)
        self.key = lax.broadcasted_iota(jnp.int32, (self.nk, 2 * WINDOW), 0)
        col = lax.broadcasted_iota(jnp.int32, (self.nk, 2 * WINDOW), 1)
        self.qry = jnp.bitwise_and(col, WINDOW - 1)
        self.first_head = lax.broadcasted_iota(jnp.int32, (1, 2 * WINDOW), 1) < WINDOW
        self.scores = {}

    def issue_scores(self, i):
        r0 = i * WINDOW
        out = []
        for p in range(Q_HEADS // 2):
            kb = self.kboth[(2 * p) // (Q_HEADS // KV_HEADS)]
            kmat = jnp.concatenate([kb[r0:r0 + 2 * WINDOW], kb[self.meta0:]], axis=0)
            qp = self.q_ref[r0:r0 + WINDOW, p * 128:(p + 1) * 128]
            q2 = jnp.concatenate([qp * self.keep[0], qp * self.keep[1]], axis=0)
            out.append(_dot_nt(kmat, q2))
        self.scores[i] = out

    def finish(self, i):
        r0 = i * WINDOW
        lo = jnp.where(self.first, WINDOW - 1, self.qry) if i == 0 else self.qry
        mask = jnp.logical_and(self.key > lo, self.key <= self.qry + WINDOW)
        mask = jnp.logical_or(mask, self.key >= 2 * WINDOW)
        vaug = []
        for h in range(KV_HEADS):
            vth = self.vt[h * HEAD_DIM:(h + 1) * HEAD_DIM]
            vaug.append(jnp.concatenate(
                [jnp.concatenate([vth[:, r0:r0 + 2 * WINDOW], vth[:, self.meta0:]], axis=1),
                 self.ones], axis=0))
        outs = []
        for p, raw in enumerate(self.scores.pop(i)):
            h = (2 * p) // (Q_HEADS // KV_HEADS)
            s = jnp.where(mask, raw, -jnp.inf)
            sk = jnp.where(self.first_head, self.sink_ref[2 * p], self.sink_ref[2 * p + 1])
            m = jnp.maximum(jnp.max(s, axis=0, keepdims=True), sk)
            e = jnp.exp(s - m).astype(BF16)
            oa = _dot(vaug[h], e)
            o = oa[:HEAD_DIM] / (oa[HEAD_DIM:HEAD_DIM + 1] + jnp.exp(sk - m))
            outs += [o[:, :WINDOW], o[:, WINDOW:]]
        self.o_ref[r0:r0 + WINDOW, :] = jnp.concatenate(outs, axis=0).T.astype(self.o_ref.dtype)


def _cumsum_rows(x):
    t = x.shape[0]
    row = lax.broadcasted_iota(jnp.int32, x.shape, 0)
    d = 1
    while d < t:
        x = x + jnp.where(row >= d, pltpu.roll(x, d, axis=0), 0.0)
        d *= 2
    return x


def _hg_out(o, gate, nw):
    parts = []
    for h in range(HG_HEADS):
        oh = o[:, h * HG_DV:(h + 1) * HG_DV]
        parts.append(oh * lax.rsqrt(jnp.mean(oh * oh, axis=-1, keepdims=True) + EPS) * nw)
    return jnp.concatenate(parts, axis=1) * gate


_HG_SLICES = [slice(h * HG_DK, (h + 1) * HG_DK) for h in range(HG_HEADS)]


def _per_head_rows(a):
    head = lax.broadcasted_iota(jnp.int32, (1, D_HG), 1) // HG_DK
    return jnp.concatenate([a * jnp.where(head == h, 1.0, 0.0).astype(a.dtype)
                            for h in range(HG_HEADS)], axis=0)


def _state_update(iv, kd):
    stacked = jnp.concatenate([iv[:, hs] for hs in _HG_SLICES], axis=0)
    return _dot_tn(stacked, _per_head_rows(kd))


def _meta_state(mk, mlogf, mi):
    b = _cumsum_rows(mlogf)
    kd = (mk * jnp.exp(b[-1:] - b)).astype(BF16)
    return _state_update(mi.astype(BF16), kd)


class _Hgrn:
    def __init__(self, q_ref, k_ref, logf_ref, i_ref, g_ref, nw, state, o_ref):
        self.q_ref, self.k_ref, self.logf_ref, self.i_ref, self.g_ref = q_ref, k_ref, logf_ref, i_ref, g_ref
        self.nw, self.state, self.o_ref = nw, state, o_ref
        t = HG_CHUNK
        self.nchunk = q_ref.shape[0] // t
        r = lax.broadcasted_iota(jnp.int32, (t, HG_HEADS * t), 0)
        c = lax.broadcasted_iota(jnp.int32, (t, HG_HEADS * t), 1)
        self.causal = r >= jnp.bitwise_and(c, t - 1)
        self.cums, self.terms, self.entering = {}, {}, {}

    def rows(self, ci):
        return slice(ci * HG_CHUNK, (ci + 1) * HG_CHUNK)

    def issue_cumsum(self, ci):
        self.cums[ci] = _cumsum_rows(self.logf_ref[self.rows(ci), :])

    def issue_local(self, ci):
        rows = self.rows(ci)
        b = self.cums.pop(ci)
        bl = b[-1:]
        k = self.k_ref[rows, :].astype(F32)
        qe = (self.q_ref[rows, :].astype(F32) * jnp.exp(b)).astype(BF16)
        ke = (k * jnp.exp(-b)).astype(BF16)
        kd = (k * jnp.exp(bl - b)).astype(BF16)
        iv = self.i_ref[rows, :].astype(BF16)
        a = jnp.where(self.causal, _dot_nt(qe, _per_head_rows(ke)), 0.0).astype(BF16)
        self.terms[ci] = (qe, _per_head_rows(iv), a)
        self.entering[ci] = self.state.astype(BF16)
        self.state = self.state * jnp.exp(bl) + _state_update(iv, kd)

    def finish(self, ci):
        rows = self.rows(ci)
        qe, iv_heads, a = self.terms.pop(ci)
        ent = self.entering.pop(ci)
        inter = jnp.concatenate([_dot_nt(qe[:, hs], ent[:, hs]) for hs in _HG_SLICES], axis=1)
        o = _hg_out(_dot(a, iv_heads) + inter, self.g_ref[rows, :].astype(F32), self.nw)
        self.o_ref[rows, :] = o.astype(self.o_ref.dtype)


def _load_as_bf16(w_hbm, w_vmem, stage, sem):
    rows = stage.shape[1]
    nslab = w_hbm.shape[0] // rows

    def slab_copy(c):
        return pltpu.make_async_copy(w_hbm.at[pl.ds(c * rows, rows), :], stage.at[c % 2], sem.at[c % 2])

    slab_copy(0).start()
    for c in range(nslab):
        if c + 1 < nslab:
            slab_copy(c + 1).start()
        slab_copy(c).wait()
        w_vmem[pl.ds(c * rows, rows), :] = stage[c % 2].astype(w_vmem.dtype)


def _mixer_kernel(sink_ref, x0_ref, xnext_ref, xs_ref, xm_ref, g_ref, lbp_ref, w_hbm, nw_ref,
                  wa32, wb32, wo32, wup32, wdn32,
                  att_ref, hg_ref, sga_ref, sgb_ref, sfin_ref, lastkv_ref,
                  qs_ref, kvs_ref, hqs_ref, hks_ref, his_ref, hgs_ref, sgas_ref, sgbs_ref,
                  wa16, wb16, wo16, wup16, wdn16,
                  w_ref, stage, sem, xn_s, zq, zkv, zhq, zhk, zlogf, zhi, zhg, kvm_ref, st_ref, mst_ref,
                  *, tiles_per_seq):
    s = pl.program_id(0)
    t = xnext_ref.shape[0]

    @pl.when(s == 0)
    def _():
        _load_as_bf16(w_hbm, w_ref, stage, sem)
        xn_s[...] = _rmsnorm(x0_ref[...], g_ref[...]).astype(xn_s.dtype)
        for ref in (zq, zkv, zhq, zhk, zlogf, zhi, zhg, st_ref):
            ref[...] = jnp.zeros(ref.shape, ref.dtype)
        nb = xs_ref.shape[0]
        small = jnp.concatenate([xs_ref[...], xm_ref[...]], axis=0)
        p = _Projection(_rmsnorm(small, g_ref[...]).astype(BF16), lbp_ref[...], w_ref)
        qs_ref[...] = p.q_att().astype(qs_ref.dtype)
        kv = p.kv()
        kvs_ref[...] = kv
        kvm_ref[...] = kv[nb:]
        hqs_ref[...] = p.q_hg()
        k, logf = p.forget()
        hks_ref[...] = k
        iv = p.i_hg()
        his_ref[...] = iv
        hgs_ref[...] = p.swish_gate()
        sgas_ref[...] = p.branch_gate(C_GA).astype(sgas_ref.dtype)
        sgbs_ref[...] = p.branch_gate(C_GB).astype(sgbs_ref.dtype)
        mst_ref[...] = _meta_state(k[nb:], logf[nb:], iv[nb:])

    first = lax.rem(jnp.maximum(s - 1, 0), tiles_per_seq) == 0

    att = _Attention(sink_ref, zq, jnp.concatenate([zkv[...], kvm_ref[...]], axis=0), first, att_ref)
    last = zkv[t:t + WINDOW, :]
    lastkv_ref[...] = last
    zkv[0:WINDOW, :] = last
    entering = jnp.where(first, mst_ref[...], st_ref[...])
    hg = _Hgrn(zhq, zhk, zlogf, zhi, zhg, nw_ref[...], entering, hg_ref)
    proj = _Projection(xn_s, lbp_ref[...], w_ref)

    def gate_a(p):
        sga_ref[:, p] = proj.branch_gate(C_GA, p).astype(sga_ref.dtype)

    def gate_b(p):
        sgb_ref[:, p] = proj.branch_gate(C_GB, p).astype(sgb_ref.dtype)

    def new_kv(_):
        zkv[WINDOW:WINDOW + t, :] = proj.kv()

    def new_forget(p):
        k_new, logf_new = proj.forget(p)
        zhk[:, p] = k_new.astype(zhk.dtype)
        zlogf[:, p] = logf_new

    def new_q(p):
        zq[:, p] = proj.q_att(p).astype(zq.dtype)

    def new_hq(p):
        zhq[:, p] = proj.q_hg(p).astype(zhq.dtype)

    def new_hi(p):
        zhi[:, p] = proj.i_hg(p).astype(zhi.dtype)

    def new_gate(p):
        zhg[:, p] = proj.swish_gate(p).astype(zhg.dtype)

    def parts(n):
        return [slice(c, c + PROJ_GROUP) for c in range(0, n, PROJ_GROUP)]

    groups = ([(gate_a, p) for p in parts(D_MODEL)] + [(gate_b, p) for p in parts(D_MODEL)]
              + [(new_kv, None)] + [(new_q, p) for p in parts(D_ATT)]
              + [(new_forget, p) for p in parts(D_HG)] + [(new_hq, p) for p in parts(D_HG)]
              + [(new_hi, p) for p in parts(D_HG)] + [(new_gate, p) for p in parts(D_HG)])

    def save_state():
        st_ref[...] = hg.state
        for h, hs in enumerate(_HG_SLICES):
            sfin_ref[h] = hg.state[:, hs].T

    qk, cs, loc, fin, out = att.issue_scores, hg.issue_cumsum, hg.issue_local, att.finish, hg.finish
    mixer_work = {
        0: [(qk, 0), (cs, 0), (cs, 1), (cs, 2), (cs, 3)],
        1: [(cs, 4), (cs, 5), (loc, 0)],
        2: [(cs, 6), (cs, 7), (loc, 1)],
        3: [(qk, 1), (loc, 2)],
        4: [(fin, 0), (loc, 3)],
        5: [(out, 0), (loc, 4)],
        6: [(qk, 2), (out, 1), (loc, 5)],
        7: [(fin, 1), (out, 2), (loc, 6)],
        8: [(qk, 3), (out, 3), (loc, 7), (save_state,)],
        9: [(out, 4)],
        10: [(fin, 2), (out, 5)],
        11: [(out, 6)],
        12: [(out, 7)],
        13: [(fin, 3)],
    }
    for slot, (fn, p) in enumerate(groups):
        fn(p)
        for item in mixer_work.get(slot, []):
            item[0](*item[1:])
    xn_s[...] = _rmsnorm(xnext_ref[...], g_ref[...]).astype(xn_s.dtype)
    for src, dst in ((wa32, wa16), (wb32, wb16), (wo32, wo16), (wup32, wup16), (wdn32, wdn16)):
        dst[...] = src[...].astype(dst.dtype)


def _mixers(sinks, x, x_sample, x_meta, g, lb_param, w_f32, hg_norm, later_weights, bsz):
    n = x.shape[0]
    t = MIX_ROWS
    nt = n // t
    per_seq = nt // bsz
    rows = x_sample.shape[0] + x_meta.shape[0]

    def this_tile(c):
        return pl.BlockSpec((t, c), lambda s: (jnp.minimum(s, nt - 1), 0))

    def prev_tile(c):
        return pl.BlockSpec((t, c), lambda s: (jnp.maximum(s - 1, 0), 0))

    def prev_seq(shape):
        return pl.BlockSpec((None,) + shape,
                            lambda s: (jnp.maximum(s - 1, 0) // per_seq,) + (0,) * len(shape))

    def small(c):
        return pl.BlockSpec((rows, c), lambda s: (0, 0))

    first_tile = pl.BlockSpec((t, D_MODEL), lambda s: (0, 0), pipeline_mode=pl.Buffered(1))
    next_tile = pl.BlockSpec((t, D_MODEL), lambda s: (jnp.minimum(s + 1, nt - 1), 0))
    small_widths = (D_ATT, 2 * D_KV, D_HG, D_HG, D_HG, D_HG, D_MODEL, D_MODEL)
    small_dtypes = (BF16, F32, F32, F32, F32, F32, BF16, BF16)

    def row_block(wt):
        return pl.BlockSpec((wt.shape[0] // nt, wt.shape[1]), lambda s: (jnp.minimum(s, nt - 1), 0))

    return pl.pallas_call(
        functools.partial(_mixer_kernel, tiles_per_seq=per_seq),
        grid=(nt + 1,),
        in_specs=[pl.BlockSpec(memory_space=pltpu.SMEM), first_tile, next_tile,
                  _resident(x_sample.shape), _resident(x_meta.shape), _resident((1, D_MODEL)),
                  _resident(lb_param.shape), pl.BlockSpec(memory_space=pl.ANY), _resident((1, HG_DV))]
                 + [row_block(wt) for wt in later_weights],
        out_specs=[prev_tile(D_ATT), prev_tile(D_HG), this_tile(D_MODEL), this_tile(D_MODEL),
                   prev_seq((HG_HEADS, HG_DK, HG_DV)), prev_seq((WINDOW, 2 * D_KV))]
                  + [small(c) for c in small_widths] + [row_block(wt) for wt in later_weights],
        out_shape=[jax.ShapeDtypeStruct((n, D_ATT), BF16), jax.ShapeDtypeStruct((n, D_HG), BF16),
                   jax.ShapeDtypeStruct((n, D_MODEL), BF16), jax.ShapeDtypeStruct((n, D_MODEL), BF16),
                   jax.ShapeDtypeStruct((bsz, HG_HEADS, HG_DK, HG_DV), F32),
                   jax.ShapeDtypeStruct((bsz, WINDOW, 2 * D_KV), F32)]
                  + [jax.ShapeDtypeStruct((rows, c), d) for c, d in zip(small_widths, small_dtypes)]
                  + [jax.ShapeDtypeStruct(wt.shape, BF16) for wt in later_weights],
        scratch_shapes=[pltpu.VMEM((D_MODEL, D_IN), BF16),
                        pltpu.VMEM((2, WEIGHT_SLAB_ROWS, D_IN), F32),
                        pltpu.SemaphoreType.DMA((2,)),
                        pltpu.VMEM((t, D_MODEL), BF16),
                        pltpu.VMEM((t, D_ATT), BF16), pltpu.VMEM((WINDOW + t, 2 * D_KV), F32),
                        pltpu.VMEM((t, D_HG), BF16), pltpu.VMEM((t, D_HG), BF16),
                        pltpu.VMEM((t, D_HG), F32), pltpu.VMEM((t, D_HG), BF16),
                        pltpu.VMEM((t, D_HG), BF16),
                        pltpu.VMEM((N_META, 2 * D_KV), F32),
                        pltpu.VMEM((HG_DV, D_HG), F32),
                        pltpu.VMEM((HG_DV, D_HG), F32)],
        compiler_params=_params("arbitrary"),
        name="mixers",
    )(sinks, x, x, x_sample, x_meta, g, lb_param, w_f32, hg_norm, *later_weights)


def _merge_ffn_rows(x, att, hg, sga, sgb, wa_ref, wb_ref, wo_ref, ln_ffn, wup_ref, wdn_ref, ln_f):
    ya = _dot(att.astype(BF16), wa_ref[...])
    yb = _dot(hg.astype(BF16), wb_ref[...])
    mix = sga.astype(F32) * ya + sgb.astype(F32) * yb
    h1 = x + _dot(mix.astype(BF16), wo_ref[...])
    xn = _rmsnorm(h1, ln_ffn).astype(BF16)
    acc = jnp.zeros_like(h1)
    step = 1024
    for c in range(D_FF // step):
        u = jnp.maximum(_dot(xn, wup_ref[:, c * step:(c + 1) * step]), 0.0)
        acc = acc + _dot((u * u).astype(BF16), wdn_ref[c * step:(c + 1) * step, :])
    return _rmsnorm(h1 + acc, ln_f)


def _sample_hgrn_group(rows, nw, q_ref, k_ref, i_ref, g_ref, s_ref, snew_ref, hg_ref):
    k = k_ref[rows, :]
    q = q_ref[rows, :]
    iv = i_ref[rows, :]
    r = lax.broadcasted_iota(jnp.int32, (HG_DK, HG_DK), 0)
    c = lax.broadcasted_iota(jnp.int32, (HG_DK, HG_DK), 1)
    eye = r == c

    def column(row):
        return jnp.sum(jnp.where(eye, row, 0.0), axis=1, keepdims=True)

    outs = []
    for b in range(s_ref.shape[0]):
        heads = []
        for h, hs in enumerate(_HG_SLICES):
            kc = column(k[b:b + 1, hs])
            qc = column(q[b:b + 1, hs])
            s_old = s_ref[b, h]
            s = s_old + kc * (iv[b:b + 1, hs] - s_old)
            snew_ref[b, h] = s
            heads.append(jnp.sum(qc * s, axis=0, keepdims=True))
        outs.append(jnp.concatenate(heads, axis=1))
    hg_ref[rows, :] = _hg_out(jnp.concatenate(outs, axis=0), g_ref[rows, :], nw)


def _merge_ffn_kernel(x_ref, att_ref, hg_ref, sga_ref, sgb_ref,
                      xs_ref, atts_ref, sgas_ref, sgbs_ref,
                      nw_ref, hqs_ref, hks_ref, his_ref, hgates_ref, state_ref,
                      wa_ref, wb_ref, wo_ref, wup_ref, wdn_ref, lnffn_ref, lnf_ref,
                      y_ref, ys_ref, snew_ref, hgs, *, prompt_steps):
    i = pl.program_id(0)
    weights = (wa_ref, wb_ref, wo_ref, lnffn_ref[...], wup_ref, wdn_ref, lnf_ref[...])
    group = state_ref.shape[0]
    ngroups = xs_ref.shape[0] // group

    @pl.when(i < prompt_steps)
    def _():
        y_ref[...] = _merge_ffn_rows(x_ref[...], att_ref[...], hg_ref[...], sga_ref[...], sgb_ref[...],
                                     *weights)
        rows = pl.ds(pl.multiple_of(jnp.minimum(i, ngroups - 1) * group, group), group)
        _sample_hgrn_group(rows, nw_ref[...], hqs_ref, hks_ref, his_ref, hgates_ref,
                           state_ref, snew_ref, hgs)

    @pl.when(i == prompt_steps)
    def _():
        ys_ref[...] = _merge_ffn_rows(xs_ref[...], atts_ref[...], hgs[...], sgas_ref[...],
                                      sgbs_ref[...], *weights)


def _merge_ffn(x, att, hg, sga, sgb, xs, att_s, sga_s, sgb_s, hg_norm, hq_s, hk_s, hi_s, hgate_s, state,
               wa, wb, wo, ln_ffn, w_up, w_down, ln_f):
    n = x.shape[0]
    nb = xs.shape[0]
    rows = MERGE_ROWS
    nt = n // rows
    g = SAMPLE_HG_GROUP
    assert nb // g <= nt

    def blk(c):
        return pl.BlockSpec((rows, c), lambda i: (jnp.minimum(i, nt - 1), 0))

    def sample(c):
        return pl.BlockSpec((nb, c), lambda i: (0, 0))

    sblk = pl.BlockSpec((g, HG_HEADS, HG_DK, HG_DV), lambda i: (jnp.minimum(i, nb // g - 1), 0, 0, 0))
    return pl.pallas_call(
        functools.partial(_merge_ffn_kernel, prompt_steps=nt),
        grid=(nt + 1,),
        in_specs=[blk(D_MODEL), blk(D_ATT), blk(D_HG), blk(D_MODEL), blk(D_MODEL),
                  sample(D_MODEL), sample(D_ATT), sample(D_MODEL), sample(D_MODEL),
                  _resident((1, HG_DV)), _resident(hq_s.shape), _resident(hk_s.shape),
                  _resident(hi_s.shape), _resident(hgate_s.shape), sblk,
                  _resident(wa.shape), _resident(wb.shape), _resident(wo.shape),
                  _resident(w_up.shape), _resident(w_down.shape),
                  _resident((1, D_MODEL)), _resident((1, D_MODEL))],
        out_specs=[blk(D_MODEL), sample(D_MODEL), sblk],
        out_shape=[jax.ShapeDtypeStruct((n, D_MODEL), F32), jax.ShapeDtypeStruct((nb, D_MODEL), F32),
                   jax.ShapeDtypeStruct(state.shape, F32)],
        scratch_shapes=[pltpu.VMEM((nb, D_HG), F32)],
        compiler_params=_params("arbitrary"),
        name="merge_ffn",
    )(x, att, hg, sga, sgb, xs, att_s, sga_s, sgb_s, hg_norm, hq_s, hk_s, hi_s, hgate_s, state,
      wa, wb, wo, w_up, w_down, ln_ffn, ln_f)


def _sample_attn_kernel(sink_ref, qm_ref, kvn_ref, ckt_ref, cvt_ref, mk_ref, mv_ref,
                        o_ref, nkt_ref, nvt_ref):
    nb = qm_ref.shape[0]
    head = lax.broadcasted_iota(jnp.int32, (Q_HEADS, 1), 0)
    sk = jnp.zeros((Q_HEADS, 1), F32)
    for j in range(Q_HEADS):
        sk = jnp.where(head == j, sink_ref[j], sk)
    newest = lax.broadcasted_iota(jnp.int32, (D_KV, WINDOW), 1) == WINDOW - 1
    kvn = kvn_ref[...]
    hi = kvn.astype(BF16)
    rest = kvn - hi.astype(F32)
    mid = rest.astype(BF16)
    lo = (rest - mid.astype(F32)).astype(BF16)
    r = lax.broadcasted_iota(jnp.int32, (2 * D_KV, 2 * D_KV), 0)
    c = lax.broadcasted_iota(jnp.int32, (2 * D_KV, 2 * D_KV), 1)
    eye = jnp.where(r == c, 1.0, 0.0).astype(BF16)
    pieces_t = _dot_nt(eye, jnp.concatenate([hi, mid, lo], axis=0)).astype(BF16)
    pr = lax.broadcasted_iota(jnp.int32, (3 * nb, WINDOW), 0)
    pc = lax.broadcasted_iota(jnp.int32, (3 * nb, WINDOW), 1)
    for b in range(nb):
        pick = jnp.logical_and(lax.rem(pr, nb) == b, pc == WINDOW - 1)
        new_cols = _dot(pieces_t, jnp.where(pick, 1.0, 0.0).astype(BF16))
        nkt_ref[b] = jnp.where(newest, new_cols[:D_KV], pltpu.roll(ckt_ref[b], WINDOW - 1, axis=1))
        nvt_ref[b] = jnp.where(newest, new_cols[D_KV:], pltpu.roll(cvt_ref[b], WINDOW - 1, axis=1))
    scores = [(_dot(qm_ref[b], nkt_ref[b].astype(BF16)),
               _dot_nt(qm_ref[b], mk_ref[b].astype(BF16)))
              for b in range(nb)]
    for b, (s_w, s_m) in enumerate(scores):
        m = jnp.maximum(jnp.maximum(jnp.max(s_w, axis=1, keepdims=True),
                                    jnp.max(s_m, axis=1, keepdims=True)), sk)
        e_w = jnp.exp(s_w - m)
        e_m = jnp.exp(s_m - m)
        l = (jnp.sum(e_w, axis=1, keepdims=True) + jnp.sum(e_m, axis=1, keepdims=True)
             + jnp.exp(sk - m))
        o = (_dot_nt(e_w.astype(BF16), nvt_ref[b].astype(BF16))
             + _dot(e_m.astype(BF16), mv_ref[b].astype(BF16)))
        o_ref[b] = o / l


def _sample_attention(sinks, qm, kv_new, ck, cv, mk, mv):
    nb = ck.shape[0]
    g = SAMPLE_ATT_GROUP

    def blk3(a, c):
        return pl.BlockSpec((g, a, c), lambda i: (i, 0, 0))

    return pl.pallas_call(
        _sample_attn_kernel,
        grid=(nb // g,),
        in_specs=[pl.BlockSpec(memory_space=pltpu.SMEM), blk3(Q_HEADS, D_KV),
                  pl.BlockSpec((g, 2 * D_KV), lambda i: (i, 0)),
                  blk3(D_KV, WINDOW), blk3(D_KV, WINDOW), blk3(N_META, D_KV), blk3(N_META, D_KV)],
        out_specs=[blk3(Q_HEADS, D_KV), blk3(D_KV, WINDOW), blk3(D_KV, WINDOW)],
        out_shape=[jax.ShapeDtypeStruct((nb, Q_HEADS, D_KV), F32),
                   jax.ShapeDtypeStruct((nb, D_KV, WINDOW), F32),
                   jax.ShapeDtypeStruct((nb, D_KV, WINDOW), F32)],
        compiler_params=_params("parallel"),
        name="sample_attn",
    )(sinks, qm, kv_new, ck, cv, mk, mv)


def kernel(x_prompt, x_sample, cache_k, cache_v, cache_meta_k, cache_meta_v, state_hgrn, meta,
           w_in, sinks, lb_param, hg_norm, w_att_out, w_hg_out, w_o, ln_mix, ln_ffn, w_up,
           w_down, ln_f):
    bsz, seq, _ = x_prompt.shape
    nb = x_sample.shape[0]
    ln_mix2 = ln_mix.reshape(1, D_MODEL)
    ln_ffn2 = ln_ffn.reshape(1, D_MODEL)
    ln_f2 = ln_f.reshape(1, D_MODEL)
    nw = hg_norm.reshape(1, HG_DV)

    xs = x_sample.reshape(nb, D_MODEL)

    xp = x_prompt.reshape(bsz * seq, D_MODEL)
    (att_p, hg_p, sga_p, sgb_p, state_p, lastkv_p,
     q_s, kv_s, hq_s, hk_s, hi_s, hgate_s, sga_s, sgb_s,
     wa, wb, wo, wup, wdn) = _mixers(
        sinks.reshape(Q_HEADS), xp, xs, meta, ln_mix2, lb_param, w_in[0], nw,
        (w_att_out[0], w_hg_out[0], w_o[0], w_up[0], w_down[0]), bsz)

    grp = Q_HEADS // KV_HEADS
    rows = nb + N_META
    qs4 = q_s.reshape(rows, KV_HEADS, grp, 1, HEAD_DIM)
    sel = jnp.eye(KV_HEADS, dtype=BF16).reshape(1, KV_HEADS, 1, KV_HEADS, 1)
    qm = (qs4 * sel).reshape(rows, Q_HEADS, D_KV)

    def window_t(c):
        return jnp.swapaxes(c[0].reshape(nb, WINDOW, D_KV), 1, 2)

    o_s, nkt_s, nvt_s = _sample_attention(
        sinks.reshape(Q_HEADS), qm, kv_s, window_t(cache_k), window_t(cache_v),
        cache_meta_k[0].reshape(nb, N_META, D_KV), cache_meta_v[0].reshape(nb, N_META, D_KV))
    o5 = o_s.reshape(nb, KV_HEADS, grp, KV_HEADS, HEAD_DIM)
    att_s = jnp.stack([o5[:, h, :, h, :] for h in range(KV_HEADS)], axis=1).reshape(nb, D_ATT)

    y_p, y_s, state_s = _merge_ffn(xp, att_p, hg_p, sga_p, sgb_p, xs, att_s, sga_s, sgb_s,
                                   nw, hq_s, hk_s, hi_s, hgate_s, state_hgrn[0],
                                   wa, wb, wo, ln_ffn2, wup, wdn, ln_f2)

    kv5 = lastkv_p.reshape(bsz, WINDOW, 2, KV_HEADS, HEAD_DIM)
    meta5 = jnp.broadcast_to(kv_s[nb:].reshape(1, N_META, 2, KV_HEADS, HEAD_DIM),
                             (bsz, N_META, 2, KV_HEADS, HEAD_DIM))
    return (y_p.reshape(bsz, seq, D_MODEL),
            y_s.reshape(nb, 1, D_MODEL),
            kv5[None, :, :, 0],
            kv5[None, :, :, 1],
            meta5[None, :, :, 0],
            meta5[None, :, :, 1],
            state_p[None],
            jnp.swapaxes(nkt_s, 1, 2).reshape(1, nb, WINDOW, KV_HEADS, HEAD_DIM),
            jnp.swapaxes(nvt_s, 1, 2).reshape(1, nb, WINDOW, KV_HEADS, HEAD_DIM),
            state_s[None])
```

```python
import functools

import jax
import jax.numpy as jnp
from jax import lax
from jax.experimental import pallas as pl
from jax.experimental.pallas import tpu as pltpu

F32 = jnp.float32
BF16 = jnp.bfloat16

D_MODEL = 1024
N_META = 16
WINDOW = 128
HEAD_DIM = 64
Q_HEADS = 8
KV_HEADS = 2
D_ATT = Q_HEADS * HEAD_DIM
D_KV = KV_HEADS * HEAD_DIM
HG_HEADS = 4
HG_DK = 128
HG_DV = 128
D_HG = HG_HEADS * HG_DK
HG_CHUNK = 64
D_FF = 4 * D_MODEL
EPS = 1e-6
C_Q = 0
C_KV = C_Q + D_ATT
C_HQ = C_KV + 2 * D_KV
C_HF = C_HQ + D_HG
C_HI = C_HF + D_HG
C_HGATE = C_HI + D_HG
C_GA = C_HGATE + D_HG
C_GB = C_GA + D_MODEL
D_IN = C_GB + D_MODEL

VMEM_LIMIT_BYTES = 56 * 1024 * 1024
MIX_ROWS = 512
PROJ_GROUP = 256
WEIGHT_SLAB_ROWS = 64
MERGE_ROWS = 512
FFN_CHUNK = 2048
LANES = 128
BF16_ROWS = 16
SAMPLE_ATT_GROUP = 16
SAMPLE_HG_GROUP = 8

_NT = (((1,), (1,)), ((), ()))
_TN = (((0,), (0,)), ((), ()))


def _dot(a, b):
    return jnp.dot(a, b, preferred_element_type=F32)


def _dot_nt(a, b):
    return lax.dot_general(a, b, _NT, preferred_element_type=F32)


def _dot_tn(a, b):
    return lax.dot_general(a, b, _TN, preferred_element_type=F32)


def _rmsnorm(x, g):
    return x * lax.rsqrt(jnp.mean(x * x, axis=-1, keepdims=True) + EPS) * g


def _resident(shape):
    return pl.BlockSpec(shape, lambda *_: (0,) * len(shape), pipeline_mode=pl.Buffered(1))


def _params(*sem):
    return pltpu.CompilerParams(dimension_semantics=sem, vmem_limit_bytes=VMEM_LIMIT_BYTES)


def _lower_bound(lbp):
    e = jnp.exp(lbp - jnp.max(lbp, axis=0, keepdims=True))
    return e[0:1] / jnp.sum(e, axis=0, keepdims=True)


class _Projection:
    def __init__(self, xn, lbp, w_ref):
        self.xn = xn
        self.w_ref = w_ref
        self.lb = _lower_bound(lbp)

    def cols(self, base, part):
        return _dot(self.xn[...], self.w_ref[:, base + part.start:base + part.stop])

    def q_att(self, part=slice(0, D_ATT)):
        return self.cols(C_Q, part) * (HEAD_DIM ** -0.5)

    def kv(self):
        return self.cols(C_KV, slice(0, 2 * D_KV))

    def q_hg(self, part=slice(0, D_HG)):
        return self.cols(C_HQ, part) * (HG_DK ** -0.5)

    def forget(self, part=slice(0, D_HG)):
        lb = self.lb[:, part]
        f = lb + (1.0 - lb) * jax.nn.sigmoid(self.cols(C_HF, part))
        return 1.0 - f, jnp.log(f)

    def i_hg(self, part=slice(0, D_HG)):
        return self.cols(C_HI, part)

    def swish_gate(self, part=slice(0, D_HG)):
        g = self.cols(C_HGATE, part)
        return g * jax.nn.sigmoid(g)

    def branch_gate(self, base, part=slice(0, D_MODEL)):
        return jax.nn.sigmoid(self.cols(base, part))


class _Attention:
    def __init__(self, sink_ref, q_ref, kv, first, o_ref):
        self.sink_ref, self.q_ref, self.first, self.o_ref = sink_ref, q_ref, first, o_ref
        self.nsub = q_ref.shape[0] // WINDOW
        self.nk = 2 * WINDOW + N_META
        self.meta0 = WINDOW + q_ref.shape[0]
        lane = lax.broadcasted_iota(jnp.int32, (kv.shape[0], D_KV), 1)
        low = lane < HEAD_DIM
        k = kv[:, :D_KV]
        ksw = pltpu.roll(k, HEAD_DIM, axis=1)
        self.kboth = (jnp.where(low, k, ksw).astype(BF16), jnp.where(low, ksw, k).astype(BF16))
        qlane = lax.broadcasted_iota(jnp.int32, (1, LANES), 1)
        self.keep = (jnp.where(qlane < HEAD_DIM, 1.0, 0.0).astype(BF16),
                     jnp.where(qlane < HEAD_DIM, 0.0, 1.0).astype(BF16))
        r = lax.broadcasted_iota(jnp.int32, (D_KV, D_KV), 0)
        c = lax.broadcasted_iota(jnp.int32, (D_KV, D_KV), 1)
        eye = jnp.where(r == c, 1.0, 0.0).astype(BF16)
        self.vt = _dot_nt(eye, kv[:, D_KV:].astype(BF16)).astype(BF16)
        self.ones = jnp.ones((BF16_ROWS, self.nk), BF16)
        self.key = lax.broadcasted_iota(jnp.int32, (self.nk, 2 * WINDOW), 0)
        col = lax.broadcasted_iota(jnp.int32, (self.nk, 2 * WINDOW), 1)
        self.qry = jnp.bitwise_and(col, WINDOW - 1)
        self.first_head = lax.broadcasted_iota(jnp.int32, (1, 2 * WINDOW), 1) < WINDOW
        self.scores = {}

    def issue_scores(self, i):
        r0 = i * WINDOW
        out = []
        for p in range(Q_HEADS // 2):
            kb = self.kboth[(2 * p) // (Q_HEADS // KV_HEADS)]
            kmat = jnp.concatenate([kb[r0:r0 + 2 * WINDOW], kb[self.meta0:]], axis=0)
            qp = self.q_ref[r0:r0 + WINDOW, p * LANES:(p + 1) * LANES]
            q2 = jnp.concatenate([qp * self.keep[0], qp * self.keep[1]], axis=0)
            out.append(_dot_nt(kmat, q2))
        self.scores[i] = out

    def finish(self, i):
        r0 = i * WINDOW
        lo = jnp.where(self.first, WINDOW - 1, self.qry) if i == 0 else self.qry
        mask = jnp.logical_and(self.key > lo, self.key <= self.qry + WINDOW)
        mask = jnp.logical_or(mask, self.key >= 2 * WINDOW)
        vaug = []
        for h in range(KV_HEADS):
            vth = self.vt[h * HEAD_DIM:(h + 1) * HEAD_DIM]
            vaug.append(jnp.concatenate(
                [jnp.concatenate([vth[:, r0:r0 + 2 * WINDOW], vth[:, self.meta0:]], axis=1),
                 self.ones], axis=0))
        outs = []
        for p, raw in enumerate(self.scores.pop(i)):
            h = (2 * p) // (Q_HEADS // KV_HEADS)
            s = jnp.where(mask, raw, -jnp.inf)
            sk = jnp.where(self.first_head, self.sink_ref[2 * p], self.sink_ref[2 * p + 1])
            m = jnp.maximum(jnp.max(s, axis=0, keepdims=True), sk)
            e = jnp.exp(s - m).astype(BF16)
            oa = _dot(vaug[h], e)
            o = oa[:HEAD_DIM] / (oa[HEAD_DIM:HEAD_DIM + 1] + jnp.exp(sk - m))
            outs += [o[:, :WINDOW], o[:, WINDOW:]]
        self.o_ref[r0:r0 + WINDOW, :] = jnp.concatenate(outs, axis=0).T.astype(self.o_ref.dtype)


def _cumsum_rows(x):
    t = x.shape[0]
    row = lax.broadcasted_iota(jnp.int32, x.shape, 0)
    d = 1
    while d < t:
        x = x + jnp.where(row >= d, pltpu.roll(x, d, axis=0), 0.0)
        d *= 2
    return x


def _hg_out(o, gate, nw):
    parts = []
    for h in range(HG_HEADS):
        oh = o[:, h * HG_DV:(h + 1) * HG_DV]
        parts.append(oh * lax.rsqrt(jnp.mean(oh * oh, axis=-1, keepdims=True) + EPS) * nw)
    return jnp.concatenate(parts, axis=1) * gate


_HG_SLICES = [slice(h * HG_DK, (h + 1) * HG_DK) for h in range(HG_HEADS)]


def _state_update(iv, kd):
    return jnp.concatenate([_dot_tn(iv[:, hs], kd[:, hs]) for hs in _HG_SLICES], axis=1)


def _meta_state(mk, mlogf, mi):
    b = _cumsum_rows(mlogf)
    kd = (mk * jnp.exp(b[-1:] - b)).astype(BF16)
    return _state_update(mi.astype(BF16), kd)


class _Hgrn:
    def __init__(self, q_ref, k_ref, logf_ref, i_ref, g_ref, nw, state, o_ref):
        self.q_ref, self.k_ref, self.logf_ref, self.i_ref, self.g_ref = q_ref, k_ref, logf_ref, i_ref, g_ref
        self.nw, self.state, self.o_ref = nw, state, o_ref
        t = HG_CHUNK
        self.nchunk = q_ref.shape[0] // t
        r = lax.broadcasted_iota(jnp.int32, (t, t), 0)
        c = lax.broadcasted_iota(jnp.int32, (t, t), 1)
        self.causal = r >= c
        self.cums, self.terms, self.entering = {}, {}, {}

    def rows(self, ci):
        return slice(ci * HG_CHUNK, (ci + 1) * HG_CHUNK)

    def issue_cumsum(self, ci):
        self.cums[ci] = _cumsum_rows(self.logf_ref[self.rows(ci), :])

    def issue_local(self, ci):
        rows = self.rows(ci)
        b = self.cums.pop(ci)
        bl = b[-1:]
        k = self.k_ref[rows, :].astype(F32)
        qe = (self.q_ref[rows, :].astype(F32) * jnp.exp(b)).astype(BF16)
        ke = (k * jnp.exp(-b)).astype(BF16)
        kd = (k * jnp.exp(bl - b)).astype(BF16)
        iv = self.i_ref[rows, :].astype(BF16)
        a = [jnp.where(self.causal, _dot_nt(qe[:, hs], ke[:, hs]), 0.0).astype(BF16) for hs in _HG_SLICES]
        self.terms[ci] = (qe, iv, a)
        self.entering[ci] = self.state.astype(BF16)
        self.state = self.state * jnp.exp(bl) + _state_update(iv, kd)

    def finish(self, ci):
        rows = self.rows(ci)
        qe, iv, a = self.terms.pop(ci)
        ent = self.entering.pop(ci)
        outs = [_dot(a[h], iv[:, hs]) + _dot_nt(qe[:, hs], ent[:, hs]) for h, hs in enumerate(_HG_SLICES)]
        o = _hg_out(jnp.concatenate(outs, axis=1), self.g_ref[rows, :].astype(F32), self.nw)
        self.o_ref[rows, :] = o.astype(self.o_ref.dtype)


def _load_as_bf16(w_hbm, w_vmem, stage, sem):
    rows = stage.shape[1]
    nslab = w_hbm.shape[0] // rows

    def slab_copy(c):
        return pltpu.make_async_copy(w_hbm.at[pl.ds(c * rows, rows), :], stage.at[c % 2], sem.at[c % 2])

    slab_copy(0).start()
    for c in range(nslab):
        if c + 1 < nslab:
            slab_copy(c + 1).start()
        slab_copy(c).wait()
        w_vmem[pl.ds(c * rows, rows), :] = stage[c % 2].astype(w_vmem.dtype)


def _mixer_kernel(sink_ref, x0_ref, xnext_ref, xs_ref, xm_ref, g_ref, lbp_ref, w_hbm, nw_ref,
                  wa32, wb32, wo32, wup32, wdn32,
                  att_ref, hg_ref, sga_ref, sgb_ref, sfin_ref, lastkv_ref,
                  qs_ref, kvs_ref, hqs_ref, hks_ref, his_ref, hgs_ref, sgas_ref, sgbs_ref,
                  wa16, wb16, wo16, wup16, wdn16,
                  w_ref, stage, sem, xn_s, zq, zkv, zhq, zhk, zlogf, zhi, zhg, kvm_ref, st_ref, mst_ref,
                  *, tiles_per_seq):
    s = pl.program_id(0)
    t = xnext_ref.shape[0]

    @pl.when(s == 0)
    def _():
        _load_as_bf16(w_hbm, w_ref, stage, sem)
        xn_s[...] = _rmsnorm(x0_ref[...], g_ref[...]).astype(xn_s.dtype)
        for ref in (zq, zkv, zhq, zhk, zlogf, zhi, zhg, st_ref):
            ref[...] = jnp.zeros(ref.shape, ref.dtype)
        nb = xs_ref.shape[0]
        small = jnp.concatenate([xs_ref[...], xm_ref[...]], axis=0)
        p = _Projection(_rmsnorm(small, g_ref[...]).astype(BF16), lbp_ref[...], w_ref)
        qs_ref[...] = p.q_att().astype(qs_ref.dtype)
        kv = p.kv()
        kvs_ref[...] = kv
        kvm_ref[...] = kv[nb:]
        hqs_ref[...] = p.q_hg()
        k, logf = p.forget()
        hks_ref[...] = k
        iv = p.i_hg()
        his_ref[...] = iv
        hgs_ref[...] = p.swish_gate()
        sgas_ref[...] = p.branch_gate(C_GA).astype(sgas_ref.dtype)
        sgbs_ref[...] = p.branch_gate(C_GB).astype(sgbs_ref.dtype)
        mst_ref[...] = _meta_state(k[nb:], logf[nb:], iv[nb:])

    first = lax.rem(jnp.maximum(s - 1, 0), tiles_per_seq) == 0

    att = _Attention(sink_ref, zq, jnp.concatenate([zkv[...], kvm_ref[...]], axis=0), first, att_ref)
    last = zkv[t:t + WINDOW, :]
    lastkv_ref[...] = last
    zkv[0:WINDOW, :] = last
    entering = jnp.where(first, mst_ref[...], st_ref[...])
    hg = _Hgrn(zhq, zhk, zlogf, zhi, zhg, nw_ref[...], entering, hg_ref)
    proj = _Projection(xn_s, lbp_ref[...], w_ref)

    def gate_a(p):
        sga_ref[:, p] = proj.branch_gate(C_GA, p).astype(sga_ref.dtype)

    def gate_b(p):
        sgb_ref[:, p] = proj.branch_gate(C_GB, p).astype(sgb_ref.dtype)

    def new_kv(_):
        zkv[WINDOW:WINDOW + t, :] = proj.kv()

    def new_forget(p):
        k_new, logf_new = proj.forget(p)
        zhk[:, p] = k_new.astype(zhk.dtype)
        zlogf[:, p] = logf_new

    def new_q(p):
        zq[:, p] = proj.q_att(p).astype(zq.dtype)

    def new_hq(p):
        zhq[:, p] = proj.q_hg(p).astype(zhq.dtype)

    def new_hi(p):
        zhi[:, p] = proj.i_hg(p).astype(zhi.dtype)

    def new_gate(p):
        zhg[:, p] = proj.swish_gate(p).astype(zhg.dtype)

    def parts(n):
        return [slice(c, c + PROJ_GROUP) for c in range(0, n, PROJ_GROUP)]

    groups = ([(gate_a, p) for p in parts(D_MODEL)] + [(gate_b, p) for p in parts(D_MODEL)]
              + [(new_kv, None)] + [(new_q, p) for p in parts(D_ATT)]
              + [(new_forget, p) for p in parts(D_HG)] + [(new_hq, p) for p in parts(D_HG)]
              + [(new_hi, p) for p in parts(D_HG)] + [(new_gate, p) for p in parts(D_HG)])

    def save_state():
        st_ref[...] = hg.state
        for h, hs in enumerate(_HG_SLICES):
            sfin_ref[h] = hg.state[:, hs].T

    qk, cs, loc, fin, out = att.issue_scores, hg.issue_cumsum, hg.issue_local, att.finish, hg.finish
    mixer_work = {
        0: [(qk, 0), (cs, 0), (cs, 1), (cs, 2), (cs, 3)],
        1: [(cs, 4), (cs, 5), (loc, 0)],
        2: [(cs, 6), (cs, 7), (loc, 1)],
        3: [(qk, 1), (loc, 2)],
        4: [(fin, 0), (loc, 3)],
        5: [(out, 0), (loc, 4)],
        6: [(qk, 2), (out, 1), (loc, 5)],
        7: [(fin, 1), (out, 2), (loc, 6)],
        8: [(qk, 3), (out, 3), (loc, 7), (save_state,)],
        9: [(out, 4)],
        10: [(fin, 2), (out, 5)],
        11: [(out, 6)],
        12: [(out, 7)],
        13: [(fin, 3)],
    }
    for slot, (fn, p) in enumerate(groups):
        fn(p)
        for item in mixer_work.get(slot, []):
            item[0](*item[1:])
    xn_s[...] = _rmsnorm(xnext_ref[...], g_ref[...]).astype(xn_s.dtype)
    for src, dst in ((wa32, wa16), (wb32, wb16), (wo32, wo16), (wup32, wup16), (wdn32, wdn16)):
        dst[...] = src[...].astype(dst.dtype)


def _mixers(sinks, x, x_sample, x_meta, g, lb_param, w_f32, hg_norm, later_weights, bsz):
    n = x.shape[0]
    t = MIX_ROWS
    nt = n // t
    per_seq = nt // bsz
    rows = x_sample.shape[0] + x_meta.shape[0]

    def this_tile(c):
        return pl.BlockSpec((t, c), lambda s: (jnp.minimum(s, nt - 1), 0))

    def prev_tile(c):
        return pl.BlockSpec((t, c), lambda s: (jnp.maximum(s - 1, 0), 0))

    def prev_seq(shape):
        return pl.BlockSpec((None,) + shape,
                            lambda s: (jnp.maximum(s - 1, 0) // per_seq,) + (0,) * len(shape))

    def small(c):
        return pl.BlockSpec((rows, c), lambda s: (0, 0))

    first_tile = pl.BlockSpec((t, D_MODEL), lambda s: (0, 0), pipeline_mode=pl.Buffered(1))
    next_tile = pl.BlockSpec((t, D_MODEL), lambda s: (jnp.minimum(s + 1, nt - 1), 0))
    small_widths = (D_ATT, 2 * D_KV, D_HG, D_HG, D_HG, D_HG, D_MODEL, D_MODEL)
    small_dtypes = (BF16, F32, F32, F32, F32, F32, BF16, BF16)

    def row_block(wt):
        return pl.BlockSpec((wt.shape[0] // nt, wt.shape[1]), lambda s: (jnp.minimum(s, nt - 1), 0))

    return pl.pallas_call(
        functools.partial(_mixer_kernel, tiles_per_seq=per_seq),
        grid=(nt + 1,),
        in_specs=[pl.BlockSpec(memory_space=pltpu.SMEM), first_tile, next_tile,
                  _resident(x_sample.shape), _resident(x_meta.shape), _resident((1, D_MODEL)),
                  _resident(lb_param.shape), pl.BlockSpec(memory_space=pl.ANY), _resident((1, HG_DV))]
                 + [row_block(wt) for wt in later_weights],
        out_specs=[prev_tile(D_ATT), prev_tile(D_HG), this_tile(D_MODEL), this_tile(D_MODEL),
                   prev_seq((HG_HEADS, HG_DK, HG_DV)), prev_seq((WINDOW, 2 * D_KV))]
                  + [small(c) for c in small_widths] + [row_block(wt) for wt in later_weights],
        out_shape=[jax.ShapeDtypeStruct((n, D_ATT), BF16), jax.ShapeDtypeStruct((n, D_HG), BF16),
                   jax.ShapeDtypeStruct((n, D_MODEL), BF16), jax.ShapeDtypeStruct((n, D_MODEL), BF16),
                   jax.ShapeDtypeStruct((bsz, HG_HEADS, HG_DK, HG_DV), F32),
                   jax.ShapeDtypeStruct((bsz, WINDOW, 2 * D_KV), F32)]
                  + [jax.ShapeDtypeStruct((rows, c), d) for c, d in zip(small_widths, small_dtypes)]
                  + [jax.ShapeDtypeStruct(wt.shape, BF16) for wt in later_weights],
        scratch_shapes=[pltpu.VMEM((D_MODEL, D_IN), BF16),
                        pltpu.VMEM((2, WEIGHT_SLAB_ROWS, D_IN), F32),
                        pltpu.SemaphoreType.DMA((2,)),
                        pltpu.VMEM((t, D_MODEL), BF16),
                        pltpu.VMEM((t, D_ATT), BF16), pltpu.VMEM((WINDOW + t, 2 * D_KV), F32),
                        pltpu.VMEM((t, D_HG), BF16), pltpu.VMEM((t, D_HG), BF16),
                        pltpu.VMEM((t, D_HG), F32), pltpu.VMEM((t, D_HG), BF16),
                        pltpu.VMEM((t, D_HG), BF16),
                        pltpu.VMEM((N_META, 2 * D_KV), F32),
                        pltpu.VMEM((HG_DV, D_HG), F32),
                        pltpu.VMEM((HG_DV, D_HG), F32)],
        compiler_params=_params("arbitrary"),
        name="mixers",
    )(sinks, x, x, x_sample, x_meta, g, lb_param, w_f32, hg_norm, *later_weights)


def _merge_ffn_rows(x, att, hg, sga, sgb, wa_ref, wb_ref, wo_ref, ln_ffn, wup_ref, wdn_ref, ln_f):
    ya = _dot(att.astype(BF16), wa_ref[...])
    yb = _dot(hg.astype(BF16), wb_ref[...])
    mix = sga.astype(F32) * ya + sgb.astype(F32) * yb
    h1 = x + _dot(mix.astype(BF16), wo_ref[...])
    xn = _rmsnorm(h1, ln_ffn).astype(BF16)
    acc = jnp.zeros_like(h1)
    for c in range(0, D_FF, FFN_CHUNK):
        u = jnp.maximum(_dot(xn, wup_ref[:, c:c + FFN_CHUNK]), 0.0)
        acc = acc + _dot((u * u).astype(BF16), wdn_ref[c:c + FFN_CHUNK, :])
    return _rmsnorm(h1 + acc, ln_f)


def _sample_hgrn_group(rows, nw, q_ref, k_ref, i_ref, g_ref, s_ref, snew_ref, hg_ref):
    k = k_ref[rows, :]
    q = q_ref[rows, :]
    iv = i_ref[rows, :]
    r = lax.broadcasted_iota(jnp.int32, (HG_DK, HG_DK), 0)
    c = lax.broadcasted_iota(jnp.int32, (HG_DK, HG_DK), 1)
    eye = r == c

    def column(row):
        return jnp.sum(jnp.where(eye, row, 0.0), axis=1, keepdims=True)

    outs = []
    for b in range(s_ref.shape[0]):
        heads = []
        for h, hs in enumerate(_HG_SLICES):
            kc = column(k[b:b + 1, hs])
            qc = column(q[b:b + 1, hs])
            s_old = s_ref[b, h]
            s = s_old + kc * (iv[b:b + 1, hs] - s_old)
            snew_ref[b, h] = s
            heads.append(jnp.sum(qc * s, axis=0, keepdims=True))
        outs.append(jnp.concatenate(heads, axis=1))
    hg_ref[rows, :] = _hg_out(jnp.concatenate(outs, axis=0), g_ref[rows, :], nw)


def _merge_ffn_kernel(x_ref, att_ref, hg_ref, sga_ref, sgb_ref,
                      xs_ref, atts_ref, sgas_ref, sgbs_ref,
                      nw_ref, hqs_ref, hks_ref, his_ref, hgates_ref, state_ref,
                      wa_ref, wb_ref, wo_ref, wup_ref, wdn_ref, lnffn_ref, lnf_ref,
                      y_ref, ys_ref, snew_ref, hgs, *, prompt_steps):
    i = pl.program_id(0)
    weights = (wa_ref, wb_ref, wo_ref, lnffn_ref[...], wup_ref, wdn_ref, lnf_ref[...])
    group = state_ref.shape[0]
    ngroups = xs_ref.shape[0] // group

    @pl.when(i < prompt_steps)
    def _():
        y_ref[...] = _merge_ffn_rows(x_ref[...], att_ref[...], hg_ref[...], sga_ref[...], sgb_ref[...],
                                     *weights)
        rows = pl.ds(pl.multiple_of(jnp.minimum(i, ngroups - 1) * group, group), group)
        _sample_hgrn_group(rows, nw_ref[...], hqs_ref, hks_ref, his_ref, hgates_ref,
                           state_ref, snew_ref, hgs)

    @pl.when(i == prompt_steps)
    def _():
        ys_ref[...] = _merge_ffn_rows(xs_ref[...], atts_ref[...], hgs[...], sgas_ref[...],
                                      sgbs_ref[...], *weights)


def _merge_ffn(x, att, hg, sga, sgb, xs, att_s, sga_s, sgb_s, hg_norm, hq_s, hk_s, hi_s, hgate_s, state,
               wa, wb, wo, ln_ffn, w_up, w_down, ln_f):
    n = x.shape[0]
    nb = xs.shape[0]
    rows = MERGE_ROWS
    nt = n // rows
    g = SAMPLE_HG_GROUP
    assert nb // g <= nt

    def blk(c):
        return pl.BlockSpec((rows, c), lambda i: (jnp.minimum(i, nt - 1), 0))

    def sample(c):
        return pl.BlockSpec((nb, c), lambda i: (0, 0))

    sblk = pl.BlockSpec((g, HG_HEADS, HG_DK, HG_DV), lambda i: (jnp.minimum(i, nb // g - 1), 0, 0, 0))
    return pl.pallas_call(
        functools.partial(_merge_ffn_kernel, prompt_steps=nt),
        grid=(nt + 1,),
        in_specs=[blk(D_MODEL), blk(D_ATT), blk(D_HG), blk(D_MODEL), blk(D_MODEL),
                  sample(D_MODEL), sample(D_ATT), sample(D_MODEL), sample(D_MODEL),
                  _resident((1, HG_DV)), _resident(hq_s.shape), _resident(hk_s.shape),
                  _resident(hi_s.shape), _resident(hgate_s.shape), sblk,
                  _resident(wa.shape), _resident(wb.shape), _resident(wo.shape),
                  _resident(w_up.shape), _resident(w_down.shape),
                  _resident((1, D_MODEL)), _resident((1, D_MODEL))],
        out_specs=[blk(D_MODEL), sample(D_MODEL), sblk],
        out_shape=[jax.ShapeDtypeStruct((n, D_MODEL), F32), jax.ShapeDtypeStruct((nb, D_MODEL), F32),
                   jax.ShapeDtypeStruct(state.shape, F32)],
        scratch_shapes=[pltpu.VMEM((nb, D_HG), F32)],
        compiler_params=_params("arbitrary"),
        name="merge_ffn",
    )(x, att, hg, sga, sgb, xs, att_s, sga_s, sgb_s, hg_norm, hq_s, hk_s, hi_s, hgate_s, state,
      wa, wb, wo, w_up, w_down, ln_ffn, ln_f)


def _sample_attn_kernel(sink_ref, qm_ref, kvn_ref, ckt_ref, cvt_ref, mk_ref, mv_ref,
                        o_ref, nkt_ref, nvt_ref):
    nb = qm_ref.shape[0]
    head = lax.broadcasted_iota(jnp.int32, (Q_HEADS, 1), 0)
    sk = jnp.zeros((Q_HEADS, 1), F32)
    for j in range(Q_HEADS):
        sk = jnp.where(head == j, sink_ref[j], sk)
    newest = lax.broadcasted_iota(jnp.int32, (D_KV, WINDOW), 1) == WINDOW - 1
    kvn = kvn_ref[...]
    hi = kvn.astype(BF16)
    rest = kvn - hi.astype(F32)
    mid = rest.astype(BF16)
    lo = (rest - mid.astype(F32)).astype(BF16)
    r = lax.broadcasted_iota(jnp.int32, (2 * D_KV, 2 * D_KV), 0)
    c = lax.broadcasted_iota(jnp.int32, (2 * D_KV, 2 * D_KV), 1)
    eye = jnp.where(r == c, 1.0, 0.0).astype(BF16)
    pieces_t = _dot_nt(eye, jnp.concatenate([hi, mid, lo], axis=0)).astype(BF16)
    pr = lax.broadcasted_iota(jnp.int32, (3 * nb, WINDOW), 0)
    pc = lax.broadcasted_iota(jnp.int32, (3 * nb, WINDOW), 1)
    for b in range(nb):
        pick = jnp.logical_and(lax.rem(pr, nb) == b, pc == WINDOW - 1)
        new_cols = _dot(pieces_t, jnp.where(pick, 1.0, 0.0).astype(BF16))
        nkt_ref[b] = jnp.where(newest, new_cols[:D_KV], pltpu.roll(ckt_ref[b], WINDOW - 1, axis=1))
        nvt_ref[b] = jnp.where(newest, new_cols[D_KV:], pltpu.roll(cvt_ref[b], WINDOW - 1, axis=1))
    scores = [(_dot(qm_ref[b], nkt_ref[b].astype(BF16)),
               _dot_nt(qm_ref[b], mk_ref[b].astype(BF16)))
              for b in range(nb)]
    for b, (s_w, s_m) in enumerate(scores):
        m = jnp.maximum(jnp.maximum(jnp.max(s_w, axis=1, keepdims=True),
                                    jnp.max(s_m, axis=1, keepdims=True)), sk)
        e_w = jnp.exp(s_w - m)
        e_m = jnp.exp(s_m - m)
        l = (jnp.sum(e_w, axis=1, keepdims=True) + jnp.sum(e_m, axis=1, keepdims=True)
             + jnp.exp(sk - m))
        o = (_dot_nt(e_w.astype(BF16), nvt_ref[b].astype(BF16))
             + _dot(e_m.astype(BF16), mv_ref[b].astype(BF16)))
        o_ref[b] = o / l


def _sample_attention(sinks, qm, kv_new, ck, cv, mk, mv):
    nb = ck.shape[0]
    g = SAMPLE_ATT_GROUP

    def blk3(a, c):
        return pl.BlockSpec((g, a, c), lambda i: (i, 0, 0))

    return pl.pallas_call(
        _sample_attn_kernel,
        grid=(nb // g,),
        in_specs=[pl.BlockSpec(memory_space=pltpu.SMEM), blk3(Q_HEADS, D_KV),
                  pl.BlockSpec((g, 2 * D_KV), lambda i: (i, 0)),
                  blk3(D_KV, WINDOW), blk3(D_KV, WINDOW), blk3(N_META, D_KV), blk3(N_META, D_KV)],
        out_specs=[blk3(Q_HEADS, D_KV), blk3(D_KV, WINDOW), blk3(D_KV, WINDOW)],
        out_shape=[jax.ShapeDtypeStruct((nb, Q_HEADS, D_KV), F32),
                   jax.ShapeDtypeStruct((nb, D_KV, WINDOW), F32),
                   jax.ShapeDtypeStruct((nb, D_KV, WINDOW), F32)],
        compiler_params=_params("parallel"),
        name="sample_attn",
    )(sinks, qm, kv_new, ck, cv, mk, mv)


def kernel(x_prompt, x_sample, cache_k, cache_v, cache_meta_k, cache_meta_v, state_hgrn, meta,
           w_in, sinks, lb_param, hg_norm, w_att_out, w_hg_out, w_o, ln_mix, ln_ffn, w_up,
           w_down, ln_f):
    bsz, seq, _ = x_prompt.shape
    nb = x_sample.shape[0]
    ln_mix2 = ln_mix.reshape(1, D_MODEL)
    ln_ffn2 = ln_ffn.reshape(1, D_MODEL)
    ln_f2 = ln_f.reshape(1, D_MODEL)
    nw = hg_norm.reshape(1, HG_DV)

    xs = x_sample.reshape(nb, D_MODEL)

    xp = x_prompt.reshape(bsz * seq, D_MODEL)
    (att_p, hg_p, sga_p, sgb_p, state_p, lastkv_p,
     q_s, kv_s, hq_s, hk_s, hi_s, hgate_s, sga_s, sgb_s,
     wa, wb, wo, wup, wdn) = _mixers(
        sinks.reshape(Q_HEADS), xp, xs, meta, ln_mix2, lb_param, w_in[0], nw,
        (w_att_out[0], w_hg_out[0], w_o[0], w_up[0], w_down[0]), bsz)

    grp = Q_HEADS // KV_HEADS
    rows = nb + N_META
    qs4 = q_s.reshape(rows, KV_HEADS, grp, 1, HEAD_DIM)
    sel = jnp.eye(KV_HEADS, dtype=BF16).reshape(1, KV_HEADS, 1, KV_HEADS, 1)
    qm = (qs4 * sel).reshape(rows, Q_HEADS, D_KV)

    def window_t(c):
        return jnp.swapaxes(c[0].reshape(nb, WINDOW, D_KV), 1, 2)

    o_s, nkt_s, nvt_s = _sample_attention(
        sinks.reshape(Q_HEADS), qm, kv_s, window_t(cache_k), window_t(cache_v),
        cache_meta_k[0].reshape(nb, N_META, D_KV), cache_meta_v[0].reshape(nb, N_META, D_KV))
    o5 = o_s.reshape(nb, KV_HEADS, grp, KV_HEADS, HEAD_DIM)
    att_s = jnp.stack([o5[:, h, :, h, :] for h in range(KV_HEADS)], axis=1).reshape(nb, D_ATT)

    y_p, y_s, state_s = _merge_ffn(xp, att_p, hg_p, sga_p, sgb_p, xs, att_s, sga_s, sgb_s,
                                   nw, hq_s, hk_s, hi_s, hgate_s, state_hgrn[0],
                                   wa, wb, wo, ln_ffn2, wup, wdn, ln_f2)

    kv5 = lastkv_p.reshape(bsz, WINDOW, 2, KV_HEADS, HEAD_DIM)
    meta5 = jnp.broadcast_to(kv_s[nb:].reshape(1, N_META, 2, KV_HEADS, HEAD_DIM),
                             (bsz, N_META, 2, KV_HEADS, HEAD_DIM))
    return (y_p.reshape(bsz, seq, D_MODEL),
            y_s.reshape(nb, 1, D_MODEL),
            kv5[None, :, :, 0],
            kv5[None, :, :, 1],
            meta5[None, :, :, 0],
            meta5[None, :, :, 1],
            state_p[None],
            jnp.swapaxes(nkt_s, 1, 2).reshape(1, nb, WINDOW, KV_HEADS, HEAD_DIM),
            jnp.swapaxes(nvt_s, 1, 2).reshape(1, nb, WINDOW, KV_HEADS, HEAD_DIM),
            state_s[None])
```

```python
import functools

import jax
import jax.numpy as jnp
from jax import lax
from jax.experimental import pallas as pl
from jax.experimental.pallas import tpu as pltpu

F32 = jnp.float32
BF16 = jnp.bfloat16

D_MODEL = 1024
N_META = 16
WINDOW = 128
HEAD_DIM = 64
Q_HEADS = 8
KV_HEADS = 2
D_ATT = Q_HEADS * HEAD_DIM
D_KV = KV_HEADS * HEAD_DIM
HG_HEADS = 4
HG_DK = 128
HG_DV = 128
D_HG = HG_HEADS * HG_DK
HG_CHUNK = 64
D_FF = 4 * D_MODEL
EPS = 1e-6
C_Q = 0
C_KV = C_Q + D_ATT
C_HQ = C_KV + 2 * D_KV
C_HF = C_HQ + D_HG
C_HI = C_HF + D_HG
C_HGATE = C_HI + D_HG
C_GA = C_HGATE + D_HG
C_GB = C_GA + D_MODEL
D_IN = C_GB + D_MODEL

VMEM_LIMIT_BYTES = 56 * 1024 * 1024
MIX_ROWS = 512
PROJ_GROUP = 256
WEIGHT_SLAB_ROWS = 64
MERGE_ROWS = 512
FFN_CHUNK = 2048
LANES = 128
BF16_ROWS = 16
SAMPLE_ATT_GROUP = 16
SAMPLE_HG_GROUP = 8

_NT = (((1,), (1,)), ((), ()))
_TN = (((0,), (0,)), ((), ()))


def _dot(a, b):
    return jnp.dot(a, b, preferred_element_type=F32)


def _dot_nt(a, b):
    return lax.dot_general(a, b, _NT, preferred_element_type=F32)


def _dot_tn(a, b):
    return lax.dot_general(a, b, _TN, preferred_element_type=F32)


def _rmsnorm(x, g):
    return x * lax.rsqrt(jnp.mean(x * x, axis=-1, keepdims=True) + EPS) * g


def _resident(shape):
    return pl.BlockSpec(shape, lambda *_: (0,) * len(shape), pipeline_mode=pl.Buffered(1))


def _params(*sem):
    return pltpu.CompilerParams(dimension_semantics=sem, vmem_limit_bytes=VMEM_LIMIT_BYTES)


def _lower_bound(lbp):
    e = jnp.exp(lbp - jnp.max(lbp, axis=0, keepdims=True))
    return e[0:1] / jnp.sum(e, axis=0, keepdims=True)


class _Projection:
    def __init__(self, xn, lbp, w_ref):
        self.xn = xn
        self.w_ref = w_ref
        self.lb = _lower_bound(lbp)

    def cols(self, base, part):
        return _dot(self.xn[...], self.w_ref[:, base + part.start:base + part.stop])

    def q_att(self, part=slice(0, D_ATT)):
        return self.cols(C_Q, part) * (HEAD_DIM ** -0.5)

    def kv(self):
        return self.cols(C_KV, slice(0, 2 * D_KV))

    def q_hg(self, part=slice(0, D_HG)):
        return self.cols(C_HQ, part) * (HG_DK ** -0.5)

    def forget(self, part=slice(0, D_HG)):
        lb = self.lb[:, part]
        f = lb + (1.0 - lb) * jax.nn.sigmoid(self.cols(C_HF, part))
        return 1.0 - f, jnp.log(f)

    def i_hg(self, part=slice(0, D_HG)):
        return self.cols(C_HI, part)

    def swish_gate(self, part=slice(0, D_HG)):
        g = self.cols(C_HGATE, part)
        return g * jax.nn.sigmoid(g)

    def branch_gate(self, base, part=slice(0, D_MODEL)):
        return jax.nn.sigmoid(self.cols(base, part))


class _Attention:
    def __init__(self, sink_ref, q_ref, kv, first, o_ref):
        self.sink_ref, self.q_ref, self.first, self.o_ref = sink_ref, q_ref, first, o_ref
        self.nsub = q_ref.shape[0] // WINDOW
        self.nk = 2 * WINDOW + N_META
        self.meta0 = WINDOW + q_ref.shape[0]
        lane = lax.broadcasted_iota(jnp.int32, (kv.shape[0], D_KV), 1)
        low = lane < HEAD_DIM
        k = kv[:, :D_KV]
        ksw = pltpu.roll(k, HEAD_DIM, axis=1)
        self.kboth = (jnp.where(low, k, ksw).astype(BF16), jnp.where(low, ksw, k).astype(BF16))
        qlane = lax.broadcasted_iota(jnp.int32, (1, LANES), 1)
        self.keep = (jnp.where(qlane < HEAD_DIM, 1.0, 0.0).astype(BF16),
                     jnp.where(qlane < HEAD_DIM, 0.0, 1.0).astype(BF16))
        r = lax.broadcasted_iota(jnp.int32, (D_KV, D_KV), 0)
        c = lax.broadcasted_iota(jnp.int32, (D_KV, D_KV), 1)
        eye = jnp.where(r == c, 1.0, 0.0).astype(BF16)
        self.vt = _dot_nt(eye, kv[:, D_KV:].astype(BF16)).astype(BF16)
        self.ones = jnp.ones((BF16_ROWS, self.nk), BF16)
        self.key = lax.broadcasted_iota(jnp.int32, (self.nk, 2 * WINDOW), 0)
        col = lax.broadcasted_iota(jnp.int32, (self.nk, 2 * WINDOW), 1)
        self.qry = jnp.bitwise_and(col, WINDOW - 1)
        self.first_head = lax.broadcasted_iota(jnp.int32, (1, 2 * WINDOW), 1) < WINDOW
        self.scores = {}

    def issue_scores(self, i):
        r0 = i * WINDOW
        out = []
        for p in range(Q_HEADS // 2):
            kb = self.kboth[(2 * p) // (Q_HEADS // KV_HEADS)]
            kmat = jnp.concatenate([kb[r0:r0 + 2 * WINDOW], kb[self.meta0:]], axis=0)
            qp = self.q_ref[r0:r0 + WINDOW, p * LANES:(p + 1) * LANES]
            q2 = jnp.concatenate([qp * self.keep[0], qp * self.keep[1]], axis=0)
            out.append(_dot_nt(kmat, q2))
        self.scores[i] = out

    def finish(self, i):
        r0 = i * WINDOW
        lo = jnp.where(self.first, WINDOW - 1, self.qry) if i == 0 else self.qry
        mask = jnp.logical_and(self.key > lo, self.key <= self.qry + WINDOW)
        mask = jnp.logical_or(mask, self.key >= 2 * WINDOW)
        vaug = []
        for h in range(KV_HEADS):
            vth = self.vt[h * HEAD_DIM:(h + 1) * HEAD_DIM]
            vaug.append(jnp.concatenate(
                [jnp.concatenate([vth[:, r0:r0 + 2 * WINDOW], vth[:, self.meta0:]], axis=1),
                 self.ones], axis=0))
        outs = []
        for p, raw in enumerate(self.scores.pop(i)):
            h = (2 * p) // (Q_HEADS // KV_HEADS)
            s = jnp.where(mask, raw, -jnp.inf)
            sk = jnp.where(self.first_head, self.sink_ref[2 * p], self.sink_ref[2 * p + 1])
            m = jnp.maximum(jnp.max(s, axis=0, keepdims=True), sk)
            e = jnp.exp(s - m).astype(BF16)
            oa = _dot(vaug[h], e)
            o = oa[:HEAD_DIM] / (oa[HEAD_DIM:HEAD_DIM + 1] + jnp.exp(sk - m))
            outs += [o[:, :WINDOW], o[:, WINDOW:]]
        self.o_ref[r0:r0 + WINDOW, :] = jnp.concatenate(outs, axis=0).T.astype(self.o_ref.dtype)


def _cumsum_rows(x):
    t = x.shape[0]
    row = lax.broadcasted_iota(jnp.int32, x.shape, 0)
    d = 1
    while d < t:
        x = x + jnp.where(row >= d, pltpu.roll(x, d, axis=0), 0.0)
        d *= 2
    return x


def _hg_out(o, gate, nw):
    parts = []
    for h in range(HG_HEADS):
        oh = o[:, h * HG_DV:(h + 1) * HG_DV]
        parts.append(oh * lax.rsqrt(jnp.mean(oh * oh, axis=-1, keepdims=True) + EPS) * nw)
    return jnp.concatenate(parts, axis=1) * gate


_HG_SLICES = [slice(h * HG_DK, (h + 1) * HG_DK) for h in range(HG_HEADS)]


def _state_update(iv, kd):
    return jnp.concatenate([_dot_tn(iv[:, hs], kd[:, hs]) for hs in _HG_SLICES], axis=1)


def _meta_state(mk, mlogf, mi):
    b = _cumsum_rows(mlogf)
    kd = (mk * jnp.exp(b[-1:] - b)).astype(BF16)
    return _state_update(mi.astype(BF16), kd)


class _Hgrn:
    def __init__(self, q_ref, k_ref, logf_ref, i_ref, g_ref, nw, state, o_ref):
        self.q_ref, self.k_ref, self.logf_ref, self.i_ref, self.g_ref = q_ref, k_ref, logf_ref, i_ref, g_ref
        self.nw, self.state, self.o_ref = nw, state, o_ref
        t = HG_CHUNK
        self.nchunk = q_ref.shape[0] // t
        r = lax.broadcasted_iota(jnp.int32, (t, t), 0)
        c = lax.broadcasted_iota(jnp.int32, (t, t), 1)
        self.causal = r >= c
        self.cums, self.terms, self.entering = {}, {}, {}

    def rows(self, ci):
        return slice(ci * HG_CHUNK, (ci + 1) * HG_CHUNK)

    def issue_cumsum(self, ci):
        self.cums[ci] = _cumsum_rows(self.logf_ref[self.rows(ci), :])

    def issue_local(self, ci):
        rows = self.rows(ci)
        b = self.cums.pop(ci)
        bl = b[-1:]
        k = self.k_ref[rows, :].astype(F32)
        qe = (self.q_ref[rows, :].astype(F32) * jnp.exp(b)).astype(BF16)
        ke = (k * jnp.exp(-b)).astype(BF16)
        kd = (k * jnp.exp(bl - b)).astype(BF16)
        iv = self.i_ref[rows, :].astype(BF16)
        a = [jnp.where(self.causal, _dot_nt(qe[:, hs], ke[:, hs]), 0.0).astype(BF16) for hs in _HG_SLICES]
        self.terms[ci] = (qe, iv, a)
        self.entering[ci] = self.state.astype(BF16)
        self.state = self.state * jnp.exp(bl) + _state_update(iv, kd)

    def finish(self, ci):
        rows = self.rows(ci)
        qe, iv, a = self.terms.pop(ci)
        ent = self.entering.pop(ci)
        outs = [_dot(a[h], iv[:, hs]) + _dot_nt(qe[:, hs], ent[:, hs]) for h, hs in enumerate(_HG_SLICES)]
        o = _hg_out(jnp.concatenate(outs, axis=1), self.g_ref[rows, :].astype(F32), self.nw)
        self.o_ref[rows, :] = o.astype(self.o_ref.dtype)


def _load_as_bf16(w_hbm, w_vmem, stage, sem):
    rows = stage.shape[1]
    nslab = w_hbm.shape[0] // rows

    def slab_copy(c):
        return pltpu.make_async_copy(w_hbm.at[pl.ds(c * rows, rows), :], stage.at[c % 2], sem.at[c % 2])

    slab_copy(0).start()
    for c in range(nslab):
        if c + 1 < nslab:
            slab_copy(c + 1).start()
        slab_copy(c).wait()
        w_vmem[pl.ds(c * rows, rows), :] = stage[c % 2].astype(w_vmem.dtype)


def _mixer_kernel(sink_ref, x0_ref, xnext_ref, xs_ref, xm_ref, g_ref, lbp_ref, w_hbm, nw_ref,
                  wa32, wb32, wo32, wup32, wdn32,
                  att_ref, hg_ref, sga_ref, sgb_ref, sfin_ref, lastkv_ref,
                  qs_ref, kvs_ref, hqs_ref, hks_ref, his_ref, hgs_ref, sgas_ref, sgbs_ref,
                  wa16, wb16, wo16, wup16, wdn16,
                  w_ref, stage, sem, xn_s, zq, zkv, zhq, zhk, zlogf, zhi, zhg, kvm_ref, st_ref, mst_ref,
                  *, tiles_per_seq):
    s = pl.program_id(0)
    t = xnext_ref.shape[0]

    @pl.when(s == 0)
    def _():
        _load_as_bf16(w_hbm, w_ref, stage, sem)
        xn_s[...] = _rmsnorm(x0_ref[...], g_ref[...]).astype(xn_s.dtype)
        for ref in (zq, zkv, zhq, zhk, zlogf, zhi, zhg, st_ref):
            ref[...] = jnp.zeros(ref.shape, ref.dtype)
        nb = xs_ref.shape[0]
        small = jnp.concatenate([xs_ref[...], xm_ref[...]], axis=0)
        p = _Projection(_rmsnorm(small, g_ref[...]).astype(BF16), lbp_ref[...], w_ref)
        q_small = p.q_att()
        low = lax.broadcasted_iota(jnp.int32, (1, LANES), 1) < HEAD_DIM
        for j in range(Q_HEADS):
            kv_head, half = j // (Q_HEADS // KV_HEADS), j % 2
            pair = q_small[:, (j // 2) * LANES:(j // 2 + 1) * LANES]
            moved = pair if half == kv_head else pltpu.roll(pair, HEAD_DIM, axis=1)
            qs_ref[:, j, :] = jnp.where(low if kv_head == 0 else jnp.logical_not(low), moved, 0.0)
        kv = p.kv()
        kvs_ref[...] = kv
        kvm_ref[...] = kv[nb:]
        hqs_ref[...] = p.q_hg()
        k, logf = p.forget()
        hks_ref[...] = k
        iv = p.i_hg()
        his_ref[...] = iv
        hgs_ref[...] = p.swish_gate()
        sgas_ref[...] = p.branch_gate(C_GA).astype(sgas_ref.dtype)
        sgbs_ref[...] = p.branch_gate(C_GB).astype(sgbs_ref.dtype)
        mst_ref[...] = _meta_state(k[nb:], logf[nb:], iv[nb:])

    first = lax.rem(jnp.maximum(s - 1, 0), tiles_per_seq) == 0

    att = _Attention(sink_ref, zq, jnp.concatenate([zkv[...], kvm_ref[...]], axis=0), first, att_ref)
    last = zkv[t:t + WINDOW, :]
    lastkv_ref[...] = last
    zkv[0:WINDOW, :] = last
    entering = jnp.where(first, mst_ref[...], st_ref[...])
    hg = _Hgrn(zhq, zhk, zlogf, zhi, zhg, nw_ref[...], entering, hg_ref)
    proj = _Projection(xn_s, lbp_ref[...], w_ref)

    def gate_a(p):
        sga_ref[:, p] = proj.branch_gate(C_GA, p).astype(sga_ref.dtype)

    def gate_b(p):
        sgb_ref[:, p] = proj.branch_gate(C_GB, p).astype(sgb_ref.dtype)

    def new_kv(_):
        zkv[WINDOW:WINDOW + t, :] = proj.kv()

    def new_forget(p):
        k_new, logf_new = proj.forget(p)
        zhk[:, p] = k_new.astype(zhk.dtype)
        zlogf[:, p] = logf_new

    def new_q(p):
        zq[:, p] = proj.q_att(p).astype(zq.dtype)

    def new_hq(p):
        zhq[:, p] = proj.q_hg(p).astype(zhq.dtype)

    def new_hi(p):
        zhi[:, p] = proj.i_hg(p).astype(zhi.dtype)

    def new_gate(p):
        zhg[:, p] = proj.swish_gate(p).astype(zhg.dtype)

    def parts(n):
        return [slice(c, c + PROJ_GROUP) for c in range(0, n, PROJ_GROUP)]

    groups = ([(gate_a, p) for p in parts(D_MODEL)] + [(gate_b, p) for p in parts(D_MODEL)]
              + [(new_kv, None)] + [(new_q, p) for p in parts(D_ATT)]
              + [(new_forget, p) for p in parts(D_HG)] + [(new_hq, p) for p in parts(D_HG)]
              + [(new_hi, p) for p in parts(D_HG)] + [(new_gate, p) for p in parts(D_HG)])

    def save_state():
        st_ref[...] = hg.state
        for h, hs in enumerate(_HG_SLICES):
            sfin_ref[h] = hg.state[:, hs].T

    qk, cs, loc, fin, out = att.issue_scores, hg.issue_cumsum, hg.issue_local, att.finish, hg.finish
    mixer_work = {
        0: [(qk, 0), (cs, 0), (cs, 1), (cs, 2), (cs, 3)],
        1: [(cs, 4), (cs, 5), (loc, 0)],
        2: [(cs, 6), (cs, 7), (loc, 1)],
        3: [(qk, 1), (loc, 2)],
        4: [(fin, 0), (loc, 3)],
        5: [(out, 0), (loc, 4)],
        6: [(qk, 2), (out, 1), (loc, 5)],
        7: [(fin, 1), (out, 2), (loc, 6)],
        8: [(qk, 3), (out, 3), (loc, 7), (save_state,)],
        9: [(out, 4)],
        10: [(fin, 2), (out, 5)],
        11: [(out, 6)],
        12: [(out, 7)],
        13: [(fin, 3)],
    }
    for slot, (fn, p) in enumerate(groups):
        fn(p)
        for item in mixer_work.get(slot, []):
            item[0](*item[1:])
    xn_s[...] = _rmsnorm(xnext_ref[...], g_ref[...]).astype(xn_s.dtype)
    for src, dst in ((wa32, wa16), (wb32, wb16), (wo32, wo16), (wup32, wup16), (wdn32, wdn16)):
        dst[...] = src[...].astype(dst.dtype)


def _mixers(sinks, x, x_sample, x_meta, g, lb_param, w_f32, hg_norm, later_weights, bsz):
    n = x.shape[0]
    t = MIX_ROWS
    nt = n // t
    per_seq = nt // bsz
    rows = x_sample.shape[0] + x_meta.shape[0]

    def this_tile(c):
        return pl.BlockSpec((t, c), lambda s: (jnp.minimum(s, nt - 1), 0))

    def prev_tile(c):
        return pl.BlockSpec((t, c), lambda s: (jnp.maximum(s - 1, 0), 0))

    def prev_seq(shape):
        return pl.BlockSpec((None,) + shape,
                            lambda s: (jnp.maximum(s - 1, 0) // per_seq,) + (0,) * len(shape))

    def small(c):
        return pl.BlockSpec((rows, c), lambda s: (0, 0))

    first_tile = pl.BlockSpec((t, D_MODEL), lambda s: (0, 0), pipeline_mode=pl.Buffered(1))
    next_tile = pl.BlockSpec((t, D_MODEL), lambda s: (jnp.minimum(s + 1, nt - 1), 0))
    small_widths = (2 * D_KV, D_HG, D_HG, D_HG, D_HG, D_MODEL, D_MODEL)
    small_dtypes = (F32, F32, F32, F32, F32, BF16, BF16)
    qm_shape = (rows, Q_HEADS, LANES)
    qm_spec = pl.BlockSpec(qm_shape, lambda s: (0, 0, 0))

    def row_block(wt):
        return pl.BlockSpec((wt.shape[0] // nt, wt.shape[1]), lambda s: (jnp.minimum(s, nt - 1), 0))

    return pl.pallas_call(
        functools.partial(_mixer_kernel, tiles_per_seq=per_seq),
        grid=(nt + 1,),
        in_specs=[pl.BlockSpec(memory_space=pltpu.SMEM), first_tile, next_tile,
                  _resident(x_sample.shape), _resident(x_meta.shape), _resident((1, D_MODEL)),
                  _resident(lb_param.shape), pl.BlockSpec(memory_space=pl.ANY), _resident((1, HG_DV))]
                 + [row_block(wt) for wt in later_weights],
        out_specs=[prev_tile(D_ATT), prev_tile(D_HG), this_tile(D_MODEL), this_tile(D_MODEL),
                   prev_seq((HG_HEADS, HG_DK, HG_DV)), prev_seq((WINDOW, 2 * D_KV))]
                  + [qm_spec] + [small(c) for c in small_widths] + [row_block(wt) for wt in later_weights],
        out_shape=[jax.ShapeDtypeStruct((n, D_ATT), BF16), jax.ShapeDtypeStruct((n, D_HG), BF16),
                   jax.ShapeDtypeStruct((n, D_MODEL), BF16), jax.ShapeDtypeStruct((n, D_MODEL), BF16),
                   jax.ShapeDtypeStruct((bsz, HG_HEADS, HG_DK, HG_DV), F32),
                   jax.ShapeDtypeStruct((bsz, WINDOW, 2 * D_KV), F32)]
                  + [jax.ShapeDtypeStruct(qm_shape, F32)]
                  + [jax.ShapeDtypeStruct((rows, c), d) for c, d in zip(small_widths, small_dtypes)]
                  + [jax.ShapeDtypeStruct(wt.shape, BF16) for wt in later_weights],
        scratch_shapes=[pltpu.VMEM((D_MODEL, D_IN), BF16),
                        pltpu.VMEM((2, WEIGHT_SLAB_ROWS, D_IN), F32),
                        pltpu.SemaphoreType.DMA((2,)),
                        pltpu.VMEM((t, D_MODEL), BF16),
                        pltpu.VMEM((t, D_ATT), BF16), pltpu.VMEM((WINDOW + t, 2 * D_KV), F32),
                        pltpu.VMEM((t, D_HG), BF16), pltpu.VMEM((t, D_HG), BF16),
                        pltpu.VMEM((t, D_HG), F32), pltpu.VMEM((t, D_HG), BF16),
                        pltpu.VMEM((t, D_HG), BF16),
                        pltpu.VMEM((N_META, 2 * D_KV), F32),
                        pltpu.VMEM((HG_DV, D_HG), F32),
                        pltpu.VMEM((HG_DV, D_HG), F32)],
        compiler_params=_params("arbitrary"),
        name="mixers",
    )(sinks, x, x, x_sample, x_meta, g, lb_param, w_f32, hg_norm, *later_weights)


def _merge_ffn_rows(x, att, hg, sga, sgb, wa_ref, wb_ref, wo_ref, ln_ffn, wup_ref, wdn_ref, ln_f):
    ya = _dot(att.astype(BF16), wa_ref[...])
    yb = _dot(hg.astype(BF16), wb_ref[...])
    mix = sga.astype(F32) * ya + sgb.astype(F32) * yb
    h1 = x + _dot(mix.astype(BF16), wo_ref[...])
    xn = _rmsnorm(h1, ln_ffn).astype(BF16)
    acc = jnp.zeros_like(h1)
    for c in range(0, D_FF, FFN_CHUNK):
        u = jnp.maximum(_dot(xn, wup_ref[:, c:c + FFN_CHUNK]), 0.0)
        acc = acc + _dot((u * u).astype(BF16), wdn_ref[c:c + FFN_CHUNK, :])
    return _rmsnorm(h1 + acc, ln_f)


def _sample_hgrn_group(rows, nw, q_ref, k_ref, i_ref, g_ref, s_ref, snew_ref, hg_ref):
    k = k_ref[rows, :]
    q = q_ref[rows, :]
    iv = i_ref[rows, :]
    r = lax.broadcasted_iota(jnp.int32, (HG_DK, HG_DK), 0)
    c = lax.broadcasted_iota(jnp.int32, (HG_DK, HG_DK), 1)
    eye = r == c

    def column(row):
        return jnp.sum(jnp.where(eye, row, 0.0), axis=1, keepdims=True)

    outs = []
    for b in range(s_ref.shape[0]):
        heads = []
        for h, hs in enumerate(_HG_SLICES):
            kc = column(k[b:b + 1, hs])
            qc = column(q[b:b + 1, hs])
            s_old = s_ref[b, h]
            s = s_old + kc * (iv[b:b + 1, hs] - s_old)
            snew_ref[b, h] = s
            heads.append(jnp.sum(qc * s, axis=0, keepdims=True))
        outs.append(jnp.concatenate(heads, axis=1))
    hg_ref[rows, :] = _hg_out(jnp.concatenate(outs, axis=0), g_ref[rows, :], nw)


def _merge_ffn_kernel(x_ref, att_ref, hg_ref, sga_ref, sgb_ref,
                      xs_ref, atts_ref, sgas_ref, sgbs_ref,
                      nw_ref, hqs_ref, hks_ref, his_ref, hgates_ref, state_ref,
                      wa_ref, wb_ref, wo_ref, wup_ref, wdn_ref, lnffn_ref, lnf_ref,
                      y_ref, ys_ref, snew_ref, hgs, *, prompt_steps):
    i = pl.program_id(0)
    weights = (wa_ref, wb_ref, wo_ref, lnffn_ref[...], wup_ref, wdn_ref, lnf_ref[...])
    group = state_ref.shape[0]
    ngroups = xs_ref.shape[0] // group

    @pl.when(i < prompt_steps)
    def _():
        y_ref[...] = _merge_ffn_rows(x_ref[...], att_ref[...], hg_ref[...], sga_ref[...], sgb_ref[...],
                                     *weights)
        rows = pl.ds(pl.multiple_of(jnp.minimum(i, ngroups - 1) * group, group), group)
        _sample_hgrn_group(rows, nw_ref[...], hqs_ref, hks_ref, his_ref, hgates_ref,
                           state_ref, snew_ref, hgs)

    @pl.when(i == prompt_steps)
    def _():
        ys_ref[...] = _merge_ffn_rows(xs_ref[...], atts_ref[...], hgs[...], sgas_ref[...],
                                      sgbs_ref[...], *weights)


def _merge_ffn(x, att, hg, sga, sgb, xs, att_s, sga_s, sgb_s, hg_norm, hq_s, hk_s, hi_s, hgate_s, state,
               wa, wb, wo, ln_ffn, w_up, w_down, ln_f):
    n = x.shape[0]
    nb = xs.shape[0]
    rows = MERGE_ROWS
    nt = n // rows
    g = SAMPLE_HG_GROUP
    assert nb // g <= nt

    def blk(c):
        return pl.BlockSpec((rows, c), lambda i: (jnp.minimum(i, nt - 1), 0))

    def sample(c):
        return pl.BlockSpec((nb, c), lambda i: (0, 0))

    sblk = pl.BlockSpec((g, HG_HEADS, HG_DK, HG_DV), lambda i: (jnp.minimum(i, nb // g - 1), 0, 0, 0))
    return pl.pallas_call(
        functools.partial(_merge_ffn_kernel, prompt_steps=nt),
        grid=(nt + 1,),
        in_specs=[blk(D_MODEL), blk(D_ATT), blk(D_HG), blk(D_MODEL), blk(D_MODEL),
                  sample(D_MODEL), sample(D_ATT), sample(D_MODEL), sample(D_MODEL),
                  _resident((1, HG_DV)), _resident(hq_s.shape), _resident(hk_s.shape),
                  _resident(hi_s.shape), _resident(hgate_s.shape), sblk,
                  _resident(wa.shape), _resident(wb.shape), _resident(wo.shape),
                  _resident(w_up.shape), _resident(w_down.shape),
                  _resident((1, D_MODEL)), _resident((1, D_MODEL))],
        out_specs=[blk(D_MODEL), sample(D_MODEL), sblk],
        out_shape=[jax.ShapeDtypeStruct((n, D_MODEL), F32), jax.ShapeDtypeStruct((nb, D_MODEL), F32),
                   jax.ShapeDtypeStruct(state.shape, F32)],
        scratch_shapes=[pltpu.VMEM((nb, D_HG), F32)],
        compiler_params=_params("arbitrary"),
        name="merge_ffn",
    )(x, att, hg, sga, sgb, xs, att_s, sga_s, sgb_s, hg_norm, hq_s, hk_s, hi_s, hgate_s, state,
      wa, wb, wo, w_up, w_down, ln_ffn, ln_f)


def _sample_attn_kernel(sink_ref, qm_ref, kvn_ref, ckt_ref, cvt_ref, mk_ref, mv_ref,
                        att_ref, nkt_ref, nvt_ref, o_all):
    nb = qm_ref.shape[0]
    head = lax.broadcasted_iota(jnp.int32, (Q_HEADS, 1), 0)
    sk = jnp.zeros((Q_HEADS, 1), F32)
    for j in range(Q_HEADS):
        sk = jnp.where(head == j, sink_ref[j], sk)
    newest = lax.broadcasted_iota(jnp.int32, (D_KV, WINDOW), 1) == WINDOW - 1
    kvn = kvn_ref[...]
    hi = kvn.astype(BF16)
    rest = kvn - hi.astype(F32)
    mid = rest.astype(BF16)
    lo = (rest - mid.astype(F32)).astype(BF16)
    r = lax.broadcasted_iota(jnp.int32, (2 * D_KV, 2 * D_KV), 0)
    c = lax.broadcasted_iota(jnp.int32, (2 * D_KV, 2 * D_KV), 1)
    eye = jnp.where(r == c, 1.0, 0.0).astype(BF16)
    pieces_t = _dot_nt(eye, jnp.concatenate([hi, mid, lo], axis=0)).astype(BF16)
    pr = lax.broadcasted_iota(jnp.int32, (3 * nb, WINDOW), 0)
    pc = lax.broadcasted_iota(jnp.int32, (3 * nb, WINDOW), 1)
    for b in range(nb):
        pick = jnp.logical_and(lax.rem(pr, nb) == b, pc == WINDOW - 1)
        new_cols = _dot(pieces_t, jnp.where(pick, 1.0, 0.0).astype(BF16))
        nkt_ref[b] = jnp.where(newest, new_cols[:D_KV], pltpu.roll(ckt_ref[b], WINDOW - 1, axis=1))
        nvt_ref[b] = jnp.where(newest, new_cols[D_KV:], pltpu.roll(cvt_ref[b], WINDOW - 1, axis=1))
    qms = [qm_ref[b].astype(BF16) for b in range(nb)]
    scores = [(_dot(qms[b], nkt_ref[b].astype(BF16)),
               _dot_nt(qms[b], mk_ref[b].astype(BF16)))
              for b in range(nb)]
    for b, (s_w, s_m) in enumerate(scores):
        m = jnp.maximum(jnp.maximum(jnp.max(s_w, axis=1, keepdims=True),
                                    jnp.max(s_m, axis=1, keepdims=True)), sk)
        e_w = jnp.exp(s_w - m)
        e_m = jnp.exp(s_m - m)
        l = (jnp.sum(e_w, axis=1, keepdims=True) + jnp.sum(e_m, axis=1, keepdims=True)
             + jnp.exp(sk - m))
        o = (_dot_nt(e_w.astype(BF16), nvt_ref[b].astype(BF16))
             + _dot(e_m.astype(BF16), mv_ref[b].astype(BF16)))
        o_all[b] = o / l
    low = lax.broadcasted_iota(jnp.int32, (1, LANES), 1) < HEAD_DIM
    pairs = []
    for p in range(Q_HEADS // 2):
        halves = []
        for half in range(2):
            j = 2 * p + half
            kv_head = j // (Q_HEADS // KV_HEADS)
            oj = jnp.where(low if kv_head == 0 else jnp.logical_not(low), o_all[:, j, :], 0.0)
            halves.append(oj if half == kv_head else pltpu.roll(oj, HEAD_DIM, axis=1))
        pairs.append(halves[0] + halves[1])
    att_ref[...] = jnp.concatenate(pairs, axis=1)


def _sample_attention(sinks, qm, kv_new, ck, cv, mk, mv):
    nb = ck.shape[0]
    g = SAMPLE_ATT_GROUP

    def blk3(a, c):
        return pl.BlockSpec((g, a, c), lambda i: (i, 0, 0))

    return pl.pallas_call(
        _sample_attn_kernel,
        grid=(nb // g,),
        in_specs=[pl.BlockSpec(memory_space=pltpu.SMEM), blk3(Q_HEADS, D_KV),
                  pl.BlockSpec((g, 2 * D_KV), lambda i: (i, 0)),
                  blk3(D_KV, WINDOW), blk3(D_KV, WINDOW), blk3(N_META, D_KV), blk3(N_META, D_KV)],
        out_specs=[pl.BlockSpec((g, D_ATT), lambda i: (i, 0)), blk3(D_KV, WINDOW), blk3(D_KV, WINDOW)],
        out_shape=[jax.ShapeDtypeStruct((nb, D_ATT), F32),
                   jax.ShapeDtypeStruct((nb, D_KV, WINDOW), F32),
                   jax.ShapeDtypeStruct((nb, D_KV, WINDOW), F32)],
        scratch_shapes=[pltpu.VMEM((g, Q_HEADS, LANES), F32)],
        compiler_params=_params("parallel"),
        name="sample_attn",
    )(sinks, qm, kv_new, ck, cv, mk, mv)


def kernel(x_prompt, x_sample, cache_k, cache_v, cache_meta_k, cache_meta_v, state_hgrn, meta,
           w_in, sinks, lb_param, hg_norm, w_att_out, w_hg_out, w_o, ln_mix, ln_ffn, w_up,
           w_down, ln_f):
    bsz, seq, _ = x_prompt.shape
    nb = x_sample.shape[0]
    ln_mix2 = ln_mix.reshape(1, D_MODEL)
    ln_ffn2 = ln_ffn.reshape(1, D_MODEL)
    ln_f2 = ln_f.reshape(1, D_MODEL)
    nw = hg_norm.reshape(1, HG_DV)

    xs = x_sample.reshape(nb, D_MODEL)

    xp = x_prompt.reshape(bsz * seq, D_MODEL)
    (att_p, hg_p, sga_p, sgb_p, state_p, lastkv_p,
     qm_s, kv_s, hq_s, hk_s, hi_s, hgate_s, sga_s, sgb_s,
     wa, wb, wo, wup, wdn) = _mixers(
        sinks.reshape(Q_HEADS), xp, xs, meta, ln_mix2, lb_param, w_in[0], nw,
        (w_att_out[0], w_hg_out[0], w_o[0], w_up[0], w_down[0]), bsz)

    def window_t(c):
        return jnp.swapaxes(c[0].reshape(nb, WINDOW, D_KV), 1, 2)

    att_s, nkt_s, nvt_s = _sample_attention(
        sinks.reshape(Q_HEADS), qm_s, kv_s, window_t(cache_k), window_t(cache_v),
        cache_meta_k[0].reshape(nb, N_META, D_KV), cache_meta_v[0].reshape(nb, N_META, D_KV))

    y_p, y_s, state_s = _merge_ffn(xp, att_p, hg_p, sga_p, sgb_p, xs, att_s, sga_s, sgb_s,
                                   nw, hq_s, hk_s, hi_s, hgate_s, state_hgrn[0],
                                   wa, wb, wo, ln_ffn2, wup, wdn, ln_f2)

    kv5 = lastkv_p.reshape(bsz, WINDOW, 2, KV_HEADS, HEAD_DIM)
    meta5 = jnp.broadcast_to(kv_s[nb:].reshape(1, N_META, 2, KV_HEADS, HEAD_DIM),
                             (bsz, N_META, 2, KV_HEADS, HEAD_DIM))
    return (y_p.reshape(bsz, seq, D_MODEL),
            y_s.reshape(nb, 1, D_MODEL),
            kv5[None, :, :, 0],
            kv5[None, :, :, 1],
            meta5[None, :, :, 0],
            meta5[None, :, :, 1],
            state_p[None],
            jnp.swapaxes(nkt_s, 1, 2).reshape(1, nb, WINDOW, KV_HEADS, HEAD_DIM),
            jnp.swapaxes(nvt_s, 1, 2).reshape(1, nb, WINDOW, KV_HEADS, HEAD_DIM),
            state_s[None])
```

```python
import functools

import jax
import jax.numpy as jnp
from jax import lax
from jax.experimental import pallas as pl
from jax.experimental.pallas import tpu as pltpu

F32 = jnp.float32
BF16 = jnp.bfloat16

D_MODEL = 1024
N_META = 16
WINDOW = 128
HEAD_DIM = 64
Q_HEADS = 8
KV_HEADS = 2
D_ATT = Q_HEADS * HEAD_DIM
D_KV = KV_HEADS * HEAD_DIM
HG_HEADS = 4
HG_DK = 128
HG_DV = 128
D_HG = HG_HEADS * HG_DK
HG_CHUNK = 64
D_FF = 4 * D_MODEL
EPS = 1e-6
C_Q = 0
C_KV = C_Q + D_ATT
C_HQ = C_KV + 2 * D_KV
C_HF = C_HQ + D_HG
C_HI = C_HF + D_HG
C_HGATE = C_HI + D_HG
C_GA = C_HGATE + D_HG
C_GB = C_GA + D_MODEL
D_IN = C_GB + D_MODEL

VMEM_LIMIT_BYTES = 56 * 1024 * 1024
MIX_ROWS = 512
PROJ_GROUP = 256
WEIGHT_SLAB_ROWS = 64
MERGE_ROWS = 512
FFN_CHUNK = 2048
LANES = 128
BF16_ROWS = 16
SAMPLE_ATT_GROUP = 16
SAMPLE_HG_GROUP = 8

_NT = (((1,), (1,)), ((), ()))
_TN = (((0,), (0,)), ((), ()))


def _dot(a, b):
    return jnp.dot(a, b, preferred_element_type=F32)


def _dot_nt(a, b):
    return lax.dot_general(a, b, _NT, preferred_element_type=F32)


def _dot_tn(a, b):
    return lax.dot_general(a, b, _TN, preferred_element_type=F32)


def _rows_from_linear(ref):
    per = D_MODEL // LANES
    rows = ref.shape[0] // per
    return jnp.concatenate([ref[pl.ds(c, rows, stride=per), :] for c in range(per)], axis=1)


def _rows_to_linear(ref, value):
    per = D_MODEL // LANES
    for c in range(per):
        ref[pl.ds(c, value.shape[0], stride=per), :] = value[:, c * LANES:(c + 1) * LANES]


def _rmsnorm(x, g):
    return x * lax.rsqrt(jnp.mean(x * x, axis=-1, keepdims=True) + EPS) * g


def _resident(shape):
    return pl.BlockSpec(shape, lambda *_: (0,) * len(shape), pipeline_mode=pl.Buffered(1))


def _params(*sem):
    return pltpu.CompilerParams(dimension_semantics=sem, vmem_limit_bytes=VMEM_LIMIT_BYTES)


def _lower_bound(lbp):
    e = jnp.exp(lbp - jnp.max(lbp, axis=0, keepdims=True))
    return e[0:1] / jnp.sum(e, axis=0, keepdims=True)


class _Projection:
    def __init__(self, xn, lbp, w_ref):
        self.xn = xn
        self.w_ref = w_ref
        self.lb = _lower_bound(lbp)

    def cols(self, base, part):
        return _dot(self.xn[...], self.w_ref[:, base + part.start:base + part.stop])

    def q_att(self, part=slice(0, D_ATT)):
        return self.cols(C_Q, part) * (HEAD_DIM ** -0.5)

    def kv(self):
        return self.cols(C_KV, slice(0, 2 * D_KV))

    def q_hg(self, part=slice(0, D_HG)):
        return self.cols(C_HQ, part) * (HG_DK ** -0.5)

    def forget(self, part=slice(0, D_HG)):
        lb = self.lb[:, part]
        f = lb + (1.0 - lb) * jax.nn.sigmoid(self.cols(C_HF, part))
        return 1.0 - f, jnp.log(f)

    def i_hg(self, part=slice(0, D_HG)):
        return self.cols(C_HI, part)

    def swish_gate(self, part=slice(0, D_HG)):
        g = self.cols(C_HGATE, part)
        return g * jax.nn.sigmoid(g)

    def branch_gate(self, base, part=slice(0, D_MODEL)):
        return jax.nn.sigmoid(self.cols(base, part))


class _Attention:
    def __init__(self, sink_ref, q_ref, kv, first, o_ref):
        self.sink_ref, self.q_ref, self.first, self.o_ref = sink_ref, q_ref, first, o_ref
        self.nsub = q_ref.shape[0] // WINDOW
        self.nk = 2 * WINDOW + N_META
        self.meta0 = WINDOW + q_ref.shape[0]
        lane = lax.broadcasted_iota(jnp.int32, (kv.shape[0], D_KV), 1)
        low = lane < HEAD_DIM
        k = kv[:, :D_KV]
        ksw = pltpu.roll(k, HEAD_DIM, axis=1)
        self.kboth = (jnp.where(low, k, ksw).astype(BF16), jnp.where(low, ksw, k).astype(BF16))
        qlane = lax.broadcasted_iota(jnp.int32, (1, LANES), 1)
        self.keep = (jnp.where(qlane < HEAD_DIM, 1.0, 0.0).astype(BF16),
                     jnp.where(qlane < HEAD_DIM, 0.0, 1.0).astype(BF16))
        r = lax.broadcasted_iota(jnp.int32, (D_KV, D_KV), 0)
        c = lax.broadcasted_iota(jnp.int32, (D_KV, D_KV), 1)
        eye = jnp.where(r == c, 1.0, 0.0).astype(BF16)
        self.vt = _dot_nt(eye, kv[:, D_KV:].astype(BF16)).astype(BF16)
        self.ones = jnp.ones((BF16_ROWS, self.nk), BF16)
        self.key = lax.broadcasted_iota(jnp.int32, (self.nk, 2 * WINDOW), 0)
        col = lax.broadcasted_iota(jnp.int32, (self.nk, 2 * WINDOW), 1)
        self.qry = jnp.bitwise_and(col, WINDOW - 1)
        self.first_head = lax.broadcasted_iota(jnp.int32, (1, 2 * WINDOW), 1) < WINDOW
        self.scores = {}

    def issue_scores(self, i):
        r0 = i * WINDOW
        out = []
        for p in range(Q_HEADS // 2):
            kb = self.kboth[(2 * p) // (Q_HEADS // KV_HEADS)]
            kmat = jnp.concatenate([kb[r0:r0 + 2 * WINDOW], kb[self.meta0:]], axis=0)
            qp = self.q_ref[r0:r0 + WINDOW, p * LANES:(p + 1) * LANES]
            q2 = jnp.concatenate([qp * self.keep[0], qp * self.keep[1]], axis=0)
            out.append(_dot_nt(kmat, q2))
        self.scores[i] = out

    def finish(self, i):
        r0 = i * WINDOW
        lo = jnp.where(self.first, WINDOW - 1, self.qry) if i == 0 else self.qry
        mask = jnp.logical_and(self.key > lo, self.key <= self.qry + WINDOW)
        mask = jnp.logical_or(mask, self.key >= 2 * WINDOW)
        vaug = []
        for h in range(KV_HEADS):
            vth = self.vt[h * HEAD_DIM:(h + 1) * HEAD_DIM]
            vaug.append(jnp.concatenate(
                [jnp.concatenate([vth[:, r0:r0 + 2 * WINDOW], vth[:, self.meta0:]], axis=1),
                 self.ones], axis=0))
        outs = []
        for p, raw in enumerate(self.scores.pop(i)):
            h = (2 * p) // (Q_HEADS // KV_HEADS)
            s = jnp.where(mask, raw, -jnp.inf)
            sk = jnp.where(self.first_head, self.sink_ref[2 * p], self.sink_ref[2 * p + 1])
            m = jnp.maximum(jnp.max(s, axis=0, keepdims=True), sk)
            e = jnp.exp(s - m).astype(BF16)
            oa = _dot(vaug[h], e)
            o = oa[:HEAD_DIM] / (oa[HEAD_DIM:HEAD_DIM + 1] + jnp.exp(sk - m))
            outs += [o[:, :WINDOW], o[:, WINDOW:]]
        self.o_ref[r0:r0 + WINDOW, :] = jnp.concatenate(outs, axis=0).T.astype(self.o_ref.dtype)


def _cumsum_rows(x):
    t = x.shape[0]
    row = lax.broadcasted_iota(jnp.int32, x.shape, 0)
    d = 1
    while d < t:
        x = x + jnp.where(row >= d, pltpu.roll(x, d, axis=0), 0.0)
        d *= 2
    return x


def _hg_out(o, gate, nw):
    parts = []
    for h in range(HG_HEADS):
        oh = o[:, h * HG_DV:(h + 1) * HG_DV]
        parts.append(oh * lax.rsqrt(jnp.mean(oh * oh, axis=-1, keepdims=True) + EPS) * nw)
    return jnp.concatenate(parts, axis=1) * gate


_HG_SLICES = [slice(h * HG_DK, (h + 1) * HG_DK) for h in range(HG_HEADS)]


def _state_update(iv, kd):
    return jnp.concatenate([_dot_tn(iv[:, hs], kd[:, hs]) for hs in _HG_SLICES], axis=1)


def _meta_state(mk, mlogf, mi):
    b = _cumsum_rows(mlogf)
    kd = (mk * jnp.exp(b[-1:] - b)).astype(BF16)
    return _state_update(mi.astype(BF16), kd)


class _Hgrn:
    def __init__(self, q_ref, k_ref, logf_ref, i_ref, g_ref, nw, state, o_ref):
        self.q_ref, self.k_ref, self.logf_ref, self.i_ref, self.g_ref = q_ref, k_ref, logf_ref, i_ref, g_ref
        self.nw, self.state, self.o_ref = nw, state, o_ref
        t = HG_CHUNK
        self.nchunk = q_ref.shape[0] // t
        r = lax.broadcasted_iota(jnp.int32, (t, t), 0)
        c = lax.broadcasted_iota(jnp.int32, (t, t), 1)
        self.causal = r >= c
        self.cums, self.terms, self.entering = {}, {}, {}

    def rows(self, ci):
        return slice(ci * HG_CHUNK, (ci + 1) * HG_CHUNK)

    def issue_cumsum(self, ci):
        self.cums[ci] = _cumsum_rows(self.logf_ref[self.rows(ci), :])

    def issue_local(self, ci):
        rows = self.rows(ci)
        b = self.cums.pop(ci)
        bl = b[-1:]
        k = self.k_ref[rows, :].astype(F32)
        qe = (self.q_ref[rows, :].astype(F32) * jnp.exp(b)).astype(BF16)
        ke = (k * jnp.exp(-b)).astype(BF16)
        kd = (k * jnp.exp(bl - b)).astype(BF16)
        iv = self.i_ref[rows, :].astype(BF16)
        a = [jnp.where(self.causal, _dot_nt(qe[:, hs], ke[:, hs]), 0.0).astype(BF16) for hs in _HG_SLICES]
        self.terms[ci] = (qe, iv, a)
        self.entering[ci] = self.state.astype(BF16)
        self.state = self.state * jnp.exp(bl) + _state_update(iv, kd)

    def finish(self, ci):
        rows = self.rows(ci)
        qe, iv, a = self.terms.pop(ci)
        ent = self.entering.pop(ci)
        outs = [_dot(a[h], iv[:, hs]) + _dot_nt(qe[:, hs], ent[:, hs]) for h, hs in enumerate(_HG_SLICES)]
        o = _hg_out(jnp.concatenate(outs, axis=1), self.g_ref[rows, :].astype(F32), self.nw)
        self.o_ref[rows, :] = o.astype(self.o_ref.dtype)


def _load_as_bf16(w_hbm, w_vmem, stage, sem):
    rows = stage.shape[1]
    nslab = w_hbm.shape[0] // rows

    def slab_copy(c):
        return pltpu.make_async_copy(w_hbm.at[pl.ds(c * rows, rows), :], stage.at[c % 2], sem.at[c % 2])

    slab_copy(0).start()
    for c in range(nslab):
        if c + 1 < nslab:
            slab_copy(c + 1).start()
        slab_copy(c).wait()
        w_vmem[pl.ds(c * rows, rows), :] = stage[c % 2].astype(w_vmem.dtype)


def _mixer_kernel(sink_ref, x0_ref, xnext_ref, xs_ref, xm_ref, g_ref, lbp_ref, w_hbm, nw_ref,
                  wa32, wb32, wo32, wup32, wdn32,
                  att_ref, hg_ref, sga_ref, sgb_ref, sfin_ref, lastkv_ref,
                  qs_ref, kvs_ref, hqs_ref, hks_ref, his_ref, hgs_ref, sgas_ref, sgbs_ref,
                  wa16, wb16, wo16, wup16, wdn16,
                  w_ref, stage, sem, xn_s, zq, zkv, zhq, zhk, zlogf, zhi, zhg, kvm_ref, st_ref, mst_ref,
                  *, tiles_per_seq):
    s = pl.program_id(0)
    t = xnext_ref.shape[0]

    @pl.when(s == 0)
    def _():
        _load_as_bf16(w_hbm, w_ref, stage, sem)
        xn_s[...] = _rmsnorm(x0_ref[...], g_ref[...]).astype(xn_s.dtype)
        for ref in (zq, zkv, zhq, zhk, zlogf, zhi, zhg, st_ref):
            ref[...] = jnp.zeros(ref.shape, ref.dtype)
        xs = _rows_from_linear(xs_ref)
        nb = xs.shape[0]
        small = jnp.concatenate([xs, xm_ref[...]], axis=0)
        p = _Projection(_rmsnorm(small, g_ref[...]).astype(BF16), lbp_ref[...], w_ref)
        q_small = p.q_att()
        low = lax.broadcasted_iota(jnp.int32, (1, LANES), 1) < HEAD_DIM
        for j in range(Q_HEADS):
            kv_head, half = j // (Q_HEADS // KV_HEADS), j % 2
            pair = q_small[:, (j // 2) * LANES:(j // 2 + 1) * LANES]
            moved = pair if half == kv_head else pltpu.roll(pair, HEAD_DIM, axis=1)
            qs_ref[:, j, :] = jnp.where(low if kv_head == 0 else jnp.logical_not(low), moved, 0.0)
        kv = p.kv()
        kvs_ref[...] = kv
        kvm_ref[...] = kv[nb:]
        hqs_ref[...] = p.q_hg()
        k, logf = p.forget()
        hks_ref[...] = k
        iv = p.i_hg()
        his_ref[...] = iv
        hgs_ref[...] = p.swish_gate()
        sgas_ref[...] = p.branch_gate(C_GA).astype(sgas_ref.dtype)
        sgbs_ref[...] = p.branch_gate(C_GB).astype(sgbs_ref.dtype)
        mst_ref[...] = _meta_state(k[nb:], logf[nb:], iv[nb:])

    first = lax.rem(jnp.maximum(s - 1, 0), tiles_per_seq) == 0

    att = _Attention(sink_ref, zq, jnp.concatenate([zkv[...], kvm_ref[...]], axis=0), first, att_ref)
    last = zkv[t:t + WINDOW, :]
    lastkv_ref[...] = last
    zkv[0:WINDOW, :] = last
    entering = jnp.where(first, mst_ref[...], st_ref[...])
    hg = _Hgrn(zhq, zhk, zlogf, zhi, zhg, nw_ref[...], entering, hg_ref)
    proj = _Projection(xn_s, lbp_ref[...], w_ref)

    def gate_a(p):
        sga_ref[:, p] = proj.branch_gate(C_GA, p).astype(sga_ref.dtype)

    def gate_b(p):
        sgb_ref[:, p] = proj.branch_gate(C_GB, p).astype(sgb_ref.dtype)

    def new_kv(_):
        zkv[WINDOW:WINDOW + t, :] = proj.kv()

    def new_forget(p):
        k_new, logf_new = proj.forget(p)
        zhk[:, p] = k_new.astype(zhk.dtype)
        zlogf[:, p] = logf_new

    def new_q(p):
        zq[:, p] = proj.q_att(p).astype(zq.dtype)

    def new_hq(p):
        zhq[:, p] = proj.q_hg(p).astype(zhq.dtype)

    def new_hi(p):
        zhi[:, p] = proj.i_hg(p).astype(zhi.dtype)

    def new_gate(p):
        zhg[:, p] = proj.swish_gate(p).astype(zhg.dtype)

    def parts(n):
        return [slice(c, c + PROJ_GROUP) for c in range(0, n, PROJ_GROUP)]

    groups = ([(gate_a, p) for p in parts(D_MODEL)] + [(gate_b, p) for p in parts(D_MODEL)]
              + [(new_kv, None)] + [(new_q, p) for p in parts(D_ATT)]
              + [(new_forget, p) for p in parts(D_HG)] + [(new_hq, p) for p in parts(D_HG)]
              + [(new_hi, p) for p in parts(D_HG)] + [(new_gate, p) for p in parts(D_HG)])

    def save_state():
        st_ref[...] = hg.state
        for h, hs in enumerate(_HG_SLICES):
            sfin_ref[h] = hg.state[:, hs].T

    qk, cs, loc, fin, out = att.issue_scores, hg.issue_cumsum, hg.issue_local, att.finish, hg.finish
    mixer_work = {
        0: [(qk, 0), (cs, 0), (cs, 1), (cs, 2), (cs, 3)],
        1: [(cs, 4), (cs, 5), (loc, 0)],
        2: [(cs, 6), (cs, 7), (loc, 1)],
        3: [(qk, 1), (loc, 2)],
        4: [(fin, 0), (loc, 3)],
        5: [(out, 0), (loc, 4)],
        6: [(qk, 2), (out, 1), (loc, 5)],
        7: [(fin, 1), (out, 2), (loc, 6)],
        8: [(qk, 3), (out, 3), (loc, 7), (save_state,)],
        9: [(out, 4)],
        10: [(fin, 2), (out, 5)],
        11: [(out, 6)],
        12: [(out, 7)],
        13: [(fin, 3)],
    }
    for slot, (fn, p) in enumerate(groups):
        fn(p)
        for item in mixer_work.get(slot, []):
            item[0](*item[1:])
    xn_s[...] = _rmsnorm(xnext_ref[...], g_ref[...]).astype(xn_s.dtype)
    for src, dst in ((wa32, wa16), (wb32, wb16), (wo32, wo16), (wup32, wup16), (wdn32, wdn16)):
        dst[...] = src[...].astype(dst.dtype)


def _mixers(sinks, x, x_sample, x_meta, g, lb_param, w_f32, hg_norm, later_weights, bsz):
    n = x.shape[0]
    t = MIX_ROWS
    nt = n // t
    per_seq = nt // bsz
    rows = x_sample.shape[0] * LANES // D_MODEL + x_meta.shape[0]

    def this_tile(c):
        return pl.BlockSpec((t, c), lambda s: (jnp.minimum(s, nt - 1), 0))

    def prev_tile(c):
        return pl.BlockSpec((t, c), lambda s: (jnp.maximum(s - 1, 0), 0))

    def prev_seq(shape):
        return pl.BlockSpec((None,) + shape,
                            lambda s: (jnp.maximum(s - 1, 0) // per_seq,) + (0,) * len(shape))

    def small(c):
        return pl.BlockSpec((rows, c), lambda s: (0, 0))

    first_tile = pl.BlockSpec((t, D_MODEL), lambda s: (0, 0), pipeline_mode=pl.Buffered(1))
    next_tile = pl.BlockSpec((t, D_MODEL), lambda s: (jnp.minimum(s + 1, nt - 1), 0))
    small_widths = (2 * D_KV, D_HG, D_HG, D_HG, D_HG, D_MODEL, D_MODEL)
    small_dtypes = (F32, F32, F32, F32, F32, BF16, BF16)
    qm_shape = (rows, Q_HEADS, LANES)
    qm_spec = pl.BlockSpec(qm_shape, lambda s: (0, 0, 0))

    def row_block(wt):
        return pl.BlockSpec((wt.shape[0] // nt, wt.shape[1]), lambda s: (jnp.minimum(s, nt - 1), 0))

    return pl.pallas_call(
        functools.partial(_mixer_kernel, tiles_per_seq=per_seq),
        grid=(nt + 1,),
        in_specs=[pl.BlockSpec(memory_space=pltpu.SMEM), first_tile, next_tile,
                  _resident(x_sample.shape), _resident(x_meta.shape), _resident((1, D_MODEL)),
                  _resident(lb_param.shape), pl.BlockSpec(memory_space=pl.ANY), _resident((1, HG_DV))]
                 + [row_block(wt) for wt in later_weights],
        out_specs=[prev_tile(D_ATT), prev_tile(D_HG), this_tile(D_MODEL), this_tile(D_MODEL),
                   prev_seq((HG_HEADS, HG_DK, HG_DV)), prev_seq((WINDOW, 2 * D_KV))]
                  + [qm_spec] + [small(c) for c in small_widths] + [row_block(wt) for wt in later_weights],
        out_shape=[jax.ShapeDtypeStruct((n, D_ATT), BF16), jax.ShapeDtypeStruct((n, D_HG), BF16),
                   jax.ShapeDtypeStruct((n, D_MODEL), BF16), jax.ShapeDtypeStruct((n, D_MODEL), BF16),
                   jax.ShapeDtypeStruct((bsz, HG_HEADS, HG_DK, HG_DV), F32),
                   jax.ShapeDtypeStruct((bsz, WINDOW, 2 * D_KV), F32)]
                  + [jax.ShapeDtypeStruct(qm_shape, F32)]
                  + [jax.ShapeDtypeStruct((rows, c), d) for c, d in zip(small_widths, small_dtypes)]
                  + [jax.ShapeDtypeStruct(wt.shape, BF16) for wt in later_weights],
        scratch_shapes=[pltpu.VMEM((D_MODEL, D_IN), BF16),
                        pltpu.VMEM((2, WEIGHT_SLAB_ROWS, D_IN), F32),
                        pltpu.SemaphoreType.DMA((2,)),
                        pltpu.VMEM((t, D_MODEL), BF16),
                        pltpu.VMEM((t, D_ATT), BF16), pltpu.VMEM((WINDOW + t, 2 * D_KV), F32),
                        pltpu.VMEM((t, D_HG), BF16), pltpu.VMEM((t, D_HG), BF16),
                        pltpu.VMEM((t, D_HG), F32), pltpu.VMEM((t, D_HG), BF16),
                        pltpu.VMEM((t, D_HG), BF16),
                        pltpu.VMEM((N_META, 2 * D_KV), F32),
                        pltpu.VMEM((HG_DV, D_HG), F32),
                        pltpu.VMEM((HG_DV, D_HG), F32)],
        compiler_params=_params("arbitrary"),
        name="mixers",
    )(sinks, x, x, x_sample, x_meta, g, lb_param, w_f32, hg_norm, *later_weights)


def _merge_ffn_rows(x, att, hg, sga, sgb, wa_ref, wb_ref, wo_ref, ln_ffn, wup_ref, wdn_ref, ln_f):
    ya = _dot(att.astype(BF16), wa_ref[...])
    yb = _dot(hg.astype(BF16), wb_ref[...])
    mix = sga.astype(F32) * ya + sgb.astype(F32) * yb
    h1 = x + _dot(mix.astype(BF16), wo_ref[...])
    xn = _rmsnorm(h1, ln_ffn).astype(BF16)
    acc = jnp.zeros_like(h1)
    for c in range(0, D_FF, FFN_CHUNK):
        u = jnp.maximum(_dot(xn, wup_ref[:, c:c + FFN_CHUNK]), 0.0)
        acc = acc + _dot((u * u).astype(BF16), wdn_ref[c:c + FFN_CHUNK, :])
    return _rmsnorm(h1 + acc, ln_f)


def _sample_hgrn_group(rows, nw, q_ref, k_ref, i_ref, g_ref, s_ref, snew_ref, hg_ref):
    k = k_ref[rows, :]
    q = q_ref[rows, :]
    iv = i_ref[rows, :]
    r = lax.broadcasted_iota(jnp.int32, (HG_DK, HG_DK), 0)
    c = lax.broadcasted_iota(jnp.int32, (HG_DK, HG_DK), 1)
    eye = r == c

    def column(row):
        return jnp.sum(jnp.where(eye, row, 0.0), axis=1, keepdims=True)

    outs = []
    for b in range(s_ref.shape[0]):
        heads = []
        for h, hs in enumerate(_HG_SLICES):
            kc = column(k[b:b + 1, hs])
            qc = column(q[b:b + 1, hs])
            s_old = s_ref[b, h]
            s = s_old + kc * (iv[b:b + 1, hs] - s_old)
            snew_ref[b, h] = s
            heads.append(jnp.sum(qc * s, axis=0, keepdims=True))
        outs.append(jnp.concatenate(heads, axis=1))
    hg_ref[rows, :] = _hg_out(jnp.concatenate(outs, axis=0), g_ref[rows, :], nw)


def _merge_ffn_kernel(x_ref, att_ref, hg_ref, sga_ref, sgb_ref,
                      xs_ref, atts_ref, sgas_ref, sgbs_ref,
                      nw_ref, hqs_ref, hks_ref, his_ref, hgates_ref, state_ref,
                      wa_ref, wb_ref, wo_ref, wup_ref, wdn_ref, lnffn_ref, lnf_ref,
                      y_ref, ys_ref, snew_ref, hgs, *, prompt_steps):
    i = pl.program_id(0)
    weights = (wa_ref, wb_ref, wo_ref, lnffn_ref[...], wup_ref, wdn_ref, lnf_ref[...])
    group = state_ref.shape[0]
    ngroups = atts_ref.shape[0] // group

    @pl.when(i < prompt_steps)
    def _():
        y_ref[...] = _merge_ffn_rows(x_ref[...], att_ref[...], hg_ref[...], sga_ref[...], sgb_ref[...],
                                     *weights)
        rows = pl.ds(pl.multiple_of(jnp.minimum(i, ngroups - 1) * group, group), group)
        _sample_hgrn_group(rows, nw_ref[...], hqs_ref, hks_ref, his_ref, hgates_ref,
                           state_ref, snew_ref, hgs)

    @pl.when(i == prompt_steps)
    def _():
        _rows_to_linear(ys_ref, _merge_ffn_rows(_rows_from_linear(xs_ref), atts_ref[...], hgs[...],
                                                sgas_ref[...], sgbs_ref[...], *weights))


def _merge_ffn(x, att, hg, sga, sgb, xs, att_s, sga_s, sgb_s, hg_norm, hq_s, hk_s, hi_s, hgate_s, state,
               wa, wb, wo, ln_ffn, w_up, w_down, ln_f):
    n = x.shape[0]
    nb = att_s.shape[0]
    linear = pl.BlockSpec(xs.shape, lambda i: (0, 0))
    rows = MERGE_ROWS
    nt = n // rows
    g = SAMPLE_HG_GROUP
    assert nb // g <= nt

    def blk(c):
        return pl.BlockSpec((rows, c), lambda i: (jnp.minimum(i, nt - 1), 0))

    def sample(c):
        return pl.BlockSpec((nb, c), lambda i: (0, 0))

    sblk = pl.BlockSpec((g, HG_HEADS, HG_DK, HG_DV), lambda i: (jnp.minimum(i, nb // g - 1), 0, 0, 0))
    return pl.pallas_call(
        functools.partial(_merge_ffn_kernel, prompt_steps=nt),
        grid=(nt + 1,),
        in_specs=[blk(D_MODEL), blk(D_ATT), blk(D_HG), blk(D_MODEL), blk(D_MODEL),
                  linear, sample(D_ATT), sample(D_MODEL), sample(D_MODEL),
                  _resident((1, HG_DV)), _resident(hq_s.shape), _resident(hk_s.shape),
                  _resident(hi_s.shape), _resident(hgate_s.shape), sblk,
                  _resident(wa.shape), _resident(wb.shape), _resident(wo.shape),
                  _resident(w_up.shape), _resident(w_down.shape),
                  _resident((1, D_MODEL)), _resident((1, D_MODEL))],
        out_specs=[blk(D_MODEL), linear, sblk],
        out_shape=[jax.ShapeDtypeStruct((n, D_MODEL), F32), jax.ShapeDtypeStruct(xs.shape, F32),
                   jax.ShapeDtypeStruct(state.shape, F32)],
        scratch_shapes=[pltpu.VMEM((nb, D_HG), F32)],
        compiler_params=_params("arbitrary"),
        name="merge_ffn",
    )(x, att, hg, sga, sgb, xs, att_s, sga_s, sgb_s, hg_norm, hq_s, hk_s, hi_s, hgate_s, state,
      wa, wb, wo, w_up, w_down, ln_ffn, ln_f)


def _sample_attn_kernel(sink_ref, qm_ref, kvn_ref, ckt_ref, cvt_ref, mk_ref, mv_ref,
                        att_ref, nkt_ref, nvt_ref, o_all):
    nb = qm_ref.shape[0]
    head = lax.broadcasted_iota(jnp.int32, (Q_HEADS, 1), 0)
    sk = jnp.zeros((Q_HEADS, 1), F32)
    for j in range(Q_HEADS):
        sk = jnp.where(head == j, sink_ref[j], sk)
    newest = lax.broadcasted_iota(jnp.int32, (D_KV, WINDOW), 1) == WINDOW - 1
    kvn = kvn_ref[...]
    hi = kvn.astype(BF16)
    rest = kvn - hi.astype(F32)
    mid = rest.astype(BF16)
    lo = (rest - mid.astype(F32)).astype(BF16)
    r = lax.broadcasted_iota(jnp.int32, (2 * D_KV, 2 * D_KV), 0)
    c = lax.broadcasted_iota(jnp.int32, (2 * D_KV, 2 * D_KV), 1)
    eye = jnp.where(r == c, 1.0, 0.0).astype(BF16)
    pieces_t = _dot_nt(eye, jnp.concatenate([hi, mid, lo], axis=0)).astype(BF16)
    pr = lax.broadcasted_iota(jnp.int32, (3 * nb, WINDOW), 0)
    pc = lax.broadcasted_iota(jnp.int32, (3 * nb, WINDOW), 1)
    for b in range(nb):
        pick = jnp.logical_and(lax.rem(pr, nb) == b, pc == WINDOW - 1)
        new_cols = _dot(pieces_t, jnp.where(pick, 1.0, 0.0).astype(BF16))
        nkt_ref[b] = jnp.where(newest, new_cols[:D_KV], pltpu.roll(ckt_ref[b], WINDOW - 1, axis=1))
        nvt_ref[b] = jnp.where(newest, new_cols[D_KV:], pltpu.roll(cvt_ref[b], WINDOW - 1, axis=1))
    qms = [qm_ref[b].astype(BF16) for b in range(nb)]
    scores = [(_dot(qms[b], nkt_ref[b].astype(BF16)),
               _dot_nt(qms[b], mk_ref[b].astype(BF16)))
              for b in range(nb)]
    for b, (s_w, s_m) in enumerate(scores):
        m = jnp.maximum(jnp.maximum(jnp.max(s_w, axis=1, keepdims=True),
                                    jnp.max(s_m, axis=1, keepdims=True)), sk)
        e_w = jnp.exp(s_w - m)
        e_m = jnp.exp(s_m - m)
        l = (jnp.sum(e_w, axis=1, keepdims=True) + jnp.sum(e_m, axis=1, keepdims=True)
             + jnp.exp(sk - m))
        o = (_dot_nt(e_w.astype(BF16), nvt_ref[b].astype(BF16))
             + _dot(e_m.astype(BF16), mv_ref[b].astype(BF16)))
        o_all[b] = o / l
    low = lax.broadcasted_iota(jnp.int32, (1, LANES), 1) < HEAD_DIM
    pairs = []
    for p in range(Q_HEADS // 2):
        halves = []
        for half in range(2):
            j = 2 * p + half
            kv_head = j // (Q_HEADS // KV_HEADS)
            oj = jnp.where(low if kv_head == 0 else jnp.logical_not(low), o_all[:, j, :], 0.0)
            halves.append(oj if half == kv_head else pltpu.roll(oj, HEAD_DIM, axis=1))
        pairs.append(halves[0] + halves[1])
    att_ref[...] = jnp.concatenate(pairs, axis=1)


def _sample_attention(sinks, qm, kv_new, ck, cv, mk, mv):
    nb = ck.shape[0]
    g = SAMPLE_ATT_GROUP

    def blk3(a, c):
        return pl.BlockSpec((g, a, c), lambda i: (i, 0, 0))

    return pl.pallas_call(
        _sample_attn_kernel,
        grid=(nb // g,),
        in_specs=[pl.BlockSpec(memory_space=pltpu.SMEM), blk3(Q_HEADS, D_KV),
                  pl.BlockSpec((g, 2 * D_KV), lambda i: (i, 0)),
                  blk3(D_KV, WINDOW), blk3(D_KV, WINDOW), blk3(N_META, D_KV), blk3(N_META, D_KV)],
        out_specs=[pl.BlockSpec((g, D_ATT), lambda i: (i, 0)), blk3(D_KV, WINDOW), blk3(D_KV, WINDOW)],
        out_shape=[jax.ShapeDtypeStruct((nb, D_ATT), F32),
                   jax.ShapeDtypeStruct((nb, D_KV, WINDOW), F32),
                   jax.ShapeDtypeStruct((nb, D_KV, WINDOW), F32)],
        scratch_shapes=[pltpu.VMEM((g, Q_HEADS, LANES), F32)],
        compiler_params=_params("parallel"),
        name="sample_attn",
    )(sinks, qm, kv_new, ck, cv, mk, mv)


def kernel(x_prompt, x_sample, cache_k, cache_v, cache_meta_k, cache_meta_v, state_hgrn, meta,
           w_in, sinks, lb_param, hg_norm, w_att_out, w_hg_out, w_o, ln_mix, ln_ffn, w_up,
           w_down, ln_f):
    bsz, seq, _ = x_prompt.shape
    nb = x_sample.shape[0]
    ln_mix2 = ln_mix.reshape(1, D_MODEL)
    ln_ffn2 = ln_ffn.reshape(1, D_MODEL)
    ln_f2 = ln_f.reshape(1, D_MODEL)
    nw = hg_norm.reshape(1, HG_DV)

    xs = x_sample.reshape(nb * D_MODEL // LANES, LANES)

    xp = x_prompt.reshape(bsz * seq, D_MODEL)
    (att_p, hg_p, sga_p, sgb_p, state_p, lastkv_p,
     qm_s, kv_s, hq_s, hk_s, hi_s, hgate_s, sga_s, sgb_s,
     wa, wb, wo, wup, wdn) = _mixers(
        sinks.reshape(Q_HEADS), xp, xs, meta, ln_mix2, lb_param, w_in[0], nw,
        (w_att_out[0], w_hg_out[0], w_o[0], w_up[0], w_down[0]), bsz)

    def window_t(c):
        return jnp.swapaxes(c[0].reshape(nb, WINDOW, D_KV), 1, 2)

    att_s, nkt_s, nvt_s = _sample_attention(
        sinks.reshape(Q_HEADS), qm_s, kv_s, window_t(cache_k), window_t(cache_v),
        cache_meta_k[0].reshape(nb, N_META, D_KV), cache_meta_v[0].reshape(nb, N_META, D_KV))

    y_p, y_s, state_s = _merge_ffn(xp, att_p, hg_p, sga_p, sgb_p, xs, att_s, sga_s, sgb_s,
                                   nw, hq_s, hk_s, hi_s, hgate_s, state_hgrn[0],
                                   wa, wb, wo, ln_ffn2, wup, wdn, ln_f2)

    kv5 = lastkv_p.reshape(bsz, WINDOW, 2, KV_HEADS, HEAD_DIM)
    meta5 = jnp.broadcast_to(kv_s[nb:].reshape(1, N_META, 2, KV_HEADS, HEAD_DIM),
                             (bsz, N_META, 2, KV_HEADS, HEAD_DIM))
    return (y_p.reshape(bsz, seq, D_MODEL),
            y_s.reshape(nb, 1, D_MODEL),
            kv5[None, :, :, 0],
            kv5[None, :, :, 1],
            meta5[None, :, :, 0],
            meta5[None, :, :, 1],
            state_p[None],
            jnp.swapaxes(nkt_s, 1, 2).reshape(1, nb, WINDOW, KV_HEADS, HEAD_DIM),
            jnp.swapaxes(nvt_s, 1, 2).reshape(1, nb, WINDOW, KV_HEADS, HEAD_DIM),
            state_s[None])
```

```python
import functools

import jax
import jax.numpy as jnp
from jax import lax
from jax.experimental import pallas as pl
from jax.experimental.pallas import tpu as pltpu

F32 = jnp.float32
BF16 = jnp.bfloat16

D_MODEL = 1024
N_META = 16
WINDOW = 128
HEAD_DIM = 64
Q_HEADS = 8
KV_HEADS = 2
D_ATT = Q_HEADS * HEAD_DIM
D_KV = KV_HEADS * HEAD_DIM
HG_HEADS = 4
HG_DK = 128
HG_DV = 128
D_HG = HG_HEADS * HG_DK
HG_CHUNK = 64
D_FF = 4 * D_MODEL
EPS = 1e-6
C_Q = 0
C_KV = C_Q + D_ATT
C_HQ = C_KV + 2 * D_KV
C_HF = C_HQ + D_HG
C_HI = C_HF + D_HG
C_HGATE = C_HI + D_HG
C_GA = C_HGATE + D_HG
C_GB = C_GA + D_MODEL
D_IN = C_GB + D_MODEL

VMEM_LIMIT_BYTES = 56 * 1024 * 1024
MIX_ROWS = 512
PROJ_GROUP = 256
WEIGHT_SLAB_ROWS = 64
MERGE_ROWS = 512
FFN_CHUNK = 2048
LANES = 128
BF16_ROWS = 16
SAMPLE_ATT_GROUP = 16
SAMPLE_HG_GROUP = 8

_NT = (((1,), (1,)), ((), ()))
_TN = (((0,), (0,)), ((), ()))


def _dot(a, b):
    return jnp.dot(a, b, preferred_element_type=F32)


def _dot_nt(a, b):
    return lax.dot_general(a, b, _NT, preferred_element_type=F32)


def _dot_tn(a, b):
    return lax.dot_general(a, b, _TN, preferred_element_type=F32)


def _rows_from_linear(ref):
    per = D_MODEL // LANES
    rows = ref.shape[0] // per
    return jnp.concatenate([ref[pl.ds(c, rows, stride=per), :] for c in range(per)], axis=1)


def _rows_to_linear(ref, value):
    per = D_MODEL // LANES
    for c in range(per):
        ref[pl.ds(c, value.shape[0], stride=per), :] = value[:, c * LANES:(c + 1) * LANES]


def _rmsnorm(x, g):
    return x * lax.rsqrt(jnp.mean(x * x, axis=-1, keepdims=True) + EPS) * g


def _resident(shape):
    return pl.BlockSpec(shape, lambda *_: (0,) * len(shape), pipeline_mode=pl.Buffered(1))


def _params(*sem):
    return pltpu.CompilerParams(dimension_semantics=sem, vmem_limit_bytes=VMEM_LIMIT_BYTES)


def _lower_bound(lbp):
    e = jnp.exp(lbp - jnp.max(lbp, axis=0, keepdims=True))
    return e[0:1] / jnp.sum(e, axis=0, keepdims=True)


class _Projection:
    def __init__(self, xn, lbp, w_ref):
        self.xn = xn
        self.w_ref = w_ref
        self.lb = _lower_bound(lbp)

    def cols(self, base, part):
        return _dot(self.xn[...], self.w_ref[:, base + part.start:base + part.stop])

    def q_att(self, part=slice(0, D_ATT)):
        return self.cols(C_Q, part) * (HEAD_DIM ** -0.5)

    def kv(self):
        return self.cols(C_KV, slice(0, 2 * D_KV))

    def q_hg(self, part=slice(0, D_HG)):
        return self.cols(C_HQ, part) * (HG_DK ** -0.5)

    def forget(self, part=slice(0, D_HG)):
        lb = self.lb[:, part]
        f = lb + (1.0 - lb) * jax.nn.sigmoid(self.cols(C_HF, part))
        return 1.0 - f, jnp.log(f)

    def i_hg(self, part=slice(0, D_HG)):
        return self.cols(C_HI, part)

    def swish_gate(self, part=slice(0, D_HG)):
        g = self.cols(C_HGATE, part)
        return g * jax.nn.sigmoid(g)

    def branch_gate(self, base, part=slice(0, D_MODEL)):
        return jax.nn.sigmoid(self.cols(base, part))


class _Attention:
    def __init__(self, sink_ref, q_ref, kv, first, o_ref):
        self.sink_ref, self.q_ref, self.first, self.o_ref = sink_ref, q_ref, first, o_ref
        self.nsub = q_ref.shape[0] // WINDOW
        self.nk = 2 * WINDOW + N_META
        self.meta0 = WINDOW + q_ref.shape[0]
        lane = lax.broadcasted_iota(jnp.int32, (kv.shape[0], D_KV), 1)
        low = lane < HEAD_DIM
        k = kv[:, :D_KV]
        ksw = pltpu.roll(k, HEAD_DIM, axis=1)
        self.kboth = (jnp.where(low, k, ksw).astype(BF16), jnp.where(low, ksw, k).astype(BF16))
        qlane = lax.broadcasted_iota(jnp.int32, (1, LANES), 1)
        self.keep = (jnp.where(qlane < HEAD_DIM, 1.0, 0.0).astype(BF16),
                     jnp.where(qlane < HEAD_DIM, 0.0, 1.0).astype(BF16))
        r = lax.broadcasted_iota(jnp.int32, (D_KV, D_KV), 0)
        c = lax.broadcasted_iota(jnp.int32, (D_KV, D_KV), 1)
        eye = jnp.where(r == c, 1.0, 0.0).astype(BF16)
        self.vt = _dot_nt(eye, kv[:, D_KV:].astype(BF16)).astype(BF16)
        self.ones = jnp.ones((BF16_ROWS, self.nk), BF16)
        self.key = lax.broadcasted_iota(jnp.int32, (self.nk, 2 * WINDOW), 0)
        col = lax.broadcasted_iota(jnp.int32, (self.nk, 2 * WINDOW), 1)
        self.qry = jnp.bitwise_and(col, WINDOW - 1)
        self.first_head = lax.broadcasted_iota(jnp.int32, (1, 2 * WINDOW), 1) < WINDOW
        self.scores = {}

    def issue_scores(self, i):
        r0 = i * WINDOW
        out = []
        for p in range(Q_HEADS // 2):
            kb = self.kboth[(2 * p) // (Q_HEADS // KV_HEADS)]
            kmat = jnp.concatenate([kb[r0:r0 + 2 * WINDOW], kb[self.meta0:]], axis=0)
            qp = self.q_ref[r0:r0 + WINDOW, p * LANES:(p + 1) * LANES]
            q2 = jnp.concatenate([qp * self.keep[0], qp * self.keep[1]], axis=0)
            out.append(_dot_nt(kmat, q2))
        self.scores[i] = out

    def finish(self, i):
        r0 = i * WINDOW
        lo = jnp.where(self.first, WINDOW - 1, self.qry) if i == 0 else self.qry
        mask = jnp.logical_and(self.key > lo, self.key <= self.qry + WINDOW)
        mask = jnp.logical_or(mask, self.key >= 2 * WINDOW)
        vaug = []
        for h in range(KV_HEADS):
            vth = self.vt[h * HEAD_DIM:(h + 1) * HEAD_DIM]
            vaug.append(jnp.concatenate(
                [jnp.concatenate([vth[:, r0:r0 + 2 * WINDOW], vth[:, self.meta0:]], axis=1),
                 self.ones], axis=0))
        outs = []
        for p, raw in enumerate(self.scores.pop(i)):
            h = (2 * p) // (Q_HEADS // KV_HEADS)
            s = jnp.where(mask, raw, -jnp.inf)
            sk = jnp.where(self.first_head, self.sink_ref[2 * p], self.sink_ref[2 * p + 1])
            m = jnp.maximum(jnp.max(s, axis=0, keepdims=True), sk)
            e = jnp.exp(s - m).astype(BF16)
            oa = _dot(vaug[h], e)
            o = oa[:HEAD_DIM] / (oa[HEAD_DIM:HEAD_DIM + 1] + jnp.exp(sk - m))
            outs += [o[:, :WINDOW], o[:, WINDOW:]]
        self.o_ref[r0:r0 + WINDOW, :] = jnp.concatenate(outs, axis=0).T.astype(self.o_ref.dtype)


def _cumsum_rows(x):
    t = x.shape[0]
    row = lax.broadcasted_iota(jnp.int32, x.shape, 0)
    d = 1
    while d < t:
        x = x + jnp.where(row >= d, pltpu.roll(x, d, axis=0), 0.0)
        d *= 2
    return x


def _hg_out(o, gate, nw):
    parts = []
    for h in range(HG_HEADS):
        oh = o[:, h * HG_DV:(h + 1) * HG_DV]
        parts.append(oh * lax.rsqrt(jnp.mean(oh * oh, axis=-1, keepdims=True) + EPS) * nw)
    return jnp.concatenate(parts, axis=1) * gate


_HG_SLICES = [slice(h * HG_DK, (h + 1) * HG_DK) for h in range(HG_HEADS)]


def _state_update(iv, kd):
    return jnp.concatenate([_dot_tn(iv[:, hs], kd[:, hs]) for hs in _HG_SLICES], axis=1)


def _meta_state(mk, mlogf, mi):
    b = _cumsum_rows(mlogf)
    kd = (mk * jnp.exp(b[-1:] - b)).astype(BF16)
    return _state_update(mi.astype(BF16), kd)


class _Hgrn:
    def __init__(self, q_ref, k_ref, logf_ref, i_ref, g_ref, nw, state, o_ref):
        self.q_ref, self.k_ref, self.logf_ref, self.i_ref, self.g_ref = q_ref, k_ref, logf_ref, i_ref, g_ref
        self.nw, self.state, self.o_ref = nw, state, o_ref
        t = HG_CHUNK
        self.nchunk = q_ref.shape[0] // t
        r = lax.broadcasted_iota(jnp.int32, (t, t), 0)
        c = lax.broadcasted_iota(jnp.int32, (t, t), 1)
        self.causal = r >= c
        self.cums, self.terms, self.entering = {}, {}, {}

    def rows(self, ci):
        return slice(ci * HG_CHUNK, (ci + 1) * HG_CHUNK)

    def issue_cumsum(self, ci):
        self.cums[ci] = _cumsum_rows(self.logf_ref[self.rows(ci), :])

    def issue_local(self, ci):
        rows = self.rows(ci)
        b = self.cums.pop(ci)
        bl = b[-1:]
        k = self.k_ref[rows, :].astype(F32)
        qe = (self.q_ref[rows, :].astype(F32) * jnp.exp(b)).astype(BF16)
        ke = (k * jnp.exp(-b)).astype(BF16)
        kd = (k * jnp.exp(bl - b)).astype(BF16)
        iv = self.i_ref[rows, :].astype(BF16)
        a = [jnp.where(self.causal, _dot_nt(qe[:, hs], ke[:, hs]), 0.0).astype(BF16) for hs in _HG_SLICES]
        self.terms[ci] = (qe, iv, a)
        self.entering[ci] = self.state.astype(BF16)
        self.state = self.state * jnp.exp(bl) + _state_update(iv, kd)

    def finish(self, ci):
        rows = self.rows(ci)
        qe, iv, a = self.terms.pop(ci)
        ent = self.entering.pop(ci)
        outs = [_dot(a[h], iv[:, hs]) + _dot_nt(qe[:, hs], ent[:, hs]) for h, hs in enumerate(_HG_SLICES)]
        o = _hg_out(jnp.concatenate(outs, axis=1), self.g_ref[rows, :].astype(F32), self.nw)
        self.o_ref[rows, :] = o.astype(self.o_ref.dtype)


def _load_as_bf16(w_hbm, w_vmem, stage, sem):
    rows = stage.shape[1]
    nslab = w_hbm.shape[0] // rows

    def slab_copy(c):
        return pltpu.make_async_copy(w_hbm.at[pl.ds(c * rows, rows), :], stage.at[c % 2], sem.at[c % 2])

    slab_copy(0).start()
    for c in range(nslab):
        if c + 1 < nslab:
            slab_copy(c + 1).start()
        slab_copy(c).wait()
        w_vmem[pl.ds(c * rows, rows), :] = stage[c % 2].astype(w_vmem.dtype)


def _mixer_kernel(sink_ref, x0_ref, xnext_ref, xs_ref, xm_ref, g_ref, lbp_ref, w_hbm, nw_ref,
                  wa32, wb32, wo32, wup32, wdn32,
                  att_ref, hg_ref, sga_ref, sgb_ref, sfin_ref, lastkv_ref,
                  qs_ref, kvs_ref, hqs_ref, hks_ref, his_ref, hgs_ref, sgas_ref, sgbs_ref,
                  wa16, wb16, wo16, wup16, wdn16,
                  w_ref, stage, sem, xn_s, zq, zkv, zhq, zhk, zlogf, zhi, zhg, kvm_ref, st_ref, mst_ref,
                  *, tiles_per_seq):
    s = pl.program_id(0)
    t = xnext_ref.shape[0]

    @pl.when(s == 0)
    def _():
        _load_as_bf16(w_hbm, w_ref, stage, sem)
        xn_s[...] = _rmsnorm(x0_ref[...], g_ref[...]).astype(xn_s.dtype)
        for ref in (zq, zkv, zhq, zhk, zlogf, zhi, zhg, st_ref):
            ref[...] = jnp.zeros(ref.shape, ref.dtype)
        xs = _rows_from_linear(xs_ref)
        nb = xs.shape[0]
        small = jnp.concatenate([xs, xm_ref[...]], axis=0)
        p = _Projection(_rmsnorm(small, g_ref[...]).astype(BF16), lbp_ref[...], w_ref)
        q_small = p.q_att()
        low = lax.broadcasted_iota(jnp.int32, (1, LANES), 1) < HEAD_DIM
        for j in range(Q_HEADS):
            kv_head, half = j // (Q_HEADS // KV_HEADS), j % 2
            pair = q_small[:, (j // 2) * LANES:(j // 2 + 1) * LANES]
            moved = pair if half == kv_head else pltpu.roll(pair, HEAD_DIM, axis=1)
            qs_ref[:, j, :] = jnp.where(low if kv_head == 0 else jnp.logical_not(low), moved, 0.0)
        kv = p.kv()
        kvs_ref[...] = kv
        kvm_ref[...] = kv[nb:]
        hqs_ref[...] = p.q_hg()
        k, logf = p.forget()
        hks_ref[...] = k
        iv = p.i_hg()
        his_ref[...] = iv
        hgs_ref[...] = p.swish_gate()
        sgas_ref[...] = p.branch_gate(C_GA).astype(sgas_ref.dtype)
        sgbs_ref[...] = p.branch_gate(C_GB).astype(sgbs_ref.dtype)
        mst_ref[...] = _meta_state(k[nb:], logf[nb:], iv[nb:])

    first = lax.rem(jnp.maximum(s - 1, 0), tiles_per_seq) == 0

    att = _Attention(sink_ref, zq, jnp.concatenate([zkv[...], kvm_ref[...]], axis=0), first, att_ref)
    last = zkv[t:t + WINDOW, :]
    lastkv_ref[...] = last
    zkv[0:WINDOW, :] = last
    entering = jnp.where(first, mst_ref[...], st_ref[...])
    hg = _Hgrn(zhq, zhk, zlogf, zhi, zhg, nw_ref[...], entering, hg_ref)
    proj = _Projection(xn_s, lbp_ref[...], w_ref)

    def gate_a(p):
        sga_ref[:, p] = proj.branch_gate(C_GA, p).astype(sga_ref.dtype)

    def gate_b(p):
        sgb_ref[:, p] = proj.branch_gate(C_GB, p).astype(sgb_ref.dtype)

    def new_kv(_):
        zkv[WINDOW:WINDOW + t, :] = proj.kv()

    def new_forget(p):
        k_new, logf_new = proj.forget(p)
        zhk[:, p] = k_new.astype(zhk.dtype)
        zlogf[:, p] = logf_new

    def new_q(p):
        zq[:, p] = proj.q_att(p).astype(zq.dtype)

    def new_hq(p):
        zhq[:, p] = proj.q_hg(p).astype(zhq.dtype)

    def new_hi(p):
        zhi[:, p] = proj.i_hg(p).astype(zhi.dtype)

    def new_gate(p):
        zhg[:, p] = proj.swish_gate(p).astype(zhg.dtype)

    def parts(n):
        return [slice(c, c + PROJ_GROUP) for c in range(0, n, PROJ_GROUP)]

    groups = ([(gate_a, p) for p in parts(D_MODEL)] + [(gate_b, p) for p in parts(D_MODEL)]
              + [(new_kv, None)] + [(new_q, p) for p in parts(D_ATT)]
              + [(new_forget, p) for p in parts(D_HG)] + [(new_hq, p) for p in parts(D_HG)]
              + [(new_hi, p) for p in parts(D_HG)] + [(new_gate, p) for p in parts(D_HG)])

    def save_state():
        st_ref[...] = hg.state
        for h, hs in enumerate(_HG_SLICES):
            sfin_ref[h] = hg.state[:, hs].T

    qk, cs, loc, fin, out = att.issue_scores, hg.issue_cumsum, hg.issue_local, att.finish, hg.finish
    mixer_work = {
        0: [(qk, 0), (cs, 0), (cs, 1), (cs, 2), (cs, 3)],
        1: [(cs, 4), (cs, 5), (loc, 0)],
        2: [(cs, 6), (cs, 7), (loc, 1)],
        3: [(qk, 1), (loc, 2)],
        4: [(fin, 0), (loc, 3)],
        5: [(out, 0), (loc, 4)],
        6: [(qk, 2), (out, 1), (loc, 5)],
        7: [(fin, 1), (out, 2), (loc, 6)],
        8: [(qk, 3), (out, 3), (loc, 7), (save_state,)],
        9: [(out, 4)],
        10: [(fin, 2), (out, 5)],
        11: [(out, 6)],
        12: [(out, 7)],
        13: [(fin, 3)],
    }
    for slot, (fn, p) in enumerate(groups):
        fn(p)
        for item in mixer_work.get(slot, []):
            item[0](*item[1:])
    xn_s[...] = _rmsnorm(xnext_ref[...], g_ref[...]).astype(xn_s.dtype)
    for src, dst in ((wa32, wa16), (wb32, wb16), (wo32, wo16), (wup32, wup16), (wdn32, wdn16)):
        dst[...] = src[...].astype(dst.dtype)


def _mixers(sinks, x, x_sample, x_meta, g, lb_param, w_f32, hg_norm, later_weights, bsz):
    n = x.shape[0]
    t = MIX_ROWS
    nt = n // t
    per_seq = nt // bsz
    rows = x_sample.shape[0] * LANES // D_MODEL + x_meta.shape[0]

    def this_tile(c):
        return pl.BlockSpec((t, c), lambda s: (jnp.minimum(s, nt - 1), 0))

    def prev_tile(c):
        return pl.BlockSpec((t, c), lambda s: (jnp.maximum(s - 1, 0), 0))

    def prev_seq(shape):
        return pl.BlockSpec((None,) + shape,
                            lambda s: (jnp.maximum(s - 1, 0) // per_seq,) + (0,) * len(shape))

    def small(c):
        return pl.BlockSpec((rows, c), lambda s: (0, 0))

    first_tile = pl.BlockSpec((t, D_MODEL), lambda s: (0, 0), pipeline_mode=pl.Buffered(1))
    next_tile = pl.BlockSpec((t, D_MODEL), lambda s: (jnp.minimum(s + 1, nt - 1), 0))
    small_widths = (2 * D_KV, D_HG, D_HG, D_HG, D_HG, D_MODEL, D_MODEL)
    small_dtypes = (F32, F32, F32, F32, F32, BF16, BF16)
    qm_shape = (rows, Q_HEADS, LANES)
    qm_spec = pl.BlockSpec(qm_shape, lambda s: (0, 0, 0))

    def row_block(wt):
        return pl.BlockSpec((wt.shape[0] // nt, wt.shape[1]), lambda s: (jnp.minimum(s, nt - 1), 0))

    return pl.pallas_call(
        functools.partial(_mixer_kernel, tiles_per_seq=per_seq),
        grid=(nt + 1,),
        in_specs=[pl.BlockSpec(memory_space=pltpu.SMEM), first_tile, next_tile,
                  _resident(x_sample.shape), _resident(x_meta.shape), _resident((1, D_MODEL)),
                  _resident(lb_param.shape), pl.BlockSpec(memory_space=pl.ANY), _resident((1, HG_DV))]
                 + [row_block(wt) for wt in later_weights],
        out_specs=[prev_tile(D_ATT), prev_tile(D_HG), this_tile(D_MODEL), this_tile(D_MODEL),
                   prev_seq((HG_HEADS, HG_DK, HG_DV)), prev_seq((WINDOW, 2 * D_KV))]
                  + [qm_spec] + [small(c) for c in small_widths] + [row_block(wt) for wt in later_weights],
        out_shape=[jax.ShapeDtypeStruct((n, D_ATT), BF16), jax.ShapeDtypeStruct((n, D_HG), BF16),
                   jax.ShapeDtypeStruct((n, D_MODEL), BF16), jax.ShapeDtypeStruct((n, D_MODEL), BF16),
                   jax.ShapeDtypeStruct((bsz, HG_HEADS, HG_DK, HG_DV), F32),
                   jax.ShapeDtypeStruct((bsz, WINDOW, 2 * D_KV), F32)]
                  + [jax.ShapeDtypeStruct(qm_shape, F32)]
                  + [jax.ShapeDtypeStruct((rows, c), d) for c, d in zip(small_widths, small_dtypes)]
                  + [jax.ShapeDtypeStruct(wt.shape, BF16) for wt in later_weights],
        scratch_shapes=[pltpu.VMEM((D_MODEL, D_IN), BF16),
                        pltpu.VMEM((2, WEIGHT_SLAB_ROWS, D_IN), F32),
                        pltpu.SemaphoreType.DMA((2,)),
                        pltpu.VMEM((t, D_MODEL), BF16),
                        pltpu.VMEM((t, D_ATT), BF16), pltpu.VMEM((WINDOW + t, 2 * D_KV), F32),
                        pltpu.VMEM((t, D_HG), BF16), pltpu.VMEM((t, D_HG), BF16),
                        pltpu.VMEM((t, D_HG), F32), pltpu.VMEM((t, D_HG), BF16),
                        pltpu.VMEM((t, D_HG), BF16),
                        pltpu.VMEM((N_META, 2 * D_KV), F32),
                        pltpu.VMEM((HG_DV, D_HG), F32),
                        pltpu.VMEM((HG_DV, D_HG), F32)],
        compiler_params=_params("arbitrary"),
        name="mixers",
    )(sinks, x, x, x_sample, x_meta, g, lb_param, w_f32, hg_norm, *later_weights)


def _merge_ffn_rows(x, att, hg, sga, sgb, wa_ref, wb_ref, wo_ref, ln_ffn, wup_ref, wdn_ref, ln_f):
    ya = _dot(att.astype(BF16), wa_ref[...])
    yb = _dot(hg.astype(BF16), wb_ref[...])
    mix = sga.astype(F32) * ya + sgb.astype(F32) * yb
    h1 = x + _dot(mix.astype(BF16), wo_ref[...])
    xn = _rmsnorm(h1, ln_ffn).astype(BF16)
    acc = jnp.zeros_like(h1)
    for c in range(0, D_FF, FFN_CHUNK):
        u = jnp.maximum(_dot(xn, wup_ref[:, c:c + FFN_CHUNK]), 0.0)
        acc = acc + _dot((u * u).astype(BF16), wdn_ref[c:c + FFN_CHUNK, :])
    return _rmsnorm(h1 + acc, ln_f)


def _sample_hgrn_group(rows, nw, q_ref, k_ref, i_ref, g_ref, s_ref, snew_ref, hg_ref):
    k = k_ref[rows, :]
    q = q_ref[rows, :]
    iv = i_ref[rows, :]
    r = lax.broadcasted_iota(jnp.int32, (HG_DK, HG_DK), 0)
    c = lax.broadcasted_iota(jnp.int32, (HG_DK, HG_DK), 1)
    eye = r == c

    def column(row):
        return jnp.sum(jnp.where(eye, row, 0.0), axis=1, keepdims=True)

    outs = []
    for b in range(s_ref.shape[0]):
        heads = []
        for h, hs in enumerate(_HG_SLICES):
            kc = column(k[b:b + 1, hs])
            qc = column(q[b:b + 1, hs])
            s_old = s_ref[b, h]
            s = s_old + kc * (iv[b:b + 1, hs] - s_old)
            snew_ref[b, h] = s
            heads.append(jnp.sum(qc * s, axis=0, keepdims=True))
        outs.append(jnp.concatenate(heads, axis=1))
    hg_ref[rows, :] = _hg_out(jnp.concatenate(outs, axis=0), g_ref[rows, :], nw)


def _merge_ffn_kernel(x_ref, att_ref, hg_ref, sga_ref, sgb_ref,
                      xs_ref, atts_ref, sgas_ref, sgbs_ref,
                      nw_ref, hqs_ref, hks_ref, his_ref, hgates_ref, state_ref,
                      wa_ref, wb_ref, wo_ref, wup_ref, wdn_ref, lnffn_ref, lnf_ref,
                      y_ref, ys_ref, snew_ref, hgs, *, prompt_steps):
    i = pl.program_id(0)
    weights = (wa_ref, wb_ref, wo_ref, lnffn_ref[...], wup_ref, wdn_ref, lnf_ref[...])
    group = state_ref.shape[0]
    ngroups = atts_ref.shape[0] // group

    @pl.when(i < prompt_steps)
    def _():
        y_ref[...] = _merge_ffn_rows(x_ref[...], att_ref[...], hg_ref[...], sga_ref[...], sgb_ref[...],
                                     *weights)
        rows = pl.ds(pl.multiple_of(jnp.minimum(i, ngroups - 1) * group, group), group)
        _sample_hgrn_group(rows, nw_ref[...], hqs_ref, hks_ref, his_ref, hgates_ref,
                           state_ref, snew_ref, hgs)

    @pl.when(i == prompt_steps)
    def _():
        _rows_to_linear(ys_ref, _merge_ffn_rows(_rows_from_linear(xs_ref), atts_ref[...], hgs[...],
                                                sgas_ref[...], sgbs_ref[...], *weights))


def _merge_ffn(x, att, hg, sga, sgb, xs, att_s, sga_s, sgb_s, hg_norm, hq_s, hk_s, hi_s, hgate_s, state,
               wa, wb, wo, ln_ffn, w_up, w_down, ln_f):
    n = x.shape[0]
    nb = att_s.shape[0]
    linear = pl.BlockSpec(xs.shape, lambda i: (0, 0))
    rows = MERGE_ROWS
    nt = n // rows
    g = SAMPLE_HG_GROUP
    assert nb // g <= nt

    def blk(c):
        return pl.BlockSpec((rows, c), lambda i: (jnp.minimum(i, nt - 1), 0))

    def sample(c):
        return pl.BlockSpec((nb, c), lambda i: (0, 0))

    sblk = pl.BlockSpec((g, HG_HEADS, HG_DK, HG_DV), lambda i: (jnp.minimum(i, nb // g - 1), 0, 0, 0))
    return pl.pallas_call(
        functools.partial(_merge_ffn_kernel, prompt_steps=nt),
        grid=(nt + 1,),
        in_specs=[blk(D_MODEL), blk(D_ATT), blk(D_HG), blk(D_MODEL), blk(D_MODEL),
                  linear, sample(D_ATT), sample(D_MODEL), sample(D_MODEL),
                  _resident((1, HG_DV)), _resident(hq_s.shape), _resident(hk_s.shape),
                  _resident(hi_s.shape), _resident(hgate_s.shape), sblk,
                  _resident(wa.shape), _resident(wb.shape), _resident(wo.shape),
                  _resident(w_up.shape), _resident(w_down.shape),
                  _resident((1, D_MODEL)), _resident((1, D_MODEL))],
        out_specs=[blk(D_MODEL), linear, sblk],
        out_shape=[jax.ShapeDtypeStruct((n, D_MODEL), F32), jax.ShapeDtypeStruct(xs.shape, F32),
                   jax.ShapeDtypeStruct(state.shape, F32)],
        scratch_shapes=[pltpu.VMEM((nb, D_HG), F32)],
        compiler_params=_params("arbitrary"),
        name="merge_ffn",
    )(x, att, hg, sga, sgb, xs, att_s, sga_s, sgb_s, hg_norm, hq_s, hk_s, hi_s, hgate_s, state,
      wa, wb, wo, w_up, w_down, ln_ffn, ln_f)


def _sample_attn_kernel(sink_ref, qm_ref, kvn_ref, ckt_ref, cvt_ref, mkt_ref, mvt_ref,
                        att_ref, nkt_ref, nvt_ref, o_all, mk_s, mv_s):
    nb = qm_ref.shape[0]
    first_seq = pl.multiple_of(pl.program_id(0) * nb, nb)

    @pl.when(pl.program_id(0) == 0)
    def _():
        for r in range(N_META):
            mk_s[:, r, :] = mkt_ref[r].T
            mv_s[:, r, :] = mvt_ref[r].T
    head = lax.broadcasted_iota(jnp.int32, (Q_HEADS, 1), 0)
    sk = jnp.zeros((Q_HEADS, 1), F32)
    for j in range(Q_HEADS):
        sk = jnp.where(head == j, sink_ref[j], sk)
    newest = lax.broadcasted_iota(jnp.int32, (D_KV, WINDOW), 1) == WINDOW - 1
    kvn = kvn_ref[...]
    hi = kvn.astype(BF16)
    rest = kvn - hi.astype(F32)
    mid = rest.astype(BF16)
    lo = (rest - mid.astype(F32)).astype(BF16)
    r = lax.broadcasted_iota(jnp.int32, (2 * D_KV, 2 * D_KV), 0)
    c = lax.broadcasted_iota(jnp.int32, (2 * D_KV, 2 * D_KV), 1)
    eye = jnp.where(r == c, 1.0, 0.0).astype(BF16)
    pieces_t = _dot_nt(eye, jnp.concatenate([hi, mid, lo], axis=0)).astype(BF16)
    pr = lax.broadcasted_iota(jnp.int32, (3 * nb, WINDOW), 0)
    pc = lax.broadcasted_iota(jnp.int32, (3 * nb, WINDOW), 1)
    for b in range(nb):
        pick = jnp.logical_and(lax.rem(pr, nb) == b, pc == WINDOW - 1)
        new_cols = _dot(pieces_t, jnp.where(pick, 1.0, 0.0).astype(BF16))
        nkt_ref[b] = jnp.where(newest, new_cols[:D_KV], pltpu.roll(ckt_ref[b], WINDOW - 1, axis=1))
        nvt_ref[b] = jnp.where(newest, new_cols[D_KV:], pltpu.roll(cvt_ref[b], WINDOW - 1, axis=1))
    qms = [qm_ref[b].astype(BF16) for b in range(nb)]
    scores = [(_dot(qms[b], nkt_ref[b].astype(BF16)),
               _dot_nt(qms[b], mk_s[first_seq + b].astype(BF16)))
              for b in range(nb)]
    for b, (s_w, s_m) in enumerate(scores):
        m = jnp.maximum(jnp.maximum(jnp.max(s_w, axis=1, keepdims=True),
                                    jnp.max(s_m, axis=1, keepdims=True)), sk)
        e_w = jnp.exp(s_w - m)
        e_m = jnp.exp(s_m - m)
        l = (jnp.sum(e_w, axis=1, keepdims=True) + jnp.sum(e_m, axis=1, keepdims=True)
             + jnp.exp(sk - m))
        o = (_dot_nt(e_w.astype(BF16), nvt_ref[b].astype(BF16))
             + _dot(e_m.astype(BF16), mv_s[first_seq + b].astype(BF16)))
        o_all[b] = o / l
    low = lax.broadcasted_iota(jnp.int32, (1, LANES), 1) < HEAD_DIM
    pairs = []
    for p in range(Q_HEADS // 2):
        halves = []
        for half in range(2):
            j = 2 * p + half
            kv_head = j // (Q_HEADS // KV_HEADS)
            oj = jnp.where(low if kv_head == 0 else jnp.logical_not(low), o_all[:, j, :], 0.0)
            halves.append(oj if half == kv_head else pltpu.roll(oj, HEAD_DIM, axis=1))
        pairs.append(halves[0] + halves[1])
    att_ref[...] = jnp.concatenate(pairs, axis=1)


def _sample_attention(sinks, qm, kv_new, ck, cv, mk, mv):
    nb = ck.shape[0]
    g = SAMPLE_ATT_GROUP

    def blk3(a, c):
        return pl.BlockSpec((g, a, c), lambda i: (i, 0, 0))

    return pl.pallas_call(
        _sample_attn_kernel,
        grid=(nb // g,),
        in_specs=[pl.BlockSpec(memory_space=pltpu.SMEM), blk3(Q_HEADS, D_KV),
                  pl.BlockSpec((g, 2 * D_KV), lambda i: (i, 0)),
                  blk3(D_KV, WINDOW), blk3(D_KV, WINDOW), _resident(mk.shape), _resident(mv.shape)],
        out_specs=[pl.BlockSpec((g, D_ATT), lambda i: (i, 0)), blk3(D_KV, WINDOW), blk3(D_KV, WINDOW)],
        out_shape=[jax.ShapeDtypeStruct((nb, D_ATT), F32),
                   jax.ShapeDtypeStruct((nb, D_KV, WINDOW), F32),
                   jax.ShapeDtypeStruct((nb, D_KV, WINDOW), F32)],
        scratch_shapes=[pltpu.VMEM((g, Q_HEADS, LANES), F32),
                        pltpu.VMEM((nb, N_META, D_KV), F32), pltpu.VMEM((nb, N_META, D_KV), F32)],
        compiler_params=_params("arbitrary"),
        name="sample_attn",
    )(sinks, qm, kv_new, ck, cv, mk, mv)


def kernel(x_prompt, x_sample, cache_k, cache_v, cache_meta_k, cache_meta_v, state_hgrn, meta,
           w_in, sinks, lb_param, hg_norm, w_att_out, w_hg_out, w_o, ln_mix, ln_ffn, w_up,
           w_down, ln_f):
    bsz, seq, _ = x_prompt.shape
    nb = x_sample.shape[0]
    ln_mix2 = ln_mix.reshape(1, D_MODEL)
    ln_ffn2 = ln_ffn.reshape(1, D_MODEL)
    ln_f2 = ln_f.reshape(1, D_MODEL)
    nw = hg_norm.reshape(1, HG_DV)

    xs = x_sample.reshape(nb * D_MODEL // LANES, LANES)

    xp = x_prompt.reshape(bsz * seq, D_MODEL)
    (att_p, hg_p, sga_p, sgb_p, state_p, lastkv_p,
     qm_s, kv_s, hq_s, hk_s, hi_s, hgate_s, sga_s, sgb_s,
     wa, wb, wo, wup, wdn) = _mixers(
        sinks.reshape(Q_HEADS), xp, xs, meta, ln_mix2, lb_param, w_in[0], nw,
        (w_att_out[0], w_hg_out[0], w_o[0], w_up[0], w_down[0]), bsz)

    def window_t(c):
        return jnp.swapaxes(c[0].reshape(nb, WINDOW, D_KV), 1, 2)

    def meta_t(c):
        return jnp.transpose(c[0].reshape(nb, N_META, D_KV), (1, 2, 0))

    att_s, nkt_s, nvt_s = _sample_attention(
        sinks.reshape(Q_HEADS), qm_s, kv_s, window_t(cache_k), window_t(cache_v),
        meta_t(cache_meta_k), meta_t(cache_meta_v))

    y_p, y_s, state_s = _merge_ffn(xp, att_p, hg_p, sga_p, sgb_p, xs, att_s, sga_s, sgb_s,
                                   nw, hq_s, hk_s, hi_s, hgate_s, state_hgrn[0],
                                   wa, wb, wo, ln_ffn2, wup, wdn, ln_f2)

    kv5 = lastkv_p.reshape(bsz, WINDOW, 2, KV_HEADS, HEAD_DIM)
    meta5 = jnp.broadcast_to(kv_s[nb:].reshape(1, N_META, 2, KV_HEADS, HEAD_DIM),
                             (bsz, N_META, 2, KV_HEADS, HEAD_DIM))
    return (y_p.reshape(bsz, seq, D_MODEL),
            y_s.reshape(nb, 1, D_MODEL),
            kv5[None, :, :, 0],
            kv5[None, :, :, 1],
            meta5[None, :, :, 0],
            meta5[None, :, :, 1],
            state_p[None],
            jnp.swapaxes(nkt_s, 1, 2).reshape(1, nb, WINDOW, KV_HEADS, HEAD_DIM),
            jnp.swapaxes(nvt_s, 1, 2).reshape(1, nb, WINDOW, KV_HEADS, HEAD_DIM),
            state_s[None])
```

```python
import functools

import jax
import jax.numpy as jnp
from jax import lax
from jax.experimental import pallas as pl
from jax.experimental.pallas import tpu as pltpu

F32 = jnp.float32
BF16 = jnp.bfloat16

D_MODEL = 1024
N_META = 16
WINDOW = 128
HEAD_DIM = 64
Q_HEADS = 8
KV_HEADS = 2
D_ATT = Q_HEADS * HEAD_DIM
D_KV = KV_HEADS * HEAD_DIM
HG_HEADS = 4
HG_DK = 128
HG_DV = 128
D_HG = HG_HEADS * HG_DK
HG_CHUNK = 64
D_FF = 4 * D_MODEL
EPS = 1e-6
C_Q = 0
C_KV = C_Q + D_ATT
C_HQ = C_KV + 2 * D_KV
C_HF = C_HQ + D_HG
C_HI = C_HF + D_HG
C_HGATE = C_HI + D_HG
C_GA = C_HGATE + D_HG
C_GB = C_GA + D_MODEL
D_IN = C_GB + D_MODEL

VMEM_LIMIT_BYTES = 56 * 1024 * 1024
MIX_ROWS = 512
PROJ_GROUP = 256
WEIGHT_SLAB_ROWS = 64
MERGE_ROWS = 512
FFN_CHUNK = 2048
LANES = 128
BF16_ROWS = 16
SAMPLE_ATT_GROUP = 32
SAMPLE_HG_GROUP = 8

_NT = (((1,), (1,)), ((), ()))
_TN = (((0,), (0,)), ((), ()))


def _dot(a, b):
    return jnp.dot(a, b, preferred_element_type=F32)


def _dot_nt(a, b):
    return lax.dot_general(a, b, _NT, preferred_element_type=F32)


def _dot_tn(a, b):
    return lax.dot_general(a, b, _TN, preferred_element_type=F32)


def _rows_from_linear(ref):
    per = D_MODEL // LANES
    rows = ref.shape[0] // per
    return jnp.concatenate([ref[pl.ds(c, rows, stride=per), :] for c in range(per)], axis=1)


def _rows_to_linear(ref, value):
    per = D_MODEL // LANES
    for c in range(per):
        ref[pl.ds(c, value.shape[0], stride=per), :] = value[:, c * LANES:(c + 1) * LANES]


def _rmsnorm(x, g):
    return x * lax.rsqrt(jnp.mean(x * x, axis=-1, keepdims=True) + EPS) * g


def _resident(shape):
    return pl.BlockSpec(shape, lambda *_: (0,) * len(shape), pipeline_mode=pl.Buffered(1))


def _params(*sem):
    return pltpu.CompilerParams(dimension_semantics=sem, vmem_limit_bytes=VMEM_LIMIT_BYTES)


def _lower_bound(lbp):
    e = jnp.exp(lbp - jnp.max(lbp, axis=0, keepdims=True))
    return e[0:1] / jnp.sum(e, axis=0, keepdims=True)


class _Projection:
    def __init__(self, xn, lbp, w_ref):
        self.xn = xn
        self.w_ref = w_ref
        self.lb = _lower_bound(lbp)

    def cols(self, base, part):
        return _dot(self.xn[...], self.w_ref[:, base + part.start:base + part.stop])

    def q_att(self, part=slice(0, D_ATT)):
        return self.cols(C_Q, part) * (HEAD_DIM ** -0.5)

    def kv(self):
        return self.cols(C_KV, slice(0, 2 * D_KV))

    def q_hg(self, part=slice(0, D_HG)):
        return self.cols(C_HQ, part) * (HG_DK ** -0.5)

    def forget(self, part=slice(0, D_HG)):
        lb = self.lb[:, part]
        f = lb + (1.0 - lb) * jax.nn.sigmoid(self.cols(C_HF, part))
        return 1.0 - f, jnp.log(f)

    def i_hg(self, part=slice(0, D_HG)):
        return self.cols(C_HI, part)

    def swish_gate(self, part=slice(0, D_HG)):
        g = self.cols(C_HGATE, part)
        return g * jax.nn.sigmoid(g)

    def branch_gate(self, base, part=slice(0, D_MODEL)):
        return jax.nn.sigmoid(self.cols(base, part))


class _Attention:
    def __init__(self, sink_ref, q_ref, kv, first, o_ref):
        self.sink_ref, self.q_ref, self.first, self.o_ref = sink_ref, q_ref, first, o_ref
        self.nsub = q_ref.shape[0] // WINDOW
        self.nk = 2 * WINDOW + N_META
        self.meta0 = WINDOW + q_ref.shape[0]
        lane = lax.broadcasted_iota(jnp.int32, (kv.shape[0], D_KV), 1)
        low = lane < HEAD_DIM
        k = kv[:, :D_KV]
        ksw = pltpu.roll(k, HEAD_DIM, axis=1)
        self.kboth = (jnp.where(low, k, ksw).astype(BF16), jnp.where(low, ksw, k).astype(BF16))
        qlane = lax.broadcasted_iota(jnp.int32, (1, LANES), 1)
        self.keep = (jnp.where(qlane < HEAD_DIM, 1.0, 0.0).astype(BF16),
                     jnp.where(qlane < HEAD_DIM, 0.0, 1.0).astype(BF16))
        r = lax.broadcasted_iota(jnp.int32, (D_KV, D_KV), 0)
        c = lax.broadcasted_iota(jnp.int32, (D_KV, D_KV), 1)
        eye = jnp.where(r == c, 1.0, 0.0).astype(BF16)
        self.vt = _dot_nt(eye, kv[:, D_KV:].astype(BF16)).astype(BF16)
        self.ones = jnp.ones((BF16_ROWS, self.nk), BF16)
        self.key = lax.broadcasted_iota(jnp.int32, (self.nk, 2 * WINDOW), 0)
        col = lax.broadcasted_iota(jnp.int32, (self.nk, 2 * WINDOW), 1)
        self.qry = jnp.bitwise_and(col, WINDOW - 1)
        self.first_head = lax.broadcasted_iota(jnp.int32, (1, 2 * WINDOW), 1) < WINDOW
        self.scores = {}

    def issue_scores(self, i):
        r0 = i * WINDOW
        out = []
        for p in range(Q_HEADS // 2):
            kb = self.kboth[(2 * p) // (Q_HEADS // KV_HEADS)]
            kmat = jnp.concatenate([kb[r0:r0 + 2 * WINDOW], kb[self.meta0:]], axis=0)
            qp = self.q_ref[r0:r0 + WINDOW, p * LANES:(p + 1) * LANES]
            q2 = jnp.concatenate([qp * self.keep[0], qp * self.keep[1]], axis=0)
            out.append(_dot_nt(kmat, q2))
        self.scores[i] = out

    def finish(self, i):
        r0 = i * WINDOW
        lo = jnp.where(self.first, WINDOW - 1, self.qry) if i == 0 else self.qry
        mask = jnp.logical_and(self.key > lo, self.key <= self.qry + WINDOW)
        mask = jnp.logical_or(mask, self.key >= 2 * WINDOW)
        vaug = []
        for h in range(KV_HEADS):
            vth = self.vt[h * HEAD_DIM:(h + 1) * HEAD_DIM]
            vaug.append(jnp.concatenate(
                [jnp.concatenate([vth[:, r0:r0 + 2 * WINDOW], vth[:, self.meta0:]], axis=1),
                 self.ones], axis=0))
        outs = []
        for p, raw in enumerate(self.scores.pop(i)):
            h = (2 * p) // (Q_HEADS // KV_HEADS)
            s = jnp.where(mask, raw, -jnp.inf)
            sk = jnp.where(self.first_head, self.sink_ref[2 * p], self.sink_ref[2 * p + 1])
            m = jnp.maximum(jnp.max(s, axis=0, keepdims=True), sk)
            e = jnp.exp(s - m).astype(BF16)
            oa = _dot(vaug[h], e)
            o = oa[:HEAD_DIM] / (oa[HEAD_DIM:HEAD_DIM + 1] + jnp.exp(sk - m))
            outs += [o[:, :WINDOW], o[:, WINDOW:]]
        self.o_ref[r0:r0 + WINDOW, :] = jnp.concatenate(outs, axis=0).T.astype(self.o_ref.dtype)


def _cumsum_rows(x):
    t = x.shape[0]
    row = lax.broadcasted_iota(jnp.int32, x.shape, 0)
    d = 1
    while d < t:
        x = x + jnp.where(row >= d, pltpu.roll(x, d, axis=0), 0.0)
        d *= 2
    return x


def _hg_out(o, gate, nw):
    parts = []
    for h in range(HG_HEADS):
        oh = o[:, h * HG_DV:(h + 1) * HG_DV]
        parts.append(oh * lax.rsqrt(jnp.mean(oh * oh, axis=-1, keepdims=True) + EPS) * nw)
    return jnp.concatenate(parts, axis=1) * gate


_HG_SLICES = [slice(h * HG_DK, (h + 1) * HG_DK) for h in range(HG_HEADS)]


def _state_update(iv, kd):
    return jnp.concatenate([_dot_tn(iv[:, hs], kd[:, hs]) for hs in _HG_SLICES], axis=1)


def _meta_state(mk, mlogf, mi):
    b = _cumsum_rows(mlogf)
    kd = (mk * jnp.exp(b[-1:] - b)).astype(BF16)
    return _state_update(mi.astype(BF16), kd)


class _Hgrn:
    def __init__(self, q_ref, k_ref, logf_ref, i_ref, g_ref, nw, state, o_ref):
        self.q_ref, self.k_ref, self.logf_ref, self.i_ref, self.g_ref = q_ref, k_ref, logf_ref, i_ref, g_ref
        self.nw, self.state, self.o_ref = nw, state, o_ref
        t = HG_CHUNK
        self.nchunk = q_ref.shape[0] // t
        r = lax.broadcasted_iota(jnp.int32, (t, t), 0)
        c = lax.broadcasted_iota(jnp.int32, (t, t), 1)
        self.causal = r >= c
        self.cums, self.terms, self.entering = {}, {}, {}

    def rows(self, ci):
        return slice(ci * HG_CHUNK, (ci + 1) * HG_CHUNK)

    def issue_cumsum(self, ci):
        self.cums[ci] = _cumsum_rows(self.logf_ref[self.rows(ci), :])

    def issue_local(self, ci):
        rows = self.rows(ci)
        b = self.cums.pop(ci)
        bl = b[-1:]
        k = self.k_ref[rows, :].astype(F32)
        qe = (self.q_ref[rows, :].astype(F32) * jnp.exp(b)).astype(BF16)
        ke = (k * jnp.exp(-b)).astype(BF16)
        kd = (k * jnp.exp(bl - b)).astype(BF16)
        iv = self.i_ref[rows, :].astype(BF16)
        a = [jnp.where(self.causal, _dot_nt(qe[:, hs], ke[:, hs]), 0.0).astype(BF16) for hs in _HG_SLICES]
        self.terms[ci] = (qe, iv, a)
        self.entering[ci] = self.state.astype(BF16)
        self.state = self.state * jnp.exp(bl) + _state_update(iv, kd)

    def finish(self, ci):
        rows = self.rows(ci)
        qe, iv, a = self.terms.pop(ci)
        ent = self.entering.pop(ci)
        outs = [_dot(a[h], iv[:, hs]) + _dot_nt(qe[:, hs], ent[:, hs]) for h, hs in enumerate(_HG_SLICES)]
        o = _hg_out(jnp.concatenate(outs, axis=1), self.g_ref[rows, :].astype(F32), self.nw)
        self.o_ref[rows, :] = o.astype(self.o_ref.dtype)


def _load_as_bf16(w_hbm, w_vmem, stage, sem):
    rows = stage.shape[1]
    nslab = w_hbm.shape[0] // rows

    def slab_copy(c):
        return pltpu.make_async_copy(w_hbm.at[pl.ds(c * rows, rows), :], stage.at[c % 2], sem.at[c % 2])

    slab_copy(0).start()
    for c in range(nslab):
        if c + 1 < nslab:
            slab_copy(c + 1).start()
        slab_copy(c).wait()
        w_vmem[pl.ds(c * rows, rows), :] = stage[c % 2].astype(w_vmem.dtype)


def _mixer_kernel(sink_ref, x0_ref, xnext_ref, xs_ref, xm_ref, g_ref, lbp_ref, w_hbm, nw_ref,
                  wa32, wb32, wo32, wup32, wdn32,
                  att_ref, hg_ref, sga_ref, sgb_ref, sfin_ref, lastkv_ref,
                  qs_ref, kvs_ref, hqs_ref, hks_ref, his_ref, hgs_ref, sgas_ref, sgbs_ref,
                  wa16, wb16, wo16, wup16, wdn16,
                  w_ref, stage, sem, xn_s, zq, zkv, zhq, zhk, zlogf, zhi, zhg, kvm_ref, st_ref, mst_ref,
                  *, tiles_per_seq):
    s = pl.program_id(0)
    t = xnext_ref.shape[0]

    @pl.when(s == 0)
    def _():
        _load_as_bf16(w_hbm, w_ref, stage, sem)
        xn_s[...] = _rmsnorm(x0_ref[...], g_ref[...]).astype(xn_s.dtype)
        for ref in (zq, zkv, zhq, zhk, zlogf, zhi, zhg, st_ref):
            ref[...] = jnp.zeros(ref.shape, ref.dtype)
        xs = _rows_from_linear(xs_ref)
        nb = xs.shape[0]
        small = jnp.concatenate([xs, xm_ref[...]], axis=0)
        p = _Projection(_rmsnorm(small, g_ref[...]).astype(BF16), lbp_ref[...], w_ref)
        q_small = p.q_att()
        low = lax.broadcasted_iota(jnp.int32, (1, LANES), 1) < HEAD_DIM
        for j in range(Q_HEADS):
            kv_head, half = j // (Q_HEADS // KV_HEADS), j % 2
            pair = q_small[:, (j // 2) * LANES:(j // 2 + 1) * LANES]
            moved = pair if half == kv_head else pltpu.roll(pair, HEAD_DIM, axis=1)
            qs_ref[:, j, :] = jnp.where(low if kv_head == 0 else jnp.logical_not(low), moved, 0.0)
        kv = p.kv()
        kvs_ref[...] = kv
        kvm_ref[...] = kv[nb:]
        hqs_ref[...] = p.q_hg()
        k, logf = p.forget()
        hks_ref[...] = k
        iv = p.i_hg()
        his_ref[...] = iv
        hgs_ref[...] = p.swish_gate()
        sgas_ref[...] = p.branch_gate(C_GA).astype(sgas_ref.dtype)
        sgbs_ref[...] = p.branch_gate(C_GB).astype(sgbs_ref.dtype)
        mst_ref[...] = _meta_state(k[nb:], logf[nb:], iv[nb:])

    first = lax.rem(jnp.maximum(s - 1, 0), tiles_per_seq) == 0

    att = _Attention(sink_ref, zq, jnp.concatenate([zkv[...], kvm_ref[...]], axis=0), first, att_ref)
    last = zkv[t:t + WINDOW, :]
    lastkv_ref[...] = last
    zkv[0:WINDOW, :] = last
    entering = jnp.where(first, mst_ref[...], st_ref[...])
    hg = _Hgrn(zhq, zhk, zlogf, zhi, zhg, nw_ref[...], entering, hg_ref)
    proj = _Projection(xn_s, lbp_ref[...], w_ref)

    def gate_a(p):
        sga_ref[:, p] = proj.branch_gate(C_GA, p).astype(sga_ref.dtype)

    def gate_b(p):
        sgb_ref[:, p] = proj.branch_gate(C_GB, p).astype(sgb_ref.dtype)

    def new_kv(_):
        zkv[WINDOW:WINDOW + t, :] = proj.kv()

    def new_forget(p):
        k_new, logf_new = proj.forget(p)
        zhk[:, p] = k_new.astype(zhk.dtype)
        zlogf[:, p] = logf_new

    def new_q(p):
        zq[:, p] = proj.q_att(p).astype(zq.dtype)

    def new_hq(p):
        zhq[:, p] = proj.q_hg(p).astype(zhq.dtype)

    def new_hi(p):
        zhi[:, p] = proj.i_hg(p).astype(zhi.dtype)

    def new_gate(p):
        zhg[:, p] = proj.swish_gate(p).astype(zhg.dtype)

    def parts(n):
        return [slice(c, c + PROJ_GROUP) for c in range(0, n, PROJ_GROUP)]

    groups = ([(gate_a, p) for p in parts(D_MODEL)] + [(gate_b, p) for p in parts(D_MODEL)]
              + [(new_kv, None)] + [(new_q, p) for p in parts(D_ATT)]
              + [(new_forget, p) for p in parts(D_HG)] + [(new_hq, p) for p in parts(D_HG)]
              + [(new_hi, p) for p in parts(D_HG)] + [(new_gate, p) for p in parts(D_HG)])

    def save_state():
        st_ref[...] = hg.state
        for h, hs in enumerate(_HG_SLICES):
            sfin_ref[h] = hg.state[:, hs].T

    qk, cs, loc, fin, out = att.issue_scores, hg.issue_cumsum, hg.issue_local, att.finish, hg.finish
    mixer_work = {
        0: [(qk, 0), (cs, 0), (cs, 1), (cs, 2), (cs, 3)],
        1: [(cs, 4), (cs, 5), (loc, 0)],
        2: [(cs, 6), (cs, 7), (loc, 1)],
        3: [(qk, 1), (loc, 2)],
        4: [(fin, 0), (loc, 3)],
        5: [(out, 0), (loc, 4)],
        6: [(qk, 2), (out, 1), (loc, 5)],
        7: [(fin, 1), (out, 2), (loc, 6)],
        8: [(qk, 3), (out, 3), (loc, 7), (save_state,)],
        9: [(out, 4)],
        10: [(fin, 2), (out, 5)],
        11: [(out, 6)],
        12: [(out, 7)],
        13: [(fin, 3)],
    }
    for slot, (fn, p) in enumerate(groups):
        fn(p)
        for item in mixer_work.get(slot, []):
            item[0](*item[1:])
    xn_s[...] = _rmsnorm(xnext_ref[...], g_ref[...]).astype(xn_s.dtype)
    for src, dst in ((wa32, wa16), (wb32, wb16), (wo32, wo16), (wup32, wup16), (wdn32, wdn16)):
        dst[...] = src[...].astype(dst.dtype)


def _mixers(sinks, x, x_sample, x_meta, g, lb_param, w_f32, hg_norm, later_weights, bsz):
    n = x.shape[0]
    t = MIX_ROWS
    nt = n // t
    per_seq = nt // bsz
    rows = x_sample.shape[0] * LANES // D_MODEL + x_meta.shape[0]

    def this_tile(c):
        return pl.BlockSpec((t, c), lambda s: (jnp.minimum(s, nt - 1), 0))

    def prev_tile(c):
        return pl.BlockSpec((t, c), lambda s: (jnp.maximum(s - 1, 0), 0))

    def prev_seq(shape):
        return pl.BlockSpec((None,) + shape,
                            lambda s: (jnp.maximum(s - 1, 0) // per_seq,) + (0,) * len(shape))

    def small(c):
        return pl.BlockSpec((rows, c), lambda s: (0, 0))

    first_tile = pl.BlockSpec((t, D_MODEL), lambda s: (0, 0), pipeline_mode=pl.Buffered(1))
    next_tile = pl.BlockSpec((t, D_MODEL), lambda s: (jnp.minimum(s + 1, nt - 1), 0))
    small_widths = (2 * D_KV, D_HG, D_HG, D_HG, D_HG, D_MODEL, D_MODEL)
    small_dtypes = (F32, F32, F32, F32, F32, BF16, BF16)
    qm_shape = (rows, Q_HEADS, LANES)
    qm_spec = pl.BlockSpec(qm_shape, lambda s: (0, 0, 0))

    def row_block(wt):
        return pl.BlockSpec((wt.shape[0] // nt, wt.shape[1]), lambda s: (jnp.minimum(s, nt - 1), 0))

    return pl.pallas_call(
        functools.partial(_mixer_kernel, tiles_per_seq=per_seq),
        grid=(nt + 1,),
        in_specs=[pl.BlockSpec(memory_space=pltpu.SMEM), first_tile, next_tile,
                  _resident(x_sample.shape), _resident(x_meta.shape), _resident((1, D_MODEL)),
                  _resident(lb_param.shape), pl.BlockSpec(memory_space=pl.ANY), _resident((1, HG_DV))]
                 + [row_block(wt) for wt in later_weights],
        out_specs=[prev_tile(D_ATT), prev_tile(D_HG), this_tile(D_MODEL), this_tile(D_MODEL),
                   prev_seq((HG_HEADS, HG_DK, HG_DV)), prev_seq((WINDOW, 2 * D_KV))]
                  + [qm_spec] + [small(c) for c in small_widths] + [row_block(wt) for wt in later_weights],
        out_shape=[jax.ShapeDtypeStruct((n, D_ATT), BF16), jax.ShapeDtypeStruct((n, D_HG), BF16),
                   jax.ShapeDtypeStruct((n, D_MODEL), BF16), jax.ShapeDtypeStruct((n, D_MODEL), BF16),
                   jax.ShapeDtypeStruct((bsz, HG_HEADS, HG_DK, HG_DV), F32),
                   jax.ShapeDtypeStruct((bsz, WINDOW, 2 * D_KV), F32)]
                  + [jax.ShapeDtypeStruct(qm_shape, F32)]
                  + [jax.ShapeDtypeStruct((rows, c), d) for c, d in zip(small_widths, small_dtypes)]
                  + [jax.ShapeDtypeStruct(wt.shape, BF16) for wt in later_weights],
        scratch_shapes=[pltpu.VMEM((D_MODEL, D_IN), BF16),
                        pltpu.VMEM((2, WEIGHT_SLAB_ROWS, D_IN), F32),
                        pltpu.SemaphoreType.DMA((2,)),
                        pltpu.VMEM((t, D_MODEL), BF16),
                        pltpu.VMEM((t, D_ATT), BF16), pltpu.VMEM((WINDOW + t, 2 * D_KV), F32),
                        pltpu.VMEM((t, D_HG), BF16), pltpu.VMEM((t, D_HG), BF16),
                        pltpu.VMEM((t, D_HG), F32), pltpu.VMEM((t, D_HG), BF16),
                        pltpu.VMEM((t, D_HG), BF16),
                        pltpu.VMEM((N_META, 2 * D_KV), F32),
                        pltpu.VMEM((HG_DV, D_HG), F32),
                        pltpu.VMEM((HG_DV, D_HG), F32)],
        compiler_params=_params("arbitrary"),
        name="mixers",
    )(sinks, x, x, x_sample, x_meta, g, lb_param, w_f32, hg_norm, *later_weights)


def _merge_ffn_rows(x, att, hg, sga, sgb, wa_ref, wb_ref, wo_ref, ln_ffn, wup_ref, wdn_ref, ln_f):
    ya = _dot(att.astype(BF16), wa_ref[...])
    yb = _dot(hg.astype(BF16), wb_ref[...])
    mix = sga.astype(F32) * ya + sgb.astype(F32) * yb
    h1 = x + _dot(mix.astype(BF16), wo_ref[...])
    xn = _rmsnorm(h1, ln_ffn).astype(BF16)
    acc = jnp.zeros_like(h1)
    for c in range(0, D_FF, FFN_CHUNK):
        u = jnp.maximum(_dot(xn, wup_ref[:, c:c + FFN_CHUNK]), 0.0)
        acc = acc + _dot((u * u).astype(BF16), wdn_ref[c:c + FFN_CHUNK, :])
    return _rmsnorm(h1 + acc, ln_f)


def _sample_hgrn_group(rows, nw, q_ref, k_ref, i_ref, g_ref, s_ref, snew_ref, hg_ref):
    k = k_ref[rows, :]
    q = q_ref[rows, :]
    iv = i_ref[rows, :]
    r = lax.broadcasted_iota(jnp.int32, (HG_DK, HG_DK), 0)
    c = lax.broadcasted_iota(jnp.int32, (HG_DK, HG_DK), 1)
    eye = r == c

    def column(row):
        return jnp.sum(jnp.where(eye, row, 0.0), axis=1, keepdims=True)

    outs = []
    for b in range(s_ref.shape[0]):
        heads = []
        for h, hs in enumerate(_HG_SLICES):
            kc = column(k[b:b + 1, hs])
            qc = column(q[b:b + 1, hs])
            s_old = s_ref[b, h]
            s = s_old + kc * (iv[b:b + 1, hs] - s_old)
            snew_ref[b, h] = s
            heads.append(jnp.sum(qc * s, axis=0, keepdims=True))
        outs.append(jnp.concatenate(heads, axis=1))
    hg_ref[rows, :] = _hg_out(jnp.concatenate(outs, axis=0), g_ref[rows, :], nw)


def _merge_ffn_kernel(x_ref, att_ref, hg_ref, sga_ref, sgb_ref,
                      xs_ref, atts_ref, sgas_ref, sgbs_ref,
                      nw_ref, hqs_ref, hks_ref, his_ref, hgates_ref, state_ref,
                      wa_ref, wb_ref, wo_ref, wup_ref, wdn_ref, lnffn_ref, lnf_ref,
                      y_ref, ys_ref, snew_ref, hgs, *, prompt_steps):
    i = pl.program_id(0)
    weights = (wa_ref, wb_ref, wo_ref, lnffn_ref[...], wup_ref, wdn_ref, lnf_ref[...])
    group = state_ref.shape[0]
    ngroups = atts_ref.shape[0] // group

    @pl.when(i < prompt_steps)
    def _():
        y_ref[...] = _merge_ffn_rows(x_ref[...], att_ref[...], hg_ref[...], sga_ref[...], sgb_ref[...],
                                     *weights)
        rows = pl.ds(pl.multiple_of(jnp.minimum(i, ngroups - 1) * group, group), group)
        _sample_hgrn_group(rows, nw_ref[...], hqs_ref, hks_ref, his_ref, hgates_ref,
                           state_ref, snew_ref, hgs)

    @pl.when(i == prompt_steps)
    def _():
        _rows_to_linear(ys_ref, _merge_ffn_rows(_rows_from_linear(xs_ref), atts_ref[...], hgs[...],
                                                sgas_ref[...], sgbs_ref[...], *weights))


def _merge_ffn(x, att, hg, sga, sgb, xs, att_s, sga_s, sgb_s, hg_norm, hq_s, hk_s, hi_s, hgate_s, state,
               wa, wb, wo, ln_ffn, w_up, w_down, ln_f):
    n = x.shape[0]
    nb = att_s.shape[0]
    linear = pl.BlockSpec(xs.shape, lambda i: (0, 0))
    rows = MERGE_ROWS
    nt = n // rows
    g = SAMPLE_HG_GROUP
    assert nb // g <= nt

    def blk(c):
        return pl.BlockSpec((rows, c), lambda i: (jnp.minimum(i, nt - 1), 0))

    def sample(c):
        return pl.BlockSpec((nb, c), lambda i: (0, 0))

    sblk = pl.BlockSpec((g, HG_HEADS, HG_DK, HG_DV), lambda i: (jnp.minimum(i, nb // g - 1), 0, 0, 0))
    return pl.pallas_call(
        functools.partial(_merge_ffn_kernel, prompt_steps=nt),
        grid=(nt + 1,),
        in_specs=[blk(D_MODEL), blk(D_ATT), blk(D_HG), blk(D_MODEL), blk(D_MODEL),
                  linear, sample(D_ATT), sample(D_MODEL), sample(D_MODEL),
                  _resident((1, HG_DV)), _resident(hq_s.shape), _resident(hk_s.shape),
                  _resident(hi_s.shape), _resident(hgate_s.shape), sblk,
                  _resident(wa.shape), _resident(wb.shape), _resident(wo.shape),
                  _resident(w_up.shape), _resident(w_down.shape),
                  _resident((1, D_MODEL)), _resident((1, D_MODEL))],
        out_specs=[blk(D_MODEL), linear, sblk],
        out_shape=[jax.ShapeDtypeStruct((n, D_MODEL), F32), jax.ShapeDtypeStruct(xs.shape, F32),
                   jax.ShapeDtypeStruct(state.shape, F32)],
        scratch_shapes=[pltpu.VMEM((nb, D_HG), F32)],
        compiler_params=_params("arbitrary"),
        name="merge_ffn",
    )(x, att, hg, sga, sgb, xs, att_s, sga_s, sgb_s, hg_norm, hq_s, hk_s, hi_s, hgate_s, state,
      wa, wb, wo, w_up, w_down, ln_ffn, ln_f)


def _sample_attn_kernel(sink_ref, qm_ref, kvn_ref, ckt_ref, cvt_ref, mkt_ref, mvt_ref,
                        att_ref, nkt_ref, nvt_ref, o_all, mk_s, mv_s):
    nb = qm_ref.shape[0]
    first_seq = pl.multiple_of(pl.program_id(0) * nb, nb)

    @pl.when(pl.program_id(0) == 0)
    def _():
        for r in range(N_META):
            mk_s[:, r, :] = mkt_ref[r].T
            mv_s[:, r, :] = mvt_ref[r].T
    head = lax.broadcasted_iota(jnp.int32, (Q_HEADS, 1), 0)
    sk = jnp.zeros((Q_HEADS, 1), F32)
    for j in range(Q_HEADS):
        sk = jnp.where(head == j, sink_ref[j], sk)
    newest = lax.broadcasted_iota(jnp.int32, (D_KV, WINDOW), 1) == WINDOW - 1
    kvn = kvn_ref[...]
    hi = kvn.astype(BF16)
    rest = kvn - hi.astype(F32)
    mid = rest.astype(BF16)
    lo = (rest - mid.astype(F32)).astype(BF16)
    r = lax.broadcasted_iota(jnp.int32, (2 * D_KV, 2 * D_KV), 0)
    c = lax.broadcasted_iota(jnp.int32, (2 * D_KV, 2 * D_KV), 1)
    eye = jnp.where(r == c, 1.0, 0.0).astype(BF16)
    pieces_t = _dot_nt(eye, jnp.concatenate([hi, mid, lo], axis=0)).astype(BF16)
    pr = lax.broadcasted_iota(jnp.int32, (3 * nb, WINDOW), 0)
    pc = lax.broadcasted_iota(jnp.int32, (3 * nb, WINDOW), 1)
    for b in range(nb):
        pick = jnp.logical_and(lax.rem(pr, nb) == b, pc == WINDOW - 1)
        new_cols = _dot(pieces_t, jnp.where(pick, 1.0, 0.0).astype(BF16))
        nkt_ref[b] = jnp.where(newest, new_cols[:D_KV], pltpu.roll(ckt_ref[b], WINDOW - 1, axis=1))
        nvt_ref[b] = jnp.where(newest, new_cols[D_KV:], pltpu.roll(cvt_ref[b], WINDOW - 1, axis=1))
    qms = [qm_ref[b].astype(BF16) for b in range(nb)]
    scores = [(_dot(qms[b], nkt_ref[b].astype(BF16)),
               _dot_nt(qms[b], mk_s[first_seq + b].astype(BF16)))
              for b in range(nb)]
    for b, (s_w, s_m) in enumerate(scores):
        m = jnp.maximum(jnp.maximum(jnp.max(s_w, axis=1, keepdims=True),
                                    jnp.max(s_m, axis=1, keepdims=True)), sk)
        e_w = jnp.exp(s_w - m)
        e_m = jnp.exp(s_m - m)
        l = (jnp.sum(e_w, axis=1, keepdims=True) + jnp.sum(e_m, axis=1, keepdims=True)
             + jnp.exp(sk - m))
        o = (_dot_nt(e_w.astype(BF16), nvt_ref[b].astype(BF16))
             + _dot(e_m.astype(BF16), mv_s[first_seq + b].astype(BF16)))
        o_all[b] = o / l
    low = lax.broadcasted_iota(jnp.int32, (1, LANES), 1) < HEAD_DIM
    pairs = []
    for p in range(Q_HEADS // 2):
        halves = []
        for half in range(2):
            j = 2 * p + half
            kv_head = j // (Q_HEADS // KV_HEADS)
            oj = jnp.where(low if kv_head == 0 else jnp.logical_not(low), o_all[:, j, :], 0.0)
            halves.append(oj if half == kv_head else pltpu.roll(oj, HEAD_DIM, axis=1))
        pairs.append(halves[0] + halves[1])
    att_ref[...] = jnp.concatenate(pairs, axis=1)


def _sample_attention(sinks, qm, kv_new, ck, cv, mk, mv):
    nb = ck.shape[0]
    g = SAMPLE_ATT_GROUP

    def blk3(a, c):
        return pl.BlockSpec((g, a, c), lambda i: (i, 0, 0))

    return pl.pallas_call(
        _sample_attn_kernel,
        grid=(nb // g,),
        in_specs=[pl.BlockSpec(memory_space=pltpu.SMEM), blk3(Q_HEADS, D_KV),
                  pl.BlockSpec((g, 2 * D_KV), lambda i: (i, 0)),
                  blk3(D_KV, WINDOW), blk3(D_KV, WINDOW), _resident(mk.shape), _resident(mv.shape)],
        out_specs=[pl.BlockSpec((g, D_ATT), lambda i: (i, 0)), blk3(D_KV, WINDOW), blk3(D_KV, WINDOW)],
        out_shape=[jax.ShapeDtypeStruct((nb, D_ATT), F32),
                   jax.ShapeDtypeStruct((nb, D_KV, WINDOW), F32),
                   jax.ShapeDtypeStruct((nb, D_KV, WINDOW), F32)],
        scratch_shapes=[pltpu.VMEM((g, Q_HEADS, LANES), F32),
                        pltpu.VMEM((nb, N_META, D_KV), F32), pltpu.VMEM((nb, N_META, D_KV), F32)],
        compiler_params=_params("arbitrary"),
        name="sample_attn",
    )(sinks, qm, kv_new, ck, cv, mk, mv)


def kernel(x_prompt, x_sample, cache_k, cache_v, cache_meta_k, cache_meta_v, state_hgrn, meta,
           w_in, sinks, lb_param, hg_norm, w_att_out, w_hg_out, w_o, ln_mix, ln_ffn, w_up,
           w_down, ln_f):
    bsz, seq, _ = x_prompt.shape
    nb = x_sample.shape[0]
    ln_mix2 = ln_mix.reshape(1, D_MODEL)
    ln_ffn2 = ln_ffn.reshape(1, D_MODEL)
    ln_f2 = ln_f.reshape(1, D_MODEL)
    nw = hg_norm.reshape(1, HG_DV)

    xs = x_sample.reshape(nb * D_MODEL // LANES, LANES)

    xp = x_prompt.reshape(bsz * seq, D_MODEL)
    (att_p, hg_p, sga_p, sgb_p, state_p, lastkv_p,
     qm_s, kv_s, hq_s, hk_s, hi_s, hgate_s, sga_s, sgb_s,
     wa, wb, wo, wup, wdn) = _mixers(
        sinks.reshape(Q_HEADS), xp, xs, meta, ln_mix2, lb_param, w_in[0], nw,
        (w_att_out[0], w_hg_out[0], w_o[0], w_up[0], w_down[0]), bsz)

    def window_t(c):
        return jnp.swapaxes(c[0].reshape(nb, WINDOW, D_KV), 1, 2)

    def meta_t(c):
        return jnp.transpose(c[0].reshape(nb, N_META, D_KV), (1, 2, 0))

    att_s, nkt_s, nvt_s = _sample_attention(
        sinks.reshape(Q_HEADS), qm_s, kv_s, window_t(cache_k), window_t(cache_v),
        meta_t(cache_meta_k), meta_t(cache_meta_v))

    y_p, y_s, state_s = _merge_ffn(xp, att_p, hg_p, sga_p, sgb_p, xs, att_s, sga_s, sgb_s,
                                   nw, hq_s, hk_s, hi_s, hgate_s, state_hgrn[0],
                                   wa, wb, wo, ln_ffn2, wup, wdn, ln_f2)

    kv5 = lastkv_p.reshape(bsz, WINDOW, 2, KV_HEADS, HEAD_DIM)
    meta5 = jnp.broadcast_to(kv_s[nb:].reshape(1, N_META, 2, KV_HEADS, HEAD_DIM),
                             (bsz, N_META, 2, KV_HEADS, HEAD_DIM))
    return (y_p.reshape(bsz, seq, D_MODEL),
            y_s.reshape(nb, 1, D_MODEL),
            kv5[None, :, :, 0],
            kv5[None, :, :, 1],
            meta5[None, :, :, 0],
            meta5[None, :, :, 1],
            state_p[None],
            jnp.swapaxes(nkt_s, 1, 2).reshape(1, nb, WINDOW, KV_HEADS, HEAD_DIM),
            jnp.swapaxes(nvt_s, 1, 2).reshape(1, nb, WINDOW, KV_HEADS, HEAD_DIM),
            state_s[None])
```

```python
import functools

import jax
import jax.numpy as jnp
from jax import lax
from jax.experimental import pallas as pl
from jax.experimental.pallas import tpu as pltpu

F32 = jnp.float32
BF16 = jnp.bfloat16

D_MODEL = 1024
N_META = 16
WINDOW = 128
HEAD_DIM = 64
Q_HEADS = 8
KV_HEADS = 2
D_ATT = Q_HEADS * HEAD_DIM
D_KV = KV_HEADS * HEAD_DIM
HG_HEADS = 4
HG_DK = 128
HG_DV = 128
D_HG = HG_HEADS * HG_DK
HG_CHUNK = 64
D_FF = 4 * D_MODEL
EPS = 1e-6
C_Q = 0
C_KV = C_Q + D_ATT
C_HQ = C_KV + 2 * D_KV
C_HF = C_HQ + D_HG
C_HI = C_HF + D_HG
C_HGATE = C_HI + D_HG
C_GA = C_HGATE + D_HG
C_GB = C_GA + D_MODEL
D_IN = C_GB + D_MODEL

VMEM_LIMIT_BYTES = 56 * 1024 * 1024
MIX_ROWS = 512
PROJ_GROUP = 256
WEIGHT_SLAB_ROWS = 64
MERGE_ROWS = 512
FFN_CHUNK = 2048
LANES = 128
BF16_ROWS = 16
SAMPLE_ATT_GROUP = 32
SAMPLE_HG_GROUP = 8

_NT = (((1,), (1,)), ((), ()))
_TN = (((0,), (0,)), ((), ()))


def _dot(a, b):
    return jnp.dot(a, b, preferred_element_type=F32)


def _dot_nt(a, b):
    return lax.dot_general(a, b, _NT, preferred_element_type=F32)


def _dot_tn(a, b):
    return lax.dot_general(a, b, _TN, preferred_element_type=F32)


def _rows_from_linear(ref):
    per = D_MODEL // LANES
    rows = ref.shape[0] // per
    return jnp.concatenate([ref[pl.ds(c, rows, stride=per), :] for c in range(per)], axis=1)


def _rows_to_linear(ref, value):
    per = D_MODEL // LANES
    for c in range(per):
        ref[pl.ds(c, value.shape[0], stride=per), :] = value[:, c * LANES:(c + 1) * LANES]


def _rmsnorm(x, g):
    return x * lax.rsqrt(jnp.mean(x * x, axis=-1, keepdims=True) + EPS) * g


def _resident(shape):
    return pl.BlockSpec(shape, lambda *_: (0,) * len(shape), pipeline_mode=pl.Buffered(1))


def _params(*sem):
    return pltpu.CompilerParams(dimension_semantics=sem, vmem_limit_bytes=VMEM_LIMIT_BYTES)


def _lower_bound(lbp):
    e = jnp.exp(lbp - jnp.max(lbp, axis=0, keepdims=True))
    return e[0:1] / jnp.sum(e, axis=0, keepdims=True)


class _Projection:
    def __init__(self, xn, lbp, w_ref):
        self.xn = xn
        self.w_ref = w_ref
        self.lb = _lower_bound(lbp)

    def cols(self, base, part):
        return _dot(self.xn[...], self.w_ref[:, base + part.start:base + part.stop])

    def q_att(self, part=slice(0, D_ATT)):
        return self.cols(C_Q, part) * (HEAD_DIM ** -0.5)

    def kv(self):
        return self.cols(C_KV, slice(0, 2 * D_KV))

    def q_hg(self, part=slice(0, D_HG)):
        return self.cols(C_HQ, part) * (HG_DK ** -0.5)

    def forget(self, part=slice(0, D_HG)):
        lb = self.lb[:, part]
        f = lb + (1.0 - lb) * jax.nn.sigmoid(self.cols(C_HF, part))
        return 1.0 - f, jnp.log(f)

    def i_hg(self, part=slice(0, D_HG)):
        return self.cols(C_HI, part)

    def swish_gate(self, part=slice(0, D_HG)):
        g = self.cols(C_HGATE, part)
        return g * jax.nn.sigmoid(g)

    def branch_gate(self, base, part=slice(0, D_MODEL)):
        return jax.nn.sigmoid(self.cols(base, part))


class _Attention:
    def __init__(self, sink_ref, q_ref, kv, first, o_ref):
        self.sink_ref, self.q_ref, self.first, self.o_ref = sink_ref, q_ref, first, o_ref
        self.nsub = q_ref.shape[0] // WINDOW
        self.nk = 2 * WINDOW + N_META
        self.meta0 = WINDOW + q_ref.shape[0]
        lane = lax.broadcasted_iota(jnp.int32, (kv.shape[0], D_KV), 1)
        low = lane < HEAD_DIM
        k = kv[:, :D_KV]
        ksw = pltpu.roll(k, HEAD_DIM, axis=1)
        self.kboth = (jnp.where(low, k, ksw).astype(BF16), jnp.where(low, ksw, k).astype(BF16))
        qlane = lax.broadcasted_iota(jnp.int32, (1, LANES), 1)
        self.keep = (jnp.where(qlane < HEAD_DIM, 1.0, 0.0).astype(BF16),
                     jnp.where(qlane < HEAD_DIM, 0.0, 1.0).astype(BF16))
        r = lax.broadcasted_iota(jnp.int32, (D_KV, D_KV), 0)
        c = lax.broadcasted_iota(jnp.int32, (D_KV, D_KV), 1)
        eye = jnp.where(r == c, 1.0, 0.0).astype(BF16)
        self.vt = _dot_nt(eye, kv[:, D_KV:].astype(BF16)).astype(BF16)
        self.ones = jnp.ones((BF16_ROWS, self.nk), BF16)
        self.key = lax.broadcasted_iota(jnp.int32, (self.nk, 2 * WINDOW), 0)
        col = lax.broadcasted_iota(jnp.int32, (self.nk, 2 * WINDOW), 1)
        self.qry = jnp.bitwise_and(col, WINDOW - 1)
        self.first_head = lax.broadcasted_iota(jnp.int32, (1, 2 * WINDOW), 1) < WINDOW
        self.scores = {}

    def issue_scores(self, i):
        r0 = i * WINDOW
        out = []
        for p in range(Q_HEADS // 2):
            kb = self.kboth[(2 * p) // (Q_HEADS // KV_HEADS)]
            kmat = jnp.concatenate([kb[r0:r0 + 2 * WINDOW], kb[self.meta0:]], axis=0)
            qp = self.q_ref[r0:r0 + WINDOW, p * LANES:(p + 1) * LANES]
            q2 = jnp.concatenate([qp * self.keep[0], qp * self.keep[1]], axis=0)
            out.append(_dot_nt(kmat, q2))
        self.scores[i] = out

    def finish(self, i):
        r0 = i * WINDOW
        lo = jnp.where(self.first, WINDOW - 1, self.qry) if i == 0 else self.qry
        mask = jnp.logical_and(self.key > lo, self.key <= self.qry + WINDOW)
        mask = jnp.logical_or(mask, self.key >= 2 * WINDOW)
        vaug = []
        for h in range(KV_HEADS):
            vth = self.vt[h * HEAD_DIM:(h + 1) * HEAD_DIM]
            vaug.append(jnp.concatenate(
                [jnp.concatenate([vth[:, r0:r0 + 2 * WINDOW], vth[:, self.meta0:]], axis=1),
                 self.ones], axis=0))
        outs = []
        for p, raw in enumerate(self.scores.pop(i)):
            h = (2 * p) // (Q_HEADS // KV_HEADS)
            s = jnp.where(mask, raw, -jnp.inf)
            sk = jnp.where(self.first_head, self.sink_ref[2 * p], self.sink_ref[2 * p + 1])
            m = jnp.maximum(jnp.max(s, axis=0, keepdims=True), sk)
            e = jnp.exp(s - m).astype(BF16)
            oa = _dot(vaug[h], e)
            o = oa[:HEAD_DIM] / (oa[HEAD_DIM:HEAD_DIM + 1] + jnp.exp(sk - m))
            outs += [o[:, :WINDOW], o[:, WINDOW:]]
        self.o_ref[r0:r0 + WINDOW, :] = jnp.concatenate(outs, axis=0).T.astype(self.o_ref.dtype)


def _cumsum_rows(x):
    t = x.shape[0]
    row = lax.broadcasted_iota(jnp.int32, x.shape, 0)
    d = 1
    while d < t:
        x = x + jnp.where(row >= d, pltpu.roll(x, d, axis=0), 0.0)
        d *= 2
    return x


def _hg_out(o, gate, nw):
    parts = []
    for h in range(HG_HEADS):
        oh = o[:, h * HG_DV:(h + 1) * HG_DV]
        parts.append(oh * lax.rsqrt(jnp.mean(oh * oh, axis=-1, keepdims=True) + EPS) * nw)
    return jnp.concatenate(parts, axis=1) * gate


_HG_SLICES = [slice(h * HG_DK, (h + 1) * HG_DK) for h in range(HG_HEADS)]


def _state_update(iv, kd):
    return jnp.concatenate([_dot_tn(iv[:, hs], kd[:, hs]) for hs in _HG_SLICES], axis=1)


def _meta_state(mk, mlogf, mi):
    b = _cumsum_rows(mlogf)
    kd = (mk * jnp.exp(b[-1:] - b)).astype(BF16)
    return _state_update(mi.astype(BF16), kd)


class _Hgrn:
    def __init__(self, q_ref, k_ref, logf_ref, i_ref, g_ref, nw, state, o_ref):
        self.q_ref, self.k_ref, self.logf_ref, self.i_ref, self.g_ref = q_ref, k_ref, logf_ref, i_ref, g_ref
        self.nw, self.state, self.o_ref = nw, state, o_ref
        t = HG_CHUNK
        self.nchunk = q_ref.shape[0] // t
        r = lax.broadcasted_iota(jnp.int32, (t, t), 0)
        c = lax.broadcasted_iota(jnp.int32, (t, t), 1)
        self.causal = r >= c
        self.cums, self.terms, self.entering = {}, {}, {}

    def rows(self, ci):
        return slice(ci * HG_CHUNK, (ci + 1) * HG_CHUNK)

    def issue_cumsum(self, ci):
        self.cums[ci] = _cumsum_rows(self.logf_ref[self.rows(ci), :])

    def issue_local(self, ci):
        rows = self.rows(ci)
        b = self.cums.pop(ci)
        bl = b[-1:]
        k = self.k_ref[rows, :].astype(F32)
        qe = (self.q_ref[rows, :].astype(F32) * jnp.exp(b)).astype(BF16)
        ke = (k * jnp.exp(-b)).astype(BF16)
        kd = (k * jnp.exp(bl - b)).astype(BF16)
        iv = self.i_ref[rows, :].astype(BF16)
        a = [jnp.where(self.causal, _dot_nt(qe[:, hs], ke[:, hs]), 0.0).astype(BF16) for hs in _HG_SLICES]
        self.terms[ci] = (qe, iv, a)
        self.entering[ci] = self.state.astype(BF16)
        self.state = self.state * jnp.exp(bl) + _state_update(iv, kd)

    def finish(self, ci):
        rows = self.rows(ci)
        qe, iv, a = self.terms.pop(ci)
        ent = self.entering.pop(ci)
        outs = [_dot(a[h], iv[:, hs]) + _dot_nt(qe[:, hs], ent[:, hs]) for h, hs in enumerate(_HG_SLICES)]
        o = _hg_out(jnp.concatenate(outs, axis=1), self.g_ref[rows, :].astype(F32), self.nw)
        self.o_ref[rows, :] = o.astype(self.o_ref.dtype)


def _load_as_bf16(w_hbm, w_vmem, stage, sem):
    rows = stage.shape[1]
    nslab = w_hbm.shape[0] // rows

    def slab_copy(c):
        return pltpu.make_async_copy(w_hbm.at[pl.ds(c * rows, rows), :], stage.at[c % 2], sem.at[c % 2])

    slab_copy(0).start()
    for c in range(nslab):
        if c + 1 < nslab:
            slab_copy(c + 1).start()
        slab_copy(c).wait()
        w_vmem[pl.ds(c * rows, rows), :] = stage[c % 2].astype(w_vmem.dtype)


def _mixer_kernel(sink_ref, x0_ref, xnext_ref, xs_ref, xm_ref, g_ref, lbp_ref, w_hbm, nw_ref,
                  wa32, wb32, wo32, wup32, wdn32,
                  att_ref, hg_ref, sga_ref, sgb_ref, sfin_ref, lastkv_ref,
                  qs_ref, kvs_ref, hqs_ref, hks_ref, his_ref, hgs_ref, sgas_ref, sgbs_ref,
                  wa16, wb16, wo16, wup16, wdn16,
                  w_ref, stage, sem, xn_s, zq, zkv, zhq, zhk, zlogf, zhi, zhg, kvm_ref, st_ref, mst_ref,
                  *, tiles_per_seq):
    s = pl.program_id(0)
    t = xnext_ref.shape[0]

    @pl.when(s == 0)
    def _():
        _load_as_bf16(w_hbm, w_ref, stage, sem)
        xn_s[...] = _rmsnorm(x0_ref[...], g_ref[...]).astype(xn_s.dtype)
        for ref in (zq, zkv, zhq, zhk, zlogf, zhi, zhg, st_ref):
            ref[...] = jnp.zeros(ref.shape, ref.dtype)
        xs = _rows_from_linear(xs_ref)
        nb = xs.shape[0]
        small = jnp.concatenate([xs, xm_ref[...]], axis=0)
        p = _Projection(_rmsnorm(small, g_ref[...]).astype(BF16), lbp_ref[...], w_ref)
        q_small = p.q_att()
        low = lax.broadcasted_iota(jnp.int32, (1, LANES), 1) < HEAD_DIM
        for j in range(Q_HEADS):
            kv_head, half = j // (Q_HEADS // KV_HEADS), j % 2
            pair = q_small[:, (j // 2) * LANES:(j // 2 + 1) * LANES]
            moved = pair if half == kv_head else pltpu.roll(pair, HEAD_DIM, axis=1)
            qs_ref[:, j, :] = jnp.where(low if kv_head == 0 else jnp.logical_not(low), moved, 0.0)
        kv = p.kv()
        kvs_ref[...] = kv
        kvm_ref[...] = kv[nb:]
        hqs_ref[...] = p.q_hg()
        k, logf = p.forget()
        hks_ref[...] = k
        iv = p.i_hg()
        his_ref[...] = iv
        hgs_ref[...] = p.swish_gate()
        sgas_ref[...] = p.branch_gate(C_GA).astype(sgas_ref.dtype)
        sgbs_ref[...] = p.branch_gate(C_GB).astype(sgbs_ref.dtype)
        mst_ref[...] = _meta_state(k[nb:], logf[nb:], iv[nb:])

    first = lax.rem(jnp.maximum(s - 1, 0), tiles_per_seq) == 0

    att = _Attention(sink_ref, zq, jnp.concatenate([zkv[...], kvm_ref[...]], axis=0), first, att_ref)
    last = zkv[t:t + WINDOW, :]
    lastkv_ref[...] = last
    zkv[0:WINDOW, :] = last
    entering = jnp.where(first, mst_ref[...], st_ref[...])
    hg = _Hgrn(zhq, zhk, zlogf, zhi, zhg, nw_ref[...], entering, hg_ref)
    proj = _Projection(xn_s, lbp_ref[...], w_ref)

    def gate_a(p):
        sga_ref[:, p] = proj.branch_gate(C_GA, p).astype(sga_ref.dtype)

    def gate_b(p):
        sgb_ref[:, p] = proj.branch_gate(C_GB, p).astype(sgb_ref.dtype)

    def new_kv(_):
        zkv[WINDOW:WINDOW + t, :] = proj.kv()

    def new_forget(p):
        k_new, logf_new = proj.forget(p)
        zhk[:, p] = k_new.astype(zhk.dtype)
        zlogf[:, p] = logf_new

    def new_q(p):
        zq[:, p] = proj.q_att(p).astype(zq.dtype)

    def new_hq(p):
        zhq[:, p] = proj.q_hg(p).astype(zhq.dtype)

    def new_hi(p):
        zhi[:, p] = proj.i_hg(p).astype(zhi.dtype)

    def new_gate(p):
        zhg[:, p] = proj.swish_gate(p).astype(zhg.dtype)

    def parts(n):
        return [slice(c, c + PROJ_GROUP) for c in range(0, n, PROJ_GROUP)]

    groups = ([(gate_a, p) for p in parts(D_MODEL)] + [(gate_b, p) for p in parts(D_MODEL)]
              + [(new_kv, None)] + [(new_q, p) for p in parts(D_ATT)]
              + [(new_forget, p) for p in parts(D_HG)] + [(new_hq, p) for p in parts(D_HG)]
              + [(new_hi, p) for p in parts(D_HG)] + [(new_gate, p) for p in parts(D_HG)])

    def save_state():
        st_ref[...] = hg.state
        for h, hs in enumerate(_HG_SLICES):
            sfin_ref[h] = hg.state[:, hs].T

    qk, cs, loc, fin, out = att.issue_scores, hg.issue_cumsum, hg.issue_local, att.finish, hg.finish
    mixer_work = {
        0: [(qk, 0), (cs, 0), (cs, 1), (cs, 2), (cs, 3)],
        1: [(cs, 4), (cs, 5), (loc, 0)],
        2: [(cs, 6), (cs, 7), (loc, 1)],
        3: [(qk, 1), (loc, 2)],
        4: [(fin, 0), (loc, 3)],
        5: [(out, 0), (loc, 4)],
        6: [(qk, 2), (out, 1), (loc, 5)],
        7: [(fin, 1), (out, 2), (loc, 6)],
        8: [(qk, 3), (out, 3), (loc, 7), (save_state,)],
        9: [(out, 4)],
        10: [(fin, 2), (out, 5)],
        11: [(out, 6)],
        12: [(out, 7)],
        13: [(fin, 3)],
    }
    for slot, (fn, p) in enumerate(groups):
        fn(p)
        for item in mixer_work.get(slot, []):
            item[0](*item[1:])
    xn_s[...] = _rmsnorm(xnext_ref[...], g_ref[...]).astype(xn_s.dtype)
    for src, dst in ((wa32, wa16), (wb32, wb16), (wo32, wo16), (wup32, wup16), (wdn32, wdn16)):
        dst[...] = src[...].astype(dst.dtype)


def _mixers(sinks, x, x_sample, x_meta, g, lb_param, w_f32, hg_norm, later_weights, bsz):
    n = x.shape[0]
    t = MIX_ROWS
    nt = n // t
    per_seq = nt // bsz
    rows = x_sample.shape[0] * LANES // D_MODEL + x_meta.shape[0]

    def this_tile(c):
        return pl.BlockSpec((t, c), lambda s: (jnp.minimum(s, nt - 1), 0))

    def prev_tile(c):
        return pl.BlockSpec((t, c), lambda s: (jnp.maximum(s - 1, 0), 0))

    def prev_seq(shape):
        return pl.BlockSpec((None,) + shape,
                            lambda s: (jnp.maximum(s - 1, 0) // per_seq,) + (0,) * len(shape))

    def small(c):
        return pl.BlockSpec((rows, c), lambda s: (0, 0))

    first_tile = pl.BlockSpec((t, D_MODEL), lambda s: (0, 0), pipeline_mode=pl.Buffered(1))
    next_tile = pl.BlockSpec((t, D_MODEL), lambda s: (jnp.minimum(s + 1, nt - 1), 0))
    small_widths = (2 * D_KV, D_HG, D_HG, D_HG, D_HG, D_MODEL, D_MODEL)
    small_dtypes = (F32, F32, F32, F32, F32, BF16, BF16)
    qm_shape = (rows, Q_HEADS, LANES)
    qm_spec = pl.BlockSpec(qm_shape, lambda s: (0, 0, 0))

    def row_block(wt):
        return pl.BlockSpec((wt.shape[0] // nt, wt.shape[1]), lambda s: (jnp.minimum(s, nt - 1), 0))

    return pl.pallas_call(
        functools.partial(_mixer_kernel, tiles_per_seq=per_seq),
        grid=(nt + 1,),
        in_specs=[pl.BlockSpec(memory_space=pltpu.SMEM), first_tile, next_tile,
                  _resident(x_sample.shape), _resident(x_meta.shape), _resident((1, D_MODEL)),
                  _resident(lb_param.shape), pl.BlockSpec(memory_space=pl.ANY), _resident((1, HG_DV))]
                 + [row_block(wt) for wt in later_weights],
        out_specs=[prev_tile(D_ATT), prev_tile(D_HG), this_tile(D_MODEL), this_tile(D_MODEL),
                   prev_seq((HG_HEADS, HG_DK, HG_DV)), prev_seq((WINDOW, 2 * D_KV))]
                  + [qm_spec] + [small(c) for c in small_widths] + [row_block(wt) for wt in later_weights],
        out_shape=[jax.ShapeDtypeStruct((n, D_ATT), BF16), jax.ShapeDtypeStruct((n, D_HG), BF16),
                   jax.ShapeDtypeStruct((n, D_MODEL), BF16), jax.ShapeDtypeStruct((n, D_MODEL), BF16),
                   jax.ShapeDtypeStruct((bsz, HG_HEADS, HG_DK, HG_DV), F32),
                   jax.ShapeDtypeStruct((bsz, WINDOW, 2 * D_KV), F32)]
                  + [jax.ShapeDtypeStruct(qm_shape, F32)]
                  + [jax.ShapeDtypeStruct((rows, c), d) for c, d in zip(small_widths, small_dtypes)]
                  + [jax.ShapeDtypeStruct(wt.shape, BF16) for wt in later_weights],
        scratch_shapes=[pltpu.VMEM((D_MODEL, D_IN), BF16),
                        pltpu.VMEM((2, WEIGHT_SLAB_ROWS, D_IN), F32),
                        pltpu.SemaphoreType.DMA((2,)),
                        pltpu.VMEM((t, D_MODEL), BF16),
                        pltpu.VMEM((t, D_ATT), BF16), pltpu.VMEM((WINDOW + t, 2 * D_KV), F32),
                        pltpu.VMEM((t, D_HG), BF16), pltpu.VMEM((t, D_HG), BF16),
                        pltpu.VMEM((t, D_HG), F32), pltpu.VMEM((t, D_HG), BF16),
                        pltpu.VMEM((t, D_HG), BF16),
                        pltpu.VMEM((N_META, 2 * D_KV), F32),
                        pltpu.VMEM((HG_DV, D_HG), F32),
                        pltpu.VMEM((HG_DV, D_HG), F32)],
        compiler_params=_params("arbitrary"),
        name="mixers",
    )(sinks, x, x, x_sample, x_meta, g, lb_param, w_f32, hg_norm, *later_weights)


def _merge_ffn_rows(x, att, hg, sga, sgb, wa_ref, wb_ref, wo_ref, ln_ffn, wup_ref, wdn_ref, ln_f):
    ya = _dot(att.astype(BF16), wa_ref[...])
    yb = _dot(hg.astype(BF16), wb_ref[...])
    mix = sga.astype(F32) * ya + sgb.astype(F32) * yb
    h1 = x + _dot(mix.astype(BF16), wo_ref[...])
    xn = _rmsnorm(h1, ln_ffn).astype(BF16)
    acc = jnp.zeros_like(h1)
    for c in range(0, D_FF, FFN_CHUNK):
        u = jnp.maximum(_dot(xn, wup_ref[:, c:c + FFN_CHUNK]), 0.0)
        acc = acc + _dot((u * u).astype(BF16), wdn_ref[c:c + FFN_CHUNK, :])
    return _rmsnorm(h1 + acc, ln_f)


def _sample_hgrn_group(rows, nw, q_ref, k_ref, i_ref, g_ref, s_ref, snew_ref, hg_ref):
    k = k_ref[rows, :]
    q = q_ref[rows, :]
    iv = i_ref[rows, :]
    r = lax.broadcasted_iota(jnp.int32, (HG_DK, HG_DK), 0)
    c = lax.broadcasted_iota(jnp.int32, (HG_DK, HG_DK), 1)
    eye = r == c

    def column(row):
        return jnp.sum(jnp.where(eye, row, 0.0), axis=1, keepdims=True)

    outs = []
    for b in range(s_ref.shape[0]):
        heads = []
        for h, hs in enumerate(_HG_SLICES):
            kc = column(k[b:b + 1, hs])
            qc = column(q[b:b + 1, hs])
            s_old = s_ref[b, h]
            s = s_old + kc * (iv[b:b + 1, hs] - s_old)
            snew_ref[b, h] = s
            heads.append(jnp.sum(qc * s, axis=0, keepdims=True))
        outs.append(jnp.concatenate(heads, axis=1))
    hg_ref[rows, :] = _hg_out(jnp.concatenate(outs, axis=0), g_ref[rows, :], nw)


def _merge_ffn_kernel(x_ref, att_ref, hg_ref, sga_ref, sgb_ref,
                      xs_ref, atts_ref, sgas_ref, sgbs_ref,
                      nw_ref, hqs_ref, hks_ref, his_ref, hgates_ref, state_ref,
                      kvs_ref, ckt_ref, cvt_ref,
                      wa_ref, wb_ref, wo_ref, wup_ref, wdn_ref, lnffn_ref, lnf_ref,
                      y_ref, ys_ref, snew_ref, nkt_ref, nvt_ref, hgs, *, prompt_steps):
    i = pl.program_id(0)
    weights = (wa_ref, wb_ref, wo_ref, lnffn_ref[...], wup_ref, wdn_ref, lnf_ref[...])
    group = state_ref.shape[0]
    ngroups = atts_ref.shape[0] // group

    @pl.when(i < prompt_steps)
    def _():
        y_ref[...] = _merge_ffn_rows(x_ref[...], att_ref[...], hg_ref[...], sga_ref[...], sgb_ref[...],
                                     *weights)
        rows = pl.ds(pl.multiple_of(jnp.minimum(i, ngroups - 1) * group, group), group)
        _sample_hgrn_group(rows, nw_ref[...], hqs_ref, hks_ref, his_ref, hgates_ref,
                           state_ref, snew_ref, hgs)
        kvn = kvs_ref[rows, :]
        r = lax.broadcasted_iota(jnp.int32, (D_KV, WINDOW), 0)
        c = lax.broadcasted_iota(jnp.int32, (D_KV, WINDOW), 1)
        for b in range(group):
            for src, dst, lo in ((ckt_ref, nkt_ref, 0), (cvt_ref, nvt_ref, D_KV)):
                col = jnp.sum(jnp.where(r == c, kvn[b:b + 1, lo:lo + D_KV], 0.0), axis=1, keepdims=True)
                dst[b] = jnp.where(c == WINDOW - 1, col, pltpu.roll(src[b], WINDOW - 1, axis=1))

    @pl.when(i == prompt_steps)
    def _():
        _rows_to_linear(ys_ref, _merge_ffn_rows(_rows_from_linear(xs_ref), atts_ref[...], hgs[...],
                                                sgas_ref[...], sgbs_ref[...], *weights))


def _merge_ffn(x, att, hg, sga, sgb, xs, att_s, sga_s, sgb_s, hg_norm, hq_s, hk_s, hi_s, hgate_s, state,
               kv_s, ck_t, cv_t, wa, wb, wo, ln_ffn, w_up, w_down, ln_f):
    n = x.shape[0]
    nb = att_s.shape[0]
    linear = pl.BlockSpec(xs.shape, lambda i: (0, 0))
    rows = MERGE_ROWS
    nt = n // rows
    g = SAMPLE_HG_GROUP
    assert nb // g <= nt

    def blk(c):
        return pl.BlockSpec((rows, c), lambda i: (jnp.minimum(i, nt - 1), 0))

    def sample(c):
        return pl.BlockSpec((nb, c), lambda i: (0, 0))

    sblk = pl.BlockSpec((g, HG_HEADS, HG_DK, HG_DV), lambda i: (jnp.minimum(i, nb // g - 1), 0, 0, 0))
    cblk = pl.BlockSpec((g, D_KV, WINDOW), lambda i: (jnp.minimum(i, nb // g - 1), 0, 0))
    return pl.pallas_call(
        functools.partial(_merge_ffn_kernel, prompt_steps=nt),
        grid=(nt + 1,),
        in_specs=[blk(D_MODEL), blk(D_ATT), blk(D_HG), blk(D_MODEL), blk(D_MODEL),
                  linear, sample(D_ATT), sample(D_MODEL), sample(D_MODEL),
                  _resident((1, HG_DV)), _resident(hq_s.shape), _resident(hk_s.shape),
                  _resident(hi_s.shape), _resident(hgate_s.shape), sblk,
                  _resident(kv_s.shape), cblk, cblk,
                  _resident(wa.shape), _resident(wb.shape), _resident(wo.shape),
                  _resident(w_up.shape), _resident(w_down.shape),
                  _resident((1, D_MODEL)), _resident((1, D_MODEL))],
        out_specs=[blk(D_MODEL), linear, sblk, cblk, cblk],
        out_shape=[jax.ShapeDtypeStruct((n, D_MODEL), F32), jax.ShapeDtypeStruct(xs.shape, F32),
                   jax.ShapeDtypeStruct(state.shape, F32),
                   jax.ShapeDtypeStruct(ck_t.shape, F32), jax.ShapeDtypeStruct(cv_t.shape, F32)],
        scratch_shapes=[pltpu.VMEM((nb, D_HG), F32)],
        compiler_params=_params("arbitrary"),
        name="merge_ffn",
    )(x, att, hg, sga, sgb, xs, att_s, sga_s, sgb_s, hg_norm, hq_s, hk_s, hi_s, hgate_s, state,
      kv_s, ck_t, cv_t, wa, wb, wo, w_up, w_down, ln_ffn, ln_f)


def _sample_attn_kernel(sink_ref, qm_ref, kvn_ref, ckt_ref, cvt_ref, mkt_ref, mvt_ref,
                        att_ref, o_all, mk_s, mv_s):
    nb = qm_ref.shape[0]
    first_seq = pl.multiple_of(pl.program_id(0) * nb, nb)

    @pl.when(pl.program_id(0) == 0)
    def _():
        for r in range(N_META):
            mk_s[:, r, :] = mkt_ref[r].T
            mv_s[:, r, :] = mvt_ref[r].T
    head = lax.broadcasted_iota(jnp.int32, (Q_HEADS, 1), 0)
    sk = jnp.zeros((Q_HEADS, 1), F32)
    for j in range(Q_HEADS):
        sk = jnp.where(head == j, sink_ref[j], sk)
    oldest = lax.broadcasted_iota(jnp.int32, (Q_HEADS, WINDOW), 1) == 0
    kvn = kvn_ref[...]
    qms = [qm_ref[b] for b in range(nb)]
    scores = [(_dot(qms[b].astype(BF16), ckt_ref[b].astype(BF16)),
               _dot_nt(qms[b].astype(BF16), mk_s[first_seq + b].astype(BF16)))
              for b in range(nb)]
    for b, (s_w, s_m) in enumerate(scores):
        s_w = jnp.where(oldest, -jnp.inf, s_w)
        s_n = jnp.sum(qms[b] * kvn[b:b + 1, :D_KV], axis=1, keepdims=True)
        m = jnp.maximum(jnp.maximum(jnp.max(s_w, axis=1, keepdims=True),
                                    jnp.max(s_m, axis=1, keepdims=True)), jnp.maximum(s_n, sk))
        e_w = jnp.exp(s_w - m)
        e_m = jnp.exp(s_m - m)
        e_n = jnp.exp(s_n - m)
        l = (jnp.sum(e_w, axis=1, keepdims=True) + jnp.sum(e_m, axis=1, keepdims=True) + e_n
             + jnp.exp(sk - m))
        o = (_dot_nt(e_w.astype(BF16), cvt_ref[b].astype(BF16))
             + _dot(e_m.astype(BF16), mv_s[first_seq + b].astype(BF16))
             + e_n * kvn[b:b + 1, D_KV:])
        o_all[b] = o / l
    low = lax.broadcasted_iota(jnp.int32, (1, LANES), 1) < HEAD_DIM
    pairs = []
    for p in range(Q_HEADS // 2):
        halves = []
        for half in range(2):
            j = 2 * p + half
            kv_head = j // (Q_HEADS // KV_HEADS)
            oj = jnp.where(low if kv_head == 0 else jnp.logical_not(low), o_all[:, j, :], 0.0)
            halves.append(oj if half == kv_head else pltpu.roll(oj, HEAD_DIM, axis=1))
        pairs.append(halves[0] + halves[1])
    att_ref[...] = jnp.concatenate(pairs, axis=1)


def _sample_attention(sinks, qm, kv_new, ck, cv, mk, mv):
    nb = ck.shape[0]
    g = SAMPLE_ATT_GROUP

    def blk3(a, c):
        return pl.BlockSpec((g, a, c), lambda i: (i, 0, 0))

    return pl.pallas_call(
        _sample_attn_kernel,
        grid=(nb // g,),
        in_specs=[pl.BlockSpec(memory_space=pltpu.SMEM), blk3(Q_HEADS, D_KV),
                  pl.BlockSpec((g, 2 * D_KV), lambda i: (i, 0)),
                  blk3(D_KV, WINDOW), blk3(D_KV, WINDOW), _resident(mk.shape), _resident(mv.shape)],
        out_specs=pl.BlockSpec((g, D_ATT), lambda i: (i, 0)),
        out_shape=jax.ShapeDtypeStruct((nb, D_ATT), F32),
        scratch_shapes=[pltpu.VMEM((g, Q_HEADS, LANES), F32),
                        pltpu.VMEM((nb, N_META, D_KV), F32), pltpu.VMEM((nb, N_META, D_KV), F32)],
        compiler_params=_params("arbitrary"),
        name="sample_attn",
    )(sinks, qm, kv_new, ck, cv, mk, mv)


def kernel(x_prompt, x_sample, cache_k, cache_v, cache_meta_k, cache_meta_v, state_hgrn, meta,
           w_in, sinks, lb_param, hg_norm, w_att_out, w_hg_out, w_o, ln_mix, ln_ffn, w_up,
           w_down, ln_f):
    bsz, seq, _ = x_prompt.shape
    nb = x_sample.shape[0]
    ln_mix2 = ln_mix.reshape(1, D_MODEL)
    ln_ffn2 = ln_ffn.reshape(1, D_MODEL)
    ln_f2 = ln_f.reshape(1, D_MODEL)
    nw = hg_norm.reshape(1, HG_DV)

    xs = x_sample.reshape(nb * D_MODEL // LANES, LANES)

    xp = x_prompt.reshape(bsz * seq, D_MODEL)
    (att_p, hg_p, sga_p, sgb_p, state_p, lastkv_p,
     qm_s, kv_s, hq_s, hk_s, hi_s, hgate_s, sga_s, sgb_s,
     wa, wb, wo, wup, wdn) = _mixers(
        sinks.reshape(Q_HEADS), xp, xs, meta, ln_mix2, lb_param, w_in[0], nw,
        (w_att_out[0], w_hg_out[0], w_o[0], w_up[0], w_down[0]), bsz)

    def window_t(c):
        return jnp.swapaxes(c[0].reshape(nb, WINDOW, D_KV), 1, 2)

    def meta_t(c):
        return jnp.transpose(c[0].reshape(nb, N_META, D_KV), (1, 2, 0))

    ck_t, cv_t = window_t(cache_k), window_t(cache_v)
    att_s = _sample_attention(sinks.reshape(Q_HEADS), qm_s, kv_s, ck_t, cv_t,
                              meta_t(cache_meta_k), meta_t(cache_meta_v))

    y_p, y_s, state_s, nkt_s, nvt_s = _merge_ffn(
        xp, att_p, hg_p, sga_p, sgb_p, xs, att_s, sga_s, sgb_s,
        nw, hq_s, hk_s, hi_s, hgate_s, state_hgrn[0], kv_s, ck_t, cv_t,
        wa, wb, wo, ln_ffn2, wup, wdn, ln_f2)

    kv5 = lastkv_p.reshape(bsz, WINDOW, 2, KV_HEADS, HEAD_DIM)
    meta5 = jnp.broadcast_to(kv_s[nb:].reshape(1, N_META, 2, KV_HEADS, HEAD_DIM),
                             (bsz, N_META, 2, KV_HEADS, HEAD_DIM))
    return (y_p.reshape(bsz, seq, D_MODEL),
            y_s.reshape(nb, 1, D_MODEL),
            kv5[None, :, :, 0],
            kv5[None, :, :, 1],
            meta5[None, :, :, 0],
            meta5[None, :, :, 1],
            state_p[None],
            jnp.swapaxes(nkt_s, 1, 2).reshape(1, nb, WINDOW, KV_HEADS, HEAD_DIM),
            jnp.swapaxes(nvt_s, 1, 2).reshape(1, nb, WINDOW, KV_HEADS, HEAD_DIM),
            state_s[None])
```

```python
import functools

import jax
import jax.numpy as jnp
from jax import lax
from jax.experimental import pallas as pl
from jax.experimental.pallas import tpu as pltpu

F32 = jnp.float32
BF16 = jnp.bfloat16

D_MODEL = 1024
N_META = 16
WINDOW = 128
HEAD_DIM = 64
Q_HEADS = 8
KV_HEADS = 2
D_ATT = Q_HEADS * HEAD_DIM
D_KV = KV_HEADS * HEAD_DIM
HG_HEADS = 4
HG_DK = 128
HG_DV = 128
D_HG = HG_HEADS * HG_DK
HG_CHUNK = 64
D_FF = 4 * D_MODEL
EPS = 1e-6
C_Q = 0
C_KV = C_Q + D_ATT
C_HQ = C_KV + 2 * D_KV
C_HF = C_HQ + D_HG
C_HI = C_HF + D_HG
C_HGATE = C_HI + D_HG
C_GA = C_HGATE + D_HG
C_GB = C_GA + D_MODEL
D_IN = C_GB + D_MODEL

VMEM_LIMIT_BYTES = 56 * 1024 * 1024
MIX_ROWS = 512
PROJ_GROUP = 256
WEIGHT_SLAB_ROWS = 64
WEIGHT_SLOTS = 4
MERGE_ROWS = 512
FFN_CHUNK = 2048
LANES = 128
BF16_ROWS = 16
SAMPLE_ATT_GROUP = 32
SAMPLE_HG_GROUP = 8

_NT = (((1,), (1,)), ((), ()))
_TN = (((0,), (0,)), ((), ()))


def _dot(a, b):
    return jnp.dot(a, b, preferred_element_type=F32)


def _dot_nt(a, b):
    return lax.dot_general(a, b, _NT, preferred_element_type=F32)


def _dot_tn(a, b):
    return lax.dot_general(a, b, _TN, preferred_element_type=F32)


def _rows_from_linear(ref):
    per = D_MODEL // LANES
    rows = ref.shape[0] // per
    return jnp.concatenate([ref[pl.ds(c, rows, stride=per), :] for c in range(per)], axis=1)


def _rows_to_linear(ref, value):
    per = D_MODEL // LANES
    for c in range(per):
        ref[pl.ds(c, value.shape[0], stride=per), :] = value[:, c * LANES:(c + 1) * LANES]


def _rmsnorm(x, g):
    return x * lax.rsqrt(jnp.mean(x * x, axis=-1, keepdims=True) + EPS) * g


def _resident(shape):
    return pl.BlockSpec(shape, lambda *_: (0,) * len(shape), pipeline_mode=pl.Buffered(1))


def _params(*sem):
    return pltpu.CompilerParams(dimension_semantics=sem, vmem_limit_bytes=VMEM_LIMIT_BYTES)


def _lower_bound(lbp):
    e = jnp.exp(lbp - jnp.max(lbp, axis=0, keepdims=True))
    return e[0:1] / jnp.sum(e, axis=0, keepdims=True)


class _Projection:
    def __init__(self, xn, lbp, w_ref):
        self.xn = xn
        self.w_ref = w_ref
        self.lb = _lower_bound(lbp)

    def cols(self, base, part):
        return _dot(self.xn[...], self.w_ref[:, base + part.start:base + part.stop])

    def q_att(self, part=slice(0, D_ATT)):
        return self.cols(C_Q, part) * (HEAD_DIM ** -0.5)

    def kv(self):
        return self.cols(C_KV, slice(0, 2 * D_KV))

    def q_hg(self, part=slice(0, D_HG)):
        return self.cols(C_HQ, part) * (HG_DK ** -0.5)

    def forget(self, part=slice(0, D_HG)):
        lb = self.lb[:, part]
        f = lb + (1.0 - lb) * jax.nn.sigmoid(self.cols(C_HF, part))
        return 1.0 - f, jnp.log(f)

    def i_hg(self, part=slice(0, D_HG)):
        return self.cols(C_HI, part)

    def swish_gate(self, part=slice(0, D_HG)):
        g = self.cols(C_HGATE, part)
        return g * jax.nn.sigmoid(g)

    def branch_gate(self, base, part=slice(0, D_MODEL)):
        return jax.nn.sigmoid(self.cols(base, part))


class _Attention:
    def __init__(self, sink_ref, q_ref, kv, first, o_ref):
        self.sink_ref, self.q_ref, self.first, self.o_ref = sink_ref, q_ref, first, o_ref
        self.nsub = q_ref.shape[0] // WINDOW
        self.nk = 2 * WINDOW + N_META
        self.meta0 = WINDOW + q_ref.shape[0]
        lane = lax.broadcasted_iota(jnp.int32, (kv.shape[0], D_KV), 1)
        low = lane < HEAD_DIM
        k = kv[:, :D_KV]
        ksw = pltpu.roll(k, HEAD_DIM, axis=1)
        self.kboth = (jnp.where(low, k, ksw).astype(BF16), jnp.where(low, ksw, k).astype(BF16))
        qlane = lax.broadcasted_iota(jnp.int32, (1, LANES), 1)
        self.keep = (jnp.where(qlane < HEAD_DIM, 1.0, 0.0).astype(BF16),
                     jnp.where(qlane < HEAD_DIM, 0.0, 1.0).astype(BF16))
        r = lax.broadcasted_iota(jnp.int32, (D_KV, D_KV), 0)
        c = lax.broadcasted_iota(jnp.int32, (D_KV, D_KV), 1)
        eye = jnp.where(r == c, 1.0, 0.0).astype(BF16)
        self.vt = _dot_nt(eye, kv[:, D_KV:].astype(BF16)).astype(BF16)
        self.ones = jnp.ones((BF16_ROWS, self.nk), BF16)
        self.key = lax.broadcasted_iota(jnp.int32, (self.nk, 2 * WINDOW), 0)
        col = lax.broadcasted_iota(jnp.int32, (self.nk, 2 * WINDOW), 1)
        self.qry = jnp.bitwise_and(col, WINDOW - 1)
        self.first_head = lax.broadcasted_iota(jnp.int32, (1, 2 * WINDOW), 1) < WINDOW
        self.scores = {}

    def issue_scores(self, i):
        r0 = i * WINDOW
        out = []
        for p in range(Q_HEADS // 2):
            kb = self.kboth[(2 * p) // (Q_HEADS // KV_HEADS)]
            kmat = jnp.concatenate([kb[r0:r0 + 2 * WINDOW], kb[self.meta0:]], axis=0)
            qp = self.q_ref[r0:r0 + WINDOW, p * LANES:(p + 1) * LANES]
            q2 = jnp.concatenate([qp * self.keep[0], qp * self.keep[1]], axis=0)
            out.append(_dot_nt(kmat, q2))
        self.scores[i] = out

    def finish(self, i):
        r0 = i * WINDOW
        lo = jnp.where(self.first, WINDOW - 1, self.qry) if i == 0 else self.qry
        mask = jnp.logical_and(self.key > lo, self.key <= self.qry + WINDOW)
        mask = jnp.logical_or(mask, self.key >= 2 * WINDOW)
        vaug = []
        for h in range(KV_HEADS):
            vth = self.vt[h * HEAD_DIM:(h + 1) * HEAD_DIM]
            vaug.append(jnp.concatenate(
                [jnp.concatenate([vth[:, r0:r0 + 2 * WINDOW], vth[:, self.meta0:]], axis=1),
                 self.ones], axis=0))
        outs = []
        for p, raw in enumerate(self.scores.pop(i)):
            h = (2 * p) // (Q_HEADS // KV_HEADS)
            s = jnp.where(mask, raw, -jnp.inf)
            sk = jnp.where(self.first_head, self.sink_ref[2 * p], self.sink_ref[2 * p + 1])
            m = jnp.maximum(jnp.max(s, axis=0, keepdims=True), sk)
            e = jnp.exp(s - m).astype(BF16)
            oa = _dot(vaug[h], e)
            o = oa[:HEAD_DIM] / (oa[HEAD_DIM:HEAD_DIM + 1] + jnp.exp(sk - m))
            outs += [o[:, :WINDOW], o[:, WINDOW:]]
        self.o_ref[r0:r0 + WINDOW, :] = jnp.concatenate(outs, axis=0).T.astype(self.o_ref.dtype)


def _cumsum_rows(x):
    t = x.shape[0]
    row = lax.broadcasted_iota(jnp.int32, x.shape, 0)
    d = 1
    while d < t:
        x = x + jnp.where(row >= d, pltpu.roll(x, d, axis=0), 0.0)
        d *= 2
    return x


def _hg_out(o, gate, nw):
    parts = []
    for h in range(HG_HEADS):
        oh = o[:, h * HG_DV:(h + 1) * HG_DV]
        parts.append(oh * lax.rsqrt(jnp.mean(oh * oh, axis=-1, keepdims=True) + EPS) * nw)
    return jnp.concatenate(parts, axis=1) * gate


_HG_SLICES = [slice(h * HG_DK, (h + 1) * HG_DK) for h in range(HG_HEADS)]


def _state_update(iv, kd):
    return jnp.concatenate([_dot_tn(iv[:, hs], kd[:, hs]) for hs in _HG_SLICES], axis=1)


def _meta_state(mk, mlogf, mi):
    b = _cumsum_rows(mlogf)
    kd = (mk * jnp.exp(b[-1:] - b)).astype(BF16)
    return _state_update(mi.astype(BF16), kd)


class _Hgrn:
    def __init__(self, q_ref, k_ref, logf_ref, i_ref, g_ref, nw, state, o_ref):
        self.q_ref, self.k_ref, self.logf_ref, self.i_ref, self.g_ref = q_ref, k_ref, logf_ref, i_ref, g_ref
        self.nw, self.state, self.o_ref = nw, state, o_ref
        t = HG_CHUNK
        self.nchunk = q_ref.shape[0] // t
        r = lax.broadcasted_iota(jnp.int32, (t, t), 0)
        c = lax.broadcasted_iota(jnp.int32, (t, t), 1)
        self.causal = r >= c
        self.cums, self.terms, self.entering = {}, {}, {}

    def rows(self, ci):
        return slice(ci * HG_CHUNK, (ci + 1) * HG_CHUNK)

    def issue_cumsum(self, ci):
        self.cums[ci] = _cumsum_rows(self.logf_ref[self.rows(ci), :])

    def issue_local(self, ci):
        rows = self.rows(ci)
        b = self.cums.pop(ci)
        bl = b[-1:]
        k = self.k_ref[rows, :].astype(F32)
        qe = (self.q_ref[rows, :].astype(F32) * jnp.exp(b)).astype(BF16)
        ke = (k * jnp.exp(-b)).astype(BF16)
        kd = (k * jnp.exp(bl - b)).astype(BF16)
        iv = self.i_ref[rows, :].astype(BF16)
        a = [jnp.where(self.causal, _dot_nt(qe[:, hs], ke[:, hs]), 0.0).astype(BF16) for hs in _HG_SLICES]
        self.terms[ci] = (qe, iv, a)
        self.entering[ci] = self.state.astype(BF16)
        self.state = self.state * jnp.exp(bl) + _state_update(iv, kd)

    def finish(self, ci):
        rows = self.rows(ci)
        qe, iv, a = self.terms.pop(ci)
        ent = self.entering.pop(ci)
        outs = [_dot(a[h], iv[:, hs]) + _dot_nt(qe[:, hs], ent[:, hs]) for h, hs in enumerate(_HG_SLICES)]
        o = _hg_out(jnp.concatenate(outs, axis=1), self.g_ref[rows, :].astype(F32), self.nw)
        self.o_ref[rows, :] = o.astype(self.o_ref.dtype)


def _load_as_bf16(w_hbm, w_vmem, stage, sem, meanwhile):
    nslot, rows = stage.shape[0], stage.shape[1]
    nslab = w_hbm.shape[0] // rows
    ahead = nslot - 1

    def slab_copy(c):
        return pltpu.make_async_copy(w_hbm.at[pl.ds(c * rows, rows), :], stage.at[c % nslot],
                                     sem.at[c % nslot])

    for c in range(min(ahead, nslab)):
        slab_copy(c).start()
    meanwhile()
    for c in range(nslab):
        if c + ahead < nslab:
            slab_copy(c + ahead).start()
        slab_copy(c).wait()
        w_vmem[pl.ds(c * rows, rows), :] = stage[c % nslot].astype(w_vmem.dtype)


def _mixer_kernel(sink_ref, x0_ref, xnext_ref, xs_ref, xm_ref, g_ref, lbp_ref, w_hbm, nw_ref,
                  wa32, wb32, wo32, wup32, wdn32,
                  att_ref, hg_ref, sga_ref, sgb_ref, sfin_ref, lastkv_ref,
                  qs_ref, kvs_ref, hqs_ref, hks_ref, his_ref, hgs_ref, sgas_ref, sgbs_ref,
                  wa16, wb16, wo16, wup16, wdn16,
                  w_ref, stage, sem, xn_s, zq, zkv, zhq, zhk, zlogf, zhi, zhg, kvm_ref, st_ref, mst_ref,
                  *, tiles_per_seq):
    s = pl.program_id(0)
    t = xnext_ref.shape[0]

    @pl.when(s == 0)
    def _():
        def first_tile_and_zeros():
            xn_s[...] = _rmsnorm(x0_ref[...], g_ref[...]).astype(xn_s.dtype)
            for ref in (zq, zkv, zhq, zhk, zlogf, zhi, zhg, st_ref):
                ref[...] = jnp.zeros(ref.shape, ref.dtype)

        _load_as_bf16(w_hbm, w_ref, stage, sem, first_tile_and_zeros)
        xs = _rows_from_linear(xs_ref)
        nb = xs.shape[0]
        small = jnp.concatenate([xs, xm_ref[...]], axis=0)
        p = _Projection(_rmsnorm(small, g_ref[...]).astype(BF16), lbp_ref[...], w_ref)
        q_small = p.q_att()
        low = lax.broadcasted_iota(jnp.int32, (1, LANES), 1) < HEAD_DIM
        for j in range(Q_HEADS):
            kv_head, half = j // (Q_HEADS // KV_HEADS), j % 2
            pair = q_small[:, (j // 2) * LANES:(j // 2 + 1) * LANES]
            moved = pair if half == kv_head else pltpu.roll(pair, HEAD_DIM, axis=1)
            qs_ref[:, j, :] = jnp.where(low if kv_head == 0 else jnp.logical_not(low), moved, 0.0)
        kv = p.kv()
        kvs_ref[...] = kv
        kvm_ref[...] = kv[nb:]
        hqs_ref[...] = p.q_hg()
        k, logf = p.forget()
        hks_ref[...] = k
        iv = p.i_hg()
        his_ref[...] = iv
        hgs_ref[...] = p.swish_gate()
        sgas_ref[...] = p.branch_gate(C_GA).astype(sgas_ref.dtype)
        sgbs_ref[...] = p.branch_gate(C_GB).astype(sgbs_ref.dtype)
        mst_ref[...] = _meta_state(k[nb:], logf[nb:], iv[nb:])

    first = lax.rem(jnp.maximum(s - 1, 0), tiles_per_seq) == 0

    att = _Attention(sink_ref, zq, jnp.concatenate([zkv[...], kvm_ref[...]], axis=0), first, att_ref)
    last = zkv[t:t + WINDOW, :]
    lastkv_ref[...] = last
    zkv[0:WINDOW, :] = last
    entering = jnp.where(first, mst_ref[...], st_ref[...])
    hg = _Hgrn(zhq, zhk, zlogf, zhi, zhg, nw_ref[...], entering, hg_ref)
    proj = _Projection(xn_s, lbp_ref[...], w_ref)

    def gate_a(p):
        sga_ref[:, p] = proj.branch_gate(C_GA, p).astype(sga_ref.dtype)

    def gate_b(p):
        sgb_ref[:, p] = proj.branch_gate(C_GB, p).astype(sgb_ref.dtype)

    def new_kv(_):
        zkv[WINDOW:WINDOW + t, :] = proj.kv()

    def new_forget(p):
        k_new, logf_new = proj.forget(p)
        zhk[:, p] = k_new.astype(zhk.dtype)
        zlogf[:, p] = logf_new

    def new_q(p):
        zq[:, p] = proj.q_att(p).astype(zq.dtype)

    def new_hq(p):
        zhq[:, p] = proj.q_hg(p).astype(zhq.dtype)

    def new_hi(p):
        zhi[:, p] = proj.i_hg(p).astype(zhi.dtype)

    def new_gate(p):
        zhg[:, p] = proj.swish_gate(p).astype(zhg.dtype)

    def parts(n):
        return [slice(c, c + PROJ_GROUP) for c in range(0, n, PROJ_GROUP)]

    groups = ([(gate_a, p) for p in parts(D_MODEL)] + [(gate_b, p) for p in parts(D_MODEL)]
              + [(new_kv, None)] + [(new_q, p) for p in parts(D_ATT)]
              + [(new_forget, p) for p in parts(D_HG)] + [(new_hq, p) for p in parts(D_HG)]
              + [(new_hi, p) for p in parts(D_HG)] + [(new_gate, p) for p in parts(D_HG)])

    def save_state():
        st_ref[...] = hg.state
        for h, hs in enumerate(_HG_SLICES):
            sfin_ref[h] = hg.state[:, hs].T

    qk, cs, loc, fin, out = att.issue_scores, hg.issue_cumsum, hg.issue_local, att.finish, hg.finish
    mixer_work = {
        0: [(qk, 0), (cs, 0), (cs, 1), (cs, 2), (cs, 3)],
        1: [(cs, 4), (cs, 5), (loc, 0)],
        2: [(cs, 6), (cs, 7), (loc, 1)],
        3: [(qk, 1), (loc, 2)],
        4: [(fin, 0), (loc, 3)],
        5: [(out, 0), (loc, 4)],
        6: [(qk, 2), (out, 1), (loc, 5)],
        7: [(fin, 1), (out, 2), (loc, 6)],
        8: [(qk, 3), (out, 3), (loc, 7), (save_state,)],
        9: [(out, 4)],
        10: [(fin, 2), (out, 5)],
        11: [(out, 6)],
        12: [(out, 7)],
        13: [(fin, 3)],
    }
    for slot, (fn, p) in enumerate(groups):
        fn(p)
        for item in mixer_work.get(slot, []):
            item[0](*item[1:])
    xn_s[...] = _rmsnorm(xnext_ref[...], g_ref[...]).astype(xn_s.dtype)
    for src, dst in ((wa32, wa16), (wb32, wb16), (wo32, wo16), (wup32, wup16), (wdn32, wdn16)):
        dst[...] = src[...].astype(dst.dtype)


def _mixers(sinks, x, x_sample, x_meta, g, lb_param, w_f32, hg_norm, later_weights, bsz):
    n = x.shape[0]
    t = MIX_ROWS
    nt = n // t
    per_seq = nt // bsz
    rows = x_sample.shape[0] * LANES // D_MODEL + x_meta.shape[0]

    def this_tile(c):
        return pl.BlockSpec((t, c), lambda s: (jnp.minimum(s, nt - 1), 0))

    def prev_tile(c):
        return pl.BlockSpec((t, c), lambda s: (jnp.maximum(s - 1, 0), 0))

    def prev_seq(shape):
        return pl.BlockSpec((None,) + shape,
                            lambda s: (jnp.maximum(s - 1, 0) // per_seq,) + (0,) * len(shape))

    def small(c):
        return pl.BlockSpec((rows, c), lambda s: (0, 0))

    first_tile = pl.BlockSpec((t, D_MODEL), lambda s: (0, 0), pipeline_mode=pl.Buffered(1))
    next_tile = pl.BlockSpec((t, D_MODEL), lambda s: (jnp.minimum(s + 1, nt - 1), 0))
    small_widths = (2 * D_KV, D_HG, D_HG, D_HG, D_HG, D_MODEL, D_MODEL)
    small_dtypes = (F32, F32, F32, F32, F32, BF16, BF16)
    qm_shape = (rows, Q_HEADS, LANES)
    qm_spec = pl.BlockSpec(qm_shape, lambda s: (0, 0, 0))

    def row_block(wt):
        return pl.BlockSpec((wt.shape[0] // nt, wt.shape[1]), lambda s: (jnp.minimum(s, nt - 1), 0))

    return pl.pallas_call(
        functools.partial(_mixer_kernel, tiles_per_seq=per_seq),
        grid=(nt + 1,),
        in_specs=[pl.BlockSpec(memory_space=pltpu.SMEM), first_tile, next_tile,
                  _resident(x_sample.shape), _resident(x_meta.shape), _resident((1, D_MODEL)),
                  _resident(lb_param.shape), pl.BlockSpec(memory_space=pl.ANY), _resident((1, HG_DV))]
                 + [row_block(wt) for wt in later_weights],
        out_specs=[prev_tile(D_ATT), prev_tile(D_HG), this_tile(D_MODEL), this_tile(D_MODEL),
                   prev_seq((HG_HEADS, HG_DK, HG_DV)), prev_seq((WINDOW, 2 * D_KV))]
                  + [qm_spec] + [small(c) for c in small_widths] + [row_block(wt) for wt in later_weights],
        out_shape=[jax.ShapeDtypeStruct((n, D_ATT), BF16), jax.ShapeDtypeStruct((n, D_HG), BF16),
                   jax.ShapeDtypeStruct((n, D_MODEL), BF16), jax.ShapeDtypeStruct((n, D_MODEL), BF16),
                   jax.ShapeDtypeStruct((bsz, HG_HEADS, HG_DK, HG_DV), F32),
                   jax.ShapeDtypeStruct((bsz, WINDOW, 2 * D_KV), F32)]
                  + [jax.ShapeDtypeStruct(qm_shape, F32)]
                  + [jax.ShapeDtypeStruct((rows, c), d) for c, d in zip(small_widths, small_dtypes)]
                  + [jax.ShapeDtypeStruct(wt.shape, BF16) for wt in later_weights],
        scratch_shapes=[pltpu.VMEM((D_MODEL, D_IN), BF16),
                        pltpu.VMEM((WEIGHT_SLOTS, WEIGHT_SLAB_ROWS, D_IN), F32),
                        pltpu.SemaphoreType.DMA((WEIGHT_SLOTS,)),
                        pltpu.VMEM((t, D_MODEL), BF16),
                        pltpu.VMEM((t, D_ATT), BF16), pltpu.VMEM((WINDOW + t, 2 * D_KV), F32),
                        pltpu.VMEM((t, D_HG), BF16), pltpu.VMEM((t, D_HG), BF16),
                        pltpu.VMEM((t, D_HG), F32), pltpu.VMEM((t, D_HG), BF16),
                        pltpu.VMEM((t, D_HG), BF16),
                        pltpu.VMEM((N_META, 2 * D_KV), F32),
                        pltpu.VMEM((HG_DV, D_HG), F32),
                        pltpu.VMEM((HG_DV, D_HG), F32)],
        compiler_params=_params("arbitrary"),
        name="mixers",
    )(sinks, x, x, x_sample, x_meta, g, lb_param, w_f32, hg_norm, *later_weights)


def _merge_ffn_rows(x, att, hg, sga, sgb, wa_ref, wb_ref, wo_ref, ln_ffn, wup_ref, wdn_ref, ln_f):
    ya = _dot(att.astype(BF16), wa_ref[...])
    yb = _dot(hg.astype(BF16), wb_ref[...])
    mix = sga.astype(F32) * ya + sgb.astype(F32) * yb
    h1 = x + _dot(mix.astype(BF16), wo_ref[...])
    xn = _rmsnorm(h1, ln_ffn).astype(BF16)
    acc = jnp.zeros_like(h1)
    for c in range(0, D_FF, FFN_CHUNK):
        u = jnp.maximum(_dot(xn, wup_ref[:, c:c + FFN_CHUNK]), 0.0)
        acc = acc + _dot((u * u).astype(BF16), wdn_ref[c:c + FFN_CHUNK, :])
    return _rmsnorm(h1 + acc, ln_f)


def _sample_hgrn_group(rows, nw, q_ref, k_ref, i_ref, g_ref, s_ref, snew_ref, hg_ref):
    k = k_ref[rows, :]
    q = q_ref[rows, :]
    iv = i_ref[rows, :]
    r = lax.broadcasted_iota(jnp.int32, (HG_DK, HG_DK), 0)
    c = lax.broadcasted_iota(jnp.int32, (HG_DK, HG_DK), 1)
    eye = r == c

    def column(row):
        return jnp.sum(jnp.where(eye, row, 0.0), axis=1, keepdims=True)

    outs = []
    for b in range(s_ref.shape[0]):
        heads = []
        for h, hs in enumerate(_HG_SLICES):
            kc = column(k[b:b + 1, hs])
            qc = column(q[b:b + 1, hs])
            s_old = s_ref[b, h]
            s = s_old + kc * (iv[b:b + 1, hs] - s_old)
            snew_ref[b, h] = s
            heads.append(jnp.sum(qc * s, axis=0, keepdims=True))
        outs.append(jnp.concatenate(heads, axis=1))
    hg_ref[rows, :] = _hg_out(jnp.concatenate(outs, axis=0), g_ref[rows, :], nw)


def _merge_ffn_kernel(x_ref, att_ref, hg_ref, sga_ref, sgb_ref,
                      xs_ref, atts_ref, sgas_ref, sgbs_ref,
                      nw_ref, hqs_ref, hks_ref, his_ref, hgates_ref, state_ref,
                      kvs_ref, ckt_ref, cvt_ref,
                      wa_ref, wb_ref, wo_ref, wup_ref, wdn_ref, lnffn_ref, lnf_ref,
                      y_ref, ys_ref, snew_ref, nkt_ref, nvt_ref, hgs, *, prompt_steps):
    i = pl.program_id(0)
    weights = (wa_ref, wb_ref, wo_ref, lnffn_ref[...], wup_ref, wdn_ref, lnf_ref[...])
    group = state_ref.shape[0]
    ngroups = atts_ref.shape[0] // group

    @pl.when(i < prompt_steps)
    def _():
        y_ref[...] = _merge_ffn_rows(x_ref[...], att_ref[...], hg_ref[...], sga_ref[...], sgb_ref[...],
                                     *weights)
        rows = pl.ds(pl.multiple_of(jnp.minimum(i, ngroups - 1) * group, group), group)
        _sample_hgrn_group(rows, nw_ref[...], hqs_ref, hks_ref, his_ref, hgates_ref,
                           state_ref, snew_ref, hgs)
        kvn = kvs_ref[rows, :]
        r = lax.broadcasted_iota(jnp.int32, (D_KV, WINDOW), 0)
        c = lax.broadcasted_iota(jnp.int32, (D_KV, WINDOW), 1)
        for b in range(group):
            for src, dst, lo in ((ckt_ref, nkt_ref, 0), (cvt_ref, nvt_ref, D_KV)):
                col = jnp.sum(jnp.where(r == c, kvn[b:b + 1, lo:lo + D_KV], 0.0), axis=1, keepdims=True)
                dst[b] = jnp.where(c == WINDOW - 1, col, pltpu.roll(src[b], WINDOW - 1, axis=1))

    @pl.when(i == prompt_steps)
    def _():
        _rows_to_linear(ys_ref, _merge_ffn_rows(_rows_from_linear(xs_ref), atts_ref[...], hgs[...],
                                                sgas_ref[...], sgbs_ref[...], *weights))


def _merge_ffn(x, att, hg, sga, sgb, xs, att_s, sga_s, sgb_s, hg_norm, hq_s, hk_s, hi_s, hgate_s, state,
               kv_s, ck_t, cv_t, wa, wb, wo, ln_ffn, w_up, w_down, ln_f):
    n = x.shape[0]
    nb = att_s.shape[0]
    linear = pl.BlockSpec(xs.shape, lambda i: (0, 0))
    rows = MERGE_ROWS
    nt = n // rows
    g = SAMPLE_HG_GROUP
    assert nb // g <= nt

    def blk(c):
        return pl.BlockSpec((rows, c), lambda i: (jnp.minimum(i, nt - 1), 0))

    def sample(c):
        return pl.BlockSpec((nb, c), lambda i: (0, 0))

    sblk = pl.BlockSpec((g, HG_HEADS, HG_DK, HG_DV), lambda i: (jnp.minimum(i, nb // g - 1), 0, 0, 0))
    cblk = pl.BlockSpec((g, D_KV, WINDOW), lambda i: (jnp.minimum(i, nb // g - 1), 0, 0))
    return pl.pallas_call(
        functools.partial(_merge_ffn_kernel, prompt_steps=nt),
        grid=(nt + 1,),
        in_specs=[blk(D_MODEL), blk(D_ATT), blk(D_HG), blk(D_MODEL), blk(D_MODEL),
                  linear, sample(D_ATT), sample(D_MODEL), sample(D_MODEL),
                  _resident((1, HG_DV)), _resident(hq_s.shape), _resident(hk_s.shape),
                  _resident(hi_s.shape), _resident(hgate_s.shape), sblk,
                  _resident(kv_s.shape), cblk, cblk,
                  _resident(wa.shape), _resident(wb.shape), _resident(wo.shape),
                  _resident(w_up.shape), _resident(w_down.shape),
                  _resident((1, D_MODEL)), _resident((1, D_MODEL))],
        out_specs=[blk(D_MODEL), linear, sblk, cblk, cblk],
        out_shape=[jax.ShapeDtypeStruct((n, D_MODEL), F32), jax.ShapeDtypeStruct(xs.shape, F32),
                   jax.ShapeDtypeStruct(state.shape, F32),
                   jax.ShapeDtypeStruct(ck_t.shape, F32), jax.ShapeDtypeStruct(cv_t.shape, F32)],
        scratch_shapes=[pltpu.VMEM((nb, D_HG), F32)],
        compiler_params=_params("arbitrary"),
        name="merge_ffn",
    )(x, att, hg, sga, sgb, xs, att_s, sga_s, sgb_s, hg_norm, hq_s, hk_s, hi_s, hgate_s, state,
      kv_s, ck_t, cv_t, wa, wb, wo, w_up, w_down, ln_ffn, ln_f)


def _sample_attn_kernel(sink_ref, qm_ref, kvn_ref, ckt_ref, cvt_ref, mkt_ref, mvt_ref,
                        att_ref, o_all, mk_s, mv_s):
    nb = qm_ref.shape[0]
    first_seq = pl.multiple_of(pl.program_id(0) * nb, nb)

    @pl.when(pl.program_id(0) == 0)
    def _():
        for r in range(N_META):
            mk_s[:, r, :] = mkt_ref[r].T
            mv_s[:, r, :] = mvt_ref[r].T
    head = lax.broadcasted_iota(jnp.int32, (Q_HEADS, 1), 0)
    sk = jnp.zeros((Q_HEADS, 1), F32)
    for j in range(Q_HEADS):
        sk = jnp.where(head == j, sink_ref[j], sk)
    oldest = lax.broadcasted_iota(jnp.int32, (Q_HEADS, WINDOW), 1) == 0
    kvn = kvn_ref[...]
    qms = [qm_ref[b] for b in range(nb)]
    scores = [(_dot(qms[b].astype(BF16), ckt_ref[b].astype(BF16)),
               _dot_nt(qms[b].astype(BF16), mk_s[first_seq + b].astype(BF16)))
              for b in range(nb)]
    for b, (s_w, s_m) in enumerate(scores):
        s_w = jnp.where(oldest, -jnp.inf, s_w)
        s_n = jnp.sum(qms[b] * kvn[b:b + 1, :D_KV], axis=1, keepdims=True)
        m = jnp.maximum(jnp.maximum(jnp.max(s_w, axis=1, keepdims=True),
                                    jnp.max(s_m, axis=1, keepdims=True)), jnp.maximum(s_n, sk))
        e_w = jnp.exp(s_w - m)
        e_m = jnp.exp(s_m - m)
        e_n = jnp.exp(s_n - m)
        l = (jnp.sum(e_w, axis=1, keepdims=True) + jnp.sum(e_m, axis=1, keepdims=True) + e_n
             + jnp.exp(sk - m))
        o = (_dot_nt(e_w.astype(BF16), cvt_ref[b].astype(BF16))
             + _dot(e_m.astype(BF16), mv_s[first_seq + b].astype(BF16))
             + e_n * kvn[b:b + 1, D_KV:])
        o_all[b] = o / l
    low = lax.broadcasted_iota(jnp.int32, (1, LANES), 1) < HEAD_DIM
    pairs = []
    for p in range(Q_HEADS // 2):
        halves = []
        for half in range(2):
            j = 2 * p + half
            kv_head = j // (Q_HEADS // KV_HEADS)
            oj = jnp.where(low if kv_head == 0 else jnp.logical_not(low), o_all[:, j, :], 0.0)
            halves.append(oj if half == kv_head else pltpu.roll(oj, HEAD_DIM, axis=1))
        pairs.append(halves[0] + halves[1])
    att_ref[...] = jnp.concatenate(pairs, axis=1)


def _sample_attention(sinks, qm, kv_new, ck, cv, mk, mv):
    nb = ck.shape[0]
    g = SAMPLE_ATT_GROUP

    def blk3(a, c):
        return pl.BlockSpec((g, a, c), lambda i: (i, 0, 0))

    return pl.pallas_call(
        _sample_attn_kernel,
        grid=(nb // g,),
        in_specs=[pl.BlockSpec(memory_space=pltpu.SMEM), blk3(Q_HEADS, D_KV),
                  pl.BlockSpec((g, 2 * D_KV), lambda i: (i, 0)),
                  blk3(D_KV, WINDOW), blk3(D_KV, WINDOW), _resident(mk.shape), _resident(mv.shape)],
        out_specs=pl.BlockSpec((g, D_ATT), lambda i: (i, 0)),
        out_shape=jax.ShapeDtypeStruct((nb, D_ATT), F32),
        scratch_shapes=[pltpu.VMEM((g, Q_HEADS, LANES), F32),
                        pltpu.VMEM((nb, N_META, D_KV), F32), pltpu.VMEM((nb, N_META, D_KV), F32)],
        compiler_params=_params("arbitrary"),
        name="sample_attn",
    )(sinks, qm, kv_new, ck, cv, mk, mv)


def kernel(x_prompt, x_sample, cache_k, cache_v, cache_meta_k, cache_meta_v, state_hgrn, meta,
           w_in, sinks, lb_param, hg_norm, w_att_out, w_hg_out, w_o, ln_mix, ln_ffn, w_up,
           w_down, ln_f):
    bsz, seq, _ = x_prompt.shape
    nb = x_sample.shape[0]
    ln_mix2 = ln_mix.reshape(1, D_MODEL)
    ln_ffn2 = ln_ffn.reshape(1, D_MODEL)
    ln_f2 = ln_f.reshape(1, D_MODEL)
    nw = hg_norm.reshape(1, HG_DV)

    xs = x_sample.reshape(nb * D_MODEL // LANES, LANES)

    xp = x_prompt.reshape(bsz * seq, D_MODEL)
    (att_p, hg_p, sga_p, sgb_p, state_p, lastkv_p,
     qm_s, kv_s, hq_s, hk_s, hi_s, hgate_s, sga_s, sgb_s,
     wa, wb, wo, wup, wdn) = _mixers(
        sinks.reshape(Q_HEADS), xp, xs, meta, ln_mix2, lb_param, w_in[0], nw,
        (w_att_out[0], w_hg_out[0], w_o[0], w_up[0], w_down[0]), bsz)

    def window_t(c):
        return jnp.swapaxes(c[0].reshape(nb, WINDOW, D_KV), 1, 2)

    def meta_t(c):
        return jnp.transpose(c[0].reshape(nb, N_META, D_KV), (1, 2, 0))

    ck_t, cv_t = window_t(cache_k), window_t(cache_v)
    att_s = _sample_attention(sinks.reshape(Q_HEADS), qm_s, kv_s, ck_t, cv_t,
                              meta_t(cache_meta_k), meta_t(cache_meta_v))

    y_p, y_s, state_s, nkt_s, nvt_s = _merge_ffn(
        xp, att_p, hg_p, sga_p, sgb_p, xs, att_s, sga_s, sgb_s,
        nw, hq_s, hk_s, hi_s, hgate_s, state_hgrn[0], kv_s, ck_t, cv_t,
        wa, wb, wo, ln_ffn2, wup, wdn, ln_f2)

    kv5 = lastkv_p.reshape(bsz, WINDOW, 2, KV_HEADS, HEAD_DIM)
    meta5 = jnp.broadcast_to(kv_s[nb:].reshape(1, N_META, 2, KV_HEADS, HEAD_DIM),
                             (bsz, N_META, 2, KV_HEADS, HEAD_DIM))
    return (y_p.reshape(bsz, seq, D_MODEL),
            y_s.reshape(nb, 1, D_MODEL),
            kv5[None, :, :, 0],
            kv5[None, :, :, 1],
            meta5[None, :, :, 0],
            meta5[None, :, :, 1],
            state_p[None],
            jnp.swapaxes(nkt_s, 1, 2).reshape(1, nb, WINDOW, KV_HEADS, HEAD_DIM),
            jnp.swapaxes(nvt_s, 1, 2).reshape(1, nb, WINDOW, KV_HEADS, HEAD_DIM),
            state_s[None])
```

```python
import functools

import jax
import jax.numpy as jnp
from jax import lax
from jax.experimental import pallas as pl
from jax.experimental.pallas import tpu as pltpu

F32 = jnp.float32
BF16 = jnp.bfloat16

D_MODEL = 1024
N_META = 16
WINDOW = 128
HEAD_DIM = 64
Q_HEADS = 8
KV_HEADS = 2
D_ATT = Q_HEADS * HEAD_DIM
D_KV = KV_HEADS * HEAD_DIM
HG_HEADS = 4
HG_DK = 128
HG_DV = 128
D_HG = HG_HEADS * HG_DK
HG_CHUNK = 64
D_FF = 4 * D_MODEL
EPS = 1e-6
C_Q = 0
C_KV = C_Q + D_ATT
C_HQ = C_KV + 2 * D_KV
C_HF = C_HQ + D_HG
C_HI = C_HF + D_HG
C_HGATE = C_HI + D_HG
C_GA = C_HGATE + D_HG
C_GB = C_GA + D_MODEL
D_IN = C_GB + D_MODEL

VMEM_LIMIT_BYTES = 56 * 1024 * 1024
MIX_ROWS = 512
PROJ_GROUP = 256
WEIGHT_SLAB_ROWS = 64
WEIGHT_SLOTS = 8
MERGE_ROWS = 512
FFN_CHUNK = 2048
LANES = 128
BF16_ROWS = 16
SAMPLE_ATT_GROUP = 32
SAMPLE_HG_GROUP = 8

_NT = (((1,), (1,)), ((), ()))
_TN = (((0,), (0,)), ((), ()))


def _dot(a, b):
    return jnp.dot(a, b, preferred_element_type=F32)


def _dot_nt(a, b):
    return lax.dot_general(a, b, _NT, preferred_element_type=F32)


def _dot_tn(a, b):
    return lax.dot_general(a, b, _TN, preferred_element_type=F32)


def _rows_from_linear(ref):
    per = D_MODEL // LANES
    rows = ref.shape[0] // per
    return jnp.concatenate([ref[pl.ds(c, rows, stride=per), :] for c in range(per)], axis=1)


def _rows_to_linear(ref, value):
    per = D_MODEL // LANES
    for c in range(per):
        ref[pl.ds(c, value.shape[0], stride=per), :] = value[:, c * LANES:(c + 1) * LANES]


def _rmsnorm(x, g):
    return x * lax.rsqrt(jnp.mean(x * x, axis=-1, keepdims=True) + EPS) * g


def _resident(shape):
    return pl.BlockSpec(shape, lambda *_: (0,) * len(shape), pipeline_mode=pl.Buffered(1))


def _params(*sem):
    return pltpu.CompilerParams(dimension_semantics=sem, vmem_limit_bytes=VMEM_LIMIT_BYTES)


def _lower_bound(lbp):
    e = jnp.exp(lbp - jnp.max(lbp, axis=0, keepdims=True))
    return e[0:1] / jnp.sum(e, axis=0, keepdims=True)


class _Projection:
    def __init__(self, xn, lbp, w_ref):
        self.xn = xn
        self.w_ref = w_ref
        self.lb = _lower_bound(lbp)

    def cols(self, base, part):
        return _dot(self.xn[...], self.w_ref[:, base + part.start:base + part.stop])

    def q_att(self, part=slice(0, D_ATT)):
        return self.cols(C_Q, part) * (HEAD_DIM ** -0.5)

    def kv(self):
        return self.cols(C_KV, slice(0, 2 * D_KV))

    def q_hg(self, part=slice(0, D_HG)):
        return self.cols(C_HQ, part) * (HG_DK ** -0.5)

    def forget(self, part=slice(0, D_HG)):
        lb = self.lb[:, part]
        f = lb + (1.0 - lb) * jax.nn.sigmoid(self.cols(C_HF, part))
        return 1.0 - f, jnp.log(f)

    def i_hg(self, part=slice(0, D_HG)):
        return self.cols(C_HI, part)

    def swish_gate(self, part=slice(0, D_HG)):
        g = self.cols(C_HGATE, part)
        return g * jax.nn.sigmoid(g)

    def branch_gate(self, base, part=slice(0, D_MODEL)):
        return jax.nn.sigmoid(self.cols(base, part))


class _Attention:
    def __init__(self, sink_ref, q_ref, kv, first, o_ref):
        self.sink_ref, self.q_ref, self.first, self.o_ref = sink_ref, q_ref, first, o_ref
        self.nsub = q_ref.shape[0] // WINDOW
        self.nk = 2 * WINDOW + N_META
        self.meta0 = WINDOW + q_ref.shape[0]
        lane = lax.broadcasted_iota(jnp.int32, (kv.shape[0], D_KV), 1)
        low = lane < HEAD_DIM
        k = kv[:, :D_KV]
        ksw = pltpu.roll(k, HEAD_DIM, axis=1)
        self.kboth = (jnp.where(low, k, ksw).astype(BF16), jnp.where(low, ksw, k).astype(BF16))
        qlane = lax.broadcasted_iota(jnp.int32, (1, LANES), 1)
        self.keep = (jnp.where(qlane < HEAD_DIM, 1.0, 0.0).astype(BF16),
                     jnp.where(qlane < HEAD_DIM, 0.0, 1.0).astype(BF16))
        r = lax.broadcasted_iota(jnp.int32, (D_KV, D_KV), 0)
        c = lax.broadcasted_iota(jnp.int32, (D_KV, D_KV), 1)
        eye = jnp.where(r == c, 1.0, 0.0).astype(BF16)
        self.vt = _dot_nt(eye, kv[:, D_KV:].astype(BF16)).astype(BF16)
        self.ones = jnp.ones((BF16_ROWS, self.nk), BF16)
        self.key = lax.broadcasted_iota(jnp.int32, (self.nk, 2 * WINDOW), 0)
        col = lax.broadcasted_iota(jnp.int32, (self.nk, 2 * WINDOW), 1)
        self.qry = jnp.bitwise_and(col, WINDOW - 1)
        self.first_head = lax.broadcasted_iota(jnp.int32, (1, 2 * WINDOW), 1) < WINDOW
        self.scores = {}

    def issue_scores(self, i):
        r0 = i * WINDOW
        out = []
        for p in range(Q_HEADS // 2):
            kb = self.kboth[(2 * p) // (Q_HEADS // KV_HEADS)]
            kmat = jnp.concatenate([kb[r0:r0 + 2 * WINDOW], kb[self.meta0:]], axis=0)
            qp = self.q_ref[r0:r0 + WINDOW, p * LANES:(p + 1) * LANES]
            q2 = jnp.concatenate([qp * self.keep[0], qp * self.keep[1]], axis=0)
            out.append(_dot_nt(kmat, q2))
        self.scores[i] = out

    def finish(self, i):
        r0 = i * WINDOW
        lo = jnp.where(self.first, WINDOW - 1, self.qry) if i == 0 else self.qry
        mask = jnp.logical_and(self.key > lo, self.key <= self.qry + WINDOW)
        mask = jnp.logical_or(mask, self.key >= 2 * WINDOW)
        vaug = []
        for h in range(KV_HEADS):
            vth = self.vt[h * HEAD_DIM:(h + 1) * HEAD_DIM]
            vaug.append(jnp.concatenate(
                [jnp.concatenate([vth[:, r0:r0 + 2 * WINDOW], vth[:, self.meta0:]], axis=1),
                 self.ones], axis=0))
        outs = []
        for p, raw in enumerate(self.scores.pop(i)):
            h = (2 * p) // (Q_HEADS // KV_HEADS)
            s = jnp.where(mask, raw, -jnp.inf)
            sk = jnp.where(self.first_head, self.sink_ref[2 * p], self.sink_ref[2 * p + 1])
            m = jnp.maximum(jnp.max(s, axis=0, keepdims=True), sk)
            e = jnp.exp(s - m).astype(BF16)
            oa = _dot(vaug[h], e)
            o = oa[:HEAD_DIM] / (oa[HEAD_DIM:HEAD_DIM + 1] + jnp.exp(sk - m))
            outs += [o[:, :WINDOW], o[:, WINDOW:]]
        self.o_ref[r0:r0 + WINDOW, :] = jnp.concatenate(outs, axis=0).T.astype(self.o_ref.dtype)


def _cumsum_rows(x):
    t = x.shape[0]
    row = lax.broadcasted_iota(jnp.int32, x.shape, 0)
    d = 1
    while d < t:
        x = x + jnp.where(row >= d, pltpu.roll(x, d, axis=0), 0.0)
        d *= 2
    return x


def _hg_out(o, gate, nw):
    parts = []
    for h in range(HG_HEADS):
        oh = o[:, h * HG_DV:(h + 1) * HG_DV]
        parts.append(oh * lax.rsqrt(jnp.mean(oh * oh, axis=-1, keepdims=True) + EPS) * nw)
    return jnp.concatenate(parts, axis=1) * gate


_HG_SLICES = [slice(h * HG_DK, (h + 1) * HG_DK) for h in range(HG_HEADS)]


def _state_update(iv, kd):
    return jnp.concatenate([_dot_tn(iv[:, hs], kd[:, hs]) for hs in _HG_SLICES], axis=1)


def _meta_state(mk, mlogf, mi):
    b = _cumsum_rows(mlogf)
    kd = (mk * jnp.exp(b[-1:] - b)).astype(BF16)
    return _state_update(mi.astype(BF16), kd)


class _Hgrn:
    def __init__(self, q_ref, k_ref, logf_ref, i_ref, g_ref, nw, state, o_ref):
        self.q_ref, self.k_ref, self.logf_ref, self.i_ref, self.g_ref = q_ref, k_ref, logf_ref, i_ref, g_ref
        self.nw, self.state, self.o_ref = nw, state, o_ref
        t = HG_CHUNK
        self.nchunk = q_ref.shape[0] // t
        r = lax.broadcasted_iota(jnp.int32, (t, t), 0)
        c = lax.broadcasted_iota(jnp.int32, (t, t), 1)
        self.causal = r >= c
        self.cums, self.terms, self.entering = {}, {}, {}

    def rows(self, ci):
        return slice(ci * HG_CHUNK, (ci + 1) * HG_CHUNK)

    def issue_cumsum(self, ci):
        self.cums[ci] = _cumsum_rows(self.logf_ref[self.rows(ci), :])

    def issue_local(self, ci):
        rows = self.rows(ci)
        b = self.cums.pop(ci)
        bl = b[-1:]
        k = self.k_ref[rows, :].astype(F32)
        qe = (self.q_ref[rows, :].astype(F32) * jnp.exp(b)).astype(BF16)
        ke = (k * jnp.exp(-b)).astype(BF16)
        kd = (k * jnp.exp(bl - b)).astype(BF16)
        iv = self.i_ref[rows, :].astype(BF16)
        a = [jnp.where(self.causal, _dot_nt(qe[:, hs], ke[:, hs]), 0.0).astype(BF16) for hs in _HG_SLICES]
        self.terms[ci] = (qe, iv, a)
        self.entering[ci] = self.state.astype(BF16)
        self.state = self.state * jnp.exp(bl) + _state_update(iv, kd)

    def finish(self, ci):
        rows = self.rows(ci)
        qe, iv, a = self.terms.pop(ci)
        ent = self.entering.pop(ci)
        outs = [_dot(a[h], iv[:, hs]) + _dot_nt(qe[:, hs], ent[:, hs]) for h, hs in enumerate(_HG_SLICES)]
        o = _hg_out(jnp.concatenate(outs, axis=1), self.g_ref[rows, :].astype(F32), self.nw)
        self.o_ref[rows, :] = o.astype(self.o_ref.dtype)


def _load_as_bf16(w_hbm, w_vmem, stage, sem, meanwhile):
    nslot, rows = stage.shape[0], stage.shape[1]
    nslab = w_hbm.shape[0] // rows
    ahead = nslot - 1

    def slab_copy(c):
        return pltpu.make_async_copy(w_hbm.at[pl.ds(c * rows, rows), :], stage.at[c % nslot],
                                     sem.at[c % nslot])

    for c in range(min(ahead, nslab)):
        slab_copy(c).start()
    meanwhile()
    for c in range(nslab):
        if c + ahead < nslab:
            slab_copy(c + ahead).start()
        slab_copy(c).wait()
        w_vmem[pl.ds(c * rows, rows), :] = stage[c % nslot].astype(w_vmem.dtype)


def _mixer_kernel(sink_ref, x0_ref, xnext_ref, xs_ref, xm_ref, g_ref, lbp_ref, w_hbm, nw_ref,
                  wa32, wb32, wo32, wup32, wdn32,
                  att_ref, hg_ref, sga_ref, sgb_ref, sfin_ref, lastkv_ref,
                  qs_ref, kvs_ref, hqs_ref, hks_ref, his_ref, hgs_ref, sgas_ref, sgbs_ref,
                  wa16, wb16, wo16, wup16, wdn16,
                  w_ref, stage, sem, xn_s, zq, zkv, zhq, zhk, zlogf, zhi, zhg, kvm_ref, st_ref, mst_ref,
                  *, tiles_per_seq):
    s = pl.program_id(0)
    t = xnext_ref.shape[0]

    @pl.when(s == 0)
    def _():
        def first_tile_and_zeros():
            xn_s[...] = _rmsnorm(x0_ref[...], g_ref[...]).astype(xn_s.dtype)
            for ref in (zq, zkv, zhq, zhk, zlogf, zhi, zhg, st_ref):
                ref[...] = jnp.zeros(ref.shape, ref.dtype)

        _load_as_bf16(w_hbm, w_ref, stage, sem, first_tile_and_zeros)
        xs = _rows_from_linear(xs_ref)
        nb = xs.shape[0]
        small = jnp.concatenate([xs, xm_ref[...]], axis=0)
        p = _Projection(_rmsnorm(small, g_ref[...]).astype(BF16), lbp_ref[...], w_ref)
        q_small = p.q_att()
        low = lax.broadcasted_iota(jnp.int32, (1, LANES), 1) < HEAD_DIM
        for j in range(Q_HEADS):
            kv_head, half = j // (Q_HEADS // KV_HEADS), j % 2
            pair = q_small[:, (j // 2) * LANES:(j // 2 + 1) * LANES]
            moved = pair if half == kv_head else pltpu.roll(pair, HEAD_DIM, axis=1)
            qs_ref[:, j, :] = jnp.where(low if kv_head == 0 else jnp.logical_not(low), moved, 0.0)
        kv = p.kv()
        kvs_ref[...] = kv
        kvm_ref[...] = kv[nb:]
        hqs_ref[...] = p.q_hg()
        k, logf = p.forget()
        hks_ref[...] = k
        iv = p.i_hg()
        his_ref[...] = iv
        hgs_ref[...] = p.swish_gate()
        sgas_ref[...] = p.branch_gate(C_GA).astype(sgas_ref.dtype)
        sgbs_ref[...] = p.branch_gate(C_GB).astype(sgbs_ref.dtype)
        mst_ref[...] = _meta_state(k[nb:], logf[nb:], iv[nb:])

    first = lax.rem(jnp.maximum(s - 1, 0), tiles_per_seq) == 0

    att = _Attention(sink_ref, zq, jnp.concatenate([zkv[...], kvm_ref[...]], axis=0), first, att_ref)
    last = zkv[t:t + WINDOW, :]
    lastkv_ref[...] = last
    zkv[0:WINDOW, :] = last
    entering = jnp.where(first, mst_ref[...], st_ref[...])
    hg = _Hgrn(zhq, zhk, zlogf, zhi, zhg, nw_ref[...], entering, hg_ref)
    proj = _Projection(xn_s, lbp_ref[...], w_ref)

    def gate_a(p):
        sga_ref[:, p] = proj.branch_gate(C_GA, p).astype(sga_ref.dtype)

    def gate_b(p):
        sgb_ref[:, p] = proj.branch_gate(C_GB, p).astype(sgb_ref.dtype)

    def new_kv(_):
        zkv[WINDOW:WINDOW + t, :] = proj.kv()

    def new_forget(p):
        k_new, logf_new = proj.forget(p)
        zhk[:, p] = k_new.astype(zhk.dtype)
        zlogf[:, p] = logf_new

    def new_q(p):
        zq[:, p] = proj.q_att(p).astype(zq.dtype)

    def new_hq(p):
        zhq[:, p] = proj.q_hg(p).astype(zhq.dtype)

    def new_hi(p):
        zhi[:, p] = proj.i_hg(p).astype(zhi.dtype)

    def new_gate(p):
        zhg[:, p] = proj.swish_gate(p).astype(zhg.dtype)

    def parts(n):
        return [slice(c, c + PROJ_GROUP) for c in range(0, n, PROJ_GROUP)]

    groups = ([(gate_a, p) for p in parts(D_MODEL)] + [(gate_b, p) for p in parts(D_MODEL)]
              + [(new_kv, None)] + [(new_q, p) for p in parts(D_ATT)]
              + [(new_forget, p) for p in parts(D_HG)] + [(new_hq, p) for p in parts(D_HG)]
              + [(new_hi, p) for p in parts(D_HG)] + [(new_gate, p) for p in parts(D_HG)])

    def save_state():
        st_ref[...] = hg.state
        for h, hs in enumerate(_HG_SLICES):
            sfin_ref[h] = hg.state[:, hs].T

    qk, cs, loc, fin, out = att.issue_scores, hg.issue_cumsum, hg.issue_local, att.finish, hg.finish
    mixer_work = {
        0: [(qk, 0), (cs, 0), (cs, 1), (cs, 2), (cs, 3)],
        1: [(cs, 4), (cs, 5), (loc, 0)],
        2: [(cs, 6), (cs, 7), (loc, 1)],
        3: [(qk, 1), (loc, 2)],
        4: [(fin, 0), (loc, 3)],
        5: [(out, 0), (loc, 4)],
        6: [(qk, 2), (out, 1), (loc, 5)],
        7: [(fin, 1), (out, 2), (loc, 6)],
        8: [(qk, 3), (out, 3), (loc, 7), (save_state,)],
        9: [(out, 4)],
        10: [(fin, 2), (out, 5)],
        11: [(out, 6)],
        12: [(out, 7)],
        13: [(fin, 3)],
    }
    for slot, (fn, p) in enumerate(groups):
        fn(p)
        for item in mixer_work.get(slot, []):
            item[0](*item[1:])
    xn_s[...] = _rmsnorm(xnext_ref[...], g_ref[...]).astype(xn_s.dtype)
    for src, dst in ((wa32, wa16), (wb32, wb16), (wo32, wo16), (wup32, wup16), (wdn32, wdn16)):
        dst[...] = src[...].astype(dst.dtype)


def _mixers(sinks, x, x_sample, x_meta, g, lb_param, w_f32, hg_norm, later_weights, bsz):
    n = x.shape[0]
    t = MIX_ROWS
    nt = n // t
    per_seq = nt // bsz
    rows = x_sample.shape[0] * LANES // D_MODEL + x_meta.shape[0]

    def this_tile(c):
        return pl.BlockSpec((t, c), lambda s: (jnp.minimum(s, nt - 1), 0))

    def prev_tile(c):
        return pl.BlockSpec((t, c), lambda s: (jnp.maximum(s - 1, 0), 0))

    def prev_seq(shape):
        return pl.BlockSpec((None,) + shape,
                            lambda s: (jnp.maximum(s - 1, 0) // per_seq,) + (0,) * len(shape))

    def small(c):
        return pl.BlockSpec((rows, c), lambda s: (0, 0))

    first_tile = pl.BlockSpec((t, D_MODEL), lambda s: (0, 0), pipeline_mode=pl.Buffered(1))
    next_tile = pl.BlockSpec((t, D_MODEL), lambda s: (jnp.minimum(s + 1, nt - 1), 0))
    small_widths = (2 * D_KV, D_HG, D_HG, D_HG, D_HG, D_MODEL, D_MODEL)
    small_dtypes = (F32, F32, F32, F32, F32, BF16, BF16)
    qm_shape = (rows, Q_HEADS, LANES)
    qm_spec = pl.BlockSpec(qm_shape, lambda s: (0, 0, 0))

    def row_block(wt):
        return pl.BlockSpec((wt.shape[0] // nt, wt.shape[1]), lambda s: (jnp.minimum(s, nt - 1), 0))

    return pl.pallas_call(
        functools.partial(_mixer_kernel, tiles_per_seq=per_seq),
        grid=(nt + 1,),
        in_specs=[pl.BlockSpec(memory_space=pltpu.SMEM), first_tile, next_tile,
                  _resident(x_sample.shape), _resident(x_meta.shape), _resident((1, D_MODEL)),
                  _resident(lb_param.shape), pl.BlockSpec(memory_space=pl.ANY), _resident((1, HG_DV))]
                 + [row_block(wt) for wt in later_weights],
        out_specs=[prev_tile(D_ATT), prev_tile(D_HG), this_tile(D_MODEL), this_tile(D_MODEL),
                   prev_seq((HG_HEADS, HG_DK, HG_DV)), prev_seq((WINDOW, 2 * D_KV))]
                  + [qm_spec] + [small(c) for c in small_widths] + [row_block(wt) for wt in later_weights],
        out_shape=[jax.ShapeDtypeStruct((n, D_ATT), BF16), jax.ShapeDtypeStruct((n, D_HG), BF16),
                   jax.ShapeDtypeStruct((n, D_MODEL), BF16), jax.ShapeDtypeStruct((n, D_MODEL), BF16),
                   jax.ShapeDtypeStruct((bsz, HG_HEADS, HG_DK, HG_DV), F32),
                   jax.ShapeDtypeStruct((bsz, WINDOW, 2 * D_KV), F32)]
                  + [jax.ShapeDtypeStruct(qm_shape, F32)]
                  + [jax.ShapeDtypeStruct((rows, c), d) for c, d in zip(small_widths, small_dtypes)]
                  + [jax.ShapeDtypeStruct(wt.shape, BF16) for wt in later_weights],
        scratch_shapes=[pltpu.VMEM((D_MODEL, D_IN), BF16),
                        pltpu.VMEM((WEIGHT_SLOTS, WEIGHT_SLAB_ROWS, D_IN), F32),
                        pltpu.SemaphoreType.DMA((WEIGHT_SLOTS,)),
                        pltpu.VMEM((t, D_MODEL), BF16),
                        pltpu.VMEM((t, D_ATT), BF16), pltpu.VMEM((WINDOW + t, 2 * D_KV), F32),
                        pltpu.VMEM((t, D_HG), BF16), pltpu.VMEM((t, D_HG), BF16),
                        pltpu.VMEM((t, D_HG), F32), pltpu.VMEM((t, D_HG), BF16),
                        pltpu.VMEM((t, D_HG), BF16),
                        pltpu.VMEM((N_META, 2 * D_KV), F32),
                        pltpu.VMEM((HG_DV, D_HG), F32),
                        pltpu.VMEM((HG_DV, D_HG), F32)],
        compiler_params=_params("arbitrary"),
        name="mixers",
    )(sinks, x, x, x_sample, x_meta, g, lb_param, w_f32, hg_norm, *later_weights)


def _merge_ffn_rows(x, att, hg, sga, sgb, wa_ref, wb_ref, wo_ref, ln_ffn, wup_ref, wdn_ref, ln_f):
    ya = _dot(att.astype(BF16), wa_ref[...])
    yb = _dot(hg.astype(BF16), wb_ref[...])
    mix = sga.astype(F32) * ya + sgb.astype(F32) * yb
    h1 = x + _dot(mix.astype(BF16), wo_ref[...])
    xn = _rmsnorm(h1, ln_ffn).astype(BF16)
    acc = jnp.zeros_like(h1)
    for c in range(0, D_FF, FFN_CHUNK):
        u = jnp.maximum(_dot(xn, wup_ref[:, c:c + FFN_CHUNK]), 0.0)
        acc = acc + _dot((u * u).astype(BF16), wdn_ref[c:c + FFN_CHUNK, :])
    return _rmsnorm(h1 + acc, ln_f)


def _sample_hgrn_group(rows, nw, q_ref, k_ref, i_ref, g_ref, s_ref, snew_ref, hg_ref):
    k = k_ref[rows, :]
    q = q_ref[rows, :]
    iv = i_ref[rows, :]
    r = lax.broadcasted_iota(jnp.int32, (HG_DK, HG_DK), 0)
    c = lax.broadcasted_iota(jnp.int32, (HG_DK, HG_DK), 1)
    eye = r == c

    def column(row):
        return jnp.sum(jnp.where(eye, row, 0.0), axis=1, keepdims=True)

    outs = []
    for b in range(s_ref.shape[0]):
        heads = []
        for h, hs in enumerate(_HG_SLICES):
            kc = column(k[b:b + 1, hs])
            qc = column(q[b:b + 1, hs])
            s_old = s_ref[b, h]
            s = s_old + kc * (iv[b:b + 1, hs] - s_old)
            snew_ref[b, h] = s
            heads.append(jnp.sum(qc * s, axis=0, keepdims=True))
        outs.append(jnp.concatenate(heads, axis=1))
    hg_ref[rows, :] = _hg_out(jnp.concatenate(outs, axis=0), g_ref[rows, :], nw)


def _merge_ffn_kernel(x_ref, att_ref, hg_ref, sga_ref, sgb_ref,
                      xs_ref, atts_ref, sgas_ref, sgbs_ref,
                      nw_ref, hqs_ref, hks_ref, his_ref, hgates_ref, state_ref,
                      kvs_ref, ckt_ref, cvt_ref,
                      wa_ref, wb_ref, wo_ref, wup_ref, wdn_ref, lnffn_ref, lnf_ref,
                      y_ref, ys_ref, snew_ref, nkt_ref, nvt_ref, hgs, *, prompt_steps):
    i = pl.program_id(0)
    weights = (wa_ref, wb_ref, wo_ref, lnffn_ref[...], wup_ref, wdn_ref, lnf_ref[...])
    group = state_ref.shape[0]
    ngroups = atts_ref.shape[0] // group

    @pl.when(i < prompt_steps)
    def _():
        y_ref[...] = _merge_ffn_rows(x_ref[...], att_ref[...], hg_ref[...], sga_ref[...], sgb_ref[...],
                                     *weights)
        rows = pl.ds(pl.multiple_of(jnp.minimum(i, ngroups - 1) * group, group), group)
        _sample_hgrn_group(rows, nw_ref[...], hqs_ref, hks_ref, his_ref, hgates_ref,
                           state_ref, snew_ref, hgs)
        kvn = kvs_ref[rows, :]
        r = lax.broadcasted_iota(jnp.int32, (D_KV, WINDOW), 0)
        c = lax.broadcasted_iota(jnp.int32, (D_KV, WINDOW), 1)
        for b in range(group):
            for src, dst, lo in ((ckt_ref, nkt_ref, 0), (cvt_ref, nvt_ref, D_KV)):
                col = jnp.sum(jnp.where(r == c, kvn[b:b + 1, lo:lo + D_KV], 0.0), axis=1, keepdims=True)
                dst[b] = jnp.where(c == WINDOW - 1, col, pltpu.roll(src[b], WINDOW - 1, axis=1))

    @pl.when(i == prompt_steps)
    def _():
        _rows_to_linear(ys_ref, _merge_ffn_rows(_rows_from_linear(xs_ref), atts_ref[...], hgs[...],
                                                sgas_ref[...], sgbs_ref[...], *weights))


def _merge_ffn(x, att, hg, sga, sgb, xs, att_s, sga_s, sgb_s, hg_norm, hq_s, hk_s, hi_s, hgate_s, state,
               kv_s, ck_t, cv_t, wa, wb, wo, ln_ffn, w_up, w_down, ln_f):
    n = x.shape[0]
    nb = att_s.shape[0]
    linear = pl.BlockSpec(xs.shape, lambda i: (0, 0))
    rows = MERGE_ROWS
    nt = n // rows
    g = SAMPLE_HG_GROUP
    assert nb // g <= nt

    def blk(c):
        return pl.BlockSpec((rows, c), lambda i: (jnp.minimum(i, nt - 1), 0))

    def sample(c):
        return pl.BlockSpec((nb, c), lambda i: (0, 0))

    sblk = pl.BlockSpec((g, HG_HEADS, HG_DK, HG_DV), lambda i: (jnp.minimum(i, nb // g - 1), 0, 0, 0))
    cblk = pl.BlockSpec((g, D_KV, WINDOW), lambda i: (jnp.minimum(i, nb // g - 1), 0, 0))
    return pl.pallas_call(
        functools.partial(_merge_ffn_kernel, prompt_steps=nt),
        grid=(nt + 1,),
        in_specs=[blk(D_MODEL), blk(D_ATT), blk(D_HG), blk(D_MODEL), blk(D_MODEL),
                  linear, sample(D_ATT), sample(D_MODEL), sample(D_MODEL),
                  _resident((1, HG_DV)), _resident(hq_s.shape), _resident(hk_s.shape),
                  _resident(hi_s.shape), _resident(hgate_s.shape), sblk,
                  _resident(kv_s.shape), cblk, cblk,
                  _resident(wa.shape), _resident(wb.shape), _resident(wo.shape),
                  _resident(w_up.shape), _resident(w_down.shape),
                  _resident((1, D_MODEL)), _resident((1, D_MODEL))],
        out_specs=[blk(D_MODEL), linear, sblk, cblk, cblk],
        out_shape=[jax.ShapeDtypeStruct((n, D_MODEL), F32), jax.ShapeDtypeStruct(xs.shape, F32),
                   jax.ShapeDtypeStruct(state.shape, F32),
                   jax.ShapeDtypeStruct(ck_t.shape, F32), jax.ShapeDtypeStruct(cv_t.shape, F32)],
        scratch_shapes=[pltpu.VMEM((nb, D_HG), F32)],
        compiler_params=_params("arbitrary"),
        name="merge_ffn",
    )(x, att, hg, sga, sgb, xs, att_s, sga_s, sgb_s, hg_norm, hq_s, hk_s, hi_s, hgate_s, state,
      kv_s, ck_t, cv_t, wa, wb, wo, w_up, w_down, ln_ffn, ln_f)


def _sample_attn_kernel(sink_ref, qm_ref, kvn_ref, ckt_ref, cvt_ref, mkt_ref, mvt_ref,
                        att_ref, o_all, mk_s, mv_s):
    nb = qm_ref.shape[0]
    first_seq = pl.multiple_of(pl.program_id(0) * nb, nb)

    @pl.when(pl.program_id(0) == 0)
    def _():
        for r in range(N_META):
            mk_s[:, r, :] = mkt_ref[r].T
            mv_s[:, r, :] = mvt_ref[r].T
    head = lax.broadcasted_iota(jnp.int32, (Q_HEADS, 1), 0)
    sk = jnp.zeros((Q_HEADS, 1), F32)
    for j in range(Q_HEADS):
        sk = jnp.where(head == j, sink_ref[j], sk)
    oldest = lax.broadcasted_iota(jnp.int32, (Q_HEADS, WINDOW), 1) == 0
    kvn = kvn_ref[...]
    qms = [qm_ref[b] for b in range(nb)]
    scores = [(_dot(qms[b].astype(BF16), ckt_ref[b].astype(BF16)),
               _dot_nt(qms[b].astype(BF16), mk_s[first_seq + b].astype(BF16)))
              for b in range(nb)]
    for b, (s_w, s_m) in enumerate(scores):
        s_w = jnp.where(oldest, -jnp.inf, s_w)
        s_n = jnp.sum(qms[b] * kvn[b:b + 1, :D_KV], axis=1, keepdims=True)
        m = jnp.maximum(jnp.maximum(jnp.max(s_w, axis=1, keepdims=True),
                                    jnp.max(s_m, axis=1, keepdims=True)), jnp.maximum(s_n, sk))
        e_w = jnp.exp(s_w - m)
        e_m = jnp.exp(s_m - m)
        e_n = jnp.exp(s_n - m)
        l = (jnp.sum(e_w, axis=1, keepdims=True) + jnp.sum(e_m, axis=1, keepdims=True) + e_n
             + jnp.exp(sk - m))
        o = (_dot_nt(e_w.astype(BF16), cvt_ref[b].astype(BF16))
             + _dot(e_m.astype(BF16), mv_s[first_seq + b].astype(BF16))
             + e_n * kvn[b:b + 1, D_KV:])
        o_all[b] = o / l
    low = lax.broadcasted_iota(jnp.int32, (1, LANES), 1) < HEAD_DIM
    pairs = []
    for p in range(Q_HEADS // 2):
        halves = []
        for half in range(2):
            j = 2 * p + half
            kv_head = j // (Q_HEADS // KV_HEADS)
            oj = jnp.where(low if kv_head == 0 else jnp.logical_not(low), o_all[:, j, :], 0.0)
            halves.append(oj if half == kv_head else pltpu.roll(oj, HEAD_DIM, axis=1))
        pairs.append(halves[0] + halves[1])
    att_ref[...] = jnp.concatenate(pairs, axis=1)


def _sample_attention(sinks, qm, kv_new, ck, cv, mk, mv):
    nb = ck.shape[0]
    g = SAMPLE_ATT_GROUP

    def blk3(a, c):
        return pl.BlockSpec((g, a, c), lambda i: (i, 0, 0))

    return pl.pallas_call(
        _sample_attn_kernel,
        grid=(nb // g,),
        in_specs=[pl.BlockSpec(memory_space=pltpu.SMEM), blk3(Q_HEADS, D_KV),
                  pl.BlockSpec((g, 2 * D_KV), lambda i: (i, 0)),
                  blk3(D_KV, WINDOW), blk3(D_KV, WINDOW), _resident(mk.shape), _resident(mv.shape)],
        out_specs=pl.BlockSpec((g, D_ATT), lambda i: (i, 0)),
        out_shape=jax.ShapeDtypeStruct((nb, D_ATT), F32),
        scratch_shapes=[pltpu.VMEM((g, Q_HEADS, LANES), F32),
                        pltpu.VMEM((nb, N_META, D_KV), F32), pltpu.VMEM((nb, N_META, D_KV), F32)],
        compiler_params=_params("arbitrary"),
        name="sample_attn",
    )(sinks, qm, kv_new, ck, cv, mk, mv)


def kernel(x_prompt, x_sample, cache_k, cache_v, cache_meta_k, cache_meta_v, state_hgrn, meta,
           w_in, sinks, lb_param, hg_norm, w_att_out, w_hg_out, w_o, ln_mix, ln_ffn, w_up,
           w_down, ln_f):
    bsz, seq, _ = x_prompt.shape
    nb = x_sample.shape[0]
    ln_mix2 = ln_mix.reshape(1, D_MODEL)
    ln_ffn2 = ln_ffn.reshape(1, D_MODEL)
    ln_f2 = ln_f.reshape(1, D_MODEL)
    nw = hg_norm.reshape(1, HG_DV)

    xs = x_sample.reshape(nb * D_MODEL // LANES, LANES)

    xp = x_prompt.reshape(bsz * seq, D_MODEL)
    (att_p, hg_p, sga_p, sgb_p, state_p, lastkv_p,
     qm_s, kv_s, hq_s, hk_s, hi_s, hgate_s, sga_s, sgb_s,
     wa, wb, wo, wup, wdn) = _mixers(
        sinks.reshape(Q_HEADS), xp, xs, meta, ln_mix2, lb_param, w_in[0], nw,
        (w_att_out[0], w_hg_out[0], w_o[0], w_up[0], w_down[0]), bsz)

    def window_t(c):
        return jnp.swapaxes(c[0].reshape(nb, WINDOW, D_KV), 1, 2)

    def meta_t(c):
        return jnp.transpose(c[0].reshape(nb, N_META, D_KV), (1, 2, 0))

    ck_t, cv_t = window_t(cache_k), window_t(cache_v)
    att_s = _sample_attention(sinks.reshape(Q_HEADS), qm_s, kv_s, ck_t, cv_t,
                              meta_t(cache_meta_k), meta_t(cache_meta_v))

    y_p, y_s, state_s, nkt_s, nvt_s = _merge_ffn(
        xp, att_p, hg_p, sga_p, sgb_p, xs, att_s, sga_s, sgb_s,
        nw, hq_s, hk_s, hi_s, hgate_s, state_hgrn[0], kv_s, ck_t, cv_t,
        wa, wb, wo, ln_ffn2, wup, wdn, ln_f2)

    kv5 = lastkv_p.reshape(bsz, WINDOW, 2, KV_HEADS, HEAD_DIM)
    meta5 = jnp.broadcast_to(kv_s[nb:].reshape(1, N_META, 2, KV_HEADS, HEAD_DIM),
                             (bsz, N_META, 2, KV_HEADS, HEAD_DIM))
    return (y_p.reshape(bsz, seq, D_MODEL),
            y_s.reshape(nb, 1, D_MODEL),
            kv5[None, :, :, 0],
            kv5[None, :, :, 1],
            meta5[None, :, :, 0],
            meta5[None, :, :, 1],
            state_p[None],
            jnp.swapaxes(nkt_s, 1, 2).reshape(1, nb, WINDOW, KV_HEADS, HEAD_DIM),
            jnp.swapaxes(nvt_s, 1, 2).reshape(1, nb, WINDOW, KV_HEADS, HEAD_DIM),
            state_s[None])
```

```python
import functools

import jax
import jax.numpy as jnp
from jax import lax
from jax.experimental import pallas as pl
from jax.experimental.pallas import tpu as pltpu

F32 = jnp.float32
BF16 = jnp.bfloat16

D_MODEL = 1024
N_META = 16
WINDOW = 128
HEAD_DIM = 64
Q_HEADS = 8
KV_HEADS = 2
D_ATT = Q_HEADS * HEAD_DIM
D_KV = KV_HEADS * HEAD_DIM
HG_HEADS = 4
HG_DK = 128
HG_DV = 128
D_HG = HG_HEADS * HG_DK
HG_CHUNK = 64
D_FF = 4 * D_MODEL
EPS = 1e-6
C_Q = 0
C_KV = C_Q + D_ATT
C_HQ = C_KV + 2 * D_KV
C_HF = C_HQ + D_HG
C_HI = C_HF + D_HG
C_HGATE = C_HI + D_HG
C_GA = C_HGATE + D_HG
C_GB = C_GA + D_MODEL
D_IN = C_GB + D_MODEL

VMEM_LIMIT_BYTES = 56 * 1024 * 1024
MIX_ROWS = 512
PROJ_GROUP = 256
WEIGHT_SLAB_ROWS = 64
WEIGHT_SLOTS = 4
MERGE_ROWS = 512
FFN_CHUNK = 2048
LANES = 128
BF16_ROWS = 16
SAMPLE_ATT_GROUP = 32
SAMPLE_HG_GROUP = 8

_NT = (((1,), (1,)), ((), ()))
_TN = (((0,), (0,)), ((), ()))


def _dot(a, b):
    return jnp.dot(a, b, preferred_element_type=F32)


def _dot_nt(a, b):
    return lax.dot_general(a, b, _NT, preferred_element_type=F32)


def _dot_tn(a, b):
    return lax.dot_general(a, b, _TN, preferred_element_type=F32)


def _rows_from_linear(ref):
    per = D_MODEL // LANES
    rows = ref.shape[0] // per
    return jnp.concatenate([ref[pl.ds(c, rows, stride=per), :] for c in range(per)], axis=1)


def _rows_to_linear(ref, value):
    per = D_MODEL // LANES
    for c in range(per):
        ref[pl.ds(c, value.shape[0], stride=per), :] = value[:, c * LANES:(c + 1) * LANES]


def _rmsnorm(x, g):
    return x * lax.rsqrt(jnp.mean(x * x, axis=-1, keepdims=True) + EPS) * g


def _resident(shape):
    return pl.BlockSpec(shape, lambda *_: (0,) * len(shape), pipeline_mode=pl.Buffered(1))


def _params(*sem):
    return pltpu.CompilerParams(dimension_semantics=sem, vmem_limit_bytes=VMEM_LIMIT_BYTES)


def _lower_bound(lbp):
    e = jnp.exp(lbp - jnp.max(lbp, axis=0, keepdims=True))
    return e[0:1] / jnp.sum(e, axis=0, keepdims=True)


class _Projection:
    def __init__(self, xn, lbp, w_ref):
        self.xn = xn
        self.w_ref = w_ref
        self.lb = _lower_bound(lbp)

    def cols(self, base, part):
        return _dot(self.xn[...], self.w_ref[:, base + part.start:base + part.stop])

    def q_att(self, part=slice(0, D_ATT)):
        return self.cols(C_Q, part) * (HEAD_DIM ** -0.5)

    def kv(self):
        return self.cols(C_KV, slice(0, 2 * D_KV))

    def q_hg(self, part=slice(0, D_HG)):
        return self.cols(C_HQ, part) * (HG_DK ** -0.5)

    def forget(self, part=slice(0, D_HG)):
        lb = self.lb[:, part]
        f = lb + (1.0 - lb) * jax.nn.sigmoid(self.cols(C_HF, part))
        return 1.0 - f, jnp.log(f)

    def i_hg(self, part=slice(0, D_HG)):
        return self.cols(C_HI, part)

    def swish_gate(self, part=slice(0, D_HG)):
        g = self.cols(C_HGATE, part)
        return g * jax.nn.sigmoid(g)

    def branch_gate(self, base, part=slice(0, D_MODEL)):
        return jax.nn.sigmoid(self.cols(base, part))


class _Projected(_Projection):
    def __init__(self, product_ref, lbp):
        self.product_ref = product_ref
        self.lb = _lower_bound(lbp)

    def cols(self, base, part):
        return self.product_ref[:, base + part.start:base + part.stop]


class _Attention:
    def __init__(self, sink_ref, q_ref, kv, first, o_ref):
        self.sink_ref, self.q_ref, self.first, self.o_ref = sink_ref, q_ref, first, o_ref
        self.nsub = q_ref.shape[0] // WINDOW
        self.nk = 2 * WINDOW + N_META
        self.meta0 = WINDOW + q_ref.shape[0]
        lane = lax.broadcasted_iota(jnp.int32, (kv.shape[0], D_KV), 1)
        low = lane < HEAD_DIM
        k = kv[:, :D_KV]
        ksw = pltpu.roll(k, HEAD_DIM, axis=1)
        self.kboth = (jnp.where(low, k, ksw).astype(BF16), jnp.where(low, ksw, k).astype(BF16))
        qlane = lax.broadcasted_iota(jnp.int32, (1, LANES), 1)
        self.keep = (jnp.where(qlane < HEAD_DIM, 1.0, 0.0).astype(BF16),
                     jnp.where(qlane < HEAD_DIM, 0.0, 1.0).astype(BF16))
        r = lax.broadcasted_iota(jnp.int32, (D_KV, D_KV), 0)
        c = lax.broadcasted_iota(jnp.int32, (D_KV, D_KV), 1)
        eye = jnp.where(r == c, 1.0, 0.0).astype(BF16)
        self.vt = _dot_nt(eye, kv[:, D_KV:].astype(BF16)).astype(BF16)
        self.ones = jnp.ones((BF16_ROWS, self.nk), BF16)
        self.key = lax.broadcasted_iota(jnp.int32, (self.nk, 2 * WINDOW), 0)
        col = lax.broadcasted_iota(jnp.int32, (self.nk, 2 * WINDOW), 1)
        self.qry = jnp.bitwise_and(col, WINDOW - 1)
        self.first_head = lax.broadcasted_iota(jnp.int32, (1, 2 * WINDOW), 1) < WINDOW
        self.scores = {}

    def issue_scores(self, i):
        r0 = i * WINDOW
        out = []
        for p in range(Q_HEADS // 2):
            kb = self.kboth[(2 * p) // (Q_HEADS // KV_HEADS)]
            kmat = jnp.concatenate([kb[r0:r0 + 2 * WINDOW], kb[self.meta0:]], axis=0)
            qp = self.q_ref[r0:r0 + WINDOW, p * LANES:(p + 1) * LANES]
            q2 = jnp.concatenate([qp * self.keep[0], qp * self.keep[1]], axis=0)
            out.append(_dot_nt(kmat, q2))
        self.scores[i] = out

    def finish(self, i):
        r0 = i * WINDOW
        lo = jnp.where(self.first, WINDOW - 1, self.qry) if i == 0 else self.qry
        mask = jnp.logical_and(self.key > lo, self.key <= self.qry + WINDOW)
        mask = jnp.logical_or(mask, self.key >= 2 * WINDOW)
        vaug = []
        for h in range(KV_HEADS):
            vth = self.vt[h * HEAD_DIM:(h + 1) * HEAD_DIM]
            vaug.append(jnp.concatenate(
                [jnp.concatenate([vth[:, r0:r0 + 2 * WINDOW], vth[:, self.meta0:]], axis=1),
                 self.ones], axis=0))
        outs = []
        for p, raw in enumerate(self.scores.pop(i)):
            h = (2 * p) // (Q_HEADS // KV_HEADS)
            s = jnp.where(mask, raw, -jnp.inf)
            sk = jnp.where(self.first_head, self.sink_ref[2 * p], self.sink_ref[2 * p + 1])
            m = jnp.maximum(jnp.max(s, axis=0, keepdims=True), sk)
            e = jnp.exp(s - m).astype(BF16)
            oa = _dot(vaug[h], e)
            o = oa[:HEAD_DIM] / (oa[HEAD_DIM:HEAD_DIM + 1] + jnp.exp(sk - m))
            outs += [o[:, :WINDOW], o[:, WINDOW:]]
        self.o_ref[r0:r0 + WINDOW, :] = jnp.concatenate(outs, axis=0).T.astype(self.o_ref.dtype)


def _cumsum_rows(x):
    t = x.shape[0]
    row = lax.broadcasted_iota(jnp.int32, x.shape, 0)
    d = 1
    while d < t:
        x = x + jnp.where(row >= d, pltpu.roll(x, d, axis=0), 0.0)
        d *= 2
    return x


def _hg_out(o, gate, nw):
    parts = []
    for h in range(HG_HEADS):
        oh = o[:, h * HG_DV:(h + 1) * HG_DV]
        parts.append(oh * lax.rsqrt(jnp.mean(oh * oh, axis=-1, keepdims=True) + EPS) * nw)
    return jnp.concatenate(parts, axis=1) * gate


_HG_SLICES = [slice(h * HG_DK, (h + 1) * HG_DK) for h in range(HG_HEADS)]


def _state_update(iv, kd):
    return jnp.concatenate([_dot_tn(iv[:, hs], kd[:, hs]) for hs in _HG_SLICES], axis=1)


def _meta_state(mk, mlogf, mi):
    b = _cumsum_rows(mlogf)
    kd = (mk * jnp.exp(b[-1:] - b)).astype(BF16)
    return _state_update(mi.astype(BF16), kd)


class _Hgrn:
    def __init__(self, q_ref, k_ref, logf_ref, i_ref, g_ref, nw, state, o_ref):
        self.q_ref, self.k_ref, self.logf_ref, self.i_ref, self.g_ref = q_ref, k_ref, logf_ref, i_ref, g_ref
        self.nw, self.state, self.o_ref = nw, state, o_ref
        t = HG_CHUNK
        self.nchunk = q_ref.shape[0] // t
        r = lax.broadcasted_iota(jnp.int32, (t, t), 0)
        c = lax.broadcasted_iota(jnp.int32, (t, t), 1)
        self.causal = r >= c
        self.cums, self.terms, self.entering = {}, {}, {}

    def rows(self, ci):
        return slice(ci * HG_CHUNK, (ci + 1) * HG_CHUNK)

    def issue_cumsum(self, ci):
        self.cums[ci] = _cumsum_rows(self.logf_ref[self.rows(ci), :])

    def issue_local(self, ci):
        rows = self.rows(ci)
        b = self.cums.pop(ci)
        bl = b[-1:]
        k = self.k_ref[rows, :].astype(F32)
        qe = (self.q_ref[rows, :].astype(F32) * jnp.exp(b)).astype(BF16)
        ke = (k * jnp.exp(-b)).astype(BF16)
        kd = (k * jnp.exp(bl - b)).astype(BF16)
        iv = self.i_ref[rows, :].astype(BF16)
        a = [jnp.where(self.causal, _dot_nt(qe[:, hs], ke[:, hs]), 0.0).astype(BF16) for hs in _HG_SLICES]
        self.terms[ci] = (qe, iv, a)
        self.entering[ci] = self.state.astype(BF16)
        self.state = self.state * jnp.exp(bl) + _state_update(iv, kd)

    def finish(self, ci):
        rows = self.rows(ci)
        qe, iv, a = self.terms.pop(ci)
        ent = self.entering.pop(ci)
        outs = [_dot(a[h], iv[:, hs]) + _dot_nt(qe[:, hs], ent[:, hs]) for h, hs in enumerate(_HG_SLICES)]
        o = _hg_out(jnp.concatenate(outs, axis=1), self.g_ref[rows, :].astype(F32), self.nw)
        self.o_ref[rows, :] = o.astype(self.o_ref.dtype)


def _load_as_bf16(w_hbm, w_vmem, stage, sem, meanwhile, chunk_rows, on_chunk):
    nslot, rows = stage.shape[0], stage.shape[1]
    nslab = w_hbm.shape[0] // rows
    ahead = nslot - 1
    assert chunk_rows % rows == 0 and w_hbm.shape[0] % chunk_rows == 0

    def slab_copy(c):
        return pltpu.make_async_copy(w_hbm.at[pl.ds(c * rows, rows), :], stage.at[c % nslot],
                                     sem.at[c % nslot])

    for c in range(min(ahead, nslab)):
        slab_copy(c).start()
    meanwhile()
    for c in range(nslab):
        if c + ahead < nslab:
            slab_copy(c + ahead).start()
        slab_copy(c).wait()
        w_vmem[pl.ds(c * rows, rows), :] = stage[c % nslot].astype(w_vmem.dtype)
        if (c + 1) * rows % chunk_rows == 0:
            on_chunk((c + 1) * rows - chunk_rows, (c + 1) * rows)


def _mixer_kernel(sink_ref, x0_ref, xnext_ref, xs_ref, xm_ref, g_ref, lbp_ref, w_hbm, nw_ref,
                  wa32, wb32, wo32, wup32, wdn32,
                  att_ref, hg_ref, sga_ref, sgb_ref, sfin_ref, lastkv_ref,
                  qs_ref, kvs_ref, hqs_ref, hks_ref, his_ref, hgs_ref, sgas_ref, sgbs_ref,
                  wa16, wb16, wo16, wup16, wdn16,
                  w_ref, stage, sem, xn_s, zq, zkv, zhq, zhk, zlogf, zhi, zhg, kvm_ref, st_ref, mst_ref,
                  small_s, prod_s, *, tiles_per_seq):
    s = pl.program_id(0)
    t = xnext_ref.shape[0]

    @pl.when(s == 0)
    def _():
        nb = small_s.shape[0] - xm_ref.shape[0]

        def first_tile_and_zeros():
            xn_s[...] = _rmsnorm(x0_ref[...], g_ref[...]).astype(xn_s.dtype)
            for ref in (zq, zkv, zhq, zhk, zlogf, zhi, zhg, st_ref):
                ref[...] = jnp.zeros(ref.shape, ref.dtype)
            small = jnp.concatenate([_rows_from_linear(xs_ref), xm_ref[...]], axis=0)
            small_s[...] = _rmsnorm(small, g_ref[...]).astype(small_s.dtype)

        def partial_projection(r0, r1):
            part = _dot(small_s[:, r0:r1], w_ref[r0:r1, :])
            if r0 == 0:
                prod_s[...] = part
            else:
                prod_s[...] += part

        _load_as_bf16(w_hbm, w_ref, stage, sem, first_tile_and_zeros, PROJ_GROUP, partial_projection)
        p = _Projected(prod_s, lbp_ref[...])
        q_small = p.q_att()
        low = lax.broadcasted_iota(jnp.int32, (1, LANES), 1) < HEAD_DIM
        for j in range(Q_HEADS):
            kv_head, half = j // (Q_HEADS // KV_HEADS), j % 2
            pair = q_small[:, (j // 2) * LANES:(j // 2 + 1) * LANES]
            moved = pair if half == kv_head else pltpu.roll(pair, HEAD_DIM, axis=1)
            qs_ref[:, j, :] = jnp.where(low if kv_head == 0 else jnp.logical_not(low), moved, 0.0)
        kv = p.kv()
        kvs_ref[...] = kv
        kvm_ref[...] = kv[nb:]
        hqs_ref[...] = p.q_hg()
        k, logf = p.forget()
        hks_ref[...] = k
        iv = p.i_hg()
        his_ref[...] = iv
        hgs_ref[...] = p.swish_gate()
        sgas_ref[...] = p.branch_gate(C_GA).astype(sgas_ref.dtype)
        sgbs_ref[...] = p.branch_gate(C_GB).astype(sgbs_ref.dtype)
        mst_ref[...] = _meta_state(k[nb:], logf[nb:], iv[nb:])

    first = lax.rem(jnp.maximum(s - 1, 0), tiles_per_seq) == 0

    att = _Attention(sink_ref, zq, jnp.concatenate([zkv[...], kvm_ref[...]], axis=0), first, att_ref)
    last = zkv[t:t + WINDOW, :]
    lastkv_ref[...] = last
    zkv[0:WINDOW, :] = last
    entering = jnp.where(first, mst_ref[...], st_ref[...])
    hg = _Hgrn(zhq, zhk, zlogf, zhi, zhg, nw_ref[...], entering, hg_ref)
    proj = _Projection(xn_s, lbp_ref[...], w_ref)

    def gate_a(p):
        sga_ref[:, p] = proj.branch_gate(C_GA, p).astype(sga_ref.dtype)

    def gate_b(p):
        sgb_ref[:, p] = proj.branch_gate(C_GB, p).astype(sgb_ref.dtype)

    def new_kv(_):
        zkv[WINDOW:WINDOW + t, :] = proj.kv()

    def new_forget(p):
        k_new, logf_new = proj.forget(p)
        zhk[:, p] = k_new.astype(zhk.dtype)
        zlogf[:, p] = logf_new

    def new_q(p):
        zq[:, p] = proj.q_att(p).astype(zq.dtype)

    def new_hq(p):
        zhq[:, p] = proj.q_hg(p).astype(zhq.dtype)

    def new_hi(p):
        zhi[:, p] = proj.i_hg(p).astype(zhi.dtype)

    def new_gate(p):
        zhg[:, p] = proj.swish_gate(p).astype(zhg.dtype)

    def parts(n):
        return [slice(c, c + PROJ_GROUP) for c in range(0, n, PROJ_GROUP)]

    groups = ([(gate_a, p) for p in parts(D_MODEL)] + [(gate_b, p) for p in parts(D_MODEL)]
              + [(new_kv, None)] + [(new_q, p) for p in parts(D_ATT)]
              + [(new_forget, p) for p in parts(D_HG)] + [(new_hq, p) for p in parts(D_HG)]
              + [(new_hi, p) for p in parts(D_HG)] + [(new_gate, p) for p in parts(D_HG)])

    def save_state():
        st_ref[...] = hg.state
        for h, hs in enumerate(_HG_SLICES):
            sfin_ref[h] = hg.state[:, hs].T

    qk, cs, loc, fin, out = att.issue_scores, hg.issue_cumsum, hg.issue_local, att.finish, hg.finish
    mixer_work = {
        0: [(qk, 0), (cs, 0), (cs, 1), (cs, 2), (cs, 3)],
        1: [(cs, 4), (cs, 5), (loc, 0)],
        2: [(cs, 6), (cs, 7), (loc, 1)],
        3: [(qk, 1), (loc, 2)],
        4: [(fin, 0), (loc, 3)],
        5: [(out, 0), (loc, 4)],
        6: [(qk, 2), (out, 1), (loc, 5)],
        7: [(fin, 1), (out, 2), (loc, 6)],
        8: [(qk, 3), (out, 3), (loc, 7), (save_state,)],
        9: [(out, 4)],
        10: [(fin, 2), (out, 5)],
        11: [(out, 6)],
        12: [(out, 7)],
        13: [(fin, 3)],
    }
    for slot, (fn, p) in enumerate(groups):
        fn(p)
        for item in mixer_work.get(slot, []):
            item[0](*item[1:])
    xn_s[...] = _rmsnorm(xnext_ref[...], g_ref[...]).astype(xn_s.dtype)
    for src, dst in ((wa32, wa16), (wb32, wb16), (wo32, wo16), (wup32, wup16), (wdn32, wdn16)):
        dst[...] = src[...].astype(dst.dtype)


def _mixers(sinks, x, x_sample, x_meta, g, lb_param, w_f32, hg_norm, later_weights, bsz):
    n = x.shape[0]
    t = MIX_ROWS
    nt = n // t
    per_seq = nt // bsz
    rows = x_sample.shape[0] * LANES // D_MODEL + x_meta.shape[0]

    def this_tile(c):
        return pl.BlockSpec((t, c), lambda s: (jnp.minimum(s, nt - 1), 0))

    def prev_tile(c):
        return pl.BlockSpec((t, c), lambda s: (jnp.maximum(s - 1, 0), 0))

    def prev_seq(shape):
        return pl.BlockSpec((None,) + shape,
                            lambda s: (jnp.maximum(s - 1, 0) // per_seq,) + (0,) * len(shape))

    def small(c):
        return pl.BlockSpec((rows, c), lambda s: (0, 0))

    first_tile = pl.BlockSpec((t, D_MODEL), lambda s: (0, 0), pipeline_mode=pl.Buffered(1))
    next_tile = pl.BlockSpec((t, D_MODEL), lambda s: (jnp.minimum(s + 1, nt - 1), 0))
    small_widths = (2 * D_KV, D_HG, D_HG, D_HG, D_HG, D_MODEL, D_MODEL)
    small_dtypes = (F32, F32, F32, F32, F32, BF16, BF16)
    qm_shape = (rows, Q_HEADS, LANES)
    qm_spec = pl.BlockSpec(qm_shape, lambda s: (0, 0, 0))

    def row_block(wt):
        return pl.BlockSpec((wt.shape[0] // nt, wt.shape[1]), lambda s: (jnp.minimum(s, nt - 1), 0))

    return pl.pallas_call(
        functools.partial(_mixer_kernel, tiles_per_seq=per_seq),
        grid=(nt + 1,),
        in_specs=[pl.BlockSpec(memory_space=pltpu.SMEM), first_tile, next_tile,
                  _resident(x_sample.shape), _resident(x_meta.shape), _resident((1, D_MODEL)),
                  _resident(lb_param.shape), pl.BlockSpec(memory_space=pl.ANY), _resident((1, HG_DV))]
                 + [row_block(wt) for wt in later_weights],
        out_specs=[prev_tile(D_ATT), prev_tile(D_HG), this_tile(D_MODEL), this_tile(D_MODEL),
                   prev_seq((HG_HEADS, HG_DK, HG_DV)), prev_seq((WINDOW, 2 * D_KV))]
                  + [qm_spec] + [small(c) for c in small_widths] + [row_block(wt) for wt in later_weights],
        out_shape=[jax.ShapeDtypeStruct((n, D_ATT), BF16), jax.ShapeDtypeStruct((n, D_HG), BF16),
                   jax.ShapeDtypeStruct((n, D_MODEL), BF16), jax.ShapeDtypeStruct((n, D_MODEL), BF16),
                   jax.ShapeDtypeStruct((bsz, HG_HEADS, HG_DK, HG_DV), F32),
                   jax.ShapeDtypeStruct((bsz, WINDOW, 2 * D_KV), F32)]
                  + [jax.ShapeDtypeStruct(qm_shape, F32)]
                  + [jax.ShapeDtypeStruct((rows, c), d) for c, d in zip(small_widths, small_dtypes)]
                  + [jax.ShapeDtypeStruct(wt.shape, BF16) for wt in later_weights],
        scratch_shapes=[pltpu.VMEM((D_MODEL, D_IN), BF16),
                        pltpu.VMEM((WEIGHT_SLOTS, WEIGHT_SLAB_ROWS, D_IN), F32),
                        pltpu.SemaphoreType.DMA((WEIGHT_SLOTS,)),
                        pltpu.VMEM((t, D_MODEL), BF16),
                        pltpu.VMEM((t, D_ATT), BF16), pltpu.VMEM((WINDOW + t, 2 * D_KV), F32),
                        pltpu.VMEM((t, D_HG), BF16), pltpu.VMEM((t, D_HG), BF16),
                        pltpu.VMEM((t, D_HG), F32), pltpu.VMEM((t, D_HG), BF16),
                        pltpu.VMEM((t, D_HG), BF16),
                        pltpu.VMEM((N_META, 2 * D_KV), F32),
                        pltpu.VMEM((HG_DV, D_HG), F32),
                        pltpu.VMEM((HG_DV, D_HG), F32),
                        pltpu.VMEM((rows, D_MODEL), BF16), pltpu.VMEM((rows, D_IN), F32)],
        compiler_params=_params("arbitrary"),
        name="mixers",
    )(sinks, x, x, x_sample, x_meta, g, lb_param, w_f32, hg_norm, *later_weights)


def _merge_ffn_rows(x, att, hg, sga, sgb, wa_ref, wb_ref, wo_ref, ln_ffn, wup_ref, wdn_ref, ln_f):
    ya = _dot(att.astype(BF16), wa_ref[...])
    yb = _dot(hg.astype(BF16), wb_ref[...])
    mix = sga.astype(F32) * ya + sgb.astype(F32) * yb
    h1 = x + _dot(mix.astype(BF16), wo_ref[...])
    xn = _rmsnorm(h1, ln_ffn).astype(BF16)
    acc = jnp.zeros_like(h1)
    for c in range(0, D_FF, FFN_CHUNK):
        u = jnp.maximum(_dot(xn, wup_ref[:, c:c + FFN_CHUNK]), 0.0)
        acc = acc + _dot((u * u).astype(BF16), wdn_ref[c:c + FFN_CHUNK, :])
    return _rmsnorm(h1 + acc, ln_f)


def _sample_hgrn_group(rows, nw, q_ref, k_ref, i_ref, g_ref, s_ref, snew_ref, hg_ref):
    k = k_ref[rows, :]
    q = q_ref[rows, :]
    iv = i_ref[rows, :]
    r = lax.broadcasted_iota(jnp.int32, (HG_DK, HG_DK), 0)
    c = lax.broadcasted_iota(jnp.int32, (HG_DK, HG_DK), 1)
    eye = r == c

    def column(row):
        return jnp.sum(jnp.where(eye, row, 0.0), axis=1, keepdims=True)

    outs = []
    for b in range(s_ref.shape[0]):
        heads = []
        for h, hs in enumerate(_HG_SLICES):
            kc = column(k[b:b + 1, hs])
            qc = column(q[b:b + 1, hs])
            s_old = s_ref[b, h]
            s = s_old + kc * (iv[b:b + 1, hs] - s_old)
            snew_ref[b, h] = s
            heads.append(jnp.sum(qc * s, axis=0, keepdims=True))
        outs.append(jnp.concatenate(heads, axis=1))
    hg_ref[rows, :] = _hg_out(jnp.concatenate(outs, axis=0), g_ref[rows, :], nw)


def _merge_ffn_kernel(x_ref, att_ref, hg_ref, sga_ref, sgb_ref,
                      xs_ref, atts_ref, sgas_ref, sgbs_ref,
                      nw_ref, hqs_ref, hks_ref, his_ref, hgates_ref, state_ref,
                      kvs_ref, ckt_ref, cvt_ref,
                      wa_ref, wb_ref, wo_ref, wup_ref, wdn_ref, lnffn_ref, lnf_ref,
                      y_ref, ys_ref, snew_ref, nkt_ref, nvt_ref, hgs, *, prompt_steps):
    i = pl.program_id(0)
    weights = (wa_ref, wb_ref, wo_ref, lnffn_ref[...], wup_ref, wdn_ref, lnf_ref[...])
    group = state_ref.shape[0]
    ngroups = atts_ref.shape[0] // group

    @pl.when(i < prompt_steps)
    def _():
        y_ref[...] = _merge_ffn_rows(x_ref[...], att_ref[...], hg_ref[...], sga_ref[...], sgb_ref[...],
                                     *weights)
        rows = pl.ds(pl.multiple_of(jnp.minimum(i, ngroups - 1) * group, group), group)
        _sample_hgrn_group(rows, nw_ref[...], hqs_ref, hks_ref, his_ref, hgates_ref,
                           state_ref, snew_ref, hgs)
        kvn = kvs_ref[rows, :]
        r = lax.broadcasted_iota(jnp.int32, (D_KV, WINDOW), 0)
        c = lax.broadcasted_iota(jnp.int32, (D_KV, WINDOW), 1)
        for b in range(group):
            for src, dst, lo in ((ckt_ref, nkt_ref, 0), (cvt_ref, nvt_ref, D_KV)):
                col = jnp.sum(jnp.where(r == c, kvn[b:b + 1, lo:lo + D_KV], 0.0), axis=1, keepdims=True)
                dst[b] = jnp.where(c == WINDOW - 1, col, pltpu.roll(src[b], WINDOW - 1, axis=1))

    @pl.when(i == prompt_steps)
    def _():
        _rows_to_linear(ys_ref, _merge_ffn_rows(_rows_from_linear(xs_ref), atts_ref[...], hgs[...],
                                                sgas_ref[...], sgbs_ref[...], *weights))


def _merge_ffn(x, att, hg, sga, sgb, xs, att_s, sga_s, sgb_s, hg_norm, hq_s, hk_s, hi_s, hgate_s, state,
               kv_s, ck_t, cv_t, wa, wb, wo, ln_ffn, w_up, w_down, ln_f):
    n = x.shape[0]
    nb = att_s.shape[0]
    linear = pl.BlockSpec(xs.shape, lambda i: (0, 0))
    rows = MERGE_ROWS
    nt = n // rows
    g = SAMPLE_HG_GROUP
    assert nb // g <= nt

    def blk(c):
        return pl.BlockSpec((rows, c), lambda i: (jnp.minimum(i, nt - 1), 0))

    def sample(c):
        return pl.BlockSpec((nb, c), lambda i: (0, 0))

    sblk = pl.BlockSpec((g, HG_HEADS, HG_DK, HG_DV), lambda i: (jnp.minimum(i, nb // g - 1), 0, 0, 0))
    cblk = pl.BlockSpec((g, D_KV, WINDOW), lambda i: (jnp.minimum(i, nb // g - 1), 0, 0))
    return pl.pallas_call(
        functools.partial(_merge_ffn_kernel, prompt_steps=nt),
        grid=(nt + 1,),
        in_specs=[blk(D_MODEL), blk(D_ATT), blk(D_HG), blk(D_MODEL), blk(D_MODEL),
                  linear, sample(D_ATT), sample(D_MODEL), sample(D_MODEL),
                  _resident((1, HG_DV)), _resident(hq_s.shape), _resident(hk_s.shape),
                  _resident(hi_s.shape), _resident(hgate_s.shape), sblk,
                  _resident(kv_s.shape), cblk, cblk,
                  _resident(wa.shape), _resident(wb.shape), _resident(wo.shape),
                  _resident(w_up.shape), _resident(w_down.shape),
                  _resident((1, D_MODEL)), _resident((1, D_MODEL))],
        out_specs=[blk(D_MODEL), linear, sblk, cblk, cblk],
        out_shape=[jax.ShapeDtypeStruct((n, D_MODEL), F32), jax.ShapeDtypeStruct(xs.shape, F32),
                   jax.ShapeDtypeStruct(state.shape, F32),
                   jax.ShapeDtypeStruct(ck_t.shape, F32), jax.ShapeDtypeStruct(cv_t.shape, F32)],
        scratch_shapes=[pltpu.VMEM((nb, D_HG), F32)],
        compiler_params=_params("arbitrary"),
        name="merge_ffn",
    )(x, att, hg, sga, sgb, xs, att_s, sga_s, sgb_s, hg_norm, hq_s, hk_s, hi_s, hgate_s, state,
      kv_s, ck_t, cv_t, wa, wb, wo, w_up, w_down, ln_ffn, ln_f)


def _sample_attn_kernel(sink_ref, qm_ref, kvn_ref, ckt_ref, cvt_ref, mkt_ref, mvt_ref,
                        att_ref, o_all, mk_s, mv_s):
    nb = qm_ref.shape[0]
    first_seq = pl.multiple_of(pl.program_id(0) * nb, nb)

    @pl.when(pl.program_id(0) == 0)
    def _():
        for r in range(N_META):
            mk_s[:, r, :] = mkt_ref[r].T
            mv_s[:, r, :] = mvt_ref[r].T
    head = lax.broadcasted_iota(jnp.int32, (Q_HEADS, 1), 0)
    sk = jnp.zeros((Q_HEADS, 1), F32)
    for j in range(Q_HEADS):
        sk = jnp.where(head == j, sink_ref[j], sk)
    oldest = lax.broadcasted_iota(jnp.int32, (Q_HEADS, WINDOW), 1) == 0
    kvn = kvn_ref[...]
    qms = [qm_ref[b] for b in range(nb)]
    scores = [(_dot(qms[b].astype(BF16), ckt_ref[b].astype(BF16)),
               _dot_nt(qms[b].astype(BF16), mk_s[first_seq + b].astype(BF16)))
              for b in range(nb)]
    for b, (s_w, s_m) in enumerate(scores):
        s_w = jnp.where(oldest, -jnp.inf, s_w)
        s_n = jnp.sum(qms[b] * kvn[b:b + 1, :D_KV], axis=1, keepdims=True)
        m = jnp.maximum(jnp.maximum(jnp.max(s_w, axis=1, keepdims=True),
                                    jnp.max(s_m, axis=1, keepdims=True)), jnp.maximum(s_n, sk))
        e_w = jnp.exp(s_w - m)
        e_m = jnp.exp(s_m - m)
        e_n = jnp.exp(s_n - m)
        l = (jnp.sum(e_w, axis=1, keepdims=True) + jnp.sum(e_m, axis=1, keepdims=True) + e_n
             + jnp.exp(sk - m))
        o = (_dot_nt(e_w.astype(BF16), cvt_ref[b].astype(BF16))
             + _dot(e_m.astype(BF16), mv_s[first_seq + b].astype(BF16))
             + e_n * kvn[b:b + 1, D_KV:])
        o_all[b] = o / l
    low = lax.broadcasted_iota(jnp.int32, (1, LANES), 1) < HEAD_DIM
    pairs = []
    for p in range(Q_HEADS // 2):
        halves = []
        for half in range(2):
            j = 2 * p + half
            kv_head = j // (Q_HEADS // KV_HEADS)
            oj = jnp.where(low if kv_head == 0 else jnp.logical_not(low), o_all[:, j, :], 0.0)
            halves.append(oj if half == kv_head else pltpu.roll(oj, HEAD_DIM, axis=1))
        pairs.append(halves[0] + halves[1])
    att_ref[...] = jnp.concatenate(pairs, axis=1)


def _sample_attention(sinks, qm, kv_new, ck, cv, mk, mv):
    nb = ck.shape[0]
    g = SAMPLE_ATT_GROUP

    def blk3(a, c):
        return pl.BlockSpec((g, a, c), lambda i: (i, 0, 0))

    return pl.pallas_call(
        _sample_attn_kernel,
        grid=(nb // g,),
        in_specs=[pl.BlockSpec(memory_space=pltpu.SMEM), blk3(Q_HEADS, D_KV),
                  pl.BlockSpec((g, 2 * D_KV), lambda i: (i, 0)),
                  blk3(D_KV, WINDOW), blk3(D_KV, WINDOW), _resident(mk.shape), _resident(mv.shape)],
        out_specs=pl.BlockSpec((g, D_ATT), lambda i: (i, 0)),
        out_shape=jax.ShapeDtypeStruct((nb, D_ATT), F32),
        scratch_shapes=[pltpu.VMEM((g, Q_HEADS, LANES), F32),
                        pltpu.VMEM((nb, N_META, D_KV), F32), pltpu.VMEM((nb, N_META, D_KV), F32)],
        compiler_params=_params("arbitrary"),
        name="sample_attn",
    )(sinks, qm, kv_new, ck, cv, mk, mv)


def kernel(x_prompt, x_sample, cache_k, cache_v, cache_meta_k, cache_meta_v, state_hgrn, meta,
           w_in, sinks, lb_param, hg_norm, w_att_out, w_hg_out, w_o, ln_mix, ln_ffn, w_up,
           w_down, ln_f):
    bsz, seq, _ = x_prompt.shape
    nb = x_sample.shape[0]
    ln_mix2 = ln_mix.reshape(1, D_MODEL)
    ln_ffn2 = ln_ffn.reshape(1, D_MODEL)
    ln_f2 = ln_f.reshape(1, D_MODEL)
    nw = hg_norm.reshape(1, HG_DV)

    xs = x_sample.reshape(nb * D_MODEL // LANES, LANES)

    xp = x_prompt.reshape(bsz * seq, D_MODEL)
    (att_p, hg_p, sga_p, sgb_p, state_p, lastkv_p,
     qm_s, kv_s, hq_s, hk_s, hi_s, hgate_s, sga_s, sgb_s,
     wa, wb, wo, wup, wdn) = _mixers(
        sinks.reshape(Q_HEADS), xp, xs, meta, ln_mix2, lb_param, w_in[0], nw,
        (w_att_out[0], w_hg_out[0], w_o[0], w_up[0], w_down[0]), bsz)

    def window_t(c):
        return jnp.swapaxes(c[0].reshape(nb, WINDOW, D_KV), 1, 2)

    def meta_t(c):
        return jnp.transpose(c[0].reshape(nb, N_META, D_KV), (1, 2, 0))

    ck_t, cv_t = window_t(cache_k), window_t(cache_v)
    att_s = _sample_attention(sinks.reshape(Q_HEADS), qm_s, kv_s, ck_t, cv_t,
                              meta_t(cache_meta_k), meta_t(cache_meta_v))

    y_p, y_s, state_s, nkt_s, nvt_s = _merge_ffn(
        xp, att_p, hg_p, sga_p, sgb_p, xs, att_s, sga_s, sgb_s,
        nw, hq_s, hk_s, hi_s, hgate_s, state_hgrn[0], kv_s, ck_t, cv_t,
        wa, wb, wo, ln_ffn2, wup, wdn, ln_f2)

    kv5 = lastkv_p.reshape(bsz, WINDOW, 2, KV_HEADS, HEAD_DIM)
    meta5 = jnp.broadcast_to(kv_s[nb:].reshape(1, N_META, 2, KV_HEADS, HEAD_DIM),
                             (bsz, N_META, 2, KV_HEADS, HEAD_DIM))
    return (y_p.reshape(bsz, seq, D_MODEL),
            y_s.reshape(nb, 1, D_MODEL),
            kv5[None, :, :, 0],
            kv5[None, :, :, 1],
            meta5[None, :, :, 0],
            meta5[None, :, :, 1],
            state_p[None],
            jnp.swapaxes(nkt_s, 1, 2).reshape(1, nb, WINDOW, KV_HEADS, HEAD_DIM),
            jnp.swapaxes(nvt_s, 1, 2).reshape(1, nb, WINDOW, KV_HEADS, HEAD_DIM),
            state_s[None])
```

```python
import functools

import jax
import jax.numpy as jnp
from jax import lax
from jax.experimental import pallas as pl
from jax.experimental.pallas import tpu as pltpu

F32 = jnp.float32
BF16 = jnp.bfloat16

D_MODEL = 1024
N_META = 16
WINDOW = 128
HEAD_DIM = 64
Q_HEADS = 8
KV_HEADS = 2
D_ATT = Q_HEADS * HEAD_DIM
D_KV = KV_HEADS * HEAD_DIM
HG_HEADS = 4
HG_DK = 128
HG_DV = 128
D_HG = HG_HEADS * HG_DK
HG_CHUNK = 64
D_FF = 4 * D_MODEL
EPS = 1e-6
C_Q = 0
C_KV = C_Q + D_ATT
C_HQ = C_KV + 2 * D_KV
C_HF = C_HQ + D_HG
C_HI = C_HF + D_HG
C_HGATE = C_HI + D_HG
C_GA = C_HGATE + D_HG
C_GB = C_GA + D_MODEL
D_IN = C_GB + D_MODEL

VMEM_LIMIT_BYTES = 56 * 1024 * 1024
MIX_ROWS = 512
PROJ_GROUP = 256
WEIGHT_SLAB_ROWS = 64
WEIGHT_SLOTS = 4
MERGE_ROWS = 512
FFN_CHUNK = 2048
LANES = 128
SUBLANES = 8
BF16_ROWS = 16
SAMPLE_ATT_GROUP = 32
SAMPLE_HG_GROUP = 4

_NT = (((1,), (1,)), ((), ()))
_TN = (((0,), (0,)), ((), ()))


def _dot(a, b):
    return jnp.dot(a, b, preferred_element_type=F32)


def _dot_nt(a, b):
    return lax.dot_general(a, b, _NT, preferred_element_type=F32)


def _dot_tn(a, b):
    return lax.dot_general(a, b, _TN, preferred_element_type=F32)


def _rows_from_linear(ref):
    per = D_MODEL // LANES
    rows = ref.shape[0] // per
    return jnp.concatenate([ref[pl.ds(c, rows, stride=per), :] for c in range(per)], axis=1)


def _rows_to_linear(ref, value):
    per = D_MODEL // LANES
    for c in range(per):
        ref[pl.ds(c, value.shape[0], stride=per), :] = value[:, c * LANES:(c + 1) * LANES]


def _rmsnorm(x, g):
    return x * lax.rsqrt(jnp.mean(x * x, axis=-1, keepdims=True) + EPS) * g


def _resident(shape):
    return pl.BlockSpec(shape, lambda *_: (0,) * len(shape), pipeline_mode=pl.Buffered(1))


def _params(*sem):
    return pltpu.CompilerParams(dimension_semantics=sem, vmem_limit_bytes=VMEM_LIMIT_BYTES)


def _lower_bound(lbp):
    e = jnp.exp(lbp - jnp.max(lbp, axis=0, keepdims=True))
    return e[0:1] / jnp.sum(e, axis=0, keepdims=True)


class _Projection:
    def __init__(self, xn, lbp, w_ref):
        self.xn = xn
        self.w_ref = w_ref
        self.lb = _lower_bound(lbp)

    def cols(self, base, part):
        return _dot(self.xn[...], self.w_ref[:, base + part.start:base + part.stop])

    def q_att(self, part=slice(0, D_ATT)):
        return self.cols(C_Q, part) * (HEAD_DIM ** -0.5)

    def kv(self):
        return self.cols(C_KV, slice(0, 2 * D_KV))

    def q_hg(self, part=slice(0, D_HG)):
        return self.cols(C_HQ, part) * (HG_DK ** -0.5)

    def forget(self, part=slice(0, D_HG)):
        lb = self.lb[:, part]
        f = lb + (1.0 - lb) * jax.nn.sigmoid(self.cols(C_HF, part))
        return 1.0 - f, jnp.log(f)

    def i_hg(self, part=slice(0, D_HG)):
        return self.cols(C_HI, part)

    def swish_gate(self, part=slice(0, D_HG)):
        g = self.cols(C_HGATE, part)
        return g * jax.nn.sigmoid(g)

    def branch_gate(self, base, part=slice(0, D_MODEL)):
        return jax.nn.sigmoid(self.cols(base, part))


class _Attention:
    def __init__(self, sink_ref, q_ref, kv, first, o_ref):
        self.sink_ref, self.q_ref, self.first, self.o_ref = sink_ref, q_ref, first, o_ref
        self.nsub = q_ref.shape[0] // WINDOW
        self.nk = 2 * WINDOW + N_META
        self.meta0 = WINDOW + q_ref.shape[0]
        lane = lax.broadcasted_iota(jnp.int32, (kv.shape[0], D_KV), 1)
        low = lane < HEAD_DIM
        k = kv[:, :D_KV]
        ksw = pltpu.roll(k, HEAD_DIM, axis=1)
        self.kboth = (jnp.where(low, k, ksw).astype(BF16), jnp.where(low, ksw, k).astype(BF16))
        qlane = lax.broadcasted_iota(jnp.int32, (1, LANES), 1)
        self.keep = (jnp.where(qlane < HEAD_DIM, 1.0, 0.0).astype(BF16),
                     jnp.where(qlane < HEAD_DIM, 0.0, 1.0).astype(BF16))
        r = lax.broadcasted_iota(jnp.int32, (D_KV, D_KV), 0)
        c = lax.broadcasted_iota(jnp.int32, (D_KV, D_KV), 1)
        eye = jnp.where(r == c, 1.0, 0.0).astype(BF16)
        self.vt = _dot_nt(eye, kv[:, D_KV:].astype(BF16)).astype(BF16)
        self.ones = jnp.ones((BF16_ROWS, self.nk), BF16)
        self.key = lax.broadcasted_iota(jnp.int32, (self.nk, 2 * WINDOW), 0)
        col = lax.broadcasted_iota(jnp.int32, (self.nk, 2 * WINDOW), 1)
        self.qry = jnp.bitwise_and(col, WINDOW - 1)
        self.first_head = lax.broadcasted_iota(jnp.int32, (1, 2 * WINDOW), 1) < WINDOW
        self.scores = {}

    def issue_scores(self, i):
        r0 = i * WINDOW
        out = []
        for p in range(Q_HEADS // 2):
            kb = self.kboth[(2 * p) // (Q_HEADS // KV_HEADS)]
            kmat = jnp.concatenate([kb[r0:r0 + 2 * WINDOW], kb[self.meta0:]], axis=0)
            qp = self.q_ref[r0:r0 + WINDOW, p * LANES:(p + 1) * LANES]
            q2 = jnp.concatenate([qp * self.keep[0], qp * self.keep[1]], axis=0)
            out.append(_dot_nt(kmat, q2))
        self.scores[i] = out

    def finish(self, i):
        r0 = i * WINDOW
        lo = jnp.where(self.first, WINDOW - 1, self.qry) if i == 0 else self.qry
        mask = jnp.logical_and(self.key > lo, self.key <= self.qry + WINDOW)
        mask = jnp.logical_or(mask, self.key >= 2 * WINDOW)
        vaug = []
        for h in range(KV_HEADS):
            vth = self.vt[h * HEAD_DIM:(h + 1) * HEAD_DIM]
            vaug.append(jnp.concatenate(
                [jnp.concatenate([vth[:, r0:r0 + 2 * WINDOW], vth[:, self.meta0:]], axis=1),
                 self.ones], axis=0))
        outs = []
        for p, raw in enumerate(self.scores.pop(i)):
            h = (2 * p) // (Q_HEADS // KV_HEADS)
            s = jnp.where(mask, raw, -jnp.inf)
            sk = jnp.where(self.first_head, self.sink_ref[2 * p], self.sink_ref[2 * p + 1])
            m = jnp.maximum(jnp.max(s, axis=0, keepdims=True), sk)
            e = jnp.exp(s - m).astype(BF16)
            oa = _dot(vaug[h], e)
            o = oa[:HEAD_DIM] / (oa[HEAD_DIM:HEAD_DIM + 1] + jnp.exp(sk - m))
            outs += [o[:, :WINDOW], o[:, WINDOW:]]
        self.o_ref[r0:r0 + WINDOW, :] = jnp.concatenate(outs, axis=0).T.astype(self.o_ref.dtype)


def _cumsum_rows(x):
    t = x.shape[0]
    row = lax.broadcasted_iota(jnp.int32, x.shape, 0)
    d = 1
    while d < t:
        x = x + jnp.where(row >= d, pltpu.roll(x, d, axis=0), 0.0)
        d *= 2
    return x


def _hg_out(o, gate, nw):
    parts = []
    for h in range(HG_HEADS):
        oh = o[:, h * HG_DV:(h + 1) * HG_DV]
        parts.append(oh * lax.rsqrt(jnp.mean(oh * oh, axis=-1, keepdims=True) + EPS) * nw)
    return jnp.concatenate(parts, axis=1) * gate


_HG_SLICES = [slice(h * HG_DK, (h + 1) * HG_DK) for h in range(HG_HEADS)]


def _state_update(iv, kd):
    return jnp.concatenate([_dot_tn(iv[:, hs], kd[:, hs]) for hs in _HG_SLICES], axis=1)


def _meta_state(mk, mlogf, mi):
    b = _cumsum_rows(mlogf)
    kd = (mk * jnp.exp(b[-1:] - b)).astype(BF16)
    return _state_update(mi.astype(BF16), kd)


class _Hgrn:
    def __init__(self, q_ref, k_ref, logf_ref, i_ref, g_ref, nw, state, o_ref):
        self.q_ref, self.k_ref, self.logf_ref, self.i_ref, self.g_ref = q_ref, k_ref, logf_ref, i_ref, g_ref
        self.nw, self.state, self.o_ref = nw, state, o_ref
        t = HG_CHUNK
        self.nchunk = q_ref.shape[0] // t
        r = lax.broadcasted_iota(jnp.int32, (t, t), 0)
        c = lax.broadcasted_iota(jnp.int32, (t, t), 1)
        self.causal = r >= c
        self.cums, self.terms, self.entering = {}, {}, {}

    def rows(self, ci):
        return slice(ci * HG_CHUNK, (ci + 1) * HG_CHUNK)

    def issue_cumsum(self, ci):
        self.cums[ci] = _cumsum_rows(self.logf_ref[self.rows(ci), :])

    def issue_local(self, ci):
        rows = self.rows(ci)
        b = self.cums.pop(ci)
        bl = b[-1:]
        k = self.k_ref[rows, :].astype(F32)
        qe = (self.q_ref[rows, :].astype(F32) * jnp.exp(b)).astype(BF16)
        ke = (k * jnp.exp(-b)).astype(BF16)
        kd = (k * jnp.exp(bl - b)).astype(BF16)
        iv = self.i_ref[rows, :].astype(BF16)
        a = [jnp.where(self.causal, _dot_nt(qe[:, hs], ke[:, hs]), 0.0).astype(BF16) for hs in _HG_SLICES]
        self.terms[ci] = (qe, iv, a)
        self.entering[ci] = self.state.astype(BF16)
        self.state = self.state * jnp.exp(bl) + _state_update(iv, kd)

    def finish(self, ci):
        rows = self.rows(ci)
        qe, iv, a = self.terms.pop(ci)
        ent = self.entering.pop(ci)
        outs = [_dot(a[h], iv[:, hs]) + _dot_nt(qe[:, hs], ent[:, hs]) for h, hs in enumerate(_HG_SLICES)]
        o = _hg_out(jnp.concatenate(outs, axis=1), self.g_ref[rows, :].astype(F32), self.nw)
        self.o_ref[rows, :] = o.astype(self.o_ref.dtype)


def _load_as_bf16(w_hbm, w_vmem, stage, sem, meanwhile):
    nslot, rows = stage.shape[0], stage.shape[1]
    nslab = w_hbm.shape[0] // rows
    ahead = nslot - 1

    def slab_copy(c):
        return pltpu.make_async_copy(w_hbm.at[pl.ds(c * rows, rows), :], stage.at[c % nslot],
                                     sem.at[c % nslot])

    for c in range(min(ahead, nslab)):
        slab_copy(c).start()
    meanwhile()
    for c in range(nslab):
        if c + ahead < nslab:
            slab_copy(c + ahead).start()
        slab_copy(c).wait()
        w_vmem[pl.ds(c * rows, rows), :] = stage[c % nslot].astype(w_vmem.dtype)


def _mixer_kernel(sink_ref, x0_ref, xnext_ref, xs_ref, xm_ref, g_ref, lbp_ref, w_hbm, nw_ref,
                  wa32, wb32, wo32, wup32, wdn32,
                  att_ref, hg_ref, sga_ref, sgb_ref, sfin_ref, lastkv_ref,
                  qs_ref, kvs_ref, hqs_ref, hks_ref, his_ref, hgs_ref, sgas_ref, sgbs_ref,
                  wa16, wb16, wo16, wup16, wdn16,
                  w_ref, stage, sem, xn_s, zq, zkv, zhq, zhk, zlogf, zhi, zhg, kvm_ref, st_ref, mst_ref,
                  *, tiles_per_seq):
    s = pl.program_id(0)
    t = xnext_ref.shape[0]

    @pl.when(s == 0)
    def _():
        def first_tile_and_zeros():
            xn_s[...] = _rmsnorm(x0_ref[...], g_ref[...]).astype(xn_s.dtype)
            for ref in (zq, zkv, zhq, zhk, zlogf, zhi, zhg, st_ref):
                ref[...] = jnp.zeros(ref.shape, ref.dtype)

        _load_as_bf16(w_hbm, w_ref, stage, sem, first_tile_and_zeros)
        xs = _rows_from_linear(xs_ref)
        nb = xs.shape[0]
        small = jnp.concatenate([xs, xm_ref[...]], axis=0)
        p = _Projection(_rmsnorm(small, g_ref[...]).astype(BF16), lbp_ref[...], w_ref)
        q_small = p.q_att()
        low = lax.broadcasted_iota(jnp.int32, (1, LANES), 1) < HEAD_DIM
        for j in range(Q_HEADS):
            kv_head, half = j // (Q_HEADS // KV_HEADS), j % 2
            pair = q_small[:, (j // 2) * LANES:(j // 2 + 1) * LANES]
            moved = pair if half == kv_head else pltpu.roll(pair, HEAD_DIM, axis=1)
            qs_ref[:, j, :] = jnp.where(low if kv_head == 0 else jnp.logical_not(low), moved, 0.0)
        kv = p.kv()
        kvs_ref[...] = kv
        kvm_ref[...] = kv[nb:]
        hqs_ref[...] = p.q_hg()
        k, logf = p.forget()
        hks_ref[...] = k
        iv = p.i_hg()
        his_ref[...] = iv
        hgs_ref[...] = p.swish_gate()
        sgas_ref[...] = p.branch_gate(C_GA).astype(sgas_ref.dtype)
        sgbs_ref[...] = p.branch_gate(C_GB).astype(sgbs_ref.dtype)
        mst_ref[...] = _meta_state(k[nb:], logf[nb:], iv[nb:])

    first = lax.rem(jnp.maximum(s - 1, 0), tiles_per_seq) == 0

    att = _Attention(sink_ref, zq, jnp.concatenate([zkv[...], kvm_ref[...]], axis=0), first, att_ref)
    last = zkv[t:t + WINDOW, :]
    lastkv_ref[...] = last
    zkv[0:WINDOW, :] = last
    entering = jnp.where(first, mst_ref[...], st_ref[...])
    hg = _Hgrn(zhq, zhk, zlogf, zhi, zhg, nw_ref[...], entering, hg_ref)
    proj = _Projection(xn_s, lbp_ref[...], w_ref)

    def gate_a(p):
        sga_ref[:, p] = proj.branch_gate(C_GA, p).astype(sga_ref.dtype)

    def gate_b(p):
        sgb_ref[:, p] = proj.branch_gate(C_GB, p).astype(sgb_ref.dtype)

    def new_kv(_):
        zkv[WINDOW:WINDOW + t, :] = proj.kv()

    def new_forget(p):
        k_new, logf_new = proj.forget(p)
        zhk[:, p] = k_new.astype(zhk.dtype)
        zlogf[:, p] = logf_new

    def new_q(p):
        zq[:, p] = proj.q_att(p).astype(zq.dtype)

    def new_hq(p):
        zhq[:, p] = proj.q_hg(p).astype(zhq.dtype)

    def new_hi(p):
        zhi[:, p] = proj.i_hg(p).astype(zhi.dtype)

    def new_gate(p):
        zhg[:, p] = proj.swish_gate(p).astype(zhg.dtype)

    def parts(n):
        return [slice(c, c + PROJ_GROUP) for c in range(0, n, PROJ_GROUP)]

    groups = ([(gate_a, p) for p in parts(D_MODEL)] + [(gate_b, p) for p in parts(D_MODEL)]
              + [(new_kv, None)] + [(new_q, p) for p in parts(D_ATT)]
              + [(new_forget, p) for p in parts(D_HG)] + [(new_hq, p) for p in parts(D_HG)]
              + [(new_hi, p) for p in parts(D_HG)] + [(new_gate, p) for p in parts(D_HG)])

    def save_state():
        st_ref[...] = hg.state
        for h, hs in enumerate(_HG_SLICES):
            sfin_ref[h] = hg.state[:, hs].T

    qk, cs, loc, fin, out = att.issue_scores, hg.issue_cumsum, hg.issue_local, att.finish, hg.finish
    mixer_work = {
        0: [(qk, 0), (cs, 0), (cs, 1), (cs, 2), (cs, 3)],
        1: [(cs, 4), (cs, 5), (loc, 0)],
        2: [(cs, 6), (cs, 7), (loc, 1)],
        3: [(qk, 1), (loc, 2)],
        4: [(fin, 0), (loc, 3)],
        5: [(out, 0), (loc, 4)],
        6: [(qk, 2), (out, 1), (loc, 5)],
        7: [(fin, 1), (out, 2), (loc, 6)],
        8: [(qk, 3), (out, 3), (loc, 7), (save_state,)],
        9: [(out, 4)],
        10: [(fin, 2), (out, 5)],
        11: [(out, 6)],
        12: [(out, 7)],
        13: [(fin, 3)],
    }
    for slot, (fn, p) in enumerate(groups):
        fn(p)
        for item in mixer_work.get(slot, []):
            item[0](*item[1:])
    xn_s[...] = _rmsnorm(xnext_ref[...], g_ref[...]).astype(xn_s.dtype)
    for src, dst in ((wa32, wa16), (wb32, wb16), (wo32, wo16), (wup32, wup16), (wdn32, wdn16)):
        dst[...] = src[...].astype(dst.dtype)


def _mixers(sinks, x, x_sample, x_meta, g, lb_param, w_f32, hg_norm, later_weights, bsz):
    n = x.shape[0]
    t = MIX_ROWS
    nt = n // t
    per_seq = nt // bsz
    rows = x_sample.shape[0] * LANES // D_MODEL + x_meta.shape[0]

    def this_tile(c):
        return pl.BlockSpec((t, c), lambda s: (jnp.minimum(s, nt - 1), 0))

    def prev_tile(c):
        return pl.BlockSpec((t, c), lambda s: (jnp.maximum(s - 1, 0), 0))

    def prev_seq(shape):
        return pl.BlockSpec((None,) + shape,
                            lambda s: (jnp.maximum(s - 1, 0) // per_seq,) + (0,) * len(shape))

    def small(c):
        return pl.BlockSpec((rows, c), lambda s: (0, 0))

    first_tile = pl.BlockSpec((t, D_MODEL), lambda s: (0, 0), pipeline_mode=pl.Buffered(1))
    next_tile = pl.BlockSpec((t, D_MODEL), lambda s: (jnp.minimum(s + 1, nt - 1), 0))
    small_widths = (2 * D_KV, D_HG, D_HG, D_HG, D_HG, D_MODEL, D_MODEL)
    small_dtypes = (F32, F32, F32, F32, F32, BF16, BF16)
    qm_shape = (rows, Q_HEADS, LANES)
    qm_spec = pl.BlockSpec(qm_shape, lambda s: (0, 0, 0))

    def row_block(wt):
        return pl.BlockSpec((wt.shape[0] // nt, wt.shape[1]), lambda s: (jnp.minimum(s, nt - 1), 0))

    return pl.pallas_call(
        functools.partial(_mixer_kernel, tiles_per_seq=per_seq),
        grid=(nt + 1,),
        in_specs=[pl.BlockSpec(memory_space=pltpu.SMEM), first_tile, next_tile,
                  _resident(x_sample.shape), _resident(x_meta.shape), _resident((1, D_MODEL)),
                  _resident(lb_param.shape), pl.BlockSpec(memory_space=pl.ANY), _resident((1, HG_DV))]
                 + [row_block(wt) for wt in later_weights],
        out_specs=[prev_tile(D_ATT), prev_tile(D_HG), this_tile(D_MODEL), this_tile(D_MODEL),
                   prev_seq((HG_HEADS, HG_DK, HG_DV)), prev_seq((WINDOW, 2 * D_KV))]
                  + [qm_spec] + [small(c) for c in small_widths] + [row_block(wt) for wt in later_weights],
        out_shape=[jax.ShapeDtypeStruct((n, D_ATT), BF16), jax.ShapeDtypeStruct((n, D_HG), BF16),
                   jax.ShapeDtypeStruct((n, D_MODEL), BF16), jax.ShapeDtypeStruct((n, D_MODEL), BF16),
                   jax.ShapeDtypeStruct((bsz, HG_HEADS, HG_DK, HG_DV), F32),
                   jax.ShapeDtypeStruct((bsz, WINDOW, 2 * D_KV), F32)]
                  + [jax.ShapeDtypeStruct(qm_shape, F32)]
                  + [jax.ShapeDtypeStruct((rows, c), d) for c, d in zip(small_widths, small_dtypes)]
                  + [jax.ShapeDtypeStruct(wt.shape, BF16) for wt in later_weights],
        scratch_shapes=[pltpu.VMEM((D_MODEL, D_IN), BF16),
                        pltpu.VMEM((WEIGHT_SLOTS, WEIGHT_SLAB_ROWS, D_IN), F32),
                        pltpu.SemaphoreType.DMA((WEIGHT_SLOTS,)),
                        pltpu.VMEM((t, D_MODEL), BF16),
                        pltpu.VMEM((t, D_ATT), BF16), pltpu.VMEM((WINDOW + t, 2 * D_KV), F32),
                        pltpu.VMEM((t, D_HG), BF16), pltpu.VMEM((t, D_HG), BF16),
                        pltpu.VMEM((t, D_HG), F32), pltpu.VMEM((t, D_HG), BF16),
                        pltpu.VMEM((t, D_HG), BF16),
                        pltpu.VMEM((N_META, 2 * D_KV), F32),
                        pltpu.VMEM((HG_DV, D_HG), F32),
                        pltpu.VMEM((HG_DV, D_HG), F32)],
        compiler_params=_params("arbitrary"),
        name="mixers",
    )(sinks, x, x, x_sample, x_meta, g, lb_param, w_f32, hg_norm, *later_weights)


def _merge_ffn_rows(x, att, hg, sga, sgb, wa_ref, wb_ref, wo_ref, ln_ffn, wup_ref, wdn_ref, ln_f):
    ya = _dot(att.astype(BF16), wa_ref[...])
    yb = _dot(hg.astype(BF16), wb_ref[...])
    mix = sga.astype(F32) * ya + sgb.astype(F32) * yb
    h1 = x + _dot(mix.astype(BF16), wo_ref[...])
    xn = _rmsnorm(h1, ln_ffn).astype(BF16)
    acc = jnp.zeros_like(h1)
    for c in range(0, D_FF, FFN_CHUNK):
        u = jnp.maximum(_dot(xn, wup_ref[:, c:c + FFN_CHUNK]), 0.0)
        acc = acc + _dot((u * u).astype(BF16), wdn_ref[c:c + FFN_CHUNK, :])
    return _rmsnorm(h1 + acc, ln_f)


def _group_window(gi, group):
    span = max(group, SUBLANES)
    per = span // group
    return pl.ds(pl.multiple_of((gi // per) * span, span), span), lax.rem(gi, per) * group


def _load_group(ref, gi, group):
    window, off = _group_window(gi, group)
    blk = ref[window, :]
    out = blk[0:group]
    for j in range(1, blk.shape[0] // group):
        out = jnp.where(off == j * group, blk[j * group:(j + 1) * group], out)
    return out


def _store_group(ref, gi, group, value):
    window, off = _group_window(gi, group)
    span = window.size
    if span == group:
        ref[window, :] = value
        return
    row = lax.broadcasted_iota(jnp.int32, (span, 1), 0)
    mine = jnp.logical_and(row >= off, row < off + group)
    ref[window, :] = jnp.where(mine, jnp.concatenate([value] * (span // group), axis=0), ref[window, :])


def _sample_hgrn_group(gi, nw, q_ref, k_ref, i_ref, g_ref, s_ref, snew_ref, hg_ref):
    group = s_ref.shape[0]
    k = _load_group(k_ref, gi, group)
    q = _load_group(q_ref, gi, group)
    iv = _load_group(i_ref, gi, group)
    r = lax.broadcasted_iota(jnp.int32, (HG_DK, HG_DK), 0)
    c = lax.broadcasted_iota(jnp.int32, (HG_DK, HG_DK), 1)
    eye = r == c

    def column(row):
        return jnp.sum(jnp.where(eye, row, 0.0), axis=1, keepdims=True)

    outs = []
    for b in range(s_ref.shape[0]):
        heads = []
        for h, hs in enumerate(_HG_SLICES):
            kc = column(k[b:b + 1, hs])
            qc = column(q[b:b + 1, hs])
            s_old = s_ref[b, h]
            s = s_old + kc * (iv[b:b + 1, hs] - s_old)
            snew_ref[b, h] = s
            heads.append(jnp.sum(qc * s, axis=0, keepdims=True))
        outs.append(jnp.concatenate(heads, axis=1))
    _store_group(hg_ref, gi, group,
                 _hg_out(jnp.concatenate(outs, axis=0), _load_group(g_ref, gi, group), nw))


def _merge_ffn_kernel(x_ref, att_ref, hg_ref, sga_ref, sgb_ref,
                      xs_ref, atts_ref, sgas_ref, sgbs_ref,
                      nw_ref, hqs_ref, hks_ref, his_ref, hgates_ref, state_ref,
                      kvs_ref, ckt_ref, cvt_ref,
                      wa_ref, wb_ref, wo_ref, wup_ref, wdn_ref, lnffn_ref, lnf_ref,
                      y_ref, ys_ref, snew_ref, nkt_ref, nvt_ref, hgs, *, prompt_steps):
    i = pl.program_id(0)
    weights = (wa_ref, wb_ref, wo_ref, lnffn_ref[...], wup_ref, wdn_ref, lnf_ref[...])
    group = state_ref.shape[0]
    ngroups = atts_ref.shape[0] // group

    @pl.when(i < prompt_steps)
    def _():
        y_ref[...] = _merge_ffn_rows(x_ref[...], att_ref[...], hg_ref[...], sga_ref[...], sgb_ref[...],
                                     *weights)
        gi = jnp.minimum(i, ngroups - 1)
        _sample_hgrn_group(gi, nw_ref[...], hqs_ref, hks_ref, his_ref, hgates_ref,
                           state_ref, snew_ref, hgs)
        kvn = _load_group(kvs_ref, gi, group)
        r = lax.broadcasted_iota(jnp.int32, (D_KV, WINDOW), 0)
        c = lax.broadcasted_iota(jnp.int32, (D_KV, WINDOW), 1)
        for b in range(group):
            for src, dst, lo in ((ckt_ref, nkt_ref, 0), (cvt_ref, nvt_ref, D_KV)):
                col = jnp.sum(jnp.where(r == c, kvn[b:b + 1, lo:lo + D_KV], 0.0), axis=1, keepdims=True)
                dst[b] = jnp.where(c == WINDOW - 1, col, pltpu.roll(src[b], WINDOW - 1, axis=1))

    @pl.when(i == prompt_steps)
    def _():
        _rows_to_linear(ys_ref, _merge_ffn_rows(_rows_from_linear(xs_ref), atts_ref[...], hgs[...],
                                                sgas_ref[...], sgbs_ref[...], *weights))


def _merge_ffn(x, att, hg, sga, sgb, xs, att_s, sga_s, sgb_s, hg_norm, hq_s, hk_s, hi_s, hgate_s, state,
               kv_s, ck_t, cv_t, wa, wb, wo, ln_ffn, w_up, w_down, ln_f):
    n = x.shape[0]
    nb = att_s.shape[0]
    linear = pl.BlockSpec(xs.shape, lambda i: (0, 0))
    rows = MERGE_ROWS
    nt = n // rows
    g = SAMPLE_HG_GROUP
    assert nb // g <= nt

    def blk(c):
        return pl.BlockSpec((rows, c), lambda i: (jnp.minimum(i, nt - 1), 0))

    def sample(c):
        return pl.BlockSpec((nb, c), lambda i: (0, 0))

    sblk = pl.BlockSpec((g, HG_HEADS, HG_DK, HG_DV), lambda i: (jnp.minimum(i, nb // g - 1), 0, 0, 0))
    cblk = pl.BlockSpec((g, D_KV, WINDOW), lambda i: (jnp.minimum(i, nb // g - 1), 0, 0))
    return pl.pallas_call(
        functools.partial(_merge_ffn_kernel, prompt_steps=nt),
        grid=(nt + 1,),
        in_specs=[blk(D_MODEL), blk(D_ATT), blk(D_HG), blk(D_MODEL), blk(D_MODEL),
                  linear, sample(D_ATT), sample(D_MODEL), sample(D_MODEL),
                  _resident((1, HG_DV)), _resident(hq_s.shape), _resident(hk_s.shape),
                  _resident(hi_s.shape), _resident(hgate_s.shape), sblk,
                  _resident(kv_s.shape), cblk, cblk,
                  _resident(wa.shape), _resident(wb.shape), _resident(wo.shape),
                  _resident(w_up.shape), _resident(w_down.shape),
                  _resident((1, D_MODEL)), _resident((1, D_MODEL))],
        out_specs=[blk(D_MODEL), linear, sblk, cblk, cblk],
        out_shape=[jax.ShapeDtypeStruct((n, D_MODEL), F32), jax.ShapeDtypeStruct(xs.shape, F32),
                   jax.ShapeDtypeStruct(state.shape, F32),
                   jax.ShapeDtypeStruct(ck_t.shape, F32), jax.ShapeDtypeStruct(cv_t.shape, F32)],
        scratch_shapes=[pltpu.VMEM((nb, D_HG), F32)],
        compiler_params=_params("arbitrary"),
        name="merge_ffn",
    )(x, att, hg, sga, sgb, xs, att_s, sga_s, sgb_s, hg_norm, hq_s, hk_s, hi_s, hgate_s, state,
      kv_s, ck_t, cv_t, wa, wb, wo, w_up, w_down, ln_ffn, ln_f)


def _sample_attn_kernel(sink_ref, qm_ref, kvn_ref, ckt_ref, cvt_ref, mkt_ref, mvt_ref,
                        att_ref, o_all, mk_s, mv_s):
    nb = qm_ref.shape[0]
    first_seq = pl.multiple_of(pl.program_id(0) * nb, nb)

    @pl.when(pl.program_id(0) == 0)
    def _():
        for r in range(N_META):
            mk_s[:, r, :] = mkt_ref[r].T
            mv_s[:, r, :] = mvt_ref[r].T
    head = lax.broadcasted_iota(jnp.int32, (Q_HEADS, 1), 0)
    sk = jnp.zeros((Q_HEADS, 1), F32)
    for j in range(Q_HEADS):
        sk = jnp.where(head == j, sink_ref[j], sk)
    oldest = lax.broadcasted_iota(jnp.int32, (Q_HEADS, WINDOW), 1) == 0
    kvn = kvn_ref[...]
    qms = [qm_ref[b] for b in range(nb)]
    scores = [(_dot(qms[b].astype(BF16), ckt_ref[b].astype(BF16)),
               _dot_nt(qms[b].astype(BF16), mk_s[first_seq + b].astype(BF16)))
              for b in range(nb)]
    for b, (s_w, s_m) in enumerate(scores):
        s_w = jnp.where(oldest, -jnp.inf, s_w)
        s_n = jnp.sum(qms[b] * kvn[b:b + 1, :D_KV], axis=1, keepdims=True)
        m = jnp.maximum(jnp.maximum(jnp.max(s_w, axis=1, keepdims=True),
                                    jnp.max(s_m, axis=1, keepdims=True)), jnp.maximum(s_n, sk))
        e_w = jnp.exp(s_w - m)
        e_m = jnp.exp(s_m - m)
        e_n = jnp.exp(s_n - m)
        l = (jnp.sum(e_w, axis=1, keepdims=True) + jnp.sum(e_m, axis=1, keepdims=True) + e_n
             + jnp.exp(sk - m))
        o = (_dot_nt(e_w.astype(BF16), cvt_ref[b].astype(BF16))
             + _dot(e_m.astype(BF16), mv_s[first_seq + b].astype(BF16))
             + e_n * kvn[b:b + 1, D_KV:])
        o_all[b] = o / l
    low = lax.broadcasted_iota(jnp.int32, (1, LANES), 1) < HEAD_DIM
    pairs = []
    for p in range(Q_HEADS // 2):
        halves = []
        for half in range(2):
            j = 2 * p + half
            kv_head = j // (Q_HEADS // KV_HEADS)
            oj = jnp.where(low if kv_head == 0 else jnp.logical_not(low), o_all[:, j, :], 0.0)
            halves.append(oj if half == kv_head else pltpu.roll(oj, HEAD_DIM, axis=1))
        pairs.append(halves[0] + halves[1])
    att_ref[...] = jnp.concatenate(pairs, axis=1)


def _sample_attention(sinks, qm, kv_new, ck, cv, mk, mv):
    nb = ck.shape[0]
    g = SAMPLE_ATT_GROUP

    def blk3(a, c):
        return pl.BlockSpec((g, a, c), lambda i: (i, 0, 0))

    return pl.pallas_call(
        _sample_attn_kernel,
        grid=(nb // g,),
        in_specs=[pl.BlockSpec(memory_space=pltpu.SMEM), blk3(Q_HEADS, D_KV),
                  pl.BlockSpec((g, 2 * D_KV), lambda i: (i, 0)),
                  blk3(D_KV, WINDOW), blk3(D_KV, WINDOW), _resident(mk.shape), _resident(mv.shape)],
        out_specs=pl.BlockSpec((g, D_ATT), lambda i: (i, 0)),
        out_shape=jax.ShapeDtypeStruct((nb, D_ATT), F32),
        scratch_shapes=[pltpu.VMEM((g, Q_HEADS, LANES), F32),
                        pltpu.VMEM((nb, N_META, D_KV), F32), pltpu.VMEM((nb, N_META, D_KV), F32)],
        compiler_params=_params("arbitrary"),
        name="sample_attn",
    )(sinks, qm, kv_new, ck, cv, mk, mv)


def kernel(x_prompt, x_sample, cache_k, cache_v, cache_meta_k, cache_meta_v, state_hgrn, meta,
           w_in, sinks, lb_param, hg_norm, w_att_out, w_hg_out, w_o, ln_mix, ln_ffn, w_up,
           w_down, ln_f):
    bsz, seq, _ = x_prompt.shape
    nb = x_sample.shape[0]
    ln_mix2 = ln_mix.reshape(1, D_MODEL)
    ln_ffn2 = ln_ffn.reshape(1, D_MODEL)
    ln_f2 = ln_f.reshape(1, D_MODEL)
    nw = hg_norm.reshape(1, HG_DV)

    xs = x_sample.reshape(nb * D_MODEL // LANES, LANES)

    xp = x_prompt.reshape(bsz * seq, D_MODEL)
    (att_p, hg_p, sga_p, sgb_p, state_p, lastkv_p,
     qm_s, kv_s, hq_s, hk_s, hi_s, hgate_s, sga_s, sgb_s,
     wa, wb, wo, wup, wdn) = _mixers(
        sinks.reshape(Q_HEADS), xp, xs, meta, ln_mix2, lb_param, w_in[0], nw,
        (w_att_out[0], w_hg_out[0], w_o[0], w_up[0], w_down[0]), bsz)

    def window_t(c):
        return jnp.swapaxes(c[0].reshape(nb, WINDOW, D_KV), 1, 2)

    def meta_t(c):
        return jnp.transpose(c[0].reshape(nb, N_META, D_KV), (1, 2, 0))

    ck_t, cv_t = window_t(cache_k), window_t(cache_v)
    att_s = _sample_attention(sinks.reshape(Q_HEADS), qm_s, kv_s, ck_t, cv_t,
                              meta_t(cache_meta_k), meta_t(cache_meta_v))

    y_p, y_s, state_s, nkt_s, nvt_s = _merge_ffn(
        xp, att_p, hg_p, sga_p, sgb_p, xs, att_s, sga_s, sgb_s,
        nw, hq_s, hk_s, hi_s, hgate_s, state_hgrn[0], kv_s, ck_t, cv_t,
        wa, wb, wo, ln_ffn2, wup, wdn, ln_f2)

    kv5 = lastkv_p.reshape(bsz, WINDOW, 2, KV_HEADS, HEAD_DIM)
    meta5 = jnp.broadcast_to(kv_s[nb:].reshape(1, N_META, 2, KV_HEADS, HEAD_DIM),
                             (bsz, N_META, 2, KV_HEADS, HEAD_DIM))
    return (y_p.reshape(bsz, seq, D_MODEL),
            y_s.reshape(nb, 1, D_MODEL),
            kv5[None, :, :, 0],
            kv5[None, :, :, 1],
            meta5[None, :, :, 0],
            meta5[None, :, :, 1],
            state_p[None],
            jnp.swapaxes(nkt_s, 1, 2).reshape(1, nb, WINDOW, KV_HEADS, HEAD_DIM),
            jnp.swapaxes(nvt_s, 1, 2).reshape(1, nb, WINDOW, KV_HEADS, HEAD_DIM),
            state_s[None])
```

```python
import functools

import jax
import jax.numpy as jnp
from jax import lax
from jax.experimental import pallas as pl
from jax.experimental.pallas import tpu as pltpu

F32 = jnp.float32
BF16 = jnp.bfloat16

D_MODEL = 1024
N_META = 16
WINDOW = 128
HEAD_DIM = 64
Q_HEADS = 8
KV_HEADS = 2
D_ATT = Q_HEADS * HEAD_DIM
D_KV = KV_HEADS * HEAD_DIM
HG_HEADS = 4
HG_DK = 128
HG_DV = 128
D_HG = HG_HEADS * HG_DK
HG_CHUNK = 64
D_FF = 4 * D_MODEL
EPS = 1e-6
C_Q = 0
C_KV = C_Q + D_ATT
C_HQ = C_KV + 2 * D_KV
C_HF = C_HQ + D_HG
C_HI = C_HF + D_HG
C_HGATE = C_HI + D_HG
C_GA = C_HGATE + D_HG
C_GB = C_GA + D_MODEL
D_IN = C_GB + D_MODEL

VMEM_LIMIT_BYTES = 56 * 1024 * 1024
MIX_ROWS = 512
PROJ_GROUP = 256
WEIGHT_SLAB_ROWS = 64
WEIGHT_SLOTS = 4
MERGE_ROWS = 512
FFN_CHUNK = 2048
LANES = 128
BF16_ROWS = 16
SAMPLE_ATT_GROUP = 32
SAMPLE_HG_GROUP = 8

_NT = (((1,), (1,)), ((), ()))
_TN = (((0,), (0,)), ((), ()))


def _dot(a, b):
    return jnp.dot(a, b, preferred_element_type=F32)


def _dot_nt(a, b):
    return lax.dot_general(a, b, _NT, preferred_element_type=F32)


def _dot_tn(a, b):
    return lax.dot_general(a, b, _TN, preferred_element_type=F32)


def _rows_from_linear(ref):
    per = D_MODEL // LANES
    rows = ref.shape[0] // per
    return jnp.concatenate([ref[pl.ds(c, rows, stride=per), :] for c in range(per)], axis=1)


def _rows_to_linear(ref, value):
    per = D_MODEL // LANES
    for c in range(per):
        ref[pl.ds(c, value.shape[0], stride=per), :] = value[:, c * LANES:(c + 1) * LANES]


def _rmsnorm(x, g):
    return x * lax.rsqrt(jnp.mean(x * x, axis=-1, keepdims=True) + EPS) * g


def _resident(shape):
    return pl.BlockSpec(shape, lambda *_: (0,) * len(shape), pipeline_mode=pl.Buffered(1))


def _params(*sem):
    return pltpu.CompilerParams(dimension_semantics=sem, vmem_limit_bytes=VMEM_LIMIT_BYTES)


def _lower_bound(lbp):
    e = jnp.exp(lbp - jnp.max(lbp, axis=0, keepdims=True))
    return e[0:1] / jnp.sum(e, axis=0, keepdims=True)


class _Projection:
    def __init__(self, xn, lbp, w_ref):
        self.xn = xn
        self.w_ref = w_ref
        self.lb = _lower_bound(lbp)

    def cols(self, base, part):
        return _dot(self.xn[...], self.w_ref[:, base + part.start:base + part.stop])

    def q_att(self, part=slice(0, D_ATT)):
        return self.cols(C_Q, part) * (HEAD_DIM ** -0.5)

    def kv(self):
        return self.cols(C_KV, slice(0, 2 * D_KV))

    def q_hg(self, part=slice(0, D_HG)):
        return self.cols(C_HQ, part) * (HG_DK ** -0.5)

    def forget(self, part=slice(0, D_HG)):
        lb = self.lb[:, part]
        f = lb + (1.0 - lb) * jax.nn.sigmoid(self.cols(C_HF, part))
        return 1.0 - f, jnp.log(f)

    def i_hg(self, part=slice(0, D_HG)):
        return self.cols(C_HI, part)

    def swish_gate(self, part=slice(0, D_HG)):
        g = self.cols(C_HGATE, part)
        return g * jax.nn.sigmoid(g)

    def branch_gate(self, base, part=slice(0, D_MODEL)):
        return jax.nn.sigmoid(self.cols(base, part))


class _Attention:
    def __init__(self, sink_ref, q_ref, kv, first, o_ref):
        self.sink_ref, self.q_ref, self.first, self.o_ref = sink_ref, q_ref, first, o_ref
        self.nsub = q_ref.shape[0] // WINDOW
        self.nk = 2 * WINDOW + N_META
        self.meta0 = WINDOW + q_ref.shape[0]
        lane = lax.broadcasted_iota(jnp.int32, (kv.shape[0], D_KV), 1)
        low = lane < HEAD_DIM
        k = kv[:, :D_KV]
        ksw = pltpu.roll(k, HEAD_DIM, axis=1)
        self.kboth = (jnp.where(low, k, ksw).astype(BF16), jnp.where(low, ksw, k).astype(BF16))
        qlane = lax.broadcasted_iota(jnp.int32, (1, LANES), 1)
        self.keep = (jnp.where(qlane < HEAD_DIM, 1.0, 0.0).astype(BF16),
                     jnp.where(qlane < HEAD_DIM, 0.0, 1.0).astype(BF16))
        r = lax.broadcasted_iota(jnp.int32, (D_KV, D_KV), 0)
        c = lax.broadcasted_iota(jnp.int32, (D_KV, D_KV), 1)
        eye = jnp.where(r == c, 1.0, 0.0).astype(BF16)
        self.vt = _dot_nt(eye, kv[:, D_KV:].astype(BF16)).astype(BF16)
        self.ones = jnp.ones((BF16_ROWS, self.nk), BF16)
        self.key = lax.broadcasted_iota(jnp.int32, (self.nk, 2 * WINDOW), 0)
        col = lax.broadcasted_iota(jnp.int32, (self.nk, 2 * WINDOW), 1)
        self.qry = jnp.bitwise_and(col, WINDOW - 1)
        self.first_head = lax.broadcasted_iota(jnp.int32, (1, 2 * WINDOW), 1) < WINDOW
        self.scores = {}

    def issue_scores(self, i):
        r0 = i * WINDOW
        out = []
        for p in range(Q_HEADS // 2):
            kb = self.kboth[(2 * p) // (Q_HEADS // KV_HEADS)]
            kmat = jnp.concatenate([kb[r0:r0 + 2 * WINDOW], kb[self.meta0:]], axis=0)
            qp = self.q_ref[r0:r0 + WINDOW, p * LANES:(p + 1) * LANES]
            q2 = jnp.concatenate([qp * self.keep[0], qp * self.keep[1]], axis=0)
            out.append(_dot_nt(kmat, q2))
        self.scores[i] = out

    def finish(self, i):
        r0 = i * WINDOW
        lo = jnp.where(self.first, WINDOW - 1, self.qry) if i == 0 else self.qry
        mask = jnp.logical_and(self.key > lo, self.key <= self.qry + WINDOW)
        mask = jnp.logical_or(mask, self.key >= 2 * WINDOW)
        vaug = []
        for h in range(KV_HEADS):
            vth = self.vt[h * HEAD_DIM:(h + 1) * HEAD_DIM]
            vaug.append(jnp.concatenate(
                [jnp.concatenate([vth[:, r0:r0 + 2 * WINDOW], vth[:, self.meta0:]], axis=1),
                 self.ones], axis=0))
        outs = []
        for p, raw in enumerate(self.scores.pop(i)):
            h = (2 * p) // (Q_HEADS // KV_HEADS)
            s = jnp.where(mask, raw, -jnp.inf)
            sk = jnp.where(self.first_head, self.sink_ref[2 * p], self.sink_ref[2 * p + 1])
            m = jnp.maximum(jnp.max(s, axis=0, keepdims=True), sk)
            e = jnp.exp(s - m).astype(BF16)
            oa = _dot(vaug[h], e)
            o = oa[:HEAD_DIM] / (oa[HEAD_DIM:HEAD_DIM + 1] + jnp.exp(sk - m))
            outs += [o[:, :WINDOW], o[:, WINDOW:]]
        self.o_ref[r0:r0 + WINDOW, :] = jnp.concatenate(outs, axis=0).T.astype(self.o_ref.dtype)


def _cumsum_rows(x):
    t = x.shape[0]
    row = lax.broadcasted_iota(jnp.int32, x.shape, 0)
    d = 1
    while d < t:
        x = x + jnp.where(row >= d, pltpu.roll(x, d, axis=0), 0.0)
        d *= 2
    return x


def _hg_out(o, gate, nw):
    parts = []
    for h in range(HG_HEADS):
        oh = o[:, h * HG_DV:(h + 1) * HG_DV]
        parts.append(oh * lax.rsqrt(jnp.mean(oh * oh, axis=-1, keepdims=True) + EPS) * nw)
    return jnp.concatenate(parts, axis=1) * gate


_HG_SLICES = [slice(h * HG_DK, (h + 1) * HG_DK) for h in range(HG_HEADS)]


def _state_update(iv, kd):
    return jnp.concatenate([_dot_tn(iv[:, hs], kd[:, hs]) for hs in _HG_SLICES], axis=1)


def _meta_state(mk, mlogf, mi):
    b = _cumsum_rows(mlogf)
    kd = (mk * jnp.exp(b[-1:] - b)).astype(BF16)
    return _state_update(mi.astype(BF16), kd)


class _Hgrn:
    def __init__(self, q_ref, k_ref, logf_ref, i_ref, g_ref, nw, state, o_ref):
        self.q_ref, self.k_ref, self.logf_ref, self.i_ref, self.g_ref = q_ref, k_ref, logf_ref, i_ref, g_ref
        self.nw, self.state, self.o_ref = nw, state, o_ref
        t = HG_CHUNK
        self.nchunk = q_ref.shape[0] // t
        r = lax.broadcasted_iota(jnp.int32, (t, t), 0)
        c = lax.broadcasted_iota(jnp.int32, (t, t), 1)
        self.causal = r >= c
        self.cums, self.terms, self.entering = {}, {}, {}

    def rows(self, ci):
        return slice(ci * HG_CHUNK, (ci + 1) * HG_CHUNK)

    def issue_cumsum(self, ci):
        self.cums[ci] = _cumsum_rows(self.logf_ref[self.rows(ci), :])

    def issue_local(self, ci):
        rows = self.rows(ci)
        b = self.cums.pop(ci)
        bl = b[-1:]
        k = self.k_ref[rows, :].astype(F32)
        qe = (self.q_ref[rows, :].astype(F32) * jnp.exp(b)).astype(BF16)
        ke = (k * jnp.exp(-b)).astype(BF16)
        kd = (k * jnp.exp(bl - b)).astype(BF16)
        iv = self.i_ref[rows, :].astype(BF16)
        a = [jnp.where(self.causal, _dot_nt(qe[:, hs], ke[:, hs]), 0.0).astype(BF16) for hs in _HG_SLICES]
        self.terms[ci] = (qe, iv, a)
        self.entering[ci] = self.state.astype(BF16)
        self.state = self.state * jnp.exp(bl) + _state_update(iv, kd)

    def finish(self, ci):
        rows = self.rows(ci)
        qe, iv, a = self.terms.pop(ci)
        ent = self.entering.pop(ci)
        outs = [_dot(a[h], iv[:, hs]) + _dot_nt(qe[:, hs], ent[:, hs]) for h, hs in enumerate(_HG_SLICES)]
        o = _hg_out(jnp.concatenate(outs, axis=1), self.g_ref[rows, :].astype(F32), self.nw)
        self.o_ref[rows, :] = o.astype(self.o_ref.dtype)


def _load_as_bf16(w_hbm, w_vmem, stage, sem, meanwhile):
    nslot, rows = stage.shape[0], stage.shape[1]
    nslab = w_hbm.shape[0] // rows
    ahead = nslot - 1

    def slab_copy(c):
        return pltpu.make_async_copy(w_hbm.at[pl.ds(c * rows, rows), :], stage.at[c % nslot],
                                     sem.at[c % nslot])

    for c in range(min(ahead, nslab)):
        slab_copy(c).start()
    meanwhile()
    for c in range(nslab):
        if c + ahead < nslab:
            slab_copy(c + ahead).start()
        slab_copy(c).wait()
        w_vmem[pl.ds(c * rows, rows), :] = stage[c % nslot].astype(w_vmem.dtype)


def _mixer_kernel(sink_ref, x0_ref, xnext_ref, xs_ref, xm_ref, g_ref, lbp_ref, w_hbm, nw_ref,
                  wa32, wb32, wo32, wup32, wdn32,
                  att_ref, hg_ref, sga_ref, sgb_ref, sfin_ref, lastkv_ref,
                  qs_ref, kvs_ref, hqs_ref, hks_ref, his_ref, hgs_ref, sgas_ref, sgbs_ref,
                  wa16, wb16, wo16, wup16, wdn16,
                  w_ref, stage, sem, xn_s, zq, zkv, zhq, zhk, zlogf, zhi, zhg, kvm_ref, st_ref, mst_ref,
                  *, tiles_per_seq):
    s = pl.program_id(0)
    t = xnext_ref.shape[0]

    @pl.when(s == 0)
    def _():
        def first_tile_and_zeros():
            xn_s[...] = _rmsnorm(x0_ref[...], g_ref[...]).astype(xn_s.dtype)
            for ref in (zq, zkv, zhq, zhk, zlogf, zhi, zhg, st_ref):
                ref[...] = jnp.zeros(ref.shape, ref.dtype)

        _load_as_bf16(w_hbm, w_ref, stage, sem, first_tile_and_zeros)
        xs = _rows_from_linear(xs_ref)
        nb = xs.shape[0]
        small = jnp.concatenate([xs, xm_ref[...]], axis=0)
        p = _Projection(_rmsnorm(small, g_ref[...]).astype(BF16), lbp_ref[...], w_ref)
        q_small = p.q_att()
        low = lax.broadcasted_iota(jnp.int32, (1, LANES), 1) < HEAD_DIM
        for j in range(Q_HEADS):
            kv_head, half = j // (Q_HEADS // KV_HEADS), j % 2
            pair = q_small[:, (j // 2) * LANES:(j // 2 + 1) * LANES]
            moved = pair if half == kv_head else pltpu.roll(pair, HEAD_DIM, axis=1)
            qs_ref[:, j, :] = jnp.where(low if kv_head == 0 else jnp.logical_not(low), moved, 0.0)
        kv = p.kv()
        kvs_ref[...] = kv
        kvm_ref[...] = kv[nb:]
        hqs_ref[...] = p.q_hg()
        k, logf = p.forget()
        hks_ref[...] = k
        iv = p.i_hg()
        his_ref[...] = iv
        hgs_ref[...] = p.swish_gate()
        sgas_ref[...] = p.branch_gate(C_GA).astype(sgas_ref.dtype)
        sgbs_ref[...] = p.branch_gate(C_GB).astype(sgbs_ref.dtype)
        mst_ref[...] = _meta_state(k[nb:], logf[nb:], iv[nb:])

    first = lax.rem(jnp.maximum(s - 1, 0), tiles_per_seq) == 0

    att = _Attention(sink_ref, zq, jnp.concatenate([zkv[...], kvm_ref[...]], axis=0), first, att_ref)
    last = zkv[t:t + WINDOW, :]
    lastkv_ref[...] = last
    zkv[0:WINDOW, :] = last
    entering = jnp.where(first, mst_ref[...], st_ref[...])
    hg = _Hgrn(zhq, zhk, zlogf, zhi, zhg, nw_ref[...], entering, hg_ref)
    proj = _Projection(xn_s, lbp_ref[...], w_ref)

    def gate_a(p):
        sga_ref[:, p] = proj.branch_gate(C_GA, p).astype(sga_ref.dtype)

    def gate_b(p):
        sgb_ref[:, p] = proj.branch_gate(C_GB, p).astype(sgb_ref.dtype)

    def new_kv(_):
        zkv[WINDOW:WINDOW + t, :] = proj.kv()

    def new_forget(p):
        k_new, logf_new = proj.forget(p)
        zhk[:, p] = k_new.astype(zhk.dtype)
        zlogf[:, p] = logf_new

    def new_q(p):
        zq[:, p] = proj.q_att(p).astype(zq.dtype)

    def new_hq(p):
        zhq[:, p] = proj.q_hg(p).astype(zhq.dtype)

    def new_hi(p):
        zhi[:, p] = proj.i_hg(p).astype(zhi.dtype)

    def new_gate(p):
        zhg[:, p] = proj.swish_gate(p).astype(zhg.dtype)

    def parts(n):
        return [slice(c, c + PROJ_GROUP) for c in range(0, n, PROJ_GROUP)]

    groups = ([(gate_a, p) for p in parts(D_MODEL)] + [(gate_b, p) for p in parts(D_MODEL)]
              + [(new_kv, None)] + [(new_q, p) for p in parts(D_ATT)]
              + [(new_forget, p) for p in parts(D_HG)] + [(new_hq, p) for p in parts(D_HG)]
              + [(new_hi, p) for p in parts(D_HG)] + [(new_gate, p) for p in parts(D_HG)])

    def save_state():
        st_ref[...] = hg.state
        for h, hs in enumerate(_HG_SLICES):
            sfin_ref[h] = hg.state[:, hs].T

    qk, cs, loc, fin, out = att.issue_scores, hg.issue_cumsum, hg.issue_local, att.finish, hg.finish
    mixer_work = {
        0: [(qk, 0), (cs, 0), (cs, 1), (cs, 2), (cs, 3)],
        1: [(cs, 4), (cs, 5), (loc, 0)],
        2: [(cs, 6), (cs, 7), (loc, 1)],
        3: [(qk, 1), (loc, 2)],
        4: [(fin, 0), (loc, 3)],
        5: [(out, 0), (loc, 4)],
        6: [(qk, 2), (out, 1), (loc, 5)],
        7: [(fin, 1), (out, 2), (loc, 6)],
        8: [(qk, 3), (out, 3), (loc, 7), (save_state,)],
        9: [(out, 4)],
        10: [(fin, 2), (out, 5)],
        11: [(out, 6)],
        12: [(out, 7)],
        13: [(fin, 3)],
    }
    for slot, (fn, p) in enumerate(groups):
        fn(p)
        for item in mixer_work.get(slot, []):
            item[0](*item[1:])
    xn_s[...] = _rmsnorm(xnext_ref[...], g_ref[...]).astype(xn_s.dtype)
    for src, dst in ((wa32, wa16), (wb32, wb16), (wo32, wo16), (wup32, wup16), (wdn32, wdn16)):
        dst[...] = src[...].astype(dst.dtype)


def _mixers(sinks, x, x_sample, x_meta, g, lb_param, w_f32, hg_norm, later_weights, bsz):
    n = x.shape[0]
    t = MIX_ROWS
    nt = n // t
    per_seq = nt // bsz
    rows = x_sample.shape[0] * LANES // D_MODEL + x_meta.shape[0]

    def this_tile(c):
        return pl.BlockSpec((t, c), lambda s: (jnp.minimum(s, nt - 1), 0))

    def prev_tile(c):
        return pl.BlockSpec((t, c), lambda s: (jnp.maximum(s - 1, 0), 0))

    def prev_seq(shape):
        return pl.BlockSpec((None,) + shape,
                            lambda s: (jnp.maximum(s - 1, 0) // per_seq,) + (0,) * len(shape))

    def small(c):
        return pl.BlockSpec((rows, c), lambda s: (0, 0))

    first_tile = pl.BlockSpec((t, D_MODEL), lambda s: (0, 0), pipeline_mode=pl.Buffered(1))
    next_tile = pl.BlockSpec((t, D_MODEL), lambda s: (jnp.minimum(s + 1, nt - 1), 0))
    small_widths = (2 * D_KV, D_HG, D_HG, D_HG, D_HG, D_MODEL, D_MODEL)
    small_dtypes = (F32, F32, F32, F32, F32, BF16, BF16)
    qm_shape = (rows, Q_HEADS, LANES)
    qm_spec = pl.BlockSpec(qm_shape, lambda s: (0, 0, 0))

    def row_block(wt):
        return pl.BlockSpec((wt.shape[0] // nt, wt.shape[1]), lambda s: (jnp.minimum(s, nt - 1), 0))

    return pl.pallas_call(
        functools.partial(_mixer_kernel, tiles_per_seq=per_seq),
        grid=(nt + 1,),
        in_specs=[pl.BlockSpec(memory_space=pltpu.SMEM), first_tile, next_tile,
                  _resident(x_sample.shape), _resident(x_meta.shape), _resident((1, D_MODEL)),
                  _resident(lb_param.shape), pl.BlockSpec(memory_space=pl.ANY), _resident((1, HG_DV))]
                 + [row_block(wt) for wt in later_weights],
        out_specs=[prev_tile(D_ATT), prev_tile(D_HG), this_tile(D_MODEL), this_tile(D_MODEL),
                   prev_seq((HG_HEADS, HG_DK, HG_DV)), prev_seq((WINDOW, 2 * D_KV))]
                  + [qm_spec] + [small(c) for c in small_widths] + [row_block(wt) for wt in later_weights],
        out_shape=[jax.ShapeDtypeStruct((n, D_ATT), BF16), jax.ShapeDtypeStruct((n, D_HG), BF16),
                   jax.ShapeDtypeStruct((n, D_MODEL), BF16), jax.ShapeDtypeStruct((n, D_MODEL), BF16),
                   jax.ShapeDtypeStruct((bsz, HG_HEADS, HG_DK, HG_DV), F32),
                   jax.ShapeDtypeStruct((bsz, WINDOW, 2 * D_KV), F32)]
                  + [jax.ShapeDtypeStruct(qm_shape, F32)]
                  + [jax.ShapeDtypeStruct((rows, c), d) for c, d in zip(small_widths, small_dtypes)]
                  + [jax.ShapeDtypeStruct(wt.shape, BF16) for wt in later_weights],
        scratch_shapes=[pltpu.VMEM((D_MODEL, D_IN), BF16),
                        pltpu.VMEM((WEIGHT_SLOTS, WEIGHT_SLAB_ROWS, D_IN), F32),
                        pltpu.SemaphoreType.DMA((WEIGHT_SLOTS,)),
                        pltpu.VMEM((t, D_MODEL), BF16),
                        pltpu.VMEM((t, D_ATT), BF16), pltpu.VMEM((WINDOW + t, 2 * D_KV), F32),
                        pltpu.VMEM((t, D_HG), BF16), pltpu.VMEM((t, D_HG), BF16),
                        pltpu.VMEM((t, D_HG), F32), pltpu.VMEM((t, D_HG), BF16),
                        pltpu.VMEM((t, D_HG), BF16),
                        pltpu.VMEM((N_META, 2 * D_KV), F32),
                        pltpu.VMEM((HG_DV, D_HG), F32),
                        pltpu.VMEM((HG_DV, D_HG), F32)],
        compiler_params=_params("arbitrary"),
        name="mixers",
    )(sinks, x, x, x_sample, x_meta, g, lb_param, w_f32, hg_norm, *later_weights)


def _merge_ffn_rows(x, att, hg, sga, sgb, wa_ref, wb_ref, wo_ref, ln_ffn, wup_ref, wdn_ref, ln_f,
                    before_ffn=None):
    ya = _dot(att.astype(BF16), wa_ref[...])
    yb = _dot(hg.astype(BF16), wb_ref[...])
    mix = sga.astype(F32) * ya + sgb.astype(F32) * yb
    h1 = x + _dot(mix.astype(BF16), wo_ref[...])
    xn = _rmsnorm(h1, ln_ffn).astype(BF16)
    if before_ffn is not None:
        before_ffn()
    acc = jnp.zeros_like(h1)
    for c in range(0, D_FF, FFN_CHUNK):
        u = jnp.maximum(_dot(xn, wup_ref[:, c:c + FFN_CHUNK]), 0.0)
        acc = acc + _dot((u * u).astype(BF16), wdn_ref[c:c + FFN_CHUNK, :])
    return _rmsnorm(h1 + acc, ln_f)


def _sample_hgrn_group(rows, nw, q_ref, k_ref, i_ref, g_ref, s_ref, snew_ref, hg_ref):
    k = k_ref[rows, :]
    q = q_ref[rows, :]
    iv = i_ref[rows, :]
    r = lax.broadcasted_iota(jnp.int32, (HG_DK, HG_DK), 0)
    c = lax.broadcasted_iota(jnp.int32, (HG_DK, HG_DK), 1)
    eye = r == c

    def column(row):
        return jnp.sum(jnp.where(eye, row, 0.0), axis=1, keepdims=True)

    outs = []
    for b in range(s_ref.shape[0]):
        heads = []
        for h, hs in enumerate(_HG_SLICES):
            kc = column(k[b:b + 1, hs])
            qc = column(q[b:b + 1, hs])
            s_old = s_ref[b, h]
            s = s_old + kc * (iv[b:b + 1, hs] - s_old)
            snew_ref[b, h] = s
            heads.append(jnp.sum(qc * s, axis=0, keepdims=True))
        outs.append(jnp.concatenate(heads, axis=1))
    hg_ref[rows, :] = _hg_out(jnp.concatenate(outs, axis=0), g_ref[rows, :], nw)


def _merge_ffn_kernel(x_ref, att_ref, hg_ref, sga_ref, sgb_ref,
                      xs_ref, atts_ref, sgas_ref, sgbs_ref,
                      nw_ref, hqs_ref, hks_ref, his_ref, hgates_ref, state_ref,
                      kvs_ref, ckt_ref, cvt_ref,
                      wa_ref, wb_ref, wo_ref, wup_hbm, wdn_hbm, lnffn_ref, lnf_ref,
                      y_ref, ys_ref, snew_ref, nkt_ref, nvt_ref, hgs, wup_ref, wdn_ref, wsem,
                      *, prompt_steps):
    i = pl.program_id(0)
    weights = (wa_ref, wb_ref, wo_ref, lnffn_ref[...], wup_ref, wdn_ref, lnf_ref[...])
    group = state_ref.shape[0]
    ngroups = atts_ref.shape[0] // group
    ffn_copies = (pltpu.make_async_copy(wup_hbm, wup_ref, wsem.at[0]),
                  pltpu.make_async_copy(wdn_hbm, wdn_ref, wsem.at[1]))

    @pl.when(i == 0)
    def _():
        for copy in ffn_copies:
            copy.start()

    def ffn_weights_in():
        @pl.when(i == 0)
        def _():
            for copy in ffn_copies:
                copy.wait()

    @pl.when(i < prompt_steps)
    def _():
        y_ref[...] = _merge_ffn_rows(x_ref[...], att_ref[...], hg_ref[...], sga_ref[...], sgb_ref[...],
                                     *weights, before_ffn=ffn_weights_in)
        rows = pl.ds(pl.multiple_of(jnp.minimum(i, ngroups - 1) * group, group), group)
        _sample_hgrn_group(rows, nw_ref[...], hqs_ref, hks_ref, his_ref, hgates_ref,
                           state_ref, snew_ref, hgs)
        kvn = kvs_ref[rows, :]
        r = lax.broadcasted_iota(jnp.int32, (D_KV, WINDOW), 0)
        c = lax.broadcasted_iota(jnp.int32, (D_KV, WINDOW), 1)
        for b in range(group):
            for src, dst, lo in ((ckt_ref, nkt_ref, 0), (cvt_ref, nvt_ref, D_KV)):
                col = jnp.sum(jnp.where(r == c, kvn[b:b + 1, lo:lo + D_KV], 0.0), axis=1, keepdims=True)
                dst[b] = jnp.where(c == WINDOW - 1, col, pltpu.roll(src[b], WINDOW - 1, axis=1))

    @pl.when(i == prompt_steps)
    def _():
        _rows_to_linear(ys_ref, _merge_ffn_rows(_rows_from_linear(xs_ref), atts_ref[...], hgs[...],
                                                sgas_ref[...], sgbs_ref[...], *weights))


def _merge_ffn(x, att, hg, sga, sgb, xs, att_s, sga_s, sgb_s, hg_norm, hq_s, hk_s, hi_s, hgate_s, state,
               kv_s, ck_t, cv_t, wa, wb, wo, ln_ffn, w_up, w_down, ln_f):
    n = x.shape[0]
    nb = att_s.shape[0]
    linear = pl.BlockSpec(xs.shape, lambda i: (0, 0))
    rows = MERGE_ROWS
    nt = n // rows
    g = SAMPLE_HG_GROUP
    assert nb // g <= nt

    def blk(c):
        return pl.BlockSpec((rows, c), lambda i: (jnp.minimum(i, nt - 1), 0))

    def sample(c):
        return pl.BlockSpec((nb, c), lambda i: (0, 0))

    sblk = pl.BlockSpec((g, HG_HEADS, HG_DK, HG_DV), lambda i: (jnp.minimum(i, nb // g - 1), 0, 0, 0))
    cblk = pl.BlockSpec((g, D_KV, WINDOW), lambda i: (jnp.minimum(i, nb // g - 1), 0, 0))
    return pl.pallas_call(
        functools.partial(_merge_ffn_kernel, prompt_steps=nt),
        grid=(nt + 1,),
        in_specs=[blk(D_MODEL), blk(D_ATT), blk(D_HG), blk(D_MODEL), blk(D_MODEL),
                  linear, sample(D_ATT), sample(D_MODEL), sample(D_MODEL),
                  _resident((1, HG_DV)), _resident(hq_s.shape), _resident(hk_s.shape),
                  _resident(hi_s.shape), _resident(hgate_s.shape), sblk,
                  _resident(kv_s.shape), cblk, cblk,
                  _resident(wa.shape), _resident(wb.shape), _resident(wo.shape),
                  pl.BlockSpec(memory_space=pl.ANY), pl.BlockSpec(memory_space=pl.ANY),
                  _resident((1, D_MODEL)), _resident((1, D_MODEL))],
        out_specs=[blk(D_MODEL), linear, sblk, cblk, cblk],
        out_shape=[jax.ShapeDtypeStruct((n, D_MODEL), F32), jax.ShapeDtypeStruct(xs.shape, F32),
                   jax.ShapeDtypeStruct(state.shape, F32),
                   jax.ShapeDtypeStruct(ck_t.shape, F32), jax.ShapeDtypeStruct(cv_t.shape, F32)],
        scratch_shapes=[pltpu.VMEM((nb, D_HG), F32),
                        pltpu.VMEM(w_up.shape, w_up.dtype), pltpu.VMEM(w_down.shape, w_down.dtype),
                        pltpu.SemaphoreType.DMA((2,))],
        compiler_params=_params("arbitrary"),
        name="merge_ffn",
    )(x, att, hg, sga, sgb, xs, att_s, sga_s, sgb_s, hg_norm, hq_s, hk_s, hi_s, hgate_s, state,
      kv_s, ck_t, cv_t, wa, wb, wo, w_up, w_down, ln_ffn, ln_f)


def _sample_attn_kernel(sink_ref, qm_ref, kvn_ref, ckt_ref, cvt_ref, mkt_ref, mvt_ref,
                        att_ref, o_all, mk_s, mv_s):
    nb = qm_ref.shape[0]
    first_seq = pl.multiple_of(pl.program_id(0) * nb, nb)

    @pl.when(pl.program_id(0) == 0)
    def _():
        for r in range(N_META):
            mk_s[:, r, :] = mkt_ref[r].T
            mv_s[:, r, :] = mvt_ref[r].T
    head = lax.broadcasted_iota(jnp.int32, (Q_HEADS, 1), 0)
    sk = jnp.zeros((Q_HEADS, 1), F32)
    for j in range(Q_HEADS):
        sk = jnp.where(head == j, sink_ref[j], sk)
    oldest = lax.broadcasted_iota(jnp.int32, (Q_HEADS, WINDOW), 1) == 0
    kvn = kvn_ref[...]
    qms = [qm_ref[b] for b in range(nb)]
    scores = [(_dot(qms[b].astype(BF16), ckt_ref[b].astype(BF16)),
               _dot_nt(qms[b].astype(BF16), mk_s[first_seq + b].astype(BF16)))
              for b in range(nb)]
    for b, (s_w, s_m) in enumerate(scores):
        s_w = jnp.where(oldest, -jnp.inf, s_w)
        s_n = jnp.sum(qms[b] * kvn[b:b + 1, :D_KV], axis=1, keepdims=True)
        m = jnp.maximum(jnp.maximum(jnp.max(s_w, axis=1, keepdims=True),
                                    jnp.max(s_m, axis=1, keepdims=True)), jnp.maximum(s_n, sk))
        e_w = jnp.exp(s_w - m)
        e_m = jnp.exp(s_m - m)
        e_n = jnp.exp(s_n - m)
        l = (jnp.sum(e_w, axis=1, keepdims=True) + jnp.sum(e_m, axis=1, keepdims=True) + e_n
             + jnp.exp(sk - m))
        o = (_dot_nt(e_w.astype(BF16), cvt_ref[b].astype(BF16))
             + _dot(e_m.astype(BF16), mv_s[first_seq + b].astype(BF16))
             + e_n * kvn[b:b + 1, D_KV:])
        o_all[b] = o / l
    low = lax.broadcasted_iota(jnp.int32, (1, LANES), 1) < HEAD_DIM
    pairs = []
    for p in range(Q_HEADS // 2):
        halves = []
        for half in range(2):
            j = 2 * p + half
            kv_head = j // (Q_HEADS // KV_HEADS)
            oj = jnp.where(low if kv_head == 0 else jnp.logical_not(low), o_all[:, j, :], 0.0)
            halves.append(oj if half == kv_head else pltpu.roll(oj, HEAD_DIM, axis=1))
        pairs.append(halves[0] + halves[1])
    att_ref[...] = jnp.concatenate(pairs, axis=1)


def _sample_attention(sinks, qm, kv_new, ck, cv, mk, mv):
    nb = ck.shape[0]
    g = SAMPLE_ATT_GROUP

    def blk3(a, c):
        return pl.BlockSpec((g, a, c), lambda i: (i, 0, 0))

    return pl.pallas_call(
        _sample_attn_kernel,
        grid=(nb // g,),
        in_specs=[pl.BlockSpec(memory_space=pltpu.SMEM), blk3(Q_HEADS, D_KV),
                  pl.BlockSpec((g, 2 * D_KV), lambda i: (i, 0)),
                  blk3(D_KV, WINDOW), blk3(D_KV, WINDOW), _resident(mk.shape), _resident(mv.shape)],
        out_specs=pl.BlockSpec((g, D_ATT), lambda i: (i, 0)),
        out_shape=jax.ShapeDtypeStruct((nb, D_ATT), F32),
        scratch_shapes=[pltpu.VMEM((g, Q_HEADS, LANES), F32),
                        pltpu.VMEM((nb, N_META, D_KV), F32), pltpu.VMEM((nb, N_META, D_KV), F32)],
        compiler_params=_params("arbitrary"),
        name="sample_attn",
    )(sinks, qm, kv_new, ck, cv, mk, mv)


def kernel(x_prompt, x_sample, cache_k, cache_v, cache_meta_k, cache_meta_v, state_hgrn, meta,
           w_in, sinks, lb_param, hg_norm, w_att_out, w_hg_out, w_o, ln_mix, ln_ffn, w_up,
           w_down, ln_f):
    bsz, seq, _ = x_prompt.shape
    nb = x_sample.shape[0]
    ln_mix2 = ln_mix.reshape(1, D_MODEL)
    ln_ffn2 = ln_ffn.reshape(1, D_MODEL)
    ln_f2 = ln_f.reshape(1, D_MODEL)
    nw = hg_norm.reshape(1, HG_DV)

    xs = x_sample.reshape(nb * D_MODEL // LANES, LANES)

    xp = x_prompt.reshape(bsz * seq, D_MODEL)
    (att_p, hg_p, sga_p, sgb_p, state_p, lastkv_p,
     qm_s, kv_s, hq_s, hk_s, hi_s, hgate_s, sga_s, sgb_s,
     wa, wb, wo, wup, wdn) = _mixers(
        sinks.reshape(Q_HEADS), xp, xs, meta, ln_mix2, lb_param, w_in[0], nw,
        (w_att_out[0], w_hg_out[0], w_o[0], w_up[0], w_down[0]), bsz)

    def window_t(c):
        return jnp.swapaxes(c[0].reshape(nb, WINDOW, D_KV), 1, 2)

    def meta_t(c):
        return jnp.transpose(c[0].reshape(nb, N_META, D_KV), (1, 2, 0))

    ck_t, cv_t = window_t(cache_k), window_t(cache_v)
    att_s = _sample_attention(sinks.reshape(Q_HEADS), qm_s, kv_s, ck_t, cv_t,
                              meta_t(cache_meta_k), meta_t(cache_meta_v))

    y_p, y_s, state_s, nkt_s, nvt_s = _merge_ffn(
        xp, att_p, hg_p, sga_p, sgb_p, xs, att_s, sga_s, sgb_s,
        nw, hq_s, hk_s, hi_s, hgate_s, state_hgrn[0], kv_s, ck_t, cv_t,
        wa, wb, wo, ln_ffn2, wup, wdn, ln_f2)

    kv5 = lastkv_p.reshape(bsz, WINDOW, 2, KV_HEADS, HEAD_DIM)
    meta5 = jnp.broadcast_to(kv_s[nb:].reshape(1, N_META, 2, KV_HEADS, HEAD_DIM),
                             (bsz, N_META, 2, KV_HEADS, HEAD_DIM))
    return (y_p.reshape(bsz, seq, D_MODEL),
            y_s.reshape(nb, 1, D_MODEL),
            kv5[None, :, :, 0],
            kv5[None, :, :, 1],
            meta5[None, :, :, 0],
            meta5[None, :, :, 1],
            state_p[None],
            jnp.swapaxes(nkt_s, 1, 2).reshape(1, nb, WINDOW, KV_HEADS, HEAD_DIM),
            jnp.swapaxes(nvt_s, 1, 2).reshape(1, nb, WINDOW, KV_HEADS, HEAD_DIM),
            state_s[None])
```

```python
import functools

import jax
import jax.numpy as jnp
from jax import lax
from jax.experimental import pallas as pl
from jax.experimental.pallas import tpu as pltpu

F32 = jnp.float32
BF16 = jnp.bfloat16

D_MODEL = 1024
N_META = 16
WINDOW = 128
HEAD_DIM = 64
Q_HEADS = 8
KV_HEADS = 2
D_ATT = Q_HEADS * HEAD_DIM
D_KV = KV_HEADS * HEAD_DIM
HG_HEADS = 4
HG_DK = 128
HG_DV = 128
D_HG = HG_HEADS * HG_DK
HG_CHUNK = 64
D_FF = 4 * D_MODEL
EPS = 1e-6
C_Q = 0
C_KV = C_Q + D_ATT
C_HQ = C_KV + 2 * D_KV
C_HF = C_HQ + D_HG
C_HI = C_HF + D_HG
C_HGATE = C_HI + D_HG
C_GA = C_HGATE + D_HG
C_GB = C_GA + D_MODEL
D_IN = C_GB + D_MODEL

VMEM_LIMIT_BYTES = 56 * 1024 * 1024
MIX_ROWS = 512
PROJ_GROUP = 256
WEIGHT_SLAB_ROWS = 64
WEIGHT_SLOTS = 4
MERGE_ROWS = 512
FFN_CHUNK = 2048
LANES = 128
BF16_ROWS = 16
SAMPLE_ATT_GROUP = 32
SAMPLE_HG_GROUP = 8

_NT = (((1,), (1,)), ((), ()))
_TN = (((0,), (0,)), ((), ()))


def _dot(a, b):
    return jnp.dot(a, b, preferred_element_type=F32)


def _dot_nt(a, b):
    return lax.dot_general(a, b, _NT, preferred_element_type=F32)


def _dot_tn(a, b):
    return lax.dot_general(a, b, _TN, preferred_element_type=F32)


def _rows_from_linear(ref):
    per = D_MODEL // LANES
    rows = ref.shape[0] // per
    return jnp.concatenate([ref[pl.ds(c, rows, stride=per), :] for c in range(per)], axis=1)


def _rows_to_linear(ref, value):
    per = D_MODEL // LANES
    for c in range(per):
        ref[pl.ds(c, value.shape[0], stride=per), :] = value[:, c * LANES:(c + 1) * LANES]


def _rmsnorm(x, g):
    return x * lax.rsqrt(jnp.mean(x * x, axis=-1, keepdims=True) + EPS) * g


def _resident(shape):
    return pl.BlockSpec(shape, lambda *_: (0,) * len(shape), pipeline_mode=pl.Buffered(1))


def _params(*sem):
    return pltpu.CompilerParams(dimension_semantics=sem, vmem_limit_bytes=VMEM_LIMIT_BYTES)


def _lower_bound(lbp):
    e = jnp.exp(lbp - jnp.max(lbp, axis=0, keepdims=True))
    return e[0:1] / jnp.sum(e, axis=0, keepdims=True)


class _Projection:
    def __init__(self, xn, lbp, w_ref):
        self.xn = xn
        self.w_ref = w_ref
        self.lb = _lower_bound(lbp)

    def cols(self, base, part):
        return _dot(self.xn[...], self.w_ref[:, base + part.start:base + part.stop])

    def q_att(self, part=slice(0, D_ATT)):
        return self.cols(C_Q, part) * (HEAD_DIM ** -0.5)

    def kv(self):
        return self.cols(C_KV, slice(0, 2 * D_KV))

    def q_hg(self, part=slice(0, D_HG)):
        return self.cols(C_HQ, part) * (HG_DK ** -0.5)

    def forget(self, part=slice(0, D_HG)):
        lb = self.lb[:, part]
        f = lb + (1.0 - lb) * jax.nn.sigmoid(self.cols(C_HF, part))
        return 1.0 - f, jnp.log(f)

    def i_hg(self, part=slice(0, D_HG)):
        return self.cols(C_HI, part)

    def swish_gate(self, part=slice(0, D_HG)):
        g = self.cols(C_HGATE, part)
        return g * jax.nn.sigmoid(g)

    def branch_gate(self, base, part=slice(0, D_MODEL)):
        return jax.nn.sigmoid(self.cols(base, part))


class _Attention:
    def __init__(self, sink_ref, q_ref, kv, first, o_ref):
        self.sink_ref, self.q_ref, self.first, self.o_ref = sink_ref, q_ref, first, o_ref
        self.nsub = q_ref.shape[0] // WINDOW
        self.nk = 2 * WINDOW + N_META
        self.meta0 = WINDOW + q_ref.shape[0]
        lane = lax.broadcasted_iota(jnp.int32, (kv.shape[0], D_KV), 1)
        low = lane < HEAD_DIM
        k = kv[:, :D_KV]
        ksw = pltpu.roll(k, HEAD_DIM, axis=1)
        self.kboth = (jnp.where(low, k, ksw).astype(BF16), jnp.where(low, ksw, k).astype(BF16))
        qlane = lax.broadcasted_iota(jnp.int32, (1, LANES), 1)
        self.keep = (jnp.where(qlane < HEAD_DIM, 1.0, 0.0).astype(BF16),
                     jnp.where(qlane < HEAD_DIM, 0.0, 1.0).astype(BF16))
        r = lax.broadcasted_iota(jnp.int32, (D_KV, D_KV), 0)
        c = lax.broadcasted_iota(jnp.int32, (D_KV, D_KV), 1)
        eye = jnp.where(r == c, 1.0, 0.0).astype(BF16)
        self.vt = _dot_nt(eye, kv[:, D_KV:].astype(BF16)).astype(BF16)
        self.ones = jnp.ones((BF16_ROWS, self.nk), BF16)
        self.key = lax.broadcasted_iota(jnp.int32, (self.nk, 2 * WINDOW), 0)
        col = lax.broadcasted_iota(jnp.int32, (self.nk, 2 * WINDOW), 1)
        self.qry = jnp.bitwise_and(col, WINDOW - 1)
        self.first_head = lax.broadcasted_iota(jnp.int32, (1, 2 * WINDOW), 1) < WINDOW
        self.scores = {}

    def issue_scores(self, i):
        r0 = i * WINDOW
        out = []
        for p in range(Q_HEADS // 2):
            kb = self.kboth[(2 * p) // (Q_HEADS // KV_HEADS)]
            kmat = jnp.concatenate([kb[r0:r0 + 2 * WINDOW], kb[self.meta0:]], axis=0)
            qp = self.q_ref[r0:r0 + WINDOW, p * LANES:(p + 1) * LANES]
            q2 = jnp.concatenate([qp * self.keep[0], qp * self.keep[1]], axis=0)
            out.append(_dot_nt(kmat, q2))
        self.scores[i] = out

    def finish(self, i):
        r0 = i * WINDOW
        lo = jnp.where(self.first, WINDOW - 1, self.qry) if i == 0 else self.qry
        mask = jnp.logical_and(self.key > lo, self.key <= self.qry + WINDOW)
        mask = jnp.logical_or(mask, self.key >= 2 * WINDOW)
        vaug = []
        for h in range(KV_HEADS):
            vth = self.vt[h * HEAD_DIM:(h + 1) * HEAD_DIM]
            vaug.append(jnp.concatenate(
                [jnp.concatenate([vth[:, r0:r0 + 2 * WINDOW], vth[:, self.meta0:]], axis=1),
                 self.ones], axis=0))
        outs = []
        for p, raw in enumerate(self.scores.pop(i)):
            h = (2 * p) // (Q_HEADS // KV_HEADS)
            s = jnp.where(mask, raw, -jnp.inf)
            sk = jnp.where(self.first_head, self.sink_ref[2 * p], self.sink_ref[2 * p + 1])
            m = jnp.maximum(jnp.max(s, axis=0, keepdims=True), sk)
            e = jnp.exp(s - m).astype(BF16)
            oa = _dot(vaug[h], e)
            o = oa[:HEAD_DIM] / (oa[HEAD_DIM:HEAD_DIM + 1] + jnp.exp(sk - m))
            outs += [o[:, :WINDOW], o[:, WINDOW:]]
        self.o_ref[r0:r0 + WINDOW, :] = jnp.concatenate(outs, axis=0).T.astype(self.o_ref.dtype)


def _cumsum_rows(x):
    t = x.shape[0]
    row = lax.broadcasted_iota(jnp.int32, x.shape, 0)
    d = 1
    while d < t:
        x = x + jnp.where(row >= d, pltpu.roll(x, d, axis=0), 0.0)
        d *= 2
    return x


def _hg_out(o, gate, nw):
    parts = []
    for h in range(HG_HEADS):
        oh = o[:, h * HG_DV:(h + 1) * HG_DV]
        parts.append(oh * lax.rsqrt(jnp.mean(oh * oh, axis=-1, keepdims=True) + EPS) * nw)
    return jnp.concatenate(parts, axis=1) * gate


_HG_SLICES = [slice(h * HG_DK, (h + 1) * HG_DK) for h in range(HG_HEADS)]


def _state_update(iv, kd):
    return jnp.concatenate([_dot_tn(iv[:, hs], kd[:, hs]) for hs in _HG_SLICES], axis=1)


def _meta_state(mk, mlogf, mi):
    b = _cumsum_rows(mlogf)
    kd = (mk * jnp.exp(b[-1:] - b)).astype(BF16)
    return _state_update(mi.astype(BF16), kd)


class _Hgrn:
    def __init__(self, q_ref, k_ref, logf_ref, i_ref, g_ref, nw, state, o_ref):
        self.q_ref, self.k_ref, self.logf_ref, self.i_ref, self.g_ref = q_ref, k_ref, logf_ref, i_ref, g_ref
        self.nw, self.state, self.o_ref = nw, state, o_ref
        t = HG_CHUNK
        self.nchunk = q_ref.shape[0] // t
        r = lax.broadcasted_iota(jnp.int32, (t, t), 0)
        c = lax.broadcasted_iota(jnp.int32, (t, t), 1)
        self.causal = r >= c
        self.cums, self.terms, self.entering = {}, {}, {}

    def rows(self, ci):
        return slice(ci * HG_CHUNK, (ci + 1) * HG_CHUNK)

    def issue_cumsum(self, ci):
        self.cums[ci] = _cumsum_rows(self.logf_ref[self.rows(ci), :])

    def issue_local(self, ci):
        rows = self.rows(ci)
        b = self.cums.pop(ci)
        bl = b[-1:]
        k = self.k_ref[rows, :].astype(F32)
        qe = (self.q_ref[rows, :].astype(F32) * jnp.exp(b)).astype(BF16)
        ke = (k * jnp.exp(-b)).astype(BF16)
        kd = (k * jnp.exp(bl - b)).astype(BF16)
        iv = self.i_ref[rows, :].astype(BF16)
        a = [jnp.where(self.causal, _dot_nt(qe[:, hs], ke[:, hs]), 0.0).astype(BF16) for hs in _HG_SLICES]
        self.terms[ci] = (qe, iv, a)
        self.entering[ci] = self.state.astype(BF16)
        self.state = self.state * jnp.exp(bl) + _state_update(iv, kd)

    def finish(self, ci):
        rows = self.rows(ci)
        qe, iv, a = self.terms.pop(ci)
        ent = self.entering.pop(ci)
        outs = [_dot(a[h], iv[:, hs]) + _dot_nt(qe[:, hs], ent[:, hs]) for h, hs in enumerate(_HG_SLICES)]
        o = _hg_out(jnp.concatenate(outs, axis=1), self.g_ref[rows, :].astype(F32), self.nw)
        self.o_ref[rows, :] = o.astype(self.o_ref.dtype)


def _load_as_bf16(w_hbm, w_vmem, stage, sem, meanwhile):
    nslot, rows = stage.shape[0], stage.shape[1]
    nslab = w_hbm.shape[0] // rows
    ahead = nslot - 1

    def slab_copy(c):
        return pltpu.make_async_copy(w_hbm.at[pl.ds(c * rows, rows), :], stage.at[c % nslot],
                                     sem.at[c % nslot])

    for c in range(min(ahead, nslab)):
        slab_copy(c).start(priority=c % 2)
    meanwhile()
    for c in range(nslab):
        if c + ahead < nslab:
            slab_copy(c + ahead).start(priority=(c + ahead) % 2)
        slab_copy(c).wait()
        w_vmem[pl.ds(c * rows, rows), :] = stage[c % nslot].astype(w_vmem.dtype)


def _mixer_kernel(sink_ref, x0_ref, xnext_ref, xs_ref, xm_ref, g_ref, lbp_ref, w_hbm, nw_ref,
                  wa32, wb32, wo32, wup32, wdn32,
                  att_ref, hg_ref, sga_ref, sgb_ref, sfin_ref, lastkv_ref,
                  qs_ref, kvs_ref, hqs_ref, hks_ref, his_ref, hgs_ref, sgas_ref, sgbs_ref,
                  wa16, wb16, wo16, wup16, wdn16,
                  w_ref, stage, sem, xn_s, zq, zkv, zhq, zhk, zlogf, zhi, zhg, kvm_ref, st_ref, mst_ref,
                  *, tiles_per_seq):
    s = pl.program_id(0)
    t = xnext_ref.shape[0]

    @pl.when(s == 0)
    def _():
        def first_tile_and_zeros():
            xn_s[...] = _rmsnorm(x0_ref[...], g_ref[...]).astype(xn_s.dtype)
            for ref in (zq, zkv, zhq, zhk, zlogf, zhi, zhg, st_ref):
                ref[...] = jnp.zeros(ref.shape, ref.dtype)

        _load_as_bf16(w_hbm, w_ref, stage, sem, first_tile_and_zeros)
        xs = _rows_from_linear(xs_ref)
        nb = xs.shape[0]
        small = jnp.concatenate([xs, xm_ref[...]], axis=0)
        p = _Projection(_rmsnorm(small, g_ref[...]).astype(BF16), lbp_ref[...], w_ref)
        q_small = p.q_att()
        low = lax.broadcasted_iota(jnp.int32, (1, LANES), 1) < HEAD_DIM
        for j in range(Q_HEADS):
            kv_head, half = j // (Q_HEADS // KV_HEADS), j % 2
            pair = q_small[:, (j // 2) * LANES:(j // 2 + 1) * LANES]
            moved = pair if half == kv_head else pltpu.roll(pair, HEAD_DIM, axis=1)
            qs_ref[:, j, :] = jnp.where(low if kv_head == 0 else jnp.logical_not(low), moved, 0.0)
        kv = p.kv()
        kvs_ref[...] = kv
        kvm_ref[...] = kv[nb:]
        hqs_ref[...] = p.q_hg()
        k, logf = p.forget()
        hks_ref[...] = k
        iv = p.i_hg()
        his_ref[...] = iv
        hgs_ref[...] = p.swish_gate()
        sgas_ref[...] = p.branch_gate(C_GA).astype(sgas_ref.dtype)
        sgbs_ref[...] = p.branch_gate(C_GB).astype(sgbs_ref.dtype)
        mst_ref[...] = _meta_state(k[nb:], logf[nb:], iv[nb:])

    first = lax.rem(jnp.maximum(s - 1, 0), tiles_per_seq) == 0

    att = _Attention(sink_ref, zq, jnp.concatenate([zkv[...], kvm_ref[...]], axis=0), first, att_ref)
    last = zkv[t:t + WINDOW, :]
    lastkv_ref[...] = last
    zkv[0:WINDOW, :] = last
    entering = jnp.where(first, mst_ref[...], st_ref[...])
    hg = _Hgrn(zhq, zhk, zlogf, zhi, zhg, nw_ref[...], entering, hg_ref)
    proj = _Projection(xn_s, lbp_ref[...], w_ref)

    def gate_a(p):
        sga_ref[:, p] = proj.branch_gate(C_GA, p).astype(sga_ref.dtype)

    def gate_b(p):
        sgb_ref[:, p] = proj.branch_gate(C_GB, p).astype(sgb_ref.dtype)

    def new_kv(_):
        zkv[WINDOW:WINDOW + t, :] = proj.kv()

    def new_forget(p):
        k_new, logf_new = proj.forget(p)
        zhk[:, p] = k_new.astype(zhk.dtype)
        zlogf[:, p] = logf_new

    def new_q(p):
        zq[:, p] = proj.q_att(p).astype(zq.dtype)

    def new_hq(p):
        zhq[:, p] = proj.q_hg(p).astype(zhq.dtype)

    def new_hi(p):
        zhi[:, p] = proj.i_hg(p).astype(zhi.dtype)

    def new_gate(p):
        zhg[:, p] = proj.swish_gate(p).astype(zhg.dtype)

    def parts(n):
        return [slice(c, c + PROJ_GROUP) for c in range(0, n, PROJ_GROUP)]

    groups = ([(gate_a, p) for p in parts(D_MODEL)] + [(gate_b, p) for p in parts(D_MODEL)]
              + [(new_kv, None)] + [(new_q, p) for p in parts(D_ATT)]
              + [(new_forget, p) for p in parts(D_HG)] + [(new_hq, p) for p in parts(D_HG)]
              + [(new_hi, p) for p in parts(D_HG)] + [(new_gate, p) for p in parts(D_HG)])

    def save_state():
        st_ref[...] = hg.state
        for h, hs in enumerate(_HG_SLICES):
            sfin_ref[h] = hg.state[:, hs].T

    qk, cs, loc, fin, out = att.issue_scores, hg.issue_cumsum, hg.issue_local, att.finish, hg.finish
    mixer_work = {
        0: [(qk, 0), (cs, 0), (cs, 1), (cs, 2), (cs, 3)],
        1: [(cs, 4), (cs, 5), (loc, 0)],
        2: [(cs, 6), (cs, 7), (loc, 1)],
        3: [(qk, 1), (loc, 2)],
        4: [(fin, 0), (loc, 3)],
        5: [(out, 0), (loc, 4)],
        6: [(qk, 2), (out, 1), (loc, 5)],
        7: [(fin, 1), (out, 2), (loc, 6)],
        8: [(qk, 3), (out, 3), (loc, 7), (save_state,)],
        9: [(out, 4)],
        10: [(fin, 2), (out, 5)],
        11: [(out, 6)],
        12: [(out, 7)],
        13: [(fin, 3)],
    }
    for slot, (fn, p) in enumerate(groups):
        fn(p)
        for item in mixer_work.get(slot, []):
            item[0](*item[1:])
    xn_s[...] = _rmsnorm(xnext_ref[...], g_ref[...]).astype(xn_s.dtype)
    for src, dst in ((wa32, wa16), (wb32, wb16), (wo32, wo16), (wup32, wup16), (wdn32, wdn16)):
        dst[...] = src[...].astype(dst.dtype)


def _mixers(sinks, x, x_sample, x_meta, g, lb_param, w_f32, hg_norm, later_weights, bsz):
    n = x.shape[0]
    t = MIX_ROWS
    nt = n // t
    per_seq = nt // bsz
    rows = x_sample.shape[0] * LANES // D_MODEL + x_meta.shape[0]

    def this_tile(c):
        return pl.BlockSpec((t, c), lambda s: (jnp.minimum(s, nt - 1), 0))

    def prev_tile(c):
        return pl.BlockSpec((t, c), lambda s: (jnp.maximum(s - 1, 0), 0))

    def prev_seq(shape):
        return pl.BlockSpec((None,) + shape,
                            lambda s: (jnp.maximum(s - 1, 0) // per_seq,) + (0,) * len(shape))

    def small(c):
        return pl.BlockSpec((rows, c), lambda s: (0, 0))

    first_tile = pl.BlockSpec((t, D_MODEL), lambda s: (0, 0), pipeline_mode=pl.Buffered(1))
    next_tile = pl.BlockSpec((t, D_MODEL), lambda s: (jnp.minimum(s + 1, nt - 1), 0))
    small_widths = (2 * D_KV, D_HG, D_HG, D_HG, D_HG, D_MODEL, D_MODEL)
    small_dtypes = (F32, F32, F32, F32, F32, BF16, BF16)
    qm_shape = (rows, Q_HEADS, LANES)
    qm_spec = pl.BlockSpec(qm_shape, lambda s: (0, 0, 0))

    def row_block(wt):
        return pl.BlockSpec((wt.shape[0] // nt, wt.shape[1]), lambda s: (jnp.minimum(s, nt - 1), 0))

    return pl.pallas_call(
        functools.partial(_mixer_kernel, tiles_per_seq=per_seq),
        grid=(nt + 1,),
        in_specs=[pl.BlockSpec(memory_space=pltpu.SMEM), first_tile, next_tile,
                  _resident(x_sample.shape), _resident(x_meta.shape), _resident((1, D_MODEL)),
                  _resident(lb_param.shape), pl.BlockSpec(memory_space=pl.ANY), _resident((1, HG_DV))]
                 + [row_block(wt) for wt in later_weights],
        out_specs=[prev_tile(D_ATT), prev_tile(D_HG), this_tile(D_MODEL), this_tile(D_MODEL),
                   prev_seq((HG_HEADS, HG_DK, HG_DV)), prev_seq((WINDOW, 2 * D_KV))]
                  + [qm_spec] + [small(c) for c in small_widths] + [row_block(wt) for wt in later_weights],
        out_shape=[jax.ShapeDtypeStruct((n, D_ATT), BF16), jax.ShapeDtypeStruct((n, D_HG), BF16),
                   jax.ShapeDtypeStruct((n, D_MODEL), BF16), jax.ShapeDtypeStruct((n, D_MODEL), BF16),
                   jax.ShapeDtypeStruct((bsz, HG_HEADS, HG_DK, HG_DV), F32),
                   jax.ShapeDtypeStruct((bsz, WINDOW, 2 * D_KV), F32)]
                  + [jax.ShapeDtypeStruct(qm_shape, F32)]
                  + [jax.ShapeDtypeStruct((rows, c), d) for c, d in zip(small_widths, small_dtypes)]
                  + [jax.ShapeDtypeStruct(wt.shape, BF16) for wt in later_weights],
        scratch_shapes=[pltpu.VMEM((D_MODEL, D_IN), BF16),
                        pltpu.VMEM((WEIGHT_SLOTS, WEIGHT_SLAB_ROWS, D_IN), F32),
                        pltpu.SemaphoreType.DMA((WEIGHT_SLOTS,)),
                        pltpu.VMEM((t, D_MODEL), BF16),
                        pltpu.VMEM((t, D_ATT), BF16), pltpu.VMEM((WINDOW + t, 2 * D_KV), F32),
                        pltpu.VMEM((t, D_HG), BF16), pltpu.VMEM((t, D_HG), BF16),
                        pltpu.VMEM((t, D_HG), F32), pltpu.VMEM((t, D_HG), BF16),
                        pltpu.VMEM((t, D_HG), BF16),
                        pltpu.VMEM((N_META, 2 * D_KV), F32),
                        pltpu.VMEM((HG_DV, D_HG), F32),
                        pltpu.VMEM((HG_DV, D_HG), F32)],
        compiler_params=_params("arbitrary"),
        name="mixers",
    )(sinks, x, x, x_sample, x_meta, g, lb_param, w_f32, hg_norm, *later_weights)


def _merge_ffn_rows(x, att, hg, sga, sgb, wa_ref, wb_ref, wo_ref, ln_ffn, wup_ref, wdn_ref, ln_f):
    ya = _dot(att.astype(BF16), wa_ref[...])
    yb = _dot(hg.astype(BF16), wb_ref[...])
    mix = sga.astype(F32) * ya + sgb.astype(F32) * yb
    h1 = x + _dot(mix.astype(BF16), wo_ref[...])
    xn = _rmsnorm(h1, ln_ffn).astype(BF16)
    acc = jnp.zeros_like(h1)
    for c in range(0, D_FF, FFN_CHUNK):
        u = jnp.maximum(_dot(xn, wup_ref[:, c:c + FFN_CHUNK]), 0.0)
        acc = acc + _dot((u * u).astype(BF16), wdn_ref[c:c + FFN_CHUNK, :])
    return _rmsnorm(h1 + acc, ln_f)


def _sample_hgrn_group(rows, nw, q_ref, k_ref, i_ref, g_ref, s_ref, snew_ref, hg_ref):
    k = k_ref[rows, :]
    q = q_ref[rows, :]
    iv = i_ref[rows, :]
    r = lax.broadcasted_iota(jnp.int32, (HG_DK, HG_DK), 0)
    c = lax.broadcasted_iota(jnp.int32, (HG_DK, HG_DK), 1)
    eye = r == c

    def column(row):
        return jnp.sum(jnp.where(eye, row, 0.0), axis=1, keepdims=True)

    outs = []
    for b in range(s_ref.shape[0]):
        heads = []
        for h, hs in enumerate(_HG_SLICES):
            kc = column(k[b:b + 1, hs])
            qc = column(q[b:b + 1, hs])
            s_old = s_ref[b, h]
            s = s_old + kc * (iv[b:b + 1, hs] - s_old)
            snew_ref[b, h] = s
            heads.append(jnp.sum(qc * s, axis=0, keepdims=True))
        outs.append(jnp.concatenate(heads, axis=1))
    hg_ref[rows, :] = _hg_out(jnp.concatenate(outs, axis=0), g_ref[rows, :], nw)


def _merge_ffn_kernel(x_ref, att_ref, hg_ref, sga_ref, sgb_ref,
                      xs_ref, atts_ref, sgas_ref, sgbs_ref,
                      nw_ref, hqs_ref, hks_ref, his_ref, hgates_ref, state_ref,
                      kvs_ref, ckt_ref, cvt_ref,
                      wa_ref, wb_ref, wo_ref, wup_ref, wdn_ref, lnffn_ref, lnf_ref,
                      y_ref, ys_ref, snew_ref, nkt_ref, nvt_ref, hgs, *, prompt_steps):
    i = pl.program_id(0)
    weights = (wa_ref, wb_ref, wo_ref, lnffn_ref[...], wup_ref, wdn_ref, lnf_ref[...])
    group = state_ref.shape[0]
    ngroups = atts_ref.shape[0] // group

    @pl.when(i < prompt_steps)
    def _():
        y_ref[...] = _merge_ffn_rows(x_ref[...], att_ref[...], hg_ref[...], sga_ref[...], sgb_ref[...],
                                     *weights)
        rows = pl.ds(pl.multiple_of(jnp.minimum(i, ngroups - 1) * group, group), group)
        _sample_hgrn_group(rows, nw_ref[...], hqs_ref, hks_ref, his_ref, hgates_ref,
                           state_ref, snew_ref, hgs)
        kvn = kvs_ref[rows, :]
        r = lax.broadcasted_iota(jnp.int32, (D_KV, WINDOW), 0)
        c = lax.broadcasted_iota(jnp.int32, (D_KV, WINDOW), 1)
        for b in range(group):
            for src, dst, lo in ((ckt_ref, nkt_ref, 0), (cvt_ref, nvt_ref, D_KV)):
                col = jnp.sum(jnp.where(r == c, kvn[b:b + 1, lo:lo + D_KV], 0.0), axis=1, keepdims=True)
                dst[b] = jnp.where(c == WINDOW - 1, col, pltpu.roll(src[b], WINDOW - 1, axis=1))

    @pl.when(i == prompt_steps)
    def _():
        _rows_to_linear(ys_ref, _merge_ffn_rows(_rows_from_linear(xs_ref), atts_ref[...], hgs[...],
                                                sgas_ref[...], sgbs_ref[...], *weights))


def _merge_ffn(x, att, hg, sga, sgb, xs, att_s, sga_s, sgb_s, hg_norm, hq_s, hk_s, hi_s, hgate_s, state,
               kv_s, ck_t, cv_t, wa, wb, wo, ln_ffn, w_up, w_down, ln_f):
    n = x.shape[0]
    nb = att_s.shape[0]
    linear = pl.BlockSpec(xs.shape, lambda i: (0, 0))
    rows = MERGE_ROWS
    nt = n // rows
    g = SAMPLE_HG_GROUP
    assert nb // g <= nt

    def blk(c):
        return pl.BlockSpec((rows, c), lambda i: (jnp.minimum(i, nt - 1), 0))

    def sample(c):
        return pl.BlockSpec((nb, c), lambda i: (0, 0))

    sblk = pl.BlockSpec((g, HG_HEADS, HG_DK, HG_DV), lambda i: (jnp.minimum(i, nb // g - 1), 0, 0, 0))
    cblk = pl.BlockSpec((g, D_KV, WINDOW), lambda i: (jnp.minimum(i, nb // g - 1), 0, 0))
    return pl.pallas_call(
        functools.partial(_merge_ffn_kernel, prompt_steps=nt),
        grid=(nt + 1,),
        in_specs=[blk(D_MODEL), blk(D_ATT), blk(D_HG), blk(D_MODEL), blk(D_MODEL),
                  linear, sample(D_ATT), sample(D_MODEL), sample(D_MODEL),
                  _resident((1, HG_DV)), _resident(hq_s.shape), _resident(hk_s.shape),
                  _resident(hi_s.shape), _resident(hgate_s.shape), sblk,
                  _resident(kv_s.shape), cblk, cblk,
                  _resident(wa.shape), _resident(wb.shape), _resident(wo.shape),
                  _resident(w_up.shape), _resident(w_down.shape),
                  _resident((1, D_MODEL)), _resident((1, D_MODEL))],
        out_specs=[blk(D_MODEL), linear, sblk, cblk, cblk],
        out_shape=[jax.ShapeDtypeStruct((n, D_MODEL), F32), jax.ShapeDtypeStruct(xs.shape, F32),
                   jax.ShapeDtypeStruct(state.shape, F32),
                   jax.ShapeDtypeStruct(ck_t.shape, F32), jax.ShapeDtypeStruct(cv_t.shape, F32)],
        scratch_shapes=[pltpu.VMEM((nb, D_HG), F32)],
        compiler_params=_params("arbitrary"),
        name="merge_ffn",
    )(x, att, hg, sga, sgb, xs, att_s, sga_s, sgb_s, hg_norm, hq_s, hk_s, hi_s, hgate_s, state,
      kv_s, ck_t, cv_t, wa, wb, wo, w_up, w_down, ln_ffn, ln_f)


def _sample_attn_kernel(sink_ref, qm_ref, kvn_ref, ckt_ref, cvt_ref, mkt_ref, mvt_ref,
                        att_ref, o_all, mk_s, mv_s):
    nb = qm_ref.shape[0]
    first_seq = pl.multiple_of(pl.program_id(0) * nb, nb)

    @pl.when(pl.program_id(0) == 0)
    def _():
        for r in range(N_META):
            mk_s[:, r, :] = mkt_ref[r].T
            mv_s[:, r, :] = mvt_ref[r].T
    head = lax.broadcasted_iota(jnp.int32, (Q_HEADS, 1), 0)
    sk = jnp.zeros((Q_HEADS, 1), F32)
    for j in range(Q_HEADS):
        sk = jnp.where(head == j, sink_ref[j], sk)
    oldest = lax.broadcasted_iota(jnp.int32, (Q_HEADS, WINDOW), 1) == 0
    kvn = kvn_ref[...]
    qms = [qm_ref[b] for b in range(nb)]
    scores = [(_dot(qms[b].astype(BF16), ckt_ref[b].astype(BF16)),
               _dot_nt(qms[b].astype(BF16), mk_s[first_seq + b].astype(BF16)))
              for b in range(nb)]
    for b, (s_w, s_m) in enumerate(scores):
        s_w = jnp.where(oldest, -jnp.inf, s_w)
        s_n = jnp.sum(qms[b] * kvn[b:b + 1, :D_KV], axis=1, keepdims=True)
        m = jnp.maximum(jnp.maximum(jnp.max(s_w, axis=1, keepdims=True),
                                    jnp.max(s_m, axis=1, keepdims=True)), jnp.maximum(s_n, sk))
        e_w = jnp.exp(s_w - m)
        e_m = jnp.exp(s_m - m)
        e_n = jnp.exp(s_n - m)
        l = (jnp.sum(e_w, axis=1, keepdims=True) + jnp.sum(e_m, axis=1, keepdims=True) + e_n
             + jnp.exp(sk - m))
        o = (_dot_nt(e_w.astype(BF16), cvt_ref[b].astype(BF16))
             + _dot(e_m.astype(BF16), mv_s[first_seq + b].astype(BF16))
             + e_n * kvn[b:b + 1, D_KV:])
        o_all[b] = o / l
    low = lax.broadcasted_iota(jnp.int32, (1, LANES), 1) < HEAD_DIM
    pairs = []
    for p in range(Q_HEADS // 2):
        halves = []
        for half in range(2):
            j = 2 * p + half
            kv_head = j // (Q_HEADS // KV_HEADS)
            oj = jnp.where(low if kv_head == 0 else jnp.logical_not(low), o_all[:, j, :], 0.0)
            halves.append(oj if half == kv_head else pltpu.roll(oj, HEAD_DIM, axis=1))
        pairs.append(halves[0] + halves[1])
    att_ref[...] = jnp.concatenate(pairs, axis=1)


def _sample_attention(sinks, qm, kv_new, ck, cv, mk, mv):
    nb = ck.shape[0]
    g = SAMPLE_ATT_GROUP

    def blk3(a, c):
        return pl.BlockSpec((g, a, c), lambda i: (i, 0, 0))

    return pl.pallas_call(
        _sample_attn_kernel,
        grid=(nb // g,),
        in_specs=[pl.BlockSpec(memory_space=pltpu.SMEM), blk3(Q_HEADS, D_KV),
                  pl.BlockSpec((g, 2 * D_KV), lambda i: (i, 0)),
                  blk3(D_KV, WINDOW), blk3(D_KV, WINDOW), _resident(mk.shape), _resident(mv.shape)],
        out_specs=pl.BlockSpec((g, D_ATT), lambda i: (i, 0)),
        out_shape=jax.ShapeDtypeStruct((nb, D_ATT), F32),
        scratch_shapes=[pltpu.VMEM((g, Q_HEADS, LANES), F32),
                        pltpu.VMEM((nb, N_META, D_KV), F32), pltpu.VMEM((nb, N_META, D_KV), F32)],
        compiler_params=_params("arbitrary"),
        name="sample_attn",
    )(sinks, qm, kv_new, ck, cv, mk, mv)


def kernel(x_prompt, x_sample, cache_k, cache_v, cache_meta_k, cache_meta_v, state_hgrn, meta,
           w_in, sinks, lb_param, hg_norm, w_att_out, w_hg_out, w_o, ln_mix, ln_ffn, w_up,
           w_down, ln_f):
    bsz, seq, _ = x_prompt.shape
    nb = x_sample.shape[0]
    ln_mix2 = ln_mix.reshape(1, D_MODEL)
    ln_ffn2 = ln_ffn.reshape(1, D_MODEL)
    ln_f2 = ln_f.reshape(1, D_MODEL)
    nw = hg_norm.reshape(1, HG_DV)

    xs = x_sample.reshape(nb * D_MODEL // LANES, LANES)

    xp = x_prompt.reshape(bsz * seq, D_MODEL)
    (att_p, hg_p, sga_p, sgb_p, state_p, lastkv_p,
     qm_s, kv_s, hq_s, hk_s, hi_s, hgate_s, sga_s, sgb_s,
     wa, wb, wo, wup, wdn) = _mixers(
        sinks.reshape(Q_HEADS), xp, xs, meta, ln_mix2, lb_param, w_in[0], nw,
        (w_att_out[0], w_hg_out[0], w_o[0], w_up[0], w_down[0]), bsz)

    def window_t(c):
        return jnp.swapaxes(c[0].reshape(nb, WINDOW, D_KV), 1, 2)

    def meta_t(c):
        return jnp.transpose(c[0].reshape(nb, N_META, D_KV), (1, 2, 0))

    ck_t, cv_t = window_t(cache_k), window_t(cache_v)
    att_s = _sample_attention(sinks.reshape(Q_HEADS), qm_s, kv_s, ck_t, cv_t,
                              meta_t(cache_meta_k), meta_t(cache_meta_v))

    y_p, y_s, state_s, nkt_s, nvt_s = _merge_ffn(
        xp, att_p, hg_p, sga_p, sgb_p, xs, att_s, sga_s, sgb_s,
        nw, hq_s, hk_s, hi_s, hgate_s, state_hgrn[0], kv_s, ck_t, cv_t,
        wa, wb, wo, ln_ffn2, wup, wdn, ln_f2)

    kv5 = lastkv_p.reshape(bsz, WINDOW, 2, KV_HEADS, HEAD_DIM)
    meta5 = jnp.broadcast_to(kv_s[nb:].reshape(1, N_META, 2, KV_HEADS, HEAD_DIM),
                             (bsz, N_META, 2, KV_HEADS, HEAD_DIM))
    return (y_p.reshape(bsz, seq, D_MODEL),
            y_s.reshape(nb, 1, D_MODEL),
            kv5[None, :, :, 0],
            kv5[None, :, :, 1],
            meta5[None, :, :, 0],
            meta5[None, :, :, 1],
            state_p[None],
            jnp.swapaxes(nkt_s, 1, 2).reshape(1, nb, WINDOW, KV_HEADS, HEAD_DIM),
            jnp.swapaxes(nvt_s, 1, 2).reshape(1, nb, WINDOW, KV_HEADS, HEAD_DIM),
            state_s[None])
```
